```python
import jax, jax.numpy as jnp
from jax import lax
import numpy as np

D_MODEL = 1024
BATCH = 8
SEQ = 8192
DEPTH = 1

D_MIX = 2 * D_MODEL
D_POOL = D_MIX // 2
D_SGU = D_MIX - D_POOL
POOL_WINDOWS = (2, 4, 8, 16)
N_POOL_GROUPS = len(POOL_WINDOWS)
POOL_GROUP = D_POOL // N_POOL_GROUPS
N_SGU_HEADS = 4
SGU_HEAD = D_SGU // N_SGU_HEADS
CHUNK = 128
D_PLE = 256
D_IN = D_POOL + 2 * D_SGU + D_MIX
DEEPNORM_ALPHA = (2 * DEPTH) ** 0.25
DEEPNORM_BETA = (8 * DEPTH) ** -0.25
LN_EPS = 1e-5

kernel_name = "hybrid_pool_sgu_deepnorm_block"


def layer_norm(x, g, b):
    xf = x.astype(jnp.float32)
    mu = jnp.mean(xf, axis=-1, keepdims=True)
    var = jnp.mean(jnp.square(xf - mu), axis=-1, keepdims=True)
    return ((xf - mu) * lax.rsqrt(var + LN_EPS)).astype(x.dtype) * g + b


def pool_mixer(a, pool_w, pool_scale):
    bsz, s, _ = a.shape
    a4 = a.reshape(bsz, s, N_POOL_GROUPS, POOL_GROUP)
    cs = jnp.cumsum(a4.astype(jnp.float32), axis=1)
    t = jnp.arange(s)
    outs = []
    for g, w in enumerate(POOL_WINDOWS):
        c = cs[:, :, g]
        lower = jnp.pad(c, ((0, 0), (w, 0), (0, 0)))[:, :s]
        cnt = jnp.minimum(t + 1, w).astype(jnp.float32)[None, :, None]
        outs.append((c - lower) / cnt)
    pooled = jnp.stack(outs, axis=2).astype(a.dtype) - a4
    mixed = jnp.einsum('bsgc,gcd->bsgd', pooled, pool_w)
    return mixed.reshape(bsz, s, D_POOL) * pool_scale


def spatial_gating(u, v, ln_g, ln_b, w_s, b_s):
    bsz, s, _ = u.shape
    u = jax.nn.gelu(u)
    v = jax.nn.gelu(v)
    vh = v.reshape(bsz, s, N_SGU_HEADS, SGU_HEAD)
    vh = layer_norm(vh, ln_g.reshape(N_SGU_HEADS, SGU_HEAD), ln_b.reshape(N_SGU_HEADS, SGU_HEAD))
    vc = vh.reshape(bsz, s // CHUNK, CHUNK, N_SGU_HEADS, SGU_HEAD)
    mask = jnp.tril(jnp.ones((CHUNK, CHUNK), dtype=bool))
    w = jnp.where(mask[None], w_s, jnp.zeros_like(w_s))
    sv = jnp.einsum('hij,bnjhc->bnihc', w, vc) + b_s.T[None, None, :, :, None]
    return u * sv.reshape(bsz, s, D_SGU)


def _fwd_setup_inputs(seed: int = 0) -> dict:
    key = jax.random.key(seed)
    ks = jax.random.split(key, 16)
    nrm = jax.random.normal
    f32 = jnp.float32
    x = nrm(ks[0], (BATCH, SEQ, D_MODEL), f32)
    p = nrm(ks[1], (DEPTH, BATCH, SEQ, D_PLE), f32)
    w_in = nrm(ks[2], (DEPTH, D_MODEL, D_IN), f32) * D_MODEL ** -0.5
    pool_w = nrm(ks[3], (DEPTH, N_POOL_GROUPS, POOL_GROUP, POOL_GROUP), f32) * POOL_GROUP ** -0.5
    pool_scale = 1.0 + 0.1 * nrm(ks[4], (DEPTH, D_POOL), f32)
    sgu_ln_g = 1.0 + 0.02 * nrm(ks[5], (DEPTH, D_SGU), f32)
    sgu_ln_b = 0.02 * nrm(ks[6], (DEPTH, D_SGU), f32)
    sgu_w = nrm(ks[7], (DEPTH, N_SGU_HEADS, CHUNK, CHUNK), f32) * (0.5 * CHUNK ** -0.5)
    sgu_b = 1.0 + 0.01 * nrm(ks[8], (DEPTH, N_SGU_HEADS, CHUNK), f32)
    w_out = nrm(ks[9], (DEPTH, D_MIX, D_MODEL), f32) * (D_MIX ** -0.5 * DEEPNORM_BETA)
    ln_g = 1.0 + 0.02 * nrm(ks[10], (DEPTH, D_MODEL), f32)
    ln_b = 0.02 * nrm(ks[11], (DEPTH, D_MODEL), f32)
    ple_w = nrm(ks[12], (DEPTH, D_PLE, D_MODEL), f32) * D_PLE ** -0.5
    ple_gate_w = nrm(ks[13], (DEPTH, D_MODEL, D_MODEL), f32) * D_MODEL ** -0.5
    ple_gate_b = 0.02 * nrm(ks[14], (DEPTH, D_MODEL), f32)
    return {"x": x, "p": p, "w_in": w_in, "pool_w": pool_w, "pool_scale": pool_scale,
            "sgu_ln_g": sgu_ln_g, "sgu_ln_b": sgu_ln_b, "sgu_w": sgu_w, "sgu_b": sgu_b,
            "w_out": w_out, "ln_g": ln_g, "ln_b": ln_b, "ple_w": ple_w,
            "ple_gate_w": ple_gate_w, "ple_gate_b": ple_gate_b}


def _fwd_reference(x, p, w_in, pool_w, pool_scale, sgu_ln_g, sgu_ln_b, sgu_w, sgu_b,
              w_out, ln_g, ln_b, ple_w, ple_gate_w, ple_gate_b):
    for i in range(DEPTH):
        h = jnp.einsum('bsd,de->bse', x, w_in[i])
        a, u, v, z = jnp.split(h, [D_POOL, D_POOL + D_SGU, D_POOL + 2 * D_SGU], axis=-1)
        y_pool = pool_mixer(a, pool_w[i], pool_scale[i])
        y_sgu = spatial_gating(u, v, sgu_ln_g[i], sgu_ln_b[i], sgu_w[i], sgu_b[i])
        y = jnp.concatenate([y_pool, y_sgu], axis=-1) * jax.nn.silu(z)
        mix = jnp.einsum('bse,ed->bsd', y, w_out[i])
        x = layer_norm(DEEPNORM_ALPHA * x + mix, ln_g[i], ln_b[i])
        gate = jax.nn.sigmoid(jnp.einsum('bsd,de->bse', x, ple_gate_w[i]) + ple_gate_b[i])
        x = x + gate * jnp.einsum('bsk,kd->bsd', p[i], ple_w[i])
    return x


import jax as _jax
import jax.numpy as _jnp

TWIN_FORMAT = 'train_step'
FWD_PARAMS = ['x', 'p', 'w_in', 'pool_w', 'pool_scale', 'sgu_ln_g', 'sgu_ln_b', 'sgu_w', 'sgu_b', 'w_out', 'ln_g', 'ln_b', 'ple_w', 'ple_gate_w', 'ple_gate_b']
TWIN_WEIGHTS = ['w_in', 'pool_w', 'pool_scale', 'sgu_ln_g', 'sgu_ln_b', 'sgu_w', 'sgu_b', 'w_out', 'ln_g', 'ln_b', 'ple_w', 'ple_gate_w', 'ple_gate_b']
TWIN_DIFF_INPUT = 'x'
TWIN_INPUTS = ['x', 'p', 'w_in', 'pool_w', 'pool_scale', 'sgu_ln_g', 'sgu_ln_b', 'sgu_w', 'sgu_b', 'w_out', 'ln_g', 'ln_b', 'ple_w', 'ple_gate_w', 'ple_gate_b', 'loss_target', 'm_w_in', 'm_pool_w', 'm_pool_scale', 'm_sgu_ln_g', 'm_sgu_ln_b', 'm_sgu_w', 'm_sgu_b', 'm_w_out', 'm_ln_g', 'm_ln_b', 'm_ple_w', 'm_ple_gate_w', 'm_ple_gate_b', 'v_w_in', 'v_pool_w', 'v_pool_scale', 'v_sgu_ln_g', 'v_sgu_ln_b', 'v_sgu_w', 'v_sgu_b', 'v_w_out', 'v_ln_g', 'v_ln_b', 'v_ple_w', 'v_ple_gate_w', 'v_ple_gate_b']
TWIN_OUTPUTS = ['loss', 'grad_x', 'grad_w_in', 'grad_pool_w', 'grad_pool_scale', 'grad_sgu_ln_g', 'grad_sgu_ln_b', 'grad_sgu_w', 'grad_sgu_b', 'grad_w_out', 'grad_ln_g', 'grad_ln_b', 'grad_ple_w', 'grad_ple_gate_w', 'grad_ple_gate_b', 'delta_w_in', 'delta_pool_w', 'delta_pool_scale', 'delta_sgu_ln_g', 'delta_sgu_ln_b', 'delta_sgu_w', 'delta_sgu_b', 'delta_w_out', 'delta_ln_g', 'delta_ln_b', 'delta_ple_w', 'delta_ple_gate_w', 'delta_ple_gate_b', 'new_m_w_in', 'new_m_pool_w', 'new_m_pool_scale', 'new_m_sgu_ln_g', 'new_m_sgu_ln_b', 'new_m_sgu_w', 'new_m_sgu_b', 'new_m_w_out', 'new_m_ln_g', 'new_m_ln_b', 'new_m_ple_w', 'new_m_ple_gate_w', 'new_m_ple_gate_b', 'new_v_w_in', 'new_v_pool_w', 'new_v_pool_scale', 'new_v_sgu_ln_g', 'new_v_sgu_ln_b', 'new_v_sgu_w', 'new_v_sgu_b', 'new_v_w_out', 'new_v_ln_g', 'new_v_ln_b', 'new_v_ple_w', 'new_v_ple_gate_w', 'new_v_ple_gate_b']
TWIN_LEAF_KINDS = {'loss': 'loss', 'grad_x': 'grad_x', 'grad_w_in': 'grad_w', 'grad_pool_w': 'grad_w', 'grad_pool_scale': 'grad_w', 'grad_sgu_ln_g': 'grad_w', 'grad_sgu_ln_b': 'grad_w', 'grad_sgu_w': 'grad_w', 'grad_sgu_b': 'grad_w', 'grad_w_out': 'grad_w', 'grad_ln_g': 'grad_w', 'grad_ln_b': 'grad_w', 'grad_ple_w': 'grad_w', 'grad_ple_gate_w': 'grad_w', 'grad_ple_gate_b': 'grad_w', 'delta_w_in': 'delta_w', 'delta_pool_w': 'delta_w', 'delta_pool_scale': 'delta_w', 'delta_sgu_ln_g': 'delta_w', 'delta_sgu_ln_b': 'delta_w', 'delta_sgu_w': 'delta_w', 'delta_sgu_b': 'delta_w', 'delta_w_out': 'delta_w', 'delta_ln_g': 'delta_w', 'delta_ln_b': 'delta_w', 'delta_ple_w': 'delta_w', 'delta_ple_gate_w': 'delta_w', 'delta_ple_gate_b': 'delta_w', 'new_m_w_in': 'new_m', 'new_m_pool_w': 'new_m', 'new_m_pool_scale': 'new_m', 'new_m_sgu_ln_g': 'new_m', 'new_m_sgu_ln_b': 'new_m', 'new_m_sgu_w': 'new_m', 'new_m_sgu_b': 'new_m', 'new_m_w_out': 'new_m', 'new_m_ln_g': 'new_m', 'new_m_ln_b': 'new_m', 'new_m_ple_w': 'new_m', 'new_m_ple_gate_w': 'new_m', 'new_m_ple_gate_b': 'new_m', 'new_v_w_in': 'new_v', 'new_v_pool_w': 'new_v', 'new_v_pool_scale': 'new_v', 'new_v_sgu_ln_g': 'new_v', 'new_v_sgu_ln_b': 'new_v', 'new_v_sgu_w': 'new_v', 'new_v_sgu_b': 'new_v', 'new_v_w_out': 'new_v', 'new_v_ln_g': 'new_v', 'new_v_ln_b': 'new_v', 'new_v_ple_w': 'new_v', 'new_v_ple_gate_w': 'new_v', 'new_v_ple_gate_b': 'new_v'}


def _forward(args):
    return _fwd_reference(*[args[k] for k in FWD_PARAMS])


def _output_shape():
    def fwd():
        inp = _fwd_setup_inputs(0)
        return _fwd_reference(*[inp[k] for k in FWD_PARAMS])
    out = _jax.eval_shape(fwd)
    return out.shape, out.dtype

N_MICROBATCH = 1
ADAM_LR = 0.001
ADAM_B1 = 0.9
ADAM_B2 = 0.999
ADAM_EPS = 1e-08
ADAM_WD = 0.01
ADAM_STEP = 10
PER_EXAMPLE_BATCH_AXIS = {'x': 0, 'p': 1, 'loss_target': 0}
SHARED_INPUTS = []
_WEIGHT_DTYPES = {'w_in': _jnp.float32, 'pool_w': _jnp.float32, 'pool_scale': _jnp.float32, 'sgu_ln_g': _jnp.float32, 'sgu_ln_b': _jnp.float32, 'sgu_w': _jnp.float32, 'sgu_b': _jnp.float32, 'w_out': _jnp.float32, 'ln_g': _jnp.float32, 'ln_b': _jnp.float32, 'ple_w': _jnp.float32, 'ple_gate_w': _jnp.float32, 'ple_gate_b': _jnp.float32}
MOMENT_SCALE = {'w_in': 4.630645e-02, 'pool_w': 5.567504e-02, 'pool_scale': 5.338074e-02, 'sgu_ln_g': 1.426857e-02, 'sgu_ln_b': 1.569647e-02, 'sgu_w': 4.037377e-02, 'sgu_b': 6.023686e-02, 'w_out': 1.776989e-01, 'ln_g': 6.602869e+01, 'ln_b': 6.522987e+00, 'ple_w': 9.184980e-01, 'ple_gate_w': 1.508354e-01, 'ple_gate_b': 6.620203e+00}


def _to_microbatches(a, axis):
    t = _jnp.moveaxis(a, axis, 0)
    t = t.reshape((N_MICROBATCH, t.shape[0] // N_MICROBATCH) + t.shape[1:])
    return _jnp.moveaxis(t, 1, axis + 1)


def setup_inputs(seed: int = 0) -> dict:
    inp = _fwd_setup_inputs(seed)
    key = _jax.random.fold_in(_jax.random.key(seed), 7919)
    shape, _ = _output_shape()
    out = dict(inp)
    out["loss_target"] = _jax.random.normal(_jax.random.fold_in(key, 0), shape, _jnp.float32)
    for i, name in enumerate(TWIN_WEIGHTS):
        w = inp[name].astype(_jnp.float32)
        if MOMENT_SCALE is None:
            s = _jnp.sqrt(_jnp.mean(_jnp.square(w)) + 1e-30)
        else:
            s = MOMENT_SCALE[name]
        km, kv = _jax.random.split(_jax.random.fold_in(key, i + 1))
        out[name] = w
        out["m_" + name] = s * _jax.random.normal(km, w.shape, _jnp.float32)
        out["v_" + name] = (s * s) * _jax.random.uniform(kv, w.shape, _jnp.float32, 0.5, 1.5)
    if N_MICROBATCH > 1:
        for name, axis in PER_EXAMPLE_BATCH_AXIS.items():
            out[name] = _to_microbatches(out[name], axis)
    return {'x': out['x'], 'p': out['p'], 'w_in': out['w_in'], 'pool_w': out['pool_w'], 'pool_scale': out['pool_scale'], 'sgu_ln_g': out['sgu_ln_g'], 'sgu_ln_b': out['sgu_ln_b'], 'sgu_w': out['sgu_w'], 'sgu_b': out['sgu_b'], 'w_out': out['w_out'], 'ln_g': out['ln_g'], 'ln_b': out['ln_b'], 'ple_w': out['ple_w'], 'ple_gate_w': out['ple_gate_w'], 'ple_gate_b': out['ple_gate_b'], 'loss_target': out['loss_target'], 'm_w_in': out['m_w_in'], 'm_pool_w': out['m_pool_w'], 'm_pool_scale': out['m_pool_scale'], 'm_sgu_ln_g': out['m_sgu_ln_g'], 'm_sgu_ln_b': out['m_sgu_ln_b'], 'm_sgu_w': out['m_sgu_w'], 'm_sgu_b': out['m_sgu_b'], 'm_w_out': out['m_w_out'], 'm_ln_g': out['m_ln_g'], 'm_ln_b': out['m_ln_b'], 'm_ple_w': out['m_ple_w'], 'm_ple_gate_w': out['m_ple_gate_w'], 'm_ple_gate_b': out['m_ple_gate_b'], 'v_w_in': out['v_w_in'], 'v_pool_w': out['v_pool_w'], 'v_pool_scale': out['v_pool_scale'], 'v_sgu_ln_g': out['v_sgu_ln_g'], 'v_sgu_ln_b': out['v_sgu_ln_b'], 'v_sgu_w': out['v_sgu_w'], 'v_sgu_b': out['v_sgu_b'], 'v_w_out': out['v_w_out'], 'v_ln_g': out['v_ln_g'], 'v_ln_b': out['v_ln_b'], 'v_ple_w': out['v_ple_w'], 'v_ple_gate_w': out['v_ple_gate_w'], 'v_ple_gate_b': out['v_ple_gate_b']}


def _loss(weights, diff, rest, loss_target):
    with _jax.named_scope("forward"):
        args = {**rest, TWIN_DIFF_INPUT: diff, **{k: w.astype(_WEIGHT_DTYPES[k]) for k, w in weights.items()}}
        y = _forward(args)
    with _jax.named_scope("loss_head"):
        err = _jnp.square(y.astype(_jnp.float32) - loss_target)
        return 0.5 * _jnp.sum(_jnp.mean(err, axis=-1)) if err.ndim else 0.5 * err


def _adamw(w, g, m, v):
    m = ADAM_B1 * m + (1.0 - ADAM_B1) * g
    v = ADAM_B2 * v + (1.0 - ADAM_B2) * _jnp.square(g)
    m_hat = m / (1.0 - ADAM_B1 ** ADAM_STEP)
    v_hat = v / (1.0 - ADAM_B2 ** ADAM_STEP)
    delta = -ADAM_LR * (m_hat / (_jnp.sqrt(v_hat) + ADAM_EPS) + ADAM_WD * w)
    return delta, m, v


def reference(x, p, w_in, pool_w, pool_scale, sgu_ln_g, sgu_ln_b, sgu_w, sgu_b, w_out, ln_g, ln_b, ple_w, ple_gate_w, ple_gate_b, loss_target, m_w_in, m_pool_w, m_pool_scale, m_sgu_ln_g, m_sgu_ln_b, m_sgu_w, m_sgu_b, m_w_out, m_ln_g, m_ln_b, m_ple_w, m_ple_gate_w, m_ple_gate_b, v_w_in, v_pool_w, v_pool_scale, v_sgu_ln_g, v_sgu_ln_b, v_sgu_w, v_sgu_b, v_w_out, v_ln_g, v_ln_b, v_ple_w, v_ple_gate_w, v_ple_gate_b):
    given = dict(x=x, p=p, w_in=w_in, pool_w=pool_w, pool_scale=pool_scale, sgu_ln_g=sgu_ln_g, sgu_ln_b=sgu_ln_b, sgu_w=sgu_w, sgu_b=sgu_b, w_out=w_out, ln_g=ln_g, ln_b=ln_b, ple_w=ple_w, ple_gate_w=ple_gate_w, ple_gate_b=ple_gate_b, loss_target=loss_target, m_w_in=m_w_in, m_pool_w=m_pool_w, m_pool_scale=m_pool_scale, m_sgu_ln_g=m_sgu_ln_g, m_sgu_ln_b=m_sgu_ln_b, m_sgu_w=m_sgu_w, m_sgu_b=m_sgu_b, m_w_out=m_w_out, m_ln_g=m_ln_g, m_ln_b=m_ln_b, m_ple_w=m_ple_w, m_ple_gate_w=m_ple_gate_w, m_ple_gate_b=m_ple_gate_b, v_w_in=v_w_in, v_pool_w=v_pool_w, v_pool_scale=v_pool_scale, v_sgu_ln_g=v_sgu_ln_g, v_sgu_ln_b=v_sgu_ln_b, v_sgu_w=v_sgu_w, v_sgu_b=v_sgu_b, v_w_out=v_w_out, v_ln_g=v_ln_g, v_ln_b=v_ln_b, v_ple_w=v_ple_w, v_ple_gate_w=v_ple_gate_w, v_ple_gate_b=v_ple_gate_b)
    weights = {n: given[n] for n in TWIN_WEIGHTS}
    shared = {n: given[n] for n in SHARED_INPUTS}
    per_example = {n: given[n] for n in ['x', 'p']}
    grad_fn = _jax.value_and_grad(_loss, argnums=(0, 1))

    def one_microbatch(ex, loss_target):
        ex = dict(ex)
        diff = ex.pop(TWIN_DIFF_INPUT)
        return grad_fn(weights, diff, {**shared, **ex}, loss_target)

    if N_MICROBATCH == 1:
        loss, (grad_w, grad_x) = one_microbatch(per_example, given["loss_target"])
    else:
        def body(carry, xs):
            loss_sum, grad_sum = carry
            l_k, (gw_k, gx_k) = one_microbatch(xs[0], xs[1])
            with _jax.named_scope("update"):
                return (loss_sum + l_k, _jax.tree.map(_jnp.add, grad_sum, gw_k)), gx_k

        init = (_jnp.zeros((), _jnp.float32), _jax.tree.map(_jnp.zeros_like, weights))
        (loss, grad_w), grad_x = _jax.lax.scan(body, init, (per_example, given["loss_target"]))
    with _jax.named_scope("update"):
        delta_w, new_m, new_v = {}, {}, {}
        for n in TWIN_WEIGHTS:
            delta_w[n], new_m[n], new_v[n] = _adamw(weights[n], grad_w[n], given["m_" + n], given["v_" + n])
    return (loss, grad_x, *[grad_w[n] for n in TWIN_WEIGHTS], *[delta_w[n] for n in TWIN_WEIGHTS],
            *[new_m[n] for n in TWIN_WEIGHTS], *[new_v[n] for n in TWIN_WEIGHTS])
```

```python
import functools

import jax
import jax.numpy as jnp
from jax import lax
from jax.experimental import pallas as pl
from jax.experimental.pallas import tpu as pltpu

F32 = jnp.float32
BF16 = jnp.bfloat16

N_DEV = 8
D_MODEL = 1024
D_POOL = 1024
D_SGU = 1024
D_MIX = 2048
D_IN = 5120
D_PLE = 256
POOL_WINDOWS = (2, 4, 8, 16)
POOL_GROUP = 256
N_HEADS = 4
HEAD = 256
CHUNK = 128
HALO = 16
ALPHA = 2.0 ** 0.25
LN_EPS = 1e-5
ADAM_LR, ADAM_B1, ADAM_B2, ADAM_EPS, ADAM_WD, ADAM_STEP = 0.001, 0.9, 0.999, 1e-08, 0.01, 10

U0, V0, Z0 = D_POOL, D_POOL + D_SGU, D_POOL + 2 * D_SGU
VMEM_LIMIT = 56 * 1024 * 1024
MESH = pl.DeviceIdType.MESH
ANY = pl.BlockSpec(memory_space=pl.ANY)
VMEM_FULL = pl.BlockSpec(memory_space=pltpu.VMEM)

_GELU_C0 = 0.7978845608028654
_GELU_C1 = 0.044715


def _gelu_and_grad(x):
    x2 = x * x
    t = jnp.tanh(_GELU_C0 * (x + _GELU_C1 * (x * x2)))
    cdf = 0.5 * (1.0 + t)
    dg = cdf + x * (0.5 * (1.0 - t * t)) * (_GELU_C0 * (1.0 + (3.0 * _GELU_C1) * x2))
    return x * cdf, dg


def _gelu(x):
    t = jnp.tanh(_GELU_C0 * (x + _GELU_C1 * (x * x * x)))
    return x * (0.5 * (1.0 + t))


def _sigmoid(x):
    return 1.0 / (1.0 + jnp.exp(-x))


def _dot(a, b):
    return jnp.dot(a, b, preferred_element_type=F32)


def _dot_nt(a, b):
    return lax.dot_general(a, b, (((1,), (1,)), ((), ())), preferred_element_type=F32)


def _dot_tn(a, b):
    return lax.dot_general(a, b, (((0,), (0,)), ((), ())), preferred_element_type=F32)


def _row_stats(x):
    mu = jnp.mean(x, axis=-1, keepdims=True)
    xc = x - mu
    var = jnp.mean(xc * xc, axis=-1, keepdims=True)
    rstd = lax.rsqrt(var + LN_EPS)
    return xc * rstd, rstd


def _ln_bwd(dxhat, xhat, rstd):
    m1 = jnp.mean(dxhat, axis=-1, keepdims=True)
    m2 = jnp.mean(dxhat * xhat, axis=-1, keepdims=True)
    return rstd * (dxhat - m1 - xhat * m2)


def _masked_sgu_w(sw_ref, hh):
    row = lax.broadcasted_iota(jnp.int32, (CHUNK, CHUNK), 0)
    col = lax.broadcasted_iota(jnp.int32, (CHUNK, CHUNK), 1)
    return jnp.where(row >= col, sw_ref[hh], 0.0)


def _inv_count(tile_index, tok_tile, window):
    tok = tile_index * tok_tile + lax.broadcasted_iota(jnp.int32, (tok_tile, 1), 0)
    return 1.0 / jnp.minimum(tok + 1, window).astype(F32)


def _params(**kw):
    return pltpu.CompilerParams(vmem_limit_bytes=VMEM_LIMIT, **kw)


def _forward_mixers(x, w_in, pool_w, pool_scale, sgu_ln_g, sgu_ln_b, sgu_w, sgu_bias_tile, tok_tile):
    seq = x.shape[0]
    n_chunks = tok_tile // CHUNK

    def body(x_ref, win_ref, pw_ref, ps_ref, lg_ref, lb_ref, sw_ref, sb_ref,
             h_ref, y_ref, pooled_ref, xb_ref, aext_ref):
        i = pl.program_id(0)

        @pl.when(i == 0)
        def _():
            aext_ref[0:HALO, :] = jnp.zeros((HALO, D_POOL), F32)

        xb = x_ref[...].astype(BF16)
        xb_ref[...] = xb
        for s in range(D_IN // 1024):
            cs = slice(s * 1024, (s + 1) * 1024)
            h_ref[:, cs] = _dot(xb, win_ref[:, cs])

        aext_ref[HALO:HALO + tok_tile, :] = h_ref[:, 0:D_POOL]
        for g, window in enumerate(POOL_WINDOWS):
            cols = slice(g * POOL_GROUP, (g + 1) * POOL_GROUP)
            win = aext_ref[HALO:HALO + tok_tile, cols]
            for k in range(1, window):
                win = win + aext_ref[HALO - k:HALO - k + tok_tile, cols]
            pooled = (win * _inv_count(i, tok_tile, window)).astype(F32) - h_ref[:, cols]
            pb = pooled.astype(BF16)
            pooled_ref[:, cols] = pb
            mixed = _dot(pb, pw_ref[g])
            z = h_ref[:, Z0 + g * POOL_GROUP:Z0 + (g + 1) * POOL_GROUP]
            y_ref[:, cols] = (mixed * ps_ref[:, cols] * (z * _sigmoid(z))).astype(BF16)
        aext_ref[0:HALO, :] = aext_ref[tok_tile:tok_tile + HALO, :]

        for hh in range(N_HEADS):
            cols = slice(hh * HEAD, (hh + 1) * HEAD)
            swm = _masked_sgu_w(sw_ref, hh).astype(BF16)
            for n in range(n_chunks):
                rows = slice(n * CHUNK, (n + 1) * CHUNK)
                gu = _gelu(h_ref[rows, U0 + hh * HEAD:U0 + (hh + 1) * HEAD])
                gv = _gelu(h_ref[rows, V0 + hh * HEAD:V0 + (hh + 1) * HEAD])
                xhat, _ = _row_stats(gv)
                vln = xhat * lg_ref[:, cols] + lb_ref[:, cols]
                sv = _dot(swm, vln.astype(BF16)) + sb_ref[:, cols]
                z = h_ref[rows, Z0 + D_POOL + hh * HEAD:Z0 + D_POOL + (hh + 1) * HEAD]
                y_ref[rows, D_POOL + hh * HEAD:D_POOL + (hh + 1) * HEAD] = (
                    gu * sv * (z * _sigmoid(z))).astype(BF16)

    tok = lambda width: pl.BlockSpec((tok_tile, width), lambda i: (i, 0))
    return pl.pallas_call(
        body, name="forward_mixers",
        grid=(seq // tok_tile,),
        in_specs=[tok(D_MODEL), VMEM_FULL, VMEM_FULL, VMEM_FULL, VMEM_FULL, VMEM_FULL, VMEM_FULL, VMEM_FULL],
        out_specs=[tok(D_IN), tok(D_MIX), tok(D_POOL), tok(D_MODEL)],
        out_shape=[jax.ShapeDtypeStruct((seq, D_IN), F32), jax.ShapeDtypeStruct((seq, D_MIX), BF16),
                   jax.ShapeDtypeStruct((seq, D_POOL), BF16), jax.ShapeDtypeStruct((seq, D_MODEL), BF16)],
        scratch_shapes=[pltpu.VMEM((HALO + tok_tile, D_POOL), F32)],
        compiler_params=_params(dimension_semantics=("arbitrary",)),
    )(x, w_in, pool_w, pool_scale, sgu_ln_g, sgu_ln_b, sgu_w, sgu_bias_tile)


def _head_fwd_bwd(x, y, p, target, w_out, gate_w, ple_w, ln_g, ln_b, gate_b, tok_tile):
    seq = x.shape[0]

    def body(x_ref, y_ref, p_ref, t_ref, wout_ref, gw_ref, plw_ref, lng_ref, lnb_ref, gb_ref,
             dy_ref, dxr_ref, xn_ref, dgp_ref, dpe_ref, dr_ref, loss_ref, dlng_ref, dlnb_ref, dgb_ref):
        i = pl.program_id(0)

        @pl.when(i == 0)
        def _():
            loss_ref[...] = jnp.zeros_like(loss_ref)
            dlng_ref[...] = jnp.zeros_like(dlng_ref)
            dlnb_ref[...] = jnp.zeros_like(dlnb_ref)
            dgb_ref[...] = jnp.zeros_like(dgb_ref)

        r = ALPHA * x_ref[...] + _dot(y_ref[...], wout_ref[...])
        xhat, rstd = _row_stats(r)
        xn = xhat * lng_ref[...] + lnb_ref[...]
        xnb = xn.astype(BF16)
        xn_ref[...] = xnb
        gate = _sigmoid(_dot(xnb, gw_ref[...]) + gb_ref[...])
        pe = _dot(p_ref[...].astype(BF16), plw_ref[...])
        err = xn + gate * pe - t_ref[...]
        loss_ref[...] += (0.5 / D_MODEL) * jnp.sum(err * err, keepdims=True)

        dout = err * (1.0 / D_MODEL)
        dpe_ref[...] = (dout * gate).astype(BF16)
        dgpre = dout * pe * gate * (1.0 - gate)
        dgb_ref[...] += jnp.sum(dgpre, axis=0, keepdims=True)
        dgpb = dgpre.astype(BF16)
        dgp_ref[...] = dgpb
        dxn = dout + _dot_nt(dgpb, gw_ref[...])
        dlng_ref[...] += jnp.sum(dxn * xhat, axis=0, keepdims=True)
        dlnb_ref[...] += jnp.sum(dxn, axis=0, keepdims=True)
        dr = _ln_bwd(dxn * lng_ref[...], xhat, rstd)
        dxr_ref[...] = ALPHA * dr
        drb = dr.astype(BF16)
        dr_ref[...] = drb
        dy_ref[...] = _dot_nt(drb, wout_ref[...])

    tok = lambda width: pl.BlockSpec((tok_tile, width), lambda i: (i, 0))
    acc = lambda width: pl.BlockSpec((1, width), lambda i: (0, 0))
    vec = jax.ShapeDtypeStruct((1, D_MODEL), F32)
    return pl.pallas_call(
        body, name="head_fwd_bwd",
        grid=(seq // tok_tile,),
        in_specs=[tok(D_MODEL), tok(D_MIX), tok(D_PLE), tok(D_MODEL),
                  VMEM_FULL, VMEM_FULL, VMEM_FULL, VMEM_FULL, VMEM_FULL, VMEM_FULL],
        out_specs=[tok(D_MIX), tok(D_MODEL), tok(D_MODEL), tok(D_MODEL), tok(D_MODEL), tok(D_MODEL),
                   acc(1), acc(D_MODEL), acc(D_MODEL), acc(D_MODEL)],
        out_shape=[jax.ShapeDtypeStruct((seq, D_MIX), F32), jax.ShapeDtypeStruct((seq, D_MODEL), F32),
                   jax.ShapeDtypeStruct((seq, D_MODEL), BF16), jax.ShapeDtypeStruct((seq, D_MODEL), BF16),
                   jax.ShapeDtypeStruct((seq, D_MODEL), BF16), jax.ShapeDtypeStruct((seq, D_MODEL), BF16),
                   jax.ShapeDtypeStruct((1, 1), F32), vec, vec, vec],
        compiler_params=_params(dimension_semantics=("arbitrary",)),
    )(x, y, p, target, w_out, gate_w, ple_w, ln_g, ln_b, gate_b)


def _mixers_bwd(h, dy, dxr, pooled, w_in, pool_w, pool_scale, sgu_ln_g, sgu_ln_b, sgu_w, sgu_bias_tile, tok_tile):
    seq = h.shape[0]
    n_tiles = seq // tok_tile
    n_chunks = tok_tile // CHUNK
    pool_rows = POOL_GROUP // N_DEV

    def body(h_ref, dy_ref, dxr_ref, pooled_ref, win_ref, pw_ref, ps_ref, lg_ref, lb_ref, sw_ref, sb_ref,
             dh_ref, dx_ref, dpw_ref, dps_ref, dlg_ref, dlb_ref, dsw_ref, dsb_ref,
             qext_ref, dpw_acc, dsb_acc):
        i = pl.program_id(0)
        tile = n_tiles - 1 - i

        @pl.when(i == 0)
        def _():
            qext_ref[tok_tile:tok_tile + HALO, :] = jnp.zeros((HALO, D_POOL), F32)
            dpw_acc[...] = jnp.zeros_like(dpw_acc)
            dsb_acc[...] = jnp.zeros_like(dsb_acc)
            dps_ref[...] = jnp.zeros_like(dps_ref)
            dlg_ref[...] = jnp.zeros_like(dlg_ref)
            dlb_ref[...] = jnp.zeros_like(dlb_ref)
            dsw_ref[...] = jnp.zeros_like(dsw_ref)

        for g, window in enumerate(POOL_WINDOWS):
            cols = slice(g * POOL_GROUP, (g + 1) * POOL_GROUP)
            zcols = slice(Z0 + g * POOL_GROUP, Z0 + (g + 1) * POOL_GROUP)
            z = h_ref[:, zcols]
            sz = _sigmoid(z)
            pb = pooled_ref[:, cols]
            mixed = _dot(pb, pw_ref[g])
            dyp = dy_ref[:, cols]
            dh_ref[:, zcols] = (dyp * (mixed * ps_ref[:, cols]) * (sz * (1.0 + z * (1.0 - sz)))).astype(BF16)
            dms = dyp * (z * sz)
            dps_ref[:, cols] += jnp.sum(dms * mixed, axis=0, keepdims=True)
            dmixed = (dms * ps_ref[:, cols]).astype(BF16)
            dpw_acc[g] += _dot_tn(pb, dmixed)
            dpooled = _dot_nt(dmixed, pw_ref[g])
            qext_ref[0:tok_tile, cols] = dpooled * _inv_count(tile, tok_tile, window)
            da = qext_ref[0:tok_tile, cols] - dpooled
            for k in range(1, window):
                da = da + qext_ref[k:k + tok_tile, cols]
            dh_ref[:, cols] = da.astype(BF16)
        qext_ref[tok_tile:tok_tile + HALO, :] = qext_ref[0:HALO, :]

        for hh in range(N_HEADS):
            cols = slice(hh * HEAD, (hh + 1) * HEAD)
            ucols = slice(U0 + hh * HEAD, U0 + (hh + 1) * HEAD)
            vcols = slice(V0 + hh * HEAD, V0 + (hh + 1) * HEAD)
            zcols = slice(Z0 + D_POOL + hh * HEAD, Z0 + D_POOL + (hh + 1) * HEAD)
            sw32 = _masked_sgu_w(sw_ref, hh)
            swm = sw32.astype(BF16)
            swm_t = sw32.T.astype(BF16)
            for n in range(n_chunks):
                rows = slice(n * CHUNK, (n + 1) * CHUNK)
                gu, dgu_du = _gelu_and_grad(h_ref[rows, ucols])
                gv, dgv_dv = _gelu_and_grad(h_ref[rows, vcols])
                xhat, rstd = _row_stats(gv)
                vb = (xhat * lg_ref[:, cols] + lb_ref[:, cols]).astype(BF16)
                sv = _dot(swm, vb) + sb_ref[:, cols]
                z = h_ref[rows, zcols]
                sz = _sigmoid(z)
                dys = dy_ref[rows, D_POOL + hh * HEAD:D_POOL + (hh + 1) * HEAD]
                dh_ref[rows, zcols] = (dys * (gu * sv) * (sz * (1.0 + z * (1.0 - sz)))).astype(BF16)
                dyg = dys * (z * sz)
                dh_ref[rows, ucols] = (dyg * sv * dgu_du).astype(BF16)
                dsv = dyg * gu
                dsb_acc[:, cols] += dsv
                dsvb = dsv.astype(BF16)
                dsw_ref[hh] += _dot_nt(dsvb, vb)
                dvln = _dot(swm_t, dsvb)
                dlg_ref[:, cols] += jnp.sum(dvln * xhat, axis=0, keepdims=True)
                dlb_ref[:, cols] += jnp.sum(dvln, axis=0, keepdims=True)
                dgv = _ln_bwd(dvln * lg_ref[:, cols], xhat, rstd)
                dh_ref[rows, vcols] = (dgv * dgv_dv).astype(BF16)

        dx_ref[...] = dxr_ref[...] + _dot_nt(dh_ref[...], win_ref[...])

        @pl.when(i == n_tiles - 1)
        def _():
            for g in range(len(POOL_WINDOWS)):
                for b in range(N_DEV):
                    dpw_ref[b, g] = dpw_acc[g, b * pool_rows:(b + 1) * pool_rows, :]
            row = lax.broadcasted_iota(jnp.int32, (CHUNK, CHUNK), 0)
            col = lax.broadcasted_iota(jnp.int32, (CHUNK, CHUNK), 1)
            for hh in range(N_HEADS):
                dsw_ref[hh] = jnp.where(row >= col, dsw_ref[hh], 0.0)
                total = jnp.sum(dsb_acc[:, hh * HEAD:(hh + 1) * HEAD], axis=1, keepdims=True)
                dsb_ref[hh] = jnp.broadcast_to(total, (CHUNK, CHUNK))

    tok = lambda width: pl.BlockSpec((tok_tile, width), lambda i: (n_tiles - 1 - i, 0))
    whole = lambda shape: pl.BlockSpec(shape, lambda i: (0,) * len(shape))
    vec = jax.ShapeDtypeStruct((1, D_MODEL), F32)
    return pl.pallas_call(
        body, name="mixers_bwd",
        grid=(n_tiles,),
        in_specs=[tok(D_IN), tok(D_MIX), tok(D_MODEL), tok(D_POOL),
                  VMEM_FULL, VMEM_FULL, VMEM_FULL, VMEM_FULL, VMEM_FULL, VMEM_FULL, VMEM_FULL],
        out_specs=[tok(D_IN), tok(D_MODEL), whole((N_DEV, len(POOL_WINDOWS), pool_rows, POOL_GROUP)),
                   whole((1, D_POOL)), whole((1, D_SGU)), whole((1, D_SGU)),
                   whole((N_HEADS, CHUNK, CHUNK)), whole((N_HEADS, CHUNK, CHUNK))],
        out_shape=[jax.ShapeDtypeStruct((seq, D_IN), BF16), jax.ShapeDtypeStruct((seq, D_MODEL), F32),
                   jax.ShapeDtypeStruct((N_DEV, len(POOL_WINDOWS), pool_rows, POOL_GROUP), F32),
                   vec, vec, vec,
                   jax.ShapeDtypeStruct((N_HEADS, CHUNK, CHUNK), F32),
                   jax.ShapeDtypeStruct((N_HEADS, CHUNK, CHUNK), F32)],
        scratch_shapes=[pltpu.VMEM((tok_tile + HALO, D_POOL), F32),
                        pltpu.VMEM((len(POOL_WINDOWS), POOL_GROUP, POOL_GROUP), F32),
                        pltpu.VMEM((CHUNK, D_SGU), F32)],
        compiler_params=_params(dimension_semantics=("arbitrary",)),
    )(h, dy, dxr, pooled, w_in, pool_w, pool_scale, sgu_ln_g, sgu_ln_b, sgu_w, sgu_bias_tile)


def _weight_grad(a, b, n_col_blocks, blocks_per_step, tok_tile, name):
    seq, m = a.shape
    n = b.shape[1]
    nb = n // n_col_blocks
    n_steps = n_col_blocks // blocks_per_step

    def body(a_ref, b_ref, out_ref):
        @pl.when(pl.program_id(1) == 0)
        def _():
            out_ref[...] = jnp.zeros_like(out_ref)

        res = _dot_tn(a_ref[...].astype(BF16), b_ref[...])
        for blk in range(blocks_per_step):
            out_ref[blk] += res[:, blk * nb:(blk + 1) * nb]

    return pl.pallas_call(
        body, name=name,
        grid=(n_steps, seq // tok_tile),
        in_specs=[pl.BlockSpec((tok_tile, m), lambda j, k: (k, 0)),
                  pl.BlockSpec((tok_tile, blocks_per_step * nb), lambda j, k: (k, j))],
        out_specs=pl.BlockSpec((blocks_per_step, m, nb), lambda j, k: (j, 0, 0)),
        out_shape=jax.ShapeDtypeStruct((n_col_blocks, m, nb), F32),
        compiler_params=_params(dimension_semantics=("arbitrary", "arbitrary")),
    )(a, b)


def _place():
    return lax.axis_index("x"), lax.axis_index("y"), lax.axis_index("c")


def _other_chips(x, y):
    return [(1 - x, y), (x, 1 - y), (1 - x, 1 - y)]


def _all_gather(shards, out_shapes, views, name):
    n = len(shards)

    def body(*refs):
        ins, outs = refs[:n], refs[n:2 * n]
        send_sems, recv_sems, local_sems = refs[2 * n:]
        x, y, c = _place()
        me, sibling = (x, y, c), (x, y, 1 - c)
        chips = _other_chips(x, y)

        def copy(k, s, block, to, src=None):
            px, py, pc = block
            dst = views[k](outs[k], 4 * px + 2 * py + pc)
            return pltpu.make_async_remote_copy(
                src_ref=dst if src is None else src, dst_ref=dst,
                send_sem=send_sems.at[k, s], recv_sem=recv_sems.at[k, s],
                device_id=to, device_id_type=MESH)

        started = []
        for k in range(n):
            mine = pltpu.make_async_copy(ins[k], views[k](outs[k], 4 * x + 2 * y + c), local_sems.at[k])
            mine.start()
            started.append(mine)
        sends = []
        for k in range(n):
            sends.append(copy(k, 0, me, sibling, src=ins[k]))
            sends += [copy(k, 1 + j, me, (*chip, c), src=ins[k]) for j, chip in enumerate(chips)]
        for cp in sends:
            cp.start()
        for k in range(n):
            for j, chip in enumerate(chips):
                copy(k, 1 + j, (*chip, c), me).wait_recv()
                fwd = copy(k, 4 + j, (*chip, c), sibling)
                fwd.start()
                sends.append(fwd)
        for k in range(n):
            copy(k, 0, sibling, me).wait_recv()
            for j, chip in enumerate(chips):
                copy(k, 4 + j, (*chip, 1 - c), me).wait_recv()
        for cp in sends:
            cp.wait_send()
        for mine in started:
            mine.wait()

    return pl.pallas_call(
        body, name=name,
        in_specs=[ANY] * n, out_specs=[ANY] * n,
        out_shape=out_shapes,
        scratch_shapes=[pltpu.SemaphoreType.DMA((n, 7)), pltpu.SemaphoreType.DMA((n, 7)),
                        pltpu.SemaphoreType.DMA((n,))],
    )(*shards)


def _pair_exchange(grads, name):
    n = len(grads)

    def body(*refs):
        ins = refs[:n]
        mine, theirs = refs[n:2 * n], refs[2 * n:3 * n]
        send_sems, recv_sems, local_sems = refs[3 * n:]
        x, y, c = _place()
        sibling = (x, y, 1 - c)
        remote, local = [], []
        for k in range(n):
            for j in range(4):
                cp = pltpu.make_async_copy(ins[k].at[2 * j + c], mine[k].at[j], local_sems.at[k, j])
                cp.start()
                local.append(cp)
                rc = pltpu.make_async_remote_copy(
                    src_ref=ins[k].at[2 * j + (1 - c)], dst_ref=theirs[k].at[j],
                    send_sem=send_sems.at[k, j], recv_sem=recv_sems.at[k, j],
                    device_id=sibling, device_id_type=MESH)
                rc.start()
                remote.append(rc)
        for rc in remote:
            rc.wait_recv()
        for rc in remote:
            rc.wait_send()
        for cp in local:
            cp.wait()

    half = [jax.ShapeDtypeStruct((4,) + g.shape[1:], g.dtype) for g in grads]
    outs = pl.pallas_call(
        body, name=name,
        in_specs=[ANY] * n, out_specs=[ANY] * (2 * n),
        out_shape=half + half,
        scratch_shapes=[pltpu.SemaphoreType.DMA((n, 4)), pltpu.SemaphoreType.DMA((n, 4)),
                        pltpu.SemaphoreType.DMA((n, 4))],
    )(*grads)
    return outs[:n], outs[n:]


def _chip_exchange(sums, name):
    n = len(sums)

    def body(*refs):
        ins = refs[:n]
        own, others = refs[n:2 * n], refs[2 * n:3 * n]
        send_sems, recv_sems, local_sems = refs[3 * n:]
        x, y, c = _place()
        chips = _other_chips(x, y)
        remote, local = [], []
        for k in range(n):
            cp = pltpu.make_async_copy(ins[k].at[2 * x + y], own[k], local_sems.at[k])
            cp.start()
            local.append(cp)
            for r, (px, py) in enumerate(chips):
                rc = pltpu.make_async_remote_copy(
                    src_ref=ins[k].at[2 * px + py], dst_ref=others[k].at[r],
                    send_sem=send_sems.at[k, r], recv_sem=recv_sems.at[k, r],
                    device_id=(px, py, c), device_id_type=MESH)
                rc.start()
                remote.append(rc)
        for rc in remote:
            rc.wait_recv()
        for rc in remote:
            rc.wait_send()
        for cp in local:
            cp.wait()

    own_shapes = [jax.ShapeDtypeStruct(s.shape[1:], s.dtype) for s in sums]
    other_shapes = [jax.ShapeDtypeStruct((3,) + s.shape[1:], s.dtype) for s in sums]
    outs = pl.pallas_call(
        body, name=name,
        in_specs=[ANY] * n, out_specs=[ANY] * (2 * n),
        out_shape=own_shapes + other_shapes,
        scratch_shapes=[pltpu.SemaphoreType.DMA((n, 3)), pltpu.SemaphoreType.DMA((n, 3)),
                        pltpu.SemaphoreType.DMA((n,))],
    )(*sums)
    return outs[:n], outs[n:]


def _row_tile(rows, cols):
    tile = rows
    while tile * cols > 256 * 1024 and tile % 16 == 0:
        tile //= 2
    return tile


def _pair_sum(mine, theirs, name):
    _, rows, cols = mine.shape
    rt = _row_tile(rows, cols)

    def body(a_ref, b_ref, o_ref):
        o_ref[...] = a_ref[...] + b_ref[...]

    spec = pl.BlockSpec((1, rt, cols), lambda j, i: (j, i, 0))
    return pl.pallas_call(
        body, name=name, grid=(4, rows // rt),
        in_specs=[spec, spec], out_specs=spec,
        out_shape=jax.ShapeDtypeStruct(mine.shape, F32),
        compiler_params=_params(dimension_semantics=("arbitrary", "arbitrary")),
    )(mine, theirs)


def _adamw(w, g, m, v):
    m = ADAM_B1 * m + (1.0 - ADAM_B1) * g
    v = ADAM_B2 * v + (1.0 - ADAM_B2) * (g * g)
    m_hat = m / (1.0 - ADAM_B1 ** ADAM_STEP)
    v_hat = v / (1.0 - ADAM_B2 ** ADAM_STEP)
    delta = -ADAM_LR * (m_hat / (jnp.sqrt(v_hat) + ADAM_EPS) + ADAM_WD * w)
    return delta, m, v


def _sum_adamw(own, others, w, m, v, name):
    rows, cols = own.shape
    rt = _row_tile(rows, cols)

    def body(own_ref, oth_ref, w_ref, m_ref, v_ref, g_ref, d_ref, nm_ref, nv_ref):
        g = ((own_ref[...] + oth_ref[0]) + oth_ref[1]) + oth_ref[2]
        g_ref[...] = g
        d_ref[...], nm_ref[...], nv_ref[...] = _adamw(w_ref[...], g, m_ref[...], v_ref[...])

    spec = pl.BlockSpec((rt, cols), lambda i: (i, 0))
    shape = jax.ShapeDtypeStruct((rows, cols), F32)
    return pl.pallas_call(
        body, name=name, grid=(rows // rt,),
        in_specs=[spec, pl.BlockSpec((3, rt, cols), lambda i: (0, i, 0)), spec, spec, spec],
        out_specs=[spec] * 4, out_shape=[shape] * 4,
        compiler_params=_params(dimension_semantics=("arbitrary",)),
    )(own, others, w, m, v)


def _small_sum_adamw(gathered, w, m, v):
    def body(g8_ref, w_ref, m_ref, v_ref, g_ref, d_ref, nm_ref, nv_ref):
        g = g8_ref[0]
        for b in range(1, N_DEV):
            g = g + g8_ref[b]
        g_ref[...] = g
        d_ref[...], nm_ref[...], nv_ref[...] = _adamw(w_ref[...], g, m_ref[...], v_ref[...])

    shape = jax.ShapeDtypeStruct(w.shape, F32)
    return pl.pallas_call(
        body, name="small_sum_adamw",
        in_specs=[VMEM_FULL] * 4, out_specs=[VMEM_FULL] * 4, out_shape=[shape] * 4,
        compiler_params=_params(),
    )(gathered, w, m, v)


SMALL_NAMES = ("pool_scale", "sgu_ln_g", "sgu_ln_b", "sgu_w", "sgu_b", "ln_g", "ln_b", "ple_gate_b", "loss")


def _pack_small(parts):
    rows = []
    for a in parts:
        a = a.reshape(-1).astype(F32)
        pad = (-a.shape[0]) % 1024
        rows.append(jnp.pad(a, (0, pad)).reshape(-1, 128))
    return jnp.concatenate(rows, axis=0)


def _unpack_small(packed, shapes):
    out, row = [], 0
    for shape in shapes:
        size = 1
        for d in shape:
            size *= d
        n_rows = (size + 1023) // 1024 * 8
        out.append(packed[row:row + n_rows].reshape(-1)[:size].reshape(shape))
        row += n_rows
    return out


def _local_step(x, p, target, w_in, pool_w, w_out, ple_w, gate_w,
                pool_scale, sgu_ln_g, sgu_ln_b, sgu_w, sgu_b, ln_g, ln_b, gate_b,
                tok_tile=256, grad_tile=512):
    bias_tile = jnp.repeat(sgu_b.T, HEAD, axis=1)
    h, y, pooled, xb = _forward_mixers(x, w_in, pool_w, pool_scale, sgu_ln_g, sgu_ln_b, sgu_w, bias_tile, tok_tile)
    dy, dxr, xn, dgp, dpe, dr, loss, d_ln_g, d_ln_b, d_gate_b = _head_fwd_bwd(
        x, y, p, target, w_out, gate_w, ple_w, ln_g, ln_b, gate_b, tok_tile)
    dh, dx, d_pool_w, d_pool_scale, d_sgu_ln_g, d_sgu_ln_b, d_sgu_w, d_sgu_b_tiles = _mixers_bwd(
        h, dy, dxr, pooled, w_in, pool_w, pool_scale, sgu_ln_g, sgu_ln_b, sgu_w, bias_tile, tok_tile)
    d_w_in = _weight_grad(xb, dh, N_DEV, 2, grad_tile, "grad_w_in")
    d_w_out = _weight_grad(y, dr, 1, 1, grad_tile, "grad_w_out")
    d_ple_w = _weight_grad(p, dpe, N_DEV, N_DEV, grad_tile, "grad_ple_w")
    d_gate_w = _weight_grad(xn, dgp, 1, 1, grad_tile, "grad_gate_w")
    d_sgu_b = d_sgu_b_tiles[:, :, 0]
    small = (d_pool_scale, d_sgu_ln_g, d_sgu_ln_b, d_sgu_w, d_sgu_b, d_ln_g, d_ln_b, d_gate_b)
    return loss, dx, (d_w_in, d_pool_w, d_w_out, d_ple_w, d_gate_w), small


def kernel(x, p, w_in, pool_w, pool_scale, sgu_ln_g, sgu_ln_b, sgu_w, sgu_b, w_out, ln_g, ln_b, ple_w, ple_gate_w, ple_gate_b, loss_target, m_w_in, m_pool_w, m_pool_scale, m_sgu_ln_g, m_sgu_ln_b, m_sgu_w, m_sgu_b, m_w_out, m_ln_g, m_ln_b, m_ple_w, m_ple_gate_w, m_ple_gate_b, v_w_in, v_pool_w, v_pool_scale, v_sgu_ln_g, v_sgu_ln_b, v_sgu_w, v_sgu_b, v_w_out, v_ln_g, v_ln_b, v_ple_w, v_ple_gate_w, v_ple_gate_b):
    seq = x.shape[1]

    shards = [w_in[0].astype(BF16), pool_w[0].astype(BF16), w_out[0].astype(BF16),
              ple_w[0].astype(BF16), ple_gate_w[0].astype(BF16)]
    full_shapes = [jax.ShapeDtypeStruct((D_MODEL, D_IN), BF16),
                   jax.ShapeDtypeStruct((len(POOL_WINDOWS), POOL_GROUP, POOL_GROUP), BF16),
                   jax.ShapeDtypeStruct((D_MIX, D_MODEL), BF16),
                   jax.ShapeDtypeStruct((D_PLE, D_MODEL), BF16),
                   jax.ShapeDtypeStruct((D_MODEL, D_MODEL), BF16)]
    cols = lambda width: (lambda ref, b: ref.at[:, pl.ds(pl.multiple_of(b * width, 128), width)])
    rows = lambda height: (lambda ref, b: ref.at[pl.ds(pl.multiple_of(b * height, 16), height), :])
    pool_rows = POOL_GROUP // N_DEV
    views = [cols(D_IN // N_DEV),
             lambda ref, b: ref.at[:, pl.ds(pl.multiple_of(b * pool_rows, 16), pool_rows), :],
             rows(D_MIX // N_DEV), cols(D_MODEL // N_DEV), rows(D_MODEL // N_DEV)]
    w_in_f, pool_w_f, w_out_f, ple_w_f, gate_w_f = _all_gather(shards, full_shapes, views, "gather_weights")

    loss, dx, big, small = _local_step(
        x[0], p[0, 0], loss_target[0], w_in_f, pool_w_f, w_out_f, ple_w_f, gate_w_f,
        pool_scale, sgu_ln_g, sgu_ln_b, sgu_w[0], sgu_b[0], ln_g, ln_b, ple_gate_b)
    d_w_in, d_pool_w, d_w_out, d_ple_w, d_gate_w = big

    blocks = [d_w_in,
              d_pool_w.reshape(N_DEV, len(POOL_WINDOWS) * pool_rows, POOL_GROUP),
              d_w_out.reshape(N_DEV, D_MIX // N_DEV, D_MODEL),
              d_ple_w,
              d_gate_w.reshape(N_DEV, D_MODEL // N_DEV, D_MODEL)]
    names = ("w_in", "pool_w", "w_out", "ple_w", "gate_w")
    mine, theirs = _pair_exchange(blocks, "pair_exchange")
    sums = [_pair_sum(a, b, "pair_sum_" + nm) for a, b, nm in zip(mine, theirs, names)]
    own, others = _chip_exchange(sums, "chip_exchange")

    big_w = (w_in, pool_w, w_out, ple_w, ple_gate_w)
    big_m = (m_w_in, m_pool_w, m_w_out, m_ple_w, m_ple_gate_w)
    big_v = (v_w_in, v_pool_w, v_w_out, v_ple_w, v_ple_gate_w)
    big_out = {}
    for nm, o, oth, w, m, v in zip(names, own, others, big_w, big_m, big_v):
        two_d = o.shape
        res = _sum_adamw(o, oth, w.reshape(two_d), m.reshape(two_d), v.reshape(two_d), "adamw_" + nm)
        big_out[nm] = [r.reshape(w.shape) for r in res]

    small_w = (pool_scale, sgu_ln_g, sgu_ln_b, sgu_w, sgu_b, ln_g, ln_b, ple_gate_b)
    small_m = (m_pool_scale, m_sgu_ln_g, m_sgu_ln_b, m_sgu_w, m_sgu_b, m_ln_g, m_ln_b, m_ple_gate_b)
    small_v = (v_pool_scale, v_sgu_ln_g, v_sgu_ln_b, v_sgu_w, v_sgu_b, v_ln_g, v_ln_b, v_ple_gate_b)
    zero = jnp.zeros((1, 1), F32)
    packed_g = _pack_small(small + (loss,))
    n_rows = packed_g.shape[0]
    (gathered,) = _all_gather([packed_g], [jax.ShapeDtypeStruct((N_DEV, n_rows, 128), F32)],
                              [lambda ref, b: ref.at[b]], "gather_small_grads")
    res = _small_sum_adamw(gathered, _pack_small(small_w + (zero,)), _pack_small(small_m + (zero,)),
                           _pack_small(small_v + (zero,)))
    shapes = [w.shape for w in small_w] + [(1, 1)]
    g_s, d_s, m_s, v_s = [_unpack_small(r, shapes) for r in res]
    total_loss = g_s[-1].reshape(())

    small_names = SMALL_NAMES[:-1]
    order = ("w_in", "pool_w", "pool_scale", "sgu_ln_g", "sgu_ln_b", "sgu_w", "sgu_b", "w_out", "ln_g", "ln_b",
             "ple_w", "ple_gate_w", "ple_gate_b")
    outs = [total_loss, dx.reshape(1, seq, D_MODEL)]
    for kind in range(4):
        for nm in order:
            key = "gate_w" if nm == "ple_gate_w" else nm
            if key in big_out:
                outs.append(big_out[key][kind])
            else:
                outs.append((g_s, d_s, m_s, v_s)[kind][small_names.index(nm)])
    return tuple(outs)
```

```python
import functools

import jax
import jax.numpy as jnp
from jax import lax
from jax.experimental import pallas as pl
from jax.experimental.pallas import tpu as pltpu

F32 = jnp.float32
BF16 = jnp.bfloat16

N_DEV = 8
D_MODEL = 1024
D_POOL = 1024
D_SGU = 1024
D_MIX = 2048
D_IN = 5120
D_PLE = 256
POOL_WINDOWS = (2, 4, 8, 16)
POOL_GROUP = 256
N_HEADS = 4
HEAD = 256
CHUNK = 128
HALO = 16
ALPHA = 2.0 ** 0.25
LN_EPS = 1e-5
ADAM_LR, ADAM_B1, ADAM_B2, ADAM_EPS, ADAM_WD, ADAM_STEP = 0.001, 0.9, 0.999, 1e-08, 0.01, 10

U0, V0, Z0 = D_POOL, D_POOL + D_SGU, D_POOL + 2 * D_SGU
VMEM_LIMIT = 56 * 1024 * 1024
MESH = pl.DeviceIdType.MESH
ANY = pl.BlockSpec(memory_space=pl.ANY)
VMEM_FULL = pl.BlockSpec(memory_space=pltpu.VMEM)

_GELU_C0 = 0.7978845608028654
_GELU_C1 = 0.044715


def _gelu_and_grad(x):
    x2 = x * x
    t = jnp.tanh(_GELU_C0 * (x + _GELU_C1 * (x * x2)))
    cdf = 0.5 * (1.0 + t)
    dg = cdf + x * (0.5 * (1.0 - t * t)) * (_GELU_C0 * (1.0 + (3.0 * _GELU_C1) * x2))
    return x * cdf, dg


def _gelu(x):
    t = jnp.tanh(_GELU_C0 * (x + _GELU_C1 * (x * x * x)))
    return x * (0.5 * (1.0 + t))


def _sigmoid(x):
    return 1.0 / (1.0 + jnp.exp(-x))


def _dot(a, b):
    return jnp.dot(a, b, preferred_element_type=F32)


def _dot_nt(a, b):
    return lax.dot_general(a, b, (((1,), (1,)), ((), ())), preferred_element_type=F32)


def _dot_tn(a, b):
    return lax.dot_general(a, b, (((0,), (0,)), ((), ())), preferred_element_type=F32)


def _row_stats(x):
    mu = jnp.mean(x, axis=-1, keepdims=True)
    xc = x - mu
    var = jnp.mean(xc * xc, axis=-1, keepdims=True)
    rstd = lax.rsqrt(var + LN_EPS)
    return xc * rstd, rstd


def _ln_bwd(dxhat, xhat, rstd):
    m1 = jnp.mean(dxhat, axis=-1, keepdims=True)
    m2 = jnp.mean(dxhat * xhat, axis=-1, keepdims=True)
    return rstd * (dxhat - m1 - xhat * m2)


def _masked_sgu_w(sw_ref, hh):
    row = lax.broadcasted_iota(jnp.int32, (CHUNK, CHUNK), 0)
    col = lax.broadcasted_iota(jnp.int32, (CHUNK, CHUNK), 1)
    return jnp.where(row >= col, sw_ref[hh], 0.0)


def _inv_count(tile_index, tok_tile, window):
    tok = tile_index * tok_tile + lax.broadcasted_iota(jnp.int32, (tok_tile, 1), 0)
    return 1.0 / jnp.minimum(tok + 1, window).astype(F32)


def _params(**kw):
    return pltpu.CompilerParams(vmem_limit_bytes=VMEM_LIMIT, **kw)


def _forward_mixers(x, w_in, pool_w, pool_scale, sgu_ln_g, sgu_ln_b, sgu_w, sgu_bias_tile, tok_tile):
    seq = x.shape[0]
    n_chunks = tok_tile // CHUNK

    def body(x_ref, win_ref, pw_ref, ps_ref, lg_ref, lb_ref, sw_ref, sb_ref,
             h_ref, y_ref, pooled_ref, xb_ref, aext_ref):
        i = pl.program_id(0)

        @pl.when(i == 0)
        def _():
            aext_ref[0:HALO, :] = jnp.zeros((HALO, D_POOL), F32)

        xb = x_ref[...].astype(BF16)
        xb_ref[...] = xb
        for s in range(D_IN // 1024):
            cs = slice(s * 1024, (s + 1) * 1024)
            h_ref[:, cs] = _dot(xb, win_ref[:, cs])

        aext_ref[HALO:HALO + tok_tile, :] = h_ref[:, 0:D_POOL]
        for g, window in enumerate(POOL_WINDOWS):
            cols = slice(g * POOL_GROUP, (g + 1) * POOL_GROUP)
            win = aext_ref[HALO:HALO + tok_tile, cols]
            for k in range(1, window):
                win = win + aext_ref[HALO - k:HALO - k + tok_tile, cols]
            pooled = (win * _inv_count(i, tok_tile, window)).astype(F32) - h_ref[:, cols]
            pb = pooled.astype(BF16)
            pooled_ref[:, cols] = pb
            mixed = _dot(pb, pw_ref[g])
            z = h_ref[:, Z0 + g * POOL_GROUP:Z0 + (g + 1) * POOL_GROUP]
            y_ref[:, cols] = (mixed * ps_ref[:, cols] * (z * _sigmoid(z))).astype(BF16)
        aext_ref[0:HALO, :] = aext_ref[tok_tile:tok_tile + HALO, :]

        for hh in range(N_HEADS):
            cols = slice(hh * HEAD, (hh + 1) * HEAD)
            swm = _masked_sgu_w(sw_ref, hh).astype(BF16)
            for n in range(n_chunks):
                rows = slice(n * CHUNK, (n + 1) * CHUNK)
                gu = _gelu(h_ref[rows, U0 + hh * HEAD:U0 + (hh + 1) * HEAD])
                gv = _gelu(h_ref[rows, V0 + hh * HEAD:V0 + (hh + 1) * HEAD])
                xhat, _ = _row_stats(gv)
                vln = xhat * lg_ref[:, cols] + lb_ref[:, cols]
                sv = _dot(swm, vln.astype(BF16)) + sb_ref[:, cols]
                z = h_ref[rows, Z0 + D_POOL + hh * HEAD:Z0 + D_POOL + (hh + 1) * HEAD]
                y_ref[rows, D_POOL + hh * HEAD:D_POOL + (hh + 1) * HEAD] = (
                    gu * sv * (z * _sigmoid(z))).astype(BF16)

    tok = lambda width: pl.BlockSpec((tok_tile, width), lambda i: (i, 0))
    return pl.pallas_call(
        body, name="forward_mixers",
        grid=(seq // tok_tile,),
        in_specs=[tok(D_MODEL), VMEM_FULL, VMEM_FULL, VMEM_FULL, VMEM_FULL, VMEM_FULL, VMEM_FULL, VMEM_FULL],
        out_specs=[tok(D_IN), tok(D_MIX), tok(D_POOL), tok(D_MODEL)],
        out_shape=[jax.ShapeDtypeStruct((seq, D_IN), F32), jax.ShapeDtypeStruct((seq, D_MIX), BF16),
                   jax.ShapeDtypeStruct((seq, D_POOL), BF16), jax.ShapeDtypeStruct((seq, D_MODEL), BF16)],
        scratch_shapes=[pltpu.VMEM((HALO + tok_tile, D_POOL), F32)],
        compiler_params=_params(dimension_semantics=("arbitrary",)),
    )(x, w_in, pool_w, pool_scale, sgu_ln_g, sgu_ln_b, sgu_w, sgu_bias_tile)


def _head_fwd_bwd(x, y, p, target, w_out, gate_w, ple_w, ln_g, ln_b, gate_b, tok_tile):
    seq = x.shape[0]

    def body(x_ref, y_ref, p_ref, t_ref, wout_ref, gw_ref, plw_ref, lng_ref, lnb_ref, gb_ref,
             dy_ref, dxr_ref, xn_ref, dgp_ref, dpe_ref, dr_ref, loss_ref, dlng_ref, dlnb_ref, dgb_ref):
        i = pl.program_id(0)

        @pl.when(i == 0)
        def _():
            loss_ref[...] = jnp.zeros_like(loss_ref)
            dlng_ref[...] = jnp.zeros_like(dlng_ref)
            dlnb_ref[...] = jnp.zeros_like(dlnb_ref)
            dgb_ref[...] = jnp.zeros_like(dgb_ref)

        r = ALPHA * x_ref[...] + _dot(y_ref[...], wout_ref[...])
        xhat, rstd = _row_stats(r)
        xn = xhat * lng_ref[...] + lnb_ref[...]
        xnb = xn.astype(BF16)
        xn_ref[...] = xnb
        gate = _sigmoid(_dot(xnb, gw_ref[...]) + gb_ref[...])
        pe = _dot(p_ref[...].astype(BF16), plw_ref[...])
        err = xn + gate * pe - t_ref[...]
        loss_ref[...] += (0.5 / D_MODEL) * jnp.sum(err * err, keepdims=True)

        dout = err * (1.0 / D_MODEL)
        dpe_ref[...] = (dout * gate).astype(BF16)
        dgpre = dout * pe * gate * (1.0 - gate)
        dgb_ref[...] += jnp.sum(dgpre, axis=0, keepdims=True)
        dgpb = dgpre.astype(BF16)
        dgp_ref[...] = dgpb
        dxn = dout + _dot_nt(dgpb, gw_ref[...])
        dlng_ref[...] += jnp.sum(dxn * xhat, axis=0, keepdims=True)
        dlnb_ref[...] += jnp.sum(dxn, axis=0, keepdims=True)
        dr = _ln_bwd(dxn * lng_ref[...], xhat, rstd)
        dxr_ref[...] = ALPHA * dr
        drb = dr.astype(BF16)
        dr_ref[...] = drb
        dy_ref[...] = _dot_nt(drb, wout_ref[...])

    tok = lambda width: pl.BlockSpec((tok_tile, width), lambda i: (i, 0))
    acc = lambda width: pl.BlockSpec((1, width), lambda i: (0, 0))
    vec = jax.ShapeDtypeStruct((1, D_MODEL), F32)
    return pl.pallas_call(
        body, name="head_fwd_bwd",
        grid=(seq // tok_tile,),
        in_specs=[tok(D_MODEL), tok(D_MIX), tok(D_PLE), tok(D_MODEL),
                  VMEM_FULL, VMEM_FULL, VMEM_FULL, VMEM_FULL, VMEM_FULL, VMEM_FULL],
        out_specs=[tok(D_MIX), tok(D_MODEL), tok(D_MODEL), tok(D_MODEL), tok(D_MODEL), tok(D_MODEL),
                   acc(1), acc(D_MODEL), acc(D_MODEL), acc(D_MODEL)],
        out_shape=[jax.ShapeDtypeStruct((seq, D_MIX), F32), jax.ShapeDtypeStruct((seq, D_MODEL), F32),
                   jax.ShapeDtypeStruct((seq, D_MODEL), BF16), jax.ShapeDtypeStruct((seq, D_MODEL), BF16),
                   jax.ShapeDtypeStruct((seq, D_MODEL), BF16), jax.ShapeDtypeStruct((seq, D_MODEL), BF16),
                   jax.ShapeDtypeStruct((1, 1), F32), vec, vec, vec],
        compiler_params=_params(dimension_semantics=("arbitrary",)),
    )(x, y, p, target, w_out, gate_w, ple_w, ln_g, ln_b, gate_b)


def _mixers_bwd(h, dy, dxr, pooled, w_in, pool_w, pool_scale, sgu_ln_g, sgu_ln_b, sgu_w, sgu_bias_tile, tok_tile):
    seq = h.shape[0]
    n_tiles = seq // tok_tile
    n_chunks = tok_tile // CHUNK
    pool_rows = POOL_GROUP // N_DEV

    def body(h_ref, dy_ref, dxr_ref, pooled_ref, win_ref, pw_ref, ps_ref, lg_ref, lb_ref, sw_ref, sb_ref,
             dh_ref, dx_ref, dpw_ref, dps_ref, dlg_ref, dlb_ref, dsw_ref, dsb_ref,
             qext_ref, dpw_acc, dsb_acc):
        i = pl.program_id(0)
        tile = n_tiles - 1 - i

        @pl.when(i == 0)
        def _():
            qext_ref[tok_tile:tok_tile + HALO, :] = jnp.zeros((HALO, D_POOL), F32)
            dpw_acc[...] = jnp.zeros_like(dpw_acc)
            dsb_acc[...] = jnp.zeros_like(dsb_acc)
            dps_ref[...] = jnp.zeros_like(dps_ref)
            dlg_ref[...] = jnp.zeros_like(dlg_ref)
            dlb_ref[...] = jnp.zeros_like(dlb_ref)
            dsw_ref[...] = jnp.zeros_like(dsw_ref)

        for g, window in enumerate(POOL_WINDOWS):
            cols = slice(g * POOL_GROUP, (g + 1) * POOL_GROUP)
            zcols = slice(Z0 + g * POOL_GROUP, Z0 + (g + 1) * POOL_GROUP)
            z = h_ref[:, zcols]
            sz = _sigmoid(z)
            pb = pooled_ref[:, cols]
            mixed = _dot(pb, pw_ref[g])
            dyp = dy_ref[:, cols]
            dh_ref[:, zcols] = (dyp * (mixed * ps_ref[:, cols]) * (sz * (1.0 + z * (1.0 - sz)))).astype(BF16)
            dms = dyp * (z * sz)
            dps_ref[:, cols] += jnp.sum(dms * mixed, axis=0, keepdims=True)
            dmixed = (dms * ps_ref[:, cols]).astype(BF16)
            dpw_acc[g] += _dot_tn(pb, dmixed)
            dpooled = _dot_nt(dmixed, pw_ref[g])
            qext_ref[0:tok_tile, cols] = dpooled * _inv_count(tile, tok_tile, window)
            da = qext_ref[0:tok_tile, cols] - dpooled
            for k in range(1, window):
                da = da + qext_ref[k:k + tok_tile, cols]
            dh_ref[:, cols] = da.astype(BF16)
        qext_ref[tok_tile:tok_tile + HALO, :] = qext_ref[0:HALO, :]

        for hh in range(N_HEADS):
            cols = slice(hh * HEAD, (hh + 1) * HEAD)
            ucols = slice(U0 + hh * HEAD, U0 + (hh + 1) * HEAD)
            vcols = slice(V0 + hh * HEAD, V0 + (hh + 1) * HEAD)
            zcols = slice(Z0 + D_POOL + hh * HEAD, Z0 + D_POOL + (hh + 1) * HEAD)
            sw32 = _masked_sgu_w(sw_ref, hh)
            swm = sw32.astype(BF16)
            swm_t = sw32.T.astype(BF16)
            for n in range(n_chunks):
                rows = slice(n * CHUNK, (n + 1) * CHUNK)
                gu, dgu_du = _gelu_and_grad(h_ref[rows, ucols])
                gv, dgv_dv = _gelu_and_grad(h_ref[rows, vcols])
                xhat, rstd = _row_stats(gv)
                vb = (xhat * lg_ref[:, cols] + lb_ref[:, cols]).astype(BF16)
                sv = _dot(swm, vb) + sb_ref[:, cols]
                z = h_ref[rows, zcols]
                sz = _sigmoid(z)
                dys = dy_ref[rows, D_POOL + hh * HEAD:D_POOL + (hh + 1) * HEAD]
                dh_ref[rows, zcols] = (dys * (gu * sv) * (sz * (1.0 + z * (1.0 - sz)))).astype(BF16)
                dyg = dys * (z * sz)
                dh_ref[rows, ucols] = (dyg * sv * dgu_du).astype(BF16)
                dsv = dyg * gu
                dsb_acc[:, cols] += dsv
                dsvb = dsv.astype(BF16)
                dsw_ref[hh] += _dot_nt(dsvb, vb)
                dvln = _dot(swm_t, dsvb)
                dlg_ref[:, cols] += jnp.sum(dvln * xhat, axis=0, keepdims=True)
                dlb_ref[:, cols] += jnp.sum(dvln, axis=0, keepdims=True)
                dgv = _ln_bwd(dvln * lg_ref[:, cols], xhat, rstd)
                dh_ref[rows, vcols] = (dgv * dgv_dv).astype(BF16)

        dx_ref[...] = dxr_ref[...] + _dot_nt(dh_ref[...], win_ref[...])

        @pl.when(i == n_tiles - 1)
        def _():
            for g in range(len(POOL_WINDOWS)):
                for b in range(N_DEV):
                    dpw_ref[b, g] = dpw_acc[g, b * pool_rows:(b + 1) * pool_rows, :]
            row = lax.broadcasted_iota(jnp.int32, (CHUNK, CHUNK), 0)
            col = lax.broadcasted_iota(jnp.int32, (CHUNK, CHUNK), 1)
            for hh in range(N_HEADS):
                dsw_ref[hh] = jnp.where(row >= col, dsw_ref[hh], 0.0)
                total = jnp.sum(dsb_acc[:, hh * HEAD:(hh + 1) * HEAD], axis=1, keepdims=True)
                dsb_ref[hh] = jnp.broadcast_to(total, (CHUNK, CHUNK))

    tok = lambda width: pl.BlockSpec((tok_tile, width), lambda i: (n_tiles - 1 - i, 0))
    whole = lambda shape: pl.BlockSpec(shape, lambda i: (0,) * len(shape))
    vec = jax.ShapeDtypeStruct((1, D_MODEL), F32)
    return pl.pallas_call(
        body, name="mixers_bwd",
        grid=(n_tiles,),
        in_specs=[tok(D_IN), tok(D_MIX), tok(D_MODEL), tok(D_POOL),
                  VMEM_FULL, VMEM_FULL, VMEM_FULL, VMEM_FULL, VMEM_FULL, VMEM_FULL, VMEM_FULL],
        out_specs=[tok(D_IN), tok(D_MODEL), whole((N_DEV, len(POOL_WINDOWS), pool_rows, POOL_GROUP)),
                   whole((1, D_POOL)), whole((1, D_SGU)), whole((1, D_SGU)),
                   whole((N_HEADS, CHUNK, CHUNK)), whole((N_HEADS, CHUNK, CHUNK))],
        out_shape=[jax.ShapeDtypeStruct((seq, D_IN), BF16), jax.ShapeDtypeStruct((seq, D_MODEL), F32),
                   jax.ShapeDtypeStruct((N_DEV, len(POOL_WINDOWS), pool_rows, POOL_GROUP), F32),
                   vec, vec, vec,
                   jax.ShapeDtypeStruct((N_HEADS, CHUNK, CHUNK), F32),
                   jax.ShapeDtypeStruct((N_HEADS, CHUNK, CHUNK), F32)],
        scratch_shapes=[pltpu.VMEM((tok_tile + HALO, D_POOL), F32),
                        pltpu.VMEM((len(POOL_WINDOWS), POOL_GROUP, POOL_GROUP), F32),
                        pltpu.VMEM((CHUNK, D_SGU), F32)],
        compiler_params=_params(dimension_semantics=("arbitrary",)),
    )(h, dy, dxr, pooled, w_in, pool_w, pool_scale, sgu_ln_g, sgu_ln_b, sgu_w, sgu_bias_tile)


def _weight_grad(a, b, n_col_blocks, blocks_per_step, tok_tile, name):
    seq, m = a.shape
    n = b.shape[1]
    nb = n // n_col_blocks
    n_steps = n_col_blocks // blocks_per_step

    def body(a_ref, b_ref, out_ref):
        @pl.when(pl.program_id(1) == 0)
        def _():
            out_ref[...] = jnp.zeros_like(out_ref)

        res = _dot_tn(a_ref[...].astype(BF16), b_ref[...])
        for blk in range(blocks_per_step):
            out_ref[blk] += res[:, blk * nb:(blk + 1) * nb]

    return pl.pallas_call(
        body, name=name,
        grid=(n_steps, seq // tok_tile),
        in_specs=[pl.BlockSpec((tok_tile, m), lambda j, k: (k, 0)),
                  pl.BlockSpec((tok_tile, blocks_per_step * nb), lambda j, k: (k, j))],
        out_specs=pl.BlockSpec((blocks_per_step, m, nb), lambda j, k: (j, 0, 0)),
        out_shape=jax.ShapeDtypeStruct((n_col_blocks, m, nb), F32),
        compiler_params=_params(dimension_semantics=("arbitrary", "arbitrary")),
    )(a, b)


def _place():
    return lax.axis_index("x"), lax.axis_index("y"), lax.axis_index("c")


def _other_chips(x, y):
    return [(1 - x, y), (x, 1 - y), (1 - x, 1 - y)]


def _all_gather(shards, out_shapes, views, name):
    n = len(shards)

    def body(*refs):
        ins, outs = refs[:n], refs[n:2 * n]
        send_sems, recv_sems, local_sems = refs[2 * n:]
        x, y, c = _place()
        me, sibling = (x, y, c), (x, y, 1 - c)
        chips = _other_chips(x, y)

        def copy(k, s, block, to, src=None):
            px, py, pc = block
            dst = views[k](outs[k], 4 * px + 2 * py + pc)
            return pltpu.make_async_remote_copy(
                src_ref=dst if src is None else src, dst_ref=dst,
                send_sem=send_sems.at[k, s], recv_sem=recv_sems.at[k, s],
                device_id=to, device_id_type=MESH)

        started = []
        for k in range(n):
            mine = pltpu.make_async_copy(ins[k], views[k](outs[k], 4 * x + 2 * y + c), local_sems.at[k])
            mine.start()
            started.append(mine)
        sends = []
        for k in range(n):
            sends.append(copy(k, 0, me, sibling, src=ins[k]))
            sends += [copy(k, 1 + j, me, (*chip, c), src=ins[k]) for j, chip in enumerate(chips)]
        for cp in sends:
            cp.start()
        for k in range(n):
            for j, chip in enumerate(chips):
                copy(k, 1 + j, (*chip, c), me).wait_recv()
                fwd = copy(k, 4 + j, (*chip, c), sibling)
                fwd.start()
                sends.append(fwd)
        for k in range(n):
            copy(k, 0, sibling, me).wait_recv()
            for j, chip in enumerate(chips):
                copy(k, 4 + j, (*chip, 1 - c), me).wait_recv()
        for cp in sends:
            cp.wait_send()
        for mine in started:
            mine.wait()

    return pl.pallas_call(
        body, name=name,
        in_specs=[ANY] * n, out_specs=[ANY] * n,
        out_shape=out_shapes,
        scratch_shapes=[pltpu.SemaphoreType.DMA((n, 7)), pltpu.SemaphoreType.DMA((n, 7)),
                        pltpu.SemaphoreType.DMA((n,))],
    )(*shards)


def _pair_exchange(grads, name):
    n = len(grads)

    def body(*refs):
        ins, theirs = refs[:n], refs[n:2 * n]
        send_sems, recv_sems = refs[2 * n:]
        x, y, c = _place()
        remote = []
        for k in range(n):
            for j in range(4):
                rc = pltpu.make_async_remote_copy(
                    src_ref=ins[k].at[2 * j + (1 - c)], dst_ref=theirs[k].at[j],
                    send_sem=send_sems.at[k, j], recv_sem=recv_sems.at[k, j],
                    device_id=(x, y, 1 - c), device_id_type=MESH)
                rc.start()
                remote.append(rc)
        for rc in remote:
            rc.wait_recv()
        for rc in remote:
            rc.wait_send()

    return pl.pallas_call(
        body, name=name,
        in_specs=[ANY] * n, out_specs=[ANY] * n,
        out_shape=[jax.ShapeDtypeStruct((4,) + g.shape[1:], g.dtype) for g in grads],
        scratch_shapes=[pltpu.SemaphoreType.DMA((n, 4)), pltpu.SemaphoreType.DMA((n, 4))],
    )(*grads)


def _chip_exchange(sums, name):
    n = len(sums)

    def body(*refs):
        ins, others = refs[:n], refs[n:2 * n]
        send_sems, recv_sems = refs[2 * n:]
        x, y, c = _place()
        chips = _other_chips(x, y)
        remote = []
        for k in range(n):
            for r, (px, py) in enumerate(chips):
                rc = pltpu.make_async_remote_copy(
                    src_ref=ins[k].at[2 * px + py], dst_ref=others[k].at[r],
                    send_sem=send_sems.at[k, r], recv_sem=recv_sems.at[k, r],
                    device_id=(px, py, c), device_id_type=MESH)
                rc.start()
                remote.append(rc)
        for rc in remote:
            rc.wait_recv()
        for rc in remote:
            rc.wait_send()

    return pl.pallas_call(
        body, name=name,
        in_specs=[ANY] * n, out_specs=[ANY] * n,
        out_shape=[jax.ShapeDtypeStruct((3,) + s.shape[1:], s.dtype) for s in sums],
        scratch_shapes=[pltpu.SemaphoreType.DMA((n, 3)), pltpu.SemaphoreType.DMA((n, 3))],
    )(*sums)


def _row_tile(rows, cols):
    tile = rows
    while tile * cols > 256 * 1024 and tile % 16 == 0:
        tile //= 2
    return tile


def _pair_sum(core, grads, theirs, name):
    _, rows, cols = theirs.shape
    rt = _row_tile(rows, cols)

    def body(core_ref, a_ref, b_ref, o_ref, ob_ref):
        total = a_ref[...] + b_ref[...]
        o_ref[...] = total
        ob_ref[...] = total.astype(BF16)

    spec = pl.BlockSpec((None, rt, cols), lambda j, i, core_ref: (j, i, 0))
    mine = pl.BlockSpec((None, None, rt, cols), lambda j, i, core_ref: (j, core_ref[0], i, 0))
    return pl.pallas_call(
        body, name=name,
        grid_spec=pltpu.PrefetchScalarGridSpec(
            num_scalar_prefetch=1, grid=(4, rows // rt), in_specs=[mine, spec], out_specs=[spec, spec]),
        out_shape=[jax.ShapeDtypeStruct(theirs.shape, F32), jax.ShapeDtypeStruct(theirs.shape, BF16)],
        compiler_params=_params(dimension_semantics=("arbitrary", "arbitrary")),
    )(core, grads.reshape(4, 2, rows, cols), theirs)


def _adamw(w, g, m, v):
    m = ADAM_B1 * m + (1.0 - ADAM_B1) * g
    v = ADAM_B2 * v + (1.0 - ADAM_B2) * (g * g)
    m_hat = m / (1.0 - ADAM_B1 ** ADAM_STEP)
    v_hat = v / (1.0 - ADAM_B2 ** ADAM_STEP)
    delta = -ADAM_LR * (m_hat / (jnp.sqrt(v_hat) + ADAM_EPS) + ADAM_WD * w)
    return delta, m, v


def _sum_adamw(chip, sums, others, w, m, v, name):
    _, rows, cols = sums.shape
    rt = _row_tile(rows, cols)

    def body(chip_ref, own_ref, oth_ref, w_ref, m_ref, v_ref, g_ref, d_ref, nm_ref, nv_ref):
        g = ((own_ref[...] + oth_ref[0].astype(F32)) + oth_ref[1].astype(F32)) + oth_ref[2].astype(F32)
        g_ref[...] = g
        d_ref[...], nm_ref[...], nv_ref[...] = _adamw(w_ref[...], g, m_ref[...], v_ref[...])

    spec = pl.BlockSpec((rt, cols), lambda i, chip_ref: (i, 0))
    own = pl.BlockSpec((None, rt, cols), lambda i, chip_ref: (chip_ref[0], i, 0))
    shape = jax.ShapeDtypeStruct((rows, cols), F32)
    return pl.pallas_call(
        body, name=name,
        grid_spec=pltpu.PrefetchScalarGridSpec(
            num_scalar_prefetch=1, grid=(rows // rt,),
            in_specs=[own, pl.BlockSpec((3, rt, cols), lambda i, chip_ref: (0, i, 0)), spec, spec, spec],
            out_specs=[spec] * 4),
        out_shape=[shape] * 4,
        compiler_params=_params(dimension_semantics=("arbitrary",)),
    )(chip, sums, others, w, m, v)


def _small_sum_adamw(gathered, w, m, v):
    def body(g8_ref, w_ref, m_ref, v_ref, g_ref, d_ref, nm_ref, nv_ref):
        g = g8_ref[0]
        for b in range(1, N_DEV):
            g = g + g8_ref[b]
        g_ref[...] = g
        d_ref[...], nm_ref[...], nv_ref[...] = _adamw(w_ref[...], g, m_ref[...], v_ref[...])

    shape = jax.ShapeDtypeStruct(w.shape, F32)
    return pl.pallas_call(
        body, name="small_sum_adamw",
        in_specs=[VMEM_FULL] * 4, out_specs=[VMEM_FULL] * 4, out_shape=[shape] * 4,
        compiler_params=_params(),
    )(gathered, w, m, v)


SMALL_NAMES = ("pool_scale", "sgu_ln_g", "sgu_ln_b", "sgu_w", "sgu_b", "ln_g", "ln_b", "ple_gate_b", "loss")


def _pack_small(parts):
    rows = []
    for a in parts:
        a = a.reshape(-1).astype(F32)
        pad = (-a.shape[0]) % 1024
        rows.append(jnp.pad(a, (0, pad)).reshape(-1, 128))
    return jnp.concatenate(rows, axis=0)


def _unpack_small(packed, shapes):
    out, row = [], 0
    for shape in shapes:
        size = 1
        for d in shape:
            size *= d
        n_rows = (size + 1023) // 1024 * 8
        out.append(packed[row:row + n_rows].reshape(-1)[:size].reshape(shape))
        row += n_rows
    return out


def _local_step(x, p, target, w_in, pool_w, w_out, ple_w, gate_w,
                pool_scale, sgu_ln_g, sgu_ln_b, sgu_w, sgu_b, ln_g, ln_b, gate_b,
                tok_tile=256, grad_tile=512):
    bias_tile = jnp.repeat(sgu_b.T, HEAD, axis=1)
    h, y, pooled, xb = _forward_mixers(x, w_in, pool_w, pool_scale, sgu_ln_g, sgu_ln_b, sgu_w, bias_tile, tok_tile)
    dy, dxr, xn, dgp, dpe, dr, loss, d_ln_g, d_ln_b, d_gate_b = _head_fwd_bwd(
        x, y, p, target, w_out, gate_w, ple_w, ln_g, ln_b, gate_b, tok_tile)
    dh, dx, d_pool_w, d_pool_scale, d_sgu_ln_g, d_sgu_ln_b, d_sgu_w, d_sgu_b_tiles = _mixers_bwd(
        h, dy, dxr, pooled, w_in, pool_w, pool_scale, sgu_ln_g, sgu_ln_b, sgu_w, bias_tile, tok_tile)
    d_w_in = _weight_grad(xb, dh, N_DEV, 2, grad_tile, "grad_w_in")
    d_w_out = _weight_grad(y, dr, 1, 1, grad_tile, "grad_w_out")
    d_ple_w = _weight_grad(p, dpe, N_DEV, N_DEV, grad_tile, "grad_ple_w")
    d_gate_w = _weight_grad(xn, dgp, 1, 1, grad_tile, "grad_gate_w")
    d_sgu_b = d_sgu_b_tiles[:, :, 0]
    small = (d_pool_scale, d_sgu_ln_g, d_sgu_ln_b, d_sgu_w, d_sgu_b, d_ln_g, d_ln_b, d_gate_b)
    return loss, dx, (d_w_in, d_pool_w, d_w_out, d_ple_w, d_gate_w), small


def kernel(x, p, w_in, pool_w, pool_scale, sgu_ln_g, sgu_ln_b, sgu_w, sgu_b, w_out, ln_g, ln_b, ple_w, ple_gate_w, ple_gate_b, loss_target, m_w_in, m_pool_w, m_pool_scale, m_sgu_ln_g, m_sgu_ln_b, m_sgu_w, m_sgu_b, m_w_out, m_ln_g, m_ln_b, m_ple_w, m_ple_gate_w, m_ple_gate_b, v_w_in, v_pool_w, v_pool_scale, v_sgu_ln_g, v_sgu_ln_b, v_sgu_w, v_sgu_b, v_w_out, v_ln_g, v_ln_b, v_ple_w, v_ple_gate_w, v_ple_gate_b):
    seq = x.shape[1]

    shards = [w_in[0].astype(BF16), pool_w[0].astype(BF16), w_out[0].astype(BF16),
              ple_w[0].astype(BF16), ple_gate_w[0].astype(BF16)]
    full_shapes = [jax.ShapeDtypeStruct((D_MODEL, D_IN), BF16),
                   jax.ShapeDtypeStruct((len(POOL_WINDOWS), POOL_GROUP, POOL_GROUP), BF16),
                   jax.ShapeDtypeStruct((D_MIX, D_MODEL), BF16),
                   jax.ShapeDtypeStruct((D_PLE, D_MODEL), BF16),
                   jax.ShapeDtypeStruct((D_MODEL, D_MODEL), BF16)]
    cols = lambda width: (lambda ref, b: ref.at[:, pl.ds(pl.multiple_of(b * width, 128), width)])
    rows = lambda height: (lambda ref, b: ref.at[pl.ds(pl.multiple_of(b * height, 16), height), :])
    pool_rows = POOL_GROUP // N_DEV
    views = [cols(D_IN // N_DEV),
             lambda ref, b: ref.at[:, pl.ds(pl.multiple_of(b * pool_rows, 16), pool_rows), :],
             rows(D_MIX // N_DEV), cols(D_MODEL // N_DEV), rows(D_MODEL // N_DEV)]
    w_in_f, pool_w_f, w_out_f, ple_w_f, gate_w_f = _all_gather(shards, full_shapes, views, "gather_weights")

    loss, dx, big, small = _local_step(
        x[0], p[0, 0], loss_target[0], w_in_f, pool_w_f, w_out_f, ple_w_f, gate_w_f,
        pool_scale, sgu_ln_g, sgu_ln_b, sgu_w[0], sgu_b[0], ln_g, ln_b, ple_gate_b)
    d_w_in, d_pool_w, d_w_out, d_ple_w, d_gate_w = big

    blocks = [d_w_in,
              d_pool_w.reshape(N_DEV, len(POOL_WINDOWS) * pool_rows, POOL_GROUP),
              d_w_out.reshape(N_DEV, D_MIX // N_DEV, D_MODEL),
              d_ple_w,
              d_gate_w.reshape(N_DEV, D_MODEL // N_DEV, D_MODEL)]
    names = ("w_in", "pool_w", "w_out", "ple_w", "gate_w")
    core = lax.axis_index("c").astype(jnp.int32).reshape(1)
    chip = (2 * lax.axis_index("x") + lax.axis_index("y")).astype(jnp.int32).reshape(1)
    theirs = _pair_exchange(blocks, "pair_exchange")
    sums = [_pair_sum(core, g, t, "pair_sum_" + nm) for g, t, nm in zip(blocks, theirs, names)]
    others = _chip_exchange([s_bf for _, s_bf in sums], "chip_exchange")

    big_w = (w_in, pool_w, w_out, ple_w, ple_gate_w)
    big_m = (m_w_in, m_pool_w, m_w_out, m_ple_w, m_ple_gate_w)
    big_v = (v_w_in, v_pool_w, v_w_out, v_ple_w, v_ple_gate_w)
    big_out = {}
    for nm, (s_f32, _), oth, w, m, v in zip(names, sums, others, big_w, big_m, big_v):
        two_d = s_f32.shape[1:]
        res = _sum_adamw(chip, s_f32, oth, w.reshape(two_d), m.reshape(two_d), v.reshape(two_d), "adamw_" + nm)
        big_out[nm] = [r.reshape(w.shape) for r in res]

    small_w = (pool_scale, sgu_ln_g, sgu_ln_b, sgu_w, sgu_b, ln_g, ln_b, ple_gate_b)
    small_m = (m_pool_scale, m_sgu_ln_g, m_sgu_ln_b, m_sgu_w, m_sgu_b, m_ln_g, m_ln_b, m_ple_gate_b)
    small_v = (v_pool_scale, v_sgu_ln_g, v_sgu_ln_b, v_sgu_w, v_sgu_b, v_ln_g, v_ln_b, v_ple_gate_b)
    zero = jnp.zeros((1, 1), F32)
    packed_g = _pack_small(small + (loss,))
    n_rows = packed_g.shape[0]
    (gathered,) = _all_gather([packed_g], [jax.ShapeDtypeStruct((N_DEV, n_rows, 128), F32)],
                              [lambda ref, b: ref.at[b]], "gather_small_grads")
    res = _small_sum_adamw(gathered, _pack_small(small_w + (zero,)), _pack_small(small_m + (zero,)),
                           _pack_small(small_v + (zero,)))
    shapes = [w.shape for w in small_w] + [(1, 1)]
    g_s, d_s, m_s, v_s = [_unpack_small(r, shapes) for r in res]
    total_loss = g_s[-1].reshape(())

    small_names = SMALL_NAMES[:-1]
    order = ("w_in", "pool_w", "pool_scale", "sgu_ln_g", "sgu_ln_b", "sgu_w", "sgu_b", "w_out", "ln_g", "ln_b",
             "ple_w", "ple_gate_w", "ple_gate_b")
    outs = [total_loss, dx.reshape(1, seq, D_MODEL)]
    for kind in range(4):
        for nm in order:
            key = "gate_w" if nm == "ple_gate_w" else nm
            if key in big_out:
                outs.append(big_out[key][kind])
            else:
                outs.append((g_s, d_s, m_s, v_s)[kind][small_names.index(nm)])
    return tuple(outs)
```

```python
import functools

import jax
import jax.numpy as jnp
from jax import lax
from jax.experimental import pallas as pl
from jax.experimental.pallas import tpu as pltpu

F32 = jnp.float32
BF16 = jnp.bfloat16

N_DEV = 8
D_MODEL = 1024
D_POOL = 1024
D_SGU = 1024
D_MIX = 2048
D_IN = 5120
D_PLE = 256
POOL_WINDOWS = (2, 4, 8, 16)
POOL_GROUP = 256
N_HEADS = 4
HEAD = 256
CHUNK = 128
HALO = 16
ALPHA = 2.0 ** 0.25
LN_EPS = 1e-5
ADAM_LR, ADAM_B1, ADAM_B2, ADAM_EPS, ADAM_WD, ADAM_STEP = 0.001, 0.9, 0.999, 1e-08, 0.01, 10

U0, V0, Z0 = D_POOL, D_POOL + D_SGU, D_POOL + 2 * D_SGU
VMEM_LIMIT = 56 * 1024 * 1024
MESH = pl.DeviceIdType.MESH
ANY = pl.BlockSpec(memory_space=pl.ANY)
VMEM_FULL = pl.BlockSpec(memory_space=pltpu.VMEM)

_GELU_C0 = 0.7978845608028654
_GELU_C1 = 0.044715


def _gelu_and_grad(x):
    x2 = x * x
    t = jnp.tanh(_GELU_C0 * (x + _GELU_C1 * (x * x2)))
    cdf = 0.5 * (1.0 + t)
    dg = cdf + x * (0.5 * (1.0 - t * t)) * (_GELU_C0 * (1.0 + (3.0 * _GELU_C1) * x2))
    return x * cdf, dg


def _gelu(x):
    t = jnp.tanh(_GELU_C0 * (x + _GELU_C1 * (x * x * x)))
    return x * (0.5 * (1.0 + t))


def _sigmoid(x):
    return 1.0 / (1.0 + jnp.exp(-x))


def _dot(a, b):
    return jnp.dot(a, b, preferred_element_type=F32)


def _dot_nt(a, b):
    return lax.dot_general(a, b, (((1,), (1,)), ((), ())), preferred_element_type=F32)


def _dot_tn(a, b):
    return lax.dot_general(a, b, (((0,), (0,)), ((), ())), preferred_element_type=F32)


def _row_stats(x):
    mu = jnp.mean(x, axis=-1, keepdims=True)
    xc = x - mu
    var = jnp.mean(xc * xc, axis=-1, keepdims=True)
    rstd = lax.rsqrt(var + LN_EPS)
    return xc * rstd, rstd


def _ln_bwd(dxhat, xhat, rstd):
    m1 = jnp.mean(dxhat, axis=-1, keepdims=True)
    m2 = jnp.mean(dxhat * xhat, axis=-1, keepdims=True)
    return rstd * (dxhat - m1 - xhat * m2)


def _masked_sgu_w(sw_ref, hh):
    row = lax.broadcasted_iota(jnp.int32, (CHUNK, CHUNK), 0)
    col = lax.broadcasted_iota(jnp.int32, (CHUNK, CHUNK), 1)
    return jnp.where(row >= col, sw_ref[hh], 0.0)


def _inv_count(tile_index, tok_tile, window):
    tok = tile_index * tok_tile + lax.broadcasted_iota(jnp.int32, (tok_tile, 1), 0)
    return 1.0 / jnp.minimum(tok + 1, window).astype(F32)


def _params(**kw):
    return pltpu.CompilerParams(vmem_limit_bytes=VMEM_LIMIT, **kw)


def _forward_mixers(x, w_in, pool_w, pool_scale, sgu_ln_g, sgu_ln_b, sgu_w, sgu_bias_tile, tok_tile):
    seq = x.shape[0]
    n_chunks = tok_tile // CHUNK

    def body(x_ref, win_ref, pw_ref, ps_ref, lg_ref, lb_ref, sw_ref, sb_ref,
             h_ref, y_ref, pooled_ref, xb_ref, aext_ref):
        i = pl.program_id(0)

        @pl.when(i == 0)
        def _():
            aext_ref[0:HALO, :] = jnp.zeros((HALO, D_POOL), F32)

        xb = x_ref[...].astype(BF16)
        xb_ref[...] = xb
        for s in range(D_IN // 1024):
            cs = slice(s * 1024, (s + 1) * 1024)
            h_ref[:, cs] = _dot(xb, win_ref[:, cs])

        aext_ref[HALO:HALO + tok_tile, :] = h_ref[:, 0:D_POOL]
        for g, window in enumerate(POOL_WINDOWS):
            cols = slice(g * POOL_GROUP, (g + 1) * POOL_GROUP)
            win = aext_ref[HALO:HALO + tok_tile, cols]
            for k in range(1, window):
                win = win + aext_ref[HALO - k:HALO - k + tok_tile, cols]
            pooled = (win * _inv_count(i, tok_tile, window)).astype(F32) - h_ref[:, cols]
            pb = pooled.astype(BF16)
            pooled_ref[:, cols] = pb
            mixed = _dot(pb, pw_ref[g])
            z = h_ref[:, Z0 + g * POOL_GROUP:Z0 + (g + 1) * POOL_GROUP]
            y_ref[:, cols] = (mixed * ps_ref[:, cols] * (z * _sigmoid(z))).astype(BF16)
        aext_ref[0:HALO, :] = aext_ref[tok_tile:tok_tile + HALO, :]

        for hh in range(N_HEADS):
            cols = slice(hh * HEAD, (hh + 1) * HEAD)
            swm = _masked_sgu_w(sw_ref, hh).astype(BF16)
            for n in range(n_chunks):
                rows = slice(n * CHUNK, (n + 1) * CHUNK)
                gu = _gelu(h_ref[rows, U0 + hh * HEAD:U0 + (hh + 1) * HEAD])
                gv = _gelu(h_ref[rows, V0 + hh * HEAD:V0 + (hh + 1) * HEAD])
                xhat, _ = _row_stats(gv)
                vln = xhat * lg_ref[:, cols] + lb_ref[:, cols]
                sv = _dot(swm, vln.astype(BF16)) + sb_ref[:, cols]
                z = h_ref[rows, Z0 + D_POOL + hh * HEAD:Z0 + D_POOL + (hh + 1) * HEAD]
                y_ref[rows, D_POOL + hh * HEAD:D_POOL + (hh + 1) * HEAD] = (
                    gu * sv * (z * _sigmoid(z))).astype(BF16)

    tok = lambda width: pl.BlockSpec((tok_tile, width), lambda i: (i, 0))
    return pl.pallas_call(
        body, name="forward_mixers",
        grid=(seq // tok_tile,),
        in_specs=[tok(D_MODEL), VMEM_FULL, VMEM_FULL, VMEM_FULL, VMEM_FULL, VMEM_FULL, VMEM_FULL, VMEM_FULL],
        out_specs=[tok(D_IN), tok(D_MIX), tok(D_POOL), tok(D_MODEL)],
        out_shape=[jax.ShapeDtypeStruct((seq, D_IN), F32), jax.ShapeDtypeStruct((seq, D_MIX), BF16),
                   jax.ShapeDtypeStruct((seq, D_POOL), BF16), jax.ShapeDtypeStruct((seq, D_MODEL), BF16)],
        scratch_shapes=[pltpu.VMEM((HALO + tok_tile, D_POOL), F32)],
        compiler_params=_params(dimension_semantics=("arbitrary",)),
    )(x, w_in, pool_w, pool_scale, sgu_ln_g, sgu_ln_b, sgu_w, sgu_bias_tile)


def _head_fwd_bwd(x, y, p, target, w_out, gate_w, ple_w, ln_g, ln_b, gate_b, tok_tile, sub_tile):
    seq = x.shape[0]

    def body(x_ref, y_ref, p_ref, t_ref, wout_ref, gw_ref, plw_ref, lng_ref, lnb_ref, gb_ref,
             dy_ref, dxr_ref, xn_ref, dgp_ref, dpe_ref, dr_ref, loss_ref, dlng_ref, dlnb_ref, dgb_ref):
        i = pl.program_id(0)

        @pl.when(i == 0)
        def _():
            loss_ref[...] = jnp.zeros_like(loss_ref)
            dlng_ref[...] = jnp.zeros_like(dlng_ref)
            dlnb_ref[...] = jnp.zeros_like(dlnb_ref)
            dgb_ref[...] = jnp.zeros_like(dgb_ref)

        subs = [slice(s * sub_tile, (s + 1) * sub_tile) for s in range(tok_tile // sub_tile)]
        stats, xns, douts = [], [], []
        for rows in subs:
            r = ALPHA * x_ref[rows, :] + _dot(y_ref[rows, :], wout_ref[...])
            xhat, rstd = _row_stats(r)
            xn = xhat * lng_ref[...] + lnb_ref[...]
            xn_ref[rows, :] = xn.astype(BF16)
            stats.append((xhat, rstd))
            xns.append(xn)
        loss = jnp.zeros((1, 1), F32)
        dgb = jnp.zeros((1, D_MODEL), F32)
        for rows, xn in zip(subs, xns):
            gate = _sigmoid(_dot(xn_ref[rows, :], gw_ref[...]) + gb_ref[...])
            pe = _dot(p_ref[rows, :].astype(BF16), plw_ref[...])
            err = xn + gate * pe - t_ref[rows, :]
            loss = loss + jnp.sum(err * err, keepdims=True)
            dout = err * (1.0 / D_MODEL)
            dpe_ref[rows, :] = (dout * gate).astype(BF16)
            dgpre = dout * pe * gate * (1.0 - gate)
            dgb = dgb + jnp.sum(dgpre, axis=0, keepdims=True)
            dgp_ref[rows, :] = dgpre.astype(BF16)
            douts.append(dout)
        loss_ref[...] += (0.5 / D_MODEL) * loss
        dgb_ref[...] += dgb
        dlng = jnp.zeros((1, D_MODEL), F32)
        dlnb = jnp.zeros((1, D_MODEL), F32)
        for rows, (xhat, rstd), dout in zip(subs, stats, douts):
            dxn = dout + _dot_nt(dgp_ref[rows, :], gw_ref[...])
            dlng = dlng + jnp.sum(dxn * xhat, axis=0, keepdims=True)
            dlnb = dlnb + jnp.sum(dxn, axis=0, keepdims=True)
            dr = _ln_bwd(dxn * lng_ref[...], xhat, rstd)
            dxr_ref[rows, :] = ALPHA * dr
            dr_ref[rows, :] = dr.astype(BF16)
        dlng_ref[...] += dlng
        dlnb_ref[...] += dlnb
        for rows in subs:
            dy_ref[rows, :] = _dot_nt(dr_ref[rows, :], wout_ref[...])

    tok = lambda width: pl.BlockSpec((tok_tile, width), lambda i: (i, 0))
    acc = lambda width: pl.BlockSpec((1, width), lambda i: (0, 0))
    vec = jax.ShapeDtypeStruct((1, D_MODEL), F32)
    return pl.pallas_call(
        body, name="head_fwd_bwd",
        grid=(seq // tok_tile,),
        in_specs=[tok(D_MODEL), tok(D_MIX), tok(D_PLE), tok(D_MODEL),
                  VMEM_FULL, VMEM_FULL, VMEM_FULL, VMEM_FULL, VMEM_FULL, VMEM_FULL],
        out_specs=[tok(D_MIX), tok(D_MODEL), tok(D_MODEL), tok(D_MODEL), tok(D_MODEL), tok(D_MODEL),
                   acc(128), acc(D_MODEL), acc(D_MODEL), acc(D_MODEL)],
        out_shape=[jax.ShapeDtypeStruct((seq, D_MIX), F32), jax.ShapeDtypeStruct((seq, D_MODEL), F32),
                   jax.ShapeDtypeStruct((seq, D_MODEL), BF16), jax.ShapeDtypeStruct((seq, D_MODEL), BF16),
                   jax.ShapeDtypeStruct((seq, D_MODEL), BF16), jax.ShapeDtypeStruct((seq, D_MODEL), BF16),
                   jax.ShapeDtypeStruct((1, 128), F32), vec, vec, vec],
        compiler_params=_params(dimension_semantics=("arbitrary",)),
    )(x, y, p, target, w_out, gate_w, ple_w, ln_g, ln_b, gate_b)


def _mixers_bwd(h, dy, dxr, pooled, w_in, pool_w, pool_scale, sgu_ln_g, sgu_ln_b, sgu_w, sgu_bias_tile, tok_tile):
    seq = h.shape[0]
    n_tiles = seq // tok_tile
    n_chunks = tok_tile // CHUNK
    pool_rows = POOL_GROUP // N_DEV

    def body(h_ref, dy_ref, dxr_ref, pooled_ref, win_ref, pw_ref, ps_ref, lg_ref, lb_ref, sw_ref, sb_ref,
             dh_ref, dx_ref, dpw_ref, dps_ref, dlg_ref, dlb_ref, dsw_ref, dsb_ref,
             qext_ref, dpw_acc, dsb_acc):
        i = pl.program_id(0)
        tile = n_tiles - 1 - i

        @pl.when(i == 0)
        def _():
            qext_ref[tok_tile:tok_tile + HALO, :] = jnp.zeros((HALO, D_POOL), F32)
            dpw_acc[...] = jnp.zeros_like(dpw_acc)
            dsb_acc[...] = jnp.zeros_like(dsb_acc)
            dps_ref[...] = jnp.zeros_like(dps_ref)
            dlg_ref[...] = jnp.zeros_like(dlg_ref)
            dlb_ref[...] = jnp.zeros_like(dlb_ref)
            dsw_ref[...] = jnp.zeros_like(dsw_ref)

        for g, window in enumerate(POOL_WINDOWS):
            cols = slice(g * POOL_GROUP, (g + 1) * POOL_GROUP)
            zcols = slice(Z0 + g * POOL_GROUP, Z0 + (g + 1) * POOL_GROUP)
            z = h_ref[:, zcols]
            sz = _sigmoid(z)
            pb = pooled_ref[:, cols]
            mixed = _dot(pb, pw_ref[g])
            dyp = dy_ref[:, cols]
            dh_ref[:, zcols] = (dyp * (mixed * ps_ref[:, cols]) * (sz * (1.0 + z * (1.0 - sz)))).astype(BF16)
            dms = dyp * (z * sz)
            dps_ref[:, cols] += jnp.sum(dms * mixed, axis=0, keepdims=True)
            dmixed = (dms * ps_ref[:, cols]).astype(BF16)
            dpw_acc[g] += _dot_tn(pb, dmixed)
            dpooled = _dot_nt(dmixed, pw_ref[g])
            qext_ref[0:tok_tile, cols] = dpooled * _inv_count(tile, tok_tile, window)
            da = qext_ref[0:tok_tile, cols] - dpooled
            for k in range(1, window):
                da = da + qext_ref[k:k + tok_tile, cols]
            dh_ref[:, cols] = da.astype(BF16)
        qext_ref[tok_tile:tok_tile + HALO, :] = qext_ref[0:HALO, :]

        for hh in range(N_HEADS):
            cols = slice(hh * HEAD, (hh + 1) * HEAD)
            ucols = slice(U0 + hh * HEAD, U0 + (hh + 1) * HEAD)
            vcols = slice(V0 + hh * HEAD, V0 + (hh + 1) * HEAD)
            zcols = slice(Z0 + D_POOL + hh * HEAD, Z0 + D_POOL + (hh + 1) * HEAD)
            sw32 = _masked_sgu_w(sw_ref, hh)
            swm = sw32.astype(BF16)
            swm_t = sw32.T.astype(BF16)
            for n in range(n_chunks):
                rows = slice(n * CHUNK, (n + 1) * CHUNK)
                gu, dgu_du = _gelu_and_grad(h_ref[rows, ucols])
                gv, dgv_dv = _gelu_and_grad(h_ref[rows, vcols])
                xhat, rstd = _row_stats(gv)
                vb = (xhat * lg_ref[:, cols] + lb_ref[:, cols]).astype(BF16)
                sv = _dot(swm, vb) + sb_ref[:, cols]
                z = h_ref[rows, zcols]
                sz = _sigmoid(z)
                dys = dy_ref[rows, D_POOL + hh * HEAD:D_POOL + (hh + 1) * HEAD]
                dh_ref[rows, zcols] = (dys * (gu * sv) * (sz * (1.0 + z * (1.0 - sz)))).astype(BF16)
                dyg = dys * (z * sz)
                dh_ref[rows, ucols] = (dyg * sv * dgu_du).astype(BF16)
                dsv = dyg * gu
                dsb_acc[:, cols] += dsv
                dsvb = dsv.astype(BF16)
                dsw_ref[hh] += _dot_nt(dsvb, vb)
                dvln = _dot(swm_t, dsvb)
                dlg_ref[:, cols] += jnp.sum(dvln * xhat, axis=0, keepdims=True)
                dlb_ref[:, cols] += jnp.sum(dvln, axis=0, keepdims=True)
                dgv = _ln_bwd(dvln * lg_ref[:, cols], xhat, rstd)
                dh_ref[rows, vcols] = (dgv * dgv_dv).astype(BF16)

        dx_ref[...] = dxr_ref[...] + _dot_nt(dh_ref[...], win_ref[...])

        @pl.when(i == n_tiles - 1)
        def _():
            for g in range(len(POOL_WINDOWS)):
                for b in range(N_DEV):
                    dpw_ref[b, g] = dpw_acc[g, b * pool_rows:(b + 1) * pool_rows, :]
            row = lax.broadcasted_iota(jnp.int32, (CHUNK, CHUNK), 0)
            col = lax.broadcasted_iota(jnp.int32, (CHUNK, CHUNK), 1)
            for hh in range(N_HEADS):
                dsw_ref[hh] = jnp.where(row >= col, dsw_ref[hh], 0.0)
                total = jnp.sum(dsb_acc[:, hh * HEAD:(hh + 1) * HEAD], axis=1, keepdims=True)
                dsb_ref[hh:hh + 1, :] = jnp.broadcast_to(total, (CHUNK, CHUNK)).T[0:1, :]

    tok = lambda width: pl.BlockSpec((tok_tile, width), lambda i: (n_tiles - 1 - i, 0))
    whole = lambda shape: pl.BlockSpec(shape, lambda i: (0,) * len(shape))
    vec = jax.ShapeDtypeStruct((1, D_MODEL), F32)
    return pl.pallas_call(
        body, name="mixers_bwd",
        grid=(n_tiles,),
        in_specs=[tok(D_IN), tok(D_MIX), tok(D_MODEL), tok(D_POOL),
                  VMEM_FULL, VMEM_FULL, VMEM_FULL, VMEM_FULL, VMEM_FULL, VMEM_FULL, VMEM_FULL],
        out_specs=[tok(D_IN), tok(D_MODEL), whole((N_DEV, len(POOL_WINDOWS), pool_rows, POOL_GROUP)),
                   whole((1, D_POOL)), whole((1, D_SGU)), whole((1, D_SGU)),
                   whole((N_HEADS, CHUNK, CHUNK)), whole((N_HEADS, CHUNK))],
        out_shape=[jax.ShapeDtypeStruct((seq, D_IN), BF16), jax.ShapeDtypeStruct((seq, D_MODEL), F32),
                   jax.ShapeDtypeStruct((N_DEV, len(POOL_WINDOWS), pool_rows, POOL_GROUP), F32),
                   vec, vec, vec,
                   jax.ShapeDtypeStruct((N_HEADS, CHUNK, CHUNK), F32),
                   jax.ShapeDtypeStruct((N_HEADS, CHUNK), F32)],
        scratch_shapes=[pltpu.VMEM((tok_tile + HALO, D_POOL), F32),
                        pltpu.VMEM((len(POOL_WINDOWS), POOL_GROUP, POOL_GROUP), F32),
                        pltpu.VMEM((CHUNK, D_SGU), F32)],
        compiler_params=_params(dimension_semantics=("arbitrary",)),
    )(h, dy, dxr, pooled, w_in, pool_w, pool_scale, sgu_ln_g, sgu_ln_b, sgu_w, sgu_bias_tile)


def _weight_grad(a, b, n_col_blocks, blocks_per_step, tok_tile, name):
    seq, m = a.shape
    n = b.shape[1]
    nb = n // n_col_blocks
    n_steps = n_col_blocks // blocks_per_step

    def body(a_ref, b_ref, out_ref):
        @pl.when(pl.program_id(1) == 0)
        def _():
            out_ref[...] = jnp.zeros_like(out_ref)

        res = _dot_tn(a_ref[...].astype(BF16), b_ref[...])
        for blk in range(blocks_per_step):
            out_ref[blk] += res[:, blk * nb:(blk + 1) * nb]

    return pl.pallas_call(
        body, name=name,
        grid=(n_steps, seq // tok_tile),
        in_specs=[pl.BlockSpec((tok_tile, m), lambda j, k: (k, 0)),
                  pl.BlockSpec((tok_tile, blocks_per_step * nb), lambda j, k: (k, j))],
        out_specs=pl.BlockSpec((blocks_per_step, m, nb), lambda j, k: (j, 0, 0)),
        out_shape=jax.ShapeDtypeStruct((n_col_blocks, m, nb), F32),
        compiler_params=_params(dimension_semantics=("arbitrary", "arbitrary")),
    )(a, b)


def _place():
    return lax.axis_index("x"), lax.axis_index("y"), lax.axis_index("c")


def _other_chips(x, y):
    return [(1 - x, y), (x, 1 - y), (1 - x, 1 - y)]


def _all_gather(shards, out_shapes, views, name):
    n = len(shards)

    def body(*refs):
        ins, outs = refs[:n], refs[n:2 * n]
        send_sems, recv_sems, local_sems = refs[2 * n:]
        x, y, c = _place()
        me, sibling = (x, y, c), (x, y, 1 - c)
        chips = _other_chips(x, y)

        def copy(k, s, block, to, src=None):
            px, py, pc = block
            dst = views[k](outs[k], 4 * px + 2 * py + pc)
            return pltpu.make_async_remote_copy(
                src_ref=dst if src is None else src, dst_ref=dst,
                send_sem=send_sems.at[k, s], recv_sem=recv_sems.at[k, s],
                device_id=to, device_id_type=MESH)

        started = []
        for k in range(n):
            mine = pltpu.make_async_copy(ins[k], views[k](outs[k], 4 * x + 2 * y + c), local_sems.at[k])
            mine.start()
            started.append(mine)
        sends = []
        for k in range(n):
            sends.append(copy(k, 0, me, sibling, src=ins[k]))
            sends += [copy(k, 1 + j, me, (*chip, c), src=ins[k]) for j, chip in enumerate(chips)]
        for cp in sends:
            cp.start()
        for k in range(n):
            for j, chip in enumerate(chips):
                copy(k, 1 + j, (*chip, c), me).wait_recv()
                fwd = copy(k, 4 + j, (*chip, c), sibling)
                fwd.start()
                sends.append(fwd)
        for k in range(n):
            copy(k, 0, sibling, me).wait_recv()
            for j, chip in enumerate(chips):
                copy(k, 4 + j, (*chip, 1 - c), me).wait_recv()
        for cp in sends:
            cp.wait_send()
        for mine in started:
            mine.wait()

    return pl.pallas_call(
        body, name=name,
        in_specs=[ANY] * n, out_specs=[ANY] * n,
        out_shape=out_shapes,
        scratch_shapes=[pltpu.SemaphoreType.DMA((n, 7)), pltpu.SemaphoreType.DMA((n, 7)),
                        pltpu.SemaphoreType.DMA((n,))],
    )(*shards)


def _pair_exchange(grads, name):
    n = len(grads)

    def body(*refs):
        ins, theirs = refs[:n], refs[n:2 * n]
        send_sems, recv_sems = refs[2 * n:]
        x, y, c = _place()
        remote = []
        for k in range(n):
            for j in range(4):
                rc = pltpu.make_async_remote_copy(
                    src_ref=ins[k].at[2 * j + (1 - c)], dst_ref=theirs[k].at[j],
                    send_sem=send_sems.at[k, j], recv_sem=recv_sems.at[k, j],
                    device_id=(x, y, 1 - c), device_id_type=MESH)
                rc.start()
                remote.append(rc)
        for rc in remote:
            rc.wait_recv()
        for rc in remote:
            rc.wait_send()

    return pl.pallas_call(
        body, name=name,
        in_specs=[ANY] * n, out_specs=[ANY] * n,
        out_shape=[jax.ShapeDtypeStruct((4,) + g.shape[1:], g.dtype) for g in grads],
        scratch_shapes=[pltpu.SemaphoreType.DMA((n, 4)), pltpu.SemaphoreType.DMA((n, 4))],
    )(*grads)


def _chip_exchange(sums, name):
    n = len(sums)

    def body(*refs):
        ins, others = refs[:n], refs[n:2 * n]
        send_sems, recv_sems = refs[2 * n:]
        x, y, c = _place()
        chips = _other_chips(x, y)
        remote = []
        for k in range(n):
            for r, (px, py) in enumerate(chips):
                rc = pltpu.make_async_remote_copy(
                    src_ref=ins[k].at[2 * px + py], dst_ref=others[k].at[r],
                    send_sem=send_sems.at[k, r], recv_sem=recv_sems.at[k, r],
                    device_id=(px, py, c), device_id_type=MESH)
                rc.start()
                remote.append(rc)
        for rc in remote:
            rc.wait_recv()
        for rc in remote:
            rc.wait_send()

    return pl.pallas_call(
        body, name=name,
        in_specs=[ANY] * n, out_specs=[ANY] * n,
        out_shape=[jax.ShapeDtypeStruct((3,) + s.shape[1:], s.dtype) for s in sums],
        scratch_shapes=[pltpu.SemaphoreType.DMA((n, 3)), pltpu.SemaphoreType.DMA((n, 3))],
    )(*sums)


def _row_tile(rows, cols):
    tile = rows
    while tile * cols > 256 * 1024 and tile % 16 == 0:
        tile //= 2
    return tile


def _pair_sum(core, grads, theirs, name):
    _, rows, cols = theirs.shape
    rt = _row_tile(rows, cols)

    def body(core_ref, a_ref, b_ref, o_ref, ob_ref):
        total = a_ref[...] + b_ref[...]
        o_ref[...] = total
        ob_ref[...] = total.astype(BF16)

    spec = pl.BlockSpec((None, rt, cols), lambda j, i, core_ref: (j, i, 0))
    mine = pl.BlockSpec((None, None, rt, cols), lambda j, i, core_ref: (j, core_ref[0], i, 0))
    return pl.pallas_call(
        body, name=name,
        grid_spec=pltpu.PrefetchScalarGridSpec(
            num_scalar_prefetch=1, grid=(4, rows // rt), in_specs=[mine, spec], out_specs=[spec, spec]),
        out_shape=[jax.ShapeDtypeStruct(theirs.shape, F32), jax.ShapeDtypeStruct(theirs.shape, BF16)],
        compiler_params=_params(dimension_semantics=("arbitrary", "arbitrary")),
    )(core, grads.reshape(4, 2, rows, cols), theirs)


def _adamw(w, g, m, v):
    m = ADAM_B1 * m + (1.0 - ADAM_B1) * g
    v = ADAM_B2 * v + (1.0 - ADAM_B2) * (g * g)
    m_hat = m / (1.0 - ADAM_B1 ** ADAM_STEP)
    v_hat = v / (1.0 - ADAM_B2 ** ADAM_STEP)
    delta = -ADAM_LR * (m_hat / (jnp.sqrt(v_hat) + ADAM_EPS) + ADAM_WD * w)
    return delta, m, v


def _sum_adamw(chip, sums, others, w, m, v, name):
    _, rows, cols = sums.shape
    rt = _row_tile(rows, cols)

    def body(chip_ref, own_ref, oth_ref, w_ref, m_ref, v_ref, g_ref, d_ref, nm_ref, nv_ref):
        g = ((own_ref[...] + oth_ref[0].astype(F32)) + oth_ref[1].astype(F32)) + oth_ref[2].astype(F32)
        g_ref[...] = g
        d_ref[...], nm_ref[...], nv_ref[...] = _adamw(w_ref[...], g, m_ref[...], v_ref[...])

    spec = pl.BlockSpec((rt, cols), lambda i, chip_ref: (i, 0))
    own = pl.BlockSpec((None, rt, cols), lambda i, chip_ref: (chip_ref[0], i, 0))
    shape = jax.ShapeDtypeStruct((rows, cols), F32)
    return pl.pallas_call(
        body, name=name,
        grid_spec=pltpu.PrefetchScalarGridSpec(
            num_scalar_prefetch=1, grid=(rows // rt,),
            in_specs=[own, pl.BlockSpec((3, rt, cols), lambda i, chip_ref: (0, i, 0)), spec, spec, spec],
            out_specs=[spec] * 4),
        out_shape=[shape] * 4,
        compiler_params=_params(dimension_semantics=("arbitrary",)),
    )(chip, sums, others, w, m, v)


def _small_sum_adamw(gathered, ws, ms, vs):
    n = len(ws)

    def body(*refs):
        g8 = refs[:n + 1]
        w, m, v = refs[n + 1:2 * n + 1], refs[2 * n + 1:3 * n + 1], refs[3 * n + 1:4 * n + 1]
        outs = refs[4 * n + 1:]
        g_out, d_out, m_out, v_out = outs[:n + 1], outs[n + 1:2 * n + 1], outs[2 * n + 1:3 * n + 1], outs[3 * n + 1:]
        for k in range(n + 1):
            g = g8[k][0]
            for b in range(1, N_DEV):
                g = g + g8[k][b]
            g_out[k][...] = g
            if k < n:
                d_out[k][...], m_out[k][...], v_out[k][...] = _adamw(w[k][...], g, m[k][...], v[k][...])

    shapes = [jax.ShapeDtypeStruct(w.shape, F32) for w in ws]
    loss_shape = jax.ShapeDtypeStruct(gathered[-1].shape[1:], F32)
    outs = pl.pallas_call(
        body, name="small_sum_adamw",
        in_specs=[VMEM_FULL] * (4 * n + 1), out_specs=[VMEM_FULL] * (4 * n + 1),
        out_shape=shapes + [loss_shape] + shapes * 3,
        compiler_params=_params(),
    )(*gathered, *ws, *ms, *vs)
    return outs[:n + 1], outs[n + 1:2 * n + 1], outs[2 * n + 1:3 * n + 1], outs[3 * n + 1:]


SMALL_NAMES = ("pool_scale", "sgu_ln_g", "sgu_ln_b", "sgu_w", "sgu_b", "ln_g", "ln_b", "ple_gate_b")


def _local_step(x, p, target, w_in, pool_w, w_out, ple_w, gate_w,
                pool_scale, sgu_ln_g, sgu_ln_b, sgu_w, sgu_b, ln_g, ln_b, gate_b,
                tok_tile=256, grad_tile=1024):
    bias_tile = jnp.repeat(sgu_b.T, HEAD, axis=1)
    h, y, pooled, xb = _forward_mixers(x, w_in, pool_w, pool_scale, sgu_ln_g, sgu_ln_b, sgu_w, bias_tile, tok_tile)
    dy, dxr, xn, dgp, dpe, dr, loss, d_ln_g, d_ln_b, d_gate_b = _head_fwd_bwd(
        x, y, p, target, w_out, gate_w, ple_w, ln_g, ln_b, gate_b, min(2 * tok_tile, x.shape[0]), tok_tile)
    dh, dx, d_pool_w, d_pool_scale, d_sgu_ln_g, d_sgu_ln_b, d_sgu_w, d_sgu_b = _mixers_bwd(
        h, dy, dxr, pooled, w_in, pool_w, pool_scale, sgu_ln_g, sgu_ln_b, sgu_w, bias_tile, tok_tile)
    d_w_in = _weight_grad(xb, dh, N_DEV, 2, grad_tile, "grad_w_in")
    d_w_out = _weight_grad(y, dr, 1, 1, grad_tile, "grad_w_out")
    d_ple_w = _weight_grad(p, dpe, N_DEV, N_DEV, grad_tile, "grad_ple_w")
    d_gate_w = _weight_grad(xn, dgp, 1, 1, grad_tile, "grad_gate_w")
    small = (d_pool_scale, d_sgu_ln_g, d_sgu_ln_b, d_sgu_w, d_sgu_b, d_ln_g, d_ln_b, d_gate_b)
    return loss, dx, (d_w_in, d_pool_w, d_w_out, d_ple_w, d_gate_w), small


def kernel(x, p, w_in, pool_w, pool_scale, sgu_ln_g, sgu_ln_b, sgu_w, sgu_b, w_out, ln_g, ln_b, ple_w, ple_gate_w, ple_gate_b, loss_target, m_w_in, m_pool_w, m_pool_scale, m_sgu_ln_g, m_sgu_ln_b, m_sgu_w, m_sgu_b, m_w_out, m_ln_g, m_ln_b, m_ple_w, m_ple_gate_w, m_ple_gate_b, v_w_in, v_pool_w, v_pool_scale, v_sgu_ln_g, v_sgu_ln_b, v_sgu_w, v_sgu_b, v_w_out, v_ln_g, v_ln_b, v_ple_w, v_ple_gate_w, v_ple_gate_b):
    seq = x.shape[1]

    shards = [w_in[0].astype(BF16), pool_w[0].astype(BF16), w_out[0].astype(BF16),
              ple_w[0].astype(BF16), ple_gate_w[0].astype(BF16)]
    full_shapes = [jax.ShapeDtypeStruct((D_MODEL, D_IN), BF16),
                   jax.ShapeDtypeStruct((len(POOL_WINDOWS), POOL_GROUP, POOL_GROUP), BF16),
                   jax.ShapeDtypeStruct((D_MIX, D_MODEL), BF16),
                   jax.ShapeDtypeStruct((D_PLE, D_MODEL), BF16),
                   jax.ShapeDtypeStruct((D_MODEL, D_MODEL), BF16)]
    cols = lambda width: (lambda ref, b: ref.at[:, pl.ds(pl.multiple_of(b * width, 128), width)])
    rows = lambda height: (lambda ref, b: ref.at[pl.ds(pl.multiple_of(b * height, 16), height), :])
    pool_rows = POOL_GROUP // N_DEV
    views = [cols(D_IN // N_DEV),
             lambda ref, b: ref.at[:, pl.ds(pl.multiple_of(b * pool_rows, 16), pool_rows), :],
             rows(D_MIX // N_DEV), cols(D_MODEL // N_DEV), rows(D_MODEL // N_DEV)]
    w_in_f, pool_w_f, w_out_f, ple_w_f, gate_w_f = _all_gather(shards, full_shapes, views, "gather_weights")

    loss, dx, big, small = _local_step(
        x[0], p[0, 0], loss_target[0], w_in_f, pool_w_f, w_out_f, ple_w_f, gate_w_f,
        pool_scale, sgu_ln_g, sgu_ln_b, sgu_w[0], sgu_b[0], ln_g, ln_b, ple_gate_b)
    d_w_in, d_pool_w, d_w_out, d_ple_w, d_gate_w = big

    blocks = [d_w_in,
              d_pool_w.reshape(N_DEV, len(POOL_WINDOWS) * pool_rows, POOL_GROUP),
              d_w_out.reshape(N_DEV, D_MIX // N_DEV, D_MODEL),
              d_ple_w,
              d_gate_w.reshape(N_DEV, D_MODEL // N_DEV, D_MODEL)]
    names = ("w_in", "pool_w", "w_out", "ple_w", "gate_w")
    core = lax.axis_index("c").astype(jnp.int32).reshape(1)
    chip = (2 * lax.axis_index("x") + lax.axis_index("y")).astype(jnp.int32).reshape(1)
    theirs = _pair_exchange(blocks, "pair_exchange")
    sums = [_pair_sum(core, g, t, "pair_sum_" + nm) for g, t, nm in zip(blocks, theirs, names)]
    others = _chip_exchange([s_bf for _, s_bf in sums], "chip_exchange")

    big_w = (w_in, pool_w, w_out, ple_w, ple_gate_w)
    big_m = (m_w_in, m_pool_w, m_w_out, m_ple_w, m_ple_gate_w)
    big_v = (v_w_in, v_pool_w, v_w_out, v_ple_w, v_ple_gate_w)
    big_out = {}
    for nm, (s_f32, _), oth, w, m, v in zip(names, sums, others, big_w, big_m, big_v):
        two_d = s_f32.shape[1:]
        res = _sum_adamw(chip, s_f32, oth, w.reshape(two_d), m.reshape(two_d), v.reshape(two_d), "adamw_" + nm)
        big_out[nm] = [r.reshape(w.shape) for r in res]

    small_w = (pool_scale, sgu_ln_g, sgu_ln_b, sgu_w, sgu_b, ln_g, ln_b, ple_gate_b)
    small_m = (m_pool_scale, m_sgu_ln_g, m_sgu_ln_b, m_sgu_w, m_sgu_b, m_ln_g, m_ln_b, m_ple_gate_b)
    small_v = (v_pool_scale, v_sgu_ln_g, v_sgu_ln_b, v_sgu_w, v_sgu_b, v_ln_g, v_ln_b, v_ple_gate_b)
    parts = small + (loss,)
    gathered = _all_gather(list(parts), [jax.ShapeDtypeStruct((N_DEV,) + a.shape, F32) for a in parts],
                           [lambda ref, b: ref.at[b]] * len(parts), "gather_small_grads")
    natural = [a.reshape(g.shape) for a, g in zip(small_w, small)], \
              [a.reshape(g.shape) for a, g in zip(small_m, small)], \
              [a.reshape(g.shape) for a, g in zip(small_v, small)]
    res = _small_sum_adamw(gathered, *natural)
    g_s, d_s, m_s, v_s = [[r.reshape(w.shape) for r, w in zip(kind, small_w)] for kind in res]
    total_loss = res[0][-1][0, 0]

    small_names = SMALL_NAMES
    order = ("w_in", "pool_w", "pool_scale", "sgu_ln_g", "sgu_ln_b", "sgu_w", "sgu_b", "w_out", "ln_g", "ln_b",
             "ple_w", "ple_gate_w", "ple_gate_b")
    outs = [total_loss, dx.reshape(1, seq, D_MODEL)]
    for kind in range(4):
        for nm in order:
            key = "gate_w" if nm == "ple_gate_w" else nm
            if key in big_out:
                outs.append(big_out[key][kind])
            else:
                outs.append((g_s, d_s, m_s, v_s)[kind][small_names.index(nm)])
    return tuple(outs)
```

```python
from typing import Callable, NamedTuple

import jax
import jax.numpy as jnp
from jax import lax
from jax.experimental import pallas as pl
from jax.experimental.pallas import tpu as pltpu

F32 = jnp.float32
BF16 = jnp.bfloat16

N_DEV = 8
D_MODEL = 1024
D_POOL = 1024
D_SGU = 1024
D_MIX = 2048
D_IN = 5120
D_PLE = 256
POOL_WINDOWS = (2, 4, 8, 16)
POOL_GROUP = 256
N_HEADS = 4
HEAD = 256
CHUNK = 128
HALO = 16
ALPHA = 2.0 ** 0.25
LN_EPS = 1e-5
ADAM_LR, ADAM_B1, ADAM_B2, ADAM_EPS, ADAM_WD, ADAM_STEP = 0.001, 0.9, 0.999, 1e-08, 0.01, 10

U0, V0, Z0 = D_POOL, D_POOL + D_SGU, D_POOL + 2 * D_SGU
VMEM_LIMIT = 56 * 1024 * 1024
MESH = pl.DeviceIdType.MESH
ANY = pl.BlockSpec(memory_space=pl.ANY)
VMEM_FULL = pl.BlockSpec(memory_space=pltpu.VMEM)

_GELU_C0 = 0.7978845608028654
_GELU_C1 = 0.044715


def _gelu_and_grad(x):
    x2 = x * x
    t = jnp.tanh(_GELU_C0 * (x + _GELU_C1 * (x * x2)))
    cdf = 0.5 * (1.0 + t)
    dg = cdf + x * (0.5 * (1.0 - t * t)) * (_GELU_C0 * (1.0 + (3.0 * _GELU_C1) * x2))
    return x * cdf, dg


def _gelu(x):
    t = jnp.tanh(_GELU_C0 * (x + _GELU_C1 * (x * x * x)))
    return x * (0.5 * (1.0 + t))


def _sigmoid(x):
    return 1.0 / (1.0 + jnp.exp(-x))


def _dot(a, b):
    return jnp.dot(a, b, preferred_element_type=F32)


def _dot_nt(a, b):
    return lax.dot_general(a, b, (((1,), (1,)), ((), ())), preferred_element_type=F32)


def _dot_tn(a, b):
    return lax.dot_general(a, b, (((0,), (0,)), ((), ())), preferred_element_type=F32)


def _row_stats(x):
    mu = jnp.mean(x, axis=-1, keepdims=True)
    xc = x - mu
    var = jnp.mean(xc * xc, axis=-1, keepdims=True)
    rstd = lax.rsqrt(var + LN_EPS)
    return xc * rstd, rstd


def _ln_bwd(dxhat, xhat, rstd):
    m1 = jnp.mean(dxhat, axis=-1, keepdims=True)
    m2 = jnp.mean(dxhat * xhat, axis=-1, keepdims=True)
    return rstd * (dxhat - m1 - xhat * m2)


def _masked_sgu_w(sw_ref, hh):
    row = lax.broadcasted_iota(jnp.int32, (CHUNK, CHUNK), 0)
    col = lax.broadcasted_iota(jnp.int32, (CHUNK, CHUNK), 1)
    return jnp.where(row >= col, sw_ref[hh], 0.0)


def _inv_count(tile_index, tok_tile, window):
    tok = tile_index * tok_tile + lax.broadcasted_iota(jnp.int32, (tok_tile, 1), 0)
    return 1.0 / jnp.minimum(tok + 1, window).astype(F32)


def _params(**kw):
    return pltpu.CompilerParams(vmem_limit_bytes=VMEM_LIMIT, **kw)


def _forward_mixers(x, w_in, pool_w, pool_scale, sgu_ln_g, sgu_ln_b, sgu_w, sgu_bias_tile, tok_tile, sides=()):
    seq = x.shape[0]
    n_tiles = seq // tok_tile
    n_chunks = tok_tile // CHUNK
    side_refs = _SideRefs(sides, 8, 4, 1)

    def body(*refs):
        (x_ref, win_ref, pw_ref, ps_ref, lg_ref, lb_ref, sw_ref, sb_ref,
         h_ref, y_ref, pooled_ref, xb_ref, aext_ref) = side_refs.split(refs)
        i = pl.program_id(0)
        side_refs.emit(i == 0, i == n_tiles // 2, False)

        @pl.when(i == 0)
        def _():
            aext_ref[0:HALO, :] = jnp.zeros((HALO, D_POOL), F32)

        xb = x_ref[...].astype(BF16)
        xb_ref[...] = xb
        for s in range(D_IN // 1024):
            cs = slice(s * 1024, (s + 1) * 1024)
            h_ref[:, cs] = _dot(xb, win_ref[:, cs])

        aext_ref[HALO:HALO + tok_tile, :] = h_ref[:, 0:D_POOL]
        for g, window in enumerate(POOL_WINDOWS):
            cols = slice(g * POOL_GROUP, (g + 1) * POOL_GROUP)
            win = aext_ref[HALO:HALO + tok_tile, cols]
            for k in range(1, window):
                win = win + aext_ref[HALO - k:HALO - k + tok_tile, cols]
            pooled = (win * _inv_count(i, tok_tile, window)).astype(F32) - h_ref[:, cols]
            pb = pooled.astype(BF16)
            pooled_ref[:, cols] = pb
            mixed = _dot(pb, pw_ref[g])
            z = h_ref[:, Z0 + g * POOL_GROUP:Z0 + (g + 1) * POOL_GROUP]
            y_ref[:, cols] = (mixed * ps_ref[:, cols] * (z * _sigmoid(z))).astype(BF16)
        aext_ref[0:HALO, :] = aext_ref[tok_tile:tok_tile + HALO, :]

        for hh in range(N_HEADS):
            cols = slice(hh * HEAD, (hh + 1) * HEAD)
            swm = _masked_sgu_w(sw_ref, hh).astype(BF16)
            for n in range(n_chunks):
                rows = slice(n * CHUNK, (n + 1) * CHUNK)
                gu = _gelu(h_ref[rows, U0 + hh * HEAD:U0 + (hh + 1) * HEAD])
                gv = _gelu(h_ref[rows, V0 + hh * HEAD:V0 + (hh + 1) * HEAD])
                xhat, _ = _row_stats(gv)
                vln = xhat * lg_ref[:, cols] + lb_ref[:, cols]
                sv = _dot(swm, vln.astype(BF16)) + sb_ref[:, cols]
                z = h_ref[rows, Z0 + D_POOL + hh * HEAD:Z0 + D_POOL + (hh + 1) * HEAD]
                y_ref[rows, D_POOL + hh * HEAD:D_POOL + (hh + 1) * HEAD] = (
                    gu * sv * (z * _sigmoid(z))).astype(BF16)

        side_refs.emit(False, False, i == n_tiles - 1)

    tok = lambda width: pl.BlockSpec((tok_tile, width), lambda i: (i, 0))
    outs = pl.pallas_call(
        body, name="forward_mixers",
        grid=(n_tiles,),
        in_specs=[tok(D_MODEL), VMEM_FULL, VMEM_FULL, VMEM_FULL, VMEM_FULL, VMEM_FULL, VMEM_FULL, VMEM_FULL]
        + side_refs.in_specs,
        out_specs=[tok(D_IN), tok(D_MIX), tok(D_POOL), tok(D_MODEL)] + side_refs.out_specs,
        out_shape=[jax.ShapeDtypeStruct((seq, D_IN), F32), jax.ShapeDtypeStruct((seq, D_MIX), BF16),
                   jax.ShapeDtypeStruct((seq, D_POOL), BF16), jax.ShapeDtypeStruct((seq, D_MODEL), BF16)]
        + side_refs.out_shapes,
        scratch_shapes=[pltpu.VMEM((HALO + tok_tile, D_POOL), F32)] + side_refs.scratch,
        compiler_params=_params(dimension_semantics=("arbitrary",)),
    )(x, w_in, pool_w, pool_scale, sgu_ln_g, sgu_ln_b, sgu_w, sgu_bias_tile, *side_refs.inputs)
    return outs[:4], outs[4:]


def _head_fwd_bwd(x, y, p, target, w_out, gate_w, ple_w, ln_g, ln_b, gate_b, tok_tile, sub_tile):
    seq = x.shape[0]

    def body(x_ref, y_ref, p_ref, t_ref, wout_ref, gw_ref, plw_ref, lng_ref, lnb_ref, gb_ref,
             dy_ref, dxr_ref, xn_ref, dgp_ref, dpe_ref, dr_ref, loss_ref, dlng_ref, dlnb_ref, dgb_ref):
        i = pl.program_id(0)

        @pl.when(i == 0)
        def _():
            loss_ref[...] = jnp.zeros_like(loss_ref)
            dlng_ref[...] = jnp.zeros_like(dlng_ref)
            dlnb_ref[...] = jnp.zeros_like(dlnb_ref)
            dgb_ref[...] = jnp.zeros_like(dgb_ref)

        subs = [slice(s * sub_tile, (s + 1) * sub_tile) for s in range(tok_tile // sub_tile)]
        stats, xns, douts = [], [], []
        for rows in subs:
            r = ALPHA * x_ref[rows, :] + _dot(y_ref[rows, :], wout_ref[...])
            xhat, rstd = _row_stats(r)
            xn = xhat * lng_ref[...] + lnb_ref[...]
            xn_ref[rows, :] = xn.astype(BF16)
            stats.append((xhat, rstd))
            xns.append(xn)
        loss = jnp.zeros((1, 1), F32)
        dgb = jnp.zeros((1, D_MODEL), F32)
        for rows, xn in zip(subs, xns):
            gate = _sigmoid(_dot(xn_ref[rows, :], gw_ref[...]) + gb_ref[...])
            pe = _dot(p_ref[rows, :].astype(BF16), plw_ref[...])
            err = xn + gate * pe - t_ref[rows, :]
            loss = loss + jnp.sum(err * err, keepdims=True)
            dout = err * (1.0 / D_MODEL)
            dpe_ref[rows, :] = (dout * gate).astype(BF16)
            dgpre = dout * pe * gate * (1.0 - gate)
            dgb = dgb + jnp.sum(dgpre, axis=0, keepdims=True)
            dgp_ref[rows, :] = dgpre.astype(BF16)
            douts.append(dout)
        loss_ref[...] += (0.5 / D_MODEL) * loss
        dgb_ref[...] += dgb
        dlng = jnp.zeros((1, D_MODEL), F32)
        dlnb = jnp.zeros((1, D_MODEL), F32)
        for rows, (xhat, rstd), dout in zip(subs, stats, douts):
            dxn = dout + _dot_nt(dgp_ref[rows, :], gw_ref[...])
            dlng = dlng + jnp.sum(dxn * xhat, axis=0, keepdims=True)
            dlnb = dlnb + jnp.sum(dxn, axis=0, keepdims=True)
            dr = _ln_bwd(dxn * lng_ref[...], xhat, rstd)
            dxr_ref[rows, :] = ALPHA * dr
            dr_ref[rows, :] = dr.astype(BF16)
        dlng_ref[...] += dlng
        dlnb_ref[...] += dlnb
        for rows in subs:
            dy_ref[rows, :] = _dot_nt(dr_ref[rows, :], wout_ref[...])

    tok = lambda width: pl.BlockSpec((tok_tile, width), lambda i: (i, 0))
    acc = lambda width: pl.BlockSpec((1, width), lambda i: (0, 0))
    vec = jax.ShapeDtypeStruct((1, D_MODEL), F32)
    return pl.pallas_call(
        body, name="head_fwd_bwd",
        grid=(seq // tok_tile,),
        in_specs=[tok(D_MODEL), tok(D_MIX), tok(D_PLE), tok(D_MODEL),
                  VMEM_FULL, VMEM_FULL, VMEM_FULL, VMEM_FULL, VMEM_FULL, VMEM_FULL],
        out_specs=[tok(D_MIX), tok(D_MODEL), tok(D_MODEL), tok(D_MODEL), tok(D_MODEL), tok(D_MODEL),
                   acc(128), acc(D_MODEL), acc(D_MODEL), acc(D_MODEL)],
        out_shape=[jax.ShapeDtypeStruct((seq, D_MIX), F32), jax.ShapeDtypeStruct((seq, D_MODEL), F32),
                   jax.ShapeDtypeStruct((seq, D_MODEL), BF16), jax.ShapeDtypeStruct((seq, D_MODEL), BF16),
                   jax.ShapeDtypeStruct((seq, D_MODEL), BF16), jax.ShapeDtypeStruct((seq, D_MODEL), BF16),
                   jax.ShapeDtypeStruct((1, 128), F32), vec, vec, vec],
        compiler_params=_params(dimension_semantics=("arbitrary",)),
    )(x, y, p, target, w_out, gate_w, ple_w, ln_g, ln_b, gate_b)


def _mixers_bwd(h, dy, dxr, pooled, w_in, pool_w, pool_scale, sgu_ln_g, sgu_ln_b, sgu_w, sgu_bias_tile, tok_tile):
    seq = h.shape[0]
    n_tiles = seq // tok_tile
    n_chunks = tok_tile // CHUNK
    pool_rows = POOL_GROUP // N_DEV

    def body(h_ref, dy_ref, dxr_ref, pooled_ref, win_ref, pw_ref, ps_ref, lg_ref, lb_ref, sw_ref, sb_ref,
             dh_ref, dx_ref, dpw_ref, dps_ref, dlg_ref, dlb_ref, dsw_ref, dsb_ref,
             qext_ref, dpw_acc, dsb_acc):
        i = pl.program_id(0)
        tile = n_tiles - 1 - i

        @pl.when(i == 0)
        def _():
            qext_ref[tok_tile:tok_tile + HALO, :] = jnp.zeros((HALO, D_POOL), F32)
            dpw_acc[...] = jnp.zeros_like(dpw_acc)
            dsb_acc[...] = jnp.zeros_like(dsb_acc)
            dps_ref[...] = jnp.zeros_like(dps_ref)
            dlg_ref[...] = jnp.zeros_like(dlg_ref)
            dlb_ref[...] = jnp.zeros_like(dlb_ref)
            dsw_ref[...] = jnp.zeros_like(dsw_ref)

        for g, window in enumerate(POOL_WINDOWS):
            cols = slice(g * POOL_GROUP, (g + 1) * POOL_GROUP)
            zcols = slice(Z0 + g * POOL_GROUP, Z0 + (g + 1) * POOL_GROUP)
            z = h_ref[:, zcols]
            sz = _sigmoid(z)
            pb = pooled_ref[:, cols]
            mixed = _dot(pb, pw_ref[g])
            dyp = dy_ref[:, cols]
            dh_ref[:, zcols] = (dyp * (mixed * ps_ref[:, cols]) * (sz * (1.0 + z * (1.0 - sz)))).astype(BF16)
            dms = dyp * (z * sz)
            dps_ref[:, cols] += jnp.sum(dms * mixed, axis=0, keepdims=True)
            dmixed = (dms * ps_ref[:, cols]).astype(BF16)
            dpw_acc[g] += _dot_tn(pb, dmixed)
            dpooled = _dot_nt(dmixed, pw_ref[g])
            qext_ref[0:tok_tile, cols] = dpooled * _inv_count(tile, tok_tile, window)
            da = qext_ref[0:tok_tile, cols] - dpooled
            for k in range(1, window):
                da = da + qext_ref[k:k + tok_tile, cols]
            dh_ref[:, cols] = da.astype(BF16)
        qext_ref[tok_tile:tok_tile + HALO, :] = qext_ref[0:HALO, :]

        for hh in range(N_HEADS):
            cols = slice(hh * HEAD, (hh + 1) * HEAD)
            ucols = slice(U0 + hh * HEAD, U0 + (hh + 1) * HEAD)
            vcols = slice(V0 + hh * HEAD, V0 + (hh + 1) * HEAD)
            zcols = slice(Z0 + D_POOL + hh * HEAD, Z0 + D_POOL + (hh + 1) * HEAD)
            sw32 = _masked_sgu_w(sw_ref, hh)
            swm = sw32.astype(BF16)
            swm_t = sw32.T.astype(BF16)
            for n in range(n_chunks):
                rows = slice(n * CHUNK, (n + 1) * CHUNK)
                gu, dgu_du = _gelu_and_grad(h_ref[rows, ucols])
                gv, dgv_dv = _gelu_and_grad(h_ref[rows, vcols])
                xhat, rstd = _row_stats(gv)
                vb = (xhat * lg_ref[:, cols] + lb_ref[:, cols]).astype(BF16)
                sv = _dot(swm, vb) + sb_ref[:, cols]
                z = h_ref[rows, zcols]
                sz = _sigmoid(z)
                dys = dy_ref[rows, D_POOL + hh * HEAD:D_POOL + (hh + 1) * HEAD]
                dh_ref[rows, zcols] = (dys * (gu * sv) * (sz * (1.0 + z * (1.0 - sz)))).astype(BF16)
                dyg = dys * (z * sz)
                dh_ref[rows, ucols] = (dyg * sv * dgu_du).astype(BF16)
                dsv = dyg * gu
                dsb_acc[:, cols] += dsv
                dsvb = dsv.astype(BF16)
                dsw_ref[hh] += _dot_nt(dsvb, vb)
                dvln = _dot(swm_t, dsvb)
                dlg_ref[:, cols] += jnp.sum(dvln * xhat, axis=0, keepdims=True)
                dlb_ref[:, cols] += jnp.sum(dvln, axis=0, keepdims=True)
                dgv = _ln_bwd(dvln * lg_ref[:, cols], xhat, rstd)
                dh_ref[rows, vcols] = (dgv * dgv_dv).astype(BF16)

        dx_ref[...] = dxr_ref[...] + _dot_nt(dh_ref[...], win_ref[...])

        @pl.when(i == n_tiles - 1)
        def _():
            for g in range(len(POOL_WINDOWS)):
                for b in range(N_DEV):
                    dpw_ref[b, g] = dpw_acc[g, b * pool_rows:(b + 1) * pool_rows, :]
            row = lax.broadcasted_iota(jnp.int32, (CHUNK, CHUNK), 0)
            col = lax.broadcasted_iota(jnp.int32, (CHUNK, CHUNK), 1)
            for hh in range(N_HEADS):
                dsw_ref[hh] = jnp.where(row >= col, dsw_ref[hh], 0.0)
                total = jnp.sum(dsb_acc[:, hh * HEAD:(hh + 1) * HEAD], axis=1, keepdims=True)
                dsb_ref[hh:hh + 1, :] = jnp.broadcast_to(total, (CHUNK, CHUNK)).T[0:1, :]

    tok = lambda width: pl.BlockSpec((tok_tile, width), lambda i: (n_tiles - 1 - i, 0))
    whole = lambda shape: pl.BlockSpec(shape, lambda i: (0,) * len(shape))
    vec = jax.ShapeDtypeStruct((1, D_MODEL), F32)
    return pl.pallas_call(
        body, name="mixers_bwd",
        grid=(n_tiles,),
        in_specs=[tok(D_IN), tok(D_MIX), tok(D_MODEL), tok(D_POOL),
                  VMEM_FULL, VMEM_FULL, VMEM_FULL, VMEM_FULL, VMEM_FULL, VMEM_FULL, VMEM_FULL],
        out_specs=[tok(D_IN), tok(D_MODEL), whole((N_DEV, len(POOL_WINDOWS), pool_rows, POOL_GROUP)),
                   whole((1, D_POOL)), whole((1, D_SGU)), whole((1, D_SGU)),
                   whole((N_HEADS, CHUNK, CHUNK)), whole((N_HEADS, CHUNK))],
        out_shape=[jax.ShapeDtypeStruct((seq, D_IN), BF16), jax.ShapeDtypeStruct((seq, D_MODEL), F32),
                   jax.ShapeDtypeStruct((N_DEV, len(POOL_WINDOWS), pool_rows, POOL_GROUP), F32),
                   vec, vec, vec,
                   jax.ShapeDtypeStruct((N_HEADS, CHUNK, CHUNK), F32),
                   jax.ShapeDtypeStruct((N_HEADS, CHUNK), F32)],
        scratch_shapes=[pltpu.VMEM((tok_tile + HALO, D_POOL), F32),
                        pltpu.VMEM((len(POOL_WINDOWS), POOL_GROUP, POOL_GROUP), F32),
                        pltpu.VMEM((CHUNK, D_SGU), F32)],
        compiler_params=_params(dimension_semantics=("arbitrary",)),
    )(h, dy, dxr, pooled, w_in, pool_w, pool_scale, sgu_ln_g, sgu_ln_b, sgu_w, sgu_bias_tile)


def _weight_grad(a, b, n_col_blocks, blocks_per_step, tok_tile, name, sides=()):
    seq, m = a.shape
    n = b.shape[1]
    nb = n // n_col_blocks
    n_steps = n_col_blocks // blocks_per_step
    n_k = seq // tok_tile
    side_refs = _SideRefs(sides, 2, 1, 0)

    def body(*refs):
        a_ref, b_ref, out_ref = side_refs.split(refs)
        step = pl.program_id(0) * n_k + pl.program_id(1)
        side_refs.emit(step == 0, step == (n_steps * n_k) // 2, False)

        @pl.when(pl.program_id(1) == 0)
        def _():
            out_ref[...] = jnp.zeros_like(out_ref)

        res = _dot_tn(a_ref[...].astype(BF16), b_ref[...])
        for blk in range(blocks_per_step):
            out_ref[blk] += res[:, blk * nb:(blk + 1) * nb]

        side_refs.emit(False, False, step == n_steps * n_k - 1)

    outs = pl.pallas_call(
        body, name=name,
        grid=(n_steps, n_k),
        in_specs=[pl.BlockSpec((tok_tile, m), lambda j, k: (k, 0)),
                  pl.BlockSpec((tok_tile, blocks_per_step * nb), lambda j, k: (k, j))] + side_refs.in_specs,
        out_specs=[pl.BlockSpec((blocks_per_step, m, nb), lambda j, k: (j, 0, 0))] + side_refs.out_specs,
        out_shape=[jax.ShapeDtypeStruct((n_col_blocks, m, nb), F32)] + side_refs.out_shapes,
        scratch_shapes=side_refs.scratch,
        compiler_params=_params(dimension_semantics=("arbitrary", "arbitrary")),
    )(a, b, *side_refs.inputs)
    return outs[0], outs[1:]


class _Side(NamedTuple):
    inputs: list
    out_shapes: list
    sem_shapes: list
    emit: Callable


def _when(cond):
    if cond is True:
        return lambda f: f()
    if cond is False:
        return lambda f: None
    return pl.when(cond)


def _place():
    return lax.axis_index("x"), lax.axis_index("y"), lax.axis_index("c")


def _other_chips(x, y):
    return [(1 - x, y), (x, 1 - y), (1 - x, 1 - y)]


def _gather_side(shards, out_shapes, views):
    n = len(shards)

    def emit(ins, outs, sems, first, mid, last):
        send_sems, recv_sems, local_sems = sems
        x, y, c = _place()
        me, sibling = (x, y, c), (x, y, 1 - c)
        chips = _other_chips(x, y)

        def copy(k, s, block, to, src=None):
            px, py, pc = block
            dst = views[k](outs[k], 4 * px + 2 * py + pc)
            return pltpu.make_async_remote_copy(
                src_ref=dst if src is None else src, dst_ref=dst,
                send_sem=send_sems.at[k, s], recv_sem=recv_sems.at[k, s],
                device_id=to, device_id_type=MESH)

        def own(k):
            return pltpu.make_async_copy(ins[k], views[k](outs[k], 4 * x + 2 * y + c), local_sems.at[k])

        def sends(k):
            return [copy(k, 0, me, sibling, src=ins[k])] + [
                copy(k, 1 + j, me, (*chip, c), src=ins[k]) for j, chip in enumerate(chips)]

        def passed_on(k, j):
            return copy(k, 4 + j, (*chips[j], c), sibling)

        @_when(first)
        def _():
            for k in range(n):
                own(k).start()
                for cp in sends(k):
                    cp.start()

        @_when(mid)
        def _():
            for k in range(n):
                for j, chip in enumerate(chips):
                    copy(k, 1 + j, (*chip, c), me).wait_recv()
                    passed_on(k, j).start()

        @_when(last)
        def _():
            for k in range(n):
                copy(k, 0, sibling, me).wait_recv()
                for j, chip in enumerate(chips):
                    copy(k, 4 + j, (*chip, 1 - c), me).wait_recv()
            for k in range(n):
                for cp in sends(k) + [passed_on(k, j) for j in range(len(chips))]:
                    cp.wait_send()
                own(k).wait()

    sems = [pltpu.SemaphoreType.DMA((n, 7)), pltpu.SemaphoreType.DMA((n, 7)), pltpu.SemaphoreType.DMA((n,))]
    return _Side(list(shards), list(out_shapes), sems, emit)


def _pair_side(grads):
    n = len(grads)

    def emit(ins, theirs, sems, first, mid, last):
        send_sems, recv_sems = sems
        x, y, c = _place()

        def copies():
            return [pltpu.make_async_remote_copy(
                src_ref=ins[k].at[2 * j + (1 - c)], dst_ref=theirs[k].at[j],
                send_sem=send_sems.at[k, j], recv_sem=recv_sems.at[k, j],
                device_id=(x, y, 1 - c), device_id_type=MESH) for k in range(n) for j in range(4)]

        @_when(first)
        def _():
            for cp in copies():
                cp.start()

        @_when(last)
        def _():
            for cp in copies():
                cp.wait_recv()
            for cp in copies():
                cp.wait_send()

    shapes = [jax.ShapeDtypeStruct((4,) + g.shape[1:], g.dtype) for g in grads]
    return _Side(list(grads), shapes, [pltpu.SemaphoreType.DMA((n, 4)), pltpu.SemaphoreType.DMA((n, 4))], emit)


def _chip_side(sums):
    n = len(sums)

    def emit(ins, others, sems, first, mid, last):
        send_sems, recv_sems = sems
        x, y, c = _place()

        def copies():
            return [pltpu.make_async_remote_copy(
                src_ref=ins[k].at[2 * px + py], dst_ref=others[k].at[r],
                send_sem=send_sems.at[k, r], recv_sem=recv_sems.at[k, r],
                device_id=(px, py, c), device_id_type=MESH)
                for k in range(n) for r, (px, py) in enumerate(_other_chips(x, y))]

        @_when(first)
        def _():
            for cp in copies():
                cp.start()

        @_when(last)
        def _():
            for cp in copies():
                cp.wait_recv()
            for cp in copies():
                cp.wait_send()

    shapes = [jax.ShapeDtypeStruct((3,) + s.shape[1:], s.dtype) for s in sums]
    return _Side(list(sums), shapes, [pltpu.SemaphoreType.DMA((n, 3)), pltpu.SemaphoreType.DMA((n, 3))], emit)


def _comm_call(side, name):
    n_in, n_out = len(side.inputs), len(side.out_shapes)

    def body(*refs):
        side.emit(refs[:n_in], refs[n_in:n_in + n_out], refs[n_in + n_out:], True, True, True)

    return pl.pallas_call(
        body, name=name, in_specs=[ANY] * n_in, out_specs=[ANY] * n_out,
        out_shape=side.out_shapes, scratch_shapes=side.sem_shapes,
    )(*side.inputs)


class _SideRefs:
    def __init__(self, sides, n_in, n_out, n_scratch):
        self.sides, self.n_in, self.n_out, self.n_scratch = sides, n_in, n_out, n_scratch
        self.inputs = [a for s in sides for a in s.inputs]
        self.out_shapes = [o for s in sides for o in s.out_shapes]
        self.scratch = [m for s in sides for m in s.sem_shapes]
        self.in_specs = [ANY] * len(self.inputs)
        self.out_specs = [ANY] * len(self.out_shapes)

    def split(self, refs):
        refs = list(refs)
        n_side_in, n_side_out = len(self.inputs), len(self.out_shapes)
        ins, rest = refs[:self.n_in], refs[self.n_in:]
        side_in, rest = rest[:n_side_in], rest[n_side_in:]
        outs, rest = rest[:self.n_out], rest[self.n_out:]
        side_out, rest = rest[:n_side_out], rest[n_side_out:]
        scratch, side_sems = rest[:self.n_scratch], rest[self.n_scratch:]
        self._refs = (side_in, side_out, side_sems)
        return ins + outs + scratch

    def emit(self, first, mid, last):
        side_in, side_out, side_sems = self._refs
        for s in self.sides:
            a, b, m = len(s.inputs), len(s.out_shapes), len(s.sem_shapes)
            s.emit(side_in[:a], side_out[:b], side_sems[:m], first, mid, last)
            side_in, side_out, side_sems = side_in[a:], side_out[b:], side_sems[m:]


def _row_tile(rows, cols):
    tile = rows
    while tile * cols > 256 * 1024 and tile % 16 == 0:
        tile //= 2
    return tile


def _pair_sum(core, grads, theirs, name):
    _, rows, cols = theirs.shape
    rt = _row_tile(rows, cols)

    def body(core_ref, a_ref, b_ref, o_ref, ob_ref):
        total = a_ref[...] + b_ref[...]
        o_ref[...] = total
        ob_ref[...] = total.astype(BF16)

    spec = pl.BlockSpec((None, rt, cols), lambda j, i, core_ref: (j, i, 0))
    mine = pl.BlockSpec((None, None, rt, cols), lambda j, i, core_ref: (j, core_ref[0], i, 0))
    return pl.pallas_call(
        body, name=name,
        grid_spec=pltpu.PrefetchScalarGridSpec(
            num_scalar_prefetch=1, grid=(4, rows // rt), in_specs=[mine, spec], out_specs=[spec, spec]),
        out_shape=[jax.ShapeDtypeStruct(theirs.shape, F32), jax.ShapeDtypeStruct(theirs.shape, BF16)],
        compiler_params=_params(dimension_semantics=("arbitrary", "arbitrary")),
    )(core, grads.reshape(4, 2, rows, cols), theirs)


def _adamw(w, g, m, v):
    m = ADAM_B1 * m + (1.0 - ADAM_B1) * g
    v = ADAM_B2 * v + (1.0 - ADAM_B2) * (g * g)
    m_hat = m / (1.0 - ADAM_B1 ** ADAM_STEP)
    v_hat = v / (1.0 - ADAM_B2 ** ADAM_STEP)
    delta = -ADAM_LR * (m_hat / (jnp.sqrt(v_hat) + ADAM_EPS) + ADAM_WD * w)
    return delta, m, v


def _sum_adamw(chip, sums, others, w, m, v, name):
    _, rows, cols = sums.shape
    rt = _row_tile(rows, cols)

    def body(chip_ref, own_ref, oth_ref, w_ref, m_ref, v_ref, g_ref, d_ref, nm_ref, nv_ref):
        g = ((own_ref[...] + oth_ref[0].astype(F32)) + oth_ref[1].astype(F32)) + oth_ref[2].astype(F32)
        g_ref[...] = g
        d_ref[...], nm_ref[...], nv_ref[...] = _adamw(w_ref[...], g, m_ref[...], v_ref[...])

    spec = pl.BlockSpec((rt, cols), lambda i, chip_ref: (i, 0))
    own = pl.BlockSpec((None, rt, cols), lambda i, chip_ref: (chip_ref[0], i, 0))
    shape = jax.ShapeDtypeStruct((rows, cols), F32)
    return pl.pallas_call(
        body, name=name,
        grid_spec=pltpu.PrefetchScalarGridSpec(
            num_scalar_prefetch=1, grid=(rows // rt,),
            in_specs=[own, pl.BlockSpec((3, rt, cols), lambda i, chip_ref: (0, i, 0)), spec, spec, spec],
            out_specs=[spec] * 4),
        out_shape=[shape] * 4,
        compiler_params=_params(dimension_semantics=("arbitrary",)),
    )(chip, sums, others, w, m, v)


def _small_sum_adamw(gathered, ws, ms, vs):
    n = len(ws)

    def body(*refs):
        g8 = refs[:n + 1]
        w, m, v = refs[n + 1:2 * n + 1], refs[2 * n + 1:3 * n + 1], refs[3 * n + 1:4 * n + 1]
        outs = refs[4 * n + 1:]
        g_out, d_out, m_out, v_out = outs[:n + 1], outs[n + 1:2 * n + 1], outs[2 * n + 1:3 * n + 1], outs[3 * n + 1:]
        for k in range(n + 1):
            g = g8[k][0]
            for b in range(1, N_DEV):
                g = g + g8[k][b]
            g_out[k][...] = g
            if k < n:
                d_out[k][...], m_out[k][...], v_out[k][...] = _adamw(w[k][...], g, m[k][...], v[k][...])

    shapes = [jax.ShapeDtypeStruct(w.shape, F32) for w in ws]
    loss_shape = jax.ShapeDtypeStruct(gathered[-1].shape[1:], F32)
    outs = pl.pallas_call(
        body, name="small_sum_adamw",
        in_specs=[VMEM_FULL] * (4 * n + 1), out_specs=[VMEM_FULL] * (4 * n + 1),
        out_shape=shapes + [loss_shape] + shapes * 3,
        compiler_params=_params(),
    )(*gathered, *ws, *ms, *vs)
    return outs[:n + 1], outs[n + 1:2 * n + 1], outs[2 * n + 1:3 * n + 1], outs[3 * n + 1:]


SMALL_NAMES = ("pool_scale", "sgu_ln_g", "sgu_ln_b", "sgu_w", "sgu_b", "ln_g", "ln_b", "ple_gate_b")


TOK_TILE = 256
GRAD_TILE = 1024


def _weight_views():
    cols = lambda width: (lambda ref, b: ref.at[:, pl.ds(pl.multiple_of(b * width, 128), width)])
    rows = lambda height: (lambda ref, b: ref.at[pl.ds(pl.multiple_of(b * height, 16), height), :])
    pool_rows = POOL_GROUP // N_DEV
    return {"w_in": cols(D_IN // N_DEV),
            "pool_w": lambda ref, b: ref.at[:, pl.ds(pl.multiple_of(b * pool_rows, 16), pool_rows), :],
            "w_out": rows(D_MIX // N_DEV), "ple_w": cols(D_MODEL // N_DEV), "gate_w": rows(D_MODEL // N_DEV)}


WEIGHT_SHAPES = {"w_in": (D_MODEL, D_IN), "pool_w": (len(POOL_WINDOWS), POOL_GROUP, POOL_GROUP),
                 "w_out": (D_MIX, D_MODEL), "ple_w": (D_PLE, D_MODEL), "gate_w": (D_MODEL, D_MODEL)}


def _weight_gather(shards, names):
    views = _weight_views()
    return _gather_side([shards[nm] for nm in names],
                        [jax.ShapeDtypeStruct(WEIGHT_SHAPES[nm], BF16) for nm in names], [views[nm] for nm in names])


def kernel(x, p, w_in, pool_w, pool_scale, sgu_ln_g, sgu_ln_b, sgu_w, sgu_b, w_out, ln_g, ln_b, ple_w, ple_gate_w, ple_gate_b, loss_target, m_w_in, m_pool_w, m_pool_scale, m_sgu_ln_g, m_sgu_ln_b, m_sgu_w, m_sgu_b, m_w_out, m_ln_g, m_ln_b, m_ple_w, m_ple_gate_w, m_ple_gate_b, v_w_in, v_pool_w, v_pool_scale, v_sgu_ln_g, v_sgu_ln_b, v_sgu_w, v_sgu_b, v_w_out, v_ln_g, v_ln_b, v_ple_w, v_ple_gate_w, v_ple_gate_b):
    seq = x.shape[1]
    x2, p2, target = x[0], p[0, 0], loss_target[0]
    core = lax.axis_index("c").astype(jnp.int32).reshape(1)
    chip = (2 * lax.axis_index("x") + lax.axis_index("y")).astype(jnp.int32).reshape(1)
    pool_rows = POOL_GROUP // N_DEV

    shards = {"w_in": w_in[0].astype(BF16), "pool_w": pool_w[0].astype(BF16), "w_out": w_out[0].astype(BF16),
              "ple_w": ple_w[0].astype(BF16), "gate_w": ple_gate_w[0].astype(BF16)}
    w_in_f, pool_w_f = _comm_call(_weight_gather(shards, ("w_in", "pool_w")), "gather_mixer_weights")
    bias_tile = jnp.repeat(sgu_b[0].T, HEAD, axis=1)
    (h, y, pooled, xb), (w_out_f, ple_w_f, gate_w_f) = _forward_mixers(
        x2, w_in_f, pool_w_f, pool_scale, sgu_ln_g, sgu_ln_b, sgu_w[0], bias_tile, TOK_TILE,
        sides=(_weight_gather(shards, ("w_out", "ple_w", "gate_w")),))

    dy, dxr, xn, dgp, dpe, dr, loss, d_ln_g, d_ln_b, d_gate_b = _head_fwd_bwd(
        x2, y, p2, target, w_out_f, gate_w_f, ple_w_f, ln_g, ln_b, ple_gate_b, 2 * TOK_TILE, TOK_TILE)

    d_w_out, _ = _weight_grad(y, dr, 1, 1, GRAD_TILE, "grad_w_out")
    d_ple_w, _ = _weight_grad(p2, dpe, N_DEV, N_DEV, GRAD_TILE, "grad_ple_w")
    d_gate_w, _ = _weight_grad(xn, dgp, 1, 1, GRAD_TILE, "grad_gate_w")
    early_names = ("w_out", "ple_w", "gate_w")
    early = [d_w_out.reshape(N_DEV, D_MIX // N_DEV, D_MODEL), d_ple_w,
             d_gate_w.reshape(N_DEV, D_MODEL // N_DEV, D_MODEL)]
    early_theirs = _comm_call(_pair_side(early), "pair_exchange_head")
    early_sums = [_pair_sum(core, g, t, "pair_sum_" + nm) for g, t, nm in zip(early, early_theirs, early_names)]

    dh, dx, d_pool_w, d_pool_scale, d_sgu_ln_g, d_sgu_ln_b, d_sgu_w, d_sgu_b = _mixers_bwd(
        h, dy, dxr, pooled, w_in_f, pool_w_f, pool_scale, sgu_ln_g, sgu_ln_b, sgu_w[0], bias_tile, TOK_TILE)

    small = (d_pool_scale, d_sgu_ln_g, d_sgu_ln_b, d_sgu_w, d_sgu_b, d_ln_g, d_ln_b, d_gate_b)
    parts = small + (loss,)
    small_gather = _gather_side(parts, [jax.ShapeDtypeStruct((N_DEV,) + a.shape, F32) for a in parts],
                                [lambda ref, b: ref.at[b]] * len(parts))
    d_w_in, side_out = _weight_grad(xb, dh, N_DEV, 2, GRAD_TILE, "grad_w_in",
                                    sides=(_chip_side([s_bf for _, s_bf in early_sums]), small_gather))
    early_others, gathered = side_out[:3], side_out[3:]

    late_names = ("w_in", "pool_w")
    late = [d_w_in, d_pool_w.reshape(N_DEV, len(POOL_WINDOWS) * pool_rows, POOL_GROUP)]
    late_theirs = _comm_call(_pair_side(late), "pair_exchange_mixers")
    late_sums = [_pair_sum(core, g, t, "pair_sum_" + nm) for g, t, nm in zip(late, late_theirs, late_names)]
    late_others = _comm_call(_chip_side([s_bf for _, s_bf in late_sums]), "chip_exchange_mixers")

    shard_of = {"w_in": (w_in, m_w_in, v_w_in), "pool_w": (pool_w, m_pool_w, v_pool_w),
                "w_out": (w_out, m_w_out, v_w_out), "ple_w": (ple_w, m_ple_w, v_ple_w),
                "gate_w": (ple_gate_w, m_ple_gate_w, v_ple_gate_w)}
    big_out = {}
    for nm, (s_f32, _), oth in zip(early_names + late_names, early_sums + late_sums,
                                   list(early_others) + list(late_others)):
        w, m, v = shard_of[nm]
        two_d = s_f32.shape[1:]
        res = _sum_adamw(chip, s_f32, oth, w.reshape(two_d), m.reshape(two_d), v.reshape(two_d), "adamw_" + nm)
        big_out[nm] = [r.reshape(w.shape) for r in res]

    small_w = (pool_scale, sgu_ln_g, sgu_ln_b, sgu_w, sgu_b, ln_g, ln_b, ple_gate_b)
    small_m = (m_pool_scale, m_sgu_ln_g, m_sgu_ln_b, m_sgu_w, m_sgu_b, m_ln_g, m_ln_b, m_ple_gate_b)
    small_v = (v_pool_scale, v_sgu_ln_g, v_sgu_ln_b, v_sgu_w, v_sgu_b, v_ln_g, v_ln_b, v_ple_gate_b)
    natural = [[a.reshape(g.shape) for a, g in zip(group, small)] for group in (small_w, small_m, small_v)]
    res = _small_sum_adamw(list(gathered), *natural)
    g_s, d_s, m_s, v_s = [[r.reshape(w.shape) for r, w in zip(kind, small_w)] for kind in res]
    total_loss = res[0][-1][0, 0]

    order = ("w_in", "pool_w", "pool_scale", "sgu_ln_g", "sgu_ln_b", "sgu_w", "sgu_b", "w_out", "ln_g", "ln_b",
             "ple_w", "ple_gate_w", "ple_gate_b")
    outs = [total_loss, dx.reshape(1, seq, D_MODEL)]
    for kind in range(4):
        for nm in order:
            key = "gate_w" if nm == "ple_gate_w" else nm
            if key in big_out:
                outs.append(big_out[key][kind])
            else:
                outs.append((g_s, d_s, m_s, v_s)[kind][SMALL_NAMES.index(nm)])
    return tuple(outs)
```

```python
from typing import Callable, NamedTuple

import numpy as np
import jax
import jax.numpy as jnp
from jax import lax
from jax.experimental import pallas as pl
from jax.experimental.pallas import tpu as pltpu

F32 = jnp.float32
BF16 = jnp.bfloat16

N_DEV = 8
D_MODEL = 1024
D_POOL = 1024
D_SGU = 1024
D_MIX = 2048
D_IN = 5120
D_PLE = 256
POOL_WINDOWS = (2, 4, 8, 16)
POOL_GROUP = 256
N_HEADS = 4
HEAD = 256
CHUNK = 128
HALO = 16
BAND_PAD = 128
ALPHA = 2.0 ** 0.25
LN_EPS = 1e-5
ADAM_LR, ADAM_B1, ADAM_B2, ADAM_EPS, ADAM_WD, ADAM_STEP = 0.001, 0.9, 0.999, 1e-08, 0.01, 10

U0, V0, Z0 = D_POOL, D_POOL + D_SGU, D_POOL + 2 * D_SGU
VMEM_LIMIT = 56 * 1024 * 1024
MESH = pl.DeviceIdType.MESH
ANY = pl.BlockSpec(memory_space=pl.ANY)
VMEM_FULL = pl.BlockSpec(memory_space=pltpu.VMEM)

_GELU_C0 = 0.7978845608028654
_GELU_C1 = 0.044715


def _gelu_cdf(x, x2):
    return 1.0 / (1.0 + jnp.exp(x * ((-2.0 * _GELU_C0) + (-2.0 * _GELU_C0 * _GELU_C1) * x2)))


def _gelu_and_grad(x):
    x2 = x * x
    cdf = _gelu_cdf(x, x2)
    g = x * cdf
    dg = cdf + g * (1.0 - cdf) * ((2.0 * _GELU_C0) + (6.0 * _GELU_C0 * _GELU_C1) * x2)
    return g, dg


def _gelu(x):
    t = jnp.tanh(_GELU_C0 * (x + _GELU_C1 * (x * x * x)))
    return x * (0.5 * (1.0 + t))


def _split_bf16(x):
    hi = x.astype(BF16)
    return hi, (x - hi.astype(F32)).astype(BF16)


def _band(tok_tile, window, transpose):
    t = np.arange(tok_tile)[:, None]
    s = np.arange(tok_tile + BAND_PAD)[None, :]
    d = (s - t) if transpose else (t + BAND_PAD - s)
    return ((d >= 0) & (d < window)).astype(np.float32)


def _bands(tok_tile, transpose):
    return jnp.asarray(np.stack([_band(tok_tile, w, transpose) for w in POOL_WINDOWS]), dtype=BF16)


def _sigmoid(x):
    return 1.0 / (1.0 + jnp.exp(-x))


def _dot(a, b):
    return jnp.dot(a, b, preferred_element_type=F32)


def _dot_nt(a, b):
    return lax.dot_general(a, b, (((1,), (1,)), ((), ())), preferred_element_type=F32)


def _dot_tn(a, b):
    return lax.dot_general(a, b, (((0,), (0,)), ((), ())), preferred_element_type=F32)


def _row_stats(x):
    mu = jnp.mean(x, axis=-1, keepdims=True)
    xc = x - mu
    var = jnp.mean(xc * xc, axis=-1, keepdims=True)
    rstd = lax.rsqrt(var + LN_EPS)
    return xc * rstd, rstd


def _ln_bwd(dxhat, xhat, rstd):
    m1 = jnp.mean(dxhat, axis=-1, keepdims=True)
    m2 = jnp.mean(dxhat * xhat, axis=-1, keepdims=True)
    return rstd * (dxhat - m1 - xhat * m2)


def _masked_sgu_w(sw_ref, hh):
    row = lax.broadcasted_iota(jnp.int32, (CHUNK, CHUNK), 0)
    col = lax.broadcasted_iota(jnp.int32, (CHUNK, CHUNK), 1)
    return jnp.where(row >= col, sw_ref[hh], 0.0)


def _inv_count(tile_index, tok_tile, window):
    tok = tile_index * tok_tile + lax.broadcasted_iota(jnp.int32, (tok_tile, 1), 0)
    return 1.0 / jnp.minimum(tok + 1, window).astype(F32)


def _params(**kw):
    return pltpu.CompilerParams(vmem_limit_bytes=VMEM_LIMIT, **kw)


def _forward_mixers(x, w_in, pool_w, pool_scale, sgu_ln_g, sgu_ln_b, sgu_w, sgu_bias_tile, tok_tile, sides=()):
    seq = x.shape[0]
    n_tiles = seq // tok_tile
    n_chunks = tok_tile // CHUNK
    side_refs = _SideRefs(sides, 8, 4, 1)

    def body(*refs):
        (x_ref, win_ref, pw_ref, ps_ref, lg_ref, lb_ref, sw_ref, sb_ref,
         h_ref, y_ref, pooled_ref, xb_ref, aext_ref) = side_refs.split(refs)
        i = pl.program_id(0)
        side_refs.emit(i == 0, i == n_tiles // 2, False)

        @pl.when(i == 0)
        def _():
            aext_ref[0:HALO, :] = jnp.zeros((HALO, D_POOL), F32)

        xb = x_ref[...].astype(BF16)
        xb_ref[...] = xb
        for s in range(D_IN // 1024):
            cs = slice(s * 1024, (s + 1) * 1024)
            h_ref[:, cs] = _dot(xb, win_ref[:, cs])

        aext_ref[HALO:HALO + tok_tile, :] = h_ref[:, 0:D_POOL]
        for g, window in enumerate(POOL_WINDOWS):
            cols = slice(g * POOL_GROUP, (g + 1) * POOL_GROUP)
            win = aext_ref[HALO:HALO + tok_tile, cols]
            for k in range(1, window):
                win = win + aext_ref[HALO - k:HALO - k + tok_tile, cols]
            pooled = win * _inv_count(i, tok_tile, window) - h_ref[:, cols]
            pb = pooled.astype(BF16)
            pooled_ref[:, cols] = pb
            mixed = _dot(pb, pw_ref[g])
            z = h_ref[:, Z0 + g * POOL_GROUP:Z0 + (g + 1) * POOL_GROUP]
            y_ref[:, cols] = (mixed * ps_ref[:, cols] * (z * _sigmoid(z))).astype(BF16)
        aext_ref[0:HALO, :] = aext_ref[tok_tile:tok_tile + HALO, :]

        for hh in range(N_HEADS):
            cols = slice(hh * HEAD, (hh + 1) * HEAD)
            swm = _masked_sgu_w(sw_ref, hh).astype(BF16)
            for n in range(n_chunks):
                rows = slice(n * CHUNK, (n + 1) * CHUNK)
                gu = _gelu(h_ref[rows, U0 + hh * HEAD:U0 + (hh + 1) * HEAD])
                gv = _gelu(h_ref[rows, V0 + hh * HEAD:V0 + (hh + 1) * HEAD])
                xhat, _ = _row_stats(gv)
                vln = xhat * lg_ref[:, cols] + lb_ref[:, cols]
                sv = _dot(swm, vln.astype(BF16)) + sb_ref[:, cols]
                z = h_ref[rows, Z0 + D_POOL + hh * HEAD:Z0 + D_POOL + (hh + 1) * HEAD]
                y_ref[rows, D_POOL + hh * HEAD:D_POOL + (hh + 1) * HEAD] = (
                    gu * sv * (z * _sigmoid(z))).astype(BF16)

        side_refs.emit(False, False, i == n_tiles - 1)

    tok = lambda width: pl.BlockSpec((tok_tile, width), lambda i: (i, 0))
    outs = pl.pallas_call(
        body, name="forward_mixers",
        grid=(n_tiles,),
        in_specs=[tok(D_MODEL)] + [VMEM_FULL] * 7 + side_refs.in_specs,
        out_specs=[tok(D_IN), tok(D_MIX), tok(D_POOL), tok(D_MODEL)] + side_refs.out_specs,
        out_shape=[jax.ShapeDtypeStruct((seq, D_IN), F32), jax.ShapeDtypeStruct((seq, D_MIX), BF16),
                   jax.ShapeDtypeStruct((seq, D_POOL), BF16), jax.ShapeDtypeStruct((seq, D_MODEL), BF16)]
        + side_refs.out_shapes,
        scratch_shapes=[pltpu.VMEM((HALO + tok_tile, D_POOL), F32)] + side_refs.scratch,
        compiler_params=_params(dimension_semantics=("arbitrary",)),
    )(x, w_in, pool_w, pool_scale, sgu_ln_g, sgu_ln_b, sgu_w, sgu_bias_tile, *side_refs.inputs)
    return outs[:4], outs[4:]


def _head_fwd_bwd(x, y, p, target, w_out, gate_w, ple_w, ln_g, ln_b, gate_b, tok_tile, sub_tile):
    seq = x.shape[0]

    def body(x_ref, y_ref, p_ref, t_ref, wout_ref, gw_ref, plw_ref, lng_ref, lnb_ref, gb_ref,
             dy_ref, dxr_ref, xn_ref, dgp_ref, dpe_ref, dr_ref, loss_ref, dlng_ref, dlnb_ref, dgb_ref):
        i = pl.program_id(0)

        @pl.when(i == 0)
        def _():
            loss_ref[...] = jnp.zeros_like(loss_ref)
            dlng_ref[...] = jnp.zeros_like(dlng_ref)
            dlnb_ref[...] = jnp.zeros_like(dlnb_ref)
            dgb_ref[...] = jnp.zeros_like(dgb_ref)

        subs = [slice(s * sub_tile, (s + 1) * sub_tile) for s in range(tok_tile // sub_tile)]
        stats, xns, douts = [], [], []
        for rows in subs:
            r = ALPHA * x_ref[rows, :] + _dot(y_ref[rows, :], wout_ref[...])
            xhat, rstd = _row_stats(r)
            xn = xhat * lng_ref[...] + lnb_ref[...]
            xn_ref[rows, :] = xn.astype(BF16)
            stats.append((xhat, rstd))
            xns.append(xn)
        loss = jnp.zeros((1, 1), F32)
        dgb = jnp.zeros((1, D_MODEL), F32)
        for rows, xn in zip(subs, xns):
            gate = _sigmoid(_dot(xn_ref[rows, :], gw_ref[...]) + gb_ref[...])
            pe = _dot(p_ref[rows, :].astype(BF16), plw_ref[...])
            err = xn + gate * pe - t_ref[rows, :]
            loss = loss + jnp.sum(err * err, keepdims=True)
            dout = err * (1.0 / D_MODEL)
            dpe_ref[rows, :] = (dout * gate).astype(BF16)
            dgpre = dout * pe * gate * (1.0 - gate)
            dgb = dgb + jnp.sum(dgpre, axis=0, keepdims=True)
            dgp_ref[rows, :] = dgpre.astype(BF16)
            douts.append(dout)
        loss_ref[...] += (0.5 / D_MODEL) * loss
        dgb_ref[...] += dgb
        dlng = jnp.zeros((1, D_MODEL), F32)
        dlnb = jnp.zeros((1, D_MODEL), F32)
        for rows, (xhat, rstd), dout in zip(subs, stats, douts):
            dxn = dout + _dot_nt(dgp_ref[rows, :], gw_ref[...])
            dlng = dlng + jnp.sum(dxn * xhat, axis=0, keepdims=True)
            dlnb = dlnb + jnp.sum(dxn, axis=0, keepdims=True)
            dr = _ln_bwd(dxn * lng_ref[...], xhat, rstd)
            dxr_ref[rows, :] = ALPHA * dr
            dr_ref[rows, :] = dr.astype(BF16)
        dlng_ref[...] += dlng
        dlnb_ref[...] += dlnb
        for rows in subs:
            dy_ref[rows, :] = _dot_nt(dr_ref[rows, :], wout_ref[...])

    tok = lambda width: pl.BlockSpec((tok_tile, width), lambda i: (i, 0))
    acc = lambda width: pl.BlockSpec((1, width), lambda i: (0, 0))
    vec = jax.ShapeDtypeStruct((1, D_MODEL), F32)
    return pl.pallas_call(
        body, name="head_fwd_bwd",
        grid=(seq // tok_tile,),
        in_specs=[tok(D_MODEL), tok(D_MIX), tok(D_PLE), tok(D_MODEL),
                  VMEM_FULL, VMEM_FULL, VMEM_FULL, VMEM_FULL, VMEM_FULL, VMEM_FULL],
        out_specs=[tok(D_MIX), tok(D_MODEL), tok(D_MODEL), tok(D_MODEL), tok(D_MODEL), tok(D_MODEL),
                   acc(128), acc(D_MODEL), acc(D_MODEL), acc(D_MODEL)],
        out_shape=[jax.ShapeDtypeStruct((seq, D_MIX), F32), jax.ShapeDtypeStruct((seq, D_MODEL), F32),
                   jax.ShapeDtypeStruct((seq, D_MODEL), BF16), jax.ShapeDtypeStruct((seq, D_MODEL), BF16),
                   jax.ShapeDtypeStruct((seq, D_MODEL), BF16), jax.ShapeDtypeStruct((seq, D_MODEL), BF16),
                   jax.ShapeDtypeStruct((1, 128), F32), vec, vec, vec],
        compiler_params=_params(dimension_semantics=("arbitrary",)),
    )(x, y, p, target, w_out, gate_w, ple_w, ln_g, ln_b, gate_b)


def _mixers_bwd(h, dy, dxr, pooled, w_in, pool_w, pool_scale, sgu_ln_g, sgu_ln_b, sgu_w, sgu_bias_tile, tok_tile):
    seq = h.shape[0]
    n_tiles = seq // tok_tile
    n_chunks = tok_tile // CHUNK
    pool_rows = POOL_GROUP // N_DEV

    def body(h_ref, dy_ref, dxr_ref, pooled_ref, win_ref, pw_ref, ps_ref, lg_ref, lb_ref, sw_ref, sb_ref, band_ref,
             dh_ref, dx_ref, dpw_ref, dps_ref, dlg_ref, dlb_ref, dsw_ref, dsb_ref,
             qhi_ref, qlo_ref, dpw_acc, dsb_acc):
        i = pl.program_id(0)
        tile = n_tiles - 1 - i

        @pl.when(i == 0)
        def _():
            qhi_ref[...] = jnp.zeros_like(qhi_ref)
            qlo_ref[...] = jnp.zeros_like(qlo_ref)
            dpw_acc[...] = jnp.zeros_like(dpw_acc)
            dsb_acc[...] = jnp.zeros_like(dsb_acc)
            dps_ref[...] = jnp.zeros_like(dps_ref)
            dlg_ref[...] = jnp.zeros_like(dlg_ref)
            dlb_ref[...] = jnp.zeros_like(dlb_ref)
            dsw_ref[...] = jnp.zeros_like(dsw_ref)

        everything = slice(0, tok_tile)
        for part in (qhi_ref, qlo_ref):
            part[tok_tile:tok_tile + HALO, :] = part[0:HALO, :]
        for g, window in enumerate(POOL_WINDOWS):
            cols = slice(g * POOL_GROUP, (g + 1) * POOL_GROUP)
            zcols = slice(Z0 + g * POOL_GROUP, Z0 + (g + 1) * POOL_GROUP)
            z = h_ref[:, zcols]
            sz = _sigmoid(z)
            pb = pooled_ref[:, cols]
            mixed = _dot(pb, pw_ref[g])
            dyp = dy_ref[:, cols]
            dh_ref[:, zcols] = (dyp * (mixed * ps_ref[:, cols]) * (sz * (1.0 + z * (1.0 - sz)))).astype(BF16)
            dms = dyp * (z * sz)
            dps_ref[:, cols] += jnp.sum(dms * mixed, axis=0, keepdims=True)
            dmixed = (dms * ps_ref[:, cols]).astype(BF16)
            dpw_acc[g] += _dot_tn(pb, dmixed)
            dpooled = _dot_nt(dmixed, pw_ref[g])
            qhi_ref[everything, cols], qlo_ref[everything, cols] = _split_bf16(
                dpooled * _inv_count(tile, tok_tile, window))
            da = _dot(band_ref[g], qhi_ref[:, cols]) + _dot(band_ref[g], qlo_ref[:, cols]) - dpooled
            dh_ref[:, cols] = da.astype(BF16)

        for hh in range(N_HEADS):
            cols = slice(hh * HEAD, (hh + 1) * HEAD)
            ucols = slice(U0 + hh * HEAD, U0 + (hh + 1) * HEAD)
            vcols = slice(V0 + hh * HEAD, V0 + (hh + 1) * HEAD)
            zcols = slice(Z0 + D_POOL + hh * HEAD, Z0 + D_POOL + (hh + 1) * HEAD)
            sw32 = _masked_sgu_w(sw_ref, hh)
            swm = sw32.astype(BF16)
            swm_t = sw32.T.astype(BF16)
            for n in range(n_chunks):
                rows = slice(n * CHUNK, (n + 1) * CHUNK)
                gu, dgu_du = _gelu_and_grad(h_ref[rows, ucols])
                gv, dgv_dv = _gelu_and_grad(h_ref[rows, vcols])
                xhat, rstd = _row_stats(gv)
                vb = (xhat * lg_ref[:, cols] + lb_ref[:, cols]).astype(BF16)
                sv = _dot(swm, vb) + sb_ref[:, cols]
                z = h_ref[rows, zcols]
                sz = _sigmoid(z)
                dys = dy_ref[rows, D_POOL + hh * HEAD:D_POOL + (hh + 1) * HEAD]
                dh_ref[rows, zcols] = (dys * (gu * sv) * (sz * (1.0 + z * (1.0 - sz)))).astype(BF16)
                dyg = dys * (z * sz)
                dh_ref[rows, ucols] = (dyg * sv * dgu_du).astype(BF16)
                dsv = dyg * gu
                dsb_acc[:, cols] += dsv
                dsvb = dsv.astype(BF16)
                dsw_ref[hh] += _dot_nt(dsvb, vb)
                dvln = _dot(swm_t, dsvb)
                dlg_ref[:, cols] += jnp.sum(dvln * xhat, axis=0, keepdims=True)
                dlb_ref[:, cols] += jnp.sum(dvln, axis=0, keepdims=True)
                dgv = _ln_bwd(dvln * lg_ref[:, cols], xhat, rstd)
                dh_ref[rows, vcols] = (dgv * dgv_dv).astype(BF16)

        dx_ref[...] = dxr_ref[...] + _dot_nt(dh_ref[...], win_ref[...])

        @pl.when(i == n_tiles - 1)
        def _():
            for g in range(len(POOL_WINDOWS)):
                for b in range(N_DEV):
                    dpw_ref[b, g] = dpw_acc[g, b * pool_rows:(b + 1) * pool_rows, :]
            row = lax.broadcasted_iota(jnp.int32, (CHUNK, CHUNK), 0)
            col = lax.broadcasted_iota(jnp.int32, (CHUNK, CHUNK), 1)
            for hh in range(N_HEADS):
                dsw_ref[hh] = jnp.where(row >= col, dsw_ref[hh], 0.0)
                total = jnp.sum(dsb_acc[:, hh * HEAD:(hh + 1) * HEAD], axis=1, keepdims=True)
                dsb_ref[hh:hh + 1, :] = jnp.broadcast_to(total, (CHUNK, CHUNK)).T[0:1, :]

    tok = lambda width: pl.BlockSpec((tok_tile, width), lambda i: (n_tiles - 1 - i, 0))
    whole = lambda shape: pl.BlockSpec(shape, lambda i: (0,) * len(shape))
    vec = jax.ShapeDtypeStruct((1, D_MODEL), F32)
    return pl.pallas_call(
        body, name="mixers_bwd",
        grid=(n_tiles,),
        in_specs=[tok(D_IN), tok(D_MIX), tok(D_MODEL), tok(D_POOL)] + [VMEM_FULL] * 8,
        out_specs=[tok(D_IN), tok(D_MODEL), whole((N_DEV, len(POOL_WINDOWS), pool_rows, POOL_GROUP)),
                   whole((1, D_POOL)), whole((1, D_SGU)), whole((1, D_SGU)),
                   whole((N_HEADS, CHUNK, CHUNK)), whole((N_HEADS, CHUNK))],
        out_shape=[jax.ShapeDtypeStruct((seq, D_IN), BF16), jax.ShapeDtypeStruct((seq, D_MODEL), F32),
                   jax.ShapeDtypeStruct((N_DEV, len(POOL_WINDOWS), pool_rows, POOL_GROUP), F32),
                   vec, vec, vec,
                   jax.ShapeDtypeStruct((N_HEADS, CHUNK, CHUNK), F32),
                   jax.ShapeDtypeStruct((N_HEADS, CHUNK), F32)],
        scratch_shapes=[pltpu.VMEM((tok_tile + BAND_PAD, D_POOL), BF16),
                        pltpu.VMEM((tok_tile + BAND_PAD, D_POOL), BF16),
                        pltpu.VMEM((len(POOL_WINDOWS), POOL_GROUP, POOL_GROUP), F32),
                        pltpu.VMEM((CHUNK, D_SGU), F32)],
        compiler_params=_params(dimension_semantics=("arbitrary",)),
    )(h, dy, dxr, pooled, w_in, pool_w, pool_scale, sgu_ln_g, sgu_ln_b, sgu_w, sgu_bias_tile, _bands(tok_tile, True))


def _weight_grad(a, b, n_col_blocks, blocks_per_step, tok_tile, name, sides=()):
    seq, m = a.shape
    n = b.shape[1]
    nb = n // n_col_blocks
    n_steps = n_col_blocks // blocks_per_step
    n_k = seq // tok_tile
    side_refs = _SideRefs(sides, 2, 1, 0)

    def body(*refs):
        a_ref, b_ref, out_ref = side_refs.split(refs)
        step = pl.program_id(0) * n_k + pl.program_id(1)
        side_refs.emit(step == 0, step == (n_steps * n_k) // 2, False)

        @pl.when(pl.program_id(1) == 0)
        def _():
            out_ref[...] = jnp.zeros_like(out_ref)

        res = _dot_tn(a_ref[...].astype(BF16), b_ref[...])
        for blk in range(blocks_per_step):
            out_ref[blk] += res[:, blk * nb:(blk + 1) * nb]

        side_refs.emit(False, False, step == n_steps * n_k - 1)

    outs = pl.pallas_call(
        body, name=name,
        grid=(n_steps, n_k),
        in_specs=[pl.BlockSpec((tok_tile, m), lambda j, k: (k, 0)),
                  pl.BlockSpec((tok_tile, blocks_per_step * nb), lambda j, k: (k, j))] + side_refs.in_specs,
        out_specs=[pl.BlockSpec((blocks_per_step, m, nb), lambda j, k: (j, 0, 0))] + side_refs.out_specs,
        out_shape=[jax.ShapeDtypeStruct((n_col_blocks, m, nb), F32)] + side_refs.out_shapes,
        scratch_shapes=side_refs.scratch,
        compiler_params=_params(dimension_semantics=("arbitrary", "arbitrary")),
    )(a, b, *side_refs.inputs)
    return outs[0], outs[1:]


class _Side(NamedTuple):
    inputs: list
    out_shapes: list
    sem_shapes: list
    emit: Callable


def _when(cond):
    if cond is True:
        return lambda f: f()
    if cond is False:
        return lambda f: None
    return pl.when(cond)


def _place():
    return lax.axis_index("x"), lax.axis_index("y"), lax.axis_index("c")


def _other_chips(x, y):
    return [(1 - x, y), (x, 1 - y), (1 - x, 1 - y)]


def _gather_side(shards, out_shapes, views):
    n = len(shards)

    def emit(ins, outs, sems, first, mid, last):
        send_sems, recv_sems, local_sems = sems
        x, y, c = _place()
        me, sibling = (x, y, c), (x, y, 1 - c)
        chips = _other_chips(x, y)

        def copy(k, s, block, to, src=None):
            px, py, pc = block
            dst = views[k](outs[k], 4 * px + 2 * py + pc)
            return pltpu.make_async_remote_copy(
                src_ref=dst if src is None else src, dst_ref=dst,
                send_sem=send_sems.at[k, s], recv_sem=recv_sems.at[k, s],
                device_id=to, device_id_type=MESH)

        def own(k):
            return pltpu.make_async_copy(ins[k], views[k](outs[k], 4 * x + 2 * y + c), local_sems.at[k])

        def sends(k):
            return [copy(k, 0, me, sibling, src=ins[k])] + [
                copy(k, 1 + j, me, (*chip, c), src=ins[k]) for j, chip in enumerate(chips)]

        def passed_on(k, j):
            return copy(k, 4 + j, (*chips[j], c), sibling)

        @_when(first)
        def _():
            for k in range(n):
                own(k).start()
                for cp in sends(k):
                    cp.start()

        @_when(mid)
        def _():
            for k in range(n):
                for j, chip in enumerate(chips):
                    copy(k, 1 + j, (*chip, c), me).wait_recv()
                    passed_on(k, j).start()

        @_when(last)
        def _():
            for k in range(n):
                copy(k, 0, sibling, me).wait_recv()
                for j, chip in enumerate(chips):
                    copy(k, 4 + j, (*chip, 1 - c), me).wait_recv()
            for k in range(n):
                for cp in sends(k) + [passed_on(k, j) for j in range(len(chips))]:
                    cp.wait_send()
                own(k).wait()

    sems = [pltpu.SemaphoreType.DMA((n, 7)), pltpu.SemaphoreType.DMA((n, 7)), pltpu.SemaphoreType.DMA((n,))]
    return _Side(list(shards), list(out_shapes), sems, emit)


def _pair_side(grads):
    n = len(grads)

    def emit(ins, theirs, sems, first, mid, last):
        send_sems, recv_sems = sems
        x, y, c = _place()

        def copies():
            return [pltpu.make_async_remote_copy(
                src_ref=ins[k].at[2 * j + (1 - c)], dst_ref=theirs[k].at[j],
                send_sem=send_sems.at[k, j], recv_sem=recv_sems.at[k, j],
                device_id=(x, y, 1 - c), device_id_type=MESH) for k in range(n) for j in range(4)]

        @_when(first)
        def _():
            for cp in copies():
                cp.start()

        @_when(last)
        def _():
            for cp in copies():
                cp.wait_recv()
            for cp in copies():
                cp.wait_send()

    shapes = [jax.ShapeDtypeStruct((4,) + g.shape[1:], g.dtype) for g in grads]
    return _Side(list(grads), shapes, [pltpu.SemaphoreType.DMA((n, 4)), pltpu.SemaphoreType.DMA((n, 4))], emit)


def _chip_side(sums):
    n = len(sums)

    def emit(ins, others, sems, first, mid, last):
        send_sems, recv_sems = sems
        x, y, c = _place()

        def copies():
            return [pltpu.make_async_remote_copy(
                src_ref=ins[k].at[2 * px + py], dst_ref=others[k].at[r],
                send_sem=send_sems.at[k, r], recv_sem=recv_sems.at[k, r],
                device_id=(px, py, c), device_id_type=MESH)
                for k in range(n) for r, (px, py) in enumerate(_other_chips(x, y))]

        @_when(first)
        def _():
            for cp in copies():
                cp.start()

        @_when(last)
        def _():
            for cp in copies():
                cp.wait_recv()
            for cp in copies():
                cp.wait_send()

    shapes = [jax.ShapeDtypeStruct((3,) + s.shape[1:], s.dtype) for s in sums]
    return _Side(list(sums), shapes, [pltpu.SemaphoreType.DMA((n, 3)), pltpu.SemaphoreType.DMA((n, 3))], emit)


def _comm_call(side, name):
    n_in, n_out = len(side.inputs), len(side.out_shapes)

    def body(*refs):
        side.emit(refs[:n_in], refs[n_in:n_in + n_out], refs[n_in + n_out:], True, True, True)

    return pl.pallas_call(
        body, name=name, in_specs=[ANY] * n_in, out_specs=[ANY] * n_out,
        out_shape=side.out_shapes, scratch_shapes=side.sem_shapes,
    )(*side.inputs)


class _SideRefs:
    def __init__(self, sides, n_in, n_out, n_scratch):
        self.sides, self.n_in, self.n_out, self.n_scratch = sides, n_in, n_out, n_scratch
        self.inputs = [a for s in sides for a in s.inputs]
        self.out_shapes = [o for s in sides for o in s.out_shapes]
        self.scratch = [m for s in sides for m in s.sem_shapes]
        self.in_specs = [ANY] * len(self.inputs)
        self.out_specs = [ANY] * len(self.out_shapes)

    def split(self, refs):
        refs = list(refs)
        n_side_in, n_side_out = len(self.inputs), len(self.out_shapes)
        ins, rest = refs[:self.n_in], refs[self.n_in:]
        side_in, rest = rest[:n_side_in], rest[n_side_in:]
        outs, rest = rest[:self.n_out], rest[self.n_out:]
        side_out, rest = rest[:n_side_out], rest[n_side_out:]
        scratch, side_sems = rest[:self.n_scratch], rest[self.n_scratch:]
        self._refs = (side_in, side_out, side_sems)
        return ins + outs + scratch

    def emit(self, first, mid, last):
        side_in, side_out, side_sems = self._refs
        for s in self.sides:
            a, b, m = len(s.inputs), len(s.out_shapes), len(s.sem_shapes)
            s.emit(side_in[:a], side_out[:b], side_sems[:m], first, mid, last)
            side_in, side_out, side_sems = side_in[a:], side_out[b:], side_sems[m:]


def _grad_w_in_reduced(order, a, b, tok_tile, sides=()):
    seq, m = a.shape
    nb = b.shape[1] // N_DEV
    n_k = seq // tok_tile
    n_rows = 4
    last_step = n_rows * n_k - 1
    assert n_k >= 3
    side_refs = _SideRefs(sides, 3, 3, 8)

    def body(*refs):
        (order_ref, a_ref, b_ref, own_ref, theirs_ref, others_ref,
         acc_ref, stage_ref, sumbf_ref, pair_send, pair_recv, ici_send, ici_recv, stage_sem) = side_refs.split(refs)
        j, k = pl.program_id(0), pl.program_id(1)
        step = j * n_k + k
        side_refs.emit(step == 0, step == (n_rows * n_k) // 2, False)
        x, y, c = _place()
        chips = _other_chips(x, y)

        def to_sibling(row):
            return pltpu.make_async_remote_copy(
                src_ref=acc_ref.at[row % 2, 1 - c], dst_ref=theirs_ref.at[row],
                send_sem=pair_send.at[row], recv_sem=pair_recv.at[row],
                device_id=(x, y, 1 - c), device_id_type=MESH)

        def to_owner(row):
            px, py = chips[row]
            return pltpu.make_async_remote_copy(
                src_ref=sumbf_ref.at[row], dst_ref=others_ref.at[row],
                send_sem=ici_send.at[row], recv_sem=ici_recv.at[row],
                device_id=(px, py, c), device_id_type=MESH)

        def staged(row):
            return pltpu.make_async_copy(theirs_ref.at[row], stage_ref, stage_sem.at[0])

        @pl.when(k == 0)
        def _():
            acc_ref[j % 2] = jnp.zeros((2, m, nb), F32)

        res = _dot_tn(a_ref[...].astype(BF16), b_ref[...])
        for blk in range(2):
            acc_ref[j % 2, blk] += res[:, blk * nb:(blk + 1) * nb]

        for row in range(n_rows):
            @pl.when((j == row) & (k == n_k - 1))
            def _():
                to_sibling(row).start()

            if row < n_rows - 1:
                @pl.when((j == row + 1) & (k == 1))
                def _():
                    to_sibling(row).wait_recv()
                    staged(row).start()

                @pl.when((j == row + 1) & (k == 2))
                def _():
                    staged(row).wait()
                    to_sibling(row).wait_send()
                    sumbf_ref[row] = (acc_ref[row % 2, c] + stage_ref[...]).astype(BF16)
                    to_owner(row).start()

        @pl.when(step == last_step)
        def _():
            row = n_rows - 1
            to_sibling(row).wait_recv()
            staged(row).start()
            staged(row).wait()
            to_sibling(row).wait_send()
            own_ref[...] = acc_ref[row % 2, c] + stage_ref[...]
            for r in range(n_rows - 1):
                to_owner(r).wait_recv()
                to_owner(r).wait_send()

        side_refs.emit(False, False, step == last_step)

    block = jax.ShapeDtypeStruct((m, nb), F32)
    outs = pl.pallas_call(
        body, name="grad_w_in",
        grid_spec=pltpu.PrefetchScalarGridSpec(
            num_scalar_prefetch=1, grid=(n_rows, n_k),
            in_specs=[pl.BlockSpec((tok_tile, m), lambda j, k, order_ref: (k, 0)),
                      pl.BlockSpec((tok_tile, 2 * nb), lambda j, k, order_ref: (k, order_ref[j]))]
            + side_refs.in_specs,
            out_specs=[pl.BlockSpec((m, nb), lambda j, k, order_ref: (0, 0)), ANY, ANY] + side_refs.out_specs,
            scratch_shapes=[pltpu.VMEM((2, 2, m, nb), F32), pltpu.VMEM((m, nb), F32),
                            pltpu.VMEM((n_rows - 1, m, nb), BF16),
                            pltpu.SemaphoreType.DMA((n_rows,)), pltpu.SemaphoreType.DMA((n_rows,)),
                            pltpu.SemaphoreType.DMA((n_rows - 1,)), pltpu.SemaphoreType.DMA((n_rows - 1,)),
                            pltpu.SemaphoreType.DMA((1,))] + side_refs.scratch),
        out_shape=[block, jax.ShapeDtypeStruct((n_rows, m, nb), F32),
                   jax.ShapeDtypeStruct((n_rows - 1, m, nb), BF16)] + side_refs.out_shapes,
        compiler_params=_params(dimension_semantics=("arbitrary", "arbitrary")),
    )(order, a, b, *side_refs.inputs)
    return outs[0], outs[2], outs[3:]


def _row_tile(rows, cols):
    tile = rows
    while tile * cols > 256 * 1024 and tile % 16 == 0:
        tile //= 2
    return tile


def _pair_sum(core, grads, theirs, name):
    _, rows, cols = theirs.shape
    rt = _row_tile(rows, cols)

    def body(core_ref, a_ref, b_ref, o_ref, ob_ref):
        total = a_ref[...] + b_ref[...]
        o_ref[...] = total
        ob_ref[...] = total.astype(BF16)

    spec = pl.BlockSpec((None, rt, cols), lambda j, i, core_ref: (j, i, 0))
    mine = pl.BlockSpec((None, None, rt, cols), lambda j, i, core_ref: (j, core_ref[0], i, 0))
    return pl.pallas_call(
        body, name=name,
        grid_spec=pltpu.PrefetchScalarGridSpec(
            num_scalar_prefetch=1, grid=(4, rows // rt), in_specs=[mine, spec], out_specs=[spec, spec]),
        out_shape=[jax.ShapeDtypeStruct(theirs.shape, F32), jax.ShapeDtypeStruct(theirs.shape, BF16)],
        compiler_params=_params(dimension_semantics=("arbitrary", "arbitrary")),
    )(core, grads.reshape(4, 2, rows, cols), theirs)


def _adamw(w, g, m, v):
    m = ADAM_B1 * m + (1.0 - ADAM_B1) * g
    v = ADAM_B2 * v + (1.0 - ADAM_B2) * (g * g)
    m_hat = m / (1.0 - ADAM_B1 ** ADAM_STEP)
    v_hat = v / (1.0 - ADAM_B2 ** ADAM_STEP)
    delta = -ADAM_LR * (m_hat / (jnp.sqrt(v_hat) + ADAM_EPS) + ADAM_WD * w)
    return delta, m, v


def _sum_adamw(chip, sums, others, w, m, v, name):
    _, rows, cols = sums.shape
    rt = _row_tile(rows, cols)

    def body(chip_ref, own_ref, oth_ref, w_ref, m_ref, v_ref, g_ref, d_ref, nm_ref, nv_ref):
        g = ((own_ref[...] + oth_ref[0].astype(F32)) + oth_ref[1].astype(F32)) + oth_ref[2].astype(F32)
        g_ref[...] = g
        d_ref[...], nm_ref[...], nv_ref[...] = _adamw(w_ref[...], g, m_ref[...], v_ref[...])

    spec = pl.BlockSpec((rt, cols), lambda i, chip_ref: (i, 0))
    own = pl.BlockSpec((None, rt, cols), lambda i, chip_ref: (chip_ref[0], i, 0))
    shape = jax.ShapeDtypeStruct((rows, cols), F32)
    return pl.pallas_call(
        body, name=name,
        grid_spec=pltpu.PrefetchScalarGridSpec(
            num_scalar_prefetch=1, grid=(rows // rt,),
            in_specs=[own, pl.BlockSpec((3, rt, cols), lambda i, chip_ref: (0, i, 0)), spec, spec, spec],
            out_specs=[spec] * 4),
        out_shape=[shape] * 4,
        compiler_params=_params(dimension_semantics=("arbitrary",)),
    )(chip, sums, others, w, m, v)


def _small_sum_adamw(gathered, ws, ms, vs):
    n = len(ws)

    def body(*refs):
        g8 = refs[:n + 1]
        w, m, v = refs[n + 1:2 * n + 1], refs[2 * n + 1:3 * n + 1], refs[3 * n + 1:4 * n + 1]
        outs = refs[4 * n + 1:]
        g_out, d_out, m_out, v_out = outs[:n + 1], outs[n + 1:2 * n + 1], outs[2 * n + 1:3 * n + 1], outs[3 * n + 1:]
        for k in range(n + 1):
            g = g8[k][0]
            for b in range(1, N_DEV):
                g = g + g8[k][b]
            g_out[k][...] = g
            if k < n:
                d_out[k][...], m_out[k][...], v_out[k][...] = _adamw(w[k][...], g, m[k][...], v[k][...])

    shapes = [jax.ShapeDtypeStruct(w.shape, F32) for w in ws]
    loss_shape = jax.ShapeDtypeStruct(gathered[-1].shape[1:], F32)
    outs = pl.pallas_call(
        body, name="small_sum_adamw",
        in_specs=[VMEM_FULL] * (4 * n + 1), out_specs=[VMEM_FULL] * (4 * n + 1),
        out_shape=shapes + [loss_shape] + shapes * 3,
        compiler_params=_params(),
    )(*gathered, *ws, *ms, *vs)
    return outs[:n + 1], outs[n + 1:2 * n + 1], outs[2 * n + 1:3 * n + 1], outs[3 * n + 1:]


SMALL_NAMES = ("pool_scale", "sgu_ln_g", "sgu_ln_b", "sgu_w", "sgu_b", "ln_g", "ln_b", "ple_gate_b")


TOK_TILE = 256
GRAD_TILE = 1024


def _weight_views():
    cols = lambda width: (lambda ref, b: ref.at[:, pl.ds(pl.multiple_of(b * width, 128), width)])
    rows = lambda height: (lambda ref, b: ref.at[pl.ds(pl.multiple_of(b * height, 16), height), :])
    pool_rows = POOL_GROUP // N_DEV
    return {"w_in": cols(D_IN // N_DEV),
            "pool_w": lambda ref, b: ref.at[:, pl.ds(pl.multiple_of(b * pool_rows, 16), pool_rows), :],
            "w_out": rows(D_MIX // N_DEV), "ple_w": cols(D_MODEL // N_DEV), "gate_w": rows(D_MODEL // N_DEV)}


WEIGHT_SHAPES = {"w_in": (D_MODEL, D_IN), "pool_w": (len(POOL_WINDOWS), POOL_GROUP, POOL_GROUP),
                 "w_out": (D_MIX, D_MODEL), "ple_w": (D_PLE, D_MODEL), "gate_w": (D_MODEL, D_MODEL)}


def _weight_gather(shards, names):
    views = _weight_views()
    return _gather_side([shards[nm] for nm in names],
                        [jax.ShapeDtypeStruct(WEIGHT_SHAPES[nm], BF16) for nm in names], [views[nm] for nm in names])


def kernel(x, p, w_in, pool_w, pool_scale, sgu_ln_g, sgu_ln_b, sgu_w, sgu_b, w_out, ln_g, ln_b, ple_w, ple_gate_w, ple_gate_b, loss_target, m_w_in, m_pool_w, m_pool_scale, m_sgu_ln_g, m_sgu_ln_b, m_sgu_w, m_sgu_b, m_w_out, m_ln_g, m_ln_b, m_ple_w, m_ple_gate_w, m_ple_gate_b, v_w_in, v_pool_w, v_pool_scale, v_sgu_ln_g, v_sgu_ln_b, v_sgu_w, v_sgu_b, v_w_out, v_ln_g, v_ln_b, v_ple_w, v_ple_gate_w, v_ple_gate_b):
    seq = x.shape[1]
    x2, p2, target = x[0], p[0, 0], loss_target[0]
    core = lax.axis_index("c").astype(jnp.int32).reshape(1)
    chip = (2 * lax.axis_index("x") + lax.axis_index("y")).astype(jnp.int32).reshape(1)
    pool_rows = POOL_GROUP // N_DEV

    shards = {"w_in": w_in[0].astype(BF16), "pool_w": pool_w[0].astype(BF16), "w_out": w_out[0].astype(BF16),
              "ple_w": ple_w[0].astype(BF16), "gate_w": ple_gate_w[0].astype(BF16)}
    w_in_f, pool_w_f = _comm_call(_weight_gather(shards, ("w_in", "pool_w")), "gather_mixer_weights")
    bias_tile = jnp.repeat(sgu_b[0].T, HEAD, axis=1)
    (h, y, pooled, xb), (w_out_f, ple_w_f, gate_w_f) = _forward_mixers(
        x2, w_in_f, pool_w_f, pool_scale, sgu_ln_g, sgu_ln_b, sgu_w[0], bias_tile, TOK_TILE,
        sides=(_weight_gather(shards, ("w_out", "ple_w", "gate_w")),))

    dy, dxr, xn, dgp, dpe, dr, loss, d_ln_g, d_ln_b, d_gate_b = _head_fwd_bwd(
        x2, y, p2, target, w_out_f, gate_w_f, ple_w_f, ln_g, ln_b, ple_gate_b, 2 * TOK_TILE, TOK_TILE)

    d_w_out, _ = _weight_grad(y, dr, 1, 1, GRAD_TILE, "grad_w_out")
    d_ple_w, _ = _weight_grad(p2, dpe, N_DEV, N_DEV, GRAD_TILE, "grad_ple_w")
    d_gate_w, _ = _weight_grad(xn, dgp, 1, 1, GRAD_TILE, "grad_gate_w")
    early_names = ("w_out", "ple_w", "gate_w")
    early = [d_w_out.reshape(N_DEV, D_MIX // N_DEV, D_MODEL), d_ple_w,
             d_gate_w.reshape(N_DEV, D_MODEL // N_DEV, D_MODEL)]
    early_theirs = _comm_call(_pair_side(early), "pair_exchange_head")
    early_sums = [_pair_sum(core, g, t, "pair_sum_" + nm) for g, t, nm in zip(early, early_theirs, early_names)]

    dh, dx, d_pool_w, d_pool_scale, d_sgu_ln_g, d_sgu_ln_b, d_sgu_w, d_sgu_b = _mixers_bwd(
        h, dy, dxr, pooled, w_in_f, pool_w_f, pool_scale, sgu_ln_g, sgu_ln_b, sgu_w[0], bias_tile, TOK_TILE)

    pool_blocks = d_pool_w.reshape(N_DEV, len(POOL_WINDOWS) * pool_rows, POOL_GROUP)
    (pool_theirs,) = _comm_call(_pair_side([pool_blocks]), "pair_exchange_pool")
    hosted_names = early_names + ("pool_w",)
    hosted_sums = early_sums + [_pair_sum(core, pool_blocks, pool_theirs, "pair_sum_pool_w")]

    small = (d_pool_scale, d_sgu_ln_g, d_sgu_ln_b, d_sgu_w, d_sgu_b, d_ln_g, d_ln_b, d_gate_b)
    parts = small + (loss,)
    small_gather = _gather_side(parts, [jax.ShapeDtypeStruct((N_DEV,) + a.shape, F32) for a in parts],
                                [lambda ref, b: ref.at[b]] * len(parts))
    cx, cy = lax.axis_index("x"), lax.axis_index("y")
    order = jnp.stack([2 * px + py for px, py in _other_chips(cx, cy) + [(cx, cy)]]).astype(jnp.int32)
    w_in_own, w_in_others, side_out = _grad_w_in_reduced(
        order, xb, dh, GRAD_TILE, sides=(_chip_side([s_bf for _, s_bf in hosted_sums]), small_gather))
    hosted_others, gathered = side_out[:len(hosted_sums)], side_out[len(hosted_sums):]

    shard_of = {"w_in": (w_in, m_w_in, v_w_in), "pool_w": (pool_w, m_pool_w, v_pool_w),
                "w_out": (w_out, m_w_out, v_w_out), "ple_w": (ple_w, m_ple_w, v_ple_w),
                "gate_w": (ple_gate_w, m_ple_gate_w, v_ple_gate_w)}
    reduced = [(nm, chip, s_f32, oth) for nm, (s_f32, _), oth in zip(hosted_names, hosted_sums, hosted_others)]
    reduced.append(("w_in", jnp.zeros((1,), jnp.int32), w_in_own[None], w_in_others))
    big_out = {}
    for nm, which, s_f32, oth in reduced:
        w, m, v = shard_of[nm]
        two_d = s_f32.shape[1:]
        res = _sum_adamw(which, s_f32, oth, w.reshape(two_d), m.reshape(two_d), v.reshape(two_d), "adamw_" + nm)
        big_out[nm] = [r.reshape(w.shape) for r in res]

    small_w = (pool_scale, sgu_ln_g, sgu_ln_b, sgu_w, sgu_b, ln_g, ln_b, ple_gate_b)
    small_m = (m_pool_scale, m_sgu_ln_g, m_sgu_ln_b, m_sgu_w, m_sgu_b, m_ln_g, m_ln_b, m_ple_gate_b)
    small_v = (v_pool_scale, v_sgu_ln_g, v_sgu_ln_b, v_sgu_w, v_sgu_b, v_ln_g, v_ln_b, v_ple_gate_b)
    natural = [[a.reshape(g.shape) for a, g in zip(group, small)] for group in (small_w, small_m, small_v)]
    res = _small_sum_adamw(list(gathered), *natural)
    g_s, d_s, m_s, v_s = [[r.reshape(w.shape) for r, w in zip(kind, small_w)] for kind in res]
    total_loss = res[0][-1][0, 0]

    order = ("w_in", "pool_w", "pool_scale", "sgu_ln_g", "sgu_ln_b", "sgu_w", "sgu_b", "w_out", "ln_g", "ln_b",
             "ple_w", "ple_gate_w", "ple_gate_b")
    outs = [total_loss, dx.reshape(1, seq, D_MODEL)]
    for kind in range(4):
        for nm in order:
            key = "gate_w" if nm == "ple_gate_w" else nm
            if key in big_out:
                outs.append(big_out[key][kind])
            else:
                outs.append((g_s, d_s, m_s, v_s)[kind][SMALL_NAMES.index(nm)])
    return tuple(outs)
```

```python
from typing import Callable, NamedTuple

import numpy as np
import jax
import jax.numpy as jnp
from jax import lax
from jax.experimental import pallas as pl
from jax.experimental.pallas import tpu as pltpu

F32 = jnp.float32
BF16 = jnp.bfloat16

N_DEV = 8
D_MODEL = 1024
D_POOL = 1024
D_SGU = 1024
D_MIX = 2048
D_IN = 5120
D_PLE = 256
POOL_WINDOWS = (2, 4, 8, 16)
POOL_GROUP = 256
N_HEADS = 4
HEAD = 256
CHUNK = 128
HALO = 16
BAND_PAD = 128
ALPHA = 2.0 ** 0.25
LN_EPS = 1e-5
ADAM_LR, ADAM_B1, ADAM_B2, ADAM_EPS, ADAM_WD, ADAM_STEP = 0.001, 0.9, 0.999, 1e-08, 0.01, 10

U0, V0, Z0 = D_POOL, D_POOL + D_SGU, D_POOL + 2 * D_SGU
VMEM_LIMIT = 56 * 1024 * 1024
MESH = pl.DeviceIdType.MESH
ANY = pl.BlockSpec(memory_space=pl.ANY)
VMEM_FULL = pl.BlockSpec(memory_space=pltpu.VMEM)

_GELU_C0 = 0.7978845608028654
_GELU_C1 = 0.044715


def _gelu_cdf(x, x2):
    return 1.0 / (1.0 + jnp.exp(x * ((-2.0 * _GELU_C0) + (-2.0 * _GELU_C0 * _GELU_C1) * x2)))


def _gelu_and_grad(x):
    x2 = x * x
    cdf = _gelu_cdf(x, x2)
    g = x * cdf
    dg = cdf + g * (1.0 - cdf) * ((2.0 * _GELU_C0) + (6.0 * _GELU_C0 * _GELU_C1) * x2)
    return g, dg


def _gelu(x):
    t = jnp.tanh(_GELU_C0 * (x + _GELU_C1 * (x * x * x)))
    return x * (0.5 * (1.0 + t))


def _split_bf16(x):
    hi = x.astype(BF16)
    return hi, (x - hi.astype(F32)).astype(BF16)


def _band(tok_tile, window, transpose):
    t = np.arange(tok_tile)[:, None]
    s = np.arange(tok_tile + BAND_PAD)[None, :]
    d = (s - t) if transpose else (t + BAND_PAD - s)
    return ((d >= 0) & (d < window)).astype(np.float32)


def _bands(tok_tile, transpose):
    return jnp.asarray(np.stack([_band(tok_tile, w, transpose) for w in POOL_WINDOWS]), dtype=BF16)


def _sigmoid(x):
    return 1.0 / (1.0 + jnp.exp(-x))


def _dot(a, b):
    return jnp.dot(a, b, preferred_element_type=F32)


def _dot_nt(a, b):
    return lax.dot_general(a, b, (((1,), (1,)), ((), ())), preferred_element_type=F32)


def _dot_tn(a, b):
    return lax.dot_general(a, b, (((0,), (0,)), ((), ())), preferred_element_type=F32)


def _row_stats(x):
    mu = jnp.mean(x, axis=-1, keepdims=True)
    xc = x - mu
    var = jnp.mean(xc * xc, axis=-1, keepdims=True)
    rstd = lax.rsqrt(var + LN_EPS)
    return xc * rstd, rstd


def _ln_bwd(dxhat, xhat, rstd):
    m1 = jnp.mean(dxhat, axis=-1, keepdims=True)
    m2 = jnp.mean(dxhat * xhat, axis=-1, keepdims=True)
    return rstd * (dxhat - m1 - xhat * m2)


def _masked_sgu_w(sw_ref, hh):
    row = lax.broadcasted_iota(jnp.int32, (CHUNK, CHUNK), 0)
    col = lax.broadcasted_iota(jnp.int32, (CHUNK, CHUNK), 1)
    return jnp.where(row >= col, sw_ref[hh], 0.0)


def _inv_count(tile_index, tok_tile, window):
    tok = tile_index * tok_tile + lax.broadcasted_iota(jnp.int32, (tok_tile, 1), 0)
    return 1.0 / jnp.minimum(tok + 1, window).astype(F32)


def _params(**kw):
    return pltpu.CompilerParams(vmem_limit_bytes=VMEM_LIMIT, **kw)


def _forward_mixers(x, w_in, pool_w, pool_scale, sgu_ln_g, sgu_ln_b, sgu_w, sgu_bias_tile, tok_tile, sides=()):
    seq = x.shape[0]
    n_tiles = seq // tok_tile
    n_chunks = tok_tile // CHUNK
    side_refs = _SideRefs(sides, 8, 4, 2)

    def body(*refs):
        (x_ref, win_ref, pw_ref, ps_ref, lg_ref, lb_ref, sw_ref, sb_ref,
         hb_ref, y_ref, pooled_ref, xb_ref, aext_ref, h_ref) = side_refs.split(refs)
        i = pl.program_id(0)
        side_refs.emit(i == 0, i == n_tiles // 2, False)

        @pl.when(i == 0)
        def _():
            aext_ref[0:HALO, :] = jnp.zeros((HALO, D_POOL), F32)

        xb = x_ref[...].astype(BF16)
        xb_ref[...] = xb
        for s in range(D_IN // 1024):
            cs = slice(s * 1024, (s + 1) * 1024)
            section = _dot(xb, win_ref[:, cs])
            h_ref[:, cs] = section
            if s >= 1:
                hb_ref[:, (s - 1) * 1024:s * 1024] = section.astype(BF16)

        aext_ref[HALO:HALO + tok_tile, :] = h_ref[:, 0:D_POOL]
        for g, window in enumerate(POOL_WINDOWS):
            cols = slice(g * POOL_GROUP, (g + 1) * POOL_GROUP)
            win = aext_ref[HALO:HALO + tok_tile, cols]
            for k in range(1, window):
                win = win + aext_ref[HALO - k:HALO - k + tok_tile, cols]
            pooled = win * _inv_count(i, tok_tile, window) - h_ref[:, cols]
            pb = pooled.astype(BF16)
            pooled_ref[:, cols] = pb
            mixed = _dot(pb, pw_ref[g])
            z = h_ref[:, Z0 + g * POOL_GROUP:Z0 + (g + 1) * POOL_GROUP]
            y_ref[:, cols] = (mixed * ps_ref[:, cols] * (z * _sigmoid(z))).astype(BF16)
        aext_ref[0:HALO, :] = aext_ref[tok_tile:tok_tile + HALO, :]

        for hh in range(N_HEADS):
            cols = slice(hh * HEAD, (hh + 1) * HEAD)
            swm = _masked_sgu_w(sw_ref, hh).astype(BF16)
            for n in range(n_chunks):
                rows = slice(n * CHUNK, (n + 1) * CHUNK)
                gu = _gelu(h_ref[rows, U0 + hh * HEAD:U0 + (hh + 1) * HEAD])
                gv = _gelu(h_ref[rows, V0 + hh * HEAD:V0 + (hh + 1) * HEAD])
                xhat, _ = _row_stats(gv)
                vln = xhat * lg_ref[:, cols] + lb_ref[:, cols]
                sv = _dot(swm, vln.astype(BF16)) + sb_ref[:, cols]
                z = h_ref[rows, Z0 + D_POOL + hh * HEAD:Z0 + D_POOL + (hh + 1) * HEAD]
                y_ref[rows, D_POOL + hh * HEAD:D_POOL + (hh + 1) * HEAD] = (
                    gu * sv * (z * _sigmoid(z))).astype(BF16)

        side_refs.emit(False, False, i == n_tiles - 1)

    tok = lambda width: pl.BlockSpec((tok_tile, width), lambda i: (i, 0))
    outs = pl.pallas_call(
        body, name="forward_mixers",
        grid=(n_tiles,),
        in_specs=[tok(D_MODEL)] + [VMEM_FULL] * 7 + side_refs.in_specs,
        out_specs=[tok(D_IN - D_POOL), tok(D_MIX), tok(D_POOL), tok(D_MODEL)] + side_refs.out_specs,
        out_shape=[jax.ShapeDtypeStruct((seq, D_IN - D_POOL), BF16), jax.ShapeDtypeStruct((seq, D_MIX), BF16),
                   jax.ShapeDtypeStruct((seq, D_POOL), BF16), jax.ShapeDtypeStruct((seq, D_MODEL), BF16)]
        + side_refs.out_shapes,
        scratch_shapes=[pltpu.VMEM((HALO + tok_tile, D_POOL), F32), pltpu.VMEM((tok_tile, D_IN), F32)]
        + side_refs.scratch,
        compiler_params=_params(dimension_semantics=("arbitrary",)),
    )(x, w_in, pool_w, pool_scale, sgu_ln_g, sgu_ln_b, sgu_w, sgu_bias_tile, *side_refs.inputs)
    return outs[:4], outs[4:]


def _head_fwd_bwd(x, y, p, target, w_out, gate_w, ple_w, ln_g, ln_b, gate_b, tok_tile, sub_tile):
    seq = x.shape[0]

    def body(x_ref, y_ref, p_ref, t_ref, wout_ref, gw_ref, plw_ref, lng_ref, lnb_ref, gb_ref,
             dy_ref, dxr_ref, xn_ref, dgp_ref, dpe_ref, dr_ref, loss_ref, dlng_ref, dlnb_ref, dgb_ref):
        i = pl.program_id(0)

        @pl.when(i == 0)
        def _():
            loss_ref[...] = jnp.zeros_like(loss_ref)
            dlng_ref[...] = jnp.zeros_like(dlng_ref)
            dlnb_ref[...] = jnp.zeros_like(dlnb_ref)
            dgb_ref[...] = jnp.zeros_like(dgb_ref)

        subs = [slice(s * sub_tile, (s + 1) * sub_tile) for s in range(tok_tile // sub_tile)]
        stats, xns, douts = [], [], []
        for rows in subs:
            r = ALPHA * x_ref[rows, :] + _dot(y_ref[rows, :], wout_ref[...])
            xhat, rstd = _row_stats(r)
            xn = xhat * lng_ref[...] + lnb_ref[...]
            xn_ref[rows, :] = xn.astype(BF16)
            stats.append((xhat, rstd))
            xns.append(xn)
        loss = jnp.zeros((1, 1), F32)
        dgb = jnp.zeros((1, D_MODEL), F32)
        for rows, xn in zip(subs, xns):
            gate = _sigmoid(_dot(xn_ref[rows, :], gw_ref[...]) + gb_ref[...])
            pe = _dot(p_ref[rows, :].astype(BF16), plw_ref[...])
            err = xn + gate * pe - t_ref[rows, :]
            loss = loss + jnp.sum(err * err, keepdims=True)
            dout = err * (1.0 / D_MODEL)
            dpe_ref[rows, :] = (dout * gate).astype(BF16)
            dgpre = dout * pe * gate * (1.0 - gate)
            dgb = dgb + jnp.sum(dgpre, axis=0, keepdims=True)
            dgp_ref[rows, :] = dgpre.astype(BF16)
            douts.append(dout)
        loss_ref[...] += (0.5 / D_MODEL) * loss
        dgb_ref[...] += dgb
        dlng = jnp.zeros((1, D_MODEL), F32)
        dlnb = jnp.zeros((1, D_MODEL), F32)
        for rows, (xhat, rstd), dout in zip(subs, stats, douts):
            dxn = dout + _dot_nt(dgp_ref[rows, :], gw_ref[...])
            dlng = dlng + jnp.sum(dxn * xhat, axis=0, keepdims=True)
            dlnb = dlnb + jnp.sum(dxn, axis=0, keepdims=True)
            dr = _ln_bwd(dxn * lng_ref[...], xhat, rstd)
            dxr_ref[rows, :] = ALPHA * dr
            dr_ref[rows, :] = dr.astype(BF16)
        dlng_ref[...] += dlng
        dlnb_ref[...] += dlnb
        for rows in subs:
            dy_ref[rows, :] = _dot_nt(dr_ref[rows, :], wout_ref[...])

    tok = lambda width: pl.BlockSpec((tok_tile, width), lambda i: (i, 0))
    acc = lambda width: pl.BlockSpec((1, width), lambda i: (0, 0))
    vec = jax.ShapeDtypeStruct((1, D_MODEL), F32)
    return pl.pallas_call(
        body, name="head_fwd_bwd",
        grid=(seq // tok_tile,),
        in_specs=[tok(D_MODEL), tok(D_MIX), tok(D_PLE), tok(D_MODEL),
                  VMEM_FULL, VMEM_FULL, VMEM_FULL, VMEM_FULL, VMEM_FULL, VMEM_FULL],
        out_specs=[tok(D_MIX), tok(D_MODEL), tok(D_MODEL), tok(D_MODEL), tok(D_MODEL), tok(D_MODEL),
                   acc(128), acc(D_MODEL), acc(D_MODEL), acc(D_MODEL)],
        out_shape=[jax.ShapeDtypeStruct((seq, D_MIX), F32), jax.ShapeDtypeStruct((seq, D_MODEL), F32),
                   jax.ShapeDtypeStruct((seq, D_MODEL), BF16), jax.ShapeDtypeStruct((seq, D_MODEL), BF16),
                   jax.ShapeDtypeStruct((seq, D_MODEL), BF16), jax.ShapeDtypeStruct((seq, D_MODEL), BF16),
                   jax.ShapeDtypeStruct((1, 128), F32), vec, vec, vec],
        compiler_params=_params(dimension_semantics=("arbitrary",)),
    )(x, y, p, target, w_out, gate_w, ple_w, ln_g, ln_b, gate_b)


def _mixers_bwd(h, dy, dxr, pooled, w_in, pool_w, pool_scale, sgu_ln_g, sgu_ln_b, sgu_w, sgu_bias_tile, tok_tile):
    seq = h.shape[0]
    n_tiles = seq // tok_tile
    n_chunks = tok_tile // CHUNK
    pool_rows = POOL_GROUP // N_DEV

    def body(h_ref, dy_ref, dxr_ref, pooled_ref, win_ref, pw_ref, ps_ref, lg_ref, lb_ref, sw_ref, sb_ref, band_ref,
             dh_ref, dx_ref, dpw_ref, dps_ref, dlg_ref, dlb_ref, dsw_ref, dsb_ref,
             qhi_ref, qlo_ref, dpw_acc, dsb_acc):
        i = pl.program_id(0)
        tile = n_tiles - 1 - i

        @pl.when(i == 0)
        def _():
            qhi_ref[...] = jnp.zeros_like(qhi_ref)
            qlo_ref[...] = jnp.zeros_like(qlo_ref)
            dpw_acc[...] = jnp.zeros_like(dpw_acc)
            dsb_acc[...] = jnp.zeros_like(dsb_acc)
            dps_ref[...] = jnp.zeros_like(dps_ref)
            dlg_ref[...] = jnp.zeros_like(dlg_ref)
            dlb_ref[...] = jnp.zeros_like(dlb_ref)
            dsw_ref[...] = jnp.zeros_like(dsw_ref)

        def h_at(rows, cols):
            return h_ref[rows, cols.start - D_POOL:cols.stop - D_POOL].astype(F32)

        everything = slice(0, tok_tile)
        for part in (qhi_ref, qlo_ref):
            part[tok_tile:tok_tile + HALO, :] = part[0:HALO, :]
        for g, window in enumerate(POOL_WINDOWS):
            cols = slice(g * POOL_GROUP, (g + 1) * POOL_GROUP)
            zcols = slice(Z0 + g * POOL_GROUP, Z0 + (g + 1) * POOL_GROUP)
            z = h_at(everything, zcols)
            sz = _sigmoid(z)
            pb = pooled_ref[:, cols]
            mixed = _dot(pb, pw_ref[g])
            dyp = dy_ref[:, cols]
            dh_ref[:, zcols] = (dyp * (mixed * ps_ref[:, cols]) * (sz * (1.0 + z * (1.0 - sz)))).astype(BF16)
            dms = dyp * (z * sz)
            dps_ref[:, cols] += jnp.sum(dms * mixed, axis=0, keepdims=True)
            dmixed = (dms * ps_ref[:, cols]).astype(BF16)
            dpw_acc[g] += _dot_tn(pb, dmixed)
            dpooled = _dot_nt(dmixed, pw_ref[g])
            qhi_ref[everything, cols], qlo_ref[everything, cols] = _split_bf16(
                dpooled * _inv_count(tile, tok_tile, window))
            da = _dot(band_ref[g], qhi_ref[:, cols]) + _dot(band_ref[g], qlo_ref[:, cols]) - dpooled
            dh_ref[:, cols] = da.astype(BF16)

        for hh in range(N_HEADS):
            cols = slice(hh * HEAD, (hh + 1) * HEAD)
            ucols = slice(U0 + hh * HEAD, U0 + (hh + 1) * HEAD)
            vcols = slice(V0 + hh * HEAD, V0 + (hh + 1) * HEAD)
            zcols = slice(Z0 + D_POOL + hh * HEAD, Z0 + D_POOL + (hh + 1) * HEAD)
            sw32 = _masked_sgu_w(sw_ref, hh)
            swm = sw32.astype(BF16)
            swm_t = sw32.T.astype(BF16)
            for n in range(n_chunks):
                rows = slice(n * CHUNK, (n + 1) * CHUNK)
                gu, dgu_du = _gelu_and_grad(h_at(rows, ucols))
                gv, dgv_dv = _gelu_and_grad(h_at(rows, vcols))
                xhat, rstd = _row_stats(gv)
                vb = (xhat * lg_ref[:, cols] + lb_ref[:, cols]).astype(BF16)
                sv = _dot(swm, vb) + sb_ref[:, cols]
                z = h_at(rows, zcols)
                sz = _sigmoid(z)
                dys = dy_ref[rows, D_POOL + hh * HEAD:D_POOL + (hh + 1) * HEAD]
                dh_ref[rows, zcols] = (dys * (gu * sv) * (sz * (1.0 + z * (1.0 - sz)))).astype(BF16)
                dyg = dys * (z * sz)
                dh_ref[rows, ucols] = (dyg * sv * dgu_du).astype(BF16)
                dsv = dyg * gu
                dsb_acc[:, cols] += dsv
                dsvb = dsv.astype(BF16)
                dsw_ref[hh] += _dot_nt(dsvb, vb)
                dvln = _dot(swm_t, dsvb)
                dlg_ref[:, cols] += jnp.sum(dvln * xhat, axis=0, keepdims=True)
                dlb_ref[:, cols] += jnp.sum(dvln, axis=0, keepdims=True)
                dgv = _ln_bwd(dvln * lg_ref[:, cols], xhat, rstd)
                dh_ref[rows, vcols] = (dgv * dgv_dv).astype(BF16)

        dx_ref[...] = dxr_ref[...] + _dot_nt(dh_ref[...], win_ref[...])

        @pl.when(i == n_tiles - 1)
        def _():
            for g in range(len(POOL_WINDOWS)):
                for b in range(N_DEV):
                    dpw_ref[b, g] = dpw_acc[g, b * pool_rows:(b + 1) * pool_rows, :]
            row = lax.broadcasted_iota(jnp.int32, (CHUNK, CHUNK), 0)
            col = lax.broadcasted_iota(jnp.int32, (CHUNK, CHUNK), 1)
            for hh in range(N_HEADS):
                dsw_ref[hh] = jnp.where(row >= col, dsw_ref[hh], 0.0)
                total = jnp.sum(dsb_acc[:, hh * HEAD:(hh + 1) * HEAD], axis=1, keepdims=True)
                dsb_ref[hh:hh + 1, :] = jnp.broadcast_to(total, (CHUNK, CHUNK)).T[0:1, :]

    tok = lambda width: pl.BlockSpec((tok_tile, width), lambda i: (n_tiles - 1 - i, 0))
    whole = lambda shape: pl.BlockSpec(shape, lambda i: (0,) * len(shape))
    vec = jax.ShapeDtypeStruct((1, D_MODEL), F32)
    return pl.pallas_call(
        body, name="mixers_bwd",
        grid=(n_tiles,),
        in_specs=[tok(D_IN - D_POOL), tok(D_MIX), tok(D_MODEL), tok(D_POOL)] + [VMEM_FULL] * 8,
        out_specs=[tok(D_IN), tok(D_MODEL), whole((N_DEV, len(POOL_WINDOWS), pool_rows, POOL_GROUP)),
                   whole((1, D_POOL)), whole((1, D_SGU)), whole((1, D_SGU)),
                   whole((N_HEADS, CHUNK, CHUNK)), whole((N_HEADS, CHUNK))],
        out_shape=[jax.ShapeDtypeStruct((seq, D_IN), BF16), jax.ShapeDtypeStruct((seq, D_MODEL), F32),
                   jax.ShapeDtypeStruct((N_DEV, len(POOL_WINDOWS), pool_rows, POOL_GROUP), F32),
                   vec, vec, vec,
                   jax.ShapeDtypeStruct((N_HEADS, CHUNK, CHUNK), F32),
                   jax.ShapeDtypeStruct((N_HEADS, CHUNK), F32)],
        scratch_shapes=[pltpu.VMEM((tok_tile + BAND_PAD, D_POOL), BF16),
                        pltpu.VMEM((tok_tile + BAND_PAD, D_POOL), BF16),
                        pltpu.VMEM((len(POOL_WINDOWS), POOL_GROUP, POOL_GROUP), F32),
                        pltpu.VMEM((CHUNK, D_SGU), F32)],
        compiler_params=_params(dimension_semantics=("arbitrary",)),
    )(h, dy, dxr, pooled, w_in, pool_w, pool_scale, sgu_ln_g, sgu_ln_b, sgu_w, sgu_bias_tile, _bands(tok_tile, True))


def _weight_grad(a, b, n_col_blocks, blocks_per_step, tok_tile, name, sides=()):
    seq, m = a.shape
    n = b.shape[1]
    nb = n // n_col_blocks
    n_steps = n_col_blocks // blocks_per_step
    n_k = seq // tok_tile
    side_refs = _SideRefs(sides, 2, 1, 0)

    def body(*refs):
        a_ref, b_ref, out_ref = side_refs.split(refs)
        step = pl.program_id(0) * n_k + pl.program_id(1)
        side_refs.emit(step == 0, step == (n_steps * n_k) // 2, False)

        @pl.when(pl.program_id(1) == 0)
        def _():
            out_ref[...] = jnp.zeros_like(out_ref)

        res = _dot_tn(a_ref[...].astype(BF16), b_ref[...])
        for blk in range(blocks_per_step):
            out_ref[blk] += res[:, blk * nb:(blk + 1) * nb]

        side_refs.emit(False, False, step == n_steps * n_k - 1)

    outs = pl.pallas_call(
        body, name=name,
        grid=(n_steps, n_k),
        in_specs=[pl.BlockSpec((tok_tile, m), lambda j, k: (k, 0)),
                  pl.BlockSpec((tok_tile, blocks_per_step * nb), lambda j, k: (k, j))] + side_refs.in_specs,
        out_specs=[pl.BlockSpec((blocks_per_step, m, nb), lambda j, k: (j, 0, 0))] + side_refs.out_specs,
        out_shape=[jax.ShapeDtypeStruct((n_col_blocks, m, nb), F32)] + side_refs.out_shapes,
        scratch_shapes=side_refs.scratch,
        compiler_params=_params(dimension_semantics=("arbitrary", "arbitrary")),
    )(a, b, *side_refs.inputs)
    return outs[0], outs[1:]


class _Side(NamedTuple):
    inputs: list
    out_shapes: list
    sem_shapes: list
    emit: Callable


def _when(cond):
    if cond is True:
        return lambda f: f()
    if cond is False:
        return lambda f: None
    return pl.when(cond)


def _place():
    return lax.axis_index("x"), lax.axis_index("y"), lax.axis_index("c")


def _other_chips(x, y):
    return [(1 - x, y), (x, 1 - y), (1 - x, 1 - y)]


def _gather_side(shards, out_shapes, views):
    n = len(shards)

    def emit(ins, outs, sems, first, mid, last):
        send_sems, recv_sems, local_sems = sems
        x, y, c = _place()
        me, sibling = (x, y, c), (x, y, 1 - c)
        chips = _other_chips(x, y)

        def copy(k, s, block, to, src=None):
            px, py, pc = block
            dst = views[k](outs[k], 4 * px + 2 * py + pc)
            return pltpu.make_async_remote_copy(
                src_ref=dst if src is None else src, dst_ref=dst,
                send_sem=send_sems.at[k, s], recv_sem=recv_sems.at[k, s],
                device_id=to, device_id_type=MESH)

        def own(k):
            return pltpu.make_async_copy(ins[k], views[k](outs[k], 4 * x + 2 * y + c), local_sems.at[k])

        def sends(k):
            return [copy(k, 0, me, sibling, src=ins[k])] + [
                copy(k, 1 + j, me, (*chip, c), src=ins[k]) for j, chip in enumerate(chips)]

        def passed_on(k, j):
            return copy(k, 4 + j, (*chips[j], c), sibling)

        @_when(first)
        def _():
            for k in range(n):
                own(k).start()
                for cp in sends(k):
                    cp.start()

        @_when(mid)
        def _():
            for k in range(n):
                for j, chip in enumerate(chips):
                    copy(k, 1 + j, (*chip, c), me).wait_recv()
                    passed_on(k, j).start()

        @_when(last)
        def _():
            for k in range(n):
                copy(k, 0, sibling, me).wait_recv()
                for j, chip in enumerate(chips):
                    copy(k, 4 + j, (*chip, 1 - c), me).wait_recv()
            for k in range(n):
                for cp in sends(k) + [passed_on(k, j) for j in range(len(chips))]:
                    cp.wait_send()
                own(k).wait()

    sems = [pltpu.SemaphoreType.DMA((n, 7)), pltpu.SemaphoreType.DMA((n, 7)), pltpu.SemaphoreType.DMA((n,))]
    return _Side(list(shards), list(out_shapes), sems, emit)


def _pair_side(grads):
    n = len(grads)

    def emit(ins, theirs, sems, first, mid, last):
        send_sems, recv_sems = sems
        x, y, c = _place()

        def copies():
            return [pltpu.make_async_remote_copy(
                src_ref=ins[k].at[2 * j + (1 - c)], dst_ref=theirs[k].at[j],
                send_sem=send_sems.at[k, j], recv_sem=recv_sems.at[k, j],
                device_id=(x, y, 1 - c), device_id_type=MESH) for k in range(n) for j in range(4)]

        @_when(first)
        def _():
            for cp in copies():
                cp.start()

        @_when(last)
        def _():
            for cp in copies():
                cp.wait_recv()
            for cp in copies():
                cp.wait_send()

    shapes = [jax.ShapeDtypeStruct((4,) + g.shape[1:], g.dtype) for g in grads]
    return _Side(list(grads), shapes, [pltpu.SemaphoreType.DMA((n, 4)), pltpu.SemaphoreType.DMA((n, 4))], emit)


def _chip_side(sums):
    n = len(sums)

    def emit(ins, others, sems, first, mid, last):
        send_sems, recv_sems = sems
        x, y, c = _place()

        def copies():
            return [pltpu.make_async_remote_copy(
                src_ref=ins[k].at[2 * px + py], dst_ref=others[k].at[r],
                send_sem=send_sems.at[k, r], recv_sem=recv_sems.at[k, r],
                device_id=(px, py, c), device_id_type=MESH)
                for k in range(n) for r, (px, py) in enumerate(_other_chips(x, y))]

        @_when(first)
        def _():
            for cp in copies():
                cp.start()

        @_when(last)
        def _():
            for cp in copies():
                cp.wait_recv()
            for cp in copies():
                cp.wait_send()

    shapes = [jax.ShapeDtypeStruct((3,) + s.shape[1:], s.dtype) for s in sums]
    return _Side(list(sums), shapes, [pltpu.SemaphoreType.DMA((n, 3)), pltpu.SemaphoreType.DMA((n, 3))], emit)


def _comm_call(side, name):
    n_in, n_out = len(side.inputs), len(side.out_shapes)

    def body(*refs):
        side.emit(refs[:n_in], refs[n_in:n_in + n_out], refs[n_in + n_out:], True, True, True)

    return pl.pallas_call(
        body, name=name, in_specs=[ANY] * n_in, out_specs=[ANY] * n_out,
        out_shape=side.out_shapes, scratch_shapes=side.sem_shapes,
    )(*side.inputs)


class _SideRefs:
    def __init__(self, sides, n_in, n_out, n_scratch):
        self.sides, self.n_in, self.n_out, self.n_scratch = sides, n_in, n_out, n_scratch
        self.inputs = [a for s in sides for a in s.inputs]
        self.out_shapes = [o for s in sides for o in s.out_shapes]
        self.scratch = [m for s in sides for m in s.sem_shapes]
        self.in_specs = [ANY] * len(self.inputs)
        self.out_specs = [ANY] * len(self.out_shapes)

    def split(self, refs):
        refs = list(refs)
        n_side_in, n_side_out = len(self.inputs), len(self.out_shapes)
        ins, rest = refs[:self.n_in], refs[self.n_in:]
        side_in, rest = rest[:n_side_in], rest[n_side_in:]
        outs, rest = rest[:self.n_out], rest[self.n_out:]
        side_out, rest = rest[:n_side_out], rest[n_side_out:]
        scratch, side_sems = rest[:self.n_scratch], rest[self.n_scratch:]
        self._refs = (side_in, side_out, side_sems)
        return ins + outs + scratch

    def emit(self, first, mid, last):
        side_in, side_out, side_sems = self._refs
        for s in self.sides:
            a, b, m = len(s.inputs), len(s.out_shapes), len(s.sem_shapes)
            s.emit(side_in[:a], side_out[:b], side_sems[:m], first, mid, last)
            side_in, side_out, side_sems = side_in[a:], side_out[b:], side_sems[m:]


def _grad_w_in_reduced(order, a, b, tok_tile, sides=()):
    seq, m = a.shape
    nb = b.shape[1] // N_DEV
    n_k = seq // tok_tile
    n_rows = 4
    last_step = n_rows * n_k - 1
    assert n_k >= 3
    side_refs = _SideRefs(sides, 3, 3, 8)

    def body(*refs):
        (order_ref, a_ref, b_ref, own_ref, theirs_ref, others_ref,
         acc_ref, stage_ref, sumbf_ref, pair_send, pair_recv, ici_send, ici_recv, stage_sem) = side_refs.split(refs)
        j, k = pl.program_id(0), pl.program_id(1)
        step = j * n_k + k
        side_refs.emit(step == 0, step == (n_rows * n_k) // 2, False)
        x, y, c = _place()
        chips = _other_chips(x, y)

        def to_sibling(row):
            return pltpu.make_async_remote_copy(
                src_ref=acc_ref.at[row % 2, 1 - c], dst_ref=theirs_ref.at[row],
                send_sem=pair_send.at[row], recv_sem=pair_recv.at[row],
                device_id=(x, y, 1 - c), device_id_type=MESH)

        def to_owner(row):
            px, py = chips[row]
            return pltpu.make_async_remote_copy(
                src_ref=sumbf_ref.at[row], dst_ref=others_ref.at[row],
                send_sem=ici_send.at[row], recv_sem=ici_recv.at[row],
                device_id=(px, py, c), device_id_type=MESH)

        def staged(row):
            return pltpu.make_async_copy(theirs_ref.at[row], stage_ref, stage_sem.at[0])

        @pl.when(k == 0)
        def _():
            acc_ref[j % 2] = jnp.zeros((2, m, nb), F32)

        res = _dot_tn(a_ref[...].astype(BF16), b_ref[...])
        for blk in range(2):
            acc_ref[j % 2, blk] += res[:, blk * nb:(blk + 1) * nb]

        for row in range(n_rows):
            @pl.when((j == row) & (k == n_k - 1))
            def _():
                to_sibling(row).start()

            if row < n_rows - 1:
                @pl.when((j == row + 1) & (k == 1))
                def _():
                    to_sibling(row).wait_recv()
                    staged(row).start()

                @pl.when((j == row + 1) & (k == 2))
                def _():
                    staged(row).wait()
                    to_sibling(row).wait_send()
                    sumbf_ref[row] = (acc_ref[row % 2, c] + stage_ref[...]).astype(BF16)
                    to_owner(row).start()

        @pl.when(step == last_step)
        def _():
            row = n_rows - 1
            to_sibling(row).wait_recv()
            staged(row).start()
            staged(row).wait()
            to_sibling(row).wait_send()
            own_ref[...] = acc_ref[row % 2, c] + stage_ref[...]
            for r in range(n_rows - 1):
                to_owner(r).wait_recv()
                to_owner(r).wait_send()

        side_refs.emit(False, False, step == last_step)

    block = jax.ShapeDtypeStruct((m, nb), F32)
    outs = pl.pallas_call(
        body, name="grad_w_in",
        grid_spec=pltpu.PrefetchScalarGridSpec(
            num_scalar_prefetch=1, grid=(n_rows, n_k),
            in_specs=[pl.BlockSpec((tok_tile, m), lambda j, k, order_ref: (k, 0)),
                      pl.BlockSpec((tok_tile, 2 * nb), lambda j, k, order_ref: (k, order_ref[j]))]
            + side_refs.in_specs,
            out_specs=[pl.BlockSpec((m, nb), lambda j, k, order_ref: (0, 0)), ANY, ANY] + side_refs.out_specs,
            scratch_shapes=[pltpu.VMEM((2, 2, m, nb), F32), pltpu.VMEM((m, nb), F32),
                            pltpu.VMEM((n_rows - 1, m, nb), BF16),
                            pltpu.SemaphoreType.DMA((n_rows,)), pltpu.SemaphoreType.DMA((n_rows,)),
                            pltpu.SemaphoreType.DMA((n_rows - 1,)), pltpu.SemaphoreType.DMA((n_rows - 1,)),
                            pltpu.SemaphoreType.DMA((1,))] + side_refs.scratch),
        out_shape=[block, jax.ShapeDtypeStruct((n_rows, m, nb), F32),
                   jax.ShapeDtypeStruct((n_rows - 1, m, nb), BF16)] + side_refs.out_shapes,
        compiler_params=_params(dimension_semantics=("arbitrary", "arbitrary")),
    )(order, a, b, *side_refs.inputs)
    return outs[0], outs[2], outs[3:]


def _row_tile(rows, cols):
    tile = rows
    while tile * cols > 256 * 1024 and tile % 16 == 0:
        tile //= 2
    return tile


def _pair_sum(core, grads, theirs, name):
    _, rows, cols = theirs.shape
    rt = _row_tile(rows, cols)

    def body(core_ref, a_ref, b_ref, o_ref, ob_ref):
        total = a_ref[...] + b_ref[...]
        o_ref[...] = total
        ob_ref[...] = total.astype(BF16)

    spec = pl.BlockSpec((None, rt, cols), lambda j, i, core_ref: (j, i, 0))
    mine = pl.BlockSpec((None, None, rt, cols), lambda j, i, core_ref: (j, core_ref[0], i, 0))
    return pl.pallas_call(
        body, name=name,
        grid_spec=pltpu.PrefetchScalarGridSpec(
            num_scalar_prefetch=1, grid=(4, rows // rt), in_specs=[mine, spec], out_specs=[spec, spec]),
        out_shape=[jax.ShapeDtypeStruct(theirs.shape, F32), jax.ShapeDtypeStruct(theirs.shape, BF16)],
        compiler_params=_params(dimension_semantics=("arbitrary", "arbitrary")),
    )(core, grads.reshape(4, 2, rows, cols), theirs)


def _adamw(w, g, m, v):
    m = ADAM_B1 * m + (1.0 - ADAM_B1) * g
    v = ADAM_B2 * v + (1.0 - ADAM_B2) * (g * g)
    m_hat = m / (1.0 - ADAM_B1 ** ADAM_STEP)
    v_hat = v / (1.0 - ADAM_B2 ** ADAM_STEP)
    delta = -ADAM_LR * (m_hat / (jnp.sqrt(v_hat) + ADAM_EPS) + ADAM_WD * w)
    return delta, m, v


def _sum_adamw(chip, sums, others, w, m, v, name):
    _, rows, cols = sums.shape
    rt = _row_tile(rows, cols)

    def body(chip_ref, own_ref, oth_ref, w_ref, m_ref, v_ref, g_ref, d_ref, nm_ref, nv_ref):
        g = ((own_ref[...] + oth_ref[0].astype(F32)) + oth_ref[1].astype(F32)) + oth_ref[2].astype(F32)
        g_ref[...] = g
        d_ref[...], nm_ref[...], nv_ref[...] = _adamw(w_ref[...], g, m_ref[...], v_ref[...])

    spec = pl.BlockSpec((rt, cols), lambda i, chip_ref: (i, 0))
    own = pl.BlockSpec((None, rt, cols), lambda i, chip_ref: (chip_ref[0], i, 0))
    shape = jax.ShapeDtypeStruct((rows, cols), F32)
    return pl.pallas_call(
        body, name=name,
        grid_spec=pltpu.PrefetchScalarGridSpec(
            num_scalar_prefetch=1, grid=(rows // rt,),
            in_specs=[own, pl.BlockSpec((3, rt, cols), lambda i, chip_ref: (0, i, 0)), spec, spec, spec],
            out_specs=[spec] * 4),
        out_shape=[shape] * 4,
        compiler_params=_params(dimension_semantics=("arbitrary",)),
    )(chip, sums, others, w, m, v)


def _small_sum_adamw(gathered, ws, ms, vs):
    n = len(ws)

    def body(*refs):
        g8 = refs[:n + 1]
        w, m, v = refs[n + 1:2 * n + 1], refs[2 * n + 1:3 * n + 1], refs[3 * n + 1:4 * n + 1]
        outs = refs[4 * n + 1:]
        g_out, d_out, m_out, v_out = outs[:n + 1], outs[n + 1:2 * n + 1], outs[2 * n + 1:3 * n + 1], outs[3 * n + 1:]
        for k in range(n + 1):
            g = g8[k][0]
            for b in range(1, N_DEV):
                g = g + g8[k][b]
            g_out[k][...] = g
            if k < n:
                d_out[k][...], m_out[k][...], v_out[k][...] = _adamw(w[k][...], g, m[k][...], v[k][...])

    shapes = [jax.ShapeDtypeStruct(w.shape, F32) for w in ws]
    loss_shape = jax.ShapeDtypeStruct(gathered[-1].shape[1:], F32)
    outs = pl.pallas_call(
        body, name="small_sum_adamw",
        in_specs=[VMEM_FULL] * (4 * n + 1), out_specs=[VMEM_FULL] * (4 * n + 1),
        out_shape=shapes + [loss_shape] + shapes * 3,
        compiler_params=_params(),
    )(*gathered, *ws, *ms, *vs)
    return outs[:n + 1], outs[n + 1:2 * n + 1], outs[2 * n + 1:3 * n + 1], outs[3 * n + 1:]


SMALL_NAMES = ("pool_scale", "sgu_ln_g", "sgu_ln_b", "sgu_w", "sgu_b", "ln_g", "ln_b", "ple_gate_b")


TOK_TILE = 256
GRAD_TILE = 1024


def _weight_views():
    cols = lambda width: (lambda ref, b: ref.at[:, pl.ds(pl.multiple_of(b * width, 128), width)])
    rows = lambda height: (lambda ref, b: ref.at[pl.ds(pl.multiple_of(b * height, 16), height), :])
    pool_rows = POOL_GROUP // N_DEV
    return {"w_in": cols(D_IN // N_DEV),
            "pool_w": lambda ref, b: ref.at[:, pl.ds(pl.multiple_of(b * pool_rows, 16), pool_rows), :],
            "w_out": rows(D_MIX // N_DEV), "ple_w": cols(D_MODEL // N_DEV), "gate_w": rows(D_MODEL // N_DEV)}


WEIGHT_SHAPES = {"w_in": (D_MODEL, D_IN), "pool_w": (len(POOL_WINDOWS), POOL_GROUP, POOL_GROUP),
                 "w_out": (D_MIX, D_MODEL), "ple_w": (D_PLE, D_MODEL), "gate_w": (D_MODEL, D_MODEL)}


def _weight_gather(shards, names):
    views = _weight_views()
    return _gather_side([shards[nm] for nm in names],
                        [jax.ShapeDtypeStruct(WEIGHT_SHAPES[nm], BF16) for nm in names], [views[nm] for nm in names])


def kernel(x, p, w_in, pool_w, pool_scale, sgu_ln_g, sgu_ln_b, sgu_w, sgu_b, w_out, ln_g, ln_b, ple_w, ple_gate_w, ple_gate_b, loss_target, m_w_in, m_pool_w, m_pool_scale, m_sgu_ln_g, m_sgu_ln_b, m_sgu_w, m_sgu_b, m_w_out, m_ln_g, m_ln_b, m_ple_w, m_ple_gate_w, m_ple_gate_b, v_w_in, v_pool_w, v_pool_scale, v_sgu_ln_g, v_sgu_ln_b, v_sgu_w, v_sgu_b, v_w_out, v_ln_g, v_ln_b, v_ple_w, v_ple_gate_w, v_ple_gate_b):
    seq = x.shape[1]
    x2, p2, target = x[0], p[0, 0], loss_target[0]
    core = lax.axis_index("c").astype(jnp.int32).reshape(1)
    chip = (2 * lax.axis_index("x") + lax.axis_index("y")).astype(jnp.int32).reshape(1)
    pool_rows = POOL_GROUP // N_DEV

    shards = {"w_in": w_in[0].astype(BF16), "pool_w": pool_w[0].astype(BF16), "w_out": w_out[0].astype(BF16),
              "ple_w": ple_w[0].astype(BF16), "gate_w": ple_gate_w[0].astype(BF16)}
    w_in_f, pool_w_f = _comm_call(_weight_gather(shards, ("w_in", "pool_w")), "gather_mixer_weights")
    bias_tile = jnp.repeat(sgu_b[0].T, HEAD, axis=1)
    (h, y, pooled, xb), (w_out_f, ple_w_f, gate_w_f) = _forward_mixers(
        x2, w_in_f, pool_w_f, pool_scale, sgu_ln_g, sgu_ln_b, sgu_w[0], bias_tile, TOK_TILE,
        sides=(_weight_gather(shards, ("w_out", "ple_w", "gate_w")),))

    dy, dxr, xn, dgp, dpe, dr, loss, d_ln_g, d_ln_b, d_gate_b = _head_fwd_bwd(
        x2, y, p2, target, w_out_f, gate_w_f, ple_w_f, ln_g, ln_b, ple_gate_b, 2 * TOK_TILE, TOK_TILE)

    d_w_out, _ = _weight_grad(y, dr, 1, 1, GRAD_TILE, "grad_w_out")
    d_ple_w, _ = _weight_grad(p2, dpe, N_DEV, N_DEV, GRAD_TILE, "grad_ple_w")
    d_gate_w, _ = _weight_grad(xn, dgp, 1, 1, GRAD_TILE, "grad_gate_w")
    early_names = ("w_out", "ple_w", "gate_w")
    early = [d_w_out.reshape(N_DEV, D_MIX // N_DEV, D_MODEL), d_ple_w,
             d_gate_w.reshape(N_DEV, D_MODEL // N_DEV, D_MODEL)]
    early_theirs = _comm_call(_pair_side(early), "pair_exchange_head")
    early_sums = [_pair_sum(core, g, t, "pair_sum_" + nm) for g, t, nm in zip(early, early_theirs, early_names)]

    dh, dx, d_pool_w, d_pool_scale, d_sgu_ln_g, d_sgu_ln_b, d_sgu_w, d_sgu_b = _mixers_bwd(
        h, dy, dxr, pooled, w_in_f, pool_w_f, pool_scale, sgu_ln_g, sgu_ln_b, sgu_w[0], bias_tile, TOK_TILE)

    pool_blocks = d_pool_w.reshape(N_DEV, len(POOL_WINDOWS) * pool_rows, POOL_GROUP)
    (pool_theirs,) = _comm_call(_pair_side([pool_blocks]), "pair_exchange_pool")
    hosted_names = early_names + ("pool_w",)
    hosted_sums = early_sums + [_pair_sum(core, pool_blocks, pool_theirs, "pair_sum_pool_w")]

    small = (d_pool_scale, d_sgu_ln_g, d_sgu_ln_b, d_sgu_w, d_sgu_b, d_ln_g, d_ln_b, d_gate_b)
    parts = small + (loss,)
    small_gather = _gather_side(parts, [jax.ShapeDtypeStruct((N_DEV,) + a.shape, F32) for a in parts],
                                [lambda ref, b: ref.at[b]] * len(parts))
    cx, cy = lax.axis_index("x"), lax.axis_index("y")
    order = jnp.stack([2 * px + py for px, py in _other_chips(cx, cy) + [(cx, cy)]]).astype(jnp.int32)
    w_in_own, w_in_others, side_out = _grad_w_in_reduced(
        order, xb, dh, GRAD_TILE, sides=(_chip_side([s_bf for _, s_bf in hosted_sums]), small_gather))
    hosted_others, gathered = side_out[:len(hosted_sums)], side_out[len(hosted_sums):]

    shard_of = {"w_in": (w_in, m_w_in, v_w_in), "pool_w": (pool_w, m_pool_w, v_pool_w),
                "w_out": (w_out, m_w_out, v_w_out), "ple_w": (ple_w, m_ple_w, v_ple_w),
                "gate_w": (ple_gate_w, m_ple_gate_w, v_ple_gate_w)}
    reduced = [(nm, chip, s_f32, oth) for nm, (s_f32, _), oth in zip(hosted_names, hosted_sums, hosted_others)]
    reduced.append(("w_in", jnp.zeros((1,), jnp.int32), w_in_own[None], w_in_others))
    big_out = {}
    for nm, which, s_f32, oth in reduced:
        w, m, v = shard_of[nm]
        two_d = s_f32.shape[1:]
        res = _sum_adamw(which, s_f32, oth, w.reshape(two_d), m.reshape(two_d), v.reshape(two_d), "adamw_" + nm)
        big_out[nm] = [r.reshape(w.shape) for r in res]

    small_w = (pool_scale, sgu_ln_g, sgu_ln_b, sgu_w, sgu_b, ln_g, ln_b, ple_gate_b)
    small_m = (m_pool_scale, m_sgu_ln_g, m_sgu_ln_b, m_sgu_w, m_sgu_b, m_ln_g, m_ln_b, m_ple_gate_b)
    small_v = (v_pool_scale, v_sgu_ln_g, v_sgu_ln_b, v_sgu_w, v_sgu_b, v_ln_g, v_ln_b, v_ple_gate_b)
    natural = [[a.reshape(g.shape) for a, g in zip(group, small)] for group in (small_w, small_m, small_v)]
    res = _small_sum_adamw(list(gathered), *natural)
    g_s, d_s, m_s, v_s = [[r.reshape(w.shape) for r, w in zip(kind, small_w)] for kind in res]
    total_loss = res[0][-1][0, 0]

    order = ("w_in", "pool_w", "pool_scale", "sgu_ln_g", "sgu_ln_b", "sgu_w", "sgu_b", "w_out", "ln_g", "ln_b",
             "ple_w", "ple_gate_w", "ple_gate_b")
    outs = [total_loss, dx.reshape(1, seq, D_MODEL)]
    for kind in range(4):
        for nm in order:
            key = "gate_w" if nm == "ple_gate_w" else nm
            if key in big_out:
                outs.append(big_out[key][kind])
            else:
                outs.append((g_s, d_s, m_s, v_s)[kind][SMALL_NAMES.index(nm)])
    return tuple(outs)
```

```python
from typing import Callable, NamedTuple

import numpy as np
import jax
import jax.numpy as jnp
from jax import lax
from jax.experimental import pallas as pl
from jax.experimental.pallas import tpu as pltpu

F32 = jnp.float32
BF16 = jnp.bfloat16

N_DEV = 8
D_MODEL = 1024
D_POOL = 1024
D_SGU = 1024
D_MIX = 2048
D_IN = 5120
D_PLE = 256
POOL_WINDOWS = (2, 4, 8, 16)
POOL_GROUP = 256
N_HEADS = 4
HEAD = 256
CHUNK = 128
HALO = 16
BAND_PAD = 128
ALPHA = 2.0 ** 0.25
LN_EPS = 1e-5
ADAM_LR, ADAM_B1, ADAM_B2, ADAM_EPS, ADAM_WD, ADAM_STEP = 0.001, 0.9, 0.999, 1e-08, 0.01, 10

U0, V0, Z0 = D_POOL, D_POOL + D_SGU, D_POOL + 2 * D_SGU
VMEM_LIMIT = 56 * 1024 * 1024
MESH = pl.DeviceIdType.MESH
ANY = pl.BlockSpec(memory_space=pl.ANY)
VMEM_FULL = pl.BlockSpec(memory_space=pltpu.VMEM)

_GELU_C0 = 0.7978845608028654
_GELU_C1 = 0.044715


def _gelu_cdf(x, x2):
    return 1.0 / (1.0 + jnp.exp(x * ((-2.0 * _GELU_C0) + (-2.0 * _GELU_C0 * _GELU_C1) * x2)))


def _gelu_and_grad(x):
    x2 = x * x
    cdf = _gelu_cdf(x, x2)
    g = x * cdf
    dg = cdf + g * (1.0 - cdf) * ((2.0 * _GELU_C0) + (6.0 * _GELU_C0 * _GELU_C1) * x2)
    return g, dg


def _gelu(x):
    t = jnp.tanh(_GELU_C0 * (x + _GELU_C1 * (x * x * x)))
    return x * (0.5 * (1.0 + t))


def _split_bf16(x):
    hi = x.astype(BF16)
    return hi, (x - hi.astype(F32)).astype(BF16)


def _band(tok_tile, window, transpose):
    t = np.arange(tok_tile)[:, None]
    s = np.arange(tok_tile + BAND_PAD)[None, :]
    d = (s - t) if transpose else (t + BAND_PAD - s)
    return ((d >= 0) & (d < window)).astype(np.float32)


def _bands(tok_tile, transpose):
    return jnp.asarray(np.stack([_band(tok_tile, w, transpose) for w in POOL_WINDOWS]), dtype=BF16)


def _sigmoid(x):
    return 1.0 / (1.0 + jnp.exp(-x))


def _dot(a, b):
    return jnp.dot(a, b, preferred_element_type=F32)


def _dot_nt(a, b):
    return lax.dot_general(a, b, (((1,), (1,)), ((), ())), preferred_element_type=F32)


def _dot_tn(a, b):
    return lax.dot_general(a, b, (((0,), (0,)), ((), ())), preferred_element_type=F32)


def _row_stats(x):
    mu = jnp.mean(x, axis=-1, keepdims=True)
    xc = x - mu
    var = jnp.mean(xc * xc, axis=-1, keepdims=True)
    rstd = lax.rsqrt(var + LN_EPS)
    return xc * rstd, rstd


def _ln_bwd(dxhat, xhat, rstd):
    m1 = jnp.mean(dxhat, axis=-1, keepdims=True)
    m2 = jnp.mean(dxhat * xhat, axis=-1, keepdims=True)
    return rstd * (dxhat - m1 - xhat * m2)


def _masked_sgu_w(sw_ref, hh):
    row = lax.broadcasted_iota(jnp.int32, (CHUNK, CHUNK), 0)
    col = lax.broadcasted_iota(jnp.int32, (CHUNK, CHUNK), 1)
    return jnp.where(row >= col, sw_ref[hh], 0.0)


def _inv_count(tile_index, tok_tile, window):
    tok = tile_index * tok_tile + lax.broadcasted_iota(jnp.int32, (tok_tile, 1), 0)
    return 1.0 / jnp.minimum(tok + 1, window).astype(F32)


def _params(**kw):
    return pltpu.CompilerParams(vmem_limit_bytes=VMEM_LIMIT, **kw)


def _forward_mixers(x, w_in, pool_w, pool_scale, sgu_ln_g, sgu_ln_b, sgu_w, sgu_bias_tile, tok_tile, sides=()):
    seq = x.shape[0]
    n_tiles = seq // tok_tile
    n_chunks = tok_tile // CHUNK
    side_refs = _SideRefs(sides, 8, 4, 2)

    def body(*refs):
        (x_ref, win_ref, pw_ref, ps_ref, lg_ref, lb_ref, sw_ref, sb_ref,
         hb_ref, y_ref, pooled_ref, xb_ref, aext_ref, h_ref) = side_refs.split(refs)
        i = pl.program_id(0)
        side_refs.emit(i == 0, i == n_tiles // 2, False)

        @pl.when(i == 0)
        def _():
            aext_ref[0:HALO, :] = jnp.zeros((HALO, D_POOL), F32)

        xb = x_ref[...].astype(BF16)
        xb_ref[...] = xb
        for s in range(D_IN // 1024):
            cs = slice(s * 1024, (s + 1) * 1024)
            section = _dot(xb, win_ref[:, cs])
            h_ref[:, cs] = section
            if s >= 1:
                hb_ref[:, (s - 1) * 1024:s * 1024] = section.astype(BF16)

        aext_ref[HALO:HALO + tok_tile, :] = h_ref[:, 0:D_POOL]
        for g, window in enumerate(POOL_WINDOWS):
            cols = slice(g * POOL_GROUP, (g + 1) * POOL_GROUP)
            win = aext_ref[HALO:HALO + tok_tile, cols]
            for k in range(1, window):
                win = win + aext_ref[HALO - k:HALO - k + tok_tile, cols]
            pooled = win * _inv_count(i, tok_tile, window) - h_ref[:, cols]
            pb = pooled.astype(BF16)
            pooled_ref[:, cols] = pb
            mixed = _dot(pb, pw_ref[g])
            z = h_ref[:, Z0 + g * POOL_GROUP:Z0 + (g + 1) * POOL_GROUP]
            y_ref[:, cols] = (mixed * ps_ref[:, cols] * (z * _sigmoid(z))).astype(BF16)
        aext_ref[0:HALO, :] = aext_ref[tok_tile:tok_tile + HALO, :]

        for hh in range(N_HEADS):
            cols = slice(hh * HEAD, (hh + 1) * HEAD)
            swm = _masked_sgu_w(sw_ref, hh).astype(BF16)
            for n in range(n_chunks):
                rows = slice(n * CHUNK, (n + 1) * CHUNK)
                gu = _gelu(h_ref[rows, U0 + hh * HEAD:U0 + (hh + 1) * HEAD])
                gv = _gelu(h_ref[rows, V0 + hh * HEAD:V0 + (hh + 1) * HEAD])
                xhat, _ = _row_stats(gv)
                vln = xhat * lg_ref[:, cols] + lb_ref[:, cols]
                sv = _dot(swm, vln.astype(BF16)) + sb_ref[:, cols]
                z = h_ref[rows, Z0 + D_POOL + hh * HEAD:Z0 + D_POOL + (hh + 1) * HEAD]
                y_ref[rows, D_POOL + hh * HEAD:D_POOL + (hh + 1) * HEAD] = (
                    gu * sv * (z * _sigmoid(z))).astype(BF16)

        side_refs.emit(False, False, i == n_tiles - 1)

    tok = lambda width: pl.BlockSpec((tok_tile, width), lambda i: (i, 0))
    outs = pl.pallas_call(
        body, name="forward_mixers",
        grid=(n_tiles,),
        in_specs=[tok(D_MODEL)] + [VMEM_FULL] * 7 + side_refs.in_specs,
        out_specs=[tok(D_IN - D_POOL), tok(D_MIX), tok(D_POOL), tok(D_MODEL)] + side_refs.out_specs,
        out_shape=[jax.ShapeDtypeStruct((seq, D_IN - D_POOL), BF16), jax.ShapeDtypeStruct((seq, D_MIX), BF16),
                   jax.ShapeDtypeStruct((seq, D_POOL), BF16), jax.ShapeDtypeStruct((seq, D_MODEL), BF16)]
        + side_refs.out_shapes,
        scratch_shapes=[pltpu.VMEM((HALO + tok_tile, D_POOL), F32), pltpu.VMEM((tok_tile, D_IN), F32)]
        + side_refs.scratch,
        compiler_params=_params(dimension_semantics=("arbitrary",)),
    )(x, w_in, pool_w, pool_scale, sgu_ln_g, sgu_ln_b, sgu_w, sgu_bias_tile, *side_refs.inputs)
    return outs[:4], outs[4:]


def _head_fwd_bwd(x, y, p, target, w_out, gate_w, ple_w, ln_g, ln_b, gate_b, tok_tile, sub_tile):
    seq = x.shape[0]

    def body(x_ref, y_ref, p_ref, t_ref, wout_ref, gw_ref, plw_ref, lng_ref, lnb_ref, gb_ref,
             dy_ref, dxr_ref, xn_ref, dgp_ref, dpe_ref, dr_ref, loss_ref, dlng_ref, dlnb_ref, dgb_ref):
        i = pl.program_id(0)

        @pl.when(i == 0)
        def _():
            loss_ref[...] = jnp.zeros_like(loss_ref)
            dlng_ref[...] = jnp.zeros_like(dlng_ref)
            dlnb_ref[...] = jnp.zeros_like(dlnb_ref)
            dgb_ref[...] = jnp.zeros_like(dgb_ref)

        subs = [slice(s * sub_tile, (s + 1) * sub_tile) for s in range(tok_tile // sub_tile)]
        stats, xns, douts = [], [], []
        for rows in subs:
            r = ALPHA * x_ref[rows, :] + _dot(y_ref[rows, :], wout_ref[...])
            xhat, rstd = _row_stats(r)
            xn = xhat * lng_ref[...] + lnb_ref[...]
            xn_ref[rows, :] = xn.astype(BF16)
            stats.append((xhat, rstd))
            xns.append(xn)
        loss = jnp.zeros((1, 1), F32)
        dgb = jnp.zeros((1, D_MODEL), F32)
        for rows, xn in zip(subs, xns):
            gate = _sigmoid(_dot(xn_ref[rows, :], gw_ref[...]) + gb_ref[...])
            pe = _dot(p_ref[rows, :].astype(BF16), plw_ref[...])
            err = xn + gate * pe - t_ref[rows, :]
            loss = loss + jnp.sum(err * err, keepdims=True)
            dout = err * (1.0 / D_MODEL)
            dpe_ref[rows, :] = (dout * gate).astype(BF16)
            dgpre = dout * pe * gate * (1.0 - gate)
            dgb = dgb + jnp.sum(dgpre, axis=0, keepdims=True)
            dgp_ref[rows, :] = dgpre.astype(BF16)
            douts.append(dout)
        loss_ref[...] += (0.5 / D_MODEL) * loss
        dgb_ref[...] += dgb
        dlng = jnp.zeros((1, D_MODEL), F32)
        dlnb = jnp.zeros((1, D_MODEL), F32)
        for rows, (xhat, rstd), dout in zip(subs, stats, douts):
            dxn = dout + _dot_nt(dgp_ref[rows, :], gw_ref[...])
            dlng = dlng + jnp.sum(dxn * xhat, axis=0, keepdims=True)
            dlnb = dlnb + jnp.sum(dxn, axis=0, keepdims=True)
            dr = _ln_bwd(dxn * lng_ref[...], xhat, rstd)
            dxr_ref[rows, :] = ALPHA * dr
            dr_ref[rows, :] = dr.astype(BF16)
        dlng_ref[...] += dlng
        dlnb_ref[...] += dlnb
        for rows in subs:
            dy_ref[rows, :] = _dot_nt(dr_ref[rows, :], wout_ref[...])

    tok = lambda width: pl.BlockSpec((tok_tile, width), lambda i: (i, 0))
    acc = lambda width: pl.BlockSpec((1, width), lambda i: (0, 0))
    vec = jax.ShapeDtypeStruct((1, D_MODEL), F32)
    return pl.pallas_call(
        body, name="head_fwd_bwd",
        grid=(seq // tok_tile,),
        in_specs=[tok(D_MODEL), tok(D_MIX), tok(D_PLE), tok(D_MODEL),
                  VMEM_FULL, VMEM_FULL, VMEM_FULL, VMEM_FULL, VMEM_FULL, VMEM_FULL],
        out_specs=[tok(D_MIX), tok(D_MODEL), tok(D_MODEL), tok(D_MODEL), tok(D_MODEL), tok(D_MODEL),
                   acc(128), acc(D_MODEL), acc(D_MODEL), acc(D_MODEL)],
        out_shape=[jax.ShapeDtypeStruct((seq, D_MIX), F32), jax.ShapeDtypeStruct((seq, D_MODEL), F32),
                   jax.ShapeDtypeStruct((seq, D_MODEL), BF16), jax.ShapeDtypeStruct((seq, D_MODEL), BF16),
                   jax.ShapeDtypeStruct((seq, D_MODEL), BF16), jax.ShapeDtypeStruct((seq, D_MODEL), BF16),
                   jax.ShapeDtypeStruct((1, 128), F32), vec, vec, vec],
        compiler_params=_params(dimension_semantics=("arbitrary",)),
    )(x, y, p, target, w_out, gate_w, ple_w, ln_g, ln_b, gate_b)


def _mixers_bwd(h, dy, dxr, pooled, w_in, pool_w, pool_scale, sgu_ln_g, sgu_ln_b, sgu_w, sgu_bias_tile, tok_tile):
    seq = h.shape[0]
    n_tiles = seq // tok_tile
    n_chunks = tok_tile // CHUNK
    pool_rows = POOL_GROUP // N_DEV

    def body(h_ref, dy_ref, dxr_ref, pooled_ref, win_ref, pw_ref, ps_ref, lg_ref, lb_ref, sw_ref, sb_ref, band_ref,
             dh_ref, dx_ref, dpw_ref, dps_ref, dlg_ref, dlb_ref, dsw_ref, dsb_ref,
             qhi_ref, qlo_ref, dpw_acc, dsb_acc):
        i = pl.program_id(0)
        tile = n_tiles - 1 - i

        @pl.when(i == 0)
        def _():
            qhi_ref[...] = jnp.zeros_like(qhi_ref)
            qlo_ref[...] = jnp.zeros_like(qlo_ref)
            dpw_acc[...] = jnp.zeros_like(dpw_acc)
            dsb_acc[...] = jnp.zeros_like(dsb_acc)
            dps_ref[...] = jnp.zeros_like(dps_ref)
            dlg_ref[...] = jnp.zeros_like(dlg_ref)
            dlb_ref[...] = jnp.zeros_like(dlb_ref)
            dsw_ref[...] = jnp.zeros_like(dsw_ref)

        def h_at(rows, cols):
            return h_ref[rows, cols.start - D_POOL:cols.stop - D_POOL].astype(F32)

        everything = slice(0, tok_tile)
        for part in (qhi_ref, qlo_ref):
            part[tok_tile:tok_tile + HALO, :] = part[0:HALO, :]
        for g, window in enumerate(POOL_WINDOWS):
            cols = slice(g * POOL_GROUP, (g + 1) * POOL_GROUP)
            zcols = slice(Z0 + g * POOL_GROUP, Z0 + (g + 1) * POOL_GROUP)
            z = h_at(everything, zcols)
            sz = _sigmoid(z)
            pb = pooled_ref[:, cols]
            mixed = _dot(pb, pw_ref[g])
            dyp = dy_ref[:, cols]
            dh_ref[:, zcols] = (dyp * (mixed * ps_ref[:, cols]) * (sz * (1.0 + z * (1.0 - sz)))).astype(BF16)
            dms = dyp * (z * sz)
            dps_ref[:, cols] += jnp.sum(dms * mixed, axis=0, keepdims=True)
            dmixed = (dms * ps_ref[:, cols]).astype(BF16)
            dpw_acc[g] += _dot_tn(pb, dmixed)
            dpooled = _dot_nt(dmixed, pw_ref[g])
            qhi_ref[everything, cols], qlo_ref[everything, cols] = _split_bf16(
                dpooled * _inv_count(tile, tok_tile, window))
            da = _dot(band_ref[g], qhi_ref[:, cols]) + _dot(band_ref[g], qlo_ref[:, cols]) - dpooled
            dh_ref[:, cols] = da.astype(BF16)

        pool_cols = [slice(o + g * POOL_GROUP, o + (g + 1) * POOL_GROUP)
                     for o in (0, Z0) for g in range(len(POOL_WINDOWS))]
        pool_slices = [pool_cols[0:3], pool_cols[3:6], pool_cols[6:8], []]
        for hh in range(N_HEADS):
            cols = slice(hh * HEAD, (hh + 1) * HEAD)
            ucols = slice(U0 + hh * HEAD, U0 + (hh + 1) * HEAD)
            vcols = slice(V0 + hh * HEAD, V0 + (hh + 1) * HEAD)
            zcols = slice(Z0 + D_POOL + hh * HEAD, Z0 + D_POOL + (hh + 1) * HEAD)
            sw32 = _masked_sgu_w(sw_ref, hh)
            swm = sw32.astype(BF16)
            swm_t = sw32.T.astype(BF16)
            for n in range(n_chunks):
                rows = slice(n * CHUNK, (n + 1) * CHUNK)
                gu, dgu_du = _gelu_and_grad(h_at(rows, ucols))
                gv, dgv_dv = _gelu_and_grad(h_at(rows, vcols))
                xhat, rstd = _row_stats(gv)
                vb = (xhat * lg_ref[:, cols] + lb_ref[:, cols]).astype(BF16)
                sv = _dot(swm, vb) + sb_ref[:, cols]
                z = h_at(rows, zcols)
                sz = _sigmoid(z)
                dys = dy_ref[rows, D_POOL + hh * HEAD:D_POOL + (hh + 1) * HEAD]
                dh_ref[rows, zcols] = (dys * (gu * sv) * (sz * (1.0 + z * (1.0 - sz)))).astype(BF16)
                dyg = dys * (z * sz)
                dh_ref[rows, ucols] = (dyg * sv * dgu_du).astype(BF16)
                dsv = dyg * gu
                dsb_acc[:, cols] += dsv
                dsvb = dsv.astype(BF16)
                dsw_ref[hh] += _dot_nt(dsvb, vb)
                dvln = _dot(swm_t, dsvb)
                dlg_ref[:, cols] += jnp.sum(dvln * xhat, axis=0, keepdims=True)
                dlb_ref[:, cols] += jnp.sum(dvln, axis=0, keepdims=True)
                dgv = _ln_bwd(dvln * lg_ref[:, cols], xhat, rstd)
                dh_ref[rows, vcols] = (dgv * dgv_dv).astype(BF16)
            ready = [ucols, vcols, zcols] + pool_slices[hh]
            part = _dot_nt(dh_ref[:, ready[0]], win_ref[:, ready[0]])
            for sl in ready[1:]:
                part = part + _dot_nt(dh_ref[:, sl], win_ref[:, sl])
            if hh == 0:
                dx_ref[...] = dxr_ref[...] + part
            else:
                dx_ref[...] += part

        @pl.when(i == n_tiles - 1)
        def _():
            for g in range(len(POOL_WINDOWS)):
                for b in range(N_DEV):
                    dpw_ref[b, g] = dpw_acc[g, b * pool_rows:(b + 1) * pool_rows, :]
            row = lax.broadcasted_iota(jnp.int32, (CHUNK, CHUNK), 0)
            col = lax.broadcasted_iota(jnp.int32, (CHUNK, CHUNK), 1)
            for hh in range(N_HEADS):
                dsw_ref[hh] = jnp.where(row >= col, dsw_ref[hh], 0.0)
                total = jnp.sum(dsb_acc[:, hh * HEAD:(hh + 1) * HEAD], axis=1, keepdims=True)
                dsb_ref[hh:hh + 1, :] = jnp.broadcast_to(total, (CHUNK, CHUNK)).T[0:1, :]

    tok = lambda width: pl.BlockSpec((tok_tile, width), lambda i: (n_tiles - 1 - i, 0))
    whole = lambda shape: pl.BlockSpec(shape, lambda i: (0,) * len(shape))
    vec = jax.ShapeDtypeStruct((1, D_MODEL), F32)
    return pl.pallas_call(
        body, name="mixers_bwd",
        grid=(n_tiles,),
        in_specs=[tok(D_IN - D_POOL), tok(D_MIX), tok(D_MODEL), tok(D_POOL)] + [VMEM_FULL] * 8,
        out_specs=[tok(D_IN), tok(D_MODEL), whole((N_DEV, len(POOL_WINDOWS), pool_rows, POOL_GROUP)),
                   whole((1, D_POOL)), whole((1, D_SGU)), whole((1, D_SGU)),
                   whole((N_HEADS, CHUNK, CHUNK)), whole((N_HEADS, CHUNK))],
        out_shape=[jax.ShapeDtypeStruct((seq, D_IN), BF16), jax.ShapeDtypeStruct((seq, D_MODEL), F32),
                   jax.ShapeDtypeStruct((N_DEV, len(POOL_WINDOWS), pool_rows, POOL_GROUP), F32),
                   vec, vec, vec,
                   jax.ShapeDtypeStruct((N_HEADS, CHUNK, CHUNK), F32),
                   jax.ShapeDtypeStruct((N_HEADS, CHUNK), F32)],
        scratch_shapes=[pltpu.VMEM((tok_tile + BAND_PAD, D_POOL), BF16),
                        pltpu.VMEM((tok_tile + BAND_PAD, D_POOL), BF16),
                        pltpu.VMEM((len(POOL_WINDOWS), POOL_GROUP, POOL_GROUP), F32),
                        pltpu.VMEM((CHUNK, D_SGU), F32)],
        compiler_params=_params(dimension_semantics=("arbitrary",)),
    )(h, dy, dxr, pooled, w_in, pool_w, pool_scale, sgu_ln_g, sgu_ln_b, sgu_w, sgu_bias_tile, _bands(tok_tile, True))


def _weight_grad(a, b, n_col_blocks, blocks_per_step, tok_tile, name, sides=()):
    seq, m = a.shape
    n = b.shape[1]
    nb = n // n_col_blocks
    n_steps = n_col_blocks // blocks_per_step
    n_k = seq // tok_tile
    side_refs = _SideRefs(sides, 2, 1, 0)

    def body(*refs):
        a_ref, b_ref, out_ref = side_refs.split(refs)
        step = pl.program_id(0) * n_k + pl.program_id(1)
        side_refs.emit(step == 0, step == (n_steps * n_k) // 2, False)

        @pl.when(pl.program_id(1) == 0)
        def _():
            out_ref[...] = jnp.zeros_like(out_ref)

        res = _dot_tn(a_ref[...].astype(BF16), b_ref[...])
        for blk in range(blocks_per_step):
            out_ref[blk] += res[:, blk * nb:(blk + 1) * nb]

        side_refs.emit(False, False, step == n_steps * n_k - 1)

    outs = pl.pallas_call(
        body, name=name,
        grid=(n_steps, n_k),
        in_specs=[pl.BlockSpec((tok_tile, m), lambda j, k: (k, 0)),
                  pl.BlockSpec((tok_tile, blocks_per_step * nb), lambda j, k: (k, j))] + side_refs.in_specs,
        out_specs=[pl.BlockSpec((blocks_per_step, m, nb), lambda j, k: (j, 0, 0))] + side_refs.out_specs,
        out_shape=[jax.ShapeDtypeStruct((n_col_blocks, m, nb), F32)] + side_refs.out_shapes,
        scratch_shapes=side_refs.scratch,
        compiler_params=_params(dimension_semantics=("arbitrary", "arbitrary")),
    )(a, b, *side_refs.inputs)
    return outs[0], outs[1:]


class _Side(NamedTuple):
    inputs: list
    out_shapes: list
    sem_shapes: list
    emit: Callable


def _when(cond):
    if cond is True:
        return lambda f: f()
    if cond is False:
        return lambda f: None
    return pl.when(cond)


def _place():
    return lax.axis_index("x"), lax.axis_index("y"), lax.axis_index("c")


def _other_chips(x, y):
    return [(1 - x, y), (x, 1 - y), (1 - x, 1 - y)]


def _gather_side(shards, out_shapes, views):
    n = len(shards)

    def emit(ins, outs, sems, first, mid, last):
        send_sems, recv_sems, local_sems = sems
        x, y, c = _place()
        me, sibling = (x, y, c), (x, y, 1 - c)
        chips = _other_chips(x, y)

        def copy(k, s, block, to, src=None):
            px, py, pc = block
            dst = views[k](outs[k], 4 * px + 2 * py + pc)
            return pltpu.make_async_remote_copy(
                src_ref=dst if src is None else src, dst_ref=dst,
                send_sem=send_sems.at[k, s], recv_sem=recv_sems.at[k, s],
                device_id=to, device_id_type=MESH)

        def own(k):
            return pltpu.make_async_copy(ins[k], views[k](outs[k], 4 * x + 2 * y + c), local_sems.at[k])

        def sends(k):
            return [copy(k, 0, me, sibling, src=ins[k])] + [
                copy(k, 1 + j, me, (*chip, c), src=ins[k]) for j, chip in enumerate(chips)]

        def passed_on(k, j):
            return copy(k, 4 + j, (*chips[j], c), sibling)

        @_when(first)
        def _():
            for k in range(n):
                own(k).start()
                for cp in sends(k):
                    cp.start()

        @_when(mid)
        def _():
            for k in range(n):
                for j, chip in enumerate(chips):
                    copy(k, 1 + j, (*chip, c), me).wait_recv()
                    passed_on(k, j).start()

        @_when(last)
        def _():
            for k in range(n):
                copy(k, 0, sibling, me).wait_recv()
                for j, chip in enumerate(chips):
                    copy(k, 4 + j, (*chip, 1 - c), me).wait_recv()
            for k in range(n):
                for cp in sends(k) + [passed_on(k, j) for j in range(len(chips))]:
                    cp.wait_send()
                own(k).wait()

    sems = [pltpu.SemaphoreType.DMA((n, 7)), pltpu.SemaphoreType.DMA((n, 7)), pltpu.SemaphoreType.DMA((n,))]
    return _Side(list(shards), list(out_shapes), sems, emit)


def _pair_side(grads):
    n = len(grads)

    def emit(ins, theirs, sems, first, mid, last):
        send_sems, recv_sems = sems
        x, y, c = _place()

        def copies():
            return [pltpu.make_async_remote_copy(
                src_ref=ins[k].at[2 * j + (1 - c)], dst_ref=theirs[k].at[j],
                send_sem=send_sems.at[k, j], recv_sem=recv_sems.at[k, j],
                device_id=(x, y, 1 - c), device_id_type=MESH) for k in range(n) for j in range(4)]

        @_when(first)
        def _():
            for cp in copies():
                cp.start()

        @_when(last)
        def _():
            for cp in copies():
                cp.wait_recv()
            for cp in copies():
                cp.wait_send()

    shapes = [jax.ShapeDtypeStruct((4,) + g.shape[1:], g.dtype) for g in grads]
    return _Side(list(grads), shapes, [pltpu.SemaphoreType.DMA((n, 4)), pltpu.SemaphoreType.DMA((n, 4))], emit)


def _chip_side(sums):
    n = len(sums)

    def emit(ins, others, sems, first, mid, last):
        send_sems, recv_sems = sems
        x, y, c = _place()

        def copies():
            return [pltpu.make_async_remote_copy(
                src_ref=ins[k].at[2 * px + py], dst_ref=others[k].at[r],
                send_sem=send_sems.at[k, r], recv_sem=recv_sems.at[k, r],
                device_id=(px, py, c), device_id_type=MESH)
                for k in range(n) for r, (px, py) in enumerate(_other_chips(x, y))]

        @_when(first)
        def _():
            for cp in copies():
                cp.start()

        @_when(last)
        def _():
            for cp in copies():
                cp.wait_recv()
            for cp in copies():
                cp.wait_send()

    shapes = [jax.ShapeDtypeStruct((3,) + s.shape[1:], s.dtype) for s in sums]
    return _Side(list(sums), shapes, [pltpu.SemaphoreType.DMA((n, 3)), pltpu.SemaphoreType.DMA((n, 3))], emit)


def _comm_call(side, name):
    n_in, n_out = len(side.inputs), len(side.out_shapes)

    def body(*refs):
        side.emit(refs[:n_in], refs[n_in:n_in + n_out], refs[n_in + n_out:], True, True, True)

    return pl.pallas_call(
        body, name=name, in_specs=[ANY] * n_in, out_specs=[ANY] * n_out,
        out_shape=side.out_shapes, scratch_shapes=side.sem_shapes,
    )(*side.inputs)


class _SideRefs:
    def __init__(self, sides, n_in, n_out, n_scratch):
        self.sides, self.n_in, self.n_out, self.n_scratch = sides, n_in, n_out, n_scratch
        self.inputs = [a for s in sides for a in s.inputs]
        self.out_shapes = [o for s in sides for o in s.out_shapes]
        self.scratch = [m for s in sides for m in s.sem_shapes]
        self.in_specs = [ANY] * len(self.inputs)
        self.out_specs = [ANY] * len(self.out_shapes)

    def split(self, refs):
        refs = list(refs)
        n_side_in, n_side_out = len(self.inputs), len(self.out_shapes)
        ins, rest = refs[:self.n_in], refs[self.n_in:]
        side_in, rest = rest[:n_side_in], rest[n_side_in:]
        outs, rest = rest[:self.n_out], rest[self.n_out:]
        side_out, rest = rest[:n_side_out], rest[n_side_out:]
        scratch, side_sems = rest[:self.n_scratch], rest[self.n_scratch:]
        self._refs = (side_in, side_out, side_sems)
        return ins + outs + scratch

    def emit(self, first, mid, last):
        side_in, side_out, side_sems = self._refs
        for s in self.sides:
            a, b, m = len(s.inputs), len(s.out_shapes), len(s.sem_shapes)
            s.emit(side_in[:a], side_out[:b], side_sems[:m], first, mid, last)
            side_in, side_out, side_sems = side_in[a:], side_out[b:], side_sems[m:]


def _grad_w_in_reduced(order, a, b, tok_tile, sides=()):
    seq, m = a.shape
    nb = b.shape[1] // N_DEV
    n_k = seq // tok_tile
    n_rows = 4
    last_step = n_rows * n_k - 1
    assert n_k >= 3
    side_refs = _SideRefs(sides, 3, 3, 8)

    def body(*refs):
        (order_ref, a_ref, b_ref, own_ref, theirs_ref, others_ref,
         acc_ref, stage_ref, sumbf_ref, pair_send, pair_recv, ici_send, ici_recv, stage_sem) = side_refs.split(refs)
        j, k = pl.program_id(0), pl.program_id(1)
        step = j * n_k + k
        side_refs.emit(step == 0, step == (n_rows * n_k) // 2, False)
        x, y, c = _place()
        chips = _other_chips(x, y)

        def to_sibling(row):
            return pltpu.make_async_remote_copy(
                src_ref=acc_ref.at[row % 2, 1 - c], dst_ref=theirs_ref.at[row],
                send_sem=pair_send.at[row], recv_sem=pair_recv.at[row],
                device_id=(x, y, 1 - c), device_id_type=MESH)

        def to_owner(row):
            px, py = chips[row]
            return pltpu.make_async_remote_copy(
                src_ref=sumbf_ref.at[row], dst_ref=others_ref.at[row],
                send_sem=ici_send.at[row], recv_sem=ici_recv.at[row],
                device_id=(px, py, c), device_id_type=MESH)

        def staged(row):
            return pltpu.make_async_copy(theirs_ref.at[row], stage_ref, stage_sem.at[0])

        @pl.when(k == 0)
        def _():
            acc_ref[j % 2] = jnp.zeros((2, m, nb), F32)

        res = _dot_tn(a_ref[...].astype(BF16), b_ref[...])
        for blk in range(2):
            acc_ref[j % 2, blk] += res[:, blk * nb:(blk + 1) * nb]

        for row in range(n_rows):
            @pl.when((j == row) & (k == n_k - 1))
            def _():
                to_sibling(row).start()

            if row < n_rows - 1:
                @pl.when((j == row + 1) & (k == 1))
                def _():
                    to_sibling(row).wait_recv()
                    staged(row).start()

                @pl.when((j == row + 1) & (k == 2))
                def _():
                    staged(row).wait()
                    to_sibling(row).wait_send()
                    sumbf_ref[row] = (acc_ref[row % 2, c] + stage_ref[...]).astype(BF16)
                    to_owner(row).start()

        @pl.when(step == last_step)
        def _():
            row = n_rows - 1
            to_sibling(row).wait_recv()
            staged(row).start()
            staged(row).wait()
            to_sibling(row).wait_send()
            own_ref[...] = acc_ref[row % 2, c] + stage_ref[...]
            for r in range(n_rows - 1):
                to_owner(r).wait_recv()
                to_owner(r).wait_send()

        side_refs.emit(False, False, step == last_step)

    block = jax.ShapeDtypeStruct((m, nb), F32)
    outs = pl.pallas_call(
        body, name="grad_w_in",
        grid_spec=pltpu.PrefetchScalarGridSpec(
            num_scalar_prefetch=1, grid=(n_rows, n_k),
            in_specs=[pl.BlockSpec((tok_tile, m), lambda j, k, order_ref: (k, 0)),
                      pl.BlockSpec((tok_tile, 2 * nb), lambda j, k, order_ref: (k, order_ref[j]))]
            + side_refs.in_specs,
            out_specs=[pl.BlockSpec((m, nb), lambda j, k, order_ref: (0, 0)), ANY, ANY] + side_refs.out_specs,
            scratch_shapes=[pltpu.VMEM((2, 2, m, nb), F32), pltpu.VMEM((m, nb), F32),
                            pltpu.VMEM((n_rows - 1, m, nb), BF16),
                            pltpu.SemaphoreType.DMA((n_rows,)), pltpu.SemaphoreType.DMA((n_rows,)),
                            pltpu.SemaphoreType.DMA((n_rows - 1,)), pltpu.SemaphoreType.DMA((n_rows - 1,)),
                            pltpu.SemaphoreType.DMA((1,))] + side_refs.scratch),
        out_shape=[block, jax.ShapeDtypeStruct((n_rows, m, nb), F32),
                   jax.ShapeDtypeStruct((n_rows - 1, m, nb), BF16)] + side_refs.out_shapes,
        compiler_params=_params(dimension_semantics=("arbitrary", "arbitrary")),
    )(order, a, b, *side_refs.inputs)
    return outs[0], outs[2], outs[3:]


def _row_tile(rows, cols):
    tile = rows
    while tile * cols > 256 * 1024 and tile % 16 == 0:
        tile //= 2
    return tile


def _pair_sum(core, grads, theirs, name):
    _, rows, cols = theirs.shape
    rt = _row_tile(rows, cols)

    def body(core_ref, a_ref, b_ref, o_ref, ob_ref):
        total = a_ref[...] + b_ref[...]
        o_ref[...] = total
        ob_ref[...] = total.astype(BF16)

    spec = pl.BlockSpec((None, rt, cols), lambda j, i, core_ref: (j, i, 0))
    mine = pl.BlockSpec((None, None, rt, cols), lambda j, i, core_ref: (j, core_ref[0], i, 0))
    return pl.pallas_call(
        body, name=name,
        grid_spec=pltpu.PrefetchScalarGridSpec(
            num_scalar_prefetch=1, grid=(4, rows // rt), in_specs=[mine, spec], out_specs=[spec, spec]),
        out_shape=[jax.ShapeDtypeStruct(theirs.shape, F32), jax.ShapeDtypeStruct(theirs.shape, BF16)],
        compiler_params=_params(dimension_semantics=("arbitrary", "arbitrary")),
    )(core, grads.reshape(4, 2, rows, cols), theirs)


def _adamw(w, g, m, v):
    m = ADAM_B1 * m + (1.0 - ADAM_B1) * g
    v = ADAM_B2 * v + (1.0 - ADAM_B2) * (g * g)
    m_hat = m / (1.0 - ADAM_B1 ** ADAM_STEP)
    v_hat = v / (1.0 - ADAM_B2 ** ADAM_STEP)
    delta = -ADAM_LR * (m_hat / (jnp.sqrt(v_hat) + ADAM_EPS) + ADAM_WD * w)
    return delta, m, v


def _sum_adamw(chip, sums, others, w, m, v, name):
    _, rows, cols = sums.shape
    rt = _row_tile(rows, cols)

    def body(chip_ref, own_ref, oth_ref, w_ref, m_ref, v_ref, g_ref, d_ref, nm_ref, nv_ref):
        g = ((own_ref[...] + oth_ref[0].astype(F32)) + oth_ref[1].astype(F32)) + oth_ref[2].astype(F32)
        g_ref[...] = g
        d_ref[...], nm_ref[...], nv_ref[...] = _adamw(w_ref[...], g, m_ref[...], v_ref[...])

    spec = pl.BlockSpec((rt, cols), lambda i, chip_ref: (i, 0))
    own = pl.BlockSpec((None, rt, cols), lambda i, chip_ref: (chip_ref[0], i, 0))
    shape = jax.ShapeDtypeStruct((rows, cols), F32)
    return pl.pallas_call(
        body, name=name,
        grid_spec=pltpu.PrefetchScalarGridSpec(
            num_scalar_prefetch=1, grid=(rows // rt,),
            in_specs=[own, pl.BlockSpec((3, rt, cols), lambda i, chip_ref: (0, i, 0)), spec, spec, spec],
            out_specs=[spec] * 4),
        out_shape=[shape] * 4,
        compiler_params=_params(dimension_semantics=("arbitrary",)),
    )(chip, sums, others, w, m, v)


def _small_sum_adamw(gathered, ws, ms, vs):
    n = len(ws)

    def body(*refs):
        g8 = refs[:n + 1]
        w, m, v = refs[n + 1:2 * n + 1], refs[2 * n + 1:3 * n + 1], refs[3 * n + 1:4 * n + 1]
        outs = refs[4 * n + 1:]
        g_out, d_out, m_out, v_out = outs[:n + 1], outs[n + 1:2 * n + 1], outs[2 * n + 1:3 * n + 1], outs[3 * n + 1:]
        for k in range(n + 1):
            g = g8[k][0]
            for b in range(1, N_DEV):
                g = g + g8[k][b]
            g_out[k][...] = g
            if k < n:
                d_out[k][...], m_out[k][...], v_out[k][...] = _adamw(w[k][...], g, m[k][...], v[k][...])

    shapes = [jax.ShapeDtypeStruct(w.shape, F32) for w in ws]
    loss_shape = jax.ShapeDtypeStruct(gathered[-1].shape[1:], F32)
    outs = pl.pallas_call(
        body, name="small_sum_adamw",
        in_specs=[VMEM_FULL] * (4 * n + 1), out_specs=[VMEM_FULL] * (4 * n + 1),
        out_shape=shapes + [loss_shape] + shapes * 3,
        compiler_params=_params(),
    )(*gathered, *ws, *ms, *vs)
    return outs[:n + 1], outs[n + 1:2 * n + 1], outs[2 * n + 1:3 * n + 1], outs[3 * n + 1:]


SMALL_NAMES = ("pool_scale", "sgu_ln_g", "sgu_ln_b", "sgu_w", "sgu_b", "ln_g", "ln_b", "ple_gate_b")


TOK_TILE = 256
GRAD_TILE = 1024


def _weight_views():
    cols = lambda width: (lambda ref, b: ref.at[:, pl.ds(pl.multiple_of(b * width, 128), width)])
    rows = lambda height: (lambda ref, b: ref.at[pl.ds(pl.multiple_of(b * height, 16), height), :])
    pool_rows = POOL_GROUP // N_DEV
    return {"w_in": cols(D_IN // N_DEV),
            "pool_w": lambda ref, b: ref.at[:, pl.ds(pl.multiple_of(b * pool_rows, 16), pool_rows), :],
            "w_out": rows(D_MIX // N_DEV), "ple_w": cols(D_MODEL // N_DEV), "gate_w": rows(D_MODEL // N_DEV)}


WEIGHT_SHAPES = {"w_in": (D_MODEL, D_IN), "pool_w": (len(POOL_WINDOWS), POOL_GROUP, POOL_GROUP),
                 "w_out": (D_MIX, D_MODEL), "ple_w": (D_PLE, D_MODEL), "gate_w": (D_MODEL, D_MODEL)}


def _weight_gather(shards, names):
    views = _weight_views()
    return _gather_side([shards[nm] for nm in names],
                        [jax.ShapeDtypeStruct(WEIGHT_SHAPES[nm], BF16) for nm in names], [views[nm] for nm in names])


def kernel(x, p, w_in, pool_w, pool_scale, sgu_ln_g, sgu_ln_b, sgu_w, sgu_b, w_out, ln_g, ln_b, ple_w, ple_gate_w, ple_gate_b, loss_target, m_w_in, m_pool_w, m_pool_scale, m_sgu_ln_g, m_sgu_ln_b, m_sgu_w, m_sgu_b, m_w_out, m_ln_g, m_ln_b, m_ple_w, m_ple_gate_w, m_ple_gate_b, v_w_in, v_pool_w, v_pool_scale, v_sgu_ln_g, v_sgu_ln_b, v_sgu_w, v_sgu_b, v_w_out, v_ln_g, v_ln_b, v_ple_w, v_ple_gate_w, v_ple_gate_b):
    seq = x.shape[1]
    x2, p2, target = x[0], p[0, 0], loss_target[0]
    core = lax.axis_index("c").astype(jnp.int32).reshape(1)
    chip = (2 * lax.axis_index("x") + lax.axis_index("y")).astype(jnp.int32).reshape(1)
    pool_rows = POOL_GROUP // N_DEV

    shards = {"w_in": w_in[0].astype(BF16), "pool_w": pool_w[0].astype(BF16), "w_out": w_out[0].astype(BF16),
              "ple_w": ple_w[0].astype(BF16), "gate_w": ple_gate_w[0].astype(BF16)}
    w_in_f, pool_w_f = _comm_call(_weight_gather(shards, ("w_in", "pool_w")), "gather_mixer_weights")
    bias_tile = jnp.repeat(sgu_b[0].T, HEAD, axis=1)
    (h, y, pooled, xb), (w_out_f, ple_w_f, gate_w_f) = _forward_mixers(
        x2, w_in_f, pool_w_f, pool_scale, sgu_ln_g, sgu_ln_b, sgu_w[0], bias_tile, TOK_TILE,
        sides=(_weight_gather(shards, ("w_out", "ple_w", "gate_w")),))

    dy, dxr, xn, dgp, dpe, dr, loss, d_ln_g, d_ln_b, d_gate_b = _head_fwd_bwd(
        x2, y, p2, target, w_out_f, gate_w_f, ple_w_f, ln_g, ln_b, ple_gate_b, 2 * TOK_TILE, TOK_TILE)

    d_ple_w, _ = _weight_grad(p2, dpe, N_DEV, N_DEV, GRAD_TILE, "grad_ple_w")
    d_w_out, (ple_theirs,) = _weight_grad(y, dr, 1, 1, GRAD_TILE, "grad_w_out", sides=(_pair_side([d_ple_w]),))
    w_out_blocks = d_w_out.reshape(N_DEV, D_MIX // N_DEV, D_MODEL)
    d_gate_w, (w_out_theirs,) = _weight_grad(xn, dgp, 1, 1, GRAD_TILE, "grad_gate_w",
                                             sides=(_pair_side([w_out_blocks]),))
    gate_blocks = d_gate_w.reshape(N_DEV, D_MODEL // N_DEV, D_MODEL)

    dh, dx, d_pool_w, d_pool_scale, d_sgu_ln_g, d_sgu_ln_b, d_sgu_w, d_sgu_b = _mixers_bwd(
        h, dy, dxr, pooled, w_in_f, pool_w_f, pool_scale, sgu_ln_g, sgu_ln_b, sgu_w[0], bias_tile, TOK_TILE)

    pool_blocks = d_pool_w.reshape(N_DEV, len(POOL_WINDOWS) * pool_rows, POOL_GROUP)
    gate_theirs, pool_theirs = _comm_call(_pair_side([gate_blocks, pool_blocks]), "pair_exchange_late")
    hosted_names = ("w_out", "ple_w", "gate_w", "pool_w")
    hosted_sums = [_pair_sum(core, g, t, "pair_sum_" + nm) for g, t, nm in zip(
        (w_out_blocks, d_ple_w, gate_blocks, pool_blocks),
        (w_out_theirs, ple_theirs, gate_theirs, pool_theirs), hosted_names)]

    small = (d_pool_scale, d_sgu_ln_g, d_sgu_ln_b, d_sgu_w, d_sgu_b, d_ln_g, d_ln_b, d_gate_b)
    parts = small + (loss,)
    small_gather = _gather_side(parts, [jax.ShapeDtypeStruct((N_DEV,) + a.shape, F32) for a in parts],
                                [lambda ref, b: ref.at[b]] * len(parts))
    cx, cy = lax.axis_index("x"), lax.axis_index("y")
    order = jnp.stack([2 * px + py for px, py in _other_chips(cx, cy) + [(cx, cy)]]).astype(jnp.int32)
    w_in_own, w_in_others, side_out = _grad_w_in_reduced(
        order, xb, dh, GRAD_TILE, sides=(_chip_side([s_bf for _, s_bf in hosted_sums]), small_gather))
    hosted_others, gathered = side_out[:len(hosted_sums)], side_out[len(hosted_sums):]

    shard_of = {"w_in": (w_in, m_w_in, v_w_in), "pool_w": (pool_w, m_pool_w, v_pool_w),
                "w_out": (w_out, m_w_out, v_w_out), "ple_w": (ple_w, m_ple_w, v_ple_w),
                "gate_w": (ple_gate_w, m_ple_gate_w, v_ple_gate_w)}
    reduced = [(nm, chip, s_f32, oth) for nm, (s_f32, _), oth in zip(hosted_names, hosted_sums, hosted_others)]
    reduced.append(("w_in", jnp.zeros((1,), jnp.int32), w_in_own[None], w_in_others))
    big_out = {}
    for nm, which, s_f32, oth in reduced:
        w, m, v = shard_of[nm]
        two_d = s_f32.shape[1:]
        res = _sum_adamw(which, s_f32, oth, w.reshape(two_d), m.reshape(two_d), v.reshape(two_d), "adamw_" + nm)
        big_out[nm] = [r.reshape(w.shape) for r in res]

    small_w = (pool_scale, sgu_ln_g, sgu_ln_b, sgu_w, sgu_b, ln_g, ln_b, ple_gate_b)
    small_m = (m_pool_scale, m_sgu_ln_g, m_sgu_ln_b, m_sgu_w, m_sgu_b, m_ln_g, m_ln_b, m_ple_gate_b)
    small_v = (v_pool_scale, v_sgu_ln_g, v_sgu_ln_b, v_sgu_w, v_sgu_b, v_ln_g, v_ln_b, v_ple_gate_b)
    natural = [[a.reshape(g.shape) for a, g in zip(group, small)] for group in (small_w, small_m, small_v)]
    res = _small_sum_adamw(list(gathered), *natural)
    g_s, d_s, m_s, v_s = [[r.reshape(w.shape) for r, w in zip(kind, small_w)] for kind in res]
    total_loss = res[0][-1][0, 0]

    order = ("w_in", "pool_w", "pool_scale", "sgu_ln_g", "sgu_ln_b", "sgu_w", "sgu_b", "w_out", "ln_g", "ln_b",
             "ple_w", "ple_gate_w", "ple_gate_b")
    outs = [total_loss, dx.reshape(1, seq, D_MODEL)]
    for kind in range(4):
        for nm in order:
            key = "gate_w" if nm == "ple_gate_w" else nm
            if key in big_out:
                outs.append(big_out[key][kind])
            else:
                outs.append((g_s, d_s, m_s, v_s)[kind][SMALL_NAMES.index(nm)])
    return tuple(outs)
```

```python
from typing import Callable, NamedTuple

import numpy as np
import jax
import jax.numpy as jnp
from jax import lax
from jax.experimental import pallas as pl
from jax.experimental.pallas import tpu as pltpu

F32 = jnp.float32
BF16 = jnp.bfloat16

N_DEV = 8
D_MODEL = 1024
D_POOL = 1024
D_SGU = 1024
D_MIX = 2048
D_IN = 5120
D_PLE = 256
POOL_WINDOWS = (2, 4, 8, 16)
POOL_GROUP = 256
N_HEADS = 4
HEAD = 256
CHUNK = 128
HALO = 16
BAND_PAD = 128
ALPHA = 2.0 ** 0.25
LN_EPS = 1e-5
ADAM_LR, ADAM_B1, ADAM_B2, ADAM_EPS, ADAM_WD, ADAM_STEP = 0.001, 0.9, 0.999, 1e-08, 0.01, 10

U0, V0, Z0 = D_POOL, D_POOL + D_SGU, D_POOL + 2 * D_SGU
VMEM_LIMIT = 56 * 1024 * 1024
MESH = pl.DeviceIdType.MESH
ANY = pl.BlockSpec(memory_space=pl.ANY)
VMEM_FULL = pl.BlockSpec(memory_space=pltpu.VMEM)

_GELU_C0 = 0.7978845608028654
_GELU_C1 = 0.044715


def _gelu_cdf(x, x2):
    return 1.0 / (1.0 + jnp.exp(x * ((-2.0 * _GELU_C0) + (-2.0 * _GELU_C0 * _GELU_C1) * x2)))


def _gelu_and_grad(x):
    x2 = x * x
    cdf = _gelu_cdf(x, x2)
    g = x * cdf
    dg = cdf + g * (1.0 - cdf) * ((2.0 * _GELU_C0) + (6.0 * _GELU_C0 * _GELU_C1) * x2)
    return g, dg


def _gelu(x):
    t = jnp.tanh(_GELU_C0 * (x + _GELU_C1 * (x * x * x)))
    return x * (0.5 * (1.0 + t))


def _split_bf16(x):
    hi = x.astype(BF16)
    return hi, (x - hi.astype(F32)).astype(BF16)


def _band(tok_tile, window, transpose):
    t = np.arange(tok_tile)[:, None]
    s = np.arange(tok_tile + BAND_PAD)[None, :]
    d = (s - t) if transpose else (t + BAND_PAD - s)
    return ((d >= 0) & (d < window)).astype(np.float32)


def _bands(tok_tile, transpose):
    return jnp.asarray(np.stack([_band(tok_tile, w, transpose) for w in POOL_WINDOWS]), dtype=BF16)


def _sigmoid(x):
    return 1.0 / (1.0 + jnp.exp(-x))


def _dot(a, b):
    return jnp.dot(a, b, preferred_element_type=F32)


def _dot_nt(a, b):
    return lax.dot_general(a, b, (((1,), (1,)), ((), ())), preferred_element_type=F32)


def _dot_tn(a, b):
    return lax.dot_general(a, b, (((0,), (0,)), ((), ())), preferred_element_type=F32)


def _row_stats(x):
    mu = jnp.mean(x, axis=-1, keepdims=True)
    xc = x - mu
    var = jnp.mean(xc * xc, axis=-1, keepdims=True)
    rstd = lax.rsqrt(var + LN_EPS)
    return xc * rstd, rstd


def _ln_bwd(dxhat, xhat, rstd):
    m1 = jnp.mean(dxhat, axis=-1, keepdims=True)
    m2 = jnp.mean(dxhat * xhat, axis=-1, keepdims=True)
    return rstd * (dxhat - m1 - xhat * m2)


def _masked_sgu_w(sw_ref, hh):
    row = lax.broadcasted_iota(jnp.int32, (CHUNK, CHUNK), 0)
    col = lax.broadcasted_iota(jnp.int32, (CHUNK, CHUNK), 1)
    return jnp.where(row >= col, sw_ref[hh], 0.0)


def _inv_count(tile_index, tok_tile, window):
    tok = tile_index * tok_tile + lax.broadcasted_iota(jnp.int32, (tok_tile, 1), 0)
    return 1.0 / jnp.minimum(tok + 1, window).astype(F32)


def _params(**kw):
    return pltpu.CompilerParams(vmem_limit_bytes=VMEM_LIMIT, **kw)


def _forward_mixers(x, w_in, pool_w, pool_scale, sgu_ln_g, sgu_ln_b, sgu_w, sgu_bias_tile, tok_tile, sides=()):
    seq = x.shape[0]
    n_tiles = seq // tok_tile
    n_chunks = tok_tile // CHUNK
    side_refs = _SideRefs(sides, 8, 4, 2)

    def body(*refs):
        (x_ref, win_ref, pw_ref, ps_ref, lg_ref, lb_ref, sw_ref, sb_ref,
         hb_ref, y_ref, pooled_ref, xb_ref, aext_ref, h_ref) = side_refs.split(refs)
        i = pl.program_id(0)
        side_refs.emit(i == 0, i == n_tiles // 2, False)

        @pl.when(i == 0)
        def _():
            aext_ref[0:HALO, :] = jnp.zeros((HALO, D_POOL), F32)

        xb = x_ref[...].astype(BF16)
        xb_ref[...] = xb
        for s in range(D_IN // 1024):
            cs = slice(s * 1024, (s + 1) * 1024)
            section = _dot(xb, win_ref[:, cs])
            h_ref[:, cs] = section
            if s >= 1:
                hb_ref[:, (s - 1) * 1024:s * 1024] = section.astype(BF16)

        aext_ref[HALO:HALO + tok_tile, :] = h_ref[:, 0:D_POOL]
        for g, window in enumerate(POOL_WINDOWS):
            cols = slice(g * POOL_GROUP, (g + 1) * POOL_GROUP)
            win = aext_ref[HALO:HALO + tok_tile, cols]
            for k in range(1, window):
                win = win + aext_ref[HALO - k:HALO - k + tok_tile, cols]
            pooled = win * _inv_count(i, tok_tile, window) - h_ref[:, cols]
            pb = pooled.astype(BF16)
            pooled_ref[:, cols] = pb
            mixed = _dot(pb, pw_ref[g])
            z = h_ref[:, Z0 + g * POOL_GROUP:Z0 + (g + 1) * POOL_GROUP]
            y_ref[:, cols] = (mixed * ps_ref[:, cols] * (z * _sigmoid(z))).astype(BF16)
        aext_ref[0:HALO, :] = aext_ref[tok_tile:tok_tile + HALO, :]

        for hh in range(N_HEADS):
            cols = slice(hh * HEAD, (hh + 1) * HEAD)
            swm = _masked_sgu_w(sw_ref, hh).astype(BF16)
            for n in range(n_chunks):
                rows = slice(n * CHUNK, (n + 1) * CHUNK)
                gu = _gelu(h_ref[rows, U0 + hh * HEAD:U0 + (hh + 1) * HEAD])
                gv = _gelu(h_ref[rows, V0 + hh * HEAD:V0 + (hh + 1) * HEAD])
                xhat, _ = _row_stats(gv)
                vln = xhat * lg_ref[:, cols] + lb_ref[:, cols]
                sv = _dot(swm, vln.astype(BF16)) + sb_ref[:, cols]
                z = h_ref[rows, Z0 + D_POOL + hh * HEAD:Z0 + D_POOL + (hh + 1) * HEAD]
                y_ref[rows, D_POOL + hh * HEAD:D_POOL + (hh + 1) * HEAD] = (
                    gu * sv * (z * _sigmoid(z))).astype(BF16)

        side_refs.emit(False, False, i == n_tiles - 1)

    tok = lambda width: pl.BlockSpec((tok_tile, width), lambda i: (i, 0))
    outs = pl.pallas_call(
        body, name="forward_mixers",
        grid=(n_tiles,),
        in_specs=[tok(D_MODEL)] + [VMEM_FULL] * 7 + side_refs.in_specs,
        out_specs=[tok(D_IN - D_POOL), tok(D_MIX), tok(D_POOL), tok(D_MODEL)] + side_refs.out_specs,
        out_shape=[jax.ShapeDtypeStruct((seq, D_IN - D_POOL), BF16), jax.ShapeDtypeStruct((seq, D_MIX), BF16),
                   jax.ShapeDtypeStruct((seq, D_POOL), BF16), jax.ShapeDtypeStruct((seq, D_MODEL), BF16)]
        + side_refs.out_shapes,
        scratch_shapes=[pltpu.VMEM((HALO + tok_tile, D_POOL), F32), pltpu.VMEM((tok_tile, D_IN), F32)]
        + side_refs.scratch,
        compiler_params=_params(dimension_semantics=("arbitrary",)),
    )(x, w_in, pool_w, pool_scale, sgu_ln_g, sgu_ln_b, sgu_w, sgu_bias_tile, *side_refs.inputs)
    return outs[:4], outs[4:]


def _head_fwd_bwd(x, y, p, target, w_out, gate_w, ple_w, ln_g, ln_b, gate_b, tok_tile, sub_tile):
    seq = x.shape[0]

    def body(x_ref, y_ref, p_ref, t_ref, wout_ref, gw_ref, plw_ref, lng_ref, lnb_ref, gb_ref,
             dy_ref, dxr_ref, xn_ref, dgp_ref, dpe_ref, dr_ref, loss_ref, dlng_ref, dlnb_ref, dgb_ref):
        i = pl.program_id(0)

        @pl.when(i == 0)
        def _():
            loss_ref[...] = jnp.zeros_like(loss_ref)
            dlng_ref[...] = jnp.zeros_like(dlng_ref)
            dlnb_ref[...] = jnp.zeros_like(dlnb_ref)
            dgb_ref[...] = jnp.zeros_like(dgb_ref)

        subs = [slice(s * sub_tile, (s + 1) * sub_tile) for s in range(tok_tile // sub_tile)]
        stats, xns, douts = [], [], []
        for rows in subs:
            r = ALPHA * x_ref[rows, :] + _dot(y_ref[rows, :], wout_ref[...])
            xhat, rstd = _row_stats(r)
            xn = xhat * lng_ref[...] + lnb_ref[...]
            xn_ref[rows, :] = xn.astype(BF16)
            stats.append((xhat, rstd))
            xns.append(xn)
        loss = jnp.zeros((1, 1), F32)
        dgb = jnp.zeros((1, D_MODEL), F32)
        for rows, xn in zip(subs, xns):
            gate = _sigmoid(_dot(xn_ref[rows, :], gw_ref[...]) + gb_ref[...])
            pe = _dot(p_ref[rows, :].astype(BF16), plw_ref[...])
            err = xn + gate * pe - t_ref[rows, :]
            loss = loss + jnp.sum(err * err, keepdims=True)
            dout = err * (1.0 / D_MODEL)
            dpe_ref[rows, :] = (dout * gate).astype(BF16)
            dgpre = dout * pe * gate * (1.0 - gate)
            dgb = dgb + jnp.sum(dgpre, axis=0, keepdims=True)
            dgp_ref[rows, :] = dgpre.astype(BF16)
            douts.append(dout)
        loss_ref[...] += (0.5 / D_MODEL) * loss
        dgb_ref[...] += dgb
        dlng = jnp.zeros((1, D_MODEL), F32)
        dlnb = jnp.zeros((1, D_MODEL), F32)
        for rows, (xhat, rstd), dout in zip(subs, stats, douts):
            dxn = dout + _dot_nt(dgp_ref[rows, :], gw_ref[...])
            dlng = dlng + jnp.sum(dxn * xhat, axis=0, keepdims=True)
            dlnb = dlnb + jnp.sum(dxn, axis=0, keepdims=True)
            dr = _ln_bwd(dxn * lng_ref[...], xhat, rstd)
            dxr_ref[rows, :] = ALPHA * dr
            dr_ref[rows, :] = dr.astype(BF16)
        dlng_ref[...] += dlng
        dlnb_ref[...] += dlnb
        for rows in subs:
            dy_ref[rows, :] = _dot_nt(dr_ref[rows, :], wout_ref[...])

    tok = lambda width: pl.BlockSpec((tok_tile, width), lambda i: (i, 0))
    acc = lambda width: pl.BlockSpec((1, width), lambda i: (0, 0))
    vec = jax.ShapeDtypeStruct((1, D_MODEL), F32)
    return pl.pallas_call(
        body, name="head_fwd_bwd",
        grid=(seq // tok_tile,),
        in_specs=[tok(D_MODEL), tok(D_MIX), tok(D_PLE), tok(D_MODEL),
                  VMEM_FULL, VMEM_FULL, VMEM_FULL, VMEM_FULL, VMEM_FULL, VMEM_FULL],
        out_specs=[tok(D_MIX), tok(D_MODEL), tok(D_MODEL), tok(D_MODEL), tok(D_MODEL), tok(D_MODEL),
                   acc(128), acc(D_MODEL), acc(D_MODEL), acc(D_MODEL)],
        out_shape=[jax.ShapeDtypeStruct((seq, D_MIX), F32), jax.ShapeDtypeStruct((seq, D_MODEL), F32),
                   jax.ShapeDtypeStruct((seq, D_MODEL), BF16), jax.ShapeDtypeStruct((seq, D_MODEL), BF16),
                   jax.ShapeDtypeStruct((seq, D_MODEL), BF16), jax.ShapeDtypeStruct((seq, D_MODEL), BF16),
                   jax.ShapeDtypeStruct((1, 128), F32), vec, vec, vec],
        compiler_params=_params(dimension_semantics=("arbitrary",)),
    )(x, y, p, target, w_out, gate_w, ple_w, ln_g, ln_b, gate_b)


def _mixers_bwd(h, dy, dxr, pooled, w_in, pool_w, pool_scale, sgu_ln_g, sgu_ln_b, sgu_w, sgu_bias_tile, tok_tile):
    seq = h.shape[0]
    n_tiles = seq // tok_tile
    n_chunks = tok_tile // CHUNK
    pool_rows = POOL_GROUP // N_DEV

    def body(h_ref, dy_ref, dxr_ref, pooled_ref, win_ref, pw_ref, ps_ref, lg_ref, lb_ref, sw_ref, sb_ref, band_ref,
             dh_ref, dx_ref, dpw_ref, dps_ref, dlg_ref, dlb_ref, dsw_ref, dsb_ref,
             qhi_ref, qlo_ref, dpw_acc, dsb_acc):
        i = pl.program_id(0)
        tile = n_tiles - 1 - i

        @pl.when(i == 0)
        def _():
            qhi_ref[...] = jnp.zeros_like(qhi_ref)
            qlo_ref[...] = jnp.zeros_like(qlo_ref)
            dpw_acc[...] = jnp.zeros_like(dpw_acc)
            dsb_acc[...] = jnp.zeros_like(dsb_acc)
            dps_ref[...] = jnp.zeros_like(dps_ref)
            dlg_ref[...] = jnp.zeros_like(dlg_ref)
            dlb_ref[...] = jnp.zeros_like(dlb_ref)
            dsw_ref[...] = jnp.zeros_like(dsw_ref)

        def h_at(rows, cols):
            return h_ref[rows, cols.start - D_POOL:cols.stop - D_POOL].astype(F32)

        everything = slice(0, tok_tile)
        for part in (qhi_ref, qlo_ref):
            part[tok_tile:tok_tile + HALO, :] = part[0:HALO, :]
        for g, window in enumerate(POOL_WINDOWS):
            cols = slice(g * POOL_GROUP, (g + 1) * POOL_GROUP)
            zcols = slice(Z0 + g * POOL_GROUP, Z0 + (g + 1) * POOL_GROUP)
            z = h_at(everything, zcols)
            sz = _sigmoid(z)
            pb = pooled_ref[:, cols]
            mixed = _dot(pb, pw_ref[g])
            dyp = dy_ref[:, cols]
            dh_ref[:, zcols] = (dyp * (mixed * ps_ref[:, cols]) * (sz * (1.0 + z * (1.0 - sz)))).astype(BF16)
            dms = dyp * (z * sz)
            dps_ref[:, cols] += jnp.sum(dms * mixed, axis=0, keepdims=True)
            dmixed = (dms * ps_ref[:, cols]).astype(BF16)
            dpw_acc[g] += _dot_tn(pb, dmixed)
            dpooled = _dot_nt(dmixed, pw_ref[g])
            qhi_ref[everything, cols], qlo_ref[everything, cols] = _split_bf16(
                dpooled * _inv_count(tile, tok_tile, window))
            da = _dot(band_ref[g], qhi_ref[:, cols]) + _dot(band_ref[g], qlo_ref[:, cols]) - dpooled
            dh_ref[:, cols] = da.astype(BF16)

        pool_cols = [slice(o + g * POOL_GROUP, o + (g + 1) * POOL_GROUP)
                     for o in (0, Z0) for g in range(len(POOL_WINDOWS))]
        pool_slices = [pool_cols[0:3], pool_cols[3:6], pool_cols[6:8], []]
        for hh in range(N_HEADS):
            cols = slice(hh * HEAD, (hh + 1) * HEAD)
            ucols = slice(U0 + hh * HEAD, U0 + (hh + 1) * HEAD)
            vcols = slice(V0 + hh * HEAD, V0 + (hh + 1) * HEAD)
            zcols = slice(Z0 + D_POOL + hh * HEAD, Z0 + D_POOL + (hh + 1) * HEAD)
            sw32 = _masked_sgu_w(sw_ref, hh)
            swm = sw32.astype(BF16)
            swm_t = sw32.T.astype(BF16)
            for n in range(n_chunks):
                rows = slice(n * CHUNK, (n + 1) * CHUNK)
                gu, dgu_du = _gelu_and_grad(h_at(rows, ucols))
                gv, dgv_dv = _gelu_and_grad(h_at(rows, vcols))
                xhat, rstd = _row_stats(gv)
                vb = (xhat * lg_ref[:, cols] + lb_ref[:, cols]).astype(BF16)
                sv = _dot(swm, vb) + sb_ref[:, cols]
                z = h_at(rows, zcols)
                sz = _sigmoid(z)
                dys = dy_ref[rows, D_POOL + hh * HEAD:D_POOL + (hh + 1) * HEAD]
                dh_ref[rows, zcols] = (dys * (gu * sv) * (sz * (1.0 + z * (1.0 - sz)))).astype(BF16)
                dyg = dys * (z * sz)
                dh_ref[rows, ucols] = (dyg * sv * dgu_du).astype(BF16)
                dsv = dyg * gu
                dsb_acc[:, cols] += dsv
                dsvb = dsv.astype(BF16)
                dsw_ref[hh] += _dot_nt(dsvb, vb)
                dvln = _dot(swm_t, dsvb)
                dlg_ref[:, cols] += jnp.sum(dvln * xhat, axis=0, keepdims=True)
                dlb_ref[:, cols] += jnp.sum(dvln, axis=0, keepdims=True)
                dgv = _ln_bwd(dvln * lg_ref[:, cols], xhat, rstd)
                dh_ref[rows, vcols] = (dgv * dgv_dv).astype(BF16)
            ready = [ucols, vcols, zcols] + pool_slices[hh]
            part = _dot_nt(dh_ref[:, ready[0]], win_ref[:, ready[0]])
            for sl in ready[1:]:
                part = part + _dot_nt(dh_ref[:, sl], win_ref[:, sl])
            if hh == 0:
                dx_ref[...] = dxr_ref[...] + part
            else:
                dx_ref[...] += part

        @pl.when(i == n_tiles - 1)
        def _():
            for g in range(len(POOL_WINDOWS)):
                for b in range(N_DEV):
                    dpw_ref[b, g] = dpw_acc[g, b * pool_rows:(b + 1) * pool_rows, :]
            row = lax.broadcasted_iota(jnp.int32, (CHUNK, CHUNK), 0)
            col = lax.broadcasted_iota(jnp.int32, (CHUNK, CHUNK), 1)
            for hh in range(N_HEADS):
                dsw_ref[hh] = jnp.where(row >= col, dsw_ref[hh], 0.0)
                total = jnp.sum(dsb_acc[:, hh * HEAD:(hh + 1) * HEAD], axis=1, keepdims=True)
                dsb_ref[hh:hh + 1, :] = jnp.broadcast_to(total, (CHUNK, CHUNK)).T[0:1, :]

    tok = lambda width: pl.BlockSpec((tok_tile, width), lambda i: (n_tiles - 1 - i, 0))
    whole = lambda shape: pl.BlockSpec(shape, lambda i: (0,) * len(shape))
    vec = jax.ShapeDtypeStruct((1, D_MODEL), F32)
    return pl.pallas_call(
        body, name="mixers_bwd",
        grid=(n_tiles,),
        in_specs=[tok(D_IN - D_POOL), tok(D_MIX), tok(D_MODEL), tok(D_POOL)] + [VMEM_FULL] * 8,
        out_specs=[tok(D_IN), tok(D_MODEL), whole((N_DEV, len(POOL_WINDOWS), pool_rows, POOL_GROUP)),
                   whole((1, D_POOL)), whole((1, D_SGU)), whole((1, D_SGU)),
                   whole((N_HEADS, CHUNK, CHUNK)), whole((N_HEADS, CHUNK))],
        out_shape=[jax.ShapeDtypeStruct((seq, D_IN), BF16), jax.ShapeDtypeStruct((seq, D_MODEL), F32),
                   jax.ShapeDtypeStruct((N_DEV, len(POOL_WINDOWS), pool_rows, POOL_GROUP), F32),
                   vec, vec, vec,
                   jax.ShapeDtypeStruct((N_HEADS, CHUNK, CHUNK), F32),
                   jax.ShapeDtypeStruct((N_HEADS, CHUNK), F32)],
        scratch_shapes=[pltpu.VMEM((tok_tile + BAND_PAD, D_POOL), BF16),
                        pltpu.VMEM((tok_tile + BAND_PAD, D_POOL), BF16),
                        pltpu.VMEM((len(POOL_WINDOWS), POOL_GROUP, POOL_GROUP), F32),
                        pltpu.VMEM((CHUNK, D_SGU), F32)],
        compiler_params=_params(dimension_semantics=("arbitrary",)),
    )(h, dy, dxr, pooled, w_in, pool_w, pool_scale, sgu_ln_g, sgu_ln_b, sgu_w, sgu_bias_tile, _bands(tok_tile, True))


def _weight_grad(a, b, n_col_blocks, blocks_per_step, tok_tile, name, sides=()):
    seq, m = a.shape
    n = b.shape[1]
    nb = n // n_col_blocks
    n_steps = n_col_blocks // blocks_per_step
    n_k = seq // tok_tile
    side_refs = _SideRefs(sides, 2, 1, 0)

    def body(*refs):
        a_ref, b_ref, out_ref = side_refs.split(refs)
        step = pl.program_id(0) * n_k + pl.program_id(1)
        side_refs.emit(step == 0, step == (n_steps * n_k) // 2, False)

        @pl.when(pl.program_id(1) == 0)
        def _():
            out_ref[...] = jnp.zeros_like(out_ref)

        res = _dot_tn(a_ref[...].astype(BF16), b_ref[...])
        for blk in range(blocks_per_step):
            out_ref[blk] += res[:, blk * nb:(blk + 1) * nb]

        side_refs.emit(False, False, step == n_steps * n_k - 1)

    outs = pl.pallas_call(
        body, name=name,
        grid=(n_steps, n_k),
        in_specs=[pl.BlockSpec((tok_tile, m), lambda j, k: (k, 0)),
                  pl.BlockSpec((tok_tile, blocks_per_step * nb), lambda j, k: (k, j))] + side_refs.in_specs,
        out_specs=[pl.BlockSpec((blocks_per_step, m, nb), lambda j, k: (j, 0, 0))] + side_refs.out_specs,
        out_shape=[jax.ShapeDtypeStruct((n_col_blocks, m, nb), F32)] + side_refs.out_shapes,
        scratch_shapes=side_refs.scratch,
        compiler_params=_params(dimension_semantics=("arbitrary", "arbitrary")),
    )(a, b, *side_refs.inputs)
    return outs[0], outs[1:]


class _Side(NamedTuple):
    inputs: list
    out_shapes: list
    sem_shapes: list
    emit: Callable


def _when(cond):
    if cond is True:
        return lambda f: f()
    if cond is False:
        return lambda f: None
    return pl.when(cond)


def _place():
    return lax.axis_index("x"), lax.axis_index("y"), lax.axis_index("c")


def _other_chips(x, y):
    return [(1 - x, y), (x, 1 - y), (1 - x, 1 - y)]


def _gather_side(shards, out_shapes, views):
    n = len(shards)

    def emit(ins, outs, sems, first, mid, last):
        send_sems, recv_sems, local_sems = sems
        x, y, c = _place()
        me, sibling = (x, y, c), (x, y, 1 - c)
        chips = _other_chips(x, y)

        def copy(k, s, block, to, src=None):
            px, py, pc = block
            dst = views[k](outs[k], 4 * px + 2 * py + pc)
            return pltpu.make_async_remote_copy(
                src_ref=dst if src is None else src, dst_ref=dst,
                send_sem=send_sems.at[k, s], recv_sem=recv_sems.at[k, s],
                device_id=to, device_id_type=MESH)

        def own(k):
            return pltpu.make_async_copy(ins[k], views[k](outs[k], 4 * x + 2 * y + c), local_sems.at[k])

        def sends(k):
            return [copy(k, 0, me, sibling, src=ins[k])] + [
                copy(k, 1 + j, me, (*chip, c), src=ins[k]) for j, chip in enumerate(chips)]

        def passed_on(k, j):
            return copy(k, 4 + j, (*chips[j], c), sibling)

        @_when(first)
        def _():
            for k in range(n):
                own(k).start()
                for cp in sends(k):
                    cp.start()

        @_when(mid)
        def _():
            for k in range(n):
                for j, chip in enumerate(chips):
                    copy(k, 1 + j, (*chip, c), me).wait_recv()
                    passed_on(k, j).start()

        @_when(last)
        def _():
            for k in range(n):
                copy(k, 0, sibling, me).wait_recv()
                for j, chip in enumerate(chips):
                    copy(k, 4 + j, (*chip, 1 - c), me).wait_recv()
            for k in range(n):
                for cp in sends(k) + [passed_on(k, j) for j in range(len(chips))]:
                    cp.wait_send()
                own(k).wait()

    sems = [pltpu.SemaphoreType.DMA((n, 7)), pltpu.SemaphoreType.DMA((n, 7)), pltpu.SemaphoreType.DMA((n,))]
    return _Side(list(shards), list(out_shapes), sems, emit)


def _pair_side(grads):
    n = len(grads)

    def emit(ins, theirs, sems, first, mid, last):
        send_sems, recv_sems = sems
        x, y, c = _place()

        def copies():
            return [pltpu.make_async_remote_copy(
                src_ref=ins[k].at[2 * j + (1 - c)], dst_ref=theirs[k].at[j],
                send_sem=send_sems.at[k, j], recv_sem=recv_sems.at[k, j],
                device_id=(x, y, 1 - c), device_id_type=MESH) for k in range(n) for j in range(4)]

        @_when(first)
        def _():
            for cp in copies():
                cp.start()

        @_when(last)
        def _():
            for cp in copies():
                cp.wait_recv()
            for cp in copies():
                cp.wait_send()

    shapes = [jax.ShapeDtypeStruct((4,) + g.shape[1:], g.dtype) for g in grads]
    return _Side(list(grads), shapes, [pltpu.SemaphoreType.DMA((n, 4)), pltpu.SemaphoreType.DMA((n, 4))], emit)


def _chip_side(sums):
    n = len(sums)

    def emit(ins, others, sems, first, mid, last):
        send_sems, recv_sems = sems
        x, y, c = _place()

        def copies():
            return [pltpu.make_async_remote_copy(
                src_ref=ins[k].at[r], dst_ref=others[k].at[r],
                send_sem=send_sems.at[k, r], recv_sem=recv_sems.at[k, r],
                device_id=(px, py, c), device_id_type=MESH)
                for k in range(n) for r, (px, py) in enumerate(_other_chips(x, y))]

        @_when(first)
        def _():
            for cp in copies():
                cp.start()

        @_when(last)
        def _():
            for cp in copies():
                cp.wait_recv()
            for cp in copies():
                cp.wait_send()

    shapes = [jax.ShapeDtypeStruct((3,) + s.shape[1:], s.dtype) for s in sums]
    return _Side(list(sums), shapes, [pltpu.SemaphoreType.DMA((n, 3)), pltpu.SemaphoreType.DMA((n, 3))], emit)


def _comm_call(side, name):
    n_in, n_out = len(side.inputs), len(side.out_shapes)

    def body(*refs):
        side.emit(refs[:n_in], refs[n_in:n_in + n_out], refs[n_in + n_out:], True, True, True)

    return pl.pallas_call(
        body, name=name, in_specs=[ANY] * n_in, out_specs=[ANY] * n_out,
        out_shape=side.out_shapes, scratch_shapes=side.sem_shapes,
    )(*side.inputs)


class _SideRefs:
    def __init__(self, sides, n_in, n_out, n_scratch):
        self.sides, self.n_in, self.n_out, self.n_scratch = sides, n_in, n_out, n_scratch
        self.inputs = [a for s in sides for a in s.inputs]
        self.out_shapes = [o for s in sides for o in s.out_shapes]
        self.scratch = [m for s in sides for m in s.sem_shapes]
        self.in_specs = [ANY] * len(self.inputs)
        self.out_specs = [ANY] * len(self.out_shapes)

    def split(self, refs):
        refs = list(refs)
        n_side_in, n_side_out = len(self.inputs), len(self.out_shapes)
        ins, rest = refs[:self.n_in], refs[self.n_in:]
        side_in, rest = rest[:n_side_in], rest[n_side_in:]
        outs, rest = rest[:self.n_out], rest[self.n_out:]
        side_out, rest = rest[:n_side_out], rest[n_side_out:]
        scratch, side_sems = rest[:self.n_scratch], rest[self.n_scratch:]
        self._refs = (side_in, side_out, side_sems)
        return ins + outs + scratch

    def emit(self, first, mid, last):
        side_in, side_out, side_sems = self._refs
        for s in self.sides:
            a, b, m = len(s.inputs), len(s.out_shapes), len(s.sem_shapes)
            s.emit(side_in[:a], side_out[:b], side_sems[:m], first, mid, last)
            side_in, side_out, side_sems = side_in[a:], side_out[b:], side_sems[m:]


def _grad_w_in_reduced(order, a, b, tok_tile, sides=()):
    seq, m = a.shape
    nb = b.shape[1] // N_DEV
    n_k = seq // tok_tile
    n_rows = 4
    last_step = n_rows * n_k - 1
    assert n_k >= 3
    side_refs = _SideRefs(sides, 3, 3, 8)

    def body(*refs):
        (order_ref, a_ref, b_ref, own_ref, theirs_ref, others_ref,
         acc_ref, stage_ref, sumbf_ref, pair_send, pair_recv, ici_send, ici_recv, stage_sem) = side_refs.split(refs)
        j, k = pl.program_id(0), pl.program_id(1)
        step = j * n_k + k
        side_refs.emit(step == 0, step == (n_rows * n_k) // 2, False)
        x, y, c = _place()
        chips = _other_chips(x, y)

        def to_sibling(row):
            return pltpu.make_async_remote_copy(
                src_ref=acc_ref.at[row % 2, 1 - c], dst_ref=theirs_ref.at[row],
                send_sem=pair_send.at[row], recv_sem=pair_recv.at[row],
                device_id=(x, y, 1 - c), device_id_type=MESH)

        def to_owner(row):
            px, py = chips[row]
            return pltpu.make_async_remote_copy(
                src_ref=sumbf_ref.at[row], dst_ref=others_ref.at[row],
                send_sem=ici_send.at[row], recv_sem=ici_recv.at[row],
                device_id=(px, py, c), device_id_type=MESH)

        def staged(row):
            return pltpu.make_async_copy(theirs_ref.at[row], stage_ref, stage_sem.at[0])

        @pl.when(k == 0)
        def _():
            acc_ref[j % 2] = jnp.zeros((2, m, nb), F32)

        res = _dot_tn(a_ref[...].astype(BF16), b_ref[...])
        for blk in range(2):
            acc_ref[j % 2, blk] += res[:, blk * nb:(blk + 1) * nb]

        for row in range(n_rows):
            @pl.when((j == row) & (k == n_k - 1))
            def _():
                to_sibling(row).start()

            if row < n_rows - 1:
                @pl.when((j == row + 1) & (k == 1))
                def _():
                    to_sibling(row).wait_recv()
                    staged(row).start()

                @pl.when((j == row + 1) & (k == 2))
                def _():
                    staged(row).wait()
                    to_sibling(row).wait_send()
                    sumbf_ref[row] = (acc_ref[row % 2, c] + stage_ref[...]).astype(BF16)
                    to_owner(row).start()

        @pl.when(step == last_step)
        def _():
            row = n_rows - 1
            to_sibling(row).wait_recv()
            staged(row).start()
            staged(row).wait()
            to_sibling(row).wait_send()
            own_ref[...] = acc_ref[row % 2, c] + stage_ref[...]
            for r in range(n_rows - 1):
                to_owner(r).wait_recv()
                to_owner(r).wait_send()

        side_refs.emit(False, False, step == last_step)

    block = jax.ShapeDtypeStruct((m, nb), F32)
    outs = pl.pallas_call(
        body, name="grad_w_in",
        grid_spec=pltpu.PrefetchScalarGridSpec(
            num_scalar_prefetch=1, grid=(n_rows, n_k),
            in_specs=[pl.BlockSpec((tok_tile, m), lambda j, k, order_ref: (k, 0)),
                      pl.BlockSpec((tok_tile, 2 * nb), lambda j, k, order_ref: (k, order_ref[j]))]
            + side_refs.in_specs,
            out_specs=[pl.BlockSpec((m, nb), lambda j, k, order_ref: (0, 0)), ANY, ANY] + side_refs.out_specs,
            scratch_shapes=[pltpu.VMEM((2, 2, m, nb), F32), pltpu.VMEM((m, nb), F32),
                            pltpu.VMEM((n_rows - 1, m, nb), BF16),
                            pltpu.SemaphoreType.DMA((n_rows,)), pltpu.SemaphoreType.DMA((n_rows,)),
                            pltpu.SemaphoreType.DMA((n_rows - 1,)), pltpu.SemaphoreType.DMA((n_rows - 1,)),
                            pltpu.SemaphoreType.DMA((1,))] + side_refs.scratch),
        out_shape=[block, jax.ShapeDtypeStruct((n_rows, m, nb), F32),
                   jax.ShapeDtypeStruct((n_rows - 1, m, nb), BF16)] + side_refs.out_shapes,
        compiler_params=_params(dimension_semantics=("arbitrary", "arbitrary")),
    )(order, a, b, *side_refs.inputs)
    return outs[0], outs[2], outs[3:]


def _row_tile(rows, cols):
    tile = rows
    while tile * cols > 256 * 1024 and tile % 16 == 0:
        tile //= 2
    return tile


def _pair_sum(where, grads, theirs, name):
    _, rows, cols = theirs.shape
    rt = _row_tile(rows, cols)

    def body(where_ref, a_ref, b_ref, o_ref):
        o_ref[...] = (a_ref[...] + b_ref[...]).astype(BF16)

    mine = pl.BlockSpec((None, None, rt, cols), lambda r, i, where_ref: (where_ref[1 + r], where_ref[0], i, 0))
    sibling = pl.BlockSpec((None, rt, cols), lambda r, i, where_ref: (where_ref[1 + r], i, 0))
    return pl.pallas_call(
        body, name=name,
        grid_spec=pltpu.PrefetchScalarGridSpec(
            num_scalar_prefetch=1, grid=(3, rows // rt), in_specs=[mine, sibling],
            out_specs=pl.BlockSpec((None, rt, cols), lambda r, i, where_ref: (r, i, 0))),
        out_shape=jax.ShapeDtypeStruct((3, rows, cols), BF16),
        compiler_params=_params(dimension_semantics=("arbitrary", "arbitrary")),
    )(where, grads.reshape(4, 2, rows, cols), theirs)


def _adamw(w, g, m, v):
    m = ADAM_B1 * m + (1.0 - ADAM_B1) * g
    v = ADAM_B2 * v + (1.0 - ADAM_B2) * (g * g)
    m_hat = m / (1.0 - ADAM_B1 ** ADAM_STEP)
    v_hat = v / (1.0 - ADAM_B2 ** ADAM_STEP)
    delta = -ADAM_LR * (m_hat / (jnp.sqrt(v_hat) + ADAM_EPS) + ADAM_WD * w)
    return delta, m, v


def _sum_adamw(sel, mine, theirs, others, w, m, v, name):
    _, rows, cols = mine.shape
    rt = _row_tile(rows, cols)
    own_parts = [mine] if theirs is None else [mine, theirs]

    def body(sel_ref, *refs):
        own_refs, (oth_ref, w_ref, m_ref, v_ref, g_ref, d_ref, nm_ref, nv_ref) = (
            refs[:len(own_parts)], refs[len(own_parts):])
        g = own_refs[0][...]
        for part in own_refs[1:]:
            g = g + part[...]
        for r in range(3):
            g = g + oth_ref[r].astype(F32)
        g_ref[...] = g
        d_ref[...], nm_ref[...], nv_ref[...] = _adamw(w_ref[...], g, m_ref[...], v_ref[...])

    spec = pl.BlockSpec((rt, cols), lambda i, sel_ref: (i, 0))
    picked = [pl.BlockSpec((None, rt, cols), lambda i, sel_ref, k=k: (sel_ref[k], i, 0)) for k in range(len(own_parts))]
    shape = jax.ShapeDtypeStruct((rows, cols), F32)
    return pl.pallas_call(
        body, name=name,
        grid_spec=pltpu.PrefetchScalarGridSpec(
            num_scalar_prefetch=1, grid=(rows // rt,),
            in_specs=picked + [pl.BlockSpec((3, rt, cols), lambda i, sel_ref: (0, i, 0)), spec, spec, spec],
            out_specs=[spec] * 4),
        out_shape=[shape] * 4,
        compiler_params=_params(dimension_semantics=("arbitrary",)),
    )(sel, *own_parts, others, w, m, v)


def _small_sum_adamw(gathered, ws, ms, vs):
    n = len(ws)

    def body(*refs):
        g8 = refs[:n + 1]
        w, m, v = refs[n + 1:2 * n + 1], refs[2 * n + 1:3 * n + 1], refs[3 * n + 1:4 * n + 1]
        outs = refs[4 * n + 1:]
        g_out, d_out, m_out, v_out = outs[:n + 1], outs[n + 1:2 * n + 1], outs[2 * n + 1:3 * n + 1], outs[3 * n + 1:]
        for k in range(n + 1):
            g = g8[k][0]
            for b in range(1, N_DEV):
                g = g + g8[k][b]
            g_out[k][...] = g
            if k < n:
                d_out[k][...], m_out[k][...], v_out[k][...] = _adamw(w[k][...], g, m[k][...], v[k][...])

    shapes = [jax.ShapeDtypeStruct(w.shape, F32) for w in ws]
    loss_shape = jax.ShapeDtypeStruct(gathered[-1].shape[1:], F32)
    outs = pl.pallas_call(
        body, name="small_sum_adamw",
        in_specs=[VMEM_FULL] * (4 * n + 1), out_specs=[VMEM_FULL] * (4 * n + 1),
        out_shape=shapes + [loss_shape] + shapes * 3,
        compiler_params=_params(),
    )(*gathered, *ws, *ms, *vs)
    return outs[:n + 1], outs[n + 1:2 * n + 1], outs[2 * n + 1:3 * n + 1], outs[3 * n + 1:]


SMALL_NAMES = ("pool_scale", "sgu_ln_g", "sgu_ln_b", "sgu_w", "sgu_b", "ln_g", "ln_b", "ple_gate_b")


TOK_TILE = 256
GRAD_TILE = 1024


def _weight_views():
    cols = lambda width: (lambda ref, b: ref.at[:, pl.ds(pl.multiple_of(b * width, 128), width)])
    rows = lambda height: (lambda ref, b: ref.at[pl.ds(pl.multiple_of(b * height, 16), height), :])
    pool_rows = POOL_GROUP // N_DEV
    return {"w_in": cols(D_IN // N_DEV),
            "pool_w": lambda ref, b: ref.at[:, pl.ds(pl.multiple_of(b * pool_rows, 16), pool_rows), :],
            "w_out": rows(D_MIX // N_DEV), "ple_w": cols(D_MODEL // N_DEV), "gate_w": rows(D_MODEL // N_DEV)}


WEIGHT_SHAPES = {"w_in": (D_MODEL, D_IN), "pool_w": (len(POOL_WINDOWS), POOL_GROUP, POOL_GROUP),
                 "w_out": (D_MIX, D_MODEL), "ple_w": (D_PLE, D_MODEL), "gate_w": (D_MODEL, D_MODEL)}


def _weight_gather(shards, names):
    views = _weight_views()
    return _gather_side([shards[nm] for nm in names],
                        [jax.ShapeDtypeStruct(WEIGHT_SHAPES[nm], BF16) for nm in names], [views[nm] for nm in names])


def kernel(x, p, w_in, pool_w, pool_scale, sgu_ln_g, sgu_ln_b, sgu_w, sgu_b, w_out, ln_g, ln_b, ple_w, ple_gate_w, ple_gate_b, loss_target, m_w_in, m_pool_w, m_pool_scale, m_sgu_ln_g, m_sgu_ln_b, m_sgu_w, m_sgu_b, m_w_out, m_ln_g, m_ln_b, m_ple_w, m_ple_gate_w, m_ple_gate_b, v_w_in, v_pool_w, v_pool_scale, v_sgu_ln_g, v_sgu_ln_b, v_sgu_w, v_sgu_b, v_w_out, v_ln_g, v_ln_b, v_ple_w, v_ple_gate_w, v_ple_gate_b):
    seq = x.shape[1]
    x2, p2, target = x[0], p[0, 0], loss_target[0]
    pool_rows = POOL_GROUP // N_DEV

    shards = {"w_in": w_in[0].astype(BF16), "pool_w": pool_w[0].astype(BF16), "w_out": w_out[0].astype(BF16),
              "ple_w": ple_w[0].astype(BF16), "gate_w": ple_gate_w[0].astype(BF16)}
    w_in_f, pool_w_f = _comm_call(_weight_gather(shards, ("w_in", "pool_w")), "gather_mixer_weights")
    bias_tile = jnp.repeat(sgu_b[0].T, HEAD, axis=1)
    (h, y, pooled, xb), (w_out_f, ple_w_f, gate_w_f) = _forward_mixers(
        x2, w_in_f, pool_w_f, pool_scale, sgu_ln_g, sgu_ln_b, sgu_w[0], bias_tile, 2 * TOK_TILE,
        sides=(_weight_gather(shards, ("w_out", "ple_w", "gate_w")),))

    dy, dxr, xn, dgp, dpe, dr, loss, d_ln_g, d_ln_b, d_gate_b = _head_fwd_bwd(
        x2, y, p2, target, w_out_f, gate_w_f, ple_w_f, ln_g, ln_b, ple_gate_b, 2 * TOK_TILE, TOK_TILE)

    d_ple_w, _ = _weight_grad(p2, dpe, N_DEV, N_DEV, GRAD_TILE, "grad_ple_w")
    d_w_out, (ple_theirs,) = _weight_grad(y, dr, 1, 1, 2 * GRAD_TILE, "grad_w_out", sides=(_pair_side([d_ple_w]),))
    w_out_blocks = d_w_out.reshape(N_DEV, D_MIX // N_DEV, D_MODEL)
    d_gate_w, (w_out_theirs,) = _weight_grad(xn, dgp, 1, 1, GRAD_TILE, "grad_gate_w",
                                             sides=(_pair_side([w_out_blocks]),))
    gate_blocks = d_gate_w.reshape(N_DEV, D_MODEL // N_DEV, D_MODEL)

    dh, dx, d_pool_w, d_pool_scale, d_sgu_ln_g, d_sgu_ln_b, d_sgu_w, d_sgu_b = _mixers_bwd(
        h, dy, dxr, pooled, w_in_f, pool_w_f, pool_scale, sgu_ln_g, sgu_ln_b, sgu_w[0], bias_tile, TOK_TILE)

    pool_blocks = d_pool_w.reshape(N_DEV, len(POOL_WINDOWS) * pool_rows, POOL_GROUP)
    gate_theirs, pool_theirs = _comm_call(_pair_side([gate_blocks, pool_blocks]), "pair_exchange_late")
    hosted_names = ("w_out", "ple_w", "gate_w", "pool_w")
    hosted_mine = (w_out_blocks, d_ple_w, gate_blocks, pool_blocks)
    hosted_theirs = (w_out_theirs, ple_theirs, gate_theirs, pool_theirs)
    cx, cy, cc = _place()
    other_ids = [2 * px + py for px, py in _other_chips(cx, cy)]
    where = jnp.stack([cc] + other_ids).astype(jnp.int32)
    hosted_sums = [_pair_sum(where, g, t, "pair_sum_" + nm)
                   for g, t, nm in zip(hosted_mine, hosted_theirs, hosted_names)]

    small = (d_pool_scale, d_sgu_ln_g, d_sgu_ln_b, d_sgu_w, d_sgu_b, d_ln_g, d_ln_b, d_gate_b)
    parts = small + (loss,)
    small_gather = _gather_side(parts, [jax.ShapeDtypeStruct((N_DEV,) + a.shape, F32) for a in parts],
                                [lambda ref, b: ref.at[b]] * len(parts))
    my_chip = 2 * cx + cy
    order = jnp.stack(other_ids + [my_chip]).astype(jnp.int32)
    w_in_own, w_in_others, side_out = _grad_w_in_reduced(
        order, xb, dh, 2 * GRAD_TILE, sides=(_chip_side(hosted_sums), small_gather))
    hosted_others, gathered = side_out[:len(hosted_sums)], side_out[len(hosted_sums):]

    shard_of = {"w_in": (w_in, m_w_in, v_w_in), "pool_w": (pool_w, m_pool_w, v_pool_w),
                "w_out": (w_out, m_w_out, v_w_out), "ple_w": (ple_w, m_ple_w, v_ple_w),
                "gate_w": (ple_gate_w, m_ple_gate_w, v_ple_gate_w)}
    mine_theirs = jnp.stack([2 * my_chip + cc, my_chip]).astype(jnp.int32)
    reduced = [(nm, mine_theirs, g, t, oth)
               for nm, g, t, oth in zip(hosted_names, hosted_mine, hosted_theirs, hosted_others)]
    reduced.append(("w_in", jnp.zeros((1,), jnp.int32), w_in_own[None], None, w_in_others))
    big_out = {}
    for nm, sel, mine, theirs, oth in reduced:
        w, m, v = shard_of[nm]
        two_d = mine.shape[1:]
        res = _sum_adamw(sel, mine, theirs, oth, w.reshape(two_d), m.reshape(two_d), v.reshape(two_d), "adamw_" + nm)
        big_out[nm] = [r.reshape(w.shape) for r in res]

    small_w = (pool_scale, sgu_ln_g, sgu_ln_b, sgu_w, sgu_b, ln_g, ln_b, ple_gate_b)
    small_m = (m_pool_scale, m_sgu_ln_g, m_sgu_ln_b, m_sgu_w, m_sgu_b, m_ln_g, m_ln_b, m_ple_gate_b)
    small_v = (v_pool_scale, v_sgu_ln_g, v_sgu_ln_b, v_sgu_w, v_sgu_b, v_ln_g, v_ln_b, v_ple_gate_b)
    natural = [[a.reshape(g.shape) for a, g in zip(group, small)] for group in (small_w, small_m, small_v)]
    res = _small_sum_adamw(list(gathered), *natural)
    g_s, d_s, m_s, v_s = [[r.reshape(w.shape) for r, w in zip(kind, small_w)] for kind in res]
    total_loss = res[0][-1][0, 0]

    order = ("w_in", "pool_w", "pool_scale", "sgu_ln_g", "sgu_ln_b", "sgu_w", "sgu_b", "w_out", "ln_g", "ln_b",
             "ple_w", "ple_gate_w", "ple_gate_b")
    outs = [total_loss, dx.reshape(1, seq, D_MODEL)]
    for kind in range(4):
        for nm in order:
            key = "gate_w" if nm == "ple_gate_w" else nm
            if key in big_out:
                outs.append(big_out[key][kind])
            else:
                outs.append((g_s, d_s, m_s, v_s)[kind][SMALL_NAMES.index(nm)])
    return tuple(outs)
```

```python
from typing import Callable, NamedTuple

import numpy as np
import jax
import jax.numpy as jnp
from jax import lax
from jax.experimental import pallas as pl
from jax.experimental.pallas import tpu as pltpu

F32 = jnp.float32
BF16 = jnp.bfloat16

N_DEV = 8
D_MODEL = 1024
D_POOL = 1024
D_SGU = 1024
D_MIX = 2048
D_IN = 5120
D_PLE = 256
POOL_WINDOWS = (2, 4, 8, 16)
POOL_GROUP = 256
N_HEADS = 4
HEAD = 256
CHUNK = 128
HALO = 16
BAND_PAD = 128
ALPHA = 2.0 ** 0.25
LN_EPS = 1e-5
ADAM_LR, ADAM_B1, ADAM_B2, ADAM_EPS, ADAM_WD, ADAM_STEP = 0.001, 0.9, 0.999, 1e-08, 0.01, 10

U0, V0, Z0 = D_POOL, D_POOL + D_SGU, D_POOL + 2 * D_SGU
VMEM_LIMIT = 56 * 1024 * 1024
MESH = pl.DeviceIdType.MESH
ANY = pl.BlockSpec(memory_space=pl.ANY)
VMEM_FULL = pl.BlockSpec(memory_space=pltpu.VMEM)

_GELU_C0 = 0.7978845608028654
_GELU_C1 = 0.044715


def _gelu_cdf(x, x2):
    return 1.0 / (1.0 + jnp.exp(x * ((-2.0 * _GELU_C0) + (-2.0 * _GELU_C0 * _GELU_C1) * x2)))


def _gelu_and_grad(x):
    x2 = x * x
    cdf = _gelu_cdf(x, x2)
    g = x * cdf
    dg = cdf + g * (1.0 - cdf) * ((2.0 * _GELU_C0) + (6.0 * _GELU_C0 * _GELU_C1) * x2)
    return g, dg


def _gelu(x):
    t = jnp.tanh(_GELU_C0 * (x + _GELU_C1 * (x * x * x)))
    return x * (0.5 * (1.0 + t))


def _split_bf16(x):
    hi = x.astype(BF16)
    return hi, (x - hi.astype(F32)).astype(BF16)


def _band(tok_tile, window, transpose):
    t = np.arange(tok_tile)[:, None]
    s = np.arange(tok_tile + BAND_PAD)[None, :]
    d = (s - t) if transpose else (t + BAND_PAD - s)
    return ((d >= 0) & (d < window)).astype(np.float32)


def _bands(tok_tile, transpose):
    return jnp.asarray(np.stack([_band(tok_tile, w, transpose) for w in POOL_WINDOWS]), dtype=BF16)


def _sigmoid(x):
    return 1.0 / (1.0 + jnp.exp(-x))


def _dot(a, b):
    return jnp.dot(a, b, preferred_element_type=F32)


def _dot_nt(a, b):
    return lax.dot_general(a, b, (((1,), (1,)), ((), ())), preferred_element_type=F32)


def _dot_tn(a, b):
    return lax.dot_general(a, b, (((0,), (0,)), ((), ())), preferred_element_type=F32)


def _row_stats(x):
    mu = jnp.mean(x, axis=-1, keepdims=True)
    xc = x - mu
    var = jnp.mean(xc * xc, axis=-1, keepdims=True)
    rstd = lax.rsqrt(var + LN_EPS)
    return xc * rstd, rstd


def _ln_bwd(dxhat, xhat, rstd):
    m1 = jnp.mean(dxhat, axis=-1, keepdims=True)
    m2 = jnp.mean(dxhat * xhat, axis=-1, keepdims=True)
    return rstd * (dxhat - m1 - xhat * m2)


def _masked_sgu_w(sw_ref, hh):
    row = lax.broadcasted_iota(jnp.int32, (CHUNK, CHUNK), 0)
    col = lax.broadcasted_iota(jnp.int32, (CHUNK, CHUNK), 1)
    return jnp.where(row >= col, sw_ref[hh], 0.0)


def _inv_count(tile_index, tok_tile, window):
    tok = tile_index * tok_tile + lax.broadcasted_iota(jnp.int32, (tok_tile, 1), 0)
    return 1.0 / jnp.minimum(tok + 1, window).astype(F32)


def _params(**kw):
    return pltpu.CompilerParams(vmem_limit_bytes=VMEM_LIMIT, **kw)


def _forward_mixers(x, w_in, pool_w, pool_scale, sgu_ln_g, sgu_ln_b, sgu_w, sgu_bias_tile, tok_tile, sides=()):
    seq = x.shape[0]
    n_tiles = seq // tok_tile
    n_chunks = tok_tile // CHUNK
    side_refs = _SideRefs(sides, 8, 4, 2)

    def body(*refs):
        (x_ref, win_ref, pw_ref, ps_ref, lg_ref, lb_ref, sw_ref, sb_ref,
         hb_ref, y_ref, pooled_ref, xb_ref, aext_ref, h_ref) = side_refs.split(refs)
        i = pl.program_id(0)
        side_refs.emit(i == 0, i == n_tiles // 2, False)

        @pl.when(i == 0)
        def _():
            aext_ref[0:HALO, :] = jnp.zeros((HALO, D_POOL), F32)

        xb = x_ref[...].astype(BF16)
        xb_ref[...] = xb
        for s in range(D_IN // 1024):
            cs = slice(s * 1024, (s + 1) * 1024)
            section = _dot(xb, win_ref[:, cs])
            h_ref[:, cs] = section
            if s >= 1:
                hb_ref[:, (s - 1) * 1024:s * 1024] = section.astype(BF16)

        aext_ref[HALO:HALO + tok_tile, :] = h_ref[:, 0:D_POOL]
        for g, window in enumerate(POOL_WINDOWS):
            cols = slice(g * POOL_GROUP, (g + 1) * POOL_GROUP)
            win = aext_ref[HALO:HALO + tok_tile, cols]
            for k in range(1, window):
                win = win + aext_ref[HALO - k:HALO - k + tok_tile, cols]
            pooled = win * _inv_count(i, tok_tile, window) - h_ref[:, cols]
            pb = pooled.astype(BF16)
            pooled_ref[:, cols] = pb
            mixed = _dot(pb, pw_ref[g])
            z = h_ref[:, Z0 + g * POOL_GROUP:Z0 + (g + 1) * POOL_GROUP]
            y_ref[:, cols] = (mixed * ps_ref[:, cols] * (z * _sigmoid(z))).astype(BF16)
        aext_ref[0:HALO, :] = aext_ref[tok_tile:tok_tile + HALO, :]

        for hh in range(N_HEADS):
            cols = slice(hh * HEAD, (hh + 1) * HEAD)
            swm = _masked_sgu_w(sw_ref, hh).astype(BF16)
            for n in range(n_chunks):
                rows = slice(n * CHUNK, (n + 1) * CHUNK)
                gu = _gelu(h_ref[rows, U0 + hh * HEAD:U0 + (hh + 1) * HEAD])
                gv = _gelu(h_ref[rows, V0 + hh * HEAD:V0 + (hh + 1) * HEAD])
                xhat, _ = _row_stats(gv)
                vln = xhat * lg_ref[:, cols] + lb_ref[:, cols]
                sv = _dot(swm, vln.astype(BF16)) + sb_ref[:, cols]
                z = h_ref[rows, Z0 + D_POOL + hh * HEAD:Z0 + D_POOL + (hh + 1) * HEAD]
                y_ref[rows, D_POOL + hh * HEAD:D_POOL + (hh + 1) * HEAD] = (
                    gu * sv * (z * _sigmoid(z))).astype(BF16)

        side_refs.emit(False, False, i == n_tiles - 1)

    tok = lambda width: pl.BlockSpec((tok_tile, width), lambda i: (i, 0))
    outs = pl.pallas_call(
        body, name="forward_mixers",
        grid=(n_tiles,),
        in_specs=[tok(D_MODEL)] + [VMEM_FULL] * 7 + side_refs.in_specs,
        out_specs=[tok(D_IN - D_POOL), tok(D_MIX), tok(D_POOL), tok(D_MODEL)] + side_refs.out_specs,
        out_shape=[jax.ShapeDtypeStruct((seq, D_IN - D_POOL), BF16), jax.ShapeDtypeStruct((seq, D_MIX), BF16),
                   jax.ShapeDtypeStruct((seq, D_POOL), BF16), jax.ShapeDtypeStruct((seq, D_MODEL), BF16)]
        + side_refs.out_shapes,
        scratch_shapes=[pltpu.VMEM((HALO + tok_tile, D_POOL), F32), pltpu.VMEM((tok_tile, D_IN), F32)]
        + side_refs.scratch,
        compiler_params=_params(dimension_semantics=("arbitrary",)),
    )(x, w_in, pool_w, pool_scale, sgu_ln_g, sgu_ln_b, sgu_w, sgu_bias_tile, *side_refs.inputs)
    return outs[:4], outs[4:]


def _head_fwd_bwd(x, y, p, target, w_out, gate_w, ple_w, ln_g, ln_b, gate_b, tok_tile, sub_tile):
    seq = x.shape[0]

    def body(x_ref, y_ref, p_ref, t_ref, wout_ref, gw_ref, plw_ref, lng_ref, lnb_ref, gb_ref,
             dy_ref, dxr_ref, xn_ref, dgp_ref, dpe_ref, dr_ref, loss_ref, dlng_ref, dlnb_ref, dgb_ref):
        i = pl.program_id(0)

        @pl.when(i == 0)
        def _():
            loss_ref[...] = jnp.zeros_like(loss_ref)
            dlng_ref[...] = jnp.zeros_like(dlng_ref)
            dlnb_ref[...] = jnp.zeros_like(dlnb_ref)
            dgb_ref[...] = jnp.zeros_like(dgb_ref)

        subs = [slice(s * sub_tile, (s + 1) * sub_tile) for s in range(tok_tile // sub_tile)]
        stats, xns, douts = [], [], []
        for rows in subs:
            r = ALPHA * x_ref[rows, :] + _dot(y_ref[rows, :], wout_ref[...])
            xhat, rstd = _row_stats(r)
            xn = xhat * lng_ref[...] + lnb_ref[...]
            xn_ref[rows, :] = xn.astype(BF16)
            stats.append((xhat, rstd))
            xns.append(xn)
        loss = jnp.zeros((1, 1), F32)
        dgb = jnp.zeros((1, D_MODEL), F32)
        for rows, xn in zip(subs, xns):
            gate = _sigmoid(_dot(xn_ref[rows, :], gw_ref[...]) + gb_ref[...])
            pe = _dot(p_ref[rows, :].astype(BF16), plw_ref[...])
            err = xn + gate * pe - t_ref[rows, :]
            loss = loss + jnp.sum(err * err, keepdims=True)
            dout = err * (1.0 / D_MODEL)
            dpe_ref[rows, :] = (dout * gate).astype(BF16)
            dgpre = dout * pe * gate * (1.0 - gate)
            dgb = dgb + jnp.sum(dgpre, axis=0, keepdims=True)
            dgp_ref[rows, :] = dgpre.astype(BF16)
            douts.append(dout)
        loss_ref[...] += (0.5 / D_MODEL) * loss
        dgb_ref[...] += dgb
        dlng = jnp.zeros((1, D_MODEL), F32)
        dlnb = jnp.zeros((1, D_MODEL), F32)
        for rows, (xhat, rstd), dout in zip(subs, stats, douts):
            dxn = dout + _dot_nt(dgp_ref[rows, :], gw_ref[...])
            dlng = dlng + jnp.sum(dxn * xhat, axis=0, keepdims=True)
            dlnb = dlnb + jnp.sum(dxn, axis=0, keepdims=True)
            dr = _ln_bwd(dxn * lng_ref[...], xhat, rstd)
            dxr_ref[rows, :] = ALPHA * dr
            dr_ref[rows, :] = dr.astype(BF16)
        dlng_ref[...] += dlng
        dlnb_ref[...] += dlnb
        for rows in subs:
            dy_ref[rows, :] = _dot_nt(dr_ref[rows, :], wout_ref[...])

    tok = lambda width: pl.BlockSpec((tok_tile, width), lambda i: (i, 0))
    acc = lambda width: pl.BlockSpec((1, width), lambda i: (0, 0))
    vec = jax.ShapeDtypeStruct((1, D_MODEL), F32)
    return pl.pallas_call(
        body, name="head_fwd_bwd",
        grid=(seq // tok_tile,),
        in_specs=[tok(D_MODEL), tok(D_MIX), tok(D_PLE), tok(D_MODEL),
                  VMEM_FULL, VMEM_FULL, VMEM_FULL, VMEM_FULL, VMEM_FULL, VMEM_FULL],
        out_specs=[tok(D_MIX), tok(D_MODEL), tok(D_MODEL), tok(D_MODEL), tok(D_MODEL), tok(D_MODEL),
                   acc(128), acc(D_MODEL), acc(D_MODEL), acc(D_MODEL)],
        out_shape=[jax.ShapeDtypeStruct((seq, D_MIX), F32), jax.ShapeDtypeStruct((seq, D_MODEL), F32),
                   jax.ShapeDtypeStruct((seq, D_MODEL), BF16), jax.ShapeDtypeStruct((seq, D_MODEL), BF16),
                   jax.ShapeDtypeStruct((seq, D_MODEL), BF16), jax.ShapeDtypeStruct((seq, D_MODEL), BF16),
                   jax.ShapeDtypeStruct((1, 128), F32), vec, vec, vec],
        compiler_params=_params(dimension_semantics=("arbitrary",)),
    )(x, y, p, target, w_out, gate_w, ple_w, ln_g, ln_b, gate_b)


def _mixers_bwd(h, dy, dxr, pooled, w_in, pool_w, pool_scale, sgu_ln_g, sgu_ln_b, sgu_w, sgu_bias_tile, tok_tile):
    seq = h.shape[0]
    n_tiles = seq // tok_tile
    n_chunks = tok_tile // CHUNK
    pool_rows = POOL_GROUP // N_DEV

    def body(h_ref, dy_ref, dxr_ref, pooled_ref, win_ref, pw_ref, ps_ref, lg_ref, lb_ref, sw_ref, sb_ref, band_ref,
             dh_ref, dx_ref, dpw_ref, dps_ref, dlg_ref, dlb_ref, dsw_ref, dsb_ref,
             qhi_ref, qlo_ref, dpw_acc, dsb_acc):
        i = pl.program_id(0)
        tile = n_tiles - 1 - i

        @pl.when(i == 0)
        def _():
            qhi_ref[...] = jnp.zeros_like(qhi_ref)
            qlo_ref[...] = jnp.zeros_like(qlo_ref)
            dpw_acc[...] = jnp.zeros_like(dpw_acc)
            dsb_acc[...] = jnp.zeros_like(dsb_acc)
            dps_ref[...] = jnp.zeros_like(dps_ref)
            dlg_ref[...] = jnp.zeros_like(dlg_ref)
            dlb_ref[...] = jnp.zeros_like(dlb_ref)
            dsw_ref[...] = jnp.zeros_like(dsw_ref)

        def h_at(rows, cols):
            return h_ref[rows, cols.start - D_POOL:cols.stop - D_POOL].astype(F32)

        everything = slice(0, tok_tile)
        for part in (qhi_ref, qlo_ref):
            part[tok_tile:tok_tile + HALO, :] = part[0:HALO, :]
        for g, window in enumerate(POOL_WINDOWS):
            cols = slice(g * POOL_GROUP, (g + 1) * POOL_GROUP)
            zcols = slice(Z0 + g * POOL_GROUP, Z0 + (g + 1) * POOL_GROUP)
            z = h_at(everything, zcols)
            sz = _sigmoid(z)
            pb = pooled_ref[:, cols]
            mixed = _dot(pb, pw_ref[g])
            dyp = dy_ref[:, cols]
            dh_ref[:, zcols] = (dyp * (mixed * ps_ref[:, cols]) * (sz * (1.0 + z * (1.0 - sz)))).astype(BF16)
            dms = dyp * (z * sz)
            dps_ref[:, cols] += jnp.sum(dms * mixed, axis=0, keepdims=True)
            dmixed = (dms * ps_ref[:, cols]).astype(BF16)
            dpw_acc[g] += _dot_tn(pb, dmixed)
            dpooled = _dot_nt(dmixed, pw_ref[g])
            qhi_ref[everything, cols], qlo_ref[everything, cols] = _split_bf16(
                dpooled * _inv_count(tile, tok_tile, window))
            da = _dot(band_ref[g], qhi_ref[:, cols]) + _dot(band_ref[g], qlo_ref[:, cols]) - dpooled
            dh_ref[:, cols] = da.astype(BF16)

        pool_cols = [slice(o + g * POOL_GROUP, o + (g + 1) * POOL_GROUP)
                     for o in (0, Z0) for g in range(len(POOL_WINDOWS))]
        pool_slices = [pool_cols[0:3], pool_cols[3:6], pool_cols[6:8], []]
        for hh in range(N_HEADS):
            cols = slice(hh * HEAD, (hh + 1) * HEAD)
            ucols = slice(U0 + hh * HEAD, U0 + (hh + 1) * HEAD)
            vcols = slice(V0 + hh * HEAD, V0 + (hh + 1) * HEAD)
            zcols = slice(Z0 + D_POOL + hh * HEAD, Z0 + D_POOL + (hh + 1) * HEAD)
            sw32 = _masked_sgu_w(sw_ref, hh)
            swm = sw32.astype(BF16)
            swm_t = sw32.T.astype(BF16)
            for n in range(n_chunks):
                rows = slice(n * CHUNK, (n + 1) * CHUNK)
                gu, dgu_du = _gelu_and_grad(h_at(rows, ucols))
                gv, dgv_dv = _gelu_and_grad(h_at(rows, vcols))
                xhat, rstd = _row_stats(gv)
                vb = (xhat * lg_ref[:, cols] + lb_ref[:, cols]).astype(BF16)
                sv = _dot(swm, vb) + sb_ref[:, cols]
                z = h_at(rows, zcols)
                sz = _sigmoid(z)
                dys = dy_ref[rows, D_POOL + hh * HEAD:D_POOL + (hh + 1) * HEAD]
                dh_ref[rows, zcols] = (dys * (gu * sv) * (sz * (1.0 + z * (1.0 - sz)))).astype(BF16)
                dyg = dys * (z * sz)
                dh_ref[rows, ucols] = (dyg * sv * dgu_du).astype(BF16)
                dsv = dyg * gu
                dsb_acc[:, cols] += dsv
                dsvb = dsv.astype(BF16)
                dsw_ref[hh] += _dot_nt(dsvb, vb)
                dvln = _dot(swm_t, dsvb)
                dlg_ref[:, cols] += jnp.sum(dvln * xhat, axis=0, keepdims=True)
                dlb_ref[:, cols] += jnp.sum(dvln, axis=0, keepdims=True)
                dgv = _ln_bwd(dvln * lg_ref[:, cols], xhat, rstd)
                dh_ref[rows, vcols] = (dgv * dgv_dv).astype(BF16)
            ready = [ucols, vcols, zcols] + pool_slices[hh]
            part = _dot_nt(dh_ref[:, ready[0]], win_ref[:, ready[0]])
            for sl in ready[1:]:
                part = part + _dot_nt(dh_ref[:, sl], win_ref[:, sl])
            if hh == 0:
                dx_ref[...] = dxr_ref[...] + part
            else:
                dx_ref[...] += part

        @pl.when(i == n_tiles - 1)
        def _():
            for g in range(len(POOL_WINDOWS)):
                for b in range(N_DEV):
                    dpw_ref[b, g] = dpw_acc[g, b * pool_rows:(b + 1) * pool_rows, :]
            row = lax.broadcasted_iota(jnp.int32, (CHUNK, CHUNK), 0)
            col = lax.broadcasted_iota(jnp.int32, (CHUNK, CHUNK), 1)
            for hh in range(N_HEADS):
                dsw_ref[hh] = jnp.where(row >= col, dsw_ref[hh], 0.0)
                total = jnp.sum(dsb_acc[:, hh * HEAD:(hh + 1) * HEAD], axis=1, keepdims=True)
                dsb_ref[hh:hh + 1, :] = jnp.broadcast_to(total, (CHUNK, CHUNK)).T[0:1, :]

    tok = lambda width: pl.BlockSpec((tok_tile, width), lambda i: (n_tiles - 1 - i, 0))
    whole = lambda shape: pl.BlockSpec(shape, lambda i: (0,) * len(shape))
    vec = jax.ShapeDtypeStruct((1, D_MODEL), F32)
    return pl.pallas_call(
        body, name="mixers_bwd",
        grid=(n_tiles,),
        in_specs=[tok(D_IN - D_POOL), tok(D_MIX), tok(D_MODEL), tok(D_POOL)] + [VMEM_FULL] * 8,
        out_specs=[tok(D_IN), tok(D_MODEL), whole((N_DEV, len(POOL_WINDOWS), pool_rows, POOL_GROUP)),
                   whole((1, D_POOL)), whole((1, D_SGU)), whole((1, D_SGU)),
                   whole((N_HEADS, CHUNK, CHUNK)), whole((N_HEADS, CHUNK))],
        out_shape=[jax.ShapeDtypeStruct((seq, D_IN), BF16), jax.ShapeDtypeStruct((seq, D_MODEL), F32),
                   jax.ShapeDtypeStruct((N_DEV, len(POOL_WINDOWS), pool_rows, POOL_GROUP), F32),
                   vec, vec, vec,
                   jax.ShapeDtypeStruct((N_HEADS, CHUNK, CHUNK), F32),
                   jax.ShapeDtypeStruct((N_HEADS, CHUNK), F32)],
        scratch_shapes=[pltpu.VMEM((tok_tile + BAND_PAD, D_POOL), BF16),
                        pltpu.VMEM((tok_tile + BAND_PAD, D_POOL), BF16),
                        pltpu.VMEM((len(POOL_WINDOWS), POOL_GROUP, POOL_GROUP), F32),
                        pltpu.VMEM((CHUNK, D_SGU), F32)],
        compiler_params=_params(dimension_semantics=("arbitrary",)),
    )(h, dy, dxr, pooled, w_in, pool_w, pool_scale, sgu_ln_g, sgu_ln_b, sgu_w, sgu_bias_tile, _bands(tok_tile, True))


def _weight_grad(a, b, n_col_blocks, blocks_per_step, tok_tile, name, sides=()):
    seq, m = a.shape
    n = b.shape[1]
    nb = n // n_col_blocks
    n_steps = n_col_blocks // blocks_per_step
    n_k = seq // tok_tile
    side_refs = _SideRefs(sides, 2, 1, 0)

    def body(*refs):
        a_ref, b_ref, out_ref = side_refs.split(refs)
        step = pl.program_id(0) * n_k + pl.program_id(1)
        side_refs.emit(step == 0, step == (n_steps * n_k) // 2, False)

        @pl.when(pl.program_id(1) == 0)
        def _():
            out_ref[...] = jnp.zeros_like(out_ref)

        res = _dot_tn(a_ref[...].astype(BF16), b_ref[...])
        for blk in range(blocks_per_step):
            out_ref[blk] += res[:, blk * nb:(blk + 1) * nb]

        side_refs.emit(False, False, step == n_steps * n_k - 1)

    outs = pl.pallas_call(
        body, name=name,
        grid=(n_steps, n_k),
        in_specs=[pl.BlockSpec((tok_tile, m), lambda j, k: (k, 0)),
                  pl.BlockSpec((tok_tile, blocks_per_step * nb), lambda j, k: (k, j))] + side_refs.in_specs,
        out_specs=[pl.BlockSpec((blocks_per_step, m, nb), lambda j, k: (j, 0, 0))] + side_refs.out_specs,
        out_shape=[jax.ShapeDtypeStruct((n_col_blocks, m, nb), F32)] + side_refs.out_shapes,
        scratch_shapes=side_refs.scratch,
        compiler_params=_params(dimension_semantics=("arbitrary", "arbitrary")),
    )(a, b, *side_refs.inputs)
    return outs[0], outs[1:]


class _Side(NamedTuple):
    inputs: list
    out_shapes: list
    sem_shapes: list
    emit: Callable


def _when(cond):
    if cond is True:
        return lambda f: f()
    if cond is False:
        return lambda f: None
    return pl.when(cond)


def _place():
    return lax.axis_index("x"), lax.axis_index("y"), lax.axis_index("c")


def _other_chips(x, y):
    return [(1 - x, y), (x, 1 - y), (1 - x, 1 - y)]


def _gather_side(shards, out_shapes, views, halves):
    n = len(shards)
    n_sems = 10

    def emit(ins, outs, sems, first, mid, last):
        send_sems, recv_sems, local_sems = sems
        x, y, c = _place()
        here, x_nbr, y_nbr, diag = (x, y), (1 - x, y), (x, 1 - y), (1 - x, 1 - y)
        sibling = (x, y, 1 - c)

        def block(k, chip, core):
            return views[k](outs[k], 4 * chip[0] + 2 * chip[1] + core)

        def piece(k, ref, p):
            return ref if halves[k] is None else halves[k](ref)[p]

        def n_pieces(k):
            return 1 if halves[k] is None else 2

        def copy(k, s, src, dst, to):
            return pltpu.make_async_remote_copy(
                src_ref=src, dst_ref=dst, send_sem=send_sems.at[k, s], recv_sem=recv_sems.at[k, s],
                device_id=to, device_id_type=MESH)

        def outgoing(k, s):
            mine = block(k, here, c)
            if s == 0:
                return copy(k, 0, ins[k], mine, sibling)
            if s in (1, 2):
                return copy(k, s, piece(k, ins[k], s - 1), piece(k, mine, s - 1), (*x_nbr, c))
            if s in (3, 4):
                return copy(k, s, piece(k, ins[k], s - 3), piece(k, mine, s - 3), (*y_nbr, c))
            if s == 5:
                part = piece(k, block(k, x_nbr, c), 0)
                return copy(k, 5, part, part, (*y_nbr, c))
            if s == 6:
                part = piece(k, block(k, y_nbr, c), 1)
                return copy(k, 6, part, part, (*x_nbr, c))
            whole = block(k, (x_nbr, y_nbr, diag)[s - 7], c)
            return copy(k, s, whole, whole, sibling)

        def incoming(k, s):
            if s == 0:
                zone = block(k, here, 1 - c)
            elif s in (1, 2):
                zone = piece(k, block(k, x_nbr, c), s - 1)
            elif s in (3, 4):
                zone = piece(k, block(k, y_nbr, c), s - 3)
            elif s in (5, 6):
                zone = piece(k, block(k, diag, c), s - 5)
            else:
                zone = block(k, (x_nbr, y_nbr, diag)[s - 7], 1 - c)
            return copy(k, s, zone, zone, sibling)

        def own(k):
            return pltpu.make_async_copy(ins[k], block(k, here, c), local_sems.at[k])

        def used(k):
            return list(range(n_sems)) if n_pieces(k) == 2 else [0, 1, 3, 5, 7, 8, 9]

        @_when(first)
        def _():
            for k in range(n):
                own(k).start()
                for s in used(k):
                    if s <= 4:
                        outgoing(k, s).start()

        @_when(mid)
        def _():
            for k in range(n):
                incoming(k, 1).wait_recv()
                outgoing(k, 5).start()
                if n_pieces(k) == 2:
                    incoming(k, 4).wait_recv()
                    outgoing(k, 6).start()
                    incoming(k, 2).wait_recv()
                outgoing(k, 7).start()
                incoming(k, 3).wait_recv()
                outgoing(k, 8).start()

        @_when(last)
        def _():
            for k in range(n):
                incoming(k, 5).wait_recv()
                if n_pieces(k) == 2:
                    incoming(k, 6).wait_recv()
                outgoing(k, 9).start()
            for k in range(n):
                for s in (0, 7, 8, 9):
                    incoming(k, s).wait_recv()
            for k in range(n):
                for s in used(k):
                    outgoing(k, s).wait_send()
                own(k).wait()

    sems = [pltpu.SemaphoreType.DMA((n, n_sems)), pltpu.SemaphoreType.DMA((n, n_sems)),
            pltpu.SemaphoreType.DMA((n,))]
    return _Side(list(shards), list(out_shapes), sems, emit)


def _pair_side(grads):
    n = len(grads)

    def emit(ins, theirs, sems, first, mid, last):
        send_sems, recv_sems = sems
        x, y, c = _place()

        def copies():
            return [pltpu.make_async_remote_copy(
                src_ref=ins[k].at[2 * j + (1 - c)], dst_ref=theirs[k].at[j],
                send_sem=send_sems.at[k, j], recv_sem=recv_sems.at[k, j],
                device_id=(x, y, 1 - c), device_id_type=MESH) for k in range(n) for j in range(4)]

        @_when(first)
        def _():
            for cp in copies():
                cp.start()

        @_when(last)
        def _():
            for cp in copies():
                cp.wait_recv()
            for cp in copies():
                cp.wait_send()

    shapes = [jax.ShapeDtypeStruct((4,) + g.shape[1:], g.dtype) for g in grads]
    return _Side(list(grads), shapes, [pltpu.SemaphoreType.DMA((n, 4)), pltpu.SemaphoreType.DMA((n, 4))], emit)


def _chip_side(sums):
    n = len(sums)

    def emit(ins, others, sems, first, mid, last):
        send_sems, recv_sems = sems
        x, y, c = _place()

        def copies():
            return [pltpu.make_async_remote_copy(
                src_ref=ins[k].at[2 * px + py], dst_ref=others[k].at[r],
                send_sem=send_sems.at[k, r], recv_sem=recv_sems.at[k, r],
                device_id=(px, py, c), device_id_type=MESH)
                for k in range(n) for r, (px, py) in enumerate(_other_chips(x, y))]

        @_when(first)
        def _():
            for cp in copies():
                cp.start()

        @_when(last)
        def _():
            for cp in copies():
                cp.wait_recv()
            for cp in copies():
                cp.wait_send()

    shapes = [jax.ShapeDtypeStruct((3,) + s.shape[1:], s.dtype) for s in sums]
    return _Side(list(sums), shapes, [pltpu.SemaphoreType.DMA((n, 3)), pltpu.SemaphoreType.DMA((n, 3))], emit)


def _comm_call(side, name):
    n_in, n_out = len(side.inputs), len(side.out_shapes)

    def body(*refs):
        side.emit(refs[:n_in], refs[n_in:n_in + n_out], refs[n_in + n_out:], True, True, True)

    return pl.pallas_call(
        body, name=name, in_specs=[ANY] * n_in, out_specs=[ANY] * n_out,
        out_shape=side.out_shapes, scratch_shapes=side.sem_shapes,
    )(*side.inputs)


class _SideRefs:
    def __init__(self, sides, n_in, n_out, n_scratch):
        self.sides, self.n_in, self.n_out, self.n_scratch = sides, n_in, n_out, n_scratch
        self.inputs = [a for s in sides for a in s.inputs]
        self.out_shapes = [o for s in sides for o in s.out_shapes]
        self.scratch = [m for s in sides for m in s.sem_shapes]
        self.in_specs = [ANY] * len(self.inputs)
        self.out_specs = [ANY] * len(self.out_shapes)

    def split(self, refs):
        refs = list(refs)
        n_side_in, n_side_out = len(self.inputs), len(self.out_shapes)
        ins, rest = refs[:self.n_in], refs[self.n_in:]
        side_in, rest = rest[:n_side_in], rest[n_side_in:]
        outs, rest = rest[:self.n_out], rest[self.n_out:]
        side_out, rest = rest[:n_side_out], rest[n_side_out:]
        scratch, side_sems = rest[:self.n_scratch], rest[self.n_scratch:]
        self._refs = (side_in, side_out, side_sems)
        return ins + outs + scratch

    def emit(self, first, mid, last):
        side_in, side_out, side_sems = self._refs
        for s in self.sides:
            a, b, m = len(s.inputs), len(s.out_shapes), len(s.sem_shapes)
            s.emit(side_in[:a], side_out[:b], side_sems[:m], first, mid, last)
            side_in, side_out, side_sems = side_in[a:], side_out[b:], side_sems[m:]


def _grad_w_in_reduced(order, a, b, tok_tile, sides=()):
    seq, m = a.shape
    nb = b.shape[1] // N_DEV
    n_k = seq // tok_tile
    n_rows = 4
    last_step = n_rows * n_k - 1
    assert n_k >= 3
    side_refs = _SideRefs(sides, 3, 3, 8)

    def body(*refs):
        (order_ref, a_ref, b_ref, own_ref, theirs_ref, others_ref,
         acc_ref, stage_ref, sumbf_ref, pair_send, pair_recv, ici_send, ici_recv, stage_sem) = side_refs.split(refs)
        j, k = pl.program_id(0), pl.program_id(1)
        step = j * n_k + k
        side_refs.emit(step == 0, step == (n_rows * n_k) // 2, False)
        x, y, c = _place()
        chips = _other_chips(x, y)

        def to_sibling(row):
            return pltpu.make_async_remote_copy(
                src_ref=acc_ref.at[row % 2, 1 - c], dst_ref=theirs_ref.at[row],
                send_sem=pair_send.at[row], recv_sem=pair_recv.at[row],
                device_id=(x, y, 1 - c), device_id_type=MESH)

        def to_owner(row):
            px, py = chips[row]
            return pltpu.make_async_remote_copy(
                src_ref=sumbf_ref.at[row], dst_ref=others_ref.at[row],
                send_sem=ici_send.at[row], recv_sem=ici_recv.at[row],
                device_id=(px, py, c), device_id_type=MESH)

        def staged(row):
            return pltpu.make_async_copy(theirs_ref.at[row], stage_ref, stage_sem.at[0])

        @pl.when(k == 0)
        def _():
            acc_ref[j % 2] = jnp.zeros((2, m, nb), F32)

        res = _dot_tn(a_ref[...].astype(BF16), b_ref[...])
        for blk in range(2):
            acc_ref[j % 2, blk] += res[:, blk * nb:(blk + 1) * nb]

        for row in range(n_rows):
            @pl.when((j == row) & (k == n_k - 1))
            def _():
                to_sibling(row).start()

            if row < n_rows - 1:
                @pl.when((j == row + 1) & (k == 1))
                def _():
                    to_sibling(row).wait_recv()
                    staged(row).start()

                @pl.when((j == row + 1) & (k == 2))
                def _():
                    staged(row).wait()
                    to_sibling(row).wait_send()
                    sumbf_ref[row] = (acc_ref[row % 2, c] + stage_ref[...]).astype(BF16)
                    to_owner(row).start()

        @pl.when(step == last_step)
        def _():
            row = n_rows - 1
            to_sibling(row).wait_recv()
            staged(row).start()
            staged(row).wait()
            to_sibling(row).wait_send()
            own_ref[...] = acc_ref[row % 2, c] + stage_ref[...]
            for r in range(n_rows - 1):
                to_owner(r).wait_recv()
                to_owner(r).wait_send()

        side_refs.emit(False, False, step == last_step)

    block = jax.ShapeDtypeStruct((m, nb), F32)
    outs = pl.pallas_call(
        body, name="grad_w_in",
        grid_spec=pltpu.PrefetchScalarGridSpec(
            num_scalar_prefetch=1, grid=(n_rows, n_k),
            in_specs=[pl.BlockSpec((tok_tile, m), lambda j, k, order_ref: (k, 0)),
                      pl.BlockSpec((tok_tile, 2 * nb), lambda j, k, order_ref: (k, order_ref[j]))]
            + side_refs.in_specs,
            out_specs=[pl.BlockSpec((m, nb), lambda j, k, order_ref: (0, 0)), ANY, ANY] + side_refs.out_specs,
            scratch_shapes=[pltpu.VMEM((2, 2, m, nb), F32), pltpu.VMEM((m, nb), F32),
                            pltpu.VMEM((n_rows - 1, m, nb), BF16),
                            pltpu.SemaphoreType.DMA((n_rows,)), pltpu.SemaphoreType.DMA((n_rows,)),
                            pltpu.SemaphoreType.DMA((n_rows - 1,)), pltpu.SemaphoreType.DMA((n_rows - 1,)),
                            pltpu.SemaphoreType.DMA((1,))] + side_refs.scratch),
        out_shape=[block, jax.ShapeDtypeStruct((n_rows, m, nb), F32),
                   jax.ShapeDtypeStruct((n_rows - 1, m, nb), BF16)] + side_refs.out_shapes,
        compiler_params=_params(dimension_semantics=("arbitrary", "arbitrary")),
    )(order, a, b, *side_refs.inputs)
    return outs[0], outs[2], outs[3:]


def _row_tile(rows, cols):
    tile = rows
    while tile * cols > 256 * 1024 and tile % 16 == 0:
        tile //= 2
    return tile


def _pair_sum(core, grads, theirs, name):
    _, rows, cols = theirs.shape
    rt = _row_tile(rows, cols)

    def body(core_ref, a_ref, b_ref, o_ref, ob_ref):
        total = a_ref[...] + b_ref[...]
        o_ref[...] = total
        ob_ref[...] = total.astype(BF16)

    spec = pl.BlockSpec((None, rt, cols), lambda j, i, core_ref: (j, i, 0))
    mine = pl.BlockSpec((None, None, rt, cols), lambda j, i, core_ref: (j, core_ref[0], i, 0))
    return pl.pallas_call(
        body, name=name,
        grid_spec=pltpu.PrefetchScalarGridSpec(
            num_scalar_prefetch=1, grid=(4, rows // rt), in_specs=[mine, spec], out_specs=[spec, spec]),
        out_shape=[jax.ShapeDtypeStruct(theirs.shape, F32), jax.ShapeDtypeStruct(theirs.shape, BF16)],
        compiler_params=_params(dimension_semantics=("arbitrary", "arbitrary")),
    )(core, grads.reshape(4, 2, rows, cols), theirs)


def _adamw(w, g, m, v):
    m = ADAM_B1 * m + (1.0 - ADAM_B1) * g
    v = ADAM_B2 * v + (1.0 - ADAM_B2) * (g * g)
    m_hat = m / (1.0 - ADAM_B1 ** ADAM_STEP)
    v_hat = v / (1.0 - ADAM_B2 ** ADAM_STEP)
    delta = -ADAM_LR * (m_hat / (jnp.sqrt(v_hat) + ADAM_EPS) + ADAM_WD * w)
    return delta, m, v


def _sum_adamw(chip, sums, others, w, m, v, name):
    _, rows, cols = sums.shape
    rt = _row_tile(rows, cols)

    def body(chip_ref, own_ref, oth_ref, w_ref, m_ref, v_ref, g_ref, d_ref, nm_ref, nv_ref):
        g = ((own_ref[...] + oth_ref[0].astype(F32)) + oth_ref[1].astype(F32)) + oth_ref[2].astype(F32)
        g_ref[...] = g
        d_ref[...], nm_ref[...], nv_ref[...] = _adamw(w_ref[...], g, m_ref[...], v_ref[...])

    spec = pl.BlockSpec((rt, cols), lambda i, chip_ref: (i, 0))
    own = pl.BlockSpec((None, rt, cols), lambda i, chip_ref: (chip_ref[0], i, 0))
    shape = jax.ShapeDtypeStruct((rows, cols), F32)
    return pl.pallas_call(
        body, name=name,
        grid_spec=pltpu.PrefetchScalarGridSpec(
            num_scalar_prefetch=1, grid=(rows // rt,),
            in_specs=[own, pl.BlockSpec((3, rt, cols), lambda i, chip_ref: (0, i, 0)), spec, spec, spec],
            out_specs=[spec] * 4),
        out_shape=[shape] * 4,
        compiler_params=_params(dimension_semantics=("arbitrary",)),
    )(chip, sums, others, w, m, v)


def _small_sum_adamw(gathered, ws, ms, vs):
    n = len(ws)

    def body(*refs):
        g8 = refs[:n + 1]
        w, m, v = refs[n + 1:2 * n + 1], refs[2 * n + 1:3 * n + 1], refs[3 * n + 1:4 * n + 1]
        outs = refs[4 * n + 1:]
        g_out, d_out, m_out, v_out = outs[:n + 1], outs[n + 1:2 * n + 1], outs[2 * n + 1:3 * n + 1], outs[3 * n + 1:]
        for k in range(n + 1):
            g = g8[k][0]
            for b in range(1, N_DEV):
                g = g + g8[k][b]
            g_out[k][...] = g
            if k < n:
                d_out[k][...], m_out[k][...], v_out[k][...] = _adamw(w[k][...], g, m[k][...], v[k][...])

    shapes = [jax.ShapeDtypeStruct(w.shape, F32) for w in ws]
    loss_shape = jax.ShapeDtypeStruct(gathered[-1].shape[1:], F32)
    outs = pl.pallas_call(
        body, name="small_sum_adamw",
        in_specs=[VMEM_FULL] * (4 * n + 1), out_specs=[VMEM_FULL] * (4 * n + 1),
        out_shape=shapes + [loss_shape] + shapes * 3,
        compiler_params=_params(),
    )(*gathered, *ws, *ms, *vs)
    return outs[:n + 1], outs[n + 1:2 * n + 1], outs[2 * n + 1:3 * n + 1], outs[3 * n + 1:]


SMALL_NAMES = ("pool_scale", "sgu_ln_g", "sgu_ln_b", "sgu_w", "sgu_b", "ln_g", "ln_b", "ple_gate_b")


TOK_TILE = 256
GRAD_TILE = 1024


def _weight_views():
    cols = lambda width: (lambda ref, b: ref.at[:, pl.ds(pl.multiple_of(b * width, 128), width)])
    rows = lambda height: (lambda ref, b: ref.at[pl.ds(pl.multiple_of(b * height, 16), height), :])
    pool_rows = POOL_GROUP // N_DEV
    return {"w_in": cols(D_IN // N_DEV),
            "pool_w": lambda ref, b: ref.at[:, pl.ds(pl.multiple_of(b * pool_rows, 16), pool_rows), :],
            "w_out": rows(D_MIX // N_DEV), "ple_w": cols(D_MODEL // N_DEV), "gate_w": rows(D_MODEL // N_DEV)}


WEIGHT_SHAPES = {"w_in": (D_MODEL, D_IN), "pool_w": (len(POOL_WINDOWS), POOL_GROUP, POOL_GROUP),
                 "w_out": (D_MIX, D_MODEL), "ple_w": (D_PLE, D_MODEL), "gate_w": (D_MODEL, D_MODEL)}


def _leading_halves(shape):
    whole_tiles = len(shape) >= 3 or shape[0] % 32 == 0
    if shape[0] % 2 or not whole_tiles:
        return None
    half = shape[0] // 2
    return lambda ref: (ref.at[pl.ds(0, half)], ref.at[pl.ds(half, half)])


def _weight_gather(shards, names):
    views = _weight_views()
    return _gather_side([shards[nm] for nm in names],
                        [jax.ShapeDtypeStruct(WEIGHT_SHAPES[nm], BF16) for nm in names], [views[nm] for nm in names],
                        [_leading_halves(shards[nm].shape) for nm in names])


def kernel(x, p, w_in, pool_w, pool_scale, sgu_ln_g, sgu_ln_b, sgu_w, sgu_b, w_out, ln_g, ln_b, ple_w, ple_gate_w, ple_gate_b, loss_target, m_w_in, m_pool_w, m_pool_scale, m_sgu_ln_g, m_sgu_ln_b, m_sgu_w, m_sgu_b, m_w_out, m_ln_g, m_ln_b, m_ple_w, m_ple_gate_w, m_ple_gate_b, v_w_in, v_pool_w, v_pool_scale, v_sgu_ln_g, v_sgu_ln_b, v_sgu_w, v_sgu_b, v_w_out, v_ln_g, v_ln_b, v_ple_w, v_ple_gate_w, v_ple_gate_b):
    seq = x.shape[1]
    x2, p2, target = x[0], p[0, 0], loss_target[0]
    core = lax.axis_index("c").astype(jnp.int32).reshape(1)
    chip = (2 * lax.axis_index("x") + lax.axis_index("y")).astype(jnp.int32).reshape(1)
    pool_rows = POOL_GROUP // N_DEV

    shards = {"w_in": w_in[0].astype(BF16), "pool_w": pool_w[0].astype(BF16), "w_out": w_out[0].astype(BF16),
              "ple_w": ple_w[0].astype(BF16), "gate_w": ple_gate_w[0].astype(BF16)}
    w_in_f, pool_w_f = _comm_call(_weight_gather(shards, ("w_in", "pool_w")), "gather_mixer_weights")
    bias_tile = jnp.repeat(sgu_b[0].T, HEAD, axis=1)
    (h, y, pooled, xb), (w_out_f, ple_w_f, gate_w_f) = _forward_mixers(
        x2, w_in_f, pool_w_f, pool_scale, sgu_ln_g, sgu_ln_b, sgu_w[0], bias_tile, 2 * TOK_TILE,
        sides=(_weight_gather(shards, ("w_out", "ple_w", "gate_w")),))

    dy, dxr, xn, dgp, dpe, dr, loss, d_ln_g, d_ln_b, d_gate_b = _head_fwd_bwd(
        x2, y, p2, target, w_out_f, gate_w_f, ple_w_f, ln_g, ln_b, ple_gate_b, 2 * TOK_TILE, TOK_TILE)

    d_ple_w, _ = _weight_grad(p2, dpe, N_DEV, N_DEV, GRAD_TILE, "grad_ple_w")
    d_w_out, (ple_theirs,) = _weight_grad(y, dr, 1, 1, GRAD_TILE, "grad_w_out", sides=(_pair_side([d_ple_w]),))
    w_out_blocks = d_w_out.reshape(N_DEV, D_MIX // N_DEV, D_MODEL)
    d_gate_w, (w_out_theirs,) = _weight_grad(xn, dgp, 1, 1, GRAD_TILE, "grad_gate_w",
                                             sides=(_pair_side([w_out_blocks]),))
    gate_blocks = d_gate_w.reshape(N_DEV, D_MODEL // N_DEV, D_MODEL)

    dh, dx, d_pool_w, d_pool_scale, d_sgu_ln_g, d_sgu_ln_b, d_sgu_w, d_sgu_b = _mixers_bwd(
        h, dy, dxr, pooled, w_in_f, pool_w_f, pool_scale, sgu_ln_g, sgu_ln_b, sgu_w[0], bias_tile, TOK_TILE)

    pool_blocks = d_pool_w.reshape(N_DEV, len(POOL_WINDOWS) * pool_rows, POOL_GROUP)
    gate_theirs, pool_theirs = _comm_call(_pair_side([gate_blocks, pool_blocks]), "pair_exchange_late")
    hosted_names = ("w_out", "ple_w", "gate_w", "pool_w")
    hosted_sums = [_pair_sum(core, g, t, "pair_sum_" + nm) for g, t, nm in zip(
        (w_out_blocks, d_ple_w, gate_blocks, pool_blocks),
        (w_out_theirs, ple_theirs, gate_theirs, pool_theirs), hosted_names)]

    small = (d_pool_scale, d_sgu_ln_g, d_sgu_ln_b, d_sgu_w, d_sgu_b, d_ln_g, d_ln_b, d_gate_b)
    parts = small + (loss,)
    small_gather = _gather_side(parts, [jax.ShapeDtypeStruct((N_DEV,) + a.shape, F32) for a in parts],
                                [lambda ref, b: ref.at[b]] * len(parts), [_leading_halves(a.shape) for a in parts])
    cx, cy = lax.axis_index("x"), lax.axis_index("y")
    order = jnp.stack([2 * px + py for px, py in _other_chips(cx, cy) + [(cx, cy)]]).astype(jnp.int32)
    w_in_own, w_in_others, side_out = _grad_w_in_reduced(
        order, xb, dh, GRAD_TILE, sides=(_chip_side([s_bf for _, s_bf in hosted_sums]), small_gather))
    hosted_others, gathered = side_out[:len(hosted_sums)], side_out[len(hosted_sums):]

    shard_of = {"w_in": (w_in, m_w_in, v_w_in), "pool_w": (pool_w, m_pool_w, v_pool_w),
                "w_out": (w_out, m_w_out, v_w_out), "ple_w": (ple_w, m_ple_w, v_ple_w),
                "gate_w": (ple_gate_w, m_ple_gate_w, v_ple_gate_w)}
    reduced = [(nm, chip, s_f32, oth) for nm, (s_f32, _), oth in zip(hosted_names, hosted_sums, hosted_others)]
    reduced.append(("w_in", jnp.zeros((1,), jnp.int32), w_in_own[None], w_in_others))
    big_out = {}
    for nm, which, s_f32, oth in reduced:
        w, m, v = shard_of[nm]
        two_d = s_f32.shape[1:]
        res = _sum_adamw(which, s_f32, oth, w.reshape(two_d), m.reshape(two_d), v.reshape(two_d), "adamw_" + nm)
        big_out[nm] = [r.reshape(w.shape) for r in res]

    small_w = (pool_scale, sgu_ln_g, sgu_ln_b, sgu_w, sgu_b, ln_g, ln_b, ple_gate_b)
    small_m = (m_pool_scale, m_sgu_ln_g, m_sgu_ln_b, m_sgu_w, m_sgu_b, m_ln_g, m_ln_b, m_ple_gate_b)
    small_v = (v_pool_scale, v_sgu_ln_g, v_sgu_ln_b, v_sgu_w, v_sgu_b, v_ln_g, v_ln_b, v_ple_gate_b)
    natural = [[a.reshape(g.shape) for a, g in zip(group, small)] for group in (small_w, small_m, small_v)]
    res = _small_sum_adamw(list(gathered), *natural)
    g_s, d_s, m_s, v_s = [[r.reshape(w.shape) for r, w in zip(kind, small_w)] for kind in res]
    total_loss = res[0][-1][0, 0]

    order = ("w_in", "pool_w", "pool_scale", "sgu_ln_g", "sgu_ln_b", "sgu_w", "sgu_b", "w_out", "ln_g", "ln_b",
             "ple_w", "ple_gate_w", "ple_gate_b")
    outs = [total_loss, dx.reshape(1, seq, D_MODEL)]
    for kind in range(4):
        for nm in order:
            key = "gate_w" if nm == "ple_gate_w" else nm
            if key in big_out:
                outs.append(big_out[key][kind])
            else:
                outs.append((g_s, d_s, m_s, v_s)[kind][SMALL_NAMES.index(nm)])
    return tuple(outs)
```

```python
from typing import Callable, NamedTuple

import numpy as np
import jax
import jax.numpy as jnp
from jax import lax
from jax.experimental import pallas as pl
from jax.experimental.pallas import tpu as pltpu

F32 = jnp.float32
BF16 = jnp.bfloat16

N_DEV = 8
D_MODEL = 1024
D_POOL = 1024
D_SGU = 1024
D_MIX = 2048
D_IN = 5120
D_PLE = 256
POOL_WINDOWS = (2, 4, 8, 16)
POOL_GROUP = 256
N_HEADS = 4
HEAD = 256
CHUNK = 128
HALO = 16
BAND_PAD = 128
ALPHA = 2.0 ** 0.25
LN_EPS = 1e-5
ADAM_LR, ADAM_B1, ADAM_B2, ADAM_EPS, ADAM_WD, ADAM_STEP = 0.001, 0.9, 0.999, 1e-08, 0.01, 10

U0, V0, Z0 = D_POOL, D_POOL + D_SGU, D_POOL + 2 * D_SGU
VMEM_LIMIT = 56 * 1024 * 1024
MESH = pl.DeviceIdType.MESH
ANY = pl.BlockSpec(memory_space=pl.ANY)
VMEM_FULL = pl.BlockSpec(memory_space=pltpu.VMEM)

_GELU_C0 = 0.7978845608028654
_GELU_C1 = 0.044715


def _gelu_cdf(x, x2):
    return 1.0 / (1.0 + jnp.exp(x * ((-2.0 * _GELU_C0) + (-2.0 * _GELU_C0 * _GELU_C1) * x2)))


def _gelu_and_grad(x):
    x2 = x * x
    cdf = _gelu_cdf(x, x2)
    g = x * cdf
    dg = cdf + g * (1.0 - cdf) * ((2.0 * _GELU_C0) + (6.0 * _GELU_C0 * _GELU_C1) * x2)
    return g, dg


def _gelu(x):
    t = jnp.tanh(_GELU_C0 * (x + _GELU_C1 * (x * x * x)))
    return x * (0.5 * (1.0 + t))


def _split_bf16(x):
    hi = x.astype(BF16)
    return hi, (x - hi.astype(F32)).astype(BF16)


def _band(tok_tile, window, transpose):
    t = np.arange(tok_tile)[:, None]
    s = np.arange(tok_tile + BAND_PAD)[None, :]
    d = (s - t) if transpose else (t + BAND_PAD - s)
    return ((d >= 0) & (d < window)).astype(np.float32)


def _bands(tok_tile, transpose):
    return jnp.asarray(np.stack([_band(tok_tile, w, transpose) for w in POOL_WINDOWS]), dtype=BF16)


def _sigmoid(x):
    return 1.0 / (1.0 + jnp.exp(-x))


def _dot(a, b):
    return jnp.dot(a, b, preferred_element_type=F32)


def _dot_nt(a, b):
    return lax.dot_general(a, b, (((1,), (1,)), ((), ())), preferred_element_type=F32)


def _dot_tn(a, b):
    return lax.dot_general(a, b, (((0,), (0,)), ((), ())), preferred_element_type=F32)


def _row_stats(x):
    mu = jnp.mean(x, axis=-1, keepdims=True)
    xc = x - mu
    var = jnp.mean(xc * xc, axis=-1, keepdims=True)
    rstd = lax.rsqrt(var + LN_EPS)
    return xc * rstd, rstd


def _ln_bwd(dxhat, xhat, rstd):
    m1 = jnp.mean(dxhat, axis=-1, keepdims=True)
    m2 = jnp.mean(dxhat * xhat, axis=-1, keepdims=True)
    return rstd * (dxhat - m1 - xhat * m2)


def _masked_sgu_w(sw_ref, hh):
    row = lax.broadcasted_iota(jnp.int32, (CHUNK, CHUNK), 0)
    col = lax.broadcasted_iota(jnp.int32, (CHUNK, CHUNK), 1)
    return jnp.where(row >= col, sw_ref[hh], 0.0)


def _inv_count(tile_index, tok_tile, window):
    tok = tile_index * tok_tile + lax.broadcasted_iota(jnp.int32, (tok_tile, 1), 0)
    return 1.0 / jnp.minimum(tok + 1, window).astype(F32)


def _params(**kw):
    return pltpu.CompilerParams(vmem_limit_bytes=VMEM_LIMIT, **kw)


def _forward_mixers(x, w_in, pool_w, pool_scale, sgu_ln_g, sgu_ln_b, sgu_w, sgu_bias_tile, tok_tile, sides=()):
    seq = x.shape[0]
    n_tiles = seq // tok_tile
    n_chunks = tok_tile // CHUNK
    side_refs = _SideRefs(sides, 8, 4, 2)

    def body(*refs):
        (x_ref, win_ref, pw_ref, ps_ref, lg_ref, lb_ref, sw_ref, sb_ref,
         hb_ref, y_ref, pooled_ref, xb_ref, aext_ref, h_ref) = side_refs.split(refs)
        i = pl.program_id(0)
        side_refs.emit(i == 0, i == n_tiles // 2, False)

        @pl.when(i == 0)
        def _():
            aext_ref[0:HALO, :] = jnp.zeros((HALO, D_POOL), F32)

        xb = x_ref[...].astype(BF16)
        xb_ref[...] = xb
        for s in range(D_IN // 1024):
            cs = slice(s * 1024, (s + 1) * 1024)
            section = _dot(xb, win_ref[:, cs])
            h_ref[:, cs] = section
            if s >= 1:
                hb_ref[:, (s - 1) * 1024:s * 1024] = section.astype(BF16)

        aext_ref[HALO:HALO + tok_tile, :] = h_ref[:, 0:D_POOL]
        for g, window in enumerate(POOL_WINDOWS):
            cols = slice(g * POOL_GROUP, (g + 1) * POOL_GROUP)
            win = aext_ref[HALO:HALO + tok_tile, cols]
            for k in range(1, window):
                win = win + aext_ref[HALO - k:HALO - k + tok_tile, cols]
            pooled = win * _inv_count(i, tok_tile, window) - h_ref[:, cols]
            pb = pooled.astype(BF16)
            pooled_ref[:, cols] = pb
            mixed = _dot(pb, pw_ref[g])
            z = h_ref[:, Z0 + g * POOL_GROUP:Z0 + (g + 1) * POOL_GROUP]
            y_ref[:, cols] = (mixed * ps_ref[:, cols] * (z * _sigmoid(z))).astype(BF16)
        aext_ref[0:HALO, :] = aext_ref[tok_tile:tok_tile + HALO, :]

        for hh in range(N_HEADS):
            cols = slice(hh * HEAD, (hh + 1) * HEAD)
            swm = _masked_sgu_w(sw_ref, hh).astype(BF16)
            for n in range(n_chunks):
                rows = slice(n * CHUNK, (n + 1) * CHUNK)
                gu = _gelu(h_ref[rows, U0 + hh * HEAD:U0 + (hh + 1) * HEAD])
                gv = _gelu(h_ref[rows, V0 + hh * HEAD:V0 + (hh + 1) * HEAD])
                xhat, _ = _row_stats(gv)
                vln = xhat * lg_ref[:, cols] + lb_ref[:, cols]
                sv = _dot(swm, vln.astype(BF16)) + sb_ref[:, cols]
                z = h_ref[rows, Z0 + D_POOL + hh * HEAD:Z0 + D_POOL + (hh + 1) * HEAD]
                y_ref[rows, D_POOL + hh * HEAD:D_POOL + (hh + 1) * HEAD] = (
                    gu * sv * (z * _sigmoid(z))).astype(BF16)

        side_refs.emit(False, False, i == n_tiles - 1)

    tok = lambda width: pl.BlockSpec((tok_tile, width), lambda i: (i, 0))
    outs = pl.pallas_call(
        body, name="forward_mixers",
        grid=(n_tiles,),
        in_specs=[tok(D_MODEL)] + [VMEM_FULL] * 7 + side_refs.in_specs,
        out_specs=[tok(D_IN - D_POOL), tok(D_MIX), tok(D_POOL), tok(D_MODEL)] + side_refs.out_specs,
        out_shape=[jax.ShapeDtypeStruct((seq, D_IN - D_POOL), BF16), jax.ShapeDtypeStruct((seq, D_MIX), BF16),
                   jax.ShapeDtypeStruct((seq, D_POOL), BF16), jax.ShapeDtypeStruct((seq, D_MODEL), BF16)]
        + side_refs.out_shapes,
        scratch_shapes=[pltpu.VMEM((HALO + tok_tile, D_POOL), F32), pltpu.VMEM((tok_tile, D_IN), F32)]
        + side_refs.scratch,
        compiler_params=_params(dimension_semantics=("arbitrary",)),
    )(x, w_in, pool_w, pool_scale, sgu_ln_g, sgu_ln_b, sgu_w, sgu_bias_tile, *side_refs.inputs)
    return outs[:4], outs[4:]


def _head_fwd_bwd(x, y, p, target, w_out, gate_w, ple_w, ln_g, ln_b, gate_b, tok_tile, sub_tile):
    seq = x.shape[0]

    def body(x_ref, y_ref, p_ref, t_ref, wout_ref, gw_ref, plw_ref, lng_ref, lnb_ref, gb_ref,
             dy_ref, dxr_ref, xn_ref, dgp_ref, dpe_ref, dr_ref, loss_ref, dlng_ref, dlnb_ref, dgb_ref):
        i = pl.program_id(0)

        @pl.when(i == 0)
        def _():
            loss_ref[...] = jnp.zeros_like(loss_ref)
            dlng_ref[...] = jnp.zeros_like(dlng_ref)
            dlnb_ref[...] = jnp.zeros_like(dlnb_ref)
            dgb_ref[...] = jnp.zeros_like(dgb_ref)

        subs = [slice(s * sub_tile, (s + 1) * sub_tile) for s in range(tok_tile // sub_tile)]
        stats, xns, douts = [], [], []
        for rows in subs:
            r = ALPHA * x_ref[rows, :] + _dot(y_ref[rows, :], wout_ref[...])
            xhat, rstd = _row_stats(r)
            xn = xhat * lng_ref[...] + lnb_ref[...]
            xn_ref[rows, :] = xn.astype(BF16)
            stats.append((xhat, rstd))
            xns.append(xn)
        loss = jnp.zeros((1, 1), F32)
        dgb = jnp.zeros((1, D_MODEL), F32)
        for rows, xn in zip(subs, xns):
            gate = _sigmoid(_dot(xn_ref[rows, :], gw_ref[...]) + gb_ref[...])
            pe = _dot(p_ref[rows, :].astype(BF16), plw_ref[...])
            err = xn + gate * pe - t_ref[rows, :]
            loss = loss + jnp.sum(err * err, keepdims=True)
            dout = err * (1.0 / D_MODEL)
            dpe_ref[rows, :] = (dout * gate).astype(BF16)
            dgpre = dout * pe * gate * (1.0 - gate)
            dgb = dgb + jnp.sum(dgpre, axis=0, keepdims=True)
            dgp_ref[rows, :] = dgpre.astype(BF16)
            douts.append(dout)
        loss_ref[...] += (0.5 / D_MODEL) * loss
        dgb_ref[...] += dgb
        dlng = jnp.zeros((1, D_MODEL), F32)
        dlnb = jnp.zeros((1, D_MODEL), F32)
        for rows, (xhat, rstd), dout in zip(subs, stats, douts):
            dxn = dout + _dot_nt(dgp_ref[rows, :], gw_ref[...])
            dlng = dlng + jnp.sum(dxn * xhat, axis=0, keepdims=True)
            dlnb = dlnb + jnp.sum(dxn, axis=0, keepdims=True)
            dr = _ln_bwd(dxn * lng_ref[...], xhat, rstd)
            dxr_ref[rows, :] = ALPHA * dr
            dr_ref[rows, :] = dr.astype(BF16)
        dlng_ref[...] += dlng
        dlnb_ref[...] += dlnb
        for rows in subs:
            dy_ref[rows, :] = _dot_nt(dr_ref[rows, :], wout_ref[...])

    tok = lambda width: pl.BlockSpec((tok_tile, width), lambda i: (i, 0))
    acc = lambda width: pl.BlockSpec((1, width), lambda i: (0, 0))
    vec = jax.ShapeDtypeStruct((1, D_MODEL), F32)
    return pl.pallas_call(
        body, name="head_fwd_bwd",
        grid=(seq // tok_tile,),
        in_specs=[tok(D_MODEL), tok(D_MIX), tok(D_PLE), tok(D_MODEL),
                  VMEM_FULL, VMEM_FULL, VMEM_FULL, VMEM_FULL, VMEM_FULL, VMEM_FULL],
        out_specs=[tok(D_MIX), tok(D_MODEL), tok(D_MODEL), tok(D_MODEL), tok(D_MODEL), tok(D_MODEL),
                   acc(128), acc(D_MODEL), acc(D_MODEL), acc(D_MODEL)],
        out_shape=[jax.ShapeDtypeStruct((seq, D_MIX), F32), jax.ShapeDtypeStruct((seq, D_MODEL), F32),
                   jax.ShapeDtypeStruct((seq, D_MODEL), BF16), jax.ShapeDtypeStruct((seq, D_MODEL), BF16),
                   jax.ShapeDtypeStruct((seq, D_MODEL), BF16), jax.ShapeDtypeStruct((seq, D_MODEL), BF16),
                   jax.ShapeDtypeStruct((1, 128), F32), vec, vec, vec],
        compiler_params=_params(dimension_semantics=("arbitrary",)),
    )(x, y, p, target, w_out, gate_w, ple_w, ln_g, ln_b, gate_b)


def _mixers_bwd(h, dy, dxr, pooled, w_in, pool_w, pool_scale, sgu_ln_g, sgu_ln_b, sgu_w, sgu_bias_tile, tok_tile):
    seq = h.shape[0]
    n_tiles = seq // tok_tile
    n_chunks = tok_tile // CHUNK
    pool_rows = POOL_GROUP // N_DEV

    def body(h_ref, dy_ref, dxr_ref, pooled_ref, win_ref, pw_ref, ps_ref, lg_ref, lb_ref, sw_ref, sb_ref, band_ref,
             dh_ref, dx_ref, dpw_ref, dps_ref, dlg_ref, dlb_ref, dsw_ref, dsb_ref,
             qhi_ref, qlo_ref, dpw_acc, dsb_acc):
        i = pl.program_id(0)
        tile = n_tiles - 1 - i

        @pl.when(i == 0)
        def _():
            qhi_ref[...] = jnp.zeros_like(qhi_ref)
            qlo_ref[...] = jnp.zeros_like(qlo_ref)
            dpw_acc[...] = jnp.zeros_like(dpw_acc)
            dsb_acc[...] = jnp.zeros_like(dsb_acc)
            dps_ref[...] = jnp.zeros_like(dps_ref)
            dlg_ref[...] = jnp.zeros_like(dlg_ref)
            dlb_ref[...] = jnp.zeros_like(dlb_ref)
            dsw_ref[...] = jnp.zeros_like(dsw_ref)

        def h_at(rows, cols):
            return h_ref[rows, cols.start - D_POOL:cols.stop - D_POOL].astype(F32)

        everything = slice(0, tok_tile)
        for part in (qhi_ref, qlo_ref):
            part[tok_tile:tok_tile + HALO, :] = part[0:HALO, :]
        for g, window in enumerate(POOL_WINDOWS):
            cols = slice(g * POOL_GROUP, (g + 1) * POOL_GROUP)
            zcols = slice(Z0 + g * POOL_GROUP, Z0 + (g + 1) * POOL_GROUP)
            z = h_at(everything, zcols)
            sz = _sigmoid(z)
            pb = pooled_ref[:, cols]
            mixed = _dot(pb, pw_ref[g])
            dyp = dy_ref[:, cols]
            dh_ref[:, zcols] = (dyp * (mixed * ps_ref[:, cols]) * (sz * (1.0 + z * (1.0 - sz)))).astype(BF16)
            dms = dyp * (z * sz)
            dps_ref[:, cols] += jnp.sum(dms * mixed, axis=0, keepdims=True)
            dmixed = (dms * ps_ref[:, cols]).astype(BF16)
            dpw_acc[g] += _dot_tn(pb, dmixed)
            dpooled = _dot_nt(dmixed, pw_ref[g])
            qhi_ref[everything, cols], qlo_ref[everything, cols] = _split_bf16(
                dpooled * _inv_count(tile, tok_tile, window))
            da = _dot(band_ref[g], qhi_ref[:, cols]) + _dot(band_ref[g], qlo_ref[:, cols]) - dpooled
            dh_ref[:, cols] = da.astype(BF16)

        pool_cols = [slice(o + g * POOL_GROUP, o + (g + 1) * POOL_GROUP)
                     for o in (0, Z0) for g in range(len(POOL_WINDOWS))]
        pool_slices = [pool_cols[0:3], pool_cols[3:6], pool_cols[6:8], []]
        for hh in range(N_HEADS):
            cols = slice(hh * HEAD, (hh + 1) * HEAD)
            ucols = slice(U0 + hh * HEAD, U0 + (hh + 1) * HEAD)
            vcols = slice(V0 + hh * HEAD, V0 + (hh + 1) * HEAD)
            zcols = slice(Z0 + D_POOL + hh * HEAD, Z0 + D_POOL + (hh + 1) * HEAD)
            sw32 = _masked_sgu_w(sw_ref, hh)
            swm = sw32.astype(BF16)
            swm_t = sw32.T.astype(BF16)
            for n in range(n_chunks):
                rows = slice(n * CHUNK, (n + 1) * CHUNK)
                gu, dgu_du = _gelu_and_grad(h_at(rows, ucols))
                gv, dgv_dv = _gelu_and_grad(h_at(rows, vcols))
                xhat, rstd = _row_stats(gv)
                vb = (xhat * lg_ref[:, cols] + lb_ref[:, cols]).astype(BF16)
                sv = _dot(swm, vb) + sb_ref[:, cols]
                z = h_at(rows, zcols)
                sz = _sigmoid(z)
                dys = dy_ref[rows, D_POOL + hh * HEAD:D_POOL + (hh + 1) * HEAD]
                dh_ref[rows, zcols] = (dys * (gu * sv) * (sz * (1.0 + z * (1.0 - sz)))).astype(BF16)
                dyg = dys * (z * sz)
                dh_ref[rows, ucols] = (dyg * sv * dgu_du).astype(BF16)
                dsv = dyg * gu
                dsb_acc[:, cols] += dsv
                dsvb = dsv.astype(BF16)
                dsw_ref[hh] += _dot_nt(dsvb, vb)
                dvln = _dot(swm_t, dsvb)
                dlg_ref[:, cols] += jnp.sum(dvln * xhat, axis=0, keepdims=True)
                dlb_ref[:, cols] += jnp.sum(dvln, axis=0, keepdims=True)
                dgv = _ln_bwd(dvln * lg_ref[:, cols], xhat, rstd)
                dh_ref[rows, vcols] = (dgv * dgv_dv).astype(BF16)
            ready = [ucols, vcols, zcols] + pool_slices[hh]
            part = _dot_nt(dh_ref[:, ready[0]], win_ref[:, ready[0]])
            for sl in ready[1:]:
                part = part + _dot_nt(dh_ref[:, sl], win_ref[:, sl])
            if hh == 0:
                dx_ref[...] = dxr_ref[...] + part
            else:
                dx_ref[...] += part

        @pl.when(i == n_tiles - 1)
        def _():
            for g in range(len(POOL_WINDOWS)):
                for b in range(N_DEV):
                    dpw_ref[b, g] = dpw_acc[g, b * pool_rows:(b + 1) * pool_rows, :]
            row = lax.broadcasted_iota(jnp.int32, (CHUNK, CHUNK), 0)
            col = lax.broadcasted_iota(jnp.int32, (CHUNK, CHUNK), 1)
            for hh in range(N_HEADS):
                dsw_ref[hh] = jnp.where(row >= col, dsw_ref[hh], 0.0)
                total = jnp.sum(dsb_acc[:, hh * HEAD:(hh + 1) * HEAD], axis=1, keepdims=True)
                dsb_ref[hh:hh + 1, :] = jnp.broadcast_to(total, (CHUNK, CHUNK)).T[0:1, :]

    tok = lambda width: pl.BlockSpec((tok_tile, width), lambda i: (n_tiles - 1 - i, 0))
    whole = lambda shape: pl.BlockSpec(shape, lambda i: (0,) * len(shape))
    vec = jax.ShapeDtypeStruct((1, D_MODEL), F32)
    return pl.pallas_call(
        body, name="mixers_bwd",
        grid=(n_tiles,),
        in_specs=[tok(D_IN - D_POOL), tok(D_MIX), tok(D_MODEL), tok(D_POOL)] + [VMEM_FULL] * 8,
        out_specs=[tok(D_IN), tok(D_MODEL), whole((N_DEV, len(POOL_WINDOWS), pool_rows, POOL_GROUP)),
                   whole((1, D_POOL)), whole((1, D_SGU)), whole((1, D_SGU)),
                   whole((N_HEADS, CHUNK, CHUNK)), whole((N_HEADS, CHUNK))],
        out_shape=[jax.ShapeDtypeStruct((seq, D_IN), BF16), jax.ShapeDtypeStruct((seq, D_MODEL), F32),
                   jax.ShapeDtypeStruct((N_DEV, len(POOL_WINDOWS), pool_rows, POOL_GROUP), F32),
                   vec, vec, vec,
                   jax.ShapeDtypeStruct((N_HEADS, CHUNK, CHUNK), F32),
                   jax.ShapeDtypeStruct((N_HEADS, CHUNK), F32)],
        scratch_shapes=[pltpu.VMEM((tok_tile + BAND_PAD, D_POOL), BF16),
                        pltpu.VMEM((tok_tile + BAND_PAD, D_POOL), BF16),
                        pltpu.VMEM((len(POOL_WINDOWS), POOL_GROUP, POOL_GROUP), F32),
                        pltpu.VMEM((CHUNK, D_SGU), F32)],
        compiler_params=_params(dimension_semantics=("arbitrary",)),
    )(h, dy, dxr, pooled, w_in, pool_w, pool_scale, sgu_ln_g, sgu_ln_b, sgu_w, sgu_bias_tile, _bands(tok_tile, True))


def _weight_grad(a, b, n_col_blocks, blocks_per_step, tok_tile, name, sides=()):
    seq, m = a.shape
    n = b.shape[1]
    nb = n // n_col_blocks
    n_steps = n_col_blocks // blocks_per_step
    n_k = seq // tok_tile
    side_refs = _SideRefs(sides, 2, 1, 0)

    def body(*refs):
        a_ref, b_ref, out_ref = side_refs.split(refs)
        step = pl.program_id(0) * n_k + pl.program_id(1)
        side_refs.emit(step == 0, step == (n_steps * n_k) // 2, False)

        @pl.when(pl.program_id(1) == 0)
        def _():
            out_ref[...] = jnp.zeros_like(out_ref)

        res = _dot_tn(a_ref[...].astype(BF16), b_ref[...])
        for blk in range(blocks_per_step):
            out_ref[blk] += res[:, blk * nb:(blk + 1) * nb]

        side_refs.emit(False, False, step == n_steps * n_k - 1)

    outs = pl.pallas_call(
        body, name=name,
        grid=(n_steps, n_k),
        in_specs=[pl.BlockSpec((tok_tile, m), lambda j, k: (k, 0)),
                  pl.BlockSpec((tok_tile, blocks_per_step * nb), lambda j, k: (k, j))] + side_refs.in_specs,
        out_specs=[pl.BlockSpec((blocks_per_step, m, nb), lambda j, k: (j, 0, 0))] + side_refs.out_specs,
        out_shape=[jax.ShapeDtypeStruct((n_col_blocks, m, nb), F32)] + side_refs.out_shapes,
        scratch_shapes=side_refs.scratch,
        compiler_params=_params(dimension_semantics=("arbitrary", "arbitrary")),
    )(a, b, *side_refs.inputs)
    return outs[0], outs[1:]


class _Side(NamedTuple):
    inputs: list
    out_shapes: list
    sem_shapes: list
    emit: Callable


def _when(cond):
    if cond is True:
        return lambda f: f()
    if cond is False:
        return lambda f: None
    return pl.when(cond)


def _place():
    return lax.axis_index("x"), lax.axis_index("y"), lax.axis_index("c")


def _other_chips(x, y):
    return [(1 - x, y), (x, 1 - y), (1 - x, 1 - y)]


def _gather_side(shards, out_shapes, views, halves):
    n = len(shards)
    n_sems = 10

    def emit(ins, outs, sems, first, mid, last):
        send_sems, recv_sems, local_sems = sems
        x, y, c = _place()
        here, x_nbr, y_nbr, diag = (x, y), (1 - x, y), (x, 1 - y), (1 - x, 1 - y)
        sibling = (x, y, 1 - c)

        def block(k, chip, core):
            return views[k](outs[k], 4 * chip[0] + 2 * chip[1] + core)

        def piece(k, ref, p):
            return ref if halves[k] is None else halves[k](ref)[p]

        def n_pieces(k):
            return 1 if halves[k] is None else 2

        def copy(k, s, src, dst, to):
            return pltpu.make_async_remote_copy(
                src_ref=src, dst_ref=dst, send_sem=send_sems.at[k, s], recv_sem=recv_sems.at[k, s],
                device_id=to, device_id_type=MESH)

        def outgoing(k, s):
            mine = block(k, here, c)
            if s == 0:
                return copy(k, 0, ins[k], mine, sibling)
            if s in (1, 2):
                return copy(k, s, piece(k, ins[k], s - 1), piece(k, mine, s - 1), (*x_nbr, c))
            if s in (3, 4):
                return copy(k, s, piece(k, ins[k], s - 3), piece(k, mine, s - 3), (*y_nbr, c))
            if s == 5:
                part = piece(k, block(k, x_nbr, c), 0)
                return copy(k, 5, part, part, (*y_nbr, c))
            if s == 6:
                part = piece(k, block(k, y_nbr, c), 1)
                return copy(k, 6, part, part, (*x_nbr, c))
            whole = block(k, (x_nbr, y_nbr, diag)[s - 7], c)
            return copy(k, s, whole, whole, sibling)

        def incoming(k, s):
            if s == 0:
                zone = block(k, here, 1 - c)
            elif s in (1, 2):
                zone = piece(k, block(k, x_nbr, c), s - 1)
            elif s in (3, 4):
                zone = piece(k, block(k, y_nbr, c), s - 3)
            elif s in (5, 6):
                zone = piece(k, block(k, diag, c), s - 5)
            else:
                zone = block(k, (x_nbr, y_nbr, diag)[s - 7], 1 - c)
            return copy(k, s, zone, zone, sibling)

        def own(k):
            return pltpu.make_async_copy(ins[k], block(k, here, c), local_sems.at[k])

        def used(k):
            return list(range(n_sems)) if n_pieces(k) == 2 else [0, 1, 3, 5, 7, 8, 9]

        @_when(first)
        def _():
            for k in range(n):
                own(k).start()
                for s in used(k):
                    if s <= 4:
                        outgoing(k, s).start()

        @_when(mid)
        def _():
            for k in range(n):
                incoming(k, 1).wait_recv()
                outgoing(k, 5).start()
                if n_pieces(k) == 2:
                    incoming(k, 4).wait_recv()
                    outgoing(k, 6).start()
                    incoming(k, 2).wait_recv()
                outgoing(k, 7).start()
                incoming(k, 3).wait_recv()
                outgoing(k, 8).start()

        @_when(last)
        def _():
            for k in range(n):
                incoming(k, 5).wait_recv()
                if n_pieces(k) == 2:
                    incoming(k, 6).wait_recv()
                outgoing(k, 9).start()
            for k in range(n):
                for s in (0, 7, 8, 9):
                    incoming(k, s).wait_recv()
            for k in range(n):
                for s in used(k):
                    outgoing(k, s).wait_send()
                own(k).wait()

    sems = [pltpu.SemaphoreType.DMA((n, n_sems)), pltpu.SemaphoreType.DMA((n, n_sems)),
            pltpu.SemaphoreType.DMA((n,))]
    return _Side(list(shards), list(out_shapes), sems, emit)


def _pair_side(grads):
    n = len(grads)

    def emit(ins, theirs, sems, first, mid, last):
        send_sems, recv_sems = sems
        x, y, c = _place()

        def copies():
            return [pltpu.make_async_remote_copy(
                src_ref=ins[k].at[2 * j + (1 - c)], dst_ref=theirs[k].at[j],
                send_sem=send_sems.at[k, j], recv_sem=recv_sems.at[k, j],
                device_id=(x, y, 1 - c), device_id_type=MESH) for k in range(n) for j in range(4)]

        @_when(first)
        def _():
            for cp in copies():
                cp.start()

        @_when(last)
        def _():
            for cp in copies():
                cp.wait_recv()
            for cp in copies():
                cp.wait_send()

    shapes = [jax.ShapeDtypeStruct((4,) + g.shape[1:], g.dtype) for g in grads]
    return _Side(list(grads), shapes, [pltpu.SemaphoreType.DMA((n, 4)), pltpu.SemaphoreType.DMA((n, 4))], emit)


def _chip_side(sums):
    n = len(sums)

    def emit(ins, others, sems, first, mid, last):
        send_sems, recv_sems = sems
        x, y, c = _place()

        def copies():
            return [pltpu.make_async_remote_copy(
                src_ref=ins[k].at[2 * px + py], dst_ref=others[k].at[r],
                send_sem=send_sems.at[k, r], recv_sem=recv_sems.at[k, r],
                device_id=(px, py, c), device_id_type=MESH)
                for k in range(n) for r, (px, py) in enumerate(_other_chips(x, y))]

        @_when(first)
        def _():
            for cp in copies():
                cp.start()

        @_when(last)
        def _():
            for cp in copies():
                cp.wait_recv()
            for cp in copies():
                cp.wait_send()

    shapes = [jax.ShapeDtypeStruct((3,) + s.shape[1:], s.dtype) for s in sums]
    return _Side(list(sums), shapes, [pltpu.SemaphoreType.DMA((n, 3)), pltpu.SemaphoreType.DMA((n, 3))], emit)


def _comm_call(side, name):
    n_in, n_out = len(side.inputs), len(side.out_shapes)

    def body(*refs):
        side.emit(refs[:n_in], refs[n_in:n_in + n_out], refs[n_in + n_out:], True, True, True)

    return pl.pallas_call(
        body, name=name, in_specs=[ANY] * n_in, out_specs=[ANY] * n_out,
        out_shape=side.out_shapes, scratch_shapes=side.sem_shapes,
    )(*side.inputs)


class _SideRefs:
    def __init__(self, sides, n_in, n_out, n_scratch):
        self.sides, self.n_in, self.n_out, self.n_scratch = sides, n_in, n_out, n_scratch
        self.inputs = [a for s in sides for a in s.inputs]
        self.out_shapes = [o for s in sides for o in s.out_shapes]
        self.scratch = [m for s in sides for m in s.sem_shapes]
        self.in_specs = [ANY] * len(self.inputs)
        self.out_specs = [ANY] * len(self.out_shapes)

    def split(self, refs):
        refs = list(refs)
        n_side_in, n_side_out = len(self.inputs), len(self.out_shapes)
        ins, rest = refs[:self.n_in], refs[self.n_in:]
        side_in, rest = rest[:n_side_in], rest[n_side_in:]
        outs, rest = rest[:self.n_out], rest[self.n_out:]
        side_out, rest = rest[:n_side_out], rest[n_side_out:]
        scratch, side_sems = rest[:self.n_scratch], rest[self.n_scratch:]
        self._refs = (side_in, side_out, side_sems)
        return ins + outs + scratch

    def emit(self, first, mid, last):
        side_in, side_out, side_sems = self._refs
        for s in self.sides:
            a, b, m = len(s.inputs), len(s.out_shapes), len(s.sem_shapes)
            s.emit(side_in[:a], side_out[:b], side_sems[:m], first, mid, last)
            side_in, side_out, side_sems = side_in[a:], side_out[b:], side_sems[m:]


def _grad_w_in_reduced(order, a, b, tok_tile, sides=()):
    seq, m = a.shape
    nb = b.shape[1] // N_DEV
    n_k = seq // tok_tile
    n_rows = 4
    last_step = n_rows * n_k - 1
    assert n_k >= 3
    side_refs = _SideRefs(sides, 3, 3, 9)

    def body(*refs):
        (order_ref, a_ref, b_ref, own_ref, theirs_ref, others_ref,
         acc_ref, stage_ref, sumbf_ref, kept_ref,
         pair_send, pair_recv, ici_send, ici_recv, stage_sem) = side_refs.split(refs)
        j, k = pl.program_id(0), pl.program_id(1)
        step = j * n_k + k
        side_refs.emit(step == 0, step == (n_rows * n_k) // 2, False)
        x, y, c = _place()
        chips = _other_chips(x, y)

        def to_sibling(row):
            return pltpu.make_async_remote_copy(
                src_ref=acc_ref.at[row % 2, 1 - c], dst_ref=theirs_ref.at[row],
                send_sem=pair_send.at[row], recv_sem=pair_recv.at[row],
                device_id=(x, y, 1 - c), device_id_type=MESH)

        def to_owner(row):
            px, py = chips[row]
            return pltpu.make_async_remote_copy(
                src_ref=sumbf_ref.at[row], dst_ref=others_ref.at[row],
                send_sem=ici_send.at[row], recv_sem=ici_recv.at[row],
                device_id=(px, py, c), device_id_type=MESH)

        def staged(row):
            return pltpu.make_async_copy(theirs_ref.at[row], stage_ref, stage_sem.at[0])

        @pl.when(k == 0)
        def _():
            acc_ref[j % 2] = jnp.zeros((2, m, nb), F32)

        @pl.when(j == 0)
        def _():
            kept_ref[k] = a_ref[...].astype(BF16)

        res = _dot_tn(kept_ref[k], b_ref[...])
        for blk in range(2):
            acc_ref[j % 2, blk] += res[:, blk * nb:(blk + 1) * nb]

        for row in range(n_rows):
            @pl.when((j == row) & (k == n_k - 1))
            def _():
                to_sibling(row).start()

            if row < n_rows - 1:
                @pl.when((j == row + 1) & (k == 1))
                def _():
                    to_sibling(row).wait_recv()
                    staged(row).start()

                @pl.when((j == row + 1) & (k == 2))
                def _():
                    staged(row).wait()
                    to_sibling(row).wait_send()
                    sumbf_ref[row] = (acc_ref[row % 2, c] + stage_ref[...]).astype(BF16)
                    to_owner(row).start()

        @pl.when(step == last_step)
        def _():
            row = n_rows - 1
            to_sibling(row).wait_recv()
            staged(row).start()
            staged(row).wait()
            to_sibling(row).wait_send()
            own_ref[...] = acc_ref[row % 2, c] + stage_ref[...]
            for r in range(n_rows - 1):
                to_owner(r).wait_recv()
                to_owner(r).wait_send()

        side_refs.emit(False, False, step == last_step)

    block = jax.ShapeDtypeStruct((m, nb), F32)
    outs = pl.pallas_call(
        body, name="grad_w_in",
        grid_spec=pltpu.PrefetchScalarGridSpec(
            num_scalar_prefetch=1, grid=(n_rows, n_k),
            in_specs=[pl.BlockSpec((tok_tile, m), lambda j, k, order_ref: (jnp.where(j == 0, k, n_k - 1), 0)),
                      pl.BlockSpec((tok_tile, 2 * nb), lambda j, k, order_ref: (k, order_ref[j]))]
            + side_refs.in_specs,
            out_specs=[pl.BlockSpec((m, nb), lambda j, k, order_ref: (0, 0)), ANY, ANY] + side_refs.out_specs,
            scratch_shapes=[pltpu.VMEM((2, 2, m, nb), F32), pltpu.VMEM((m, nb), F32),
                            pltpu.VMEM((n_rows - 1, m, nb), BF16), pltpu.VMEM((n_k, tok_tile, m), BF16),
                            pltpu.SemaphoreType.DMA((n_rows,)), pltpu.SemaphoreType.DMA((n_rows,)),
                            pltpu.SemaphoreType.DMA((n_rows - 1,)), pltpu.SemaphoreType.DMA((n_rows - 1,)),
                            pltpu.SemaphoreType.DMA((1,))] + side_refs.scratch),
        out_shape=[block, jax.ShapeDtypeStruct((n_rows, m, nb), F32),
                   jax.ShapeDtypeStruct((n_rows - 1, m, nb), BF16)] + side_refs.out_shapes,
        compiler_params=_params(dimension_semantics=("arbitrary", "arbitrary")),
    )(order, a, b, *side_refs.inputs)
    return outs[0], outs[2], outs[3:]


def _row_tile(rows, cols):
    tile = rows
    while tile * cols > 256 * 1024 and tile % 16 == 0:
        tile //= 2
    return tile


def _pair_sum(core, grads, theirs, name):
    _, rows, cols = theirs.shape
    rt = _row_tile(rows, cols)

    def body(core_ref, a_ref, b_ref, o_ref, ob_ref):
        total = a_ref[...] + b_ref[...]
        o_ref[...] = total
        ob_ref[...] = total.astype(BF16)

    spec = pl.BlockSpec((None, rt, cols), lambda j, i, core_ref: (j, i, 0))
    mine = pl.BlockSpec((None, None, rt, cols), lambda j, i, core_ref: (j, core_ref[0], i, 0))
    return pl.pallas_call(
        body, name=name,
        grid_spec=pltpu.PrefetchScalarGridSpec(
            num_scalar_prefetch=1, grid=(4, rows // rt), in_specs=[mine, spec], out_specs=[spec, spec]),
        out_shape=[jax.ShapeDtypeStruct(theirs.shape, F32), jax.ShapeDtypeStruct(theirs.shape, BF16)],
        compiler_params=_params(dimension_semantics=("arbitrary", "arbitrary")),
    )(core, grads.reshape(4, 2, rows, cols), theirs)


def _adamw(w, g, m, v):
    m = ADAM_B1 * m + (1.0 - ADAM_B1) * g
    v = ADAM_B2 * v + (1.0 - ADAM_B2) * (g * g)
    m_hat = m / (1.0 - ADAM_B1 ** ADAM_STEP)
    v_hat = v / (1.0 - ADAM_B2 ** ADAM_STEP)
    delta = -ADAM_LR * (m_hat / (jnp.sqrt(v_hat) + ADAM_EPS) + ADAM_WD * w)
    return delta, m, v


def _sum_adamw(chip, sums, others, w, m, v, name):
    _, rows, cols = sums.shape
    rt = _row_tile(rows, cols)

    def body(chip_ref, own_ref, oth_ref, w_ref, m_ref, v_ref, g_ref, d_ref, nm_ref, nv_ref):
        g = ((own_ref[...] + oth_ref[0].astype(F32)) + oth_ref[1].astype(F32)) + oth_ref[2].astype(F32)
        g_ref[...] = g
        d_ref[...], nm_ref[...], nv_ref[...] = _adamw(w_ref[...], g, m_ref[...], v_ref[...])

    spec = pl.BlockSpec((rt, cols), lambda i, chip_ref: (i, 0))
    own = pl.BlockSpec((None, rt, cols), lambda i, chip_ref: (chip_ref[0], i, 0))
    shape = jax.ShapeDtypeStruct((rows, cols), F32)
    return pl.pallas_call(
        body, name=name,
        grid_spec=pltpu.PrefetchScalarGridSpec(
            num_scalar_prefetch=1, grid=(rows // rt,),
            in_specs=[own, pl.BlockSpec((3, rt, cols), lambda i, chip_ref: (0, i, 0)), spec, spec, spec],
            out_specs=[spec] * 4),
        out_shape=[shape] * 4,
        compiler_params=_params(dimension_semantics=("arbitrary",)),
    )(chip, sums, others, w, m, v)


def _small_sum_adamw(gathered, ws, ms, vs):
    n = len(ws)

    def body(*refs):
        g8 = refs[:n + 1]
        w, m, v = refs[n + 1:2 * n + 1], refs[2 * n + 1:3 * n + 1], refs[3 * n + 1:4 * n + 1]
        outs = refs[4 * n + 1:]
        g_out, d_out, m_out, v_out = outs[:n + 1], outs[n + 1:2 * n + 1], outs[2 * n + 1:3 * n + 1], outs[3 * n + 1:]
        for k in range(n + 1):
            g = g8[k][0]
            for b in range(1, N_DEV):
                g = g + g8[k][b]
            g_out[k][...] = g
            if k < n:
                d_out[k][...], m_out[k][...], v_out[k][...] = _adamw(w[k][...], g, m[k][...], v[k][...])

    shapes = [jax.ShapeDtypeStruct(w.shape, F32) for w in ws]
    loss_shape = jax.ShapeDtypeStruct(gathered[-1].shape[1:], F32)
    outs = pl.pallas_call(
        body, name="small_sum_adamw",
        in_specs=[VMEM_FULL] * (4 * n + 1), out_specs=[VMEM_FULL] * (4 * n + 1),
        out_shape=shapes + [loss_shape] + shapes * 3,
        compiler_params=_params(),
    )(*gathered, *ws, *ms, *vs)
    return outs[:n + 1], outs[n + 1:2 * n + 1], outs[2 * n + 1:3 * n + 1], outs[3 * n + 1:]


SMALL_NAMES = ("pool_scale", "sgu_ln_g", "sgu_ln_b", "sgu_w", "sgu_b", "ln_g", "ln_b", "ple_gate_b")


TOK_TILE = 256
GRAD_TILE = 1024


def _weight_views():
    cols = lambda width: (lambda ref, b: ref.at[:, pl.ds(pl.multiple_of(b * width, 128), width)])
    rows = lambda height: (lambda ref, b: ref.at[pl.ds(pl.multiple_of(b * height, 16), height), :])
    pool_rows = POOL_GROUP // N_DEV
    return {"w_in": cols(D_IN // N_DEV),
            "pool_w": lambda ref, b: ref.at[:, pl.ds(pl.multiple_of(b * pool_rows, 16), pool_rows), :],
            "w_out": rows(D_MIX // N_DEV), "ple_w": cols(D_MODEL // N_DEV), "gate_w": rows(D_MODEL // N_DEV)}


WEIGHT_SHAPES = {"w_in": (D_MODEL, D_IN), "pool_w": (len(POOL_WINDOWS), POOL_GROUP, POOL_GROUP),
                 "w_out": (D_MIX, D_MODEL), "ple_w": (D_PLE, D_MODEL), "gate_w": (D_MODEL, D_MODEL)}


def _to_bf16(arrays):
    def body(*refs):
        for src, dst in zip(refs[:len(arrays)], refs[len(arrays):]):
            dst[...] = src[...].astype(BF16)

    return pl.pallas_call(
        body, name="cast_shards", in_specs=[VMEM_FULL] * len(arrays), out_specs=[VMEM_FULL] * len(arrays),
        out_shape=[jax.ShapeDtypeStruct(a.shape, BF16) for a in arrays], compiler_params=_params(),
    )(*arrays)


def _leading_halves(shape):
    whole_tiles = len(shape) >= 3 or shape[0] % 32 == 0
    if shape[0] % 2 or not whole_tiles:
        return None
    half = shape[0] // 2
    return lambda ref: (ref.at[pl.ds(0, half)], ref.at[pl.ds(half, half)])


def _weight_gather(shards, names):
    views = _weight_views()
    return _gather_side([shards[nm] for nm in names],
                        [jax.ShapeDtypeStruct(WEIGHT_SHAPES[nm], BF16) for nm in names], [views[nm] for nm in names],
                        [_leading_halves(shards[nm].shape) for nm in names])


def kernel(x, p, w_in, pool_w, pool_scale, sgu_ln_g, sgu_ln_b, sgu_w, sgu_b, w_out, ln_g, ln_b, ple_w, ple_gate_w, ple_gate_b, loss_target, m_w_in, m_pool_w, m_pool_scale, m_sgu_ln_g, m_sgu_ln_b, m_sgu_w, m_sgu_b, m_w_out, m_ln_g, m_ln_b, m_ple_w, m_ple_gate_w, m_ple_gate_b, v_w_in, v_pool_w, v_pool_scale, v_sgu_ln_g, v_sgu_ln_b, v_sgu_w, v_sgu_b, v_w_out, v_ln_g, v_ln_b, v_ple_w, v_ple_gate_w, v_ple_gate_b):
    seq = x.shape[1]
    x2, p2, target = x[0], p[0, 0], loss_target[0]
    core = lax.axis_index("c").astype(jnp.int32).reshape(1)
    chip = (2 * lax.axis_index("x") + lax.axis_index("y")).astype(jnp.int32).reshape(1)
    pool_rows = POOL_GROUP // N_DEV

    shard_names = ("w_in", "pool_w", "w_out", "ple_w", "gate_w")
    shards = dict(zip(shard_names, _to_bf16([w_in[0], pool_w[0], w_out[0], ple_w[0], ple_gate_w[0]])))
    w_in_f, pool_w_f = _comm_call(_weight_gather(shards, ("w_in", "pool_w")), "gather_mixer_weights")
    bias_tile = jnp.repeat(sgu_b[0].T, HEAD, axis=1)
    (h, y, pooled, xb), (w_out_f, ple_w_f, gate_w_f) = _forward_mixers(
        x2, w_in_f, pool_w_f, pool_scale, sgu_ln_g, sgu_ln_b, sgu_w[0], bias_tile, 2 * TOK_TILE,
        sides=(_weight_gather(shards, ("w_out", "ple_w", "gate_w")),))

    dy, dxr, xn, dgp, dpe, dr, loss, d_ln_g, d_ln_b, d_gate_b = _head_fwd_bwd(
        x2, y, p2, target, w_out_f, gate_w_f, ple_w_f, ln_g, ln_b, ple_gate_b, 2 * TOK_TILE, TOK_TILE)

    d_ple_w, _ = _weight_grad(p2, dpe, N_DEV, N_DEV, GRAD_TILE, "grad_ple_w")
    d_w_out, (ple_theirs,) = _weight_grad(y, dr, 1, 1, GRAD_TILE, "grad_w_out", sides=(_pair_side([d_ple_w]),))
    w_out_blocks = d_w_out.reshape(N_DEV, D_MIX // N_DEV, D_MODEL)
    d_gate_w, (w_out_theirs,) = _weight_grad(xn, dgp, 1, 1, GRAD_TILE, "grad_gate_w",
                                             sides=(_pair_side([w_out_blocks]),))
    gate_blocks = d_gate_w.reshape(N_DEV, D_MODEL // N_DEV, D_MODEL)

    dh, dx, d_pool_w, d_pool_scale, d_sgu_ln_g, d_sgu_ln_b, d_sgu_w, d_sgu_b = _mixers_bwd(
        h, dy, dxr, pooled, w_in_f, pool_w_f, pool_scale, sgu_ln_g, sgu_ln_b, sgu_w[0], bias_tile, TOK_TILE)

    pool_blocks = d_pool_w.reshape(N_DEV, len(POOL_WINDOWS) * pool_rows, POOL_GROUP)
    gate_theirs, pool_theirs = _comm_call(_pair_side([gate_blocks, pool_blocks]), "pair_exchange_late")
    hosted_names = ("w_out", "ple_w", "gate_w", "pool_w")
    hosted_sums = [_pair_sum(core, g, t, "pair_sum_" + nm) for g, t, nm in zip(
        (w_out_blocks, d_ple_w, gate_blocks, pool_blocks),
        (w_out_theirs, ple_theirs, gate_theirs, pool_theirs), hosted_names)]

    small = (d_pool_scale, d_sgu_ln_g, d_sgu_ln_b, d_sgu_w, d_sgu_b, d_ln_g, d_ln_b, d_gate_b)
    parts = small + (loss,)
    small_gather = _gather_side(parts, [jax.ShapeDtypeStruct((N_DEV,) + a.shape, F32) for a in parts],
                                [lambda ref, b: ref.at[b]] * len(parts), [_leading_halves(a.shape) for a in parts])
    cx, cy = lax.axis_index("x"), lax.axis_index("y")
    order = jnp.stack([2 * px + py for px, py in _other_chips(cx, cy) + [(cx, cy)]]).astype(jnp.int32)
    w_in_own, w_in_others, side_out = _grad_w_in_reduced(
        order, xb, dh, GRAD_TILE, sides=(_chip_side([s_bf for _, s_bf in hosted_sums]), small_gather))
    hosted_others, gathered = side_out[:len(hosted_sums)], side_out[len(hosted_sums):]

    shard_of = {"w_in": (w_in, m_w_in, v_w_in), "pool_w": (pool_w, m_pool_w, v_pool_w),
                "w_out": (w_out, m_w_out, v_w_out), "ple_w": (ple_w, m_ple_w, v_ple_w),
                "gate_w": (ple_gate_w, m_ple_gate_w, v_ple_gate_w)}
    reduced = [(nm, chip, s_f32, oth) for nm, (s_f32, _), oth in zip(hosted_names, hosted_sums, hosted_others)]
    reduced.append(("w_in", jnp.zeros((1,), jnp.int32), w_in_own[None], w_in_others))
    big_out = {}
    for nm, which, s_f32, oth in reduced:
        w, m, v = shard_of[nm]
        two_d = s_f32.shape[1:]
        res = _sum_adamw(which, s_f32, oth, w.reshape(two_d), m.reshape(two_d), v.reshape(two_d), "adamw_" + nm)
        big_out[nm] = [r.reshape(w.shape) for r in res]

    small_w = (pool_scale, sgu_ln_g, sgu_ln_b, sgu_w, sgu_b, ln_g, ln_b, ple_gate_b)
    small_m = (m_pool_scale, m_sgu_ln_g, m_sgu_ln_b, m_sgu_w, m_sgu_b, m_ln_g, m_ln_b, m_ple_gate_b)
    small_v = (v_pool_scale, v_sgu_ln_g, v_sgu_ln_b, v_sgu_w, v_sgu_b, v_ln_g, v_ln_b, v_ple_gate_b)
    natural = [[a.reshape(g.shape) for a, g in zip(group, small)] for group in (small_w, small_m, small_v)]
    res = _small_sum_adamw(list(gathered), *natural)
    g_s, d_s, m_s, v_s = [[r.reshape(w.shape) for r, w in zip(kind, small_w)] for kind in res]
    total_loss = res[0][-1][0, 0]

    order = ("w_in", "pool_w", "pool_scale", "sgu_ln_g", "sgu_ln_b", "sgu_w", "sgu_b", "w_out", "ln_g", "ln_b",
             "ple_w", "ple_gate_w", "ple_gate_b")
    outs = [total_loss, dx.reshape(1, seq, D_MODEL)]
    for kind in range(4):
        for nm in order:
            key = "gate_w" if nm == "ple_gate_w" else nm
            if key in big_out:
                outs.append(big_out[key][kind])
            else:
                outs.append((g_s, d_s, m_s, v_s)[kind][SMALL_NAMES.index(nm)])
    return tuple(outs)
```

```python
from typing import Callable, NamedTuple

import numpy as np
import jax
import jax.numpy as jnp
from jax import lax
from jax.experimental import pallas as pl
from jax.experimental.pallas import tpu as pltpu

F32 = jnp.float32
BF16 = jnp.bfloat16

N_DEV = 8
D_MODEL = 1024
D_POOL = 1024
D_SGU = 1024
D_MIX = 2048
D_IN = 5120
D_PLE = 256
POOL_WINDOWS = (2, 4, 8, 16)
POOL_GROUP = 256
N_HEADS = 4
HEAD = 256
CHUNK = 128
HALO = 16
BAND_PAD = 128
ALPHA = 2.0 ** 0.25
LN_EPS = 1e-5
ADAM_LR, ADAM_B1, ADAM_B2, ADAM_EPS, ADAM_WD, ADAM_STEP = 0.001, 0.9, 0.999, 1e-08, 0.01, 10

U0, V0, Z0 = D_POOL, D_POOL + D_SGU, D_POOL + 2 * D_SGU
VMEM_LIMIT = 56 * 1024 * 1024
MESH = pl.DeviceIdType.MESH
ANY = pl.BlockSpec(memory_space=pl.ANY)
VMEM_FULL = pl.BlockSpec(memory_space=pltpu.VMEM)

_GELU_C0 = 0.7978845608028654
_GELU_C1 = 0.044715


def _gelu_cdf(x, x2):
    return 1.0 / (1.0 + jnp.exp(x * ((-2.0 * _GELU_C0) + (-2.0 * _GELU_C0 * _GELU_C1) * x2)))


def _gelu_and_grad(x):
    x2 = x * x
    cdf = _gelu_cdf(x, x2)
    g = x * cdf
    dg = cdf + g * (1.0 - cdf) * ((2.0 * _GELU_C0) + (6.0 * _GELU_C0 * _GELU_C1) * x2)
    return g, dg


def _gelu(x):
    t = jnp.tanh(_GELU_C0 * (x + _GELU_C1 * (x * x * x)))
    return x * (0.5 * (1.0 + t))


def _split_bf16(x):
    hi = x.astype(BF16)
    return hi, (x - hi.astype(F32)).astype(BF16)


def _band(tok_tile, window, transpose):
    t = np.arange(tok_tile)[:, None]
    s = np.arange(tok_tile + BAND_PAD)[None, :]
    d = (s - t) if transpose else (t + BAND_PAD - s)
    return ((d >= 0) & (d < window)).astype(np.float32)


def _bands(tok_tile, transpose):
    return jnp.asarray(np.stack([_band(tok_tile, w, transpose) for w in POOL_WINDOWS]), dtype=BF16)


def _sigmoid(x):
    return 1.0 / (1.0 + jnp.exp(-x))


def _dot(a, b):
    return jnp.dot(a, b, preferred_element_type=F32)


def _dot_nt(a, b):
    return lax.dot_general(a, b, (((1,), (1,)), ((), ())), preferred_element_type=F32)


def _dot_tn(a, b):
    return lax.dot_general(a, b, (((0,), (0,)), ((), ())), preferred_element_type=F32)


def _row_stats(x):
    mu = jnp.mean(x, axis=-1, keepdims=True)
    xc = x - mu
    var = jnp.mean(xc * xc, axis=-1, keepdims=True)
    rstd = lax.rsqrt(var + LN_EPS)
    return xc * rstd, rstd


def _ln_bwd(dxhat, xhat, rstd):
    m1 = jnp.mean(dxhat, axis=-1, keepdims=True)
    m2 = jnp.mean(dxhat * xhat, axis=-1, keepdims=True)
    return rstd * (dxhat - m1 - xhat * m2)


def _masked_sgu_w(sw_ref, hh):
    row = lax.broadcasted_iota(jnp.int32, (CHUNK, CHUNK), 0)
    col = lax.broadcasted_iota(jnp.int32, (CHUNK, CHUNK), 1)
    return jnp.where(row >= col, sw_ref[hh], 0.0)


def _inv_count(tile_index, tok_tile, window):
    tok = tile_index * tok_tile + lax.broadcasted_iota(jnp.int32, (tok_tile, 1), 0)
    return 1.0 / jnp.minimum(tok + 1, window).astype(F32)


def _params(**kw):
    return pltpu.CompilerParams(vmem_limit_bytes=VMEM_LIMIT, **kw)


def _forward_mixers(x, w_in, pool_w, pool_scale, sgu_ln_g, sgu_ln_b, sgu_w, sgu_bias_tile, tok_tile, sides=()):
    seq = x.shape[0]
    n_tiles = seq // tok_tile
    n_chunks = tok_tile // CHUNK
    side_refs = _SideRefs(sides, 8, 4, 2)

    def body(*refs):
        (x_ref, win_ref, pw_ref, ps_ref, lg_ref, lb_ref, sw_ref, sb_ref,
         hb_ref, y_ref, pooled_ref, xb_ref, aext_ref, h_ref) = side_refs.split(refs)
        i = pl.program_id(0)
        side_refs.emit(i == 0, i == n_tiles // 2, False)

        @pl.when(i == 0)
        def _():
            aext_ref[0:HALO, :] = jnp.zeros((HALO, D_POOL), F32)

        xb = x_ref[...].astype(BF16)
        xb_ref[...] = xb
        for s in range(D_IN // 1024):
            cs = slice(s * 1024, (s + 1) * 1024)
            section = _dot(xb, win_ref[:, cs])
            h_ref[:, cs] = section
            if s >= 1:
                hb_ref[:, (s - 1) * 1024:s * 1024] = section.astype(BF16)

        aext_ref[HALO:HALO + tok_tile, :] = h_ref[:, 0:D_POOL]
        for g, window in enumerate(POOL_WINDOWS):
            cols = slice(g * POOL_GROUP, (g + 1) * POOL_GROUP)
            win = aext_ref[HALO:HALO + tok_tile, cols]
            for k in range(1, window):
                win = win + aext_ref[HALO - k:HALO - k + tok_tile, cols]
            pooled = win * _inv_count(i, tok_tile, window) - h_ref[:, cols]
            pb = pooled.astype(BF16)
            pooled_ref[:, cols] = pb
            mixed = _dot(pb, pw_ref[g])
            z = h_ref[:, Z0 + g * POOL_GROUP:Z0 + (g + 1) * POOL_GROUP]
            y_ref[:, cols] = (mixed * ps_ref[:, cols] * (z * _sigmoid(z))).astype(BF16)
        aext_ref[0:HALO, :] = aext_ref[tok_tile:tok_tile + HALO, :]

        for hh in range(N_HEADS):
            cols = slice(hh * HEAD, (hh + 1) * HEAD)
            swm = _masked_sgu_w(sw_ref, hh).astype(BF16)
            for n in range(n_chunks):
                rows = slice(n * CHUNK, (n + 1) * CHUNK)
                gu = _gelu(h_ref[rows, U0 + hh * HEAD:U0 + (hh + 1) * HEAD])
                gv = _gelu(h_ref[rows, V0 + hh * HEAD:V0 + (hh + 1) * HEAD])
                xhat, _ = _row_stats(gv)
                vln = xhat * lg_ref[:, cols] + lb_ref[:, cols]
                sv = _dot(swm, vln.astype(BF16)) + sb_ref[:, cols]
                z = h_ref[rows, Z0 + D_POOL + hh * HEAD:Z0 + D_POOL + (hh + 1) * HEAD]
                y_ref[rows, D_POOL + hh * HEAD:D_POOL + (hh + 1) * HEAD] = (
                    gu * sv * (z * _sigmoid(z))).astype(BF16)

        side_refs.emit(False, False, i == n_tiles - 1)

    tok = lambda width: pl.BlockSpec((tok_tile, width), lambda i: (i, 0))
    outs = pl.pallas_call(
        body, name="forward_mixers",
        grid=(n_tiles,),
        in_specs=[tok(D_MODEL)] + [VMEM_FULL] * 7 + side_refs.in_specs,
        out_specs=[tok(D_IN - D_POOL), tok(D_MIX), tok(D_POOL), tok(D_MODEL)] + side_refs.out_specs,
        out_shape=[jax.ShapeDtypeStruct((seq, D_IN - D_POOL), BF16), jax.ShapeDtypeStruct((seq, D_MIX), BF16),
                   jax.ShapeDtypeStruct((seq, D_POOL), BF16), jax.ShapeDtypeStruct((seq, D_MODEL), BF16)]
        + side_refs.out_shapes,
        scratch_shapes=[pltpu.VMEM((HALO + tok_tile, D_POOL), F32), pltpu.VMEM((tok_tile, D_IN), F32)]
        + side_refs.scratch,
        compiler_params=_params(dimension_semantics=("arbitrary",)),
    )(x, w_in, pool_w, pool_scale, sgu_ln_g, sgu_ln_b, sgu_w, sgu_bias_tile, *side_refs.inputs)
    return outs[:4], outs[4:]


def _head_fwd_bwd(x, y, p, target, w_out, gate_w, ple_w, ln_g, ln_b, gate_b, tok_tile, sub_tile):
    seq = x.shape[0]

    def body(x_ref, y_ref, p_ref, t_ref, wout_ref, gw_ref, plw_ref, lng_ref, lnb_ref, gb_ref,
             dy_ref, dxr_ref, xn_ref, dgp_ref, dpe_ref, dr_ref, loss_ref, dlng_ref, dlnb_ref, dgb_ref):
        i = pl.program_id(0)

        @pl.when(i == 0)
        def _():
            loss_ref[...] = jnp.zeros_like(loss_ref)
            dlng_ref[...] = jnp.zeros_like(dlng_ref)
            dlnb_ref[...] = jnp.zeros_like(dlnb_ref)
            dgb_ref[...] = jnp.zeros_like(dgb_ref)

        subs = [slice(s * sub_tile, (s + 1) * sub_tile) for s in range(tok_tile // sub_tile)]
        stats, xns, douts = [], [], []
        for rows in subs:
            r = ALPHA * x_ref[rows, :] + _dot(y_ref[rows, :], wout_ref[...])
            xhat, rstd = _row_stats(r)
            xn = xhat * lng_ref[...] + lnb_ref[...]
            xn_ref[rows, :] = xn.astype(BF16)
            stats.append((xhat, rstd))
            xns.append(xn)
        loss = jnp.zeros((1, 1), F32)
        dgb = jnp.zeros((1, D_MODEL), F32)
        for rows, xn in zip(subs, xns):
            gate = _sigmoid(_dot(xn_ref[rows, :], gw_ref[...]) + gb_ref[...])
            pe = _dot(p_ref[rows, :].astype(BF16), plw_ref[...])
            err = xn + gate * pe - t_ref[rows, :]
            loss = loss + jnp.sum(err * err, keepdims=True)
            dout = err * (1.0 / D_MODEL)
            dpe_ref[rows, :] = (dout * gate).astype(BF16)
            dgpre = dout * pe * gate * (1.0 - gate)
            dgb = dgb + jnp.sum(dgpre, axis=0, keepdims=True)
            dgp_ref[rows, :] = dgpre.astype(BF16)
            douts.append(dout)
        loss_ref[...] += (0.5 / D_MODEL) * loss
        dgb_ref[...] += dgb
        dlng = jnp.zeros((1, D_MODEL), F32)
        dlnb = jnp.zeros((1, D_MODEL), F32)
        for rows, (xhat, rstd), dout in zip(subs, stats, douts):
            dxn = dout + _dot_nt(dgp_ref[rows, :], gw_ref[...])
            dlng = dlng + jnp.sum(dxn * xhat, axis=0, keepdims=True)
            dlnb = dlnb + jnp.sum(dxn, axis=0, keepdims=True)
            dr = _ln_bwd(dxn * lng_ref[...], xhat, rstd)
            dxr_ref[rows, :] = ALPHA * dr
            dr_ref[rows, :] = dr.astype(BF16)
        dlng_ref[...] += dlng
        dlnb_ref[...] += dlnb
        for rows in subs:
            dy_ref[rows, :] = _dot_nt(dr_ref[rows, :], wout_ref[...])

    tok = lambda width: pl.BlockSpec((tok_tile, width), lambda i: (i, 0))
    acc = lambda width: pl.BlockSpec((1, width), lambda i: (0, 0))
    vec = jax.ShapeDtypeStruct((1, D_MODEL), F32)
    return pl.pallas_call(
        body, name="head_fwd_bwd",
        grid=(seq // tok_tile,),
        in_specs=[tok(D_MODEL), tok(D_MIX), tok(D_PLE), tok(D_MODEL),
                  VMEM_FULL, VMEM_FULL, VMEM_FULL, VMEM_FULL, VMEM_FULL, VMEM_FULL],
        out_specs=[tok(D_MIX), tok(D_MODEL), tok(D_MODEL), tok(D_MODEL), tok(D_MODEL), tok(D_MODEL),
                   acc(128), acc(D_MODEL), acc(D_MODEL), acc(D_MODEL)],
        out_shape=[jax.ShapeDtypeStruct((seq, D_MIX), F32), jax.ShapeDtypeStruct((seq, D_MODEL), F32),
                   jax.ShapeDtypeStruct((seq, D_MODEL), BF16), jax.ShapeDtypeStruct((seq, D_MODEL), BF16),
                   jax.ShapeDtypeStruct((seq, D_MODEL), BF16), jax.ShapeDtypeStruct((seq, D_MODEL), BF16),
                   jax.ShapeDtypeStruct((1, 128), F32), vec, vec, vec],
        compiler_params=_params(dimension_semantics=("arbitrary",)),
    )(x, y, p, target, w_out, gate_w, ple_w, ln_g, ln_b, gate_b)


def _mixers_bwd(h, dy, dxr, pooled, w_in, pool_w, pool_scale, sgu_ln_g, sgu_ln_b, sgu_w, sgu_bias_tile, tok_tile):
    seq = h.shape[0]
    n_tiles = seq // tok_tile
    n_chunks = tok_tile // CHUNK
    pool_rows = POOL_GROUP // N_DEV

    def body(h_ref, dy_ref, dxr_ref, pooled_ref, win_ref, pw_ref, ps_ref, lg_ref, lb_ref, sw_ref, sb_ref, band_ref,
             dh_ref, dx_ref, dpw_ref, dps_ref, dlg_ref, dlb_ref, dsw_ref, dsb_ref,
             qhi_ref, qlo_ref, dpw_acc, dsb_acc):
        i = pl.program_id(0)
        tile = n_tiles - 1 - i

        @pl.when(i == 0)
        def _():
            qhi_ref[...] = jnp.zeros_like(qhi_ref)
            qlo_ref[...] = jnp.zeros_like(qlo_ref)
            dpw_acc[...] = jnp.zeros_like(dpw_acc)
            dsb_acc[...] = jnp.zeros_like(dsb_acc)
            dps_ref[...] = jnp.zeros_like(dps_ref)
            dlg_ref[...] = jnp.zeros_like(dlg_ref)
            dlb_ref[...] = jnp.zeros_like(dlb_ref)
            dsw_ref[...] = jnp.zeros_like(dsw_ref)

        def h_at(rows, cols):
            return h_ref[rows, cols.start - D_POOL:cols.stop - D_POOL].astype(F32)

        everything = slice(0, tok_tile)
        for part in (qhi_ref, qlo_ref):
            part[tok_tile:tok_tile + HALO, :] = part[0:HALO, :]
        for g, window in enumerate(POOL_WINDOWS):
            cols = slice(g * POOL_GROUP, (g + 1) * POOL_GROUP)
            zcols = slice(Z0 + g * POOL_GROUP, Z0 + (g + 1) * POOL_GROUP)
            z = h_at(everything, zcols)
            sz = _sigmoid(z)
            pb = pooled_ref[:, cols]
            mixed = _dot(pb, pw_ref[g])
            dyp = dy_ref[:, cols]
            dh_ref[:, zcols] = (dyp * (mixed * ps_ref[:, cols]) * (sz * (1.0 + z * (1.0 - sz)))).astype(BF16)
            dms = dyp * (z * sz)
            dps_ref[:, cols] += jnp.sum(dms * mixed, axis=0, keepdims=True)
            dmixed = (dms * ps_ref[:, cols]).astype(BF16)
            dpw_acc[g] += _dot_tn(pb, dmixed)
            dpooled = _dot_nt(dmixed, pw_ref[g])
            qhi_ref[everything, cols], qlo_ref[everything, cols] = _split_bf16(
                dpooled * _inv_count(tile, tok_tile, window))
            da = _dot(band_ref[g], qhi_ref[:, cols]) + _dot(band_ref[g], qlo_ref[:, cols]) - dpooled
            dh_ref[:, cols] = da.astype(BF16)

        pool_cols = [slice(o + g * POOL_GROUP, o + (g + 1) * POOL_GROUP)
                     for o in (0, Z0) for g in range(len(POOL_WINDOWS))]
        pool_slices = [pool_cols[0:3], pool_cols[3:6], pool_cols[6:8], []]
        for hh in range(N_HEADS):
            cols = slice(hh * HEAD, (hh + 1) * HEAD)
            ucols = slice(U0 + hh * HEAD, U0 + (hh + 1) * HEAD)
            vcols = slice(V0 + hh * HEAD, V0 + (hh + 1) * HEAD)
            zcols = slice(Z0 + D_POOL + hh * HEAD, Z0 + D_POOL + (hh + 1) * HEAD)
            sw32 = _masked_sgu_w(sw_ref, hh)
            swm = sw32.astype(BF16)
            swm_t = sw32.T.astype(BF16)
            for n in range(n_chunks):
                rows = slice(n * CHUNK, (n + 1) * CHUNK)
                gu, dgu_du = _gelu_and_grad(h_at(rows, ucols))
                gv, dgv_dv = _gelu_and_grad(h_at(rows, vcols))
                xhat, rstd = _row_stats(gv)
                vb = (xhat * lg_ref[:, cols] + lb_ref[:, cols]).astype(BF16)
                sv = _dot(swm, vb) + sb_ref[:, cols]
                z = h_at(rows, zcols)
                sz = _sigmoid(z)
                dys = dy_ref[rows, D_POOL + hh * HEAD:D_POOL + (hh + 1) * HEAD]
                dh_ref[rows, zcols] = (dys * (gu * sv) * (sz * (1.0 + z * (1.0 - sz)))).astype(BF16)
                dyg = dys * (z * sz)
                dh_ref[rows, ucols] = (dyg * sv * dgu_du).astype(BF16)
                dsv = dyg * gu
                dsb_acc[:, cols] += dsv
                dsvb = dsv.astype(BF16)
                dsw_ref[hh] += _dot_nt(dsvb, vb)
                dvln = _dot(swm_t, dsvb)
                dlg_ref[:, cols] += jnp.sum(dvln * xhat, axis=0, keepdims=True)
                dlb_ref[:, cols] += jnp.sum(dvln, axis=0, keepdims=True)
                dgv = _ln_bwd(dvln * lg_ref[:, cols], xhat, rstd)
                dh_ref[rows, vcols] = (dgv * dgv_dv).astype(BF16)
            ready = [ucols, vcols, zcols] + pool_slices[hh]
            part = _dot_nt(dh_ref[:, ready[0]], win_ref[:, ready[0]])
            for sl in ready[1:]:
                part = part + _dot_nt(dh_ref[:, sl], win_ref[:, sl])
            if hh == 0:
                dx_ref[...] = dxr_ref[...] + part
            else:
                dx_ref[...] += part

        @pl.when(i == n_tiles - 1)
        def _():
            for g in range(len(POOL_WINDOWS)):
                for b in range(N_DEV):
                    dpw_ref[b, g] = dpw_acc[g, b * pool_rows:(b + 1) * pool_rows, :]
            row = lax.broadcasted_iota(jnp.int32, (CHUNK, CHUNK), 0)
            col = lax.broadcasted_iota(jnp.int32, (CHUNK, CHUNK), 1)
            for hh in range(N_HEADS):
                dsw_ref[hh] = jnp.where(row >= col, dsw_ref[hh], 0.0)
                total = jnp.sum(dsb_acc[:, hh * HEAD:(hh + 1) * HEAD], axis=1, keepdims=True)
                dsb_ref[hh:hh + 1, :] = jnp.broadcast_to(total, (CHUNK, CHUNK)).T[0:1, :]

    tok = lambda width: pl.BlockSpec((tok_tile, width), lambda i: (n_tiles - 1 - i, 0))
    whole = lambda shape: pl.BlockSpec(shape, lambda i: (0,) * len(shape))
    vec = jax.ShapeDtypeStruct((1, D_MODEL), F32)
    return pl.pallas_call(
        body, name="mixers_bwd",
        grid=(n_tiles,),
        in_specs=[tok(D_IN - D_POOL), tok(D_MIX), tok(D_MODEL), tok(D_POOL)] + [VMEM_FULL] * 8,
        out_specs=[tok(D_IN), tok(D_MODEL), whole((N_DEV, len(POOL_WINDOWS), pool_rows, POOL_GROUP)),
                   whole((1, D_POOL)), whole((1, D_SGU)), whole((1, D_SGU)),
                   whole((N_HEADS, CHUNK, CHUNK)), whole((N_HEADS, CHUNK))],
        out_shape=[jax.ShapeDtypeStruct((seq, D_IN), BF16), jax.ShapeDtypeStruct((seq, D_MODEL), F32),
                   jax.ShapeDtypeStruct((N_DEV, len(POOL_WINDOWS), pool_rows, POOL_GROUP), F32),
                   vec, vec, vec,
                   jax.ShapeDtypeStruct((N_HEADS, CHUNK, CHUNK), F32),
                   jax.ShapeDtypeStruct((N_HEADS, CHUNK), F32)],
        scratch_shapes=[pltpu.VMEM((tok_tile + BAND_PAD, D_POOL), BF16),
                        pltpu.VMEM((tok_tile + BAND_PAD, D_POOL), BF16),
                        pltpu.VMEM((len(POOL_WINDOWS), POOL_GROUP, POOL_GROUP), F32),
                        pltpu.VMEM((CHUNK, D_SGU), F32)],
        compiler_params=_params(dimension_semantics=("arbitrary",)),
    )(h, dy, dxr, pooled, w_in, pool_w, pool_scale, sgu_ln_g, sgu_ln_b, sgu_w, sgu_bias_tile, _bands(tok_tile, True))


def _weight_grad(a, b, n_col_blocks, blocks_per_step, tok_tile, name, sides=()):
    seq, m = a.shape
    n = b.shape[1]
    nb = n // n_col_blocks
    n_steps = n_col_blocks // blocks_per_step
    n_k = seq // tok_tile
    side_refs = _SideRefs(sides, 2, 1, 0)

    def body(*refs):
        a_ref, b_ref, out_ref = side_refs.split(refs)
        step = pl.program_id(0) * n_k + pl.program_id(1)
        side_refs.emit(step == 0, step == (n_steps * n_k) // 2, False)

        @pl.when(pl.program_id(1) == 0)
        def _():
            out_ref[...] = jnp.zeros_like(out_ref)

        res = _dot_tn(a_ref[...].astype(BF16), b_ref[...])
        for blk in range(blocks_per_step):
            out_ref[blk] += res[:, blk * nb:(blk + 1) * nb]

        side_refs.emit(False, False, step == n_steps * n_k - 1)

    outs = pl.pallas_call(
        body, name=name,
        grid=(n_steps, n_k),
        in_specs=[pl.BlockSpec((tok_tile, m), lambda j, k: (k, 0)),
                  pl.BlockSpec((tok_tile, blocks_per_step * nb), lambda j, k: (k, j))] + side_refs.in_specs,
        out_specs=[pl.BlockSpec((blocks_per_step, m, nb), lambda j, k: (j, 0, 0))] + side_refs.out_specs,
        out_shape=[jax.ShapeDtypeStruct((n_col_blocks, m, nb), F32)] + side_refs.out_shapes,
        scratch_shapes=side_refs.scratch,
        compiler_params=_params(dimension_semantics=("arbitrary", "arbitrary")),
    )(a, b, *side_refs.inputs)
    return outs[0], outs[1:]


class _Side(NamedTuple):
    inputs: list
    out_shapes: list
    sem_shapes: list
    emit: Callable


def _when(cond):
    if cond is True:
        return lambda f: f()
    if cond is False:
        return lambda f: None
    return pl.when(cond)


def _place():
    return lax.axis_index("x"), lax.axis_index("y"), lax.axis_index("c")


def _other_chips(x, y):
    return [(1 - x, y), (x, 1 - y), (1 - x, 1 - y)]


def _gather_side(shards, out_shapes, views, halves):
    n = len(shards)
    n_sems = 10

    def emit(ins, outs, sems, first, mid, last):
        send_sems, recv_sems, local_sems = sems
        x, y, c = _place()
        here, x_nbr, y_nbr, diag = (x, y), (1 - x, y), (x, 1 - y), (1 - x, 1 - y)
        sibling = (x, y, 1 - c)

        def block(k, chip, core):
            return views[k](outs[k], 4 * chip[0] + 2 * chip[1] + core)

        def piece(k, ref, p):
            return ref if halves[k] is None else halves[k](ref)[p]

        def n_pieces(k):
            return 1 if halves[k] is None else 2

        def copy(k, s, src, dst, to):
            return pltpu.make_async_remote_copy(
                src_ref=src, dst_ref=dst, send_sem=send_sems.at[k, s], recv_sem=recv_sems.at[k, s],
                device_id=to, device_id_type=MESH)

        def outgoing(k, s):
            mine = block(k, here, c)
            if s == 0:
                return copy(k, 0, ins[k], mine, sibling)
            if s in (1, 2):
                return copy(k, s, piece(k, ins[k], s - 1), piece(k, mine, s - 1), (*x_nbr, c))
            if s in (3, 4):
                return copy(k, s, piece(k, ins[k], s - 3), piece(k, mine, s - 3), (*y_nbr, c))
            if s == 5:
                part = piece(k, block(k, x_nbr, c), 0)
                return copy(k, 5, part, part, (*y_nbr, c))
            if s == 6:
                part = piece(k, block(k, y_nbr, c), 1)
                return copy(k, 6, part, part, (*x_nbr, c))
            whole = block(k, (x_nbr, y_nbr, diag)[s - 7], c)
            return copy(k, s, whole, whole, sibling)

        def incoming(k, s):
            if s == 0:
                zone = block(k, here, 1 - c)
            elif s in (1, 2):
                zone = piece(k, block(k, x_nbr, c), s - 1)
            elif s in (3, 4):
                zone = piece(k, block(k, y_nbr, c), s - 3)
            elif s in (5, 6):
                zone = piece(k, block(k, diag, c), s - 5)
            else:
                zone = block(k, (x_nbr, y_nbr, diag)[s - 7], 1 - c)
            return copy(k, s, zone, zone, sibling)

        def own(k):
            return pltpu.make_async_copy(ins[k], block(k, here, c), local_sems.at[k])

        def used(k):
            return list(range(n_sems)) if n_pieces(k) == 2 else [0, 1, 3, 5, 7, 8, 9]

        @_when(first)
        def _():
            for k in range(n):
                own(k).start()
                for s in used(k):
                    if s <= 4:
                        outgoing(k, s).start()

        @_when(mid)
        def _():
            for k in range(n):
                incoming(k, 1).wait_recv()
                outgoing(k, 5).start()
                if n_pieces(k) == 2:
                    incoming(k, 4).wait_recv()
                    outgoing(k, 6).start()
                    incoming(k, 2).wait_recv()
                outgoing(k, 7).start()
                incoming(k, 3).wait_recv()
                outgoing(k, 8).start()

        @_when(last)
        def _():
            for k in range(n):
                incoming(k, 5).wait_recv()
                if n_pieces(k) == 2:
                    incoming(k, 6).wait_recv()
                outgoing(k, 9).start()
            for k in range(n):
                for s in (0, 7, 8, 9):
                    incoming(k, s).wait_recv()
            for k in range(n):
                for s in used(k):
                    outgoing(k, s).wait_send()
                own(k).wait()

    sems = [pltpu.SemaphoreType.DMA((n, n_sems)), pltpu.SemaphoreType.DMA((n, n_sems)),
            pltpu.SemaphoreType.DMA((n,))]
    return _Side(list(shards), list(out_shapes), sems, emit)


def _pair_side(grads):
    n = len(grads)

    def emit(ins, theirs, sems, first, mid, last):
        send_sems, recv_sems = sems
        x, y, c = _place()

        def copies():
            return [pltpu.make_async_remote_copy(
                src_ref=ins[k].at[2 * j + (1 - c)], dst_ref=theirs[k].at[j],
                send_sem=send_sems.at[k, j], recv_sem=recv_sems.at[k, j],
                device_id=(x, y, 1 - c), device_id_type=MESH) for k in range(n) for j in range(4)]

        @_when(first)
        def _():
            for cp in copies():
                cp.start()

        @_when(last)
        def _():
            for cp in copies():
                cp.wait_recv()
            for cp in copies():
                cp.wait_send()

    shapes = [jax.ShapeDtypeStruct((4,) + g.shape[1:], g.dtype) for g in grads]
    return _Side(list(grads), shapes, [pltpu.SemaphoreType.DMA((n, 4)), pltpu.SemaphoreType.DMA((n, 4))], emit)


def _chip_side(sums):
    n = len(sums)

    def emit(ins, others, sems, first, mid, last):
        send_sems, recv_sems = sems
        x, y, c = _place()

        def copies():
            return [pltpu.make_async_remote_copy(
                src_ref=ins[k].at[2 * px + py], dst_ref=others[k].at[r],
                send_sem=send_sems.at[k, r], recv_sem=recv_sems.at[k, r],
                device_id=(px, py, c), device_id_type=MESH)
                for k in range(n) for r, (px, py) in enumerate(_other_chips(x, y))]

        @_when(first)
        def _():
            for cp in copies():
                cp.start()

        @_when(last)
        def _():
            for cp in copies():
                cp.wait_recv()
            for cp in copies():
                cp.wait_send()

    shapes = [jax.ShapeDtypeStruct((3,) + s.shape[1:], s.dtype) for s in sums]
    return _Side(list(sums), shapes, [pltpu.SemaphoreType.DMA((n, 3)), pltpu.SemaphoreType.DMA((n, 3))], emit)


def _comm_call(side, name):
    n_in, n_out = len(side.inputs), len(side.out_shapes)

    def body(*refs):
        side.emit(refs[:n_in], refs[n_in:n_in + n_out], refs[n_in + n_out:], True, True, True)

    return pl.pallas_call(
        body, name=name, in_specs=[ANY] * n_in, out_specs=[ANY] * n_out,
        out_shape=side.out_shapes, scratch_shapes=side.sem_shapes,
    )(*side.inputs)


class _SideRefs:
    def __init__(self, sides, n_in, n_out, n_scratch):
        self.sides, self.n_in, self.n_out, self.n_scratch = sides, n_in, n_out, n_scratch
        self.inputs = [a for s in sides for a in s.inputs]
        self.out_shapes = [o for s in sides for o in s.out_shapes]
        self.scratch = [m for s in sides for m in s.sem_shapes]
        self.in_specs = [ANY] * len(self.inputs)
        self.out_specs = [ANY] * len(self.out_shapes)

    def split(self, refs):
        refs = list(refs)
        n_side_in, n_side_out = len(self.inputs), len(self.out_shapes)
        ins, rest = refs[:self.n_in], refs[self.n_in:]
        side_in, rest = rest[:n_side_in], rest[n_side_in:]
        outs, rest = rest[:self.n_out], rest[self.n_out:]
        side_out, rest = rest[:n_side_out], rest[n_side_out:]
        scratch, side_sems = rest[:self.n_scratch], rest[self.n_scratch:]
        self._refs = (side_in, side_out, side_sems)
        return ins + outs + scratch

    def emit(self, first, mid, last):
        side_in, side_out, side_sems = self._refs
        for s in self.sides:
            a, b, m = len(s.inputs), len(s.out_shapes), len(s.sem_shapes)
            s.emit(side_in[:a], side_out[:b], side_sems[:m], first, mid, last)
            side_in, side_out, side_sems = side_in[a:], side_out[b:], side_sems[m:]


ROW_RELATIONS = (2, 0, 1)


def _row_order(x, y):
    chips = _other_chips(x, y)
    return jnp.stack([2 * px + py for px, py in [chips[r] for r in ROW_RELATIONS] + [(x, y)]]).astype(jnp.int32)


def _grad_w_in_reduced(order, a, b, tok_tile, sides=()):
    seq, m = a.shape
    nb = b.shape[1] // N_DEV
    n_k = seq // tok_tile
    n_rows = 4
    last_step = n_rows * n_k - 1
    assert n_k >= 3
    fetch_at, sum_at = 1, 2
    side_refs = _SideRefs(sides, 3, 3, 8)

    def body(*refs):
        (order_ref, a_ref, b_ref, own_ref, theirs_ref, others_ref,
         acc_ref, stage_ref, sumbf_ref, pair_send, pair_recv, ici_send, ici_recv, stage_sem) = side_refs.split(refs)
        j, k = pl.program_id(0), pl.program_id(1)
        step = j * n_k + k
        side_refs.emit(step == 0, step == (n_rows * n_k) // 2, False)
        x, y, c = _place()
        chips = _other_chips(x, y)

        def to_sibling(row):
            return pltpu.make_async_remote_copy(
                src_ref=acc_ref.at[row % 2, 1 - c], dst_ref=theirs_ref.at[row],
                send_sem=pair_send.at[row], recv_sem=pair_recv.at[row],
                device_id=(x, y, 1 - c), device_id_type=MESH)

        def to_owner(row):
            rel = ROW_RELATIONS[row]
            px, py = chips[rel]
            return pltpu.make_async_remote_copy(
                src_ref=sumbf_ref.at[row], dst_ref=others_ref.at[rel],
                send_sem=ici_send.at[rel], recv_sem=ici_recv.at[rel],
                device_id=(px, py, c), device_id_type=MESH)

        def staged(row):
            return pltpu.make_async_copy(theirs_ref.at[row], stage_ref, stage_sem.at[0])

        @pl.when(k == 0)
        def _():
            acc_ref[j % 2] = jnp.zeros((2, m, nb), F32)

        res = _dot_tn(a_ref[...].astype(BF16), b_ref[...])
        for blk in range(2):
            acc_ref[j % 2, blk] += res[:, blk * nb:(blk + 1) * nb]

        for row in range(n_rows):
            @pl.when((j == row) & (k == n_k - 1))
            def _():
                to_sibling(row).start()

            if row < n_rows - 1:
                @pl.when((j == row + 1) & (k == fetch_at))
                def _():
                    to_sibling(row).wait_recv()
                    staged(row).start()

                @pl.when((j == row + 1) & (k == sum_at))
                def _():
                    staged(row).wait()
                    to_sibling(row).wait_send()
                    sumbf_ref[row] = (acc_ref[row % 2, c] + stage_ref[...]).astype(BF16)
                    to_owner(row).start()

        @pl.when(step == last_step)
        def _():
            row = n_rows - 1
            to_sibling(row).wait_recv()
            staged(row).start()
            staged(row).wait()
            to_sibling(row).wait_send()
            own_ref[...] = acc_ref[row % 2, c] + stage_ref[...]
            for r in range(n_rows - 1):
                to_owner(r).wait_recv()
                to_owner(r).wait_send()

        side_refs.emit(False, False, step == last_step)

    block = jax.ShapeDtypeStruct((m, nb), F32)
    outs = pl.pallas_call(
        body, name="grad_w_in",
        grid_spec=pltpu.PrefetchScalarGridSpec(
            num_scalar_prefetch=1, grid=(n_rows, n_k),
            in_specs=[pl.BlockSpec((tok_tile, m), lambda j, k, order_ref: (k, 0)),
                      pl.BlockSpec((tok_tile, 2 * nb), lambda j, k, order_ref: (k, order_ref[j]))]
            + side_refs.in_specs,
            out_specs=[pl.BlockSpec((m, nb), lambda j, k, order_ref: (0, 0)), ANY, ANY] + side_refs.out_specs,
            scratch_shapes=[pltpu.VMEM((2, 2, m, nb), F32), pltpu.VMEM((m, nb), F32),
                            pltpu.VMEM((n_rows - 1, m, nb), BF16),
                            pltpu.SemaphoreType.DMA((n_rows,)), pltpu.SemaphoreType.DMA((n_rows,)),
                            pltpu.SemaphoreType.DMA((n_rows - 1,)), pltpu.SemaphoreType.DMA((n_rows - 1,)),
                            pltpu.SemaphoreType.DMA((1,))] + side_refs.scratch),
        out_shape=[block, jax.ShapeDtypeStruct((n_rows, m, nb), F32),
                   jax.ShapeDtypeStruct((n_rows - 1, m, nb), BF16)] + side_refs.out_shapes,
        compiler_params=_params(dimension_semantics=("arbitrary", "arbitrary")),
    )(order, a, b, *side_refs.inputs)
    return outs[0], outs[2], outs[3:]


def _row_tile(rows, cols):
    tile = rows
    while tile * cols > 256 * 1024 and tile % 16 == 0:
        tile //= 2
    return tile


def _pair_sum(core, grads, theirs, name):
    _, rows, cols = theirs.shape
    rt = _row_tile(rows, cols)

    def body(core_ref, a_ref, b_ref, o_ref, ob_ref):
        total = a_ref[...] + b_ref[...]
        o_ref[...] = total
        ob_ref[...] = total.astype(BF16)

    spec = pl.BlockSpec((None, rt, cols), lambda j, i, core_ref: (j, i, 0))
    mine = pl.BlockSpec((None, None, rt, cols), lambda j, i, core_ref: (j, core_ref[0], i, 0))
    return pl.pallas_call(
        body, name=name,
        grid_spec=pltpu.PrefetchScalarGridSpec(
            num_scalar_prefetch=1, grid=(4, rows // rt), in_specs=[mine, spec], out_specs=[spec, spec]),
        out_shape=[jax.ShapeDtypeStruct(theirs.shape, F32), jax.ShapeDtypeStruct(theirs.shape, BF16)],
        compiler_params=_params(dimension_semantics=("arbitrary", "arbitrary")),
    )(core, grads.reshape(4, 2, rows, cols), theirs)


def _adamw(w, g, m, v):
    m = ADAM_B1 * m + (1.0 - ADAM_B1) * g
    v = ADAM_B2 * v + (1.0 - ADAM_B2) * (g * g)
    m_hat = m / (1.0 - ADAM_B1 ** ADAM_STEP)
    v_hat = v / (1.0 - ADAM_B2 ** ADAM_STEP)
    delta = -ADAM_LR * (m_hat / (jnp.sqrt(v_hat) + ADAM_EPS) + ADAM_WD * w)
    return delta, m, v


def _sum_adamw(chip, sums, others, w, m, v, name):
    _, rows, cols = sums.shape
    rt = _row_tile(rows, cols)

    def body(chip_ref, own_ref, oth_ref, w_ref, m_ref, v_ref, g_ref, d_ref, nm_ref, nv_ref):
        g = ((own_ref[...] + oth_ref[0].astype(F32)) + oth_ref[1].astype(F32)) + oth_ref[2].astype(F32)
        g_ref[...] = g
        d_ref[...], nm_ref[...], nv_ref[...] = _adamw(w_ref[...], g, m_ref[...], v_ref[...])

    spec = pl.BlockSpec((rt, cols), lambda i, chip_ref: (i, 0))
    own = pl.BlockSpec((None, rt, cols), lambda i, chip_ref: (chip_ref[0], i, 0))
    shape = jax.ShapeDtypeStruct((rows, cols), F32)
    return pl.pallas_call(
        body, name=name,
        grid_spec=pltpu.PrefetchScalarGridSpec(
            num_scalar_prefetch=1, grid=(rows // rt,),
            in_specs=[own, pl.BlockSpec((3, rt, cols), lambda i, chip_ref: (0, i, 0)), spec, spec, spec],
            out_specs=[spec] * 4),
        out_shape=[shape] * 4,
        compiler_params=_params(dimension_semantics=("arbitrary",)),
    )(chip, sums, others, w, m, v)


def _pair_sum_small(core, grads, theirs, name):
    n = len(grads)

    def body(core_ref, *refs):
        for k in range(n):
            total = refs[k][...] + refs[n + k][...]
            refs[2 * n + 2 * k][...] = total
            refs[2 * n + 2 * k + 1][...] = total.astype(BF16)

    whole = lambda shape: pl.BlockSpec(shape, lambda i, core_ref: (0,) * len(shape))
    mine = [pl.BlockSpec((4, None) + t.shape[1:], lambda i, core_ref: (0, core_ref[0], 0, 0)) for t in theirs]
    outs = pl.pallas_call(
        body, name=name,
        grid_spec=pltpu.PrefetchScalarGridSpec(
            num_scalar_prefetch=1, grid=(1,), in_specs=mine + [whole(t.shape) for t in theirs],
            out_specs=[whole(t.shape) for t in theirs for _ in range(2)]),
        out_shape=[jax.ShapeDtypeStruct(t.shape, dt) for t in theirs for dt in (F32, BF16)],
        compiler_params=_params(dimension_semantics=("arbitrary",)),
    )(core, *[g.reshape((4, 2) + g.shape[1:]) for g in grads], *theirs)
    return [(outs[2 * k], outs[2 * k + 1]) for k in range(n)]


def _sum_adamw_small(chip, items, name):
    n = len(items)

    def body(chip_ref, *refs):
        ins, outs = refs[:5 * n], refs[5 * n:]
        for k in range(n):
            own_ref, oth_ref, w_ref, m_ref, v_ref = ins[5 * k:5 * k + 5]
            g = ((own_ref[...] + oth_ref[0].astype(F32)) + oth_ref[1].astype(F32)) + oth_ref[2].astype(F32)
            outs[4 * k][...] = g
            outs[4 * k + 1][...], outs[4 * k + 2][...], outs[4 * k + 3][...] = _adamw(
                w_ref[...], g, m_ref[...], v_ref[...])

    whole = lambda shape: pl.BlockSpec(shape, lambda i, chip_ref: (0,) * len(shape))
    in_specs, operands, out_specs, out_shape = [], [], [], []
    for sums, others, w, m, v in items:
        in_specs += [pl.BlockSpec((None,) + sums.shape[1:], lambda i, chip_ref: (chip_ref[0], 0, 0)),
                     whole(others.shape), whole(w.shape), whole(m.shape), whole(v.shape)]
        operands += [sums, others, w, m, v]
        out_specs += [whole(w.shape)] * 4
        out_shape += [jax.ShapeDtypeStruct(w.shape, F32)] * 4
    outs = pl.pallas_call(
        body, name=name,
        grid_spec=pltpu.PrefetchScalarGridSpec(num_scalar_prefetch=1, grid=(1,), in_specs=in_specs,
                                               out_specs=out_specs),
        out_shape=out_shape, compiler_params=_params(dimension_semantics=("arbitrary",)),
    )(chip, *operands)
    return [tuple(outs[4 * k:4 * k + 4]) for k in range(n)]


def _small_sum_adamw(gathered, ws, ms, vs):
    n = len(ws)

    def body(*refs):
        g8 = refs[:n + 1]
        w, m, v = refs[n + 1:2 * n + 1], refs[2 * n + 1:3 * n + 1], refs[3 * n + 1:4 * n + 1]
        outs = refs[4 * n + 1:]
        g_out, d_out, m_out, v_out = outs[:n + 1], outs[n + 1:2 * n + 1], outs[2 * n + 1:3 * n + 1], outs[3 * n + 1:]
        for k in range(n + 1):
            g = g8[k][0]
            for b in range(1, N_DEV):
                g = g + g8[k][b]
            g_out[k][...] = g
            if k < n:
                d_out[k][...], m_out[k][...], v_out[k][...] = _adamw(w[k][...], g, m[k][...], v[k][...])

    shapes = [jax.ShapeDtypeStruct(w.shape, F32) for w in ws]
    loss_shape = jax.ShapeDtypeStruct(gathered[-1].shape[1:], F32)
    outs = pl.pallas_call(
        body, name="small_sum_adamw",
        in_specs=[VMEM_FULL] * (4 * n + 1), out_specs=[VMEM_FULL] * (4 * n + 1),
        out_shape=shapes + [loss_shape] + shapes * 3,
        compiler_params=_params(),
    )(*gathered, *ws, *ms, *vs)
    return outs[:n + 1], outs[n + 1:2 * n + 1], outs[2 * n + 1:3 * n + 1], outs[3 * n + 1:]


SMALL_NAMES = ("pool_scale", "sgu_ln_g", "sgu_ln_b", "sgu_w", "sgu_b", "ln_g", "ln_b", "ple_gate_b")


TOK_TILE = 256
GRAD_TILE = 1024


def _weight_views():
    cols = lambda width: (lambda ref, b: ref.at[:, pl.ds(pl.multiple_of(b * width, 128), width)])
    rows = lambda height: (lambda ref, b: ref.at[pl.ds(pl.multiple_of(b * height, 16), height), :])
    pool_rows = POOL_GROUP // N_DEV
    return {"w_in": cols(D_IN // N_DEV),
            "pool_w": lambda ref, b: ref.at[:, pl.ds(pl.multiple_of(b * pool_rows, 16), pool_rows), :],
            "w_out": rows(D_MIX // N_DEV), "ple_w": cols(D_MODEL // N_DEV), "gate_w": rows(D_MODEL // N_DEV)}


WEIGHT_SHAPES = {"w_in": (D_MODEL, D_IN), "pool_w": (len(POOL_WINDOWS), POOL_GROUP, POOL_GROUP),
                 "w_out": (D_MIX, D_MODEL), "ple_w": (D_PLE, D_MODEL), "gate_w": (D_MODEL, D_MODEL)}


def _to_bf16(arrays):
    def body(*refs):
        for src, dst in zip(refs[:len(arrays)], refs[len(arrays):]):
            dst[...] = src[...].astype(BF16)

    return pl.pallas_call(
        body, name="cast_shards", in_specs=[VMEM_FULL] * len(arrays), out_specs=[VMEM_FULL] * len(arrays),
        out_shape=[jax.ShapeDtypeStruct(a.shape, BF16) for a in arrays], compiler_params=_params(),
    )(*arrays)


def _leading_halves(shape):
    whole_tiles = len(shape) >= 3 or shape[0] % 32 == 0
    if shape[0] % 2 or not whole_tiles:
        return None
    half = shape[0] // 2
    return lambda ref: (ref.at[pl.ds(0, half)], ref.at[pl.ds(half, half)])


def _weight_gather(shards, names):
    views = _weight_views()
    return _gather_side([shards[nm] for nm in names],
                        [jax.ShapeDtypeStruct(WEIGHT_SHAPES[nm], BF16) for nm in names], [views[nm] for nm in names],
                        [_leading_halves(shards[nm].shape) for nm in names])


def kernel(x, p, w_in, pool_w, pool_scale, sgu_ln_g, sgu_ln_b, sgu_w, sgu_b, w_out, ln_g, ln_b, ple_w, ple_gate_w, ple_gate_b, loss_target, m_w_in, m_pool_w, m_pool_scale, m_sgu_ln_g, m_sgu_ln_b, m_sgu_w, m_sgu_b, m_w_out, m_ln_g, m_ln_b, m_ple_w, m_ple_gate_w, m_ple_gate_b, v_w_in, v_pool_w, v_pool_scale, v_sgu_ln_g, v_sgu_ln_b, v_sgu_w, v_sgu_b, v_w_out, v_ln_g, v_ln_b, v_ple_w, v_ple_gate_w, v_ple_gate_b):
    seq = x.shape[1]
    x2, p2, target = x[0], p[0, 0], loss_target[0]
    core = lax.axis_index("c").astype(jnp.int32).reshape(1)
    chip = (2 * lax.axis_index("x") + lax.axis_index("y")).astype(jnp.int32).reshape(1)
    pool_rows = POOL_GROUP // N_DEV

    shard_names = ("w_in", "pool_w", "w_out", "ple_w", "gate_w")
    shards = dict(zip(shard_names, _to_bf16([w_in[0], pool_w[0], w_out[0], ple_w[0], ple_gate_w[0]])))
    w_in_f, pool_w_f = _comm_call(_weight_gather(shards, ("w_in", "pool_w")), "gather_mixer_weights")
    bias_tile = jnp.repeat(sgu_b[0].T, HEAD, axis=1)
    (h, y, pooled, xb), (w_out_f, ple_w_f, gate_w_f) = _forward_mixers(
        x2, w_in_f, pool_w_f, pool_scale, sgu_ln_g, sgu_ln_b, sgu_w[0], bias_tile, 2 * TOK_TILE,
        sides=(_weight_gather(shards, ("w_out", "ple_w", "gate_w")),))

    dy, dxr, xn, dgp, dpe, dr, loss, d_ln_g, d_ln_b, d_gate_b = _head_fwd_bwd(
        x2, y, p2, target, w_out_f, gate_w_f, ple_w_f, ln_g, ln_b, ple_gate_b, 2 * TOK_TILE, TOK_TILE)

    d_ple_w, _ = _weight_grad(p2, dpe, N_DEV, N_DEV, GRAD_TILE, "grad_ple_w")
    d_w_out, (ple_theirs,) = _weight_grad(y, dr, 1, 1, GRAD_TILE, "grad_w_out", sides=(_pair_side([d_ple_w]),))
    w_out_blocks = d_w_out.reshape(N_DEV, D_MIX // N_DEV, D_MODEL)
    d_gate_w, (w_out_theirs,) = _weight_grad(xn, dgp, 1, 1, GRAD_TILE, "grad_gate_w",
                                             sides=(_pair_side([w_out_blocks]),))
    gate_blocks = d_gate_w.reshape(N_DEV, D_MODEL // N_DEV, D_MODEL)

    dh, dx, d_pool_w, d_pool_scale, d_sgu_ln_g, d_sgu_ln_b, d_sgu_w, d_sgu_b = _mixers_bwd(
        h, dy, dxr, pooled, w_in_f, pool_w_f, pool_scale, sgu_ln_g, sgu_ln_b, sgu_w[0], bias_tile, TOK_TILE)

    pool_blocks = d_pool_w.reshape(N_DEV, len(POOL_WINDOWS) * pool_rows, POOL_GROUP)
    gate_theirs, pool_theirs = _comm_call(_pair_side([gate_blocks, pool_blocks]), "pair_exchange_late")
    hosted_names = ("w_out", "ple_w", "gate_w", "pool_w")
    ple_sums, pool_sums = _pair_sum_small(core, [d_ple_w, pool_blocks], [ple_theirs, pool_theirs], "pair_sum_small")
    hosted_sums = [_pair_sum(core, w_out_blocks, w_out_theirs, "pair_sum_w_out"), ple_sums,
                   _pair_sum(core, gate_blocks, gate_theirs, "pair_sum_gate_w"), pool_sums]

    small = (d_pool_scale, d_sgu_ln_g, d_sgu_ln_b, d_sgu_w, d_sgu_b, d_ln_g, d_ln_b, d_gate_b)
    parts = small + (loss,)
    small_gather = _gather_side(parts, [jax.ShapeDtypeStruct((N_DEV,) + a.shape, F32) for a in parts],
                                [lambda ref, b: ref.at[b]] * len(parts), [_leading_halves(a.shape) for a in parts])
    cx, cy = lax.axis_index("x"), lax.axis_index("y")
    order = _row_order(cx, cy)
    w_in_own, w_in_others, side_out = _grad_w_in_reduced(
        order, xb, dh, GRAD_TILE, sides=(_chip_side([s_bf for _, s_bf in hosted_sums]), small_gather))
    hosted_others, gathered = side_out[:len(hosted_sums)], side_out[len(hosted_sums):]

    shard_of = {"w_in": (w_in, m_w_in, v_w_in), "pool_w": (pool_w, m_pool_w, v_pool_w),
                "w_out": (w_out, m_w_out, v_w_out), "ple_w": (ple_w, m_ple_w, v_ple_w),
                "gate_w": (ple_gate_w, m_ple_gate_w, v_ple_gate_w)}
    reduced = [(nm, chip, s_f32, oth) for nm, (s_f32, _), oth in zip(hosted_names, hosted_sums, hosted_others)]
    reduced.append(("w_in", jnp.zeros((1,), jnp.int32), w_in_own[None], w_in_others))
    big_out, small_items = {}, []
    for nm, which, s_f32, oth in reduced:
        w, m, v = shard_of[nm]
        two_d = s_f32.shape[1:]
        args = (s_f32, oth, w.reshape(two_d), m.reshape(two_d), v.reshape(two_d))
        if nm in ("ple_w", "pool_w"):
            small_items.append((nm, args))
        else:
            big_out[nm] = [r.reshape(w.shape) for r in _sum_adamw(which, *args, "adamw_" + nm)]
    for (nm, _), res in zip(small_items, _sum_adamw_small(chip, [a for _, a in small_items], "adamw_small")):
        big_out[nm] = [r.reshape(shard_of[nm][0].shape) for r in res]

    small_w = (pool_scale, sgu_ln_g, sgu_ln_b, sgu_w, sgu_b, ln_g, ln_b, ple_gate_b)
    small_m = (m_pool_scale, m_sgu_ln_g, m_sgu_ln_b, m_sgu_w, m_sgu_b, m_ln_g, m_ln_b, m_ple_gate_b)
    small_v = (v_pool_scale, v_sgu_ln_g, v_sgu_ln_b, v_sgu_w, v_sgu_b, v_ln_g, v_ln_b, v_ple_gate_b)
    natural = [[a.reshape(g.shape) for a, g in zip(group, small)] for group in (small_w, small_m, small_v)]
    res = _small_sum_adamw(list(gathered), *natural)
    g_s, d_s, m_s, v_s = [[r.reshape(w.shape) for r, w in zip(kind, small_w)] for kind in res]
    total_loss = res[0][-1][0, 0]

    order = ("w_in", "pool_w", "pool_scale", "sgu_ln_g", "sgu_ln_b", "sgu_w", "sgu_b", "w_out", "ln_g", "ln_b",
             "ple_w", "ple_gate_w", "ple_gate_b")
    outs = [total_loss, dx.reshape(1, seq, D_MODEL)]
    for kind in range(4):
        for nm in order:
            key = "gate_w" if nm == "ple_gate_w" else nm
            if key in big_out:
                outs.append(big_out[key][kind])
            else:
                outs.append((g_s, d_s, m_s, v_s)[kind][SMALL_NAMES.index(nm)])
    return tuple(outs)
```

```python
from typing import Callable, NamedTuple

import numpy as np
import jax
import jax.numpy as jnp
from jax import lax
from jax.experimental import pallas as pl
from jax.experimental.pallas import tpu as pltpu

F32 = jnp.float32
BF16 = jnp.bfloat16

N_DEV = 8
D_MODEL = 1024
D_POOL = 1024
D_SGU = 1024
D_MIX = 2048
D_IN = 5120
D_PLE = 256
POOL_WINDOWS = (2, 4, 8, 16)
POOL_GROUP = 256
N_HEADS = 4
HEAD = 256
CHUNK = 128
HALO = 16
BAND_PAD = 128
ALPHA = 2.0 ** 0.25
LN_EPS = 1e-5
ADAM_LR, ADAM_B1, ADAM_B2, ADAM_EPS, ADAM_WD, ADAM_STEP = 0.001, 0.9, 0.999, 1e-08, 0.01, 10

U0, V0, Z0 = D_POOL, D_POOL + D_SGU, D_POOL + 2 * D_SGU
VMEM_LIMIT = 56 * 1024 * 1024
MESH = pl.DeviceIdType.MESH
ANY = pl.BlockSpec(memory_space=pl.ANY)
VMEM_FULL = pl.BlockSpec(memory_space=pltpu.VMEM)

_GELU_C0 = 0.7978845608028654
_GELU_C1 = 0.044715


def _gelu_cdf(x, x2):
    return 1.0 / (1.0 + jnp.exp(x * ((-2.0 * _GELU_C0) + (-2.0 * _GELU_C0 * _GELU_C1) * x2)))


def _gelu_and_grad(x):
    x2 = x * x
    cdf = _gelu_cdf(x, x2)
    g = x * cdf
    dg = cdf + g * (1.0 - cdf) * ((2.0 * _GELU_C0) + (6.0 * _GELU_C0 * _GELU_C1) * x2)
    return g, dg


def _gelu(x):
    t = jnp.tanh(_GELU_C0 * (x + _GELU_C1 * (x * x * x)))
    return x * (0.5 * (1.0 + t))


def _split_bf16(x):
    hi = x.astype(BF16)
    return hi, (x - hi.astype(F32)).astype(BF16)


def _band(tok_tile, window, transpose):
    t = np.arange(tok_tile)[:, None]
    s = np.arange(tok_tile + BAND_PAD)[None, :]
    d = (s - t) if transpose else (t + BAND_PAD - s)
    return ((d >= 0) & (d < window)).astype(np.float32)


def _bands(tok_tile, transpose):
    return jnp.asarray(np.stack([_band(tok_tile, w, transpose) for w in POOL_WINDOWS]), dtype=BF16)


def _sigmoid(x):
    return 1.0 / (1.0 + jnp.exp(-x))


def _dot(a, b):
    return jnp.dot(a, b, preferred_element_type=F32)


def _dot_nt(a, b):
    return lax.dot_general(a, b, (((1,), (1,)), ((), ())), preferred_element_type=F32)


def _dot_tn(a, b):
    return lax.dot_general(a, b, (((0,), (0,)), ((), ())), preferred_element_type=F32)


def _row_stats(x):
    mu = jnp.mean(x, axis=-1, keepdims=True)
    xc = x - mu
    var = jnp.mean(xc * xc, axis=-1, keepdims=True)
    rstd = lax.rsqrt(var + LN_EPS)
    return xc * rstd, rstd


def _ln_bwd(dxhat, xhat, rstd):
    m1 = jnp.mean(dxhat, axis=-1, keepdims=True)
    m2 = jnp.mean(dxhat * xhat, axis=-1, keepdims=True)
    return rstd * (dxhat - m1 - xhat * m2)


def _masked_sgu_w(sw_ref, hh):
    row = lax.broadcasted_iota(jnp.int32, (CHUNK, CHUNK), 0)
    col = lax.broadcasted_iota(jnp.int32, (CHUNK, CHUNK), 1)
    return jnp.where(row >= col, sw_ref[hh], 0.0)


def _inv_count(tile_index, tok_tile, window):
    tok = tile_index * tok_tile + lax.broadcasted_iota(jnp.int32, (tok_tile, 1), 0)
    return 1.0 / jnp.minimum(tok + 1, window).astype(F32)


def _params(**kw):
    return pltpu.CompilerParams(vmem_limit_bytes=VMEM_LIMIT, **kw)


def _forward_mixers(x, w_in, pool_w, pool_scale, sgu_ln_g, sgu_ln_b, sgu_w, sgu_bias_tile, tok_tile, sides=()):
    seq = x.shape[0]
    n_tiles = seq // tok_tile
    n_chunks = tok_tile // CHUNK
    side_refs = _SideRefs(sides, 8, 4, 2)

    def body(*refs):
        (x_ref, win_ref, pw_ref, ps_ref, lg_ref, lb_ref, sw_ref, sb_ref,
         hb_ref, y_ref, pooled_ref, xb_ref, aext_ref, h_ref) = side_refs.split(refs)
        i = pl.program_id(0)
        side_refs.emit(i == 0, i == n_tiles // 2, False)

        @pl.when(i == 0)
        def _():
            aext_ref[0:HALO, :] = jnp.zeros((HALO, D_POOL), F32)

        xb = x_ref[...].astype(BF16)
        xb_ref[...] = xb
        for s in range(D_IN // 1024):
            cs = slice(s * 1024, (s + 1) * 1024)
            section = _dot(xb, win_ref[:, cs])
            h_ref[:, cs] = section
            if s >= 1:
                hb_ref[:, (s - 1) * 1024:s * 1024] = section.astype(BF16)

        aext_ref[HALO:HALO + tok_tile, :] = h_ref[:, 0:D_POOL]
        for g, window in enumerate(POOL_WINDOWS):
            cols = slice(g * POOL_GROUP, (g + 1) * POOL_GROUP)
            win = aext_ref[HALO:HALO + tok_tile, cols]
            for k in range(1, window):
                win = win + aext_ref[HALO - k:HALO - k + tok_tile, cols]
            pooled = win * _inv_count(i, tok_tile, window) - h_ref[:, cols]
            pb = pooled.astype(BF16)
            pooled_ref[:, cols] = pb
            mixed = _dot(pb, pw_ref[g])
            z = h_ref[:, Z0 + g * POOL_GROUP:Z0 + (g + 1) * POOL_GROUP]
            y_ref[:, cols] = (mixed * ps_ref[:, cols] * (z * _sigmoid(z))).astype(BF16)
        aext_ref[0:HALO, :] = aext_ref[tok_tile:tok_tile + HALO, :]

        for hh in range(N_HEADS):
            cols = slice(hh * HEAD, (hh + 1) * HEAD)
            swm = _masked_sgu_w(sw_ref, hh).astype(BF16)
            for n in range(n_chunks):
                rows = slice(n * CHUNK, (n + 1) * CHUNK)
                gu = _gelu(h_ref[rows, U0 + hh * HEAD:U0 + (hh + 1) * HEAD])
                gv = _gelu(h_ref[rows, V0 + hh * HEAD:V0 + (hh + 1) * HEAD])
                xhat, _ = _row_stats(gv)
                vln = xhat * lg_ref[:, cols] + lb_ref[:, cols]
                sv = _dot(swm, vln.astype(BF16)) + sb_ref[:, cols]
                z = h_ref[rows, Z0 + D_POOL + hh * HEAD:Z0 + D_POOL + (hh + 1) * HEAD]
                y_ref[rows, D_POOL + hh * HEAD:D_POOL + (hh + 1) * HEAD] = (
                    gu * sv * (z * _sigmoid(z))).astype(BF16)

        side_refs.emit(False, False, i == n_tiles - 1)

    tok = lambda width: pl.BlockSpec((tok_tile, width), lambda i: (i, 0))
    outs = pl.pallas_call(
        body, name="forward_mixers",
        grid=(n_tiles,),
        in_specs=[tok(D_MODEL)] + [VMEM_FULL] * 7 + side_refs.in_specs,
        out_specs=[tok(D_IN - D_POOL), tok(D_MIX), tok(D_POOL), tok(D_MODEL)] + side_refs.out_specs,
        out_shape=[jax.ShapeDtypeStruct((seq, D_IN - D_POOL), BF16), jax.ShapeDtypeStruct((seq, D_MIX), BF16),
                   jax.ShapeDtypeStruct((seq, D_POOL), BF16), jax.ShapeDtypeStruct((seq, D_MODEL), BF16)]
        + side_refs.out_shapes,
        scratch_shapes=[pltpu.VMEM((HALO + tok_tile, D_POOL), F32), pltpu.VMEM((tok_tile, D_IN), F32)]
        + side_refs.scratch,
        compiler_params=_params(dimension_semantics=("arbitrary",)),
    )(x, w_in, pool_w, pool_scale, sgu_ln_g, sgu_ln_b, sgu_w, sgu_bias_tile, *side_refs.inputs)
    return outs[:4], outs[4:]


def _head_fwd_bwd(x, y, p, target, w_out, gate_w, ple_w, ln_g, ln_b, gate_b, tok_tile, sub_tile):
    seq = x.shape[0]

    def body(x_ref, y_ref, p_ref, t_ref, wout_ref, gw_ref, plw_ref, lng_ref, lnb_ref, gb_ref,
             dy_ref, dxr_ref, xn_ref, dgp_ref, dpe_ref, dr_ref, loss_ref, dlng_ref, dlnb_ref, dgb_ref):
        i = pl.program_id(0)

        @pl.when(i == 0)
        def _():
            loss_ref[...] = jnp.zeros_like(loss_ref)
            dlng_ref[...] = jnp.zeros_like(dlng_ref)
            dlnb_ref[...] = jnp.zeros_like(dlnb_ref)
            dgb_ref[...] = jnp.zeros_like(dgb_ref)

        subs = [slice(s * sub_tile, (s + 1) * sub_tile) for s in range(tok_tile // sub_tile)]
        stats, xns, douts = [], [], []
        for rows in subs:
            r = ALPHA * x_ref[rows, :] + _dot(y_ref[rows, :], wout_ref[...])
            xhat, rstd = _row_stats(r)
            xn = xhat * lng_ref[...] + lnb_ref[...]
            xn_ref[rows, :] = xn.astype(BF16)
            stats.append((xhat, rstd))
            xns.append(xn)
        loss = jnp.zeros((1, 1), F32)
        dgb = jnp.zeros((1, D_MODEL), F32)
        for rows, xn in zip(subs, xns):
            gate = _sigmoid(_dot(xn_ref[rows, :], gw_ref[...]) + gb_ref[...])
            pe = _dot(p_ref[rows, :].astype(BF16), plw_ref[...])
            err = xn + gate * pe - t_ref[rows, :]
            loss = loss + jnp.sum(err * err, keepdims=True)
            dout = err * (1.0 / D_MODEL)
            dpe_ref[rows, :] = (dout * gate).astype(BF16)
            dgpre = dout * pe * gate * (1.0 - gate)
            dgb = dgb + jnp.sum(dgpre, axis=0, keepdims=True)
            dgp_ref[rows, :] = dgpre.astype(BF16)
            douts.append(dout)
        loss_ref[...] += (0.5 / D_MODEL) * loss
        dgb_ref[...] += dgb
        dlng = jnp.zeros((1, D_MODEL), F32)
        dlnb = jnp.zeros((1, D_MODEL), F32)
        for rows, (xhat, rstd), dout in zip(subs, stats, douts):
            dxn = dout + _dot_nt(dgp_ref[rows, :], gw_ref[...])
            dlng = dlng + jnp.sum(dxn * xhat, axis=0, keepdims=True)
            dlnb = dlnb + jnp.sum(dxn, axis=0, keepdims=True)
            dr = _ln_bwd(dxn * lng_ref[...], xhat, rstd)
            dxr_ref[rows, :] = ALPHA * dr
            dr_ref[rows, :] = dr.astype(BF16)
        dlng_ref[...] += dlng
        dlnb_ref[...] += dlnb
        for rows in subs:
            dy_ref[rows, :] = _dot_nt(dr_ref[rows, :], wout_ref[...])

    tok = lambda width: pl.BlockSpec((tok_tile, width), lambda i: (i, 0))
    acc = lambda width: pl.BlockSpec((1, width), lambda i: (0, 0))
    vec = jax.ShapeDtypeStruct((1, D_MODEL), F32)
    return pl.pallas_call(
        body, name="head_fwd_bwd",
        grid=(seq // tok_tile,),
        in_specs=[tok(D_MODEL), tok(D_MIX), tok(D_PLE), tok(D_MODEL),
                  VMEM_FULL, VMEM_FULL, VMEM_FULL, VMEM_FULL, VMEM_FULL, VMEM_FULL],
        out_specs=[tok(D_MIX), tok(D_MODEL), tok(D_MODEL), tok(D_MODEL), tok(D_MODEL), tok(D_MODEL),
                   acc(128), acc(D_MODEL), acc(D_MODEL), acc(D_MODEL)],
        out_shape=[jax.ShapeDtypeStruct((seq, D_MIX), F32), jax.ShapeDtypeStruct((seq, D_MODEL), F32),
                   jax.ShapeDtypeStruct((seq, D_MODEL), BF16), jax.ShapeDtypeStruct((seq, D_MODEL), BF16),
                   jax.ShapeDtypeStruct((seq, D_MODEL), BF16), jax.ShapeDtypeStruct((seq, D_MODEL), BF16),
                   jax.ShapeDtypeStruct((1, 128), F32), vec, vec, vec],
        compiler_params=_params(dimension_semantics=("arbitrary",)),
    )(x, y, p, target, w_out, gate_w, ple_w, ln_g, ln_b, gate_b)


def _mixers_bwd(h, dy, dxr, pooled, w_in, pool_w, pool_scale, sgu_ln_g, sgu_ln_b, sgu_w, sgu_bias_tile, tok_tile):
    seq = h.shape[0]
    n_tiles = seq // tok_tile
    n_chunks = tok_tile // CHUNK
    pool_rows = POOL_GROUP // N_DEV

    def body(h_ref, dy_ref, dxr_ref, pooled_ref, win_ref, pw_ref, ps_ref, lg_ref, lb_ref, sw_ref, sb_ref, band_ref,
             dh_ref, dx_ref, dpw_ref, dps_ref, dlg_ref, dlb_ref, dsw_ref, dsb_ref,
             qhi_ref, qlo_ref, dpw_acc, dsb_acc):
        i = pl.program_id(0)
        tile = n_tiles - 1 - i

        @pl.when(i == 0)
        def _():
            qhi_ref[...] = jnp.zeros_like(qhi_ref)
            qlo_ref[...] = jnp.zeros_like(qlo_ref)
            dpw_acc[...] = jnp.zeros_like(dpw_acc)
            dsb_acc[...] = jnp.zeros_like(dsb_acc)
            dps_ref[...] = jnp.zeros_like(dps_ref)
            dlg_ref[...] = jnp.zeros_like(dlg_ref)
            dlb_ref[...] = jnp.zeros_like(dlb_ref)
            dsw_ref[...] = jnp.zeros_like(dsw_ref)

        def h_at(rows, cols):
            return h_ref[rows, cols.start - D_POOL:cols.stop - D_POOL].astype(F32)

        everything = slice(0, tok_tile)
        for part in (qhi_ref, qlo_ref):
            part[tok_tile:tok_tile + HALO, :] = part[0:HALO, :]
        for g, window in enumerate(POOL_WINDOWS):
            cols = slice(g * POOL_GROUP, (g + 1) * POOL_GROUP)
            zcols = slice(Z0 + g * POOL_GROUP, Z0 + (g + 1) * POOL_GROUP)
            z = h_at(everything, zcols)
            sz = _sigmoid(z)
            pb = pooled_ref[:, cols]
            mixed = _dot(pb, pw_ref[g])
            dyp = dy_ref[:, cols]
            dh_ref[:, zcols] = (dyp * (mixed * ps_ref[:, cols]) * (sz * (1.0 + z * (1.0 - sz)))).astype(BF16)
            dms = dyp * (z * sz)
            dps_ref[:, cols] += jnp.sum(dms * mixed, axis=0, keepdims=True)
            dmixed = (dms * ps_ref[:, cols]).astype(BF16)
            dpw_acc[g] += _dot_tn(pb, dmixed)
            dpooled = _dot_nt(dmixed, pw_ref[g])
            qhi_ref[everything, cols], qlo_ref[everything, cols] = _split_bf16(
                dpooled * _inv_count(tile, tok_tile, window))
            da = _dot(band_ref[g], qhi_ref[:, cols]) + _dot(band_ref[g], qlo_ref[:, cols]) - dpooled
            dh_ref[:, cols] = da.astype(BF16)

        pool_cols = [slice(o + g * POOL_GROUP, o + (g + 1) * POOL_GROUP)
                     for o in (0, Z0) for g in range(len(POOL_WINDOWS))]
        pool_slices = [pool_cols[0:3], pool_cols[3:6], pool_cols[6:8], []]
        for hh in range(N_HEADS):
            cols = slice(hh * HEAD, (hh + 1) * HEAD)
            ucols = slice(U0 + hh * HEAD, U0 + (hh + 1) * HEAD)
            vcols = slice(V0 + hh * HEAD, V0 + (hh + 1) * HEAD)
            zcols = slice(Z0 + D_POOL + hh * HEAD, Z0 + D_POOL + (hh + 1) * HEAD)
            sw32 = _masked_sgu_w(sw_ref, hh)
            swm = sw32.astype(BF16)
            swm_t = sw32.T.astype(BF16)
            for n in range(n_chunks):
                rows = slice(n * CHUNK, (n + 1) * CHUNK)
                gu, dgu_du = _gelu_and_grad(h_at(rows, ucols))
                gv, dgv_dv = _gelu_and_grad(h_at(rows, vcols))
                xhat, rstd = _row_stats(gv)
                vb = (xhat * lg_ref[:, cols] + lb_ref[:, cols]).astype(BF16)
                sv = _dot(swm, vb) + sb_ref[:, cols]
                z = h_at(rows, zcols)
                sz = _sigmoid(z)
                dys = dy_ref[rows, D_POOL + hh * HEAD:D_POOL + (hh + 1) * HEAD]
                dh_ref[rows, zcols] = (dys * (gu * sv) * (sz * (1.0 + z * (1.0 - sz)))).astype(BF16)
                dyg = dys * (z * sz)
                dh_ref[rows, ucols] = (dyg * sv * dgu_du).astype(BF16)
                dsv = dyg * gu
                dsb_acc[:, cols] += dsv
                dsvb = dsv.astype(BF16)
                dsw_ref[hh] += _dot_nt(dsvb, vb)
                dvln = _dot(swm_t, dsvb)
                dlg_ref[:, cols] += jnp.sum(dvln * xhat, axis=0, keepdims=True)
                dlb_ref[:, cols] += jnp.sum(dvln, axis=0, keepdims=True)
                dgv = _ln_bwd(dvln * lg_ref[:, cols], xhat, rstd)
                dh_ref[rows, vcols] = (dgv * dgv_dv).astype(BF16)
            ready = [ucols, vcols, zcols] + pool_slices[hh]
            part = _dot_nt(dh_ref[:, ready[0]], win_ref[:, ready[0]])
            for sl in ready[1:]:
                part = part + _dot_nt(dh_ref[:, sl], win_ref[:, sl])
            if hh == 0:
                dx_ref[...] = dxr_ref[...] + part
            else:
                dx_ref[...] += part

        @pl.when(i == n_tiles - 1)
        def _():
            for g in range(len(POOL_WINDOWS)):
                for b in range(N_DEV):
                    dpw_ref[b, g] = dpw_acc[g, b * pool_rows:(b + 1) * pool_rows, :]
            row = lax.broadcasted_iota(jnp.int32, (CHUNK, CHUNK), 0)
            col = lax.broadcasted_iota(jnp.int32, (CHUNK, CHUNK), 1)
            for hh in range(N_HEADS):
                dsw_ref[hh] = jnp.where(row >= col, dsw_ref[hh], 0.0)
                total = jnp.sum(dsb_acc[:, hh * HEAD:(hh + 1) * HEAD], axis=1, keepdims=True)
                dsb_ref[hh:hh + 1, :] = jnp.broadcast_to(total, (CHUNK, CHUNK)).T[0:1, :]

    tok = lambda width: pl.BlockSpec((tok_tile, width), lambda i: (n_tiles - 1 - i, 0))
    whole = lambda shape: pl.BlockSpec(shape, lambda i: (0,) * len(shape))
    vec = jax.ShapeDtypeStruct((1, D_MODEL), F32)
    return pl.pallas_call(
        body, name="mixers_bwd",
        grid=(n_tiles,),
        in_specs=[tok(D_IN - D_POOL), tok(D_MIX), tok(D_MODEL), tok(D_POOL)] + [VMEM_FULL] * 8,
        out_specs=[tok(D_IN), tok(D_MODEL), whole((N_DEV, len(POOL_WINDOWS), pool_rows, POOL_GROUP)),
                   whole((1, D_POOL)), whole((1, D_SGU)), whole((1, D_SGU)),
                   whole((N_HEADS, CHUNK, CHUNK)), whole((N_HEADS, CHUNK))],
        out_shape=[jax.ShapeDtypeStruct((seq, D_IN), BF16), jax.ShapeDtypeStruct((seq, D_MODEL), F32),
                   jax.ShapeDtypeStruct((N_DEV, len(POOL_WINDOWS), pool_rows, POOL_GROUP), F32),
                   vec, vec, vec,
                   jax.ShapeDtypeStruct((N_HEADS, CHUNK, CHUNK), F32),
                   jax.ShapeDtypeStruct((N_HEADS, CHUNK), F32)],
        scratch_shapes=[pltpu.VMEM((tok_tile + BAND_PAD, D_POOL), BF16),
                        pltpu.VMEM((tok_tile + BAND_PAD, D_POOL), BF16),
                        pltpu.VMEM((len(POOL_WINDOWS), POOL_GROUP, POOL_GROUP), F32),
                        pltpu.VMEM((CHUNK, D_SGU), F32)],
        compiler_params=_params(dimension_semantics=("arbitrary",)),
    )(h, dy, dxr, pooled, w_in, pool_w, pool_scale, sgu_ln_g, sgu_ln_b, sgu_w, sgu_bias_tile, _bands(tok_tile, True))


def _weight_grad(a, b, n_col_blocks, blocks_per_step, tok_tile, name, sides=()):
    seq, m = a.shape
    n = b.shape[1]
    nb = n // n_col_blocks
    n_steps = n_col_blocks // blocks_per_step
    n_k = seq // tok_tile
    side_refs = _SideRefs(sides, 2, 1, 0)

    def body(*refs):
        a_ref, b_ref, out_ref = side_refs.split(refs)
        step = pl.program_id(0) * n_k + pl.program_id(1)
        side_refs.emit(step == 0, step == (n_steps * n_k) // 2, False)

        @pl.when(pl.program_id(1) == 0)
        def _():
            out_ref[...] = jnp.zeros_like(out_ref)

        res = _dot_tn(a_ref[...].astype(BF16), b_ref[...])
        for blk in range(blocks_per_step):
            out_ref[blk] += res[:, blk * nb:(blk + 1) * nb]

        side_refs.emit(False, False, step == n_steps * n_k - 1)

    outs = pl.pallas_call(
        body, name=name,
        grid=(n_steps, n_k),
        in_specs=[pl.BlockSpec((tok_tile, m), lambda j, k: (k, 0)),
                  pl.BlockSpec((tok_tile, blocks_per_step * nb), lambda j, k: (k, j))] + side_refs.in_specs,
        out_specs=[pl.BlockSpec((blocks_per_step, m, nb), lambda j, k: (j, 0, 0))] + side_refs.out_specs,
        out_shape=[jax.ShapeDtypeStruct((n_col_blocks, m, nb), F32)] + side_refs.out_shapes,
        scratch_shapes=side_refs.scratch,
        compiler_params=_params(dimension_semantics=("arbitrary", "arbitrary")),
    )(a, b, *side_refs.inputs)
    return outs[0], outs[1:]


class _Side(NamedTuple):
    inputs: list
    out_shapes: list
    sem_shapes: list
    emit: Callable


def _when(cond):
    if cond is True:
        return lambda f: f()
    if cond is False:
        return lambda f: None
    return pl.when(cond)


def _place():
    return lax.axis_index("x"), lax.axis_index("y"), lax.axis_index("c")


def _other_chips(x, y):
    return [(1 - x, y), (x, 1 - y), (1 - x, 1 - y)]


def _gather_side(shards, out_shapes, views, halves):
    n = len(shards)
    n_sems = 10

    def emit(ins, outs, sems, first, mid, last):
        send_sems, recv_sems, local_sems = sems
        x, y, c = _place()
        here, x_nbr, y_nbr, diag = (x, y), (1 - x, y), (x, 1 - y), (1 - x, 1 - y)
        sibling = (x, y, 1 - c)

        def block(k, chip, core):
            return views[k](outs[k], 4 * chip[0] + 2 * chip[1] + core)

        def piece(k, ref, p):
            return ref if halves[k] is None else halves[k](ref)[p]

        def n_pieces(k):
            return 1 if halves[k] is None else 2

        def copy(k, s, src, dst, to):
            return pltpu.make_async_remote_copy(
                src_ref=src, dst_ref=dst, send_sem=send_sems.at[k, s], recv_sem=recv_sems.at[k, s],
                device_id=to, device_id_type=MESH)

        def outgoing(k, s):
            mine = block(k, here, c)
            if s == 0:
                return copy(k, 0, ins[k], mine, sibling)
            if s in (1, 2):
                return copy(k, s, piece(k, ins[k], s - 1), piece(k, mine, s - 1), (*x_nbr, c))
            if s in (3, 4):
                return copy(k, s, piece(k, ins[k], s - 3), piece(k, mine, s - 3), (*y_nbr, c))
            if s == 5:
                part = piece(k, block(k, x_nbr, c), 0)
                return copy(k, 5, part, part, (*y_nbr, c))
            if s == 6:
                part = piece(k, block(k, y_nbr, c), 1)
                return copy(k, 6, part, part, (*x_nbr, c))
            whole = block(k, (x_nbr, y_nbr, diag)[s - 7], c)
            return copy(k, s, whole, whole, sibling)

        def incoming(k, s):
            if s == 0:
                zone = block(k, here, 1 - c)
            elif s in (1, 2):
                zone = piece(k, block(k, x_nbr, c), s - 1)
            elif s in (3, 4):
                zone = piece(k, block(k, y_nbr, c), s - 3)
            elif s in (5, 6):
                zone = piece(k, block(k, diag, c), s - 5)
            else:
                zone = block(k, (x_nbr, y_nbr, diag)[s - 7], 1 - c)
            return copy(k, s, zone, zone, sibling)

        def own(k):
            return pltpu.make_async_copy(ins[k], block(k, here, c), local_sems.at[k])

        def used(k):
            return list(range(n_sems)) if n_pieces(k) == 2 else [0, 1, 3, 5, 7, 8, 9]

        @_when(first)
        def _():
            for k in range(n):
                own(k).start()
                for s in used(k):
                    if s <= 4:
                        outgoing(k, s).start()

        @_when(mid)
        def _():
            for k in range(n):
                incoming(k, 1).wait_recv()
                outgoing(k, 5).start()
                if n_pieces(k) == 2:
                    incoming(k, 4).wait_recv()
                    outgoing(k, 6).start()
                    incoming(k, 2).wait_recv()
                outgoing(k, 7).start()
                incoming(k, 3).wait_recv()
                outgoing(k, 8).start()

        @_when(last)
        def _():
            for k in range(n):
                incoming(k, 5).wait_recv()
                if n_pieces(k) == 2:
                    incoming(k, 6).wait_recv()
                outgoing(k, 9).start()
            for k in range(n):
                for s in (0, 7, 8, 9):
                    incoming(k, s).wait_recv()
            for k in range(n):
                for s in used(k):
                    outgoing(k, s).wait_send()
                own(k).wait()

    sems = [pltpu.SemaphoreType.DMA((n, n_sems)), pltpu.SemaphoreType.DMA((n, n_sems)),
            pltpu.SemaphoreType.DMA((n,))]
    return _Side(list(shards), list(out_shapes), sems, emit)


def _pair_side(grads):
    n = len(grads)

    def emit(ins, theirs, sems, first, mid, last):
        send_sems, recv_sems = sems
        x, y, c = _place()

        def copies():
            return [pltpu.make_async_remote_copy(
                src_ref=ins[k].at[2 * j + (1 - c)], dst_ref=theirs[k].at[j],
                send_sem=send_sems.at[k, j], recv_sem=recv_sems.at[k, j],
                device_id=(x, y, 1 - c), device_id_type=MESH) for k in range(n) for j in range(4)]

        @_when(first)
        def _():
            for cp in copies():
                cp.start()

        @_when(last)
        def _():
            for cp in copies():
                cp.wait_recv()
            for cp in copies():
                cp.wait_send()

    shapes = [jax.ShapeDtypeStruct((4,) + g.shape[1:], g.dtype) for g in grads]
    return _Side(list(grads), shapes, [pltpu.SemaphoreType.DMA((n, 4)), pltpu.SemaphoreType.DMA((n, 4))], emit)


def _chip_side(sums):
    n = len(sums)

    def emit(ins, others, sems, first, mid, last):
        send_sems, recv_sems = sems
        x, y, c = _place()

        def copies():
            return [pltpu.make_async_remote_copy(
                src_ref=ins[k].at[2 * px + py], dst_ref=others[k].at[r],
                send_sem=send_sems.at[k, r], recv_sem=recv_sems.at[k, r],
                device_id=(px, py, c), device_id_type=MESH)
                for k in range(n) for r, (px, py) in enumerate(_other_chips(x, y))]

        @_when(first)
        def _():
            for cp in copies():
                cp.start()

        @_when(last)
        def _():
            for cp in copies():
                cp.wait_recv()
            for cp in copies():
                cp.wait_send()

    shapes = [jax.ShapeDtypeStruct((3,) + s.shape[1:], s.dtype) for s in sums]
    return _Side(list(sums), shapes, [pltpu.SemaphoreType.DMA((n, 3)), pltpu.SemaphoreType.DMA((n, 3))], emit)


def _comm_call(side, name):
    n_in, n_out = len(side.inputs), len(side.out_shapes)

    def body(*refs):
        side.emit(refs[:n_in], refs[n_in:n_in + n_out], refs[n_in + n_out:], True, True, True)

    return pl.pallas_call(
        body, name=name, in_specs=[ANY] * n_in, out_specs=[ANY] * n_out,
        out_shape=side.out_shapes, scratch_shapes=side.sem_shapes,
    )(*side.inputs)


class _SideRefs:
    def __init__(self, sides, n_in, n_out, n_scratch):
        self.sides, self.n_in, self.n_out, self.n_scratch = sides, n_in, n_out, n_scratch
        self.inputs = [a for s in sides for a in s.inputs]
        self.out_shapes = [o for s in sides for o in s.out_shapes]
        self.scratch = [m for s in sides for m in s.sem_shapes]
        self.in_specs = [ANY] * len(self.inputs)
        self.out_specs = [ANY] * len(self.out_shapes)

    def split(self, refs):
        refs = list(refs)
        n_side_in, n_side_out = len(self.inputs), len(self.out_shapes)
        ins, rest = refs[:self.n_in], refs[self.n_in:]
        side_in, rest = rest[:n_side_in], rest[n_side_in:]
        outs, rest = rest[:self.n_out], rest[self.n_out:]
        side_out, rest = rest[:n_side_out], rest[n_side_out:]
        scratch, side_sems = rest[:self.n_scratch], rest[self.n_scratch:]
        self._refs = (side_in, side_out, side_sems)
        return ins + outs + scratch

    def emit(self, first, mid, last):
        side_in, side_out, side_sems = self._refs
        for s in self.sides:
            a, b, m = len(s.inputs), len(s.out_shapes), len(s.sem_shapes)
            s.emit(side_in[:a], side_out[:b], side_sems[:m], first, mid, last)
            side_in, side_out, side_sems = side_in[a:], side_out[b:], side_sems[m:]


ROW_RELATIONS = (0, 1, 2)


def _row_order(x, y):
    chips = _other_chips(x, y)
    return jnp.stack([2 * px + py for px, py in [chips[r] for r in ROW_RELATIONS] + [(x, y)]]).astype(jnp.int32)


def _grad_w_in_reduced(order, a, b, tok_tile, sides=()):
    seq, m = a.shape
    nb = b.shape[1] // N_DEV
    n_k = seq // tok_tile
    n_rows = 4
    last_step = n_rows * n_k - 1
    assert n_k >= 3
    fetch_at, sum_at = 1, 2
    side_refs = _SideRefs(sides, 3, 3, 8)

    def body(*refs):
        (order_ref, a_ref, b_ref, own_ref, theirs_ref, others_ref,
         acc_ref, stage_ref, sumbf_ref, pair_send, pair_recv, ici_send, ici_recv, stage_sem) = side_refs.split(refs)
        j, k = pl.program_id(0), pl.program_id(1)
        step = j * n_k + k
        side_refs.emit(step == 0, step == (n_rows * n_k) // 2, False)
        x, y, c = _place()
        chips = _other_chips(x, y)

        def to_sibling(row):
            return pltpu.make_async_remote_copy(
                src_ref=acc_ref.at[row % 2, 1 - c], dst_ref=theirs_ref.at[row],
                send_sem=pair_send.at[row], recv_sem=pair_recv.at[row],
                device_id=(x, y, 1 - c), device_id_type=MESH)

        def to_owner(row):
            rel = ROW_RELATIONS[row]
            px, py = chips[rel]
            return pltpu.make_async_remote_copy(
                src_ref=sumbf_ref.at[row], dst_ref=others_ref.at[rel],
                send_sem=ici_send.at[rel], recv_sem=ici_recv.at[rel],
                device_id=(px, py, c), device_id_type=MESH)

        def staged(row):
            return pltpu.make_async_copy(theirs_ref.at[row], stage_ref, stage_sem.at[0])

        @pl.when(k == 0)
        def _():
            acc_ref[j % 2] = jnp.zeros((2, m, nb), F32)

        res = _dot_tn(a_ref[...].astype(BF16), b_ref[...])
        for blk in range(2):
            acc_ref[j % 2, blk] += res[:, blk * nb:(blk + 1) * nb]

        for row in range(n_rows):
            @pl.when((j == row) & (k == n_k - 1))
            def _():
                to_sibling(row).start()

            if row < n_rows - 1:
                @pl.when((j == row + 1) & (k == fetch_at))
                def _():
                    to_sibling(row).wait_recv()
                    staged(row).start()

                @pl.when((j == row + 1) & (k == sum_at))
                def _():
                    staged(row).wait()
                    to_sibling(row).wait_send()
                    sumbf_ref[row] = (acc_ref[row % 2, c] + stage_ref[...]).astype(BF16)
                    to_owner(row).start()

        @pl.when(step == last_step)
        def _():
            row = n_rows - 1
            to_sibling(row).wait_recv()
            staged(row).start()
            staged(row).wait()
            to_sibling(row).wait_send()
            own_ref[...] = acc_ref[row % 2, c] + stage_ref[...]
            for r in range(n_rows - 1):
                to_owner(r).wait_recv()
                to_owner(r).wait_send()

        side_refs.emit(False, False, step == last_step)

    block = jax.ShapeDtypeStruct((m, nb), F32)
    outs = pl.pallas_call(
        body, name="grad_w_in",
        grid_spec=pltpu.PrefetchScalarGridSpec(
            num_scalar_prefetch=1, grid=(n_rows, n_k),
            in_specs=[pl.BlockSpec((tok_tile, m), lambda j, k, order_ref: (k, 0)),
                      pl.BlockSpec((tok_tile, 2 * nb), lambda j, k, order_ref: (k, order_ref[j]))]
            + side_refs.in_specs,
            out_specs=[pl.BlockSpec((m, nb), lambda j, k, order_ref: (0, 0)), ANY, ANY] + side_refs.out_specs,
            scratch_shapes=[pltpu.VMEM((2, 2, m, nb), F32), pltpu.VMEM((m, nb), F32),
                            pltpu.VMEM((n_rows - 1, m, nb), BF16),
                            pltpu.SemaphoreType.DMA((n_rows,)), pltpu.SemaphoreType.DMA((n_rows,)),
                            pltpu.SemaphoreType.DMA((n_rows - 1,)), pltpu.SemaphoreType.DMA((n_rows - 1,)),
                            pltpu.SemaphoreType.DMA((1,))] + side_refs.scratch),
        out_shape=[block, jax.ShapeDtypeStruct((n_rows, m, nb), F32),
                   jax.ShapeDtypeStruct((n_rows - 1, m, nb), BF16)] + side_refs.out_shapes,
        compiler_params=_params(dimension_semantics=("arbitrary", "arbitrary")),
    )(order, a, b, *side_refs.inputs)
    return outs[0], outs[2], outs[3:]


def _row_tile(rows, cols):
    tile = rows
    while tile * cols > 256 * 1024 and tile % 16 == 0:
        tile //= 2
    return tile


def _pair_sum(core, grads, theirs, name):
    _, rows, cols = theirs.shape
    rt = _row_tile(rows, cols)

    def body(core_ref, a_ref, b_ref, o_ref, ob_ref):
        total = a_ref[...] + b_ref[...]
        o_ref[...] = total
        ob_ref[...] = total.astype(BF16)

    spec = pl.BlockSpec((None, rt, cols), lambda j, i, core_ref: (j, i, 0))
    mine = pl.BlockSpec((None, None, rt, cols), lambda j, i, core_ref: (j, core_ref[0], i, 0))
    return pl.pallas_call(
        body, name=name,
        grid_spec=pltpu.PrefetchScalarGridSpec(
            num_scalar_prefetch=1, grid=(4, rows // rt), in_specs=[mine, spec], out_specs=[spec, spec]),
        out_shape=[jax.ShapeDtypeStruct(theirs.shape, F32), jax.ShapeDtypeStruct(theirs.shape, BF16)],
        compiler_params=_params(dimension_semantics=("arbitrary", "arbitrary")),
    )(core, grads.reshape(4, 2, rows, cols), theirs)


def _adamw(w, g, m, v):
    m = ADAM_B1 * m + (1.0 - ADAM_B1) * g
    v = ADAM_B2 * v + (1.0 - ADAM_B2) * (g * g)
    m_hat = m / (1.0 - ADAM_B1 ** ADAM_STEP)
    v_hat = v / (1.0 - ADAM_B2 ** ADAM_STEP)
    delta = -ADAM_LR * (m_hat / (jnp.sqrt(v_hat) + ADAM_EPS) + ADAM_WD * w)
    return delta, m, v


def _sum_adamw(chip, sums, others, w, m, v, name):
    _, rows, cols = sums.shape
    rt = _row_tile(rows, cols)

    def body(chip_ref, own_ref, oth_ref, w_ref, m_ref, v_ref, g_ref, d_ref, nm_ref, nv_ref):
        g = ((own_ref[...] + oth_ref[0].astype(F32)) + oth_ref[1].astype(F32)) + oth_ref[2].astype(F32)
        g_ref[...] = g
        d_ref[...], nm_ref[...], nv_ref[...] = _adamw(w_ref[...], g, m_ref[...], v_ref[...])

    spec = pl.BlockSpec((rt, cols), lambda i, chip_ref: (i, 0))
    own = pl.BlockSpec((None, rt, cols), lambda i, chip_ref: (chip_ref[0], i, 0))
    shape = jax.ShapeDtypeStruct((rows, cols), F32)
    return pl.pallas_call(
        body, name=name,
        grid_spec=pltpu.PrefetchScalarGridSpec(
            num_scalar_prefetch=1, grid=(rows // rt,),
            in_specs=[own, pl.BlockSpec((3, rt, cols), lambda i, chip_ref: (0, i, 0)), spec, spec, spec],
            out_specs=[spec] * 4),
        out_shape=[shape] * 4,
        compiler_params=_params(dimension_semantics=("arbitrary",)),
    )(chip, sums, others, w, m, v)


def _pair_sum_small(core, grads, theirs, name):
    n = len(grads)

    def body(core_ref, *refs):
        for k in range(n):
            total = refs[k][...] + refs[n + k][...]
            refs[2 * n + 2 * k][...] = total
            refs[2 * n + 2 * k + 1][...] = total.astype(BF16)

    whole = lambda shape: pl.BlockSpec(shape, lambda i, core_ref: (0,) * len(shape))
    mine = [pl.BlockSpec((4, None) + t.shape[1:], lambda i, core_ref: (0, core_ref[0], 0, 0)) for t in theirs]
    outs = pl.pallas_call(
        body, name=name,
        grid_spec=pltpu.PrefetchScalarGridSpec(
            num_scalar_prefetch=1, grid=(1,), in_specs=mine + [whole(t.shape) for t in theirs],
            out_specs=[whole(t.shape) for t in theirs for _ in range(2)]),
        out_shape=[jax.ShapeDtypeStruct(t.shape, dt) for t in theirs for dt in (F32, BF16)],
        compiler_params=_params(dimension_semantics=("arbitrary",)),
    )(core, *[g.reshape((4, 2) + g.shape[1:]) for g in grads], *theirs)
    return [(outs[2 * k], outs[2 * k + 1]) for k in range(n)]


def _sum_adamw_small(chip, items, name):
    n = len(items)

    def body(chip_ref, *refs):
        ins, outs = refs[:5 * n], refs[5 * n:]
        for k in range(n):
            own_ref, oth_ref, w_ref, m_ref, v_ref = ins[5 * k:5 * k + 5]
            g = ((own_ref[...] + oth_ref[0].astype(F32)) + oth_ref[1].astype(F32)) + oth_ref[2].astype(F32)
            outs[4 * k][...] = g
            outs[4 * k + 1][...], outs[4 * k + 2][...], outs[4 * k + 3][...] = _adamw(
                w_ref[...], g, m_ref[...], v_ref[...])

    whole = lambda shape: pl.BlockSpec(shape, lambda i, chip_ref: (0,) * len(shape))
    in_specs, operands, out_specs, out_shape = [], [], [], []
    for sums, others, w, m, v in items:
        in_specs += [pl.BlockSpec((None,) + sums.shape[1:], lambda i, chip_ref: (chip_ref[0], 0, 0)),
                     whole(others.shape), whole(w.shape), whole(m.shape), whole(v.shape)]
        operands += [sums, others, w, m, v]
        out_specs += [whole(w.shape)] * 4
        out_shape += [jax.ShapeDtypeStruct(w.shape, F32)] * 4
    outs = pl.pallas_call(
        body, name=name,
        grid_spec=pltpu.PrefetchScalarGridSpec(num_scalar_prefetch=1, grid=(1,), in_specs=in_specs,
                                               out_specs=out_specs),
        out_shape=out_shape, compiler_params=_params(dimension_semantics=("arbitrary",)),
    )(chip, *operands)
    return [tuple(outs[4 * k:4 * k + 4]) for k in range(n)]


def _small_sum_adamw(gathered, ws, ms, vs):
    n = len(ws)

    def body(*refs):
        g8 = refs[:n + 1]
        w, m, v = refs[n + 1:2 * n + 1], refs[2 * n + 1:3 * n + 1], refs[3 * n + 1:4 * n + 1]
        outs = refs[4 * n + 1:]
        g_out, d_out, m_out, v_out = outs[:n + 1], outs[n + 1:2 * n + 1], outs[2 * n + 1:3 * n + 1], outs[3 * n + 1:]
        for k in range(n + 1):
            g = g8[k][0]
            for b in range(1, N_DEV):
                g = g + g8[k][b]
            g_out[k][...] = g
            if k < n:
                d_out[k][...], m_out[k][...], v_out[k][...] = _adamw(w[k][...], g, m[k][...], v[k][...])

    shapes = [jax.ShapeDtypeStruct(w.shape, F32) for w in ws]
    loss_shape = jax.ShapeDtypeStruct(gathered[-1].shape[1:], F32)
    outs = pl.pallas_call(
        body, name="small_sum_adamw",
        in_specs=[VMEM_FULL] * (4 * n + 1), out_specs=[VMEM_FULL] * (4 * n + 1),
        out_shape=shapes + [loss_shape] + shapes * 3,
        compiler_params=_params(),
    )(*gathered, *ws, *ms, *vs)
    return outs[:n + 1], outs[n + 1:2 * n + 1], outs[2 * n + 1:3 * n + 1], outs[3 * n + 1:]


SMALL_NAMES = ("pool_scale", "sgu_ln_g", "sgu_ln_b", "sgu_w", "sgu_b", "ln_g", "ln_b", "ple_gate_b")


TOK_TILE = 256
GRAD_TILE = 1024


def _weight_views():
    cols = lambda width: (lambda ref, b: ref.at[:, pl.ds(pl.multiple_of(b * width, 128), width)])
    rows = lambda height: (lambda ref, b: ref.at[pl.ds(pl.multiple_of(b * height, 16), height), :])
    pool_rows = POOL_GROUP // N_DEV
    return {"w_in": cols(D_IN // N_DEV),
            "pool_w": lambda ref, b: ref.at[:, pl.ds(pl.multiple_of(b * pool_rows, 16), pool_rows), :],
            "w_out": rows(D_MIX // N_DEV), "ple_w": cols(D_MODEL // N_DEV), "gate_w": rows(D_MODEL // N_DEV)}


WEIGHT_SHAPES = {"w_in": (D_MODEL, D_IN), "pool_w": (len(POOL_WINDOWS), POOL_GROUP, POOL_GROUP),
                 "w_out": (D_MIX, D_MODEL), "ple_w": (D_PLE, D_MODEL), "gate_w": (D_MODEL, D_MODEL)}


def _to_bf16(arrays):
    def body(*refs):
        for src, dst in zip(refs[:len(arrays)], refs[len(arrays):]):
            dst[...] = src[...].astype(BF16)

    return pl.pallas_call(
        body, name="cast_shards", in_specs=[VMEM_FULL] * len(arrays), out_specs=[VMEM_FULL] * len(arrays),
        out_shape=[jax.ShapeDtypeStruct(a.shape, BF16) for a in arrays], compiler_params=_params(),
    )(*arrays)


def _leading_halves(shape):
    whole_tiles = len(shape) >= 3 or shape[0] % 32 == 0
    if shape[0] % 2 or not whole_tiles:
        return None
    half = shape[0] // 2
    return lambda ref: (ref.at[pl.ds(0, half)], ref.at[pl.ds(half, half)])


def _weight_gather(shards, names):
    views = _weight_views()
    return _gather_side([shards[nm] for nm in names],
                        [jax.ShapeDtypeStruct(WEIGHT_SHAPES[nm], BF16) for nm in names], [views[nm] for nm in names],
                        [_leading_halves(shards[nm].shape) for nm in names])


def kernel(x, p, w_in, pool_w, pool_scale, sgu_ln_g, sgu_ln_b, sgu_w, sgu_b, w_out, ln_g, ln_b, ple_w, ple_gate_w, ple_gate_b, loss_target, m_w_in, m_pool_w, m_pool_scale, m_sgu_ln_g, m_sgu_ln_b, m_sgu_w, m_sgu_b, m_w_out, m_ln_g, m_ln_b, m_ple_w, m_ple_gate_w, m_ple_gate_b, v_w_in, v_pool_w, v_pool_scale, v_sgu_ln_g, v_sgu_ln_b, v_sgu_w, v_sgu_b, v_w_out, v_ln_g, v_ln_b, v_ple_w, v_ple_gate_w, v_ple_gate_b):
    seq = x.shape[1]
    x2, p2, target = x[0], p[0, 0], loss_target[0]
    core = lax.axis_index("c").astype(jnp.int32).reshape(1)
    chip = (2 * lax.axis_index("x") + lax.axis_index("y")).astype(jnp.int32).reshape(1)
    pool_rows = POOL_GROUP // N_DEV

    shard_names = ("w_in", "pool_w", "w_out", "ple_w", "gate_w")
    shards = dict(zip(shard_names, _to_bf16([w_in[0], pool_w[0], w_out[0], ple_w[0], ple_gate_w[0]])))
    w_in_f, pool_w_f = _comm_call(_weight_gather(shards, ("w_in", "pool_w")), "gather_mixer_weights")
    bias_tile = jnp.repeat(sgu_b[0].T, HEAD, axis=1)
    (h, y, pooled, xb), (w_out_f, ple_w_f, gate_w_f) = _forward_mixers(
        x2, w_in_f, pool_w_f, pool_scale, sgu_ln_g, sgu_ln_b, sgu_w[0], bias_tile, 2 * TOK_TILE,
        sides=(_weight_gather(shards, ("w_out", "ple_w", "gate_w")),))

    dy, dxr, xn, dgp, dpe, dr, loss, d_ln_g, d_ln_b, d_gate_b = _head_fwd_bwd(
        x2, y, p2, target, w_out_f, gate_w_f, ple_w_f, ln_g, ln_b, ple_gate_b, 2 * TOK_TILE, TOK_TILE)

    d_ple_w, _ = _weight_grad(p2, dpe, N_DEV, N_DEV, GRAD_TILE, "grad_ple_w")
    d_w_out, (ple_theirs,) = _weight_grad(y, dr, 1, 1, GRAD_TILE, "grad_w_out", sides=(_pair_side([d_ple_w]),))
    w_out_blocks = d_w_out.reshape(N_DEV, D_MIX // N_DEV, D_MODEL)
    d_gate_w, (w_out_theirs,) = _weight_grad(xn, dgp, 1, 1, GRAD_TILE, "grad_gate_w",
                                             sides=(_pair_side([w_out_blocks]),))
    gate_blocks = d_gate_w.reshape(N_DEV, D_MODEL // N_DEV, D_MODEL)

    dh, dx, d_pool_w, d_pool_scale, d_sgu_ln_g, d_sgu_ln_b, d_sgu_w, d_sgu_b = _mixers_bwd(
        h, dy, dxr, pooled, w_in_f, pool_w_f, pool_scale, sgu_ln_g, sgu_ln_b, sgu_w[0], bias_tile, TOK_TILE)

    pool_blocks = d_pool_w.reshape(N_DEV, len(POOL_WINDOWS) * pool_rows, POOL_GROUP)
    gate_theirs, pool_theirs = _comm_call(_pair_side([gate_blocks, pool_blocks]), "pair_exchange_late")
    hosted_names = ("w_out", "ple_w", "gate_w", "pool_w")
    ple_sums, pool_sums = _pair_sum_small(core, [d_ple_w, pool_blocks], [ple_theirs, pool_theirs], "pair_sum_small")
    hosted_sums = [_pair_sum(core, w_out_blocks, w_out_theirs, "pair_sum_w_out"), ple_sums,
                   _pair_sum(core, gate_blocks, gate_theirs, "pair_sum_gate_w"), pool_sums]

    small = (d_pool_scale, d_sgu_ln_g, d_sgu_ln_b, d_sgu_w, d_sgu_b, d_ln_g, d_ln_b, d_gate_b)
    parts = small + (loss,)
    small_gather = _gather_side(parts, [jax.ShapeDtypeStruct((N_DEV,) + a.shape, F32) for a in parts],
                                [lambda ref, b: ref.at[b]] * len(parts), [_leading_halves(a.shape) for a in parts])
    cx, cy = lax.axis_index("x"), lax.axis_index("y")
    order = _row_order(cx, cy)
    w_in_own, w_in_others, side_out = _grad_w_in_reduced(
        order, xb, dh, GRAD_TILE, sides=(_chip_side([s_bf for _, s_bf in hosted_sums]), small_gather))
    hosted_others, gathered = side_out[:len(hosted_sums)], side_out[len(hosted_sums):]

    shard_of = {"w_in": (w_in, m_w_in, v_w_in), "pool_w": (pool_w, m_pool_w, v_pool_w),
                "w_out": (w_out, m_w_out, v_w_out), "ple_w": (ple_w, m_ple_w, v_ple_w),
                "gate_w": (ple_gate_w, m_ple_gate_w, v_ple_gate_w)}
    reduced = [(nm, chip, s_f32, oth) for nm, (s_f32, _), oth in zip(hosted_names, hosted_sums, hosted_others)]
    reduced.append(("w_in", jnp.zeros((1,), jnp.int32), w_in_own[None], w_in_others))
    big_out, small_items = {}, []
    for nm, which, s_f32, oth in reduced:
        w, m, v = shard_of[nm]
        two_d = s_f32.shape[1:]
        args = (s_f32, oth, w.reshape(two_d), m.reshape(two_d), v.reshape(two_d))
        if nm in ("ple_w", "pool_w", "gate_w"):
            small_items.append((nm, args))
        else:
            big_out[nm] = [r.reshape(w.shape) for r in _sum_adamw(which, *args, "adamw_" + nm)]
    for (nm, _), res in zip(small_items, _sum_adamw_small(chip, [a for _, a in small_items], "adamw_small")):
        big_out[nm] = [r.reshape(shard_of[nm][0].shape) for r in res]

    small_w = (pool_scale, sgu_ln_g, sgu_ln_b, sgu_w, sgu_b, ln_g, ln_b, ple_gate_b)
    small_m = (m_pool_scale, m_sgu_ln_g, m_sgu_ln_b, m_sgu_w, m_sgu_b, m_ln_g, m_ln_b, m_ple_gate_b)
    small_v = (v_pool_scale, v_sgu_ln_g, v_sgu_ln_b, v_sgu_w, v_sgu_b, v_ln_g, v_ln_b, v_ple_gate_b)
    natural = [[a.reshape(g.shape) for a, g in zip(group, small)] for group in (small_w, small_m, small_v)]
    res = _small_sum_adamw(list(gathered), *natural)
    g_s, d_s, m_s, v_s = [[r.reshape(w.shape) for r, w in zip(kind, small_w)] for kind in res]
    total_loss = res[0][-1][0, 0]

    order = ("w_in", "pool_w", "pool_scale", "sgu_ln_g", "sgu_ln_b", "sgu_w", "sgu_b", "w_out", "ln_g", "ln_b",
             "ple_w", "ple_gate_w", "ple_gate_b")
    outs = [total_loss, dx.reshape(1, seq, D_MODEL)]
    for kind in range(4):
        for nm in order:
            key = "gate_w" if nm == "ple_gate_w" else nm
            if key in big_out:
                outs.append(big_out[key][kind])
            else:
                outs.append((g_s, d_s, m_s, v_s)[kind][SMALL_NAMES.index(nm)])
    return tuple(outs)
```

```python
from typing import Callable, NamedTuple

import numpy as np
import jax
import jax.numpy as jnp
from jax import lax
from jax.experimental import pallas as pl
from jax.experimental.pallas import tpu as pltpu

F32 = jnp.float32
BF16 = jnp.bfloat16

N_DEV = 8
D_MODEL = 1024
D_POOL = 1024
D_SGU = 1024
D_MIX = 2048
D_IN = 5120
D_PLE = 256
POOL_WINDOWS = (2, 4, 8, 16)
POOL_GROUP = 256
N_HEADS = 4
HEAD = 256
CHUNK = 128
HALO = 16
BAND_PAD = 128
ALPHA = 2.0 ** 0.25
LN_EPS = 1e-5
ADAM_LR, ADAM_B1, ADAM_B2, ADAM_EPS, ADAM_WD, ADAM_STEP = 0.001, 0.9, 0.999, 1e-08, 0.01, 10

U0, V0, Z0 = D_POOL, D_POOL + D_SGU, D_POOL + 2 * D_SGU
VMEM_LIMIT = 56 * 1024 * 1024
MESH = pl.DeviceIdType.MESH
ANY = pl.BlockSpec(memory_space=pl.ANY)
VMEM_FULL = pl.BlockSpec(memory_space=pltpu.VMEM)

_GELU_C0 = 0.7978845608028654
_GELU_C1 = 0.044715


def _gelu_cdf(x, x2):
    return 1.0 / (1.0 + jnp.exp(x * ((-2.0 * _GELU_C0) + (-2.0 * _GELU_C0 * _GELU_C1) * x2)))


def _gelu_and_grad(x):
    x2 = x * x
    cdf = _gelu_cdf(x, x2)
    g = x * cdf
    dg = cdf + g * (1.0 - cdf) * ((2.0 * _GELU_C0) + (6.0 * _GELU_C0 * _GELU_C1) * x2)
    return g, dg


def _gelu(x):
    t = jnp.tanh(_GELU_C0 * (x + _GELU_C1 * (x * x * x)))
    return x * (0.5 * (1.0 + t))


def _split_bf16(x):
    hi = x.astype(BF16)
    return hi, (x - hi.astype(F32)).astype(BF16)


def _band(tok_tile, window, transpose):
    t = np.arange(tok_tile)[:, None]
    s = np.arange(tok_tile + BAND_PAD)[None, :]
    d = (s - t) if transpose else (t + BAND_PAD - s)
    return ((d >= 0) & (d < window)).astype(np.float32)


def _bands(tok_tile, transpose):
    return jnp.asarray(np.stack([_band(tok_tile, w, transpose) for w in POOL_WINDOWS]), dtype=BF16)


def _sigmoid(x):
    return 1.0 / (1.0 + jnp.exp(-x))


def _dot(a, b):
    return jnp.dot(a, b, preferred_element_type=F32)


def _dot_nt(a, b):
    return lax.dot_general(a, b, (((1,), (1,)), ((), ())), preferred_element_type=F32)


def _dot_tn(a, b):
    return lax.dot_general(a, b, (((0,), (0,)), ((), ())), preferred_element_type=F32)


def _row_stats(x):
    mu = jnp.mean(x, axis=-1, keepdims=True)
    xc = x - mu
    var = jnp.mean(xc * xc, axis=-1, keepdims=True)
    rstd = lax.rsqrt(var + LN_EPS)
    return xc * rstd, rstd


def _ln_bwd(dxhat, xhat, rstd):
    m1 = jnp.mean(dxhat, axis=-1, keepdims=True)
    m2 = jnp.mean(dxhat * xhat, axis=-1, keepdims=True)
    return rstd * (dxhat - m1 - xhat * m2)


def _masked_sgu_w(sw_ref, hh):
    row = lax.broadcasted_iota(jnp.int32, (CHUNK, CHUNK), 0)
    col = lax.broadcasted_iota(jnp.int32, (CHUNK, CHUNK), 1)
    return jnp.where(row >= col, sw_ref[hh], 0.0)


def _inv_count(tile_index, tok_tile, window):
    tok = tile_index * tok_tile + lax.broadcasted_iota(jnp.int32, (tok_tile, 1), 0)
    return 1.0 / jnp.minimum(tok + 1, window).astype(F32)


def _params(**kw):
    return pltpu.CompilerParams(vmem_limit_bytes=VMEM_LIMIT, **kw)


def _forward_mixers(x, w_in, pool_w, pool_scale, sgu_ln_g, sgu_ln_b, sgu_w, sgu_bias_tile, tok_tile, sides=()):
    seq = x.shape[0]
    n_tiles = seq // tok_tile
    n_chunks = tok_tile // CHUNK
    side_refs = _SideRefs(sides, 8, 4, 2)

    def body(*refs):
        (x_ref, win_ref, pw_ref, ps_ref, lg_ref, lb_ref, sw_ref, sb_ref,
         hb_ref, y_ref, pooled_ref, xb_ref, aext_ref, h_ref) = side_refs.split(refs)
        i = pl.program_id(0)
        side_refs.emit(i == 0, i == n_tiles // 2, False, late=(i == (3 * n_tiles) // 4))

        @pl.when(i == 0)
        def _():
            aext_ref[0:HALO, :] = jnp.zeros((HALO, D_POOL), F32)

        xb = x_ref[...].astype(BF16)
        xb_ref[...] = xb
        for s in range(D_IN // 1024):
            cs = slice(s * 1024, (s + 1) * 1024)
            section = _dot(xb, win_ref[:, cs])
            h_ref[:, cs] = section
            if s >= 1:
                hb_ref[:, (s - 1) * 1024:s * 1024] = section.astype(BF16)

        aext_ref[HALO:HALO + tok_tile, :] = h_ref[:, 0:D_POOL]
        for g, window in enumerate(POOL_WINDOWS):
            cols = slice(g * POOL_GROUP, (g + 1) * POOL_GROUP)
            win = aext_ref[HALO:HALO + tok_tile, cols]
            for k in range(1, window):
                win = win + aext_ref[HALO - k:HALO - k + tok_tile, cols]
            pooled = win * _inv_count(i, tok_tile, window) - h_ref[:, cols]
            pb = pooled.astype(BF16)
            pooled_ref[:, cols] = pb
            mixed = _dot(pb, pw_ref[g])
            z = h_ref[:, Z0 + g * POOL_GROUP:Z0 + (g + 1) * POOL_GROUP]
            y_ref[:, cols] = (mixed * ps_ref[:, cols] * (z * _sigmoid(z))).astype(BF16)
        aext_ref[0:HALO, :] = aext_ref[tok_tile:tok_tile + HALO, :]

        for hh in range(N_HEADS):
            cols = slice(hh * HEAD, (hh + 1) * HEAD)
            swm = _masked_sgu_w(sw_ref, hh).astype(BF16)
            for n in range(n_chunks):
                rows = slice(n * CHUNK, (n + 1) * CHUNK)
                gu = _gelu(h_ref[rows, U0 + hh * HEAD:U0 + (hh + 1) * HEAD])
                gv = _gelu(h_ref[rows, V0 + hh * HEAD:V0 + (hh + 1) * HEAD])
                xhat, _ = _row_stats(gv)
                vln = xhat * lg_ref[:, cols] + lb_ref[:, cols]
                sv = _dot(swm, vln.astype(BF16)) + sb_ref[:, cols]
                z = h_ref[rows, Z0 + D_POOL + hh * HEAD:Z0 + D_POOL + (hh + 1) * HEAD]
                y_ref[rows, D_POOL + hh * HEAD:D_POOL + (hh + 1) * HEAD] = (
                    gu * sv * (z * _sigmoid(z))).astype(BF16)

        side_refs.emit(False, False, i == n_tiles - 1, late=False)

    tok = lambda width: pl.BlockSpec((tok_tile, width), lambda i: (i, 0))
    outs = pl.pallas_call(
        body, name="forward_mixers",
        grid=(n_tiles,),
        in_specs=[tok(D_MODEL)] + [VMEM_FULL] * 7 + side_refs.in_specs,
        out_specs=[tok(D_IN - D_POOL), tok(D_MIX), tok(D_POOL), tok(D_MODEL)] + side_refs.out_specs,
        out_shape=[jax.ShapeDtypeStruct((seq, D_IN - D_POOL), BF16), jax.ShapeDtypeStruct((seq, D_MIX), BF16),
                   jax.ShapeDtypeStruct((seq, D_POOL), BF16), jax.ShapeDtypeStruct((seq, D_MODEL), BF16)]
        + side_refs.out_shapes,
        scratch_shapes=[pltpu.VMEM((HALO + tok_tile, D_POOL), F32), pltpu.VMEM((tok_tile, D_IN), F32)]
        + side_refs.scratch,
        compiler_params=_params(dimension_semantics=("arbitrary",)),
    )(x, w_in, pool_w, pool_scale, sgu_ln_g, sgu_ln_b, sgu_w, sgu_bias_tile, *side_refs.inputs)
    return outs[:4], outs[4:]


def _head_fwd_bwd(x, y, p, target, w_out, gate_w, ple_w, ln_g, ln_b, gate_b, tok_tile, sub_tile):
    seq = x.shape[0]

    def body(x_ref, y_ref, p_ref, t_ref, wout_ref, gw_ref, plw_ref, lng_ref, lnb_ref, gb_ref,
             dy_ref, dxr_ref, xn_ref, dgp_ref, dpe_ref, dr_ref, loss_ref, dlng_ref, dlnb_ref, dgb_ref):
        i = pl.program_id(0)

        @pl.when(i == 0)
        def _():
            loss_ref[...] = jnp.zeros_like(loss_ref)
            dlng_ref[...] = jnp.zeros_like(dlng_ref)
            dlnb_ref[...] = jnp.zeros_like(dlnb_ref)
            dgb_ref[...] = jnp.zeros_like(dgb_ref)

        subs = [slice(s * sub_tile, (s + 1) * sub_tile) for s in range(tok_tile // sub_tile)]
        stats, xns, douts = [], [], []
        for rows in subs:
            r = ALPHA * x_ref[rows, :] + _dot(y_ref[rows, :], wout_ref[...])
            xhat, rstd = _row_stats(r)
            xn = xhat * lng_ref[...] + lnb_ref[...]
            xn_ref[rows, :] = xn.astype(BF16)
            stats.append((xhat, rstd))
            xns.append(xn)
        loss = jnp.zeros((1, 1), F32)
        dgb = jnp.zeros((1, D_MODEL), F32)
        for rows, xn in zip(subs, xns):
            gate = _sigmoid(_dot(xn_ref[rows, :], gw_ref[...]) + gb_ref[...])
            pe = _dot(p_ref[rows, :].astype(BF16), plw_ref[...])
            err = xn + gate * pe - t_ref[rows, :]
            loss = loss + jnp.sum(err * err, keepdims=True)
            dout = err * (1.0 / D_MODEL)
            dpe_ref[rows, :] = (dout * gate).astype(BF16)
            dgpre = dout * pe * gate * (1.0 - gate)
            dgb = dgb + jnp.sum(dgpre, axis=0, keepdims=True)
            dgp_ref[rows, :] = dgpre.astype(BF16)
            douts.append(dout)
        loss_ref[...] += (0.5 / D_MODEL) * loss
        dgb_ref[...] += dgb
        dlng = jnp.zeros((1, D_MODEL), F32)
        dlnb = jnp.zeros((1, D_MODEL), F32)
        for rows, (xhat, rstd), dout in zip(subs, stats, douts):
            dxn = dout + _dot_nt(dgp_ref[rows, :], gw_ref[...])
            dlng = dlng + jnp.sum(dxn * xhat, axis=0, keepdims=True)
            dlnb = dlnb + jnp.sum(dxn, axis=0, keepdims=True)
            dr = _ln_bwd(dxn * lng_ref[...], xhat, rstd)
            dxr_ref[rows, :] = ALPHA * dr
            dr_ref[rows, :] = dr.astype(BF16)
        dlng_ref[...] += dlng
        dlnb_ref[...] += dlnb
        for rows in subs:
            dy_ref[rows, :] = _dot_nt(dr_ref[rows, :], wout_ref[...])

    tok = lambda width: pl.BlockSpec((tok_tile, width), lambda i: (i, 0))
    acc = lambda width: pl.BlockSpec((1, width), lambda i: (0, 0))
    vec = jax.ShapeDtypeStruct((1, D_MODEL), F32)
    return pl.pallas_call(
        body, name="head_fwd_bwd",
        grid=(seq // tok_tile,),
        in_specs=[tok(D_MODEL), tok(D_MIX), tok(D_PLE), tok(D_MODEL),
                  VMEM_FULL, VMEM_FULL, VMEM_FULL, VMEM_FULL, VMEM_FULL, VMEM_FULL],
        out_specs=[tok(D_MIX), tok(D_MODEL), tok(D_MODEL), tok(D_MODEL), tok(D_MODEL), tok(D_MODEL),
                   acc(128), acc(D_MODEL), acc(D_MODEL), acc(D_MODEL)],
        out_shape=[jax.ShapeDtypeStruct((seq, D_MIX), F32), jax.ShapeDtypeStruct((seq, D_MODEL), F32),
                   jax.ShapeDtypeStruct((seq, D_MODEL), BF16), jax.ShapeDtypeStruct((seq, D_MODEL), BF16),
                   jax.ShapeDtypeStruct((seq, D_MODEL), BF16), jax.ShapeDtypeStruct((seq, D_MODEL), BF16),
                   jax.ShapeDtypeStruct((1, 128), F32), vec, vec, vec],
        compiler_params=_params(dimension_semantics=("arbitrary",)),
    )(x, y, p, target, w_out, gate_w, ple_w, ln_g, ln_b, gate_b)


def _mixers_bwd(h, dy, dxr, pooled, w_in, pool_w, pool_scale, sgu_ln_g, sgu_ln_b, sgu_w, sgu_bias_tile, tok_tile):
    seq = h.shape[0]
    n_tiles = seq // tok_tile
    n_chunks = tok_tile // CHUNK
    pool_rows = POOL_GROUP // N_DEV

    def body(h_ref, dy_ref, dxr_ref, pooled_ref, win_ref, pw_ref, ps_ref, lg_ref, lb_ref, sw_ref, sb_ref, band_ref,
             dh_ref, dx_ref, dpw_ref, dps_ref, dlg_ref, dlb_ref, dsw_ref, dsb_ref,
             qhi_ref, qlo_ref, dpw_acc, dsb_acc):
        i = pl.program_id(0)
        tile = n_tiles - 1 - i

        @pl.when(i == 0)
        def _():
            qhi_ref[...] = jnp.zeros_like(qhi_ref)
            qlo_ref[...] = jnp.zeros_like(qlo_ref)
            dpw_acc[...] = jnp.zeros_like(dpw_acc)
            dsb_acc[...] = jnp.zeros_like(dsb_acc)
            dps_ref[...] = jnp.zeros_like(dps_ref)
            dlg_ref[...] = jnp.zeros_like(dlg_ref)
            dlb_ref[...] = jnp.zeros_like(dlb_ref)
            dsw_ref[...] = jnp.zeros_like(dsw_ref)

        def h_at(rows, cols):
            return h_ref[rows, cols.start - D_POOL:cols.stop - D_POOL].astype(F32)

        everything = slice(0, tok_tile)
        for part in (qhi_ref, qlo_ref):
            part[tok_tile:tok_tile + HALO, :] = part[0:HALO, :]
        for g, window in enumerate(POOL_WINDOWS):
            cols = slice(g * POOL_GROUP, (g + 1) * POOL_GROUP)
            zcols = slice(Z0 + g * POOL_GROUP, Z0 + (g + 1) * POOL_GROUP)
            z = h_at(everything, zcols)
            sz = _sigmoid(z)
            pb = pooled_ref[:, cols]
            mixed = _dot(pb, pw_ref[g])
            dyp = dy_ref[:, cols]
            dh_ref[:, zcols] = (dyp * (mixed * ps_ref[:, cols]) * (sz * (1.0 + z * (1.0 - sz)))).astype(BF16)
            dms = dyp * (z * sz)
            dps_ref[:, cols] += jnp.sum(dms * mixed, axis=0, keepdims=True)
            dmixed = (dms * ps_ref[:, cols]).astype(BF16)
            dpw_acc[g] += _dot_tn(pb, dmixed)
            dpooled = _dot_nt(dmixed, pw_ref[g])
            qhi_ref[everything, cols], qlo_ref[everything, cols] = _split_bf16(
                dpooled * _inv_count(tile, tok_tile, window))
            da = _dot(band_ref[g], qhi_ref[:, cols]) + _dot(band_ref[g], qlo_ref[:, cols]) - dpooled
            dh_ref[:, cols] = da.astype(BF16)

        pool_cols = [slice(o + g * POOL_GROUP, o + (g + 1) * POOL_GROUP)
                     for o in (0, Z0) for g in range(len(POOL_WINDOWS))]
        pool_slices = [pool_cols[0:3], pool_cols[3:6], pool_cols[6:8], []]
        for hh in range(N_HEADS):
            cols = slice(hh * HEAD, (hh + 1) * HEAD)
            ucols = slice(U0 + hh * HEAD, U0 + (hh + 1) * HEAD)
            vcols = slice(V0 + hh * HEAD, V0 + (hh + 1) * HEAD)
            zcols = slice(Z0 + D_POOL + hh * HEAD, Z0 + D_POOL + (hh + 1) * HEAD)
            sw32 = _masked_sgu_w(sw_ref, hh)
            swm = sw32.astype(BF16)
            swm_t = sw32.T.astype(BF16)
            for n in range(n_chunks):
                rows = slice(n * CHUNK, (n + 1) * CHUNK)
                gu, dgu_du = _gelu_and_grad(h_at(rows, ucols))
                gv, dgv_dv = _gelu_and_grad(h_at(rows, vcols))
                xhat, rstd = _row_stats(gv)
                vb = (xhat * lg_ref[:, cols] + lb_ref[:, cols]).astype(BF16)
                sv = _dot(swm, vb) + sb_ref[:, cols]
                z = h_at(rows, zcols)
                sz = _sigmoid(z)
                dys = dy_ref[rows, D_POOL + hh * HEAD:D_POOL + (hh + 1) * HEAD]
                dh_ref[rows, zcols] = (dys * (gu * sv) * (sz * (1.0 + z * (1.0 - sz)))).astype(BF16)
                dyg = dys * (z * sz)
                dh_ref[rows, ucols] = (dyg * sv * dgu_du).astype(BF16)
                dsv = dyg * gu
                dsb_acc[:, cols] += dsv
                dsvb = dsv.astype(BF16)
                dsw_ref[hh] += _dot_nt(dsvb, vb)
                dvln = _dot(swm_t, dsvb)
                dlg_ref[:, cols] += jnp.sum(dvln * xhat, axis=0, keepdims=True)
                dlb_ref[:, cols] += jnp.sum(dvln, axis=0, keepdims=True)
                dgv = _ln_bwd(dvln * lg_ref[:, cols], xhat, rstd)
                dh_ref[rows, vcols] = (dgv * dgv_dv).astype(BF16)
            ready = [ucols, vcols, zcols] + pool_slices[hh]
            part = _dot_nt(dh_ref[:, ready[0]], win_ref[:, ready[0]])
            for sl in ready[1:]:
                part = part + _dot_nt(dh_ref[:, sl], win_ref[:, sl])
            if hh == 0:
                dx_ref[...] = dxr_ref[...] + part
            else:
                dx_ref[...] += part

        @pl.when(i == n_tiles - 1)
        def _():
            for g in range(len(POOL_WINDOWS)):
                for b in range(N_DEV):
                    dpw_ref[b, g] = dpw_acc[g, b * pool_rows:(b + 1) * pool_rows, :]
            row = lax.broadcasted_iota(jnp.int32, (CHUNK, CHUNK), 0)
            col = lax.broadcasted_iota(jnp.int32, (CHUNK, CHUNK), 1)
            for hh in range(N_HEADS):
                dsw_ref[hh] = jnp.where(row >= col, dsw_ref[hh], 0.0)
                total = jnp.sum(dsb_acc[:, hh * HEAD:(hh + 1) * HEAD], axis=1, keepdims=True)
                dsb_ref[hh:hh + 1, :] = jnp.broadcast_to(total, (CHUNK, CHUNK)).T[0:1, :]

    tok = lambda width: pl.BlockSpec((tok_tile, width), lambda i: (n_tiles - 1 - i, 0))
    whole = lambda shape: pl.BlockSpec(shape, lambda i: (0,) * len(shape))
    vec = jax.ShapeDtypeStruct((1, D_MODEL), F32)
    return pl.pallas_call(
        body, name="mixers_bwd",
        grid=(n_tiles,),
        in_specs=[tok(D_IN - D_POOL), tok(D_MIX), tok(D_MODEL), tok(D_POOL)] + [VMEM_FULL] * 8,
        out_specs=[tok(D_IN), tok(D_MODEL), whole((N_DEV, len(POOL_WINDOWS), pool_rows, POOL_GROUP)),
                   whole((1, D_POOL)), whole((1, D_SGU)), whole((1, D_SGU)),
                   whole((N_HEADS, CHUNK, CHUNK)), whole((N_HEADS, CHUNK))],
        out_shape=[jax.ShapeDtypeStruct((seq, D_IN), BF16), jax.ShapeDtypeStruct((seq, D_MODEL), F32),
                   jax.ShapeDtypeStruct((N_DEV, len(POOL_WINDOWS), pool_rows, POOL_GROUP), F32),
                   vec, vec, vec,
                   jax.ShapeDtypeStruct((N_HEADS, CHUNK, CHUNK), F32),
                   jax.ShapeDtypeStruct((N_HEADS, CHUNK), F32)],
        scratch_shapes=[pltpu.VMEM((tok_tile + BAND_PAD, D_POOL), BF16),
                        pltpu.VMEM((tok_tile + BAND_PAD, D_POOL), BF16),
                        pltpu.VMEM((len(POOL_WINDOWS), POOL_GROUP, POOL_GROUP), F32),
                        pltpu.VMEM((CHUNK, D_SGU), F32)],
        compiler_params=_params(dimension_semantics=("arbitrary",)),
    )(h, dy, dxr, pooled, w_in, pool_w, pool_scale, sgu_ln_g, sgu_ln_b, sgu_w, sgu_bias_tile, _bands(tok_tile, True))


def _weight_grad(a, b, n_col_blocks, blocks_per_step, tok_tile, name, sides=()):
    seq, m = a.shape
    n = b.shape[1]
    nb = n // n_col_blocks
    n_steps = n_col_blocks // blocks_per_step
    n_k = seq // tok_tile
    side_refs = _SideRefs(sides, 2, 1, 0)

    def body(*refs):
        a_ref, b_ref, out_ref = side_refs.split(refs)
        step = pl.program_id(0) * n_k + pl.program_id(1)
        side_refs.emit(step == 0, step == (n_steps * n_k) // 2, False)

        @pl.when(pl.program_id(1) == 0)
        def _():
            out_ref[...] = jnp.zeros_like(out_ref)

        res = _dot_tn(a_ref[...].astype(BF16), b_ref[...])
        for blk in range(blocks_per_step):
            out_ref[blk] += res[:, blk * nb:(blk + 1) * nb]

        side_refs.emit(False, False, step == n_steps * n_k - 1)

    outs = pl.pallas_call(
        body, name=name,
        grid=(n_steps, n_k),
        in_specs=[pl.BlockSpec((tok_tile, m), lambda j, k: (k, 0)),
                  pl.BlockSpec((tok_tile, blocks_per_step * nb), lambda j, k: (k, j))] + side_refs.in_specs,
        out_specs=[pl.BlockSpec((blocks_per_step, m, nb), lambda j, k: (j, 0, 0))] + side_refs.out_specs,
        out_shape=[jax.ShapeDtypeStruct((n_col_blocks, m, nb), F32)] + side_refs.out_shapes,
        scratch_shapes=side_refs.scratch,
        compiler_params=_params(dimension_semantics=("arbitrary", "arbitrary")),
    )(a, b, *side_refs.inputs)
    return outs[0], outs[1:]


class _Side(NamedTuple):
    inputs: list
    out_shapes: list
    sem_shapes: list
    emit: Callable


def _when(cond):
    if cond is True:
        return lambda f: f()
    if cond is False:
        return lambda f: None
    return pl.when(cond)


def _place():
    return lax.axis_index("x"), lax.axis_index("y"), lax.axis_index("c")


def _other_chips(x, y):
    return [(1 - x, y), (x, 1 - y), (1 - x, 1 - y)]


def _gather_side(shards, out_shapes, views, halves):
    n = len(shards)
    n_sems = 10

    def emit(ins, outs, sems, first, mid, last, late=None):
        send_sems, recv_sems, local_sems = sems
        x, y, c = _place()
        here, x_nbr, y_nbr, diag = (x, y), (1 - x, y), (x, 1 - y), (1 - x, 1 - y)
        sibling = (x, y, 1 - c)

        def block(k, chip, core):
            return views[k](outs[k], 4 * chip[0] + 2 * chip[1] + core)

        def piece(k, ref, p):
            return ref if halves[k] is None else halves[k](ref)[p]

        def n_pieces(k):
            return 1 if halves[k] is None else 2

        def copy(k, s, src, dst, to):
            return pltpu.make_async_remote_copy(
                src_ref=src, dst_ref=dst, send_sem=send_sems.at[k, s], recv_sem=recv_sems.at[k, s],
                device_id=to, device_id_type=MESH)

        def outgoing(k, s):
            mine = block(k, here, c)
            if s == 0:
                return copy(k, 0, ins[k], mine, sibling)
            if s in (1, 2):
                return copy(k, s, piece(k, ins[k], s - 1), piece(k, mine, s - 1), (*x_nbr, c))
            if s in (3, 4):
                return copy(k, s, piece(k, ins[k], s - 3), piece(k, mine, s - 3), (*y_nbr, c))
            if s == 5:
                part = piece(k, block(k, x_nbr, c), 0)
                return copy(k, 5, part, part, (*y_nbr, c))
            if s == 6:
                part = piece(k, block(k, y_nbr, c), 1)
                return copy(k, 6, part, part, (*x_nbr, c))
            whole = block(k, (x_nbr, y_nbr, diag)[s - 7], c)
            return copy(k, s, whole, whole, sibling)

        def incoming(k, s):
            if s == 0:
                zone = block(k, here, 1 - c)
            elif s in (1, 2):
                zone = piece(k, block(k, x_nbr, c), s - 1)
            elif s in (3, 4):
                zone = piece(k, block(k, y_nbr, c), s - 3)
            elif s in (5, 6):
                zone = piece(k, block(k, diag, c), s - 5)
            else:
                zone = block(k, (x_nbr, y_nbr, diag)[s - 7], 1 - c)
            return copy(k, s, zone, zone, sibling)

        def own(k):
            return pltpu.make_async_copy(ins[k], block(k, here, c), local_sems.at[k])

        def used(k):
            return list(range(n_sems)) if n_pieces(k) == 2 else [0, 1, 3, 5, 7, 8, 9]

        @_when(first)
        def _():
            for k in range(n):
                own(k).start()
                for s in used(k):
                    if s <= 4:
                        outgoing(k, s).start()

        @_when(mid)
        def _():
            for k in range(n):
                incoming(k, 1).wait_recv()
                outgoing(k, 5).start()
                if n_pieces(k) == 2:
                    incoming(k, 4).wait_recv()
                    outgoing(k, 6).start()
                    incoming(k, 2).wait_recv()
                outgoing(k, 7).start()
                incoming(k, 3).wait_recv()
                outgoing(k, 8).start()

        @_when(last if late is None else late)
        def _():
            for k in range(n):
                incoming(k, 5).wait_recv()
                if n_pieces(k) == 2:
                    incoming(k, 6).wait_recv()
                outgoing(k, 9).start()

        @_when(last)
        def _():
            for k in range(n):
                for s in (0, 7, 8, 9):
                    incoming(k, s).wait_recv()
            for k in range(n):
                for s in used(k):
                    outgoing(k, s).wait_send()
                own(k).wait()

    sems = [pltpu.SemaphoreType.DMA((n, n_sems)), pltpu.SemaphoreType.DMA((n, n_sems)),
            pltpu.SemaphoreType.DMA((n,))]
    return _Side(list(shards), list(out_shapes), sems, emit)


def _pair_side(grads):
    n = len(grads)

    def emit(ins, theirs, sems, first, mid, last, late=None):
        send_sems, recv_sems = sems
        x, y, c = _place()

        def copies():
            return [pltpu.make_async_remote_copy(
                src_ref=ins[k].at[2 * j + (1 - c)], dst_ref=theirs[k].at[j],
                send_sem=send_sems.at[k, j], recv_sem=recv_sems.at[k, j],
                device_id=(x, y, 1 - c), device_id_type=MESH) for k in range(n) for j in range(4)]

        @_when(first)
        def _():
            for cp in copies():
                cp.start()

        @_when(last)
        def _():
            for cp in copies():
                cp.wait_recv()
            for cp in copies():
                cp.wait_send()

    shapes = [jax.ShapeDtypeStruct((4,) + g.shape[1:], g.dtype) for g in grads]
    return _Side(list(grads), shapes, [pltpu.SemaphoreType.DMA((n, 4)), pltpu.SemaphoreType.DMA((n, 4))], emit)


def _chip_side(sums):
    n = len(sums)

    def emit(ins, others, sems, first, mid, last, late=None):
        send_sems, recv_sems = sems
        x, y, c = _place()

        def copies():
            return [pltpu.make_async_remote_copy(
                src_ref=ins[k].at[2 * px + py], dst_ref=others[k].at[r],
                send_sem=send_sems.at[k, r], recv_sem=recv_sems.at[k, r],
                device_id=(px, py, c), device_id_type=MESH)
                for k in range(n) for r, (px, py) in enumerate(_other_chips(x, y))]

        @_when(first)
        def _():
            for cp in copies():
                cp.start()

        @_when(last)
        def _():
            for cp in copies():
                cp.wait_recv()
            for cp in copies():
                cp.wait_send()

    shapes = [jax.ShapeDtypeStruct((3,) + s.shape[1:], s.dtype) for s in sums]
    return _Side(list(sums), shapes, [pltpu.SemaphoreType.DMA((n, 3)), pltpu.SemaphoreType.DMA((n, 3))], emit)


def _comm_call(side, name):
    n_in, n_out = len(side.inputs), len(side.out_shapes)

    def body(*refs):
        side.emit(refs[:n_in], refs[n_in:n_in + n_out], refs[n_in + n_out:], True, True, True)

    return pl.pallas_call(
        body, name=name, in_specs=[ANY] * n_in, out_specs=[ANY] * n_out,
        out_shape=side.out_shapes, scratch_shapes=side.sem_shapes,
    )(*side.inputs)


class _SideRefs:
    def __init__(self, sides, n_in, n_out, n_scratch):
        self.sides, self.n_in, self.n_out, self.n_scratch = sides, n_in, n_out, n_scratch
        self.inputs = [a for s in sides for a in s.inputs]
        self.out_shapes = [o for s in sides for o in s.out_shapes]
        self.scratch = [m for s in sides for m in s.sem_shapes]
        self.in_specs = [ANY] * len(self.inputs)
        self.out_specs = [ANY] * len(self.out_shapes)

    def split(self, refs):
        refs = list(refs)
        n_side_in, n_side_out = len(self.inputs), len(self.out_shapes)
        ins, rest = refs[:self.n_in], refs[self.n_in:]
        side_in, rest = rest[:n_side_in], rest[n_side_in:]
        outs, rest = rest[:self.n_out], rest[self.n_out:]
        side_out, rest = rest[:n_side_out], rest[n_side_out:]
        scratch, side_sems = rest[:self.n_scratch], rest[self.n_scratch:]
        self._refs = (side_in, side_out, side_sems)
        return ins + outs + scratch

    def emit(self, first, mid, last, late=None):
        side_in, side_out, side_sems = self._refs
        for s in self.sides:
            a, b, m = len(s.inputs), len(s.out_shapes), len(s.sem_shapes)
            s.emit(side_in[:a], side_out[:b], side_sems[:m], first, mid, last, late)
            side_in, side_out, side_sems = side_in[a:], side_out[b:], side_sems[m:]


ROW_RELATIONS = (0, 1, 2)


def _row_order(x, y):
    chips = _other_chips(x, y)
    return jnp.stack([2 * px + py for px, py in [chips[r] for r in ROW_RELATIONS] + [(x, y)]]).astype(jnp.int32)


def _grad_w_in_reduced(order, a, b, tok_tile, sides=()):
    seq, m = a.shape
    nb = b.shape[1] // N_DEV
    n_k = seq // tok_tile
    n_rows = 4
    last_step = n_rows * n_k - 1
    assert n_k >= 3
    fetch_at, sum_at = 1, 2
    side_refs = _SideRefs(sides, 3, 3, 8)

    def body(*refs):
        (order_ref, a_ref, b_ref, own_ref, theirs_ref, others_ref,
         acc_ref, stage_ref, sumbf_ref, pair_send, pair_recv, ici_send, ici_recv, stage_sem) = side_refs.split(refs)
        j, k = pl.program_id(0), pl.program_id(1)
        step = j * n_k + k
        side_refs.emit(step == 0, step == (n_rows * n_k) // 2, False, late=(step == (3 * n_rows * n_k) // 4))
        x, y, c = _place()
        chips = _other_chips(x, y)

        def to_sibling(row):
            return pltpu.make_async_remote_copy(
                src_ref=acc_ref.at[row % 2, 1 - c], dst_ref=theirs_ref.at[row],
                send_sem=pair_send.at[row], recv_sem=pair_recv.at[row],
                device_id=(x, y, 1 - c), device_id_type=MESH)

        def to_owner(row):
            rel = ROW_RELATIONS[row]
            px, py = chips[rel]
            return pltpu.make_async_remote_copy(
                src_ref=sumbf_ref.at[row], dst_ref=others_ref.at[rel],
                send_sem=ici_send.at[rel], recv_sem=ici_recv.at[rel],
                device_id=(px, py, c), device_id_type=MESH)

        def staged(row):
            return pltpu.make_async_copy(theirs_ref.at[row], stage_ref, stage_sem.at[0])

        @pl.when(k == 0)
        def _():
            acc_ref[j % 2] = jnp.zeros((2, m, nb), F32)

        res = _dot_tn(a_ref[...].astype(BF16), b_ref[...])
        for blk in range(2):
            acc_ref[j % 2, blk] += res[:, blk * nb:(blk + 1) * nb]

        for row in range(n_rows):
            @pl.when((j == row) & (k == n_k - 1))
            def _():
                to_sibling(row).start()

            if row < n_rows - 1:
                @pl.when((j == row + 1) & (k == fetch_at))
                def _():
                    to_sibling(row).wait_recv()
                    staged(row).start()

                @pl.when((j == row + 1) & (k == sum_at))
                def _():
                    staged(row).wait()
                    to_sibling(row).wait_send()
                    sumbf_ref[row] = (acc_ref[row % 2, c] + stage_ref[...]).astype(BF16)
                    to_owner(row).start()

        @pl.when(step == last_step)
        def _():
            row = n_rows - 1
            to_sibling(row).wait_recv()
            staged(row).start()
            staged(row).wait()
            to_sibling(row).wait_send()
            own_ref[...] = acc_ref[row % 2, c] + stage_ref[...]
            for r in range(n_rows - 1):
                to_owner(r).wait_recv()
                to_owner(r).wait_send()

        side_refs.emit(False, False, step == last_step, late=False)

    block = jax.ShapeDtypeStruct((m, nb), F32)
    outs = pl.pallas_call(
        body, name="grad_w_in",
        grid_spec=pltpu.PrefetchScalarGridSpec(
            num_scalar_prefetch=1, grid=(n_rows, n_k),
            in_specs=[pl.BlockSpec((tok_tile, m), lambda j, k, order_ref: (k, 0)),
                      pl.BlockSpec((tok_tile, 2 * nb), lambda j, k, order_ref: (k, order_ref[j]))]
            + side_refs.in_specs,
            out_specs=[pl.BlockSpec((m, nb), lambda j, k, order_ref: (0, 0)), ANY, ANY] + side_refs.out_specs,
            scratch_shapes=[pltpu.VMEM((2, 2, m, nb), F32), pltpu.VMEM((m, nb), F32),
                            pltpu.VMEM((n_rows - 1, m, nb), BF16),
                            pltpu.SemaphoreType.DMA((n_rows,)), pltpu.SemaphoreType.DMA((n_rows,)),
                            pltpu.SemaphoreType.DMA((n_rows - 1,)), pltpu.SemaphoreType.DMA((n_rows - 1,)),
                            pltpu.SemaphoreType.DMA((1,))] + side_refs.scratch),
        out_shape=[block, jax.ShapeDtypeStruct((n_rows, m, nb), F32),
                   jax.ShapeDtypeStruct((n_rows - 1, m, nb), BF16)] + side_refs.out_shapes,
        compiler_params=_params(dimension_semantics=("arbitrary", "arbitrary")),
    )(order, a, b, *side_refs.inputs)
    return outs[0], outs[2], outs[3:]


def _row_tile(rows, cols):
    tile = rows
    while tile * cols > 256 * 1024 and tile % 16 == 0:
        tile //= 2
    return tile


def _pair_sum(core, grads, theirs, name):
    _, rows, cols = theirs.shape
    rt = _row_tile(rows, cols)

    def body(core_ref, a_ref, b_ref, o_ref, ob_ref):
        total = a_ref[...] + b_ref[...]
        o_ref[...] = total
        ob_ref[...] = total.astype(BF16)

    spec = pl.BlockSpec((None, rt, cols), lambda j, i, core_ref: (j, i, 0))
    mine = pl.BlockSpec((None, None, rt, cols), lambda j, i, core_ref: (j, core_ref[0], i, 0))
    return pl.pallas_call(
        body, name=name,
        grid_spec=pltpu.PrefetchScalarGridSpec(
            num_scalar_prefetch=1, grid=(4, rows // rt), in_specs=[mine, spec], out_specs=[spec, spec]),
        out_shape=[jax.ShapeDtypeStruct(theirs.shape, F32), jax.ShapeDtypeStruct(theirs.shape, BF16)],
        compiler_params=_params(dimension_semantics=("arbitrary", "arbitrary")),
    )(core, grads.reshape(4, 2, rows, cols), theirs)


def _adamw(w, g, m, v):
    m = ADAM_B1 * m + (1.0 - ADAM_B1) * g
    v = ADAM_B2 * v + (1.0 - ADAM_B2) * (g * g)
    m_hat = m / (1.0 - ADAM_B1 ** ADAM_STEP)
    v_hat = v / (1.0 - ADAM_B2 ** ADAM_STEP)
    delta = -ADAM_LR * (m_hat / (jnp.sqrt(v_hat) + ADAM_EPS) + ADAM_WD * w)
    return delta, m, v


def _sum_adamw(chip, sums, others, w, m, v, name):
    _, rows, cols = sums.shape
    rt = _row_tile(rows, cols)

    def body(chip_ref, own_ref, oth_ref, w_ref, m_ref, v_ref, g_ref, d_ref, nm_ref, nv_ref):
        g = ((own_ref[...] + oth_ref[0].astype(F32)) + oth_ref[1].astype(F32)) + oth_ref[2].astype(F32)
        g_ref[...] = g
        d_ref[...], nm_ref[...], nv_ref[...] = _adamw(w_ref[...], g, m_ref[...], v_ref[...])

    spec = pl.BlockSpec((rt, cols), lambda i, chip_ref: (i, 0))
    own = pl.BlockSpec((None, rt, cols), lambda i, chip_ref: (chip_ref[0], i, 0))
    shape = jax.ShapeDtypeStruct((rows, cols), F32)
    return pl.pallas_call(
        body, name=name,
        grid_spec=pltpu.PrefetchScalarGridSpec(
            num_scalar_prefetch=1, grid=(rows // rt,),
            in_specs=[own, pl.BlockSpec((3, rt, cols), lambda i, chip_ref: (0, i, 0)), spec, spec, spec],
            out_specs=[spec] * 4),
        out_shape=[shape] * 4,
        compiler_params=_params(dimension_semantics=("arbitrary",)),
    )(chip, sums, others, w, m, v)


def _pair_sum_small(core, grads, theirs, name):
    n = len(grads)

    def body(core_ref, *refs):
        for k in range(n):
            total = refs[k][...] + refs[n + k][...]
            refs[2 * n + 2 * k][...] = total
            refs[2 * n + 2 * k + 1][...] = total.astype(BF16)

    whole = lambda shape: pl.BlockSpec(shape, lambda i, core_ref: (0,) * len(shape))
    mine = [pl.BlockSpec((4, None) + t.shape[1:], lambda i, core_ref: (0, core_ref[0], 0, 0)) for t in theirs]
    outs = pl.pallas_call(
        body, name=name,
        grid_spec=pltpu.PrefetchScalarGridSpec(
            num_scalar_prefetch=1, grid=(1,), in_specs=mine + [whole(t.shape) for t in theirs],
            out_specs=[whole(t.shape) for t in theirs for _ in range(2)]),
        out_shape=[jax.ShapeDtypeStruct(t.shape, dt) for t in theirs for dt in (F32, BF16)],
        compiler_params=_params(dimension_semantics=("arbitrary",)),
    )(core, *[g.reshape((4, 2) + g.shape[1:]) for g in grads], *theirs)
    return [(outs[2 * k], outs[2 * k + 1]) for k in range(n)]


def _sum_adamw_small(chip, items, name):
    n = len(items)

    def body(chip_ref, *refs):
        ins, outs = refs[:5 * n], refs[5 * n:]
        for k in range(n):
            own_ref, oth_ref, w_ref, m_ref, v_ref = ins[5 * k:5 * k + 5]
            g = ((own_ref[...] + oth_ref[0].astype(F32)) + oth_ref[1].astype(F32)) + oth_ref[2].astype(F32)
            outs[4 * k][...] = g
            outs[4 * k + 1][...], outs[4 * k + 2][...], outs[4 * k + 3][...] = _adamw(
                w_ref[...], g, m_ref[...], v_ref[...])

    whole = lambda shape: pl.BlockSpec(shape, lambda i, chip_ref: (0,) * len(shape))
    in_specs, operands, out_specs, out_shape = [], [], [], []
    for sums, others, w, m, v in items:
        in_specs += [pl.BlockSpec((None,) + sums.shape[1:], lambda i, chip_ref: (chip_ref[0], 0, 0)),
                     whole(others.shape), whole(w.shape), whole(m.shape), whole(v.shape)]
        operands += [sums, others, w, m, v]
        out_specs += [whole(w.shape)] * 4
        out_shape += [jax.ShapeDtypeStruct(w.shape, F32)] * 4
    outs = pl.pallas_call(
        body, name=name,
        grid_spec=pltpu.PrefetchScalarGridSpec(num_scalar_prefetch=1, grid=(1,), in_specs=in_specs,
                                               out_specs=out_specs),
        out_shape=out_shape, compiler_params=_params(dimension_semantics=("arbitrary",)),
    )(chip, *operands)
    return [tuple(outs[4 * k:4 * k + 4]) for k in range(n)]


def _small_sum_adamw(gathered, ws, ms, vs):
    n = len(ws)

    def body(*refs):
        g8 = refs[:n + 1]
        w, m, v = refs[n + 1:2 * n + 1], refs[2 * n + 1:3 * n + 1], refs[3 * n + 1:4 * n + 1]
        outs = refs[4 * n + 1:]
        g_out, d_out, m_out, v_out = outs[:n + 1], outs[n + 1:2 * n + 1], outs[2 * n + 1:3 * n + 1], outs[3 * n + 1:]
        for k in range(n + 1):
            g = g8[k][0]
            for b in range(1, N_DEV):
                g = g + g8[k][b]
            g_out[k][...] = g
            if k < n:
                d_out[k][...], m_out[k][...], v_out[k][...] = _adamw(w[k][...], g, m[k][...], v[k][...])

    shapes = [jax.ShapeDtypeStruct(w.shape, F32) for w in ws]
    loss_shape = jax.ShapeDtypeStruct(gathered[-1].shape[1:], F32)
    outs = pl.pallas_call(
        body, name="small_sum_adamw",
        in_specs=[VMEM_FULL] * (4 * n + 1), out_specs=[VMEM_FULL] * (4 * n + 1),
        out_shape=shapes + [loss_shape] + shapes * 3,
        compiler_params=_params(),
    )(*gathered, *ws, *ms, *vs)
    return outs[:n + 1], outs[n + 1:2 * n + 1], outs[2 * n + 1:3 * n + 1], outs[3 * n + 1:]


SMALL_NAMES = ("pool_scale", "sgu_ln_g", "sgu_ln_b", "sgu_w", "sgu_b", "ln_g", "ln_b", "ple_gate_b")


TOK_TILE = 256
GRAD_TILE = 1024


def _weight_views():
    cols = lambda width: (lambda ref, b: ref.at[:, pl.ds(pl.multiple_of(b * width, 128), width)])
    rows = lambda height: (lambda ref, b: ref.at[pl.ds(pl.multiple_of(b * height, 16), height), :])
    pool_rows = POOL_GROUP // N_DEV
    return {"w_in": cols(D_IN // N_DEV),
            "pool_w": lambda ref, b: ref.at[:, pl.ds(pl.multiple_of(b * pool_rows, 16), pool_rows), :],
            "w_out": rows(D_MIX // N_DEV), "ple_w": cols(D_MODEL // N_DEV), "gate_w": rows(D_MODEL // N_DEV)}


WEIGHT_SHAPES = {"w_in": (D_MODEL, D_IN), "pool_w": (len(POOL_WINDOWS), POOL_GROUP, POOL_GROUP),
                 "w_out": (D_MIX, D_MODEL), "ple_w": (D_PLE, D_MODEL), "gate_w": (D_MODEL, D_MODEL)}


def _to_bf16(arrays):
    def body(*refs):
        for src, dst in zip(refs[:len(arrays)], refs[len(arrays):]):
            dst[...] = src[...].astype(BF16)

    return pl.pallas_call(
        body, name="cast_shards", in_specs=[VMEM_FULL] * len(arrays), out_specs=[VMEM_FULL] * len(arrays),
        out_shape=[jax.ShapeDtypeStruct(a.shape, BF16) for a in arrays], compiler_params=_params(),
    )(*arrays)


def _leading_halves(shape):
    whole_tiles = len(shape) >= 3 or shape[0] % 32 == 0
    if shape[0] % 2 or not whole_tiles:
        return None
    half = shape[0] // 2
    return lambda ref: (ref.at[pl.ds(0, half)], ref.at[pl.ds(half, half)])


def _weight_gather(shards, names):
    views = _weight_views()
    return _gather_side([shards[nm] for nm in names],
                        [jax.ShapeDtypeStruct(WEIGHT_SHAPES[nm], BF16) for nm in names], [views[nm] for nm in names],
                        [_leading_halves(shards[nm].shape) for nm in names])


def kernel(x, p, w_in, pool_w, pool_scale, sgu_ln_g, sgu_ln_b, sgu_w, sgu_b, w_out, ln_g, ln_b, ple_w, ple_gate_w, ple_gate_b, loss_target, m_w_in, m_pool_w, m_pool_scale, m_sgu_ln_g, m_sgu_ln_b, m_sgu_w, m_sgu_b, m_w_out, m_ln_g, m_ln_b, m_ple_w, m_ple_gate_w, m_ple_gate_b, v_w_in, v_pool_w, v_pool_scale, v_sgu_ln_g, v_sgu_ln_b, v_sgu_w, v_sgu_b, v_w_out, v_ln_g, v_ln_b, v_ple_w, v_ple_gate_w, v_ple_gate_b):
    seq = x.shape[1]
    x2, p2, target = x[0], p[0, 0], loss_target[0]
    core = lax.axis_index("c").astype(jnp.int32).reshape(1)
    chip = (2 * lax.axis_index("x") + lax.axis_index("y")).astype(jnp.int32).reshape(1)
    pool_rows = POOL_GROUP // N_DEV

    shard_names = ("w_in", "pool_w", "w_out", "ple_w", "gate_w")
    shards = dict(zip(shard_names, _to_bf16([w_in[0], pool_w[0], w_out[0], ple_w[0], ple_gate_w[0]])))
    w_in_f, pool_w_f = _comm_call(_weight_gather(shards, ("w_in", "pool_w")), "gather_mixer_weights")
    bias_tile = jnp.repeat(sgu_b[0].T, HEAD, axis=1)
    (h, y, pooled, xb), (w_out_f, ple_w_f, gate_w_f) = _forward_mixers(
        x2, w_in_f, pool_w_f, pool_scale, sgu_ln_g, sgu_ln_b, sgu_w[0], bias_tile, 2 * TOK_TILE,
        sides=(_weight_gather(shards, ("w_out", "ple_w", "gate_w")),))

    dy, dxr, xn, dgp, dpe, dr, loss, d_ln_g, d_ln_b, d_gate_b = _head_fwd_bwd(
        x2, y, p2, target, w_out_f, gate_w_f, ple_w_f, ln_g, ln_b, ple_gate_b, 2 * TOK_TILE, TOK_TILE)

    d_ple_w, _ = _weight_grad(p2, dpe, N_DEV, N_DEV, GRAD_TILE, "grad_ple_w")
    d_w_out, (ple_theirs,) = _weight_grad(y, dr, 1, 1, GRAD_TILE, "grad_w_out", sides=(_pair_side([d_ple_w]),))
    w_out_blocks = d_w_out.reshape(N_DEV, D_MIX // N_DEV, D_MODEL)
    d_gate_w, (w_out_theirs,) = _weight_grad(xn, dgp, 1, 1, GRAD_TILE, "grad_gate_w",
                                             sides=(_pair_side([w_out_blocks]),))
    gate_blocks = d_gate_w.reshape(N_DEV, D_MODEL // N_DEV, D_MODEL)

    dh, dx, d_pool_w, d_pool_scale, d_sgu_ln_g, d_sgu_ln_b, d_sgu_w, d_sgu_b = _mixers_bwd(
        h, dy, dxr, pooled, w_in_f, pool_w_f, pool_scale, sgu_ln_g, sgu_ln_b, sgu_w[0], bias_tile, TOK_TILE)

    pool_blocks = d_pool_w.reshape(N_DEV, len(POOL_WINDOWS) * pool_rows, POOL_GROUP)
    gate_theirs, pool_theirs = _comm_call(_pair_side([gate_blocks, pool_blocks]), "pair_exchange_late")
    hosted_names = ("w_out", "ple_w", "gate_w", "pool_w")
    ple_sums, gate_sums, pool_sums = _pair_sum_small(
        core, [d_ple_w, gate_blocks, pool_blocks], [ple_theirs, gate_theirs, pool_theirs], "pair_sum_small")
    hosted_sums = [_pair_sum(core, w_out_blocks, w_out_theirs, "pair_sum_w_out"), ple_sums, gate_sums, pool_sums]

    small = (d_pool_scale, d_sgu_ln_g, d_sgu_ln_b, d_sgu_w, d_sgu_b, d_ln_g, d_ln_b, d_gate_b)
    parts = small + (loss,)
    small_gather = _gather_side(parts, [jax.ShapeDtypeStruct((N_DEV,) + a.shape, F32) for a in parts],
                                [lambda ref, b: ref.at[b]] * len(parts), [_leading_halves(a.shape) for a in parts])
    cx, cy = lax.axis_index("x"), lax.axis_index("y")
    order = _row_order(cx, cy)
    w_in_own, w_in_others, side_out = _grad_w_in_reduced(
        order, xb, dh, GRAD_TILE, sides=(_chip_side([s_bf for _, s_bf in hosted_sums]), small_gather))
    hosted_others, gathered = side_out[:len(hosted_sums)], side_out[len(hosted_sums):]

    shard_of = {"w_in": (w_in, m_w_in, v_w_in), "pool_w": (pool_w, m_pool_w, v_pool_w),
                "w_out": (w_out, m_w_out, v_w_out), "ple_w": (ple_w, m_ple_w, v_ple_w),
                "gate_w": (ple_gate_w, m_ple_gate_w, v_ple_gate_w)}
    reduced = [(nm, chip, s_f32, oth) for nm, (s_f32, _), oth in zip(hosted_names, hosted_sums, hosted_others)]
    reduced.append(("w_in", jnp.zeros((1,), jnp.int32), w_in_own[None], w_in_others))
    big_out, small_items = {}, []
    for nm, which, s_f32, oth in reduced:
        w, m, v = shard_of[nm]
        two_d = s_f32.shape[1:]
        args = (s_f32, oth, w.reshape(two_d), m.reshape(two_d), v.reshape(two_d))
        if nm in ("ple_w", "pool_w", "gate_w"):
            small_items.append((nm, args))
        else:
            big_out[nm] = [r.reshape(w.shape) for r in _sum_adamw(which, *args, "adamw_" + nm)]
    for (nm, _), res in zip(small_items, _sum_adamw_small(chip, [a for _, a in small_items], "adamw_small")):
        big_out[nm] = [r.reshape(shard_of[nm][0].shape) for r in res]

    small_w = (pool_scale, sgu_ln_g, sgu_ln_b, sgu_w, sgu_b, ln_g, ln_b, ple_gate_b)
    small_m = (m_pool_scale, m_sgu_ln_g, m_sgu_ln_b, m_sgu_w, m_sgu_b, m_ln_g, m_ln_b, m_ple_gate_b)
    small_v = (v_pool_scale, v_sgu_ln_g, v_sgu_ln_b, v_sgu_w, v_sgu_b, v_ln_g, v_ln_b, v_ple_gate_b)
    natural = [[a.reshape(g.shape) for a, g in zip(group, small)] for group in (small_w, small_m, small_v)]
    res = _small_sum_adamw(list(gathered), *natural)
    g_s, d_s, m_s, v_s = [[r.reshape(w.shape) for r, w in zip(kind, small_w)] for kind in res]
    total_loss = res[0][-1][0, 0]

    order = ("w_in", "pool_w", "pool_scale", "sgu_ln_g", "sgu_ln_b", "sgu_w", "sgu_b", "w_out", "ln_g", "ln_b",
             "ple_w", "ple_gate_w", "ple_gate_b")
    outs = [total_loss, dx.reshape(1, seq, D_MODEL)]
    for kind in range(4):
        for nm in order:
            key = "gate_w" if nm == "ple_gate_w" else nm
            if key in big_out:
                outs.append(big_out[key][kind])
            else:
                outs.append((g_s, d_s, m_s, v_s)[kind][SMALL_NAMES.index(nm)])
    return tuple(outs)
```

```python
from typing import Callable, NamedTuple

import numpy as np
import jax
import jax.numpy as jnp
from jax import lax
from jax.experimental import pallas as pl
from jax.experimental.pallas import tpu as pltpu

F32 = jnp.float32
BF16 = jnp.bfloat16

N_DEV = 8
D_MODEL = 1024
D_POOL = 1024
D_SGU = 1024
D_MIX = 2048
D_IN = 5120
D_PLE = 256
POOL_WINDOWS = (2, 4, 8, 16)
POOL_GROUP = 256
N_HEADS = 4
HEAD = 256
CHUNK = 128
HALO = 16
BAND_PAD = 128
ALPHA = 2.0 ** 0.25
LN_EPS = 1e-5
ADAM_LR, ADAM_B1, ADAM_B2, ADAM_EPS, ADAM_WD, ADAM_STEP = 0.001, 0.9, 0.999, 1e-08, 0.01, 10

U0, V0, Z0 = D_POOL, D_POOL + D_SGU, D_POOL + 2 * D_SGU
VMEM_LIMIT = 56 * 1024 * 1024
MESH = pl.DeviceIdType.MESH
ANY = pl.BlockSpec(memory_space=pl.ANY)
VMEM_FULL = pl.BlockSpec(memory_space=pltpu.VMEM)

_GELU_C0 = 0.7978845608028654
_GELU_C1 = 0.044715


def _gelu_cdf(x, x2):
    return 1.0 / (1.0 + jnp.exp(x * ((-2.0 * _GELU_C0) + (-2.0 * _GELU_C0 * _GELU_C1) * x2)))


def _gelu_and_grad(x):
    x2 = x * x
    cdf = _gelu_cdf(x, x2)
    g = x * cdf
    dg = cdf + g * (1.0 - cdf) * ((2.0 * _GELU_C0) + (6.0 * _GELU_C0 * _GELU_C1) * x2)
    return g, dg


def _gelu(x):
    t = jnp.tanh(_GELU_C0 * (x + _GELU_C1 * (x * x * x)))
    return x * (0.5 * (1.0 + t))


def _split_bf16(x):
    hi = x.astype(BF16)
    return hi, (x - hi.astype(F32)).astype(BF16)


def _band(tok_tile, window, transpose):
    t = np.arange(tok_tile)[:, None]
    s = np.arange(tok_tile + BAND_PAD)[None, :]
    d = (s - t) if transpose else (t + BAND_PAD - s)
    return ((d >= 0) & (d < window)).astype(np.float32)


def _bands(tok_tile, transpose):
    return jnp.asarray(np.stack([_band(tok_tile, w, transpose) for w in POOL_WINDOWS]), dtype=BF16)


def _sigmoid(x):
    return 1.0 / (1.0 + jnp.exp(-x))


def _dot(a, b):
    return jnp.dot(a, b, preferred_element_type=F32)


def _dot_nt(a, b):
    return lax.dot_general(a, b, (((1,), (1,)), ((), ())), preferred_element_type=F32)


def _dot_tn(a, b):
    return lax.dot_general(a, b, (((0,), (0,)), ((), ())), preferred_element_type=F32)


def _row_stats(x):
    mu = jnp.mean(x, axis=-1, keepdims=True)
    xc = x - mu
    var = jnp.mean(xc * xc, axis=-1, keepdims=True)
    rstd = lax.rsqrt(var + LN_EPS)
    return xc * rstd, rstd


def _ln_bwd(dxhat, xhat, rstd):
    m1 = jnp.mean(dxhat, axis=-1, keepdims=True)
    m2 = jnp.mean(dxhat * xhat, axis=-1, keepdims=True)
    return rstd * (dxhat - m1 - xhat * m2)


def _masked_sgu_w(sw_ref, hh):
    row = lax.broadcasted_iota(jnp.int32, (CHUNK, CHUNK), 0)
    col = lax.broadcasted_iota(jnp.int32, (CHUNK, CHUNK), 1)
    return jnp.where(row >= col, sw_ref[hh], 0.0)


def _inv_count(tile_index, tok_tile, window):
    tok = tile_index * tok_tile + lax.broadcasted_iota(jnp.int32, (tok_tile, 1), 0)
    return 1.0 / jnp.minimum(tok + 1, window).astype(F32)


def _params(**kw):
    return pltpu.CompilerParams(vmem_limit_bytes=VMEM_LIMIT, **kw)


def _forward_mixers(x, w_in, pool_w, pool_scale, sgu_ln_g, sgu_ln_b, sgu_w, sgu_bias_tile, tok_tile, sides=()):
    seq = x.shape[0]
    n_tiles = seq // tok_tile
    n_chunks = tok_tile // CHUNK
    side_refs = _SideRefs(sides, 8, 4, 2)

    def body(*refs):
        (x_ref, win_ref, pw_ref, ps_ref, lg_ref, lb_ref, sw_ref, sb_ref,
         hb_ref, y_ref, pooled_ref, xb_ref, aext_ref, h_ref) = side_refs.split(refs)
        i = pl.program_id(0)
        side_refs.emit(i == 0, i == n_tiles // 2, False, late=(i == (3 * n_tiles) // 4))

        @pl.when(i == 0)
        def _():
            aext_ref[0:HALO, :] = jnp.zeros((HALO, D_POOL), F32)

        xb = x_ref[...].astype(BF16)
        xb_ref[...] = xb
        for s in range(D_IN // 1024):
            cs = slice(s * 1024, (s + 1) * 1024)
            section = _dot(xb, win_ref[:, cs])
            h_ref[:, cs] = section
            if s >= 1:
                hb_ref[:, (s - 1) * 1024:s * 1024] = section.astype(BF16)

        aext_ref[HALO:HALO + tok_tile, :] = h_ref[:, 0:D_POOL]
        for g, window in enumerate(POOL_WINDOWS):
            cols = slice(g * POOL_GROUP, (g + 1) * POOL_GROUP)
            win = aext_ref[HALO:HALO + tok_tile, cols]
            for k in range(1, window):
                win = win + aext_ref[HALO - k:HALO - k + tok_tile, cols]
            pooled = win * _inv_count(i, tok_tile, window) - h_ref[:, cols]
            pb = pooled.astype(BF16)
            pooled_ref[:, cols] = pb
            mixed = _dot(pb, pw_ref[g])
            z = h_ref[:, Z0 + g * POOL_GROUP:Z0 + (g + 1) * POOL_GROUP]
            y_ref[:, cols] = (mixed * ps_ref[:, cols] * (z * _sigmoid(z))).astype(BF16)
        aext_ref[0:HALO, :] = aext_ref[tok_tile:tok_tile + HALO, :]

        for hh in range(N_HEADS):
            cols = slice(hh * HEAD, (hh + 1) * HEAD)
            swm = _masked_sgu_w(sw_ref, hh).astype(BF16)
            for n in range(n_chunks):
                rows = slice(n * CHUNK, (n + 1) * CHUNK)
                gu = _gelu(h_ref[rows, U0 + hh * HEAD:U0 + (hh + 1) * HEAD])
                gv = _gelu(h_ref[rows, V0 + hh * HEAD:V0 + (hh + 1) * HEAD])
                xhat, _ = _row_stats(gv)
                vln = xhat * lg_ref[:, cols] + lb_ref[:, cols]
                sv = _dot(swm, vln.astype(BF16)) + sb_ref[:, cols]
                z = h_ref[rows, Z0 + D_POOL + hh * HEAD:Z0 + D_POOL + (hh + 1) * HEAD]
                y_ref[rows, D_POOL + hh * HEAD:D_POOL + (hh + 1) * HEAD] = (
                    gu * sv * (z * _sigmoid(z))).astype(BF16)

        side_refs.emit(False, False, i == n_tiles - 1, late=False)

    tok = lambda width: pl.BlockSpec((tok_tile, width), lambda i: (i, 0))
    outs = pl.pallas_call(
        body, name="forward_mixers",
        grid=(n_tiles,),
        in_specs=[tok(D_MODEL)] + [VMEM_FULL] * 7 + side_refs.in_specs,
        out_specs=[tok(D_IN - D_POOL), tok(D_MIX), tok(D_POOL), tok(D_MODEL)] + side_refs.out_specs,
        out_shape=[jax.ShapeDtypeStruct((seq, D_IN - D_POOL), BF16), jax.ShapeDtypeStruct((seq, D_MIX), BF16),
                   jax.ShapeDtypeStruct((seq, D_POOL), BF16), jax.ShapeDtypeStruct((seq, D_MODEL), BF16)]
        + side_refs.out_shapes,
        scratch_shapes=[pltpu.VMEM((HALO + tok_tile, D_POOL), F32), pltpu.VMEM((tok_tile, D_IN), F32)]
        + side_refs.scratch,
        compiler_params=_params(dimension_semantics=("arbitrary",)),
    )(x, w_in, pool_w, pool_scale, sgu_ln_g, sgu_ln_b, sgu_w, sgu_bias_tile, *side_refs.inputs)
    return outs[:4], outs[4:]


def _head_fwd_bwd(x, y, p, target, w_out, gate_w, ple_w, ln_g, ln_b, gate_b, tok_tile, sub_tile):
    seq = x.shape[0]

    def body(x_ref, y_ref, p_ref, t_ref, wout_ref, gw_ref, plw_ref, lng_ref, lnb_ref, gb_ref,
             dy_ref, dxr_ref, xn_ref, dgp_ref, dpe_ref, dr_ref, loss_ref, dlng_ref, dlnb_ref, dgb_ref):
        i = pl.program_id(0)

        @pl.when(i == 0)
        def _():
            loss_ref[...] = jnp.zeros_like(loss_ref)
            dlng_ref[...] = jnp.zeros_like(dlng_ref)
            dlnb_ref[...] = jnp.zeros_like(dlnb_ref)
            dgb_ref[...] = jnp.zeros_like(dgb_ref)

        subs = [slice(s * sub_tile, (s + 1) * sub_tile) for s in range(tok_tile // sub_tile)]
        stats, xns, douts = [], [], []
        for rows in subs:
            r = ALPHA * x_ref[rows, :] + _dot(y_ref[rows, :], wout_ref[...])
            xhat, rstd = _row_stats(r)
            xn = xhat * lng_ref[...] + lnb_ref[...]
            xn_ref[rows, :] = xn.astype(BF16)
            stats.append((xhat, rstd))
            xns.append(xn)
        loss = jnp.zeros((1, 1), F32)
        dgb = jnp.zeros((1, D_MODEL), F32)
        for rows, xn in zip(subs, xns):
            gate = _sigmoid(_dot(xn_ref[rows, :], gw_ref[...]) + gb_ref[...])
            pe = _dot(p_ref[rows, :].astype(BF16), plw_ref[...])
            err = xn + gate * pe - t_ref[rows, :]
            loss = loss + jnp.sum(err * err, keepdims=True)
            dout = err * (1.0 / D_MODEL)
            dpe_ref[rows, :] = (dout * gate).astype(BF16)
            dgpre = dout * pe * gate * (1.0 - gate)
            dgb = dgb + jnp.sum(dgpre, axis=0, keepdims=True)
            dgp_ref[rows, :] = dgpre.astype(BF16)
            douts.append(dout)
        loss_ref[...] += (0.5 / D_MODEL) * loss
        dgb_ref[...] += dgb
        dlng = jnp.zeros((1, D_MODEL), F32)
        dlnb = jnp.zeros((1, D_MODEL), F32)
        for rows, (xhat, rstd), dout in zip(subs, stats, douts):
            dxn = dout + _dot_nt(dgp_ref[rows, :], gw_ref[...])
            dlng = dlng + jnp.sum(dxn * xhat, axis=0, keepdims=True)
            dlnb = dlnb + jnp.sum(dxn, axis=0, keepdims=True)
            dr = _ln_bwd(dxn * lng_ref[...], xhat, rstd)
            dxr_ref[rows, :] = ALPHA * dr
            dr_ref[rows, :] = dr.astype(BF16)
        dlng_ref[...] += dlng
        dlnb_ref[...] += dlnb
        for rows in subs:
            dy_ref[rows, :] = _dot_nt(dr_ref[rows, :], wout_ref[...])

    tok = lambda width: pl.BlockSpec((tok_tile, width), lambda i: (i, 0))
    acc = lambda width: pl.BlockSpec((1, width), lambda i: (0, 0))
    vec = jax.ShapeDtypeStruct((1, D_MODEL), F32)
    return pl.pallas_call(
        body, name="head_fwd_bwd",
        grid=(seq // tok_tile,),
        in_specs=[tok(D_MODEL), tok(D_MIX), tok(D_PLE), tok(D_MODEL),
                  VMEM_FULL, VMEM_FULL, VMEM_FULL, VMEM_FULL, VMEM_FULL, VMEM_FULL],
        out_specs=[tok(D_MIX), tok(D_MODEL), tok(D_MODEL), tok(D_MODEL), tok(D_MODEL), tok(D_MODEL),
                   acc(128), acc(D_MODEL), acc(D_MODEL), acc(D_MODEL)],
        out_shape=[jax.ShapeDtypeStruct((seq, D_MIX), F32), jax.ShapeDtypeStruct((seq, D_MODEL), F32),
                   jax.ShapeDtypeStruct((seq, D_MODEL), BF16), jax.ShapeDtypeStruct((seq, D_MODEL), BF16),
                   jax.ShapeDtypeStruct((seq, D_MODEL), BF16), jax.ShapeDtypeStruct((seq, D_MODEL), BF16),
                   jax.ShapeDtypeStruct((1, 128), F32), vec, vec, vec],
        compiler_params=_params(dimension_semantics=("arbitrary",)),
    )(x, y, p, target, w_out, gate_w, ple_w, ln_g, ln_b, gate_b)


def _mixers_bwd(h, dy, dxr, pooled, w_in, pool_w, pool_scale, sgu_ln_g, sgu_ln_b, sgu_w, sgu_bias_tile, tok_tile):
    seq = h.shape[0]
    n_tiles = seq // tok_tile
    n_chunks = tok_tile // CHUNK
    pool_rows = POOL_GROUP // N_DEV

    def body(h_ref, dy_ref, dxr_ref, pooled_ref, win_ref, pw_ref, ps_ref, lg_ref, lb_ref, sw_ref, sb_ref, band_ref,
             dh_ref, dx_ref, dpw_ref, dps_ref, dlg_ref, dlb_ref, dsw_ref, dsb_ref,
             qhi_ref, qlo_ref, dpw_acc, dsb_acc):
        i = pl.program_id(0)
        tile = n_tiles - 1 - i

        @pl.when(i == 0)
        def _():
            qhi_ref[...] = jnp.zeros_like(qhi_ref)
            qlo_ref[...] = jnp.zeros_like(qlo_ref)
            dpw_acc[...] = jnp.zeros_like(dpw_acc)
            dsb_acc[...] = jnp.zeros_like(dsb_acc)
            dps_ref[...] = jnp.zeros_like(dps_ref)
            dlg_ref[...] = jnp.zeros_like(dlg_ref)
            dlb_ref[...] = jnp.zeros_like(dlb_ref)
            dsw_ref[...] = jnp.zeros_like(dsw_ref)

        def h_at(rows, cols):
            return h_ref[rows, cols.start - D_POOL:cols.stop - D_POOL].astype(F32)

        everything = slice(0, tok_tile)
        for part in (qhi_ref, qlo_ref):
            part[tok_tile:tok_tile + HALO, :] = part[0:HALO, :]
        for g, window in enumerate(POOL_WINDOWS):
            cols = slice(g * POOL_GROUP, (g + 1) * POOL_GROUP)
            zcols = slice(Z0 + g * POOL_GROUP, Z0 + (g + 1) * POOL_GROUP)
            z = h_at(everything, zcols)
            sz = _sigmoid(z)
            pb = pooled_ref[:, cols]
            mixed = _dot(pb, pw_ref[g])
            dyp = dy_ref[:, cols]
            dh_ref[:, zcols] = (dyp * (mixed * ps_ref[:, cols]) * (sz * (1.0 + z * (1.0 - sz)))).astype(BF16)
            dms = dyp * (z * sz)
            dps_ref[:, cols] += jnp.sum(dms * mixed, axis=0, keepdims=True)
            dmixed = (dms * ps_ref[:, cols]).astype(BF16)
            dpw_acc[g] += _dot_tn(pb, dmixed)
            dpooled = _dot_nt(dmixed, pw_ref[g])
            qhi_ref[everything, cols], qlo_ref[everything, cols] = _split_bf16(
                dpooled * _inv_count(tile, tok_tile, window))
            da = _dot(band_ref[g], qhi_ref[:, cols]) + _dot(band_ref[g], qlo_ref[:, cols]) - dpooled
            dh_ref[:, cols] = da.astype(BF16)

        pool_cols = [slice(o + g * POOL_GROUP, o + (g + 1) * POOL_GROUP)
                     for o in (0, Z0) for g in range(len(POOL_WINDOWS))]
        pool_slices = [pool_cols[0:3], pool_cols[3:6], pool_cols[6:8], []]
        for hh in range(N_HEADS):
            cols = slice(hh * HEAD, (hh + 1) * HEAD)
            ucols = slice(U0 + hh * HEAD, U0 + (hh + 1) * HEAD)
            vcols = slice(V0 + hh * HEAD, V0 + (hh + 1) * HEAD)
            zcols = slice(Z0 + D_POOL + hh * HEAD, Z0 + D_POOL + (hh + 1) * HEAD)
            sw32 = _masked_sgu_w(sw_ref, hh)
            swm = sw32.astype(BF16)
            swm_t = sw32.T.astype(BF16)
            for n in range(n_chunks):
                rows = slice(n * CHUNK, (n + 1) * CHUNK)
                gu, dgu_du = _gelu_and_grad(h_at(rows, ucols))
                gv, dgv_dv = _gelu_and_grad(h_at(rows, vcols))
                xhat, rstd = _row_stats(gv)
                vb = (xhat * lg_ref[:, cols] + lb_ref[:, cols]).astype(BF16)
                sv = _dot(swm, vb) + sb_ref[:, cols]
                z = h_at(rows, zcols)
                sz = _sigmoid(z)
                dys = dy_ref[rows, D_POOL + hh * HEAD:D_POOL + (hh + 1) * HEAD]
                dh_ref[rows, zcols] = (dys * (gu * sv) * (sz * (1.0 + z * (1.0 - sz)))).astype(BF16)
                dyg = dys * (z * sz)
                dh_ref[rows, ucols] = (dyg * sv * dgu_du).astype(BF16)
                dsv = dyg * gu
                dsb_acc[:, cols] += dsv
                dsvb = dsv.astype(BF16)
                dsw_ref[hh] += _dot_nt(dsvb, vb)
                dvln = _dot(swm_t, dsvb)
                dlg_ref[:, cols] += jnp.sum(dvln * xhat, axis=0, keepdims=True)
                dlb_ref[:, cols] += jnp.sum(dvln, axis=0, keepdims=True)
                dgv = _ln_bwd(dvln * lg_ref[:, cols], xhat, rstd)
                dh_ref[rows, vcols] = (dgv * dgv_dv).astype(BF16)
            ready = [ucols, vcols, zcols] + pool_slices[hh]
            part = _dot_nt(dh_ref[:, ready[0]], win_ref[:, ready[0]])
            for sl in ready[1:]:
                part = part + _dot_nt(dh_ref[:, sl], win_ref[:, sl])
            if hh == 0:
                dx_ref[...] = dxr_ref[...] + part
            else:
                dx_ref[...] += part

        @pl.when(i == n_tiles - 1)
        def _():
            for g in range(len(POOL_WINDOWS)):
                for b in range(N_DEV):
                    dpw_ref[b, g] = dpw_acc[g, b * pool_rows:(b + 1) * pool_rows, :]
            row = lax.broadcasted_iota(jnp.int32, (CHUNK, CHUNK), 0)
            col = lax.broadcasted_iota(jnp.int32, (CHUNK, CHUNK), 1)
            for hh in range(N_HEADS):
                dsw_ref[hh] = jnp.where(row >= col, dsw_ref[hh], 0.0)
                total = jnp.sum(dsb_acc[:, hh * HEAD:(hh + 1) * HEAD], axis=1, keepdims=True)
                dsb_ref[hh:hh + 1, :] = jnp.broadcast_to(total, (CHUNK, CHUNK)).T[0:1, :]

    tok = lambda width: pl.BlockSpec((tok_tile, width), lambda i: (n_tiles - 1 - i, 0))
    whole = lambda shape: pl.BlockSpec(shape, lambda i: (0,) * len(shape))
    vec = jax.ShapeDtypeStruct((1, D_MODEL), F32)
    return pl.pallas_call(
        body, name="mixers_bwd",
        grid=(n_tiles,),
        in_specs=[tok(D_IN - D_POOL), tok(D_MIX), tok(D_MODEL), tok(D_POOL)] + [VMEM_FULL] * 8,
        out_specs=[tok(D_IN), tok(D_MODEL), whole((N_DEV, len(POOL_WINDOWS), pool_rows, POOL_GROUP)),
                   whole((1, D_POOL)), whole((1, D_SGU)), whole((1, D_SGU)),
                   whole((N_HEADS, CHUNK, CHUNK)), whole((N_HEADS, CHUNK))],
        out_shape=[jax.ShapeDtypeStruct((seq, D_IN), BF16), jax.ShapeDtypeStruct((seq, D_MODEL), F32),
                   jax.ShapeDtypeStruct((N_DEV, len(POOL_WINDOWS), pool_rows, POOL_GROUP), F32),
                   vec, vec, vec,
                   jax.ShapeDtypeStruct((N_HEADS, CHUNK, CHUNK), F32),
                   jax.ShapeDtypeStruct((N_HEADS, CHUNK), F32)],
        scratch_shapes=[pltpu.VMEM((tok_tile + BAND_PAD, D_POOL), BF16),
                        pltpu.VMEM((tok_tile + BAND_PAD, D_POOL), BF16),
                        pltpu.VMEM((len(POOL_WINDOWS), POOL_GROUP, POOL_GROUP), F32),
                        pltpu.VMEM((CHUNK, D_SGU), F32)],
        compiler_params=_params(dimension_semantics=("arbitrary",)),
    )(h, dy, dxr, pooled, w_in, pool_w, pool_scale, sgu_ln_g, sgu_ln_b, sgu_w, sgu_bias_tile, _bands(tok_tile, True))


def _weight_grad(a, b, n_col_blocks, blocks_per_step, tok_tile, name, sides=()):
    seq, m = a.shape
    n = b.shape[1]
    nb = n // n_col_blocks
    n_steps = n_col_blocks // blocks_per_step
    n_k = seq // tok_tile
    side_refs = _SideRefs(sides, 2, 1, 0)

    def body(*refs):
        a_ref, b_ref, out_ref = side_refs.split(refs)
        step = pl.program_id(0) * n_k + pl.program_id(1)
        side_refs.emit(step == 0, step == (n_steps * n_k) // 2, False)

        @pl.when(pl.program_id(1) == 0)
        def _():
            out_ref[...] = jnp.zeros_like(out_ref)

        res = _dot_tn(a_ref[...].astype(BF16), b_ref[...])
        for blk in range(blocks_per_step):
            out_ref[blk] += res[:, blk * nb:(blk + 1) * nb]

        side_refs.emit(False, False, step == n_steps * n_k - 1)

    outs = pl.pallas_call(
        body, name=name,
        grid=(n_steps, n_k),
        in_specs=[pl.BlockSpec((tok_tile, m), lambda j, k: (k, 0)),
                  pl.BlockSpec((tok_tile, blocks_per_step * nb), lambda j, k: (k, j))] + side_refs.in_specs,
        out_specs=[pl.BlockSpec((blocks_per_step, m, nb), lambda j, k: (j, 0, 0))] + side_refs.out_specs,
        out_shape=[jax.ShapeDtypeStruct((n_col_blocks, m, nb), F32)] + side_refs.out_shapes,
        scratch_shapes=side_refs.scratch,
        compiler_params=_params(dimension_semantics=("arbitrary", "arbitrary")),
    )(a, b, *side_refs.inputs)
    return outs[0], outs[1:]


class _Side(NamedTuple):
    inputs: list
    out_shapes: list
    sem_shapes: list
    emit: Callable


def _when(cond):
    if cond is True:
        return lambda f: f()
    if cond is False:
        return lambda f: None
    return pl.when(cond)


def _place():
    return lax.axis_index("x"), lax.axis_index("y"), lax.axis_index("c")


def _other_chips(x, y):
    return [(1 - x, y), (x, 1 - y), (1 - x, 1 - y)]


def _gather_side(shards, out_shapes, views, halves):
    n = len(shards)
    n_sems = 10

    def emit(ins, outs, sems, first, mid, last, late=None):
        send_sems, recv_sems, local_sems = sems
        x, y, c = _place()
        here, x_nbr, y_nbr, diag = (x, y), (1 - x, y), (x, 1 - y), (1 - x, 1 - y)
        sibling = (x, y, 1 - c)

        def block(k, chip, core):
            return views[k](outs[k], 4 * chip[0] + 2 * chip[1] + core)

        def piece(k, ref, p):
            return ref if halves[k] is None else halves[k](ref)[p]

        def n_pieces(k):
            return 1 if halves[k] is None else 2

        def copy(k, s, src, dst, to):
            return pltpu.make_async_remote_copy(
                src_ref=src, dst_ref=dst, send_sem=send_sems.at[k, s], recv_sem=recv_sems.at[k, s],
                device_id=to, device_id_type=MESH)

        def outgoing(k, s):
            mine = block(k, here, c)
            if s == 0:
                return copy(k, 0, ins[k], mine, sibling)
            if s in (1, 2):
                return copy(k, s, piece(k, ins[k], s - 1), piece(k, mine, s - 1), (*x_nbr, c))
            if s in (3, 4):
                return copy(k, s, piece(k, ins[k], s - 3), piece(k, mine, s - 3), (*y_nbr, c))
            if s == 5:
                part = piece(k, block(k, x_nbr, c), 0)
                return copy(k, 5, part, part, (*y_nbr, c))
            if s == 6:
                part = piece(k, block(k, y_nbr, c), 1)
                return copy(k, 6, part, part, (*x_nbr, c))
            whole = block(k, (x_nbr, y_nbr, diag)[s - 7], c)
            return copy(k, s, whole, whole, sibling)

        def incoming(k, s):
            if s == 0:
                zone = block(k, here, 1 - c)
            elif s in (1, 2):
                zone = piece(k, block(k, x_nbr, c), s - 1)
            elif s in (3, 4):
                zone = piece(k, block(k, y_nbr, c), s - 3)
            elif s in (5, 6):
                zone = piece(k, block(k, diag, c), s - 5)
            else:
                zone = block(k, (x_nbr, y_nbr, diag)[s - 7], 1 - c)
            return copy(k, s, zone, zone, sibling)

        def own(k):
            return pltpu.make_async_copy(ins[k], block(k, here, c), local_sems.at[k])

        def used(k):
            return list(range(n_sems)) if n_pieces(k) == 2 else [0, 1, 3, 5, 7, 8, 9]

        @_when(first)
        def _():
            for k in range(n):
                own(k).start()
                for s in (0, 1, 4, 2, 3):
                    if s in used(k):
                        outgoing(k, s).start()

        @_when(mid)
        def _():
            for k in range(n):
                incoming(k, 1).wait_recv()
                outgoing(k, 5).start()
                if n_pieces(k) == 2:
                    incoming(k, 4).wait_recv()
                    outgoing(k, 6).start()
                    incoming(k, 2).wait_recv()
                outgoing(k, 7).start()
                incoming(k, 3).wait_recv()
                outgoing(k, 8).start()

        @_when(last if late is None else late)
        def _():
            for k in range(n):
                incoming(k, 5).wait_recv()
                if n_pieces(k) == 2:
                    incoming(k, 6).wait_recv()
                outgoing(k, 9).start()

        @_when(last)
        def _():
            for k in range(n):
                for s in (0, 7, 8, 9):
                    incoming(k, s).wait_recv()
            for k in range(n):
                for s in used(k):
                    outgoing(k, s).wait_send()
                own(k).wait()

    sems = [pltpu.SemaphoreType.DMA((n, n_sems)), pltpu.SemaphoreType.DMA((n, n_sems)),
            pltpu.SemaphoreType.DMA((n,))]
    return _Side(list(shards), list(out_shapes), sems, emit)


def _pair_side(grads):
    n = len(grads)

    def emit(ins, theirs, sems, first, mid, last, late=None):
        send_sems, recv_sems = sems
        x, y, c = _place()

        def copies():
            return [pltpu.make_async_remote_copy(
                src_ref=ins[k].at[2 * j + (1 - c)], dst_ref=theirs[k].at[j],
                send_sem=send_sems.at[k, j], recv_sem=recv_sems.at[k, j],
                device_id=(x, y, 1 - c), device_id_type=MESH) for k in range(n) for j in range(4)]

        @_when(first)
        def _():
            for cp in copies():
                cp.start()

        @_when(last)
        def _():
            for cp in copies():
                cp.wait_recv()
            for cp in copies():
                cp.wait_send()

    shapes = [jax.ShapeDtypeStruct((4,) + g.shape[1:], g.dtype) for g in grads]
    return _Side(list(grads), shapes, [pltpu.SemaphoreType.DMA((n, 4)), pltpu.SemaphoreType.DMA((n, 4))], emit)


def _chip_side(sums):
    n = len(sums)

    def emit(ins, others, sems, first, mid, last, late=None):
        send_sems, recv_sems = sems
        x, y, c = _place()

        def copies():
            return [pltpu.make_async_remote_copy(
                src_ref=ins[k].at[2 * px + py], dst_ref=others[k].at[r],
                send_sem=send_sems.at[k, r], recv_sem=recv_sems.at[k, r],
                device_id=(px, py, c), device_id_type=MESH)
                for k in range(n) for r, (px, py) in enumerate(_other_chips(x, y))]

        @_when(first)
        def _():
            for cp in copies():
                cp.start()

        @_when(last)
        def _():
            for cp in copies():
                cp.wait_recv()
            for cp in copies():
                cp.wait_send()

    shapes = [jax.ShapeDtypeStruct((3,) + s.shape[1:], s.dtype) for s in sums]
    return _Side(list(sums), shapes, [pltpu.SemaphoreType.DMA((n, 3)), pltpu.SemaphoreType.DMA((n, 3))], emit)


def _comm_call(side, name):
    n_in, n_out = len(side.inputs), len(side.out_shapes)

    def body(*refs):
        side.emit(refs[:n_in], refs[n_in:n_in + n_out], refs[n_in + n_out:], True, True, True)

    return pl.pallas_call(
        body, name=name, in_specs=[ANY] * n_in, out_specs=[ANY] * n_out,
        out_shape=side.out_shapes, scratch_shapes=side.sem_shapes,
    )(*side.inputs)


class _SideRefs:
    def __init__(self, sides, n_in, n_out, n_scratch):
        self.sides, self.n_in, self.n_out, self.n_scratch = sides, n_in, n_out, n_scratch
        self.inputs = [a for s in sides for a in s.inputs]
        self.out_shapes = [o for s in sides for o in s.out_shapes]
        self.scratch = [m for s in sides for m in s.sem_shapes]
        self.in_specs = [ANY] * len(self.inputs)
        self.out_specs = [ANY] * len(self.out_shapes)

    def split(self, refs):
        refs = list(refs)
        n_side_in, n_side_out = len(self.inputs), len(self.out_shapes)
        ins, rest = refs[:self.n_in], refs[self.n_in:]
        side_in, rest = rest[:n_side_in], rest[n_side_in:]
        outs, rest = rest[:self.n_out], rest[self.n_out:]
        side_out, rest = rest[:n_side_out], rest[n_side_out:]
        scratch, side_sems = rest[:self.n_scratch], rest[self.n_scratch:]
        self._refs = (side_in, side_out, side_sems)
        return ins + outs + scratch

    def emit(self, first, mid, last, late=None):
        side_in, side_out, side_sems = self._refs
        for s in self.sides:
            a, b, m = len(s.inputs), len(s.out_shapes), len(s.sem_shapes)
            s.emit(side_in[:a], side_out[:b], side_sems[:m], first, mid, last, late)
            side_in, side_out, side_sems = side_in[a:], side_out[b:], side_sems[m:]


ROW_RELATIONS = (0, 1, 2)


def _row_order(x, y):
    chips = _other_chips(x, y)
    return jnp.stack([2 * px + py for px, py in [chips[r] for r in ROW_RELATIONS] + [(x, y)]]).astype(jnp.int32)


def _grad_w_in_reduced(order, a, b, tok_tile, sides=()):
    seq, m = a.shape
    nb = b.shape[1] // N_DEV
    n_k = seq // tok_tile
    n_rows = 4
    last_step = n_rows * n_k - 1
    assert n_k >= 3
    fetch_at, sum_at = 1, 2
    side_refs = _SideRefs(sides, 3, 3, 8)

    def body(*refs):
        (order_ref, a_ref, b_ref, own_ref, theirs_ref, others_ref,
         acc_ref, stage_ref, sumbf_ref, pair_send, pair_recv, ici_send, ici_recv, stage_sem) = side_refs.split(refs)
        j, k = pl.program_id(0), pl.program_id(1)
        step = j * n_k + k
        side_refs.emit(step == 0, step == (n_rows * n_k) // 2, False, late=(step == (3 * n_rows * n_k) // 4))
        x, y, c = _place()
        chips = _other_chips(x, y)

        def to_sibling(row):
            return pltpu.make_async_remote_copy(
                src_ref=acc_ref.at[row % 2, 1 - c], dst_ref=theirs_ref.at[row],
                send_sem=pair_send.at[row], recv_sem=pair_recv.at[row],
                device_id=(x, y, 1 - c), device_id_type=MESH)

        def to_owner(row):
            rel = ROW_RELATIONS[row]
            px, py = chips[rel]
            return pltpu.make_async_remote_copy(
                src_ref=sumbf_ref.at[row], dst_ref=others_ref.at[rel],
                send_sem=ici_send.at[rel], recv_sem=ici_recv.at[rel],
                device_id=(px, py, c), device_id_type=MESH)

        def staged(row):
            return pltpu.make_async_copy(theirs_ref.at[row], stage_ref, stage_sem.at[0])

        @pl.when(k == 0)
        def _():
            acc_ref[j % 2] = jnp.zeros((2, m, nb), F32)

        res = _dot_tn(a_ref[...].astype(BF16), b_ref[...])
        for blk in range(2):
            acc_ref[j % 2, blk] += res[:, blk * nb:(blk + 1) * nb]

        for row in range(n_rows):
            @pl.when((j == row) & (k == n_k - 1))
            def _():
                to_sibling(row).start()

            if row < n_rows - 1:
                @pl.when((j == row + 1) & (k == fetch_at))
                def _():
                    to_sibling(row).wait_recv()
                    staged(row).start()

                @pl.when((j == row + 1) & (k == sum_at))
                def _():
                    staged(row).wait()
                    to_sibling(row).wait_send()
                    sumbf_ref[row] = (acc_ref[row % 2, c] + stage_ref[...]).astype(BF16)
                    to_owner(row).start()

        @pl.when(step == last_step)
        def _():
            row = n_rows - 1
            to_sibling(row).wait_recv()
            staged(row).start()
            staged(row).wait()
            to_sibling(row).wait_send()
            own_ref[...] = acc_ref[row % 2, c] + stage_ref[...]
            for r in range(n_rows - 1):
                to_owner(r).wait_recv()
                to_owner(r).wait_send()

        side_refs.emit(False, False, step == last_step, late=False)

    block = jax.ShapeDtypeStruct((m, nb), F32)
    outs = pl.pallas_call(
        body, name="grad_w_in",
        grid_spec=pltpu.PrefetchScalarGridSpec(
            num_scalar_prefetch=1, grid=(n_rows, n_k),
            in_specs=[pl.BlockSpec((tok_tile, m), lambda j, k, order_ref: (k, 0)),
                      pl.BlockSpec((tok_tile, 2 * nb), lambda j, k, order_ref: (k, order_ref[j]))]
            + side_refs.in_specs,
            out_specs=[pl.BlockSpec((m, nb), lambda j, k, order_ref: (0, 0)), ANY, ANY] + side_refs.out_specs,
            scratch_shapes=[pltpu.VMEM((2, 2, m, nb), F32), pltpu.VMEM((m, nb), F32),
                            pltpu.VMEM((n_rows - 1, m, nb), BF16),
                            pltpu.SemaphoreType.DMA((n_rows,)), pltpu.SemaphoreType.DMA((n_rows,)),
                            pltpu.SemaphoreType.DMA((n_rows - 1,)), pltpu.SemaphoreType.DMA((n_rows - 1,)),
                            pltpu.SemaphoreType.DMA((1,))] + side_refs.scratch),
        out_shape=[block, jax.ShapeDtypeStruct((n_rows, m, nb), F32),
                   jax.ShapeDtypeStruct((n_rows - 1, m, nb), BF16)] + side_refs.out_shapes,
        compiler_params=_params(dimension_semantics=("arbitrary", "arbitrary")),
    )(order, a, b, *side_refs.inputs)
    return outs[0], outs[2], outs[3:]


def _row_tile(rows, cols):
    tile = rows
    while tile * cols > 256 * 1024 and tile % 16 == 0:
        tile //= 2
    return tile


def _pair_sum(core, grads, theirs, name):
    _, rows, cols = theirs.shape
    rt = _row_tile(rows, cols)

    def body(core_ref, a_ref, b_ref, o_ref, ob_ref):
        total = a_ref[...] + b_ref[...]
        o_ref[...] = total
        ob_ref[...] = total.astype(BF16)

    spec = pl.BlockSpec((None, rt, cols), lambda j, i, core_ref: (j, i, 0))
    mine = pl.BlockSpec((None, None, rt, cols), lambda j, i, core_ref: (j, core_ref[0], i, 0))
    return pl.pallas_call(
        body, name=name,
        grid_spec=pltpu.PrefetchScalarGridSpec(
            num_scalar_prefetch=1, grid=(4, rows // rt), in_specs=[mine, spec], out_specs=[spec, spec]),
        out_shape=[jax.ShapeDtypeStruct(theirs.shape, F32), jax.ShapeDtypeStruct(theirs.shape, BF16)],
        compiler_params=_params(dimension_semantics=("arbitrary", "arbitrary")),
    )(core, grads.reshape(4, 2, rows, cols), theirs)


def _adamw(w, g, m, v):
    m = ADAM_B1 * m + (1.0 - ADAM_B1) * g
    v = ADAM_B2 * v + (1.0 - ADAM_B2) * (g * g)
    m_hat = m / (1.0 - ADAM_B1 ** ADAM_STEP)
    v_hat = v / (1.0 - ADAM_B2 ** ADAM_STEP)
    delta = -ADAM_LR * (m_hat / (jnp.sqrt(v_hat) + ADAM_EPS) + ADAM_WD * w)
    return delta, m, v


def _sum_adamw(chip, sums, others, w, m, v, name):
    _, rows, cols = sums.shape
    rt = _row_tile(rows, cols)

    def body(chip_ref, own_ref, oth_ref, w_ref, m_ref, v_ref, g_ref, d_ref, nm_ref, nv_ref):
        g = ((own_ref[...] + oth_ref[0].astype(F32)) + oth_ref[1].astype(F32)) + oth_ref[2].astype(F32)
        g_ref[...] = g
        d_ref[...], nm_ref[...], nv_ref[...] = _adamw(w_ref[...], g, m_ref[...], v_ref[...])

    spec = pl.BlockSpec((rt, cols), lambda i, chip_ref: (i, 0))
    own = pl.BlockSpec((None, rt, cols), lambda i, chip_ref: (chip_ref[0], i, 0))
    shape = jax.ShapeDtypeStruct((rows, cols), F32)
    return pl.pallas_call(
        body, name=name,
        grid_spec=pltpu.PrefetchScalarGridSpec(
            num_scalar_prefetch=1, grid=(rows // rt,),
            in_specs=[own, pl.BlockSpec((3, rt, cols), lambda i, chip_ref: (0, i, 0)), spec, spec, spec],
            out_specs=[spec] * 4),
        out_shape=[shape] * 4,
        compiler_params=_params(dimension_semantics=("arbitrary",)),
    )(chip, sums, others, w, m, v)


def _pair_sum_small(core, grads, theirs, name):
    n = len(grads)

    def body(core_ref, *refs):
        for k in range(n):
            total = refs[k][...] + refs[n + k][...]
            refs[2 * n + 2 * k][...] = total
            refs[2 * n + 2 * k + 1][...] = total.astype(BF16)

    whole = lambda shape: pl.BlockSpec(shape, lambda i, core_ref: (0,) * len(shape))
    mine = [pl.BlockSpec((4, None) + t.shape[1:], lambda i, core_ref: (0, core_ref[0], 0, 0)) for t in theirs]
    outs = pl.pallas_call(
        body, name=name,
        grid_spec=pltpu.PrefetchScalarGridSpec(
            num_scalar_prefetch=1, grid=(1,), in_specs=mine + [whole(t.shape) for t in theirs],
            out_specs=[whole(t.shape) for t in theirs for _ in range(2)]),
        out_shape=[jax.ShapeDtypeStruct(t.shape, dt) for t in theirs for dt in (F32, BF16)],
        compiler_params=_params(dimension_semantics=("arbitrary",)),
    )(core, *[g.reshape((4, 2) + g.shape[1:]) for g in grads], *theirs)
    return [(outs[2 * k], outs[2 * k + 1]) for k in range(n)]


def _sum_adamw_small(chip, items, name):
    n = len(items)

    def body(chip_ref, *refs):
        ins, outs = refs[:5 * n], refs[5 * n:]
        for k in range(n):
            own_ref, oth_ref, w_ref, m_ref, v_ref = ins[5 * k:5 * k + 5]
            g = ((own_ref[...] + oth_ref[0].astype(F32)) + oth_ref[1].astype(F32)) + oth_ref[2].astype(F32)
            outs[4 * k][...] = g
            outs[4 * k + 1][...], outs[4 * k + 2][...], outs[4 * k + 3][...] = _adamw(
                w_ref[...], g, m_ref[...], v_ref[...])

    whole = lambda shape: pl.BlockSpec(shape, lambda i, chip_ref: (0,) * len(shape))
    in_specs, operands, out_specs, out_shape = [], [], [], []
    for sums, others, w, m, v in items:
        in_specs += [pl.BlockSpec((None,) + sums.shape[1:], lambda i, chip_ref: (chip_ref[0], 0, 0)),
                     whole(others.shape), whole(w.shape), whole(m.shape), whole(v.shape)]
        operands += [sums, others, w, m, v]
        out_specs += [whole(w.shape)] * 4
        out_shape += [jax.ShapeDtypeStruct(w.shape, F32)] * 4
    outs = pl.pallas_call(
        body, name=name,
        grid_spec=pltpu.PrefetchScalarGridSpec(num_scalar_prefetch=1, grid=(1,), in_specs=in_specs,
                                               out_specs=out_specs),
        out_shape=out_shape, compiler_params=_params(dimension_semantics=("arbitrary",)),
    )(chip, *operands)
    return [tuple(outs[4 * k:4 * k + 4]) for k in range(n)]


def _small_sum_adamw(gathered, ws, ms, vs):
    n = len(ws)

    def body(*refs):
        g8 = refs[:n + 1]
        w, m, v = refs[n + 1:2 * n + 1], refs[2 * n + 1:3 * n + 1], refs[3 * n + 1:4 * n + 1]
        outs = refs[4 * n + 1:]
        g_out, d_out, m_out, v_out = outs[:n + 1], outs[n + 1:2 * n + 1], outs[2 * n + 1:3 * n + 1], outs[3 * n + 1:]
        for k in range(n + 1):
            g = g8[k][0]
            for b in range(1, N_DEV):
                g = g + g8[k][b]
            g_out[k][...] = g
            if k < n:
                d_out[k][...], m_out[k][...], v_out[k][...] = _adamw(w[k][...], g, m[k][...], v[k][...])

    shapes = [jax.ShapeDtypeStruct(w.shape, F32) for w in ws]
    loss_shape = jax.ShapeDtypeStruct(gathered[-1].shape[1:], F32)
    outs = pl.pallas_call(
        body, name="small_sum_adamw",
        in_specs=[VMEM_FULL] * (4 * n + 1), out_specs=[VMEM_FULL] * (4 * n + 1),
        out_shape=shapes + [loss_shape] + shapes * 3,
        compiler_params=_params(),
    )(*gathered, *ws, *ms, *vs)
    return outs[:n + 1], outs[n + 1:2 * n + 1], outs[2 * n + 1:3 * n + 1], outs[3 * n + 1:]


SMALL_NAMES = ("pool_scale", "sgu_ln_g", "sgu_ln_b", "sgu_w", "sgu_b", "ln_g", "ln_b", "ple_gate_b")


TOK_TILE = 256
GRAD_TILE = 1024


def _weight_views():
    cols = lambda width: (lambda ref, b: ref.at[:, pl.ds(pl.multiple_of(b * width, 128), width)])
    rows = lambda height: (lambda ref, b: ref.at[pl.ds(pl.multiple_of(b * height, 16), height), :])
    pool_rows = POOL_GROUP // N_DEV
    return {"w_in": cols(D_IN // N_DEV),
            "pool_w": lambda ref, b: ref.at[:, pl.ds(pl.multiple_of(b * pool_rows, 16), pool_rows), :],
            "w_out": rows(D_MIX // N_DEV), "ple_w": cols(D_MODEL // N_DEV), "gate_w": rows(D_MODEL // N_DEV)}


WEIGHT_SHAPES = {"w_in": (D_MODEL, D_IN), "pool_w": (len(POOL_WINDOWS), POOL_GROUP, POOL_GROUP),
                 "w_out": (D_MIX, D_MODEL), "ple_w": (D_PLE, D_MODEL), "gate_w": (D_MODEL, D_MODEL)}


def _to_bf16(arrays):
    def body(*refs):
        for src, dst in zip(refs[:len(arrays)], refs[len(arrays):]):
            dst[...] = src[...].astype(BF16)

    return pl.pallas_call(
        body, name="cast_shards", in_specs=[VMEM_FULL] * len(arrays), out_specs=[VMEM_FULL] * len(arrays),
        out_shape=[jax.ShapeDtypeStruct(a.shape, BF16) for a in arrays], compiler_params=_params(),
    )(*arrays)


def _leading_halves(shape):
    whole_tiles = len(shape) >= 3 or shape[0] % 32 == 0
    if shape[0] % 2 or not whole_tiles:
        return None
    half = shape[0] // 2
    return lambda ref: (ref.at[pl.ds(0, half)], ref.at[pl.ds(half, half)])


def _weight_gather(shards, names):
    views = _weight_views()
    return _gather_side([shards[nm] for nm in names],
                        [jax.ShapeDtypeStruct(WEIGHT_SHAPES[nm], BF16) for nm in names], [views[nm] for nm in names],
                        [_leading_halves(shards[nm].shape) for nm in names])


def kernel(x, p, w_in, pool_w, pool_scale, sgu_ln_g, sgu_ln_b, sgu_w, sgu_b, w_out, ln_g, ln_b, ple_w, ple_gate_w, ple_gate_b, loss_target, m_w_in, m_pool_w, m_pool_scale, m_sgu_ln_g, m_sgu_ln_b, m_sgu_w, m_sgu_b, m_w_out, m_ln_g, m_ln_b, m_ple_w, m_ple_gate_w, m_ple_gate_b, v_w_in, v_pool_w, v_pool_scale, v_sgu_ln_g, v_sgu_ln_b, v_sgu_w, v_sgu_b, v_w_out, v_ln_g, v_ln_b, v_ple_w, v_ple_gate_w, v_ple_gate_b):
    seq = x.shape[1]
    x2, p2, target = x[0], p[0, 0], loss_target[0]
    core = lax.axis_index("c").astype(jnp.int32).reshape(1)
    chip = (2 * lax.axis_index("x") + lax.axis_index("y")).astype(jnp.int32).reshape(1)
    pool_rows = POOL_GROUP // N_DEV

    shard_names = ("w_in", "pool_w", "w_out", "ple_w", "gate_w")
    shards = dict(zip(shard_names, _to_bf16([w_in[0], pool_w[0], w_out[0], ple_w[0], ple_gate_w[0]])))
    w_in_f, pool_w_f = _comm_call(_weight_gather(shards, ("w_in", "pool_w")), "gather_mixer_weights")
    bias_tile = jnp.repeat(sgu_b[0].T, HEAD, axis=1)
    (h, y, pooled, xb), (w_out_f, ple_w_f, gate_w_f) = _forward_mixers(
        x2, w_in_f, pool_w_f, pool_scale, sgu_ln_g, sgu_ln_b, sgu_w[0], bias_tile, 2 * TOK_TILE,
        sides=(_weight_gather(shards, ("w_out", "ple_w", "gate_w")),))

    dy, dxr, xn, dgp, dpe, dr, loss, d_ln_g, d_ln_b, d_gate_b = _head_fwd_bwd(
        x2, y, p2, target, w_out_f, gate_w_f, ple_w_f, ln_g, ln_b, ple_gate_b, 2 * TOK_TILE, TOK_TILE)

    d_ple_w, _ = _weight_grad(p2, dpe, N_DEV, N_DEV, GRAD_TILE, "grad_ple_w")
    d_w_out, _ = _weight_grad(y, dr, 1, 1, GRAD_TILE, "grad_w_out")
    w_out_blocks = d_w_out.reshape(N_DEV, D_MIX // N_DEV, D_MODEL)
    d_gate_w, (w_out_theirs,) = _weight_grad(xn, dgp, 1, 1, GRAD_TILE, "grad_gate_w",
                                             sides=(_pair_side([w_out_blocks]),))
    gate_blocks = d_gate_w.reshape(N_DEV, D_MODEL // N_DEV, D_MODEL)

    dh, dx, d_pool_w, d_pool_scale, d_sgu_ln_g, d_sgu_ln_b, d_sgu_w, d_sgu_b = _mixers_bwd(
        h, dy, dxr, pooled, w_in_f, pool_w_f, pool_scale, sgu_ln_g, sgu_ln_b, sgu_w[0], bias_tile, TOK_TILE)

    pool_blocks = d_pool_w.reshape(N_DEV, len(POOL_WINDOWS) * pool_rows, POOL_GROUP)
    ple_theirs, gate_theirs, pool_theirs = _comm_call(
        _pair_side([d_ple_w, gate_blocks, pool_blocks]), "pair_exchange_late")
    hosted_names = ("w_out", "ple_w", "gate_w", "pool_w")
    ple_sums, gate_sums, pool_sums = _pair_sum_small(
        core, [d_ple_w, gate_blocks, pool_blocks], [ple_theirs, gate_theirs, pool_theirs], "pair_sum_small")
    hosted_sums = [_pair_sum(core, w_out_blocks, w_out_theirs, "pair_sum_w_out"), ple_sums, gate_sums, pool_sums]

    small = (d_pool_scale, d_sgu_ln_g, d_sgu_ln_b, d_sgu_w, d_sgu_b, d_ln_g, d_ln_b, d_gate_b)
    parts = small + (loss,)
    small_gather = _gather_side(parts, [jax.ShapeDtypeStruct((N_DEV,) + a.shape, F32) for a in parts],
                                [lambda ref, b: ref.at[b]] * len(parts), [_leading_halves(a.shape) for a in parts])
    cx, cy = lax.axis_index("x"), lax.axis_index("y")
    order = _row_order(cx, cy)
    w_in_own, w_in_others, side_out = _grad_w_in_reduced(
        order, xb, dh, GRAD_TILE, sides=(_chip_side([s_bf for _, s_bf in hosted_sums]), small_gather))
    hosted_others, gathered = side_out[:len(hosted_sums)], side_out[len(hosted_sums):]

    shard_of = {"w_in": (w_in, m_w_in, v_w_in), "pool_w": (pool_w, m_pool_w, v_pool_w),
                "w_out": (w_out, m_w_out, v_w_out), "ple_w": (ple_w, m_ple_w, v_ple_w),
                "gate_w": (ple_gate_w, m_ple_gate_w, v_ple_gate_w)}
    reduced = [(nm, chip, s_f32, oth) for nm, (s_f32, _), oth in zip(hosted_names, hosted_sums, hosted_others)]
    reduced.append(("w_in", jnp.zeros((1,), jnp.int32), w_in_own[None], w_in_others))
    big_out, small_items = {}, []
    for nm, which, s_f32, oth in reduced:
        w, m, v = shard_of[nm]
        two_d = s_f32.shape[1:]
        args = (s_f32, oth, w.reshape(two_d), m.reshape(two_d), v.reshape(two_d))
        if nm in ("ple_w", "pool_w", "gate_w"):
            small_items.append((nm, args))
        else:
            big_out[nm] = [r.reshape(w.shape) for r in _sum_adamw(which, *args, "adamw_" + nm)]
    for (nm, _), res in zip(small_items, _sum_adamw_small(chip, [a for _, a in small_items], "adamw_small")):
        big_out[nm] = [r.reshape(shard_of[nm][0].shape) for r in res]

    small_w = (pool_scale, sgu_ln_g, sgu_ln_b, sgu_w, sgu_b, ln_g, ln_b, ple_gate_b)
    small_m = (m_pool_scale, m_sgu_ln_g, m_sgu_ln_b, m_sgu_w, m_sgu_b, m_ln_g, m_ln_b, m_ple_gate_b)
    small_v = (v_pool_scale, v_sgu_ln_g, v_sgu_ln_b, v_sgu_w, v_sgu_b, v_ln_g, v_ln_b, v_ple_gate_b)
    natural = [[a.reshape(g.shape) for a, g in zip(group, small)] for group in (small_w, small_m, small_v)]
    res = _small_sum_adamw(list(gathered), *natural)
    g_s, d_s, m_s, v_s = [[r.reshape(w.shape) for r, w in zip(kind, small_w)] for kind in res]
    total_loss = res[0][-1][0, 0]

    order = ("w_in", "pool_w", "pool_scale", "sgu_ln_g", "sgu_ln_b", "sgu_w", "sgu_b", "w_out", "ln_g", "ln_b",
             "ple_w", "ple_gate_w", "ple_gate_b")
    outs = [total_loss, dx.reshape(1, seq, D_MODEL)]
    for kind in range(4):
        for nm in order:
            key = "gate_w" if nm == "ple_gate_w" else nm
            if key in big_out:
                outs.append(big_out[key][kind])
            else:
                outs.append((g_s, d_s, m_s, v_s)[kind][SMALL_NAMES.index(nm)])
    return tuple(outs)
```

```python
from typing import Callable, NamedTuple

import numpy as np
import jax
import jax.numpy as jnp
from jax import lax
from jax.experimental import pallas as pl
from jax.experimental.pallas import tpu as pltpu

F32 = jnp.float32
BF16 = jnp.bfloat16

N_DEV = 8
D_MODEL = 1024
D_POOL = 1024
D_SGU = 1024
D_MIX = 2048
D_IN = 5120
D_PLE = 256
POOL_WINDOWS = (2, 4, 8, 16)
POOL_GROUP = 256
N_HEADS = 4
HEAD = 256
CHUNK = 128
HALO = 16
BAND_PAD = 128
ALPHA = 2.0 ** 0.25
LN_EPS = 1e-5
ADAM_LR, ADAM_B1, ADAM_B2, ADAM_EPS, ADAM_WD, ADAM_STEP = 0.001, 0.9, 0.999, 1e-08, 0.01, 10

U0, V0, Z0 = D_POOL, D_POOL + D_SGU, D_POOL + 2 * D_SGU
VMEM_LIMIT = 56 * 1024 * 1024
MESH = pl.DeviceIdType.MESH
ANY = pl.BlockSpec(memory_space=pl.ANY)
VMEM_FULL = pl.BlockSpec(memory_space=pltpu.VMEM)

_GELU_C0 = 0.7978845608028654
_GELU_C1 = 0.044715


def _gelu_cdf(x, x2):
    return 1.0 / (1.0 + jnp.exp(x * ((-2.0 * _GELU_C0) + (-2.0 * _GELU_C0 * _GELU_C1) * x2)))


def _gelu_and_grad(x):
    x2 = x * x
    cdf = _gelu_cdf(x, x2)
    g = x * cdf
    dg = cdf + g * (1.0 - cdf) * ((2.0 * _GELU_C0) + (6.0 * _GELU_C0 * _GELU_C1) * x2)
    return g, dg


def _gelu(x):
    t = jnp.tanh(_GELU_C0 * (x + _GELU_C1 * (x * x * x)))
    return x * (0.5 * (1.0 + t))


def _split_bf16(x):
    hi = x.astype(BF16)
    return hi, (x - hi.astype(F32)).astype(BF16)


def _band(tok_tile, window, transpose):
    t = np.arange(tok_tile)[:, None]
    s = np.arange(tok_tile + BAND_PAD)[None, :]
    d = (s - t) if transpose else (t + BAND_PAD - s)
    return ((d >= 0) & (d < window)).astype(np.float32)


def _bands(tok_tile, transpose):
    return jnp.asarray(np.stack([_band(tok_tile, w, transpose) for w in POOL_WINDOWS]), dtype=BF16)


def _sigmoid(x):
    return 1.0 / (1.0 + jnp.exp(-x))


def _dot(a, b):
    return jnp.dot(a, b, preferred_element_type=F32)


def _dot_nt(a, b):
    return lax.dot_general(a, b, (((1,), (1,)), ((), ())), preferred_element_type=F32)


def _dot_tn(a, b):
    return lax.dot_general(a, b, (((0,), (0,)), ((), ())), preferred_element_type=F32)


def _row_stats(x):
    mu = jnp.mean(x, axis=-1, keepdims=True)
    xc = x - mu
    var = jnp.mean(xc * xc, axis=-1, keepdims=True)
    rstd = lax.rsqrt(var + LN_EPS)
    return xc * rstd, rstd


def _ln_bwd(dxhat, xhat, rstd):
    m1 = jnp.mean(dxhat, axis=-1, keepdims=True)
    m2 = jnp.mean(dxhat * xhat, axis=-1, keepdims=True)
    return rstd * (dxhat - m1 - xhat * m2)


def _masked_sgu_w(sw_ref, hh):
    row = lax.broadcasted_iota(jnp.int32, (CHUNK, CHUNK), 0)
    col = lax.broadcasted_iota(jnp.int32, (CHUNK, CHUNK), 1)
    return jnp.where(row >= col, sw_ref[hh], 0.0)


def _inv_count(tile_index, tok_tile, window):
    tok = tile_index * tok_tile + lax.broadcasted_iota(jnp.int32, (tok_tile, 1), 0)
    return 1.0 / jnp.minimum(tok + 1, window).astype(F32)


def _params(**kw):
    return pltpu.CompilerParams(vmem_limit_bytes=VMEM_LIMIT, **kw)


def _forward_mixers(x, w_in, pool_w, pool_scale, sgu_ln_g, sgu_ln_b, sgu_w, sgu_bias_tile, tok_tile, sides=()):
    seq = x.shape[0]
    n_tiles = seq // tok_tile
    n_chunks = tok_tile // CHUNK
    side_refs = _SideRefs(sides, 8, 4, 2)

    def body(*refs):
        (x_ref, win_ref, pw_ref, ps_ref, lg_ref, lb_ref, sw_ref, sb_ref,
         hb_ref, y_ref, pooled_ref, xb_ref, aext_ref, h_ref) = side_refs.split(refs)
        i = pl.program_id(0)
        side_refs.emit(i == 0, i == n_tiles // 2, False, late=(i == (3 * n_tiles) // 4))

        @pl.when(i == 0)
        def _():
            aext_ref[0:HALO, :] = jnp.zeros((HALO, D_POOL), F32)

        xb = x_ref[...].astype(BF16)
        xb_ref[...] = xb
        for s in range(D_IN // 1024):
            cs = slice(s * 1024, (s + 1) * 1024)
            section = _dot(xb, win_ref[:, cs])
            h_ref[:, cs] = section
            if s >= 1:
                hb_ref[:, (s - 1) * 1024:s * 1024] = section.astype(BF16)

        aext_ref[HALO:HALO + tok_tile, :] = h_ref[:, 0:D_POOL]
        for g, window in enumerate(POOL_WINDOWS):
            cols = slice(g * POOL_GROUP, (g + 1) * POOL_GROUP)
            win = aext_ref[HALO:HALO + tok_tile, cols]
            for k in range(1, window):
                win = win + aext_ref[HALO - k:HALO - k + tok_tile, cols]
            pooled = win * _inv_count(i, tok_tile, window) - h_ref[:, cols]
            pb = pooled.astype(BF16)
            pooled_ref[:, cols] = pb
            mixed = _dot(pb, pw_ref[g])
            z = h_ref[:, Z0 + g * POOL_GROUP:Z0 + (g + 1) * POOL_GROUP]
            y_ref[:, cols] = (mixed * ps_ref[:, cols] * (z * _sigmoid(z))).astype(BF16)
        aext_ref[0:HALO, :] = aext_ref[tok_tile:tok_tile + HALO, :]

        for hh in range(N_HEADS):
            cols = slice(hh * HEAD, (hh + 1) * HEAD)
            swm = _masked_sgu_w(sw_ref, hh).astype(BF16)
            for n in range(n_chunks):
                rows = slice(n * CHUNK, (n + 1) * CHUNK)
                gu = _gelu(h_ref[rows, U0 + hh * HEAD:U0 + (hh + 1) * HEAD])
                gv = _gelu(h_ref[rows, V0 + hh * HEAD:V0 + (hh + 1) * HEAD])
                xhat, _ = _row_stats(gv)
                vln = xhat * lg_ref[:, cols] + lb_ref[:, cols]
                sv = _dot(swm, vln.astype(BF16)) + sb_ref[:, cols]
                z = h_ref[rows, Z0 + D_POOL + hh * HEAD:Z0 + D_POOL + (hh + 1) * HEAD]
                y_ref[rows, D_POOL + hh * HEAD:D_POOL + (hh + 1) * HEAD] = (
                    gu * sv * (z * _sigmoid(z))).astype(BF16)

        side_refs.emit(False, False, i == n_tiles - 1, late=False)

    tok = lambda width: pl.BlockSpec((tok_tile, width), lambda i: (i, 0))
    outs = pl.pallas_call(
        body, name="forward_mixers",
        grid=(n_tiles,),
        in_specs=[tok(D_MODEL)] + [VMEM_FULL] * 7 + side_refs.in_specs,
        out_specs=[tok(D_IN - D_POOL), tok(D_MIX), tok(D_POOL), tok(D_MODEL)] + side_refs.out_specs,
        out_shape=[jax.ShapeDtypeStruct((seq, D_IN - D_POOL), BF16), jax.ShapeDtypeStruct((seq, D_MIX), BF16),
                   jax.ShapeDtypeStruct((seq, D_POOL), BF16), jax.ShapeDtypeStruct((seq, D_MODEL), BF16)]
        + side_refs.out_shapes,
        scratch_shapes=[pltpu.VMEM((HALO + tok_tile, D_POOL), F32), pltpu.VMEM((tok_tile, D_IN), F32)]
        + side_refs.scratch,
        compiler_params=_params(dimension_semantics=("arbitrary",)),
    )(x, w_in, pool_w, pool_scale, sgu_ln_g, sgu_ln_b, sgu_w, sgu_bias_tile, *side_refs.inputs)
    return outs[:4], outs[4:]


def _head_fwd_bwd(x, y, p, target, w_out, gate_w, ple_w, ln_g, ln_b, gate_b, tok_tile, sub_tile):
    seq = x.shape[0]

    def body(x_ref, y_ref, p_ref, t_ref, wout_ref, gw_ref, plw_ref, lng_ref, lnb_ref, gb_ref,
             dy_ref, dxr_ref, xn_ref, dgp_ref, dpe_ref, dr_ref, loss_ref, dlng_ref, dlnb_ref, dgb_ref):
        i = pl.program_id(0)

        @pl.when(i == 0)
        def _():
            loss_ref[...] = jnp.zeros_like(loss_ref)
            dlng_ref[...] = jnp.zeros_like(dlng_ref)
            dlnb_ref[...] = jnp.zeros_like(dlnb_ref)
            dgb_ref[...] = jnp.zeros_like(dgb_ref)

        subs = [slice(s * sub_tile, (s + 1) * sub_tile) for s in range(tok_tile // sub_tile)]
        stats, xns, douts = [], [], []
        for rows in subs:
            r = ALPHA * x_ref[rows, :] + _dot(y_ref[rows, :], wout_ref[...])
            xhat, rstd = _row_stats(r)
            xn = xhat * lng_ref[...] + lnb_ref[...]
            xn_ref[rows, :] = xn.astype(BF16)
            stats.append((xhat, rstd))
            xns.append(xn)
        loss = jnp.zeros((1, 1), F32)
        dgb = jnp.zeros((1, D_MODEL), F32)
        for rows, xn in zip(subs, xns):
            gate = _sigmoid(_dot(xn_ref[rows, :], gw_ref[...]) + gb_ref[...])
            pe = _dot(p_ref[rows, :].astype(BF16), plw_ref[...])
            err = xn + gate * pe - t_ref[rows, :]
            loss = loss + jnp.sum(err * err, keepdims=True)
            dout = err * (1.0 / D_MODEL)
            dpe_ref[rows, :] = (dout * gate).astype(BF16)
            dgpre = dout * pe * gate * (1.0 - gate)
            dgb = dgb + jnp.sum(dgpre, axis=0, keepdims=True)
            dgp_ref[rows, :] = dgpre.astype(BF16)
            douts.append(dout)
        loss_ref[...] += (0.5 / D_MODEL) * loss
        dgb_ref[...] += dgb
        dlng = jnp.zeros((1, D_MODEL), F32)
        dlnb = jnp.zeros((1, D_MODEL), F32)
        for rows, (xhat, rstd), dout in zip(subs, stats, douts):
            dxn = dout + _dot_nt(dgp_ref[rows, :], gw_ref[...])
            dlng = dlng + jnp.sum(dxn * xhat, axis=0, keepdims=True)
            dlnb = dlnb + jnp.sum(dxn, axis=0, keepdims=True)
            dr = _ln_bwd(dxn * lng_ref[...], xhat, rstd)
            dxr_ref[rows, :] = ALPHA * dr
            dr_ref[rows, :] = dr.astype(BF16)
        dlng_ref[...] += dlng
        dlnb_ref[...] += dlnb
        for rows in subs:
            dy_ref[rows, :] = _dot_nt(dr_ref[rows, :], wout_ref[...])

    tok = lambda width: pl.BlockSpec((tok_tile, width), lambda i: (i, 0))
    acc = lambda width: pl.BlockSpec((1, width), lambda i: (0, 0))
    vec = jax.ShapeDtypeStruct((1, D_MODEL), F32)
    return pl.pallas_call(
        body, name="head_fwd_bwd",
        grid=(seq // tok_tile,),
        in_specs=[tok(D_MODEL), tok(D_MIX), tok(D_PLE), tok(D_MODEL),
                  VMEM_FULL, VMEM_FULL, VMEM_FULL, VMEM_FULL, VMEM_FULL, VMEM_FULL],
        out_specs=[tok(D_MIX), tok(D_MODEL), tok(D_MODEL), tok(D_MODEL), tok(D_MODEL), tok(D_MODEL),
                   acc(128), acc(D_MODEL), acc(D_MODEL), acc(D_MODEL)],
        out_shape=[jax.ShapeDtypeStruct((seq, D_MIX), F32), jax.ShapeDtypeStruct((seq, D_MODEL), F32),
                   jax.ShapeDtypeStruct((seq, D_MODEL), BF16), jax.ShapeDtypeStruct((seq, D_MODEL), BF16),
                   jax.ShapeDtypeStruct((seq, D_MODEL), BF16), jax.ShapeDtypeStruct((seq, D_MODEL), BF16),
                   jax.ShapeDtypeStruct((1, 128), F32), vec, vec, vec],
        compiler_params=_params(dimension_semantics=("arbitrary",)),
    )(x, y, p, target, w_out, gate_w, ple_w, ln_g, ln_b, gate_b)


def _mixers_bwd(h, dy, dxr, pooled, w_in, pool_w, pool_scale, sgu_ln_g, sgu_ln_b, sgu_w, sgu_bias_tile, tok_tile):
    seq = h.shape[0]
    n_tiles = seq // tok_tile
    n_chunks = tok_tile // CHUNK
    pool_rows = POOL_GROUP // N_DEV

    def body(h_ref, dy_ref, dxr_ref, pooled_ref, win_ref, pw_ref, ps_ref, lg_ref, lb_ref, sw_ref, sb_ref, band_ref,
             dh_ref, dx_ref, dpw_ref, dps_ref, dlg_ref, dlb_ref, dsw_ref, dsb_ref,
             qhi_ref, qlo_ref, dpw_acc, dsb_acc):
        i = pl.program_id(0)
        tile = n_tiles - 1 - i

        @pl.when(i == 0)
        def _():
            qhi_ref[...] = jnp.zeros_like(qhi_ref)
            qlo_ref[...] = jnp.zeros_like(qlo_ref)
            dpw_acc[...] = jnp.zeros_like(dpw_acc)
            dsb_acc[...] = jnp.zeros_like(dsb_acc)
            dps_ref[...] = jnp.zeros_like(dps_ref)
            dlg_ref[...] = jnp.zeros_like(dlg_ref)
            dlb_ref[...] = jnp.zeros_like(dlb_ref)
            dsw_ref[...] = jnp.zeros_like(dsw_ref)

        def h_at(rows, cols):
            return h_ref[rows, cols.start - D_POOL:cols.stop - D_POOL].astype(F32)

        everything = slice(0, tok_tile)
        for part in (qhi_ref, qlo_ref):
            part[tok_tile:tok_tile + HALO, :] = part[0:HALO, :]
        for g, window in enumerate(POOL_WINDOWS):
            cols = slice(g * POOL_GROUP, (g + 1) * POOL_GROUP)
            zcols = slice(Z0 + g * POOL_GROUP, Z0 + (g + 1) * POOL_GROUP)
            z = h_at(everything, zcols)
            sz = _sigmoid(z)
            pb = pooled_ref[:, cols]
            mixed = _dot(pb, pw_ref[g])
            dyp = dy_ref[:, cols]
            dh_ref[:, zcols] = (dyp * (mixed * ps_ref[:, cols]) * (sz * (1.0 + z * (1.0 - sz)))).astype(BF16)
            dms = dyp * (z * sz)
            dps_ref[:, cols] += jnp.sum(dms * mixed, axis=0, keepdims=True)
            dmixed = (dms * ps_ref[:, cols]).astype(BF16)
            dpw_acc[g] += _dot_tn(pb, dmixed)
            dpooled = _dot_nt(dmixed, pw_ref[g])
            qhi_ref[everything, cols], qlo_ref[everything, cols] = _split_bf16(
                dpooled * _inv_count(tile, tok_tile, window))
            da = _dot(band_ref[g], qhi_ref[:, cols]) + _dot(band_ref[g], qlo_ref[:, cols]) - dpooled
            dh_ref[:, cols] = da.astype(BF16)

        pool_cols = [slice(o + g * POOL_GROUP, o + (g + 1) * POOL_GROUP)
                     for o in (0, Z0) for g in range(len(POOL_WINDOWS))]
        pool_slices = [pool_cols[0:3], pool_cols[3:6], pool_cols[6:8], []]
        for hh in range(N_HEADS):
            cols = slice(hh * HEAD, (hh + 1) * HEAD)
            ucols = slice(U0 + hh * HEAD, U0 + (hh + 1) * HEAD)
            vcols = slice(V0 + hh * HEAD, V0 + (hh + 1) * HEAD)
            zcols = slice(Z0 + D_POOL + hh * HEAD, Z0 + D_POOL + (hh + 1) * HEAD)
            sw32 = _masked_sgu_w(sw_ref, hh)
            swm = sw32.astype(BF16)
            swm_t = sw32.T.astype(BF16)
            for n in range(n_chunks):
                rows = slice(n * CHUNK, (n + 1) * CHUNK)
                gu, dgu_du = _gelu_and_grad(h_at(rows, ucols))
                gv, dgv_dv = _gelu_and_grad(h_at(rows, vcols))
                xhat, rstd = _row_stats(gv)
                vb = (xhat * lg_ref[:, cols] + lb_ref[:, cols]).astype(BF16)
                sv = _dot(swm, vb) + sb_ref[:, cols]
                z = h_at(rows, zcols)
                sz = _sigmoid(z)
                dys = dy_ref[rows, D_POOL + hh * HEAD:D_POOL + (hh + 1) * HEAD]
                dh_ref[rows, zcols] = (dys * (gu * sv) * (sz * (1.0 + z * (1.0 - sz)))).astype(BF16)
                dyg = dys * (z * sz)
                dh_ref[rows, ucols] = (dyg * sv * dgu_du).astype(BF16)
                dsv = dyg * gu
                dsb_acc[:, cols] += dsv
                dsvb = dsv.astype(BF16)
                dsw_ref[hh] += _dot_nt(dsvb, vb)
                dvln = _dot(swm_t, dsvb)
                dlg_ref[:, cols] += jnp.sum(dvln * xhat, axis=0, keepdims=True)
                dlb_ref[:, cols] += jnp.sum(dvln, axis=0, keepdims=True)
                dgv = _ln_bwd(dvln * lg_ref[:, cols], xhat, rstd)
                dh_ref[rows, vcols] = (dgv * dgv_dv).astype(BF16)
            ready = [ucols, vcols, zcols] + pool_slices[hh]
            part = _dot_nt(dh_ref[:, ready[0]], win_ref[:, ready[0]])
            for sl in ready[1:]:
                part = part + _dot_nt(dh_ref[:, sl], win_ref[:, sl])
            if hh == 0:
                dx_ref[...] = dxr_ref[...] + part
            else:
                dx_ref[...] += part

        @pl.when(i == n_tiles - 1)
        def _():
            for g in range(len(POOL_WINDOWS)):
                for b in range(N_DEV):
                    dpw_ref[b, g] = dpw_acc[g, b * pool_rows:(b + 1) * pool_rows, :]
            row = lax.broadcasted_iota(jnp.int32, (CHUNK, CHUNK), 0)
            col = lax.broadcasted_iota(jnp.int32, (CHUNK, CHUNK), 1)
            for hh in range(N_HEADS):
                dsw_ref[hh] = jnp.where(row >= col, dsw_ref[hh], 0.0)
                total = jnp.sum(dsb_acc[:, hh * HEAD:(hh + 1) * HEAD], axis=1, keepdims=True)
                dsb_ref[hh:hh + 1, :] = jnp.broadcast_to(total, (CHUNK, CHUNK)).T[0:1, :]

    tok = lambda width: pl.BlockSpec((tok_tile, width), lambda i: (n_tiles - 1 - i, 0))
    whole = lambda shape: pl.BlockSpec(shape, lambda i: (0,) * len(shape))
    vec = jax.ShapeDtypeStruct((1, D_MODEL), F32)
    return pl.pallas_call(
        body, name="mixers_bwd",
        grid=(n_tiles,),
        in_specs=[tok(D_IN - D_POOL), tok(D_MIX), tok(D_MODEL), tok(D_POOL)] + [VMEM_FULL] * 8,
        out_specs=[tok(D_IN), tok(D_MODEL), whole((N_DEV, len(POOL_WINDOWS), pool_rows, POOL_GROUP)),
                   whole((1, D_POOL)), whole((1, D_SGU)), whole((1, D_SGU)),
                   whole((N_HEADS, CHUNK, CHUNK)), whole((N_HEADS, CHUNK))],
        out_shape=[jax.ShapeDtypeStruct((seq, D_IN), BF16), jax.ShapeDtypeStruct((seq, D_MODEL), F32),
                   jax.ShapeDtypeStruct((N_DEV, len(POOL_WINDOWS), pool_rows, POOL_GROUP), F32),
                   vec, vec, vec,
                   jax.ShapeDtypeStruct((N_HEADS, CHUNK, CHUNK), F32),
                   jax.ShapeDtypeStruct((N_HEADS, CHUNK), F32)],
        scratch_shapes=[pltpu.VMEM((tok_tile + BAND_PAD, D_POOL), BF16),
                        pltpu.VMEM((tok_tile + BAND_PAD, D_POOL), BF16),
                        pltpu.VMEM((len(POOL_WINDOWS), POOL_GROUP, POOL_GROUP), F32),
                        pltpu.VMEM((CHUNK, D_SGU), F32)],
        compiler_params=_params(dimension_semantics=("arbitrary",)),
    )(h, dy, dxr, pooled, w_in, pool_w, pool_scale, sgu_ln_g, sgu_ln_b, sgu_w, sgu_bias_tile, _bands(tok_tile, True))


def _weight_grads(items, tok_tile, name, sides=()):
    n_items = len(items)
    seq = items[0][0].shape[0]
    n_k = seq // tok_tile
    side_refs = _SideRefs(sides, 2 * n_items, n_items, 0)

    def body(*refs):
        refs = side_refs.split(refs)
        k = pl.program_id(0)
        side_refs.emit(k == 0, k == n_k // 2, False)
        for t, (_, b, n_col_blocks) in enumerate(items):
            a_ref, b_ref, out_ref = refs[2 * t], refs[2 * t + 1], refs[2 * n_items + t]
            nb = b.shape[1] // n_col_blocks

            @pl.when(k == 0)
            def _():
                out_ref[...] = jnp.zeros_like(out_ref)

            res = _dot_tn(a_ref[...].astype(BF16), b_ref[...])
            for blk in range(n_col_blocks):
                out_ref[blk] += res[:, blk * nb:(blk + 1) * nb]
        side_refs.emit(False, False, k == n_k - 1)

    in_specs, operands, out_specs, out_shape = [], [], [], []
    for a, b, n_col_blocks in items:
        shape = (n_col_blocks, a.shape[1], b.shape[1] // n_col_blocks)
        in_specs += [pl.BlockSpec((tok_tile, a.shape[1]), lambda k: (k, 0)),
                     pl.BlockSpec((tok_tile, b.shape[1]), lambda k: (k, 0))]
        operands += [a, b]
        out_specs.append(pl.BlockSpec(shape, lambda k: (0, 0, 0)))
        out_shape.append(jax.ShapeDtypeStruct(shape, F32))
    outs = pl.pallas_call(
        body, name=name, grid=(n_k,),
        in_specs=in_specs + side_refs.in_specs, out_specs=out_specs + side_refs.out_specs,
        out_shape=out_shape + side_refs.out_shapes, scratch_shapes=side_refs.scratch,
        compiler_params=_params(dimension_semantics=("arbitrary",)),
    )(*operands, *side_refs.inputs)
    return outs[:n_items], outs[n_items:]


class _Side(NamedTuple):
    inputs: list
    out_shapes: list
    sem_shapes: list
    emit: Callable


def _when(cond):
    if cond is True:
        return lambda f: f()
    if cond is False:
        return lambda f: None
    return pl.when(cond)


def _place():
    return lax.axis_index("x"), lax.axis_index("y"), lax.axis_index("c")


def _other_chips(x, y):
    return [(1 - x, y), (x, 1 - y), (1 - x, 1 - y)]


def _gather_side(shards, out_shapes, views, halves):
    n = len(shards)
    n_sems = 10

    def emit(ins, outs, sems, first, mid, last, late=None):
        send_sems, recv_sems, local_sems = sems
        x, y, c = _place()
        here, x_nbr, y_nbr, diag = (x, y), (1 - x, y), (x, 1 - y), (1 - x, 1 - y)
        sibling = (x, y, 1 - c)

        def block(k, chip, core):
            return views[k](outs[k], 4 * chip[0] + 2 * chip[1] + core)

        def piece(k, ref, p):
            return ref if halves[k] is None else halves[k](ref)[p]

        def n_pieces(k):
            return 1 if halves[k] is None else 2

        def copy(k, s, src, dst, to):
            return pltpu.make_async_remote_copy(
                src_ref=src, dst_ref=dst, send_sem=send_sems.at[k, s], recv_sem=recv_sems.at[k, s],
                device_id=to, device_id_type=MESH)

        def outgoing(k, s):
            mine = block(k, here, c)
            if s == 0:
                return copy(k, 0, ins[k], mine, sibling)
            if s in (1, 2):
                return copy(k, s, piece(k, ins[k], s - 1), piece(k, mine, s - 1), (*x_nbr, c))
            if s in (3, 4):
                return copy(k, s, piece(k, ins[k], s - 3), piece(k, mine, s - 3), (*y_nbr, c))
            if s == 5:
                part = piece(k, block(k, x_nbr, c), 0)
                return copy(k, 5, part, part, (*y_nbr, c))
            if s == 6:
                part = piece(k, block(k, y_nbr, c), 1)
                return copy(k, 6, part, part, (*x_nbr, c))
            whole = block(k, (x_nbr, y_nbr, diag)[s - 7], c)
            return copy(k, s, whole, whole, sibling)

        def incoming(k, s):
            if s == 0:
                zone = block(k, here, 1 - c)
            elif s in (1, 2):
                zone = piece(k, block(k, x_nbr, c), s - 1)
            elif s in (3, 4):
                zone = piece(k, block(k, y_nbr, c), s - 3)
            elif s in (5, 6):
                zone = piece(k, block(k, diag, c), s - 5)
            else:
                zone = block(k, (x_nbr, y_nbr, diag)[s - 7], 1 - c)
            return copy(k, s, zone, zone, sibling)

        def own(k):
            return pltpu.make_async_copy(ins[k], block(k, here, c), local_sems.at[k])

        def used(k):
            return list(range(n_sems)) if n_pieces(k) == 2 else [0, 1, 3, 5, 7, 8, 9]

        @_when(first)
        def _():
            for k in range(n):
                own(k).start()
                for s in (0, 1, 4, 2, 3):
                    if s in used(k):
                        outgoing(k, s).start()

        @_when(mid)
        def _():
            for k in range(n):
                incoming(k, 1).wait_recv()
                outgoing(k, 5).start()
                if n_pieces(k) == 2:
                    incoming(k, 4).wait_recv()
                    outgoing(k, 6).start()
                    incoming(k, 2).wait_recv()
                outgoing(k, 7).start()
                incoming(k, 3).wait_recv()
                outgoing(k, 8).start()

        @_when(last if late is None else late)
        def _():
            for k in range(n):
                incoming(k, 5).wait_recv()
                if n_pieces(k) == 2:
                    incoming(k, 6).wait_recv()
                outgoing(k, 9).start()

        @_when(last)
        def _():
            for k in range(n):
                for s in (0, 7, 8, 9):
                    incoming(k, s).wait_recv()
            for k in range(n):
                for s in used(k):
                    outgoing(k, s).wait_send()
                own(k).wait()

    sems = [pltpu.SemaphoreType.DMA((n, n_sems)), pltpu.SemaphoreType.DMA((n, n_sems)),
            pltpu.SemaphoreType.DMA((n,))]
    return _Side(list(shards), list(out_shapes), sems, emit)


def _pair_side(grads):
    n = len(grads)

    def emit(ins, theirs, sems, first, mid, last, late=None):
        send_sems, recv_sems = sems
        x, y, c = _place()

        def copies():
            return [pltpu.make_async_remote_copy(
                src_ref=ins[k].at[2 * j + (1 - c)], dst_ref=theirs[k].at[j],
                send_sem=send_sems.at[k, j], recv_sem=recv_sems.at[k, j],
                device_id=(x, y, 1 - c), device_id_type=MESH) for k in range(n) for j in range(4)]

        @_when(first)
        def _():
            for cp in copies():
                cp.start()

        @_when(last)
        def _():
            for cp in copies():
                cp.wait_recv()
            for cp in copies():
                cp.wait_send()

    shapes = [jax.ShapeDtypeStruct((4,) + g.shape[1:], g.dtype) for g in grads]
    return _Side(list(grads), shapes, [pltpu.SemaphoreType.DMA((n, 4)), pltpu.SemaphoreType.DMA((n, 4))], emit)


def _chip_side(sums):
    n = len(sums)

    def emit(ins, others, sems, first, mid, last, late=None):
        send_sems, recv_sems = sems
        x, y, c = _place()

        def copies():
            return [pltpu.make_async_remote_copy(
                src_ref=ins[k].at[2 * px + py], dst_ref=others[k].at[r],
                send_sem=send_sems.at[k, r], recv_sem=recv_sems.at[k, r],
                device_id=(px, py, c), device_id_type=MESH)
                for k in range(n) for r, (px, py) in enumerate(_other_chips(x, y))]

        @_when(first)
        def _():
            for cp in copies():
                cp.start()

        @_when(last)
        def _():
            for cp in copies():
                cp.wait_recv()
            for cp in copies():
                cp.wait_send()

    shapes = [jax.ShapeDtypeStruct((3,) + s.shape[1:], s.dtype) for s in sums]
    return _Side(list(sums), shapes, [pltpu.SemaphoreType.DMA((n, 3)), pltpu.SemaphoreType.DMA((n, 3))], emit)


def _comm_call(side, name):
    n_in, n_out = len(side.inputs), len(side.out_shapes)

    def body(*refs):
        side.emit(refs[:n_in], refs[n_in:n_in + n_out], refs[n_in + n_out:], True, True, True)

    return pl.pallas_call(
        body, name=name, in_specs=[ANY] * n_in, out_specs=[ANY] * n_out,
        out_shape=side.out_shapes, scratch_shapes=side.sem_shapes,
    )(*side.inputs)


class _SideRefs:
    def __init__(self, sides, n_in, n_out, n_scratch):
        self.sides, self.n_in, self.n_out, self.n_scratch = sides, n_in, n_out, n_scratch
        self.inputs = [a for s in sides for a in s.inputs]
        self.out_shapes = [o for s in sides for o in s.out_shapes]
        self.scratch = [m for s in sides for m in s.sem_shapes]
        self.in_specs = [ANY] * len(self.inputs)
        self.out_specs = [ANY] * len(self.out_shapes)

    def split(self, refs):
        refs = list(refs)
        n_side_in, n_side_out = len(self.inputs), len(self.out_shapes)
        ins, rest = refs[:self.n_in], refs[self.n_in:]
        side_in, rest = rest[:n_side_in], rest[n_side_in:]
        outs, rest = rest[:self.n_out], rest[self.n_out:]
        side_out, rest = rest[:n_side_out], rest[n_side_out:]
        scratch, side_sems = rest[:self.n_scratch], rest[self.n_scratch:]
        self._refs = (side_in, side_out, side_sems)
        return ins + outs + scratch

    def emit(self, first, mid, last, late=None):
        side_in, side_out, side_sems = self._refs
        for s in self.sides:
            a, b, m = len(s.inputs), len(s.out_shapes), len(s.sem_shapes)
            s.emit(side_in[:a], side_out[:b], side_sems[:m], first, mid, last, late)
            side_in, side_out, side_sems = side_in[a:], side_out[b:], side_sems[m:]


ROW_RELATIONS = (0, 1, 2)


def _row_order(x, y):
    chips = _other_chips(x, y)
    return jnp.stack([2 * px + py for px, py in [chips[r] for r in ROW_RELATIONS] + [(x, y)]]).astype(jnp.int32)


def _grad_w_in_reduced(order, a, b, tok_tile, sides=()):
    seq, m = a.shape
    nb = b.shape[1] // N_DEV
    n_k = seq // tok_tile
    n_rows = 4
    last_step = n_rows * n_k - 1
    assert n_k >= 3
    fetch_at, sum_at = 1, 2
    side_refs = _SideRefs(sides, 3, 3, 8)

    def body(*refs):
        (order_ref, a_ref, b_ref, own_ref, theirs_ref, others_ref,
         acc_ref, stage_ref, sumbf_ref, pair_send, pair_recv, ici_send, ici_recv, stage_sem) = side_refs.split(refs)
        j, k = pl.program_id(0), pl.program_id(1)
        step = j * n_k + k
        side_refs.emit(step == 0, step == (n_rows * n_k) // 2, False, late=(step == (3 * n_rows * n_k) // 4))
        x, y, c = _place()
        chips = _other_chips(x, y)

        def to_sibling(row):
            return pltpu.make_async_remote_copy(
                src_ref=acc_ref.at[row % 2, 1 - c], dst_ref=theirs_ref.at[row],
                send_sem=pair_send.at[row], recv_sem=pair_recv.at[row],
                device_id=(x, y, 1 - c), device_id_type=MESH)

        def to_owner(row):
            rel = ROW_RELATIONS[row]
            px, py = chips[rel]
            return pltpu.make_async_remote_copy(
                src_ref=sumbf_ref.at[row], dst_ref=others_ref.at[rel],
                send_sem=ici_send.at[rel], recv_sem=ici_recv.at[rel],
                device_id=(px, py, c), device_id_type=MESH)

        def staged(row):
            return pltpu.make_async_copy(theirs_ref.at[row], stage_ref, stage_sem.at[0])

        @pl.when(k == 0)
        def _():
            acc_ref[j % 2] = jnp.zeros((2, m, nb), F32)

        res = _dot_tn(a_ref[...].astype(BF16), b_ref[...])
        for blk in range(2):
            acc_ref[j % 2, blk] += res[:, blk * nb:(blk + 1) * nb]

        for row in range(n_rows):
            @pl.when((j == row) & (k == n_k - 1))
            def _():
                to_sibling(row).start()

            if row < n_rows - 1:
                @pl.when((j == row + 1) & (k == fetch_at))
                def _():
                    to_sibling(row).wait_recv()
                    staged(row).start()

                @pl.when((j == row + 1) & (k == sum_at))
                def _():
                    staged(row).wait()
                    to_sibling(row).wait_send()
                    sumbf_ref[row] = (acc_ref[row % 2, c] + stage_ref[...]).astype(BF16)
                    to_owner(row).start()

        @pl.when(step == last_step)
        def _():
            row = n_rows - 1
            to_sibling(row).wait_recv()
            staged(row).start()
            staged(row).wait()
            to_sibling(row).wait_send()
            own_ref[...] = acc_ref[row % 2, c] + stage_ref[...]
            for r in range(n_rows - 1):
                to_owner(r).wait_recv()
                to_owner(r).wait_send()

        side_refs.emit(False, False, step == last_step, late=False)

    block = jax.ShapeDtypeStruct((m, nb), F32)
    outs = pl.pallas_call(
        body, name="grad_w_in",
        grid_spec=pltpu.PrefetchScalarGridSpec(
            num_scalar_prefetch=1, grid=(n_rows, n_k),
            in_specs=[pl.BlockSpec((tok_tile, m), lambda j, k, order_ref: (k, 0)),
                      pl.BlockSpec((tok_tile, 2 * nb), lambda j, k, order_ref: (k, order_ref[j]))]
            + side_refs.in_specs,
            out_specs=[pl.BlockSpec((m, nb), lambda j, k, order_ref: (0, 0)), ANY, ANY] + side_refs.out_specs,
            scratch_shapes=[pltpu.VMEM((2, 2, m, nb), F32), pltpu.VMEM((m, nb), F32),
                            pltpu.VMEM((n_rows - 1, m, nb), BF16),
                            pltpu.SemaphoreType.DMA((n_rows,)), pltpu.SemaphoreType.DMA((n_rows,)),
                            pltpu.SemaphoreType.DMA((n_rows - 1,)), pltpu.SemaphoreType.DMA((n_rows - 1,)),
                            pltpu.SemaphoreType.DMA((1,))] + side_refs.scratch),
        out_shape=[block, jax.ShapeDtypeStruct((n_rows, m, nb), F32),
                   jax.ShapeDtypeStruct((n_rows - 1, m, nb), BF16)] + side_refs.out_shapes,
        compiler_params=_params(dimension_semantics=("arbitrary", "arbitrary")),
    )(order, a, b, *side_refs.inputs)
    return outs[0], outs[2], outs[3:]


def _row_tile(rows, cols):
    tile = rows
    while tile * cols > 256 * 1024 and tile % 16 == 0:
        tile //= 2
    return tile


def _pair_sum(core, grads, theirs, name):
    _, rows, cols = theirs.shape
    rt = _row_tile(rows, cols)

    def body(core_ref, a_ref, b_ref, o_ref, ob_ref):
        total = a_ref[...] + b_ref[...]
        o_ref[...] = total
        ob_ref[...] = total.astype(BF16)

    spec = pl.BlockSpec((None, rt, cols), lambda j, i, core_ref: (j, i, 0))
    mine = pl.BlockSpec((None, None, rt, cols), lambda j, i, core_ref: (j, core_ref[0], i, 0))
    return pl.pallas_call(
        body, name=name,
        grid_spec=pltpu.PrefetchScalarGridSpec(
            num_scalar_prefetch=1, grid=(4, rows // rt), in_specs=[mine, spec], out_specs=[spec, spec]),
        out_shape=[jax.ShapeDtypeStruct(theirs.shape, F32), jax.ShapeDtypeStruct(theirs.shape, BF16)],
        compiler_params=_params(dimension_semantics=("arbitrary", "arbitrary")),
    )(core, grads.reshape(4, 2, rows, cols), theirs)


def _adamw(w, g, m, v):
    m = ADAM_B1 * m + (1.0 - ADAM_B1) * g
    v = ADAM_B2 * v + (1.0 - ADAM_B2) * (g * g)
    m_hat = m / (1.0 - ADAM_B1 ** ADAM_STEP)
    v_hat = v / (1.0 - ADAM_B2 ** ADAM_STEP)
    delta = -ADAM_LR * (m_hat / (jnp.sqrt(v_hat) + ADAM_EPS) + ADAM_WD * w)
    return delta, m, v


def _sum_adamw(chip, sums, others, w, m, v, name):
    _, rows, cols = sums.shape
    rt = _row_tile(rows, cols)

    def body(chip_ref, own_ref, oth_ref, w_ref, m_ref, v_ref, g_ref, d_ref, nm_ref, nv_ref):
        g = ((own_ref[...] + oth_ref[0].astype(F32)) + oth_ref[1].astype(F32)) + oth_ref[2].astype(F32)
        g_ref[...] = g
        d_ref[...], nm_ref[...], nv_ref[...] = _adamw(w_ref[...], g, m_ref[...], v_ref[...])

    spec = pl.BlockSpec((rt, cols), lambda i, chip_ref: (i, 0))
    own = pl.BlockSpec((None, rt, cols), lambda i, chip_ref: (chip_ref[0], i, 0))
    shape = jax.ShapeDtypeStruct((rows, cols), F32)
    return pl.pallas_call(
        body, name=name,
        grid_spec=pltpu.PrefetchScalarGridSpec(
            num_scalar_prefetch=1, grid=(rows // rt,),
            in_specs=[own, pl.BlockSpec((3, rt, cols), lambda i, chip_ref: (0, i, 0)), spec, spec, spec],
            out_specs=[spec] * 4),
        out_shape=[shape] * 4,
        compiler_params=_params(dimension_semantics=("arbitrary",)),
    )(chip, sums, others, w, m, v)


def _pair_exchange_sum(grads, name):
    n = len(grads)

    def body(*refs):
        ins, out32, outbf = refs[:n], refs[n:2 * n], refs[2 * n:3 * n]
        mine_v, theirs_v = refs[3 * n:4 * n], refs[4 * n:5 * n]
        send_sems, recv_sems, local_sems = refs[5 * n:]
        x, y, c = _place()
        remote = [pltpu.make_async_remote_copy(
            src_ref=ins[k].at[2 * j + (1 - c)], dst_ref=theirs_v[k].at[j],
            send_sem=send_sems.at[k, j], recv_sem=recv_sems.at[k, j],
            device_id=(x, y, 1 - c), device_id_type=MESH) for k in range(n) for j in range(4)]
        local = [pltpu.make_async_copy(ins[k].at[2 * j + c], mine_v[k].at[j], local_sems.at[k, j])
                 for k in range(n) for j in range(4)]
        for cp in remote + local:
            cp.start()
        for cp in local:
            cp.wait()
        for cp in remote:
            cp.wait_recv()
        for k in range(n):
            total = mine_v[k][...] + theirs_v[k][...]
            out32[k][...] = total
            outbf[k][...] = total.astype(BF16)
        for cp in remote:
            cp.wait_send()

    halves = [(4,) + g.shape[1:] for g in grads]
    outs = pl.pallas_call(
        body, name=name, in_specs=[ANY] * n, out_specs=[VMEM_FULL] * (2 * n),
        out_shape=[jax.ShapeDtypeStruct(h, F32) for h in halves] + [jax.ShapeDtypeStruct(h, BF16) for h in halves],
        scratch_shapes=[pltpu.VMEM(h, F32) for h in halves] * 2
        + [pltpu.SemaphoreType.DMA((n, 4))] * 3,
        compiler_params=_params(),
    )(*grads)
    return [(outs[k], outs[n + k]) for k in range(n)]


def _sum_adamw_small(chip, items, name):
    n = len(items)

    def body(chip_ref, *refs):
        ins, outs = refs[:5 * n], refs[5 * n:]
        for k in range(n):
            own_ref, oth_ref, w_ref, m_ref, v_ref = ins[5 * k:5 * k + 5]
            g = ((own_ref[...] + oth_ref[0].astype(F32)) + oth_ref[1].astype(F32)) + oth_ref[2].astype(F32)
            outs[4 * k][...] = g
            outs[4 * k + 1][...], outs[4 * k + 2][...], outs[4 * k + 3][...] = _adamw(
                w_ref[...], g, m_ref[...], v_ref[...])

    whole = lambda shape: pl.BlockSpec(shape, lambda i, chip_ref: (0,) * len(shape))
    in_specs, operands, out_specs, out_shape = [], [], [], []
    for sums, others, w, m, v in items:
        in_specs += [pl.BlockSpec((None,) + sums.shape[1:], lambda i, chip_ref: (chip_ref[0], 0, 0)),
                     whole(others.shape), whole(w.shape), whole(m.shape), whole(v.shape)]
        operands += [sums, others, w, m, v]
        out_specs += [whole(w.shape)] * 4
        out_shape += [jax.ShapeDtypeStruct(w.shape, F32)] * 4
    outs = pl.pallas_call(
        body, name=name,
        grid_spec=pltpu.PrefetchScalarGridSpec(num_scalar_prefetch=1, grid=(1,), in_specs=in_specs,
                                               out_specs=out_specs),
        out_shape=out_shape, compiler_params=_params(dimension_semantics=("arbitrary",)),
    )(chip, *operands)
    return [tuple(outs[4 * k:4 * k + 4]) for k in range(n)]


def _small_sum_adamw(gathered, ws, ms, vs):
    n = len(ws)

    def body(*refs):
        g8 = refs[:n + 1]
        w, m, v = refs[n + 1:2 * n + 1], refs[2 * n + 1:3 * n + 1], refs[3 * n + 1:4 * n + 1]
        outs = refs[4 * n + 1:]
        g_out, d_out, m_out, v_out = outs[:n + 1], outs[n + 1:2 * n + 1], outs[2 * n + 1:3 * n + 1], outs[3 * n + 1:]
        for k in range(n + 1):
            g = g8[k][0]
            for b in range(1, N_DEV):
                g = g + g8[k][b]
            g_out[k][...] = g
            if k < n:
                d_out[k][...], m_out[k][...], v_out[k][...] = _adamw(w[k][...], g, m[k][...], v[k][...])

    shapes = [jax.ShapeDtypeStruct(w.shape, F32) for w in ws]
    loss_shape = jax.ShapeDtypeStruct(gathered[-1].shape[1:], F32)
    outs = pl.pallas_call(
        body, name="small_sum_adamw",
        in_specs=[VMEM_FULL] * (4 * n + 1), out_specs=[VMEM_FULL] * (4 * n + 1),
        out_shape=shapes + [loss_shape] + shapes * 3,
        compiler_params=_params(),
    )(*gathered, *ws, *ms, *vs)
    return outs[:n + 1], outs[n + 1:2 * n + 1], outs[2 * n + 1:3 * n + 1], outs[3 * n + 1:]


SMALL_NAMES = ("pool_scale", "sgu_ln_g", "sgu_ln_b", "sgu_w", "sgu_b", "ln_g", "ln_b", "ple_gate_b")


TOK_TILE = 256
GRAD_TILE = 1024


def _weight_views():
    cols = lambda width: (lambda ref, b: ref.at[:, pl.ds(pl.multiple_of(b * width, 128), width)])
    rows = lambda height: (lambda ref, b: ref.at[pl.ds(pl.multiple_of(b * height, 16), height), :])
    pool_rows = POOL_GROUP // N_DEV
    return {"w_in": cols(D_IN // N_DEV),
            "pool_w": lambda ref, b: ref.at[:, pl.ds(pl.multiple_of(b * pool_rows, 16), pool_rows), :],
            "w_out": rows(D_MIX // N_DEV), "ple_w": cols(D_MODEL // N_DEV), "gate_w": rows(D_MODEL // N_DEV)}


WEIGHT_SHAPES = {"w_in": (D_MODEL, D_IN), "pool_w": (len(POOL_WINDOWS), POOL_GROUP, POOL_GROUP),
                 "w_out": (D_MIX, D_MODEL), "ple_w": (D_PLE, D_MODEL), "gate_w": (D_MODEL, D_MODEL)}


def _to_bf16(arrays):
    def body(*refs):
        for src, dst in zip(refs[:len(arrays)], refs[len(arrays):]):
            dst[...] = src[...].astype(BF16)

    return pl.pallas_call(
        body, name="cast_shards", in_specs=[VMEM_FULL] * len(arrays), out_specs=[VMEM_FULL] * len(arrays),
        out_shape=[jax.ShapeDtypeStruct(a.shape, BF16) for a in arrays], compiler_params=_params(),
    )(*arrays)


def _leading_halves(shape):
    whole_tiles = len(shape) >= 3 or shape[0] % 32 == 0
    if shape[0] % 2 or not whole_tiles:
        return None
    half = shape[0] // 2
    return lambda ref: (ref.at[pl.ds(0, half)], ref.at[pl.ds(half, half)])


def _weight_gather(shards, names):
    views = _weight_views()
    return _gather_side([shards[nm] for nm in names],
                        [jax.ShapeDtypeStruct(WEIGHT_SHAPES[nm], BF16) for nm in names], [views[nm] for nm in names],
                        [_leading_halves(shards[nm].shape) for nm in names])


def kernel(x, p, w_in, pool_w, pool_scale, sgu_ln_g, sgu_ln_b, sgu_w, sgu_b, w_out, ln_g, ln_b, ple_w, ple_gate_w, ple_gate_b, loss_target, m_w_in, m_pool_w, m_pool_scale, m_sgu_ln_g, m_sgu_ln_b, m_sgu_w, m_sgu_b, m_w_out, m_ln_g, m_ln_b, m_ple_w, m_ple_gate_w, m_ple_gate_b, v_w_in, v_pool_w, v_pool_scale, v_sgu_ln_g, v_sgu_ln_b, v_sgu_w, v_sgu_b, v_w_out, v_ln_g, v_ln_b, v_ple_w, v_ple_gate_w, v_ple_gate_b):
    seq = x.shape[1]
    x2, p2, target = x[0], p[0, 0], loss_target[0]
    core = lax.axis_index("c").astype(jnp.int32).reshape(1)
    chip = (2 * lax.axis_index("x") + lax.axis_index("y")).astype(jnp.int32).reshape(1)
    pool_rows = POOL_GROUP // N_DEV

    shard_names = ("w_in", "pool_w", "w_out", "ple_w", "gate_w")
    shards = dict(zip(shard_names, _to_bf16([w_in[0], pool_w[0], w_out[0], ple_w[0], ple_gate_w[0]])))
    w_in_f, pool_w_f = _comm_call(_weight_gather(shards, ("w_in", "pool_w")), "gather_mixer_weights")
    bias_tile = jnp.repeat(sgu_b[0].T, HEAD, axis=1)
    (h, y, pooled, xb), (w_out_f, ple_w_f, gate_w_f) = _forward_mixers(
        x2, w_in_f, pool_w_f, pool_scale, sgu_ln_g, sgu_ln_b, sgu_w[0], bias_tile, 2 * TOK_TILE,
        sides=(_weight_gather(shards, ("w_out", "ple_w", "gate_w")),))

    dy, dxr, xn, dgp, dpe, dr, loss, d_ln_g, d_ln_b, d_gate_b = _head_fwd_bwd(
        x2, y, p2, target, w_out_f, gate_w_f, ple_w_f, ln_g, ln_b, ple_gate_b, 2 * TOK_TILE, TOK_TILE)

    (d_w_out,), _ = _weight_grads([(y, dr, 1)], GRAD_TILE, "grad_w_out")
    w_out_blocks = d_w_out.reshape(N_DEV, D_MIX // N_DEV, D_MODEL)
    (d_gate_w, d_ple_w), (w_out_theirs,) = _weight_grads(
        [(xn, dgp, 1), (p2, dpe, N_DEV)], GRAD_TILE, "grad_gate_ple_w", sides=(_pair_side([w_out_blocks]),))
    gate_blocks = d_gate_w.reshape(N_DEV, D_MODEL // N_DEV, D_MODEL)

    dh, dx, d_pool_w, d_pool_scale, d_sgu_ln_g, d_sgu_ln_b, d_sgu_w, d_sgu_b = _mixers_bwd(
        h, dy, dxr, pooled, w_in_f, pool_w_f, pool_scale, sgu_ln_g, sgu_ln_b, sgu_w[0], bias_tile, TOK_TILE)

    pool_blocks = d_pool_w.reshape(N_DEV, len(POOL_WINDOWS) * pool_rows, POOL_GROUP)
    hosted_names = ("w_out", "ple_w", "gate_w", "pool_w")
    ple_sums, gate_sums, pool_sums = _pair_exchange_sum([d_ple_w, gate_blocks, pool_blocks], "pair_exchange_sum")
    hosted_sums = [_pair_sum(core, w_out_blocks, w_out_theirs, "pair_sum_w_out"), ple_sums, gate_sums, pool_sums]

    small = (d_pool_scale, d_sgu_ln_g, d_sgu_ln_b, d_sgu_w, d_sgu_b, d_ln_g, d_ln_b, d_gate_b)
    parts = small + (loss,)
    small_gather = _gather_side(parts, [jax.ShapeDtypeStruct((N_DEV,) + a.shape, F32) for a in parts],
                                [lambda ref, b: ref.at[b]] * len(parts), [_leading_halves(a.shape) for a in parts])
    cx, cy = lax.axis_index("x"), lax.axis_index("y")
    order = _row_order(cx, cy)
    w_in_own, w_in_others, side_out = _grad_w_in_reduced(
        order, xb, dh, GRAD_TILE, sides=(_chip_side([s_bf for _, s_bf in hosted_sums]), small_gather))
    hosted_others, gathered = side_out[:len(hosted_sums)], side_out[len(hosted_sums):]

    shard_of = {"w_in": (w_in, m_w_in, v_w_in), "pool_w": (pool_w, m_pool_w, v_pool_w),
                "w_out": (w_out, m_w_out, v_w_out), "ple_w": (ple_w, m_ple_w, v_ple_w),
                "gate_w": (ple_gate_w, m_ple_gate_w, v_ple_gate_w)}
    reduced = [(nm, chip, s_f32, oth) for nm, (s_f32, _), oth in zip(hosted_names, hosted_sums, hosted_others)]
    reduced.append(("w_in", jnp.zeros((1,), jnp.int32), w_in_own[None], w_in_others))
    big_out, small_items = {}, []
    for nm, which, s_f32, oth in reduced:
        w, m, v = shard_of[nm]
        two_d = s_f32.shape[1:]
        args = (s_f32, oth, w.reshape(two_d), m.reshape(two_d), v.reshape(two_d))
        if nm in ("ple_w", "pool_w", "gate_w"):
            small_items.append((nm, args))
        else:
            big_out[nm] = [r.reshape(w.shape) for r in _sum_adamw(which, *args, "adamw_" + nm)]
    for (nm, _), res in zip(small_items, _sum_adamw_small(chip, [a for _, a in small_items], "adamw_small")):
        big_out[nm] = [r.reshape(shard_of[nm][0].shape) for r in res]

    small_w = (pool_scale, sgu_ln_g, sgu_ln_b, sgu_w, sgu_b, ln_g, ln_b, ple_gate_b)
    small_m = (m_pool_scale, m_sgu_ln_g, m_sgu_ln_b, m_sgu_w, m_sgu_b, m_ln_g, m_ln_b, m_ple_gate_b)
    small_v = (v_pool_scale, v_sgu_ln_g, v_sgu_ln_b, v_sgu_w, v_sgu_b, v_ln_g, v_ln_b, v_ple_gate_b)
    natural = [[a.reshape(g.shape) for a, g in zip(group, small)] for group in (small_w, small_m, small_v)]
    res = _small_sum_adamw(list(gathered), *natural)
    g_s, d_s, m_s, v_s = [[r.reshape(w.shape) for r, w in zip(kind, small_w)] for kind in res]
    total_loss = res[0][-1][0, 0]

    order = ("w_in", "pool_w", "pool_scale", "sgu_ln_g", "sgu_ln_b", "sgu_w", "sgu_b", "w_out", "ln_g", "ln_b",
             "ple_w", "ple_gate_w", "ple_gate_b")
    outs = [total_loss, dx.reshape(1, seq, D_MODEL)]
    for kind in range(4):
        for nm in order:
            key = "gate_w" if nm == "ple_gate_w" else nm
            if key in big_out:
                outs.append(big_out[key][kind])
            else:
                outs.append((g_s, d_s, m_s, v_s)[kind][SMALL_NAMES.index(nm)])
    return tuple(outs)
```

```python
from typing import Callable, NamedTuple

import numpy as np
import jax
import jax.numpy as jnp
from jax import lax
from jax.experimental import pallas as pl
from jax.experimental.pallas import tpu as pltpu

F32 = jnp.float32
BF16 = jnp.bfloat16

N_DEV = 8
D_MODEL = 1024
D_POOL = 1024
D_SGU = 1024
D_MIX = 2048
D_IN = 5120
D_PLE = 256
POOL_WINDOWS = (2, 4, 8, 16)
POOL_GROUP = 256
N_HEADS = 4
HEAD = 256
CHUNK = 128
HALO = 16
BAND_PAD = 128
ALPHA = 2.0 ** 0.25
LN_EPS = 1e-5
ADAM_LR, ADAM_B1, ADAM_B2, ADAM_EPS, ADAM_WD, ADAM_STEP = 0.001, 0.9, 0.999, 1e-08, 0.01, 10

U0, V0, Z0 = D_POOL, D_POOL + D_SGU, D_POOL + 2 * D_SGU
VMEM_LIMIT = 56 * 1024 * 1024
MESH = pl.DeviceIdType.MESH
ANY = pl.BlockSpec(memory_space=pl.ANY)
VMEM_FULL = pl.BlockSpec(memory_space=pltpu.VMEM)

_GELU_C0 = 0.7978845608028654
_GELU_C1 = 0.044715


def _gelu_cdf(x, x2):
    return 1.0 / (1.0 + jnp.exp(x * ((-2.0 * _GELU_C0) + (-2.0 * _GELU_C0 * _GELU_C1) * x2)))


def _gelu_and_grad(x):
    x2 = x * x
    cdf = _gelu_cdf(x, x2)
    g = x * cdf
    dg = cdf + g * (1.0 - cdf) * ((2.0 * _GELU_C0) + (6.0 * _GELU_C0 * _GELU_C1) * x2)
    return g, dg


def _gelu(x):
    t = jnp.tanh(_GELU_C0 * (x + _GELU_C1 * (x * x * x)))
    return x * (0.5 * (1.0 + t))


def _split_bf16(x):
    hi = x.astype(BF16)
    return hi, (x - hi.astype(F32)).astype(BF16)


def _band(tok_tile, window, transpose):
    t = np.arange(tok_tile)[:, None]
    s = np.arange(tok_tile + BAND_PAD)[None, :]
    d = (s - t) if transpose else (t + BAND_PAD - s)
    return ((d >= 0) & (d < window)).astype(np.float32)


def _bands(tok_tile, transpose):
    return jnp.asarray(np.stack([_band(tok_tile, w, transpose) for w in POOL_WINDOWS]), dtype=BF16)


def _sigmoid(x):
    return 1.0 / (1.0 + jnp.exp(-x))


def _dot(a, b):
    return jnp.dot(a, b, preferred_element_type=F32)


def _dot_nt(a, b):
    return lax.dot_general(a, b, (((1,), (1,)), ((), ())), preferred_element_type=F32)


def _dot_tn(a, b):
    return lax.dot_general(a, b, (((0,), (0,)), ((), ())), preferred_element_type=F32)


def _row_stats(x):
    mu = jnp.mean(x, axis=-1, keepdims=True)
    xc = x - mu
    var = jnp.mean(xc * xc, axis=-1, keepdims=True)
    rstd = lax.rsqrt(var + LN_EPS)
    return xc * rstd, rstd


def _ln_bwd(dxhat, xhat, rstd):
    m1 = jnp.mean(dxhat, axis=-1, keepdims=True)
    m2 = jnp.mean(dxhat * xhat, axis=-1, keepdims=True)
    return rstd * (dxhat - m1 - xhat * m2)


def _masked_sgu_w(sw_ref, hh):
    row = lax.broadcasted_iota(jnp.int32, (CHUNK, CHUNK), 0)
    col = lax.broadcasted_iota(jnp.int32, (CHUNK, CHUNK), 1)
    return jnp.where(row >= col, sw_ref[hh], 0.0)


def _inv_count(tile_index, tok_tile, window):
    tok = tile_index * tok_tile + lax.broadcasted_iota(jnp.int32, (tok_tile, 1), 0)
    return 1.0 / jnp.minimum(tok + 1, window).astype(F32)


def _params(**kw):
    return pltpu.CompilerParams(vmem_limit_bytes=VMEM_LIMIT, **kw)


def _forward_mixers(x, w_in, pool_w, pool_scale, sgu_ln_g, sgu_ln_b, sgu_w, sgu_bias_tile, tok_tile, sides=()):
    seq = x.shape[0]
    n_tiles = seq // tok_tile
    n_chunks = tok_tile // CHUNK
    side_refs = _SideRefs(sides, 8, 4, 2)

    def body(*refs):
        (x_ref, win_ref, pw_ref, ps_ref, lg_ref, lb_ref, sw_ref, sb_ref,
         hb_ref, y_ref, pooled_ref, xb_ref, aext_ref, h_ref) = side_refs.split(refs)
        i = pl.program_id(0)
        side_refs.emit(i == 0, i == n_tiles // 2, False, late=(i == (3 * n_tiles) // 4))

        @pl.when(i == 0)
        def _():
            aext_ref[0:HALO, :] = jnp.zeros((HALO, D_POOL), F32)

        xb = x_ref[...].astype(BF16)
        xb_ref[...] = xb
        for s in range(D_IN // 1024):
            cs = slice(s * 1024, (s + 1) * 1024)
            section = _dot(xb, win_ref[:, cs])
            h_ref[:, cs] = section
            if s >= 1:
                hb_ref[:, (s - 1) * 1024:s * 1024] = section.astype(BF16)

        aext_ref[HALO:HALO + tok_tile, :] = h_ref[:, 0:D_POOL]
        for g, window in enumerate(POOL_WINDOWS):
            cols = slice(g * POOL_GROUP, (g + 1) * POOL_GROUP)
            win = aext_ref[HALO:HALO + tok_tile, cols]
            for k in range(1, window):
                win = win + aext_ref[HALO - k:HALO - k + tok_tile, cols]
            pooled = win * _inv_count(i, tok_tile, window) - h_ref[:, cols]
            pb = pooled.astype(BF16)
            pooled_ref[:, cols] = pb
            mixed = _dot(pb, pw_ref[g])
            z = h_ref[:, Z0 + g * POOL_GROUP:Z0 + (g + 1) * POOL_GROUP]
            y_ref[:, cols] = (mixed * ps_ref[:, cols] * (z * _sigmoid(z))).astype(BF16)
        aext_ref[0:HALO, :] = aext_ref[tok_tile:tok_tile + HALO, :]

        for hh in range(N_HEADS):
            cols = slice(hh * HEAD, (hh + 1) * HEAD)
            swm = _masked_sgu_w(sw_ref, hh).astype(BF16)
            for n in range(n_chunks):
                rows = slice(n * CHUNK, (n + 1) * CHUNK)
                gu = _gelu(h_ref[rows, U0 + hh * HEAD:U0 + (hh + 1) * HEAD])
                gv = _gelu(h_ref[rows, V0 + hh * HEAD:V0 + (hh + 1) * HEAD])
                xhat, _ = _row_stats(gv)
                vln = xhat * lg_ref[:, cols] + lb_ref[:, cols]
                sv = _dot(swm, vln.astype(BF16)) + sb_ref[:, cols]
                z = h_ref[rows, Z0 + D_POOL + hh * HEAD:Z0 + D_POOL + (hh + 1) * HEAD]
                y_ref[rows, D_POOL + hh * HEAD:D_POOL + (hh + 1) * HEAD] = (
                    gu * sv * (z * _sigmoid(z))).astype(BF16)

        side_refs.emit(False, False, i == n_tiles - 1, late=False)

    tok = lambda width: pl.BlockSpec((tok_tile, width), lambda i: (i, 0))
    outs = pl.pallas_call(
        body, name="forward_mixers",
        grid=(n_tiles,),
        in_specs=[tok(D_MODEL)] + [VMEM_FULL] * 7 + side_refs.in_specs,
        out_specs=[tok(D_IN - D_POOL), tok(D_MIX), tok(D_POOL), tok(D_MODEL)] + side_refs.out_specs,
        out_shape=[jax.ShapeDtypeStruct((seq, D_IN - D_POOL), BF16), jax.ShapeDtypeStruct((seq, D_MIX), BF16),
                   jax.ShapeDtypeStruct((seq, D_POOL), BF16), jax.ShapeDtypeStruct((seq, D_MODEL), BF16)]
        + side_refs.out_shapes,
        scratch_shapes=[pltpu.VMEM((HALO + tok_tile, D_POOL), F32), pltpu.VMEM((tok_tile, D_IN), F32)]
        + side_refs.scratch,
        compiler_params=_params(dimension_semantics=("arbitrary",)),
    )(x, w_in, pool_w, pool_scale, sgu_ln_g, sgu_ln_b, sgu_w, sgu_bias_tile, *side_refs.inputs)
    return outs[:4], outs[4:]


def _head_fwd_bwd(x, y, p, target, w_out, gate_w, ple_w, ln_g, ln_b, gate_b, tok_tile, sub_tile):
    seq = x.shape[0]

    def body(x_ref, y_ref, p_ref, t_ref, wout_ref, gw_ref, plw_ref, lng_ref, lnb_ref, gb_ref,
             dy_ref, dxr_ref, xn_ref, dgp_ref, dpe_ref, dr_ref, loss_ref, dlng_ref, dlnb_ref, dgb_ref):
        i = pl.program_id(0)

        @pl.when(i == 0)
        def _():
            loss_ref[...] = jnp.zeros_like(loss_ref)
            dlng_ref[...] = jnp.zeros_like(dlng_ref)
            dlnb_ref[...] = jnp.zeros_like(dlnb_ref)
            dgb_ref[...] = jnp.zeros_like(dgb_ref)

        subs = [slice(s * sub_tile, (s + 1) * sub_tile) for s in range(tok_tile // sub_tile)]
        stats, xns, douts = [], [], []
        for rows in subs:
            r = ALPHA * x_ref[rows, :] + _dot(y_ref[rows, :], wout_ref[...])
            xhat, rstd = _row_stats(r)
            xn = xhat * lng_ref[...] + lnb_ref[...]
            xn_ref[rows, :] = xn.astype(BF16)
            stats.append((xhat, rstd))
            xns.append(xn)
        loss = jnp.zeros((1, 1), F32)
        dgb = jnp.zeros((1, D_MODEL), F32)
        for rows, xn in zip(subs, xns):
            gate = _sigmoid(_dot(xn_ref[rows, :], gw_ref[...]) + gb_ref[...])
            pe = _dot(p_ref[rows, :].astype(BF16), plw_ref[...])
            err = xn + gate * pe - t_ref[rows, :]
            loss = loss + jnp.sum(err * err, keepdims=True)
            dout = err * (1.0 / D_MODEL)
            dpe_ref[rows, :] = (dout * gate).astype(BF16)
            dgpre = dout * pe * gate * (1.0 - gate)
            dgb = dgb + jnp.sum(dgpre, axis=0, keepdims=True)
            dgp_ref[rows, :] = dgpre.astype(BF16)
            douts.append(dout)
        loss_ref[...] += (0.5 / D_MODEL) * loss
        dgb_ref[...] += dgb
        dlng = jnp.zeros((1, D_MODEL), F32)
        dlnb = jnp.zeros((1, D_MODEL), F32)
        for rows, (xhat, rstd), dout in zip(subs, stats, douts):
            dxn = dout + _dot_nt(dgp_ref[rows, :], gw_ref[...])
            dlng = dlng + jnp.sum(dxn * xhat, axis=0, keepdims=True)
            dlnb = dlnb + jnp.sum(dxn, axis=0, keepdims=True)
            dr = _ln_bwd(dxn * lng_ref[...], xhat, rstd)
            dxr_ref[rows, :] = ALPHA * dr
            dr_ref[rows, :] = dr.astype(BF16)
        dlng_ref[...] += dlng
        dlnb_ref[...] += dlnb
        for rows in subs:
            dy_ref[rows, :] = _dot_nt(dr_ref[rows, :], wout_ref[...])

    tok = lambda width: pl.BlockSpec((tok_tile, width), lambda i: (i, 0))
    acc = lambda width: pl.BlockSpec((1, width), lambda i: (0, 0))
    vec = jax.ShapeDtypeStruct((1, D_MODEL), F32)
    return pl.pallas_call(
        body, name="head_fwd_bwd",
        grid=(seq // tok_tile,),
        in_specs=[tok(D_MODEL), tok(D_MIX), tok(D_PLE), tok(D_MODEL),
                  VMEM_FULL, VMEM_FULL, VMEM_FULL, VMEM_FULL, VMEM_FULL, VMEM_FULL],
        out_specs=[tok(D_MIX), tok(D_MODEL), tok(D_MODEL), tok(D_MODEL), tok(D_MODEL), tok(D_MODEL),
                   acc(128), acc(D_MODEL), acc(D_MODEL), acc(D_MODEL)],
        out_shape=[jax.ShapeDtypeStruct((seq, D_MIX), F32), jax.ShapeDtypeStruct((seq, D_MODEL), F32),
                   jax.ShapeDtypeStruct((seq, D_MODEL), BF16), jax.ShapeDtypeStruct((seq, D_MODEL), BF16),
                   jax.ShapeDtypeStruct((seq, D_MODEL), BF16), jax.ShapeDtypeStruct((seq, D_MODEL), BF16),
                   jax.ShapeDtypeStruct((1, 128), F32), vec, vec, vec],
        compiler_params=_params(dimension_semantics=("arbitrary",)),
    )(x, y, p, target, w_out, gate_w, ple_w, ln_g, ln_b, gate_b)


def _mixers_bwd(h, dy, dxr, pooled, w_in, pool_w, pool_scale, sgu_ln_g, sgu_ln_b, sgu_w, sgu_bias_tile, tok_tile):
    seq = h.shape[0]
    n_tiles = seq // tok_tile
    n_chunks = tok_tile // CHUNK
    pool_rows = POOL_GROUP // N_DEV

    def body(h_ref, dy_ref, dxr_ref, pooled_ref, win_ref, pw_ref, ps_ref, lg_ref, lb_ref, sw_ref, sb_ref, band_ref,
             dh_ref, dx_ref, dpw_ref, dps_ref, dlg_ref, dlb_ref, dsw_ref, dsb_ref,
             qhi_ref, qlo_ref, dpw_acc, dsb_acc):
        i = pl.program_id(0)
        tile = n_tiles - 1 - i

        @pl.when(i == 0)
        def _():
            qhi_ref[...] = jnp.zeros_like(qhi_ref)
            qlo_ref[...] = jnp.zeros_like(qlo_ref)
            dpw_acc[...] = jnp.zeros_like(dpw_acc)
            dsb_acc[...] = jnp.zeros_like(dsb_acc)
            dps_ref[...] = jnp.zeros_like(dps_ref)
            dlg_ref[...] = jnp.zeros_like(dlg_ref)
            dlb_ref[...] = jnp.zeros_like(dlb_ref)
            dsw_ref[...] = jnp.zeros_like(dsw_ref)

        def h_at(rows, cols):
            return h_ref[rows, cols.start - D_POOL:cols.stop - D_POOL].astype(F32)

        everything = slice(0, tok_tile)
        for part in (qhi_ref, qlo_ref):
            part[tok_tile:tok_tile + HALO, :] = part[0:HALO, :]
        for g, window in enumerate(POOL_WINDOWS):
            cols = slice(g * POOL_GROUP, (g + 1) * POOL_GROUP)
            zcols = slice(Z0 + g * POOL_GROUP, Z0 + (g + 1) * POOL_GROUP)
            z = h_at(everything, zcols)
            sz = _sigmoid(z)
            pb = pooled_ref[:, cols]
            mixed = _dot(pb, pw_ref[g])
            dyp = dy_ref[:, cols]
            dh_ref[:, zcols] = (dyp * (mixed * ps_ref[:, cols]) * (sz * (1.0 + z * (1.0 - sz)))).astype(BF16)
            dms = dyp * (z * sz)
            dps_ref[:, cols] += jnp.sum(dms * mixed, axis=0, keepdims=True)
            dmixed = (dms * ps_ref[:, cols]).astype(BF16)
            dpw_acc[g] += _dot_tn(pb, dmixed)
            dpooled = _dot_nt(dmixed, pw_ref[g])
            qhi_ref[everything, cols], qlo_ref[everything, cols] = _split_bf16(
                dpooled * _inv_count(tile, tok_tile, window))
            da = _dot(band_ref[g], qhi_ref[:, cols]) + _dot(band_ref[g], qlo_ref[:, cols]) - dpooled
            dh_ref[:, cols] = da.astype(BF16)

        pool_cols = [slice(o + g * POOL_GROUP, o + (g + 1) * POOL_GROUP)
                     for o in (0, Z0) for g in range(len(POOL_WINDOWS))]
        pool_slices = [pool_cols[0:3], pool_cols[3:6], pool_cols[6:8], []]
        for hh in range(N_HEADS):
            cols = slice(hh * HEAD, (hh + 1) * HEAD)
            ucols = slice(U0 + hh * HEAD, U0 + (hh + 1) * HEAD)
            vcols = slice(V0 + hh * HEAD, V0 + (hh + 1) * HEAD)
            zcols = slice(Z0 + D_POOL + hh * HEAD, Z0 + D_POOL + (hh + 1) * HEAD)
            sw32 = _masked_sgu_w(sw_ref, hh)
            swm = sw32.astype(BF16)
            swm_t = sw32.T.astype(BF16)
            for n in range(n_chunks):
                rows = slice(n * CHUNK, (n + 1) * CHUNK)
                gu, dgu_du = _gelu_and_grad(h_at(rows, ucols))
                gv, dgv_dv = _gelu_and_grad(h_at(rows, vcols))
                xhat, rstd = _row_stats(gv)
                vb = (xhat * lg_ref[:, cols] + lb_ref[:, cols]).astype(BF16)
                sv = _dot(swm, vb) + sb_ref[:, cols]
                z = h_at(rows, zcols)
                sz = _sigmoid(z)
                dys = dy_ref[rows, D_POOL + hh * HEAD:D_POOL + (hh + 1) * HEAD]
                dh_ref[rows, zcols] = (dys * (gu * sv) * (sz * (1.0 + z * (1.0 - sz)))).astype(BF16)
                dyg = dys * (z * sz)
                dh_ref[rows, ucols] = (dyg * sv * dgu_du).astype(BF16)
                dsv = dyg * gu
                dsb_acc[:, cols] += dsv
                dsvb = dsv.astype(BF16)
                dsw_ref[hh] += _dot_nt(dsvb, vb)
                dvln = _dot(swm_t, dsvb)
                dlg_ref[:, cols] += jnp.sum(dvln * xhat, axis=0, keepdims=True)
                dlb_ref[:, cols] += jnp.sum(dvln, axis=0, keepdims=True)
                dgv = _ln_bwd(dvln * lg_ref[:, cols], xhat, rstd)
                dh_ref[rows, vcols] = (dgv * dgv_dv).astype(BF16)
            ready = [ucols, vcols, zcols] + pool_slices[hh]
            part = _dot_nt(dh_ref[:, ready[0]], win_ref[:, ready[0]])
            for sl in ready[1:]:
                part = part + _dot_nt(dh_ref[:, sl], win_ref[:, sl])
            if hh == 0:
                dx_ref[...] = dxr_ref[...] + part
            else:
                dx_ref[...] += part

        @pl.when(i == n_tiles - 1)
        def _():
            for g in range(len(POOL_WINDOWS)):
                for b in range(N_DEV):
                    dpw_ref[b, g] = dpw_acc[g, b * pool_rows:(b + 1) * pool_rows, :]
            row = lax.broadcasted_iota(jnp.int32, (CHUNK, CHUNK), 0)
            col = lax.broadcasted_iota(jnp.int32, (CHUNK, CHUNK), 1)
            for hh in range(N_HEADS):
                dsw_ref[hh] = jnp.where(row >= col, dsw_ref[hh], 0.0)
                total = jnp.sum(dsb_acc[:, hh * HEAD:(hh + 1) * HEAD], axis=1, keepdims=True)
                dsb_ref[hh:hh + 1, :] = jnp.broadcast_to(total, (CHUNK, CHUNK)).T[0:1, :]

    tok = lambda width: pl.BlockSpec((tok_tile, width), lambda i: (n_tiles - 1 - i, 0))
    whole = lambda shape: pl.BlockSpec(shape, lambda i: (0,) * len(shape))
    vec = jax.ShapeDtypeStruct((1, D_MODEL), F32)
    return pl.pallas_call(
        body, name="mixers_bwd",
        grid=(n_tiles,),
        in_specs=[tok(D_IN - D_POOL), tok(D_MIX), tok(D_MODEL), tok(D_POOL)] + [VMEM_FULL] * 8,
        out_specs=[tok(D_IN), tok(D_MODEL), whole((N_DEV, len(POOL_WINDOWS), pool_rows, POOL_GROUP)),
                   whole((1, D_POOL)), whole((1, D_SGU)), whole((1, D_SGU)),
                   whole((N_HEADS, CHUNK, CHUNK)), whole((N_HEADS, CHUNK))],
        out_shape=[jax.ShapeDtypeStruct((seq, D_IN), BF16), jax.ShapeDtypeStruct((seq, D_MODEL), F32),
                   jax.ShapeDtypeStruct((N_DEV, len(POOL_WINDOWS), pool_rows, POOL_GROUP), F32),
                   vec, vec, vec,
                   jax.ShapeDtypeStruct((N_HEADS, CHUNK, CHUNK), F32),
                   jax.ShapeDtypeStruct((N_HEADS, CHUNK), F32)],
        scratch_shapes=[pltpu.VMEM((tok_tile + BAND_PAD, D_POOL), BF16),
                        pltpu.VMEM((tok_tile + BAND_PAD, D_POOL), BF16),
                        pltpu.VMEM((len(POOL_WINDOWS), POOL_GROUP, POOL_GROUP), F32),
                        pltpu.VMEM((CHUNK, D_SGU), F32)],
        compiler_params=_params(dimension_semantics=("arbitrary",)),
    )(h, dy, dxr, pooled, w_in, pool_w, pool_scale, sgu_ln_g, sgu_ln_b, sgu_w, sgu_bias_tile, _bands(tok_tile, True))


def _weight_grads(items, tok_tile, name, sides=()):
    n_items = len(items)
    seq = items[0][0].shape[0]
    n_k = seq // tok_tile
    side_refs = _SideRefs(sides, 2 * n_items, n_items, 0)

    def body(*refs):
        refs = side_refs.split(refs)
        k = pl.program_id(0)
        side_refs.emit(k == 0, k == n_k // 2, False)
        for t, (_, b, n_col_blocks) in enumerate(items):
            a_ref, b_ref, out_ref = refs[2 * t], refs[2 * t + 1], refs[2 * n_items + t]
            nb = b.shape[1] // n_col_blocks

            @pl.when(k == 0)
            def _():
                out_ref[...] = jnp.zeros_like(out_ref)

            res = _dot_tn(a_ref[...].astype(BF16), b_ref[...])
            for blk in range(n_col_blocks):
                out_ref[blk] += res[:, blk * nb:(blk + 1) * nb]
        side_refs.emit(False, False, k == n_k - 1)

    in_specs, operands, out_specs, out_shape = [], [], [], []
    for a, b, n_col_blocks in items:
        shape = (n_col_blocks, a.shape[1], b.shape[1] // n_col_blocks)
        in_specs += [pl.BlockSpec((tok_tile, a.shape[1]), lambda k: (k, 0)),
                     pl.BlockSpec((tok_tile, b.shape[1]), lambda k: (k, 0))]
        operands += [a, b]
        out_specs.append(pl.BlockSpec(shape, lambda k: (0, 0, 0)))
        out_shape.append(jax.ShapeDtypeStruct(shape, F32))
    outs = pl.pallas_call(
        body, name=name, grid=(n_k,),
        in_specs=in_specs + side_refs.in_specs, out_specs=out_specs + side_refs.out_specs,
        out_shape=out_shape + side_refs.out_shapes, scratch_shapes=side_refs.scratch,
        compiler_params=_params(dimension_semantics=("arbitrary",)),
    )(*operands, *side_refs.inputs)
    return outs[:n_items], outs[n_items:]


class _Side(NamedTuple):
    inputs: list
    out_shapes: list
    sem_shapes: list
    emit: Callable


def _when(cond):
    if cond is True:
        return lambda f: f()
    if cond is False:
        return lambda f: None
    return pl.when(cond)


def _place():
    return lax.axis_index("x"), lax.axis_index("y"), lax.axis_index("c")


def _other_chips(x, y):
    return [(1 - x, y), (x, 1 - y), (1 - x, 1 - y)]


def _gather_side(shards, out_shapes, views, halves):
    n = len(shards)
    n_sems = 10

    def emit(ins, outs, sems, first, mid, last, late=None):
        send_sems, recv_sems, local_sems = sems
        x, y, c = _place()
        here, x_nbr, y_nbr, diag = (x, y), (1 - x, y), (x, 1 - y), (1 - x, 1 - y)
        sibling = (x, y, 1 - c)

        def block(k, chip, core):
            return views[k](outs[k], 4 * chip[0] + 2 * chip[1] + core)

        def piece(k, ref, p):
            return ref if halves[k] is None else halves[k](ref)[p]

        def n_pieces(k):
            return 1 if halves[k] is None else 2

        def copy(k, s, src, dst, to):
            return pltpu.make_async_remote_copy(
                src_ref=src, dst_ref=dst, send_sem=send_sems.at[k, s], recv_sem=recv_sems.at[k, s],
                device_id=to, device_id_type=MESH)

        def outgoing(k, s):
            mine = block(k, here, c)
            if s == 0:
                return copy(k, 0, ins[k], mine, sibling)
            if s in (1, 2):
                return copy(k, s, piece(k, ins[k], s - 1), piece(k, mine, s - 1), (*x_nbr, c))
            if s in (3, 4):
                return copy(k, s, piece(k, ins[k], s - 3), piece(k, mine, s - 3), (*y_nbr, c))
            if s == 5:
                part = piece(k, block(k, x_nbr, c), 0)
                return copy(k, 5, part, part, (*y_nbr, c))
            if s == 6:
                part = piece(k, block(k, y_nbr, c), 1)
                return copy(k, 6, part, part, (*x_nbr, c))
            whole = block(k, (x_nbr, y_nbr, diag)[s - 7], c)
            return copy(k, s, whole, whole, sibling)

        def incoming(k, s):
            if s == 0:
                zone = block(k, here, 1 - c)
            elif s in (1, 2):
                zone = piece(k, block(k, x_nbr, c), s - 1)
            elif s in (3, 4):
                zone = piece(k, block(k, y_nbr, c), s - 3)
            elif s in (5, 6):
                zone = piece(k, block(k, diag, c), s - 5)
            else:
                zone = block(k, (x_nbr, y_nbr, diag)[s - 7], 1 - c)
            return copy(k, s, zone, zone, sibling)

        def own(k):
            return pltpu.make_async_copy(ins[k], block(k, here, c), local_sems.at[k])

        def used(k):
            return list(range(n_sems)) if n_pieces(k) == 2 else [0, 1, 3, 5, 7, 8, 9]

        @_when(first)
        def _():
            for k in range(n):
                own(k).start()
                for s in (0, 1, 4, 2, 3):
                    if s in used(k):
                        outgoing(k, s).start()

        @_when(mid)
        def _():
            for k in range(n):
                incoming(k, 1).wait_recv()
                outgoing(k, 5).start()
                if n_pieces(k) == 2:
                    incoming(k, 4).wait_recv()
                    outgoing(k, 6).start()
                    incoming(k, 2).wait_recv()
                outgoing(k, 7).start()
                incoming(k, 3).wait_recv()
                outgoing(k, 8).start()

        @_when(last if late is None else late)
        def _():
            for k in range(n):
                incoming(k, 5).wait_recv()
                if n_pieces(k) == 2:
                    incoming(k, 6).wait_recv()
                outgoing(k, 9).start()

        @_when(last)
        def _():
            for k in range(n):
                for s in (0, 7, 8, 9):
                    incoming(k, s).wait_recv()
            for k in range(n):
                for s in used(k):
                    outgoing(k, s).wait_send()
                own(k).wait()

    sems = [pltpu.SemaphoreType.DMA((n, n_sems)), pltpu.SemaphoreType.DMA((n, n_sems)),
            pltpu.SemaphoreType.DMA((n,))]
    return _Side(list(shards), list(out_shapes), sems, emit)


def _pair_side(grads):
    n = len(grads)

    def emit(ins, theirs, sems, first, mid, last, late=None):
        send_sems, recv_sems = sems
        x, y, c = _place()

        def copies():
            return [pltpu.make_async_remote_copy(
                src_ref=ins[k].at[2 * j + (1 - c)], dst_ref=theirs[k].at[j],
                send_sem=send_sems.at[k, j], recv_sem=recv_sems.at[k, j],
                device_id=(x, y, 1 - c), device_id_type=MESH) for k in range(n) for j in range(4)]

        @_when(first)
        def _():
            for cp in copies():
                cp.start()

        @_when(last)
        def _():
            for cp in copies():
                cp.wait_recv()
            for cp in copies():
                cp.wait_send()

    shapes = [jax.ShapeDtypeStruct((4,) + g.shape[1:], g.dtype) for g in grads]
    return _Side(list(grads), shapes, [pltpu.SemaphoreType.DMA((n, 4)), pltpu.SemaphoreType.DMA((n, 4))], emit)


def _chip_side(sums):
    n = len(sums)

    def emit(ins, others, sems, first, mid, last, late=None):
        send_sems, recv_sems = sems
        x, y, c = _place()

        def copies():
            return [pltpu.make_async_remote_copy(
                src_ref=ins[k].at[2 * px + py], dst_ref=others[k].at[r],
                send_sem=send_sems.at[k, r], recv_sem=recv_sems.at[k, r],
                device_id=(px, py, c), device_id_type=MESH)
                for k in range(n) for r, (px, py) in enumerate(_other_chips(x, y))]

        @_when(first)
        def _():
            for cp in copies():
                cp.start()

        @_when(last)
        def _():
            for cp in copies():
                cp.wait_recv()
            for cp in copies():
                cp.wait_send()

    shapes = [jax.ShapeDtypeStruct((3,) + s.shape[1:], s.dtype) for s in sums]
    return _Side(list(sums), shapes, [pltpu.SemaphoreType.DMA((n, 3)), pltpu.SemaphoreType.DMA((n, 3))], emit)


def _comm_call(side, name):
    n_in, n_out = len(side.inputs), len(side.out_shapes)

    def body(*refs):
        side.emit(refs[:n_in], refs[n_in:n_in + n_out], refs[n_in + n_out:], True, True, True)

    return pl.pallas_call(
        body, name=name, in_specs=[ANY] * n_in, out_specs=[ANY] * n_out,
        out_shape=side.out_shapes, scratch_shapes=side.sem_shapes,
    )(*side.inputs)


class _SideRefs:
    def __init__(self, sides, n_in, n_out, n_scratch):
        self.sides, self.n_in, self.n_out, self.n_scratch = sides, n_in, n_out, n_scratch
        self.inputs = [a for s in sides for a in s.inputs]
        self.out_shapes = [o for s in sides for o in s.out_shapes]
        self.scratch = [m for s in sides for m in s.sem_shapes]
        self.in_specs = [ANY] * len(self.inputs)
        self.out_specs = [ANY] * len(self.out_shapes)

    def split(self, refs):
        refs = list(refs)
        n_side_in, n_side_out = len(self.inputs), len(self.out_shapes)
        ins, rest = refs[:self.n_in], refs[self.n_in:]
        side_in, rest = rest[:n_side_in], rest[n_side_in:]
        outs, rest = rest[:self.n_out], rest[self.n_out:]
        side_out, rest = rest[:n_side_out], rest[n_side_out:]
        scratch, side_sems = rest[:self.n_scratch], rest[self.n_scratch:]
        self._refs = (side_in, side_out, side_sems)
        return ins + outs + scratch

    def emit(self, first, mid, last, late=None):
        side_in, side_out, side_sems = self._refs
        for s in self.sides:
            a, b, m = len(s.inputs), len(s.out_shapes), len(s.sem_shapes)
            s.emit(side_in[:a], side_out[:b], side_sems[:m], first, mid, last, late)
            side_in, side_out, side_sems = side_in[a:], side_out[b:], side_sems[m:]


ROW_RELATIONS = (0, 1, 2)


def _row_order(x, y):
    chips = _other_chips(x, y)
    return jnp.stack([2 * px + py for px, py in [chips[r] for r in ROW_RELATIONS] + [(x, y)]]).astype(jnp.int32)


def _grad_w_in_reduced(order, a, b, tok_tile, sides=()):
    seq, m = a.shape
    nb = b.shape[1] // N_DEV
    n_k = seq // tok_tile
    n_rows = 4
    last_step = n_rows * n_k - 1
    assert n_k >= 3
    fetch_at, sum_at = 1, 2
    side_refs = _SideRefs(sides, 3, 3, 8)

    def body(*refs):
        (order_ref, a_ref, b_ref, own_ref, theirs_ref, others_ref,
         acc_ref, stage_ref, sumbf_ref, pair_send, pair_recv, ici_send, ici_recv, stage_sem) = side_refs.split(refs)
        j, k = pl.program_id(0), pl.program_id(1)
        step = j * n_k + k
        side_refs.emit(step == 0, step == (n_rows * n_k) // 2, False, late=(step == (3 * n_rows * n_k) // 4))
        x, y, c = _place()
        chips = _other_chips(x, y)

        def to_sibling(row):
            return pltpu.make_async_remote_copy(
                src_ref=acc_ref.at[row % 2, 1 - c], dst_ref=theirs_ref.at[row],
                send_sem=pair_send.at[row], recv_sem=pair_recv.at[row],
                device_id=(x, y, 1 - c), device_id_type=MESH)

        def to_owner(row):
            rel = ROW_RELATIONS[row]
            px, py = chips[rel]
            return pltpu.make_async_remote_copy(
                src_ref=sumbf_ref.at[row], dst_ref=others_ref.at[rel],
                send_sem=ici_send.at[rel], recv_sem=ici_recv.at[rel],
                device_id=(px, py, c), device_id_type=MESH)

        def staged(row):
            return pltpu.make_async_copy(theirs_ref.at[row], stage_ref, stage_sem.at[0])

        @pl.when(k == 0)
        def _():
            acc_ref[j % 2] = jnp.zeros((2, m, nb), F32)

        res = _dot_tn(a_ref[...].astype(BF16), b_ref[...])
        for blk in range(2):
            acc_ref[j % 2, blk] += res[:, blk * nb:(blk + 1) * nb]

        for row in range(n_rows):
            @pl.when((j == row) & (k == n_k - 1))
            def _():
                to_sibling(row).start()

            if row < n_rows - 1:
                @pl.when((j == row + 1) & (k == fetch_at))
                def _():
                    to_sibling(row).wait_recv()
                    staged(row).start()

                @pl.when((j == row + 1) & (k == sum_at))
                def _():
                    staged(row).wait()
                    to_sibling(row).wait_send()
                    sumbf_ref[row] = (acc_ref[row % 2, c] + stage_ref[...]).astype(BF16)
                    to_owner(row).start()

        @pl.when(step == last_step)
        def _():
            row = n_rows - 1
            to_sibling(row).wait_recv()
            staged(row).start()
            staged(row).wait()
            to_sibling(row).wait_send()
            own_ref[...] = acc_ref[row % 2, c] + stage_ref[...]
            for r in range(n_rows - 1):
                to_owner(r).wait_recv()
                to_owner(r).wait_send()

        side_refs.emit(False, False, step == last_step, late=False)

    block = jax.ShapeDtypeStruct((m, nb), F32)
    outs = pl.pallas_call(
        body, name="grad_w_in",
        grid_spec=pltpu.PrefetchScalarGridSpec(
            num_scalar_prefetch=1, grid=(n_rows, n_k),
            in_specs=[pl.BlockSpec((tok_tile, m), lambda j, k, order_ref: (k, 0)),
                      pl.BlockSpec((tok_tile, 2 * nb), lambda j, k, order_ref: (k, order_ref[j]))]
            + side_refs.in_specs,
            out_specs=[pl.BlockSpec((m, nb), lambda j, k, order_ref: (0, 0)), ANY, ANY] + side_refs.out_specs,
            scratch_shapes=[pltpu.VMEM((2, 2, m, nb), F32), pltpu.VMEM((m, nb), F32),
                            pltpu.VMEM((n_rows - 1, m, nb), BF16),
                            pltpu.SemaphoreType.DMA((n_rows,)), pltpu.SemaphoreType.DMA((n_rows,)),
                            pltpu.SemaphoreType.DMA((n_rows - 1,)), pltpu.SemaphoreType.DMA((n_rows - 1,)),
                            pltpu.SemaphoreType.DMA((1,))] + side_refs.scratch),
        out_shape=[block, jax.ShapeDtypeStruct((n_rows, m, nb), F32),
                   jax.ShapeDtypeStruct((n_rows - 1, m, nb), BF16)] + side_refs.out_shapes,
        compiler_params=_params(dimension_semantics=("arbitrary", "arbitrary")),
    )(order, a, b, *side_refs.inputs)
    return outs[0], outs[2], outs[3:]


def _row_tile(rows, cols):
    tile = rows
    while tile * cols > 256 * 1024 and tile % 16 == 0:
        tile //= 2
    return tile


def _pair_sum(core, grads, theirs, name):
    _, rows, cols = theirs.shape
    rt = _row_tile(rows, cols)

    def body(core_ref, a_ref, b_ref, o_ref, ob_ref):
        total = a_ref[...] + b_ref[...]
        o_ref[...] = total
        ob_ref[...] = total.astype(BF16)

    spec = pl.BlockSpec((None, rt, cols), lambda j, i, core_ref: (j, i, 0))
    mine = pl.BlockSpec((None, None, rt, cols), lambda j, i, core_ref: (j, core_ref[0], i, 0))
    return pl.pallas_call(
        body, name=name,
        grid_spec=pltpu.PrefetchScalarGridSpec(
            num_scalar_prefetch=1, grid=(4, rows // rt), in_specs=[mine, spec], out_specs=[spec, spec]),
        out_shape=[jax.ShapeDtypeStruct(theirs.shape, F32), jax.ShapeDtypeStruct(theirs.shape, BF16)],
        compiler_params=_params(dimension_semantics=("arbitrary", "arbitrary")),
    )(core, grads.reshape(4, 2, rows, cols), theirs)


def _adamw(w, g, m, v):
    m = ADAM_B1 * m + (1.0 - ADAM_B1) * g
    v = ADAM_B2 * v + (1.0 - ADAM_B2) * (g * g)
    m_hat = m / (1.0 - ADAM_B1 ** ADAM_STEP)
    v_hat = v / (1.0 - ADAM_B2 ** ADAM_STEP)
    delta = -ADAM_LR * (m_hat / (jnp.sqrt(v_hat) + ADAM_EPS) + ADAM_WD * w)
    return delta, m, v


def _sum_adamw(chip, sums, others, w, m, v, name):
    _, rows, cols = sums.shape
    rt = _row_tile(rows, cols)

    def body(chip_ref, own_ref, oth_ref, w_ref, m_ref, v_ref, g_ref, d_ref, nm_ref, nv_ref):
        g = ((own_ref[...] + oth_ref[0].astype(F32)) + oth_ref[1].astype(F32)) + oth_ref[2].astype(F32)
        g_ref[...] = g
        d_ref[...], nm_ref[...], nv_ref[...] = _adamw(w_ref[...], g, m_ref[...], v_ref[...])

    spec = pl.BlockSpec((rt, cols), lambda i, chip_ref: (i, 0))
    own = pl.BlockSpec((None, rt, cols), lambda i, chip_ref: (chip_ref[0], i, 0))
    shape = jax.ShapeDtypeStruct((rows, cols), F32)
    return pl.pallas_call(
        body, name=name,
        grid_spec=pltpu.PrefetchScalarGridSpec(
            num_scalar_prefetch=1, grid=(rows // rt,),
            in_specs=[own, pl.BlockSpec((3, rt, cols), lambda i, chip_ref: (0, i, 0)), spec, spec, spec],
            out_specs=[spec] * 4),
        out_shape=[shape] * 4,
        compiler_params=_params(dimension_semantics=("arbitrary",)),
    )(chip, sums, others, w, m, v)


def _pair_exchange_sum(grads, name):
    n = len(grads)

    def body(*refs):
        ins, out32, outbf = refs[:n], refs[n:2 * n], refs[2 * n:3 * n]
        mine_v, theirs_v = refs[3 * n:4 * n], refs[4 * n:5 * n]
        send_sems, recv_sems, local_sems = refs[5 * n:]
        x, y, c = _place()
        remote = [pltpu.make_async_remote_copy(
            src_ref=ins[k].at[2 * j + (1 - c)], dst_ref=theirs_v[k].at[j],
            send_sem=send_sems.at[k, j], recv_sem=recv_sems.at[k, j],
            device_id=(x, y, 1 - c), device_id_type=MESH) for k in range(n) for j in range(4)]
        local = [pltpu.make_async_copy(ins[k].at[2 * j + c], mine_v[k].at[j], local_sems.at[k, j])
                 for k in range(n) for j in range(4)]
        for cp in remote + local:
            cp.start()
        for cp in local:
            cp.wait()
        for cp in remote:
            cp.wait_recv()
        for k in range(n):
            total = mine_v[k][...] + theirs_v[k][...]
            out32[k][...] = total
            outbf[k][...] = total.astype(BF16)
        for cp in remote:
            cp.wait_send()

    halves = [(4,) + g.shape[1:] for g in grads]
    outs = pl.pallas_call(
        body, name=name, in_specs=[ANY] * n, out_specs=[VMEM_FULL] * (2 * n),
        out_shape=[jax.ShapeDtypeStruct(h, F32) for h in halves] + [jax.ShapeDtypeStruct(h, BF16) for h in halves],
        scratch_shapes=[pltpu.VMEM(h, F32) for h in halves] * 2
        + [pltpu.SemaphoreType.DMA((n, 4))] * 3,
        compiler_params=_params(),
    )(*grads)
    return [(outs[k], outs[n + k]) for k in range(n)]


def _final_adamw(chip, items, gathered, ws, ms, vs):
    n_items, n = len(items), len(ws)

    def body(chip_ref, *refs):
        refs = list(refs)
        ins, refs = refs[:5 * n_items], refs[5 * n_items:]
        g8, refs = refs[:n + 1], refs[n + 1:]
        w, m, v, refs = refs[:n], refs[n:2 * n], refs[2 * n:3 * n], refs[3 * n:]
        outs, refs = refs[:4 * n_items], refs[4 * n_items:]
        g_out, d_out, m_out, v_out = refs[:n + 1], refs[n + 1:2 * n + 1], refs[2 * n + 1:3 * n + 1], refs[3 * n + 1:]
        for k in range(n_items):
            own_ref, oth_ref, w_ref, m_ref, v_ref = ins[5 * k:5 * k + 5]
            g = ((own_ref[...] + oth_ref[0].astype(F32)) + oth_ref[1].astype(F32)) + oth_ref[2].astype(F32)
            outs[4 * k][...] = g
            outs[4 * k + 1][...], outs[4 * k + 2][...], outs[4 * k + 3][...] = _adamw(
                w_ref[...], g, m_ref[...], v_ref[...])
        for k in range(n + 1):
            g = g8[k][0]
            for b in range(1, N_DEV):
                g = g + g8[k][b]
            g_out[k][...] = g
            if k < n:
                d_out[k][...], m_out[k][...], v_out[k][...] = _adamw(w[k][...], g, m[k][...], v[k][...])

    whole = lambda a: pl.BlockSpec(a.shape, lambda i, chip_ref: (0,) * len(a.shape))
    in_specs, operands, out_specs, out_shape = [], [], [], []
    for sums, others, w, m, v in items:
        in_specs += [pl.BlockSpec((None,) + sums.shape[1:], lambda i, chip_ref: (chip_ref[0], 0, 0)),
                     whole(others), whole(w), whole(m), whole(v)]
        operands += [sums, others, w, m, v]
        out_specs += [whole(w)] * 4
        out_shape += [jax.ShapeDtypeStruct(w.shape, F32)] * 4
    small_in = list(gathered) + list(ws) + list(ms) + list(vs)
    small_out = ([jax.ShapeDtypeStruct(w.shape, F32) for w in ws]
                 + [jax.ShapeDtypeStruct(gathered[-1].shape[1:], F32)]
                 + [jax.ShapeDtypeStruct(w.shape, F32) for w in ws] * 3)
    outs = pl.pallas_call(
        body, name="final_adamw",
        grid_spec=pltpu.PrefetchScalarGridSpec(
            num_scalar_prefetch=1, grid=(1,), in_specs=in_specs + [whole(a) for a in small_in],
            out_specs=out_specs + [whole(a) for a in small_out]),
        out_shape=out_shape + small_out, compiler_params=_params(dimension_semantics=("arbitrary",)),
    )(chip, *operands, *small_in)
    big, small = outs[:4 * n_items], outs[4 * n_items:]
    return ([tuple(big[4 * k:4 * k + 4]) for k in range(n_items)],
            small[:n + 1], small[n + 1:2 * n + 1], small[2 * n + 1:3 * n + 1], small[3 * n + 1:])


SMALL_NAMES = ("pool_scale", "sgu_ln_g", "sgu_ln_b", "sgu_w", "sgu_b", "ln_g", "ln_b", "ple_gate_b")


TOK_TILE = 256
GRAD_TILE = 1024


def _weight_views():
    cols = lambda width: (lambda ref, b: ref.at[:, pl.ds(pl.multiple_of(b * width, 128), width)])
    rows = lambda height: (lambda ref, b: ref.at[pl.ds(pl.multiple_of(b * height, 16), height), :])
    pool_rows = POOL_GROUP // N_DEV
    return {"w_in": cols(D_IN // N_DEV),
            "pool_w": lambda ref, b: ref.at[:, pl.ds(pl.multiple_of(b * pool_rows, 16), pool_rows), :],
            "w_out": rows(D_MIX // N_DEV), "ple_w": cols(D_MODEL // N_DEV), "gate_w": rows(D_MODEL // N_DEV)}


WEIGHT_SHAPES = {"w_in": (D_MODEL, D_IN), "pool_w": (len(POOL_WINDOWS), POOL_GROUP, POOL_GROUP),
                 "w_out": (D_MIX, D_MODEL), "ple_w": (D_PLE, D_MODEL), "gate_w": (D_MODEL, D_MODEL)}


def _to_bf16(arrays):
    def body(*refs):
        for src, dst in zip(refs[:len(arrays)], refs[len(arrays):]):
            dst[...] = src[...].astype(BF16)

    return pl.pallas_call(
        body, name="cast_shards", in_specs=[VMEM_FULL] * len(arrays), out_specs=[VMEM_FULL] * len(arrays),
        out_shape=[jax.ShapeDtypeStruct(a.shape, BF16) for a in arrays], compiler_params=_params(),
    )(*arrays)


def _leading_halves(shape):
    whole_tiles = len(shape) >= 3 or shape[0] % 32 == 0
    if shape[0] % 2 or not whole_tiles:
        return None
    half = shape[0] // 2
    return lambda ref: (ref.at[pl.ds(0, half)], ref.at[pl.ds(half, half)])


def _weight_gather(shards, names):
    views = _weight_views()
    return _gather_side([shards[nm] for nm in names],
                        [jax.ShapeDtypeStruct(WEIGHT_SHAPES[nm], BF16) for nm in names], [views[nm] for nm in names],
                        [_leading_halves(shards[nm].shape) for nm in names])


def kernel(x, p, w_in, pool_w, pool_scale, sgu_ln_g, sgu_ln_b, sgu_w, sgu_b, w_out, ln_g, ln_b, ple_w, ple_gate_w, ple_gate_b, loss_target, m_w_in, m_pool_w, m_pool_scale, m_sgu_ln_g, m_sgu_ln_b, m_sgu_w, m_sgu_b, m_w_out, m_ln_g, m_ln_b, m_ple_w, m_ple_gate_w, m_ple_gate_b, v_w_in, v_pool_w, v_pool_scale, v_sgu_ln_g, v_sgu_ln_b, v_sgu_w, v_sgu_b, v_w_out, v_ln_g, v_ln_b, v_ple_w, v_ple_gate_w, v_ple_gate_b):
    seq = x.shape[1]
    x2, p2, target = x[0], p[0, 0], loss_target[0]
    core = lax.axis_index("c").astype(jnp.int32).reshape(1)
    chip = (2 * lax.axis_index("x") + lax.axis_index("y")).astype(jnp.int32).reshape(1)
    pool_rows = POOL_GROUP // N_DEV

    shard_names = ("w_in", "pool_w", "w_out", "ple_w", "gate_w")
    shards = dict(zip(shard_names, _to_bf16([w_in[0], pool_w[0], w_out[0], ple_w[0], ple_gate_w[0]])))
    w_in_f, pool_w_f = _comm_call(_weight_gather(shards, ("w_in", "pool_w")), "gather_mixer_weights")
    bias_tile = jnp.repeat(sgu_b[0].T, HEAD, axis=1)
    (h, y, pooled, xb), (w_out_f, ple_w_f, gate_w_f) = _forward_mixers(
        x2, w_in_f, pool_w_f, pool_scale, sgu_ln_g, sgu_ln_b, sgu_w[0], bias_tile, 2 * TOK_TILE,
        sides=(_weight_gather(shards, ("w_out", "ple_w", "gate_w")),))

    dy, dxr, xn, dgp, dpe, dr, loss, d_ln_g, d_ln_b, d_gate_b = _head_fwd_bwd(
        x2, y, p2, target, w_out_f, gate_w_f, ple_w_f, ln_g, ln_b, ple_gate_b, 2 * TOK_TILE, TOK_TILE)

    (d_w_out,), _ = _weight_grads([(y, dr, 1)], GRAD_TILE, "grad_w_out")
    w_out_blocks = d_w_out.reshape(N_DEV, D_MIX // N_DEV, D_MODEL)
    (d_gate_w, d_ple_w), (w_out_theirs,) = _weight_grads(
        [(xn, dgp, 1), (p2, dpe, N_DEV)], GRAD_TILE, "grad_gate_ple_w", sides=(_pair_side([w_out_blocks]),))
    gate_blocks = d_gate_w.reshape(N_DEV, D_MODEL // N_DEV, D_MODEL)

    dh, dx, d_pool_w, d_pool_scale, d_sgu_ln_g, d_sgu_ln_b, d_sgu_w, d_sgu_b = _mixers_bwd(
        h, dy, dxr, pooled, w_in_f, pool_w_f, pool_scale, sgu_ln_g, sgu_ln_b, sgu_w[0], bias_tile, TOK_TILE)

    pool_blocks = d_pool_w.reshape(N_DEV, len(POOL_WINDOWS) * pool_rows, POOL_GROUP)
    hosted_names = ("w_out", "ple_w", "gate_w", "pool_w")
    ple_sums, gate_sums, pool_sums = _pair_exchange_sum([d_ple_w, gate_blocks, pool_blocks], "pair_exchange_sum")
    hosted_sums = [_pair_sum(core, w_out_blocks, w_out_theirs, "pair_sum_w_out"), ple_sums, gate_sums, pool_sums]

    small = (d_pool_scale, d_sgu_ln_g, d_sgu_ln_b, d_sgu_w, d_sgu_b, d_ln_g, d_ln_b, d_gate_b)
    parts = small + (loss,)
    small_gather = _gather_side(parts, [jax.ShapeDtypeStruct((N_DEV,) + a.shape, F32) for a in parts],
                                [lambda ref, b: ref.at[b]] * len(parts), [_leading_halves(a.shape) for a in parts])
    cx, cy = lax.axis_index("x"), lax.axis_index("y")
    order = _row_order(cx, cy)
    w_in_own, w_in_others, side_out = _grad_w_in_reduced(
        order, xb, dh, GRAD_TILE, sides=(_chip_side([s_bf for _, s_bf in hosted_sums]), small_gather))
    hosted_others, gathered = side_out[:len(hosted_sums)], side_out[len(hosted_sums):]

    shard_of = {"w_in": (w_in, m_w_in, v_w_in), "pool_w": (pool_w, m_pool_w, v_pool_w),
                "w_out": (w_out, m_w_out, v_w_out), "ple_w": (ple_w, m_ple_w, v_ple_w),
                "gate_w": (ple_gate_w, m_ple_gate_w, v_ple_gate_w)}
    reduced = [(nm, chip, s_f32, oth) for nm, (s_f32, _), oth in zip(hosted_names, hosted_sums, hosted_others)]
    reduced.append(("w_in", jnp.zeros((1,), jnp.int32), w_in_own[None], w_in_others))
    big_out, final_items = {}, []
    for nm, which, s_f32, oth in reduced:
        w, m, v = shard_of[nm]
        two_d = s_f32.shape[1:]
        args = (s_f32, oth, w.reshape(two_d), m.reshape(two_d), v.reshape(two_d))
        if nm == "w_in":
            big_out[nm] = [r.reshape(w.shape) for r in _sum_adamw(which, *args, "adamw_" + nm)]
        else:
            final_items.append((nm, args))

    small_w = (pool_scale, sgu_ln_g, sgu_ln_b, sgu_w, sgu_b, ln_g, ln_b, ple_gate_b)
    small_m = (m_pool_scale, m_sgu_ln_g, m_sgu_ln_b, m_sgu_w, m_sgu_b, m_ln_g, m_ln_b, m_ple_gate_b)
    small_v = (v_pool_scale, v_sgu_ln_g, v_sgu_ln_b, v_sgu_w, v_sgu_b, v_ln_g, v_ln_b, v_ple_gate_b)
    natural = [[a.reshape(g.shape) for a, g in zip(group, small)] for group in (small_w, small_m, small_v)]
    shard_res, *res = _final_adamw(chip, [a for _, a in final_items], list(gathered), *natural)
    for (nm, _), out in zip(final_items, shard_res):
        big_out[nm] = [r.reshape(shard_of[nm][0].shape) for r in out]
    g_s, d_s, m_s, v_s = [[r.reshape(w.shape) for r, w in zip(kind, small_w)] for kind in res]
    total_loss = res[0][-1][0, 0]

    order = ("w_in", "pool_w", "pool_scale", "sgu_ln_g", "sgu_ln_b", "sgu_w", "sgu_b", "w_out", "ln_g", "ln_b",
             "ple_w", "ple_gate_w", "ple_gate_b")
    outs = [total_loss, dx.reshape(1, seq, D_MODEL)]
    for kind in range(4):
        for nm in order:
            key = "gate_w" if nm == "ple_gate_w" else nm
            if key in big_out:
                outs.append(big_out[key][kind])
            else:
                outs.append((g_s, d_s, m_s, v_s)[kind][SMALL_NAMES.index(nm)])
    return tuple(outs)
```

```python
from typing import Callable, NamedTuple

import numpy as np
import jax
import jax.numpy as jnp
from jax import lax
from jax.experimental import pallas as pl
from jax.experimental.pallas import tpu as pltpu

F32 = jnp.float32
BF16 = jnp.bfloat16

N_DEV = 8
D_MODEL = 1024
D_POOL = 1024
D_SGU = 1024
D_MIX = 2048
D_IN = 5120
D_PLE = 256
POOL_WINDOWS = (2, 4, 8, 16)
POOL_GROUP = 256
N_HEADS = 4
HEAD = 256
CHUNK = 128
HALO = 16
BAND_PAD = 128
ALPHA = 2.0 ** 0.25
LN_EPS = 1e-5
ADAM_LR, ADAM_B1, ADAM_B2, ADAM_EPS, ADAM_WD, ADAM_STEP = 0.001, 0.9, 0.999, 1e-08, 0.01, 10

U0, V0, Z0 = D_POOL, D_POOL + D_SGU, D_POOL + 2 * D_SGU
VMEM_LIMIT = 56 * 1024 * 1024
MESH = pl.DeviceIdType.MESH
ANY = pl.BlockSpec(memory_space=pl.ANY)
VMEM_FULL = pl.BlockSpec(memory_space=pltpu.VMEM)

_GELU_C0 = 0.7978845608028654
_GELU_C1 = 0.044715


def _gelu_cdf(x, x2):
    return 1.0 / (1.0 + jnp.exp(x * ((-2.0 * _GELU_C0) + (-2.0 * _GELU_C0 * _GELU_C1) * x2)))


def _gelu_and_grad(x):
    x2 = x * x
    cdf = _gelu_cdf(x, x2)
    g = x * cdf
    dg = cdf + g * (1.0 - cdf) * ((2.0 * _GELU_C0) + (6.0 * _GELU_C0 * _GELU_C1) * x2)
    return g, dg


def _gelu(x):
    t = jnp.tanh(_GELU_C0 * (x + _GELU_C1 * (x * x * x)))
    return x * (0.5 * (1.0 + t))


def _split_bf16(x):
    hi = x.astype(BF16)
    return hi, (x - hi.astype(F32)).astype(BF16)


def _band(tok_tile, window, transpose):
    t = np.arange(tok_tile)[:, None]
    s = np.arange(tok_tile + BAND_PAD)[None, :]
    d = (s - t) if transpose else (t + BAND_PAD - s)
    return ((d >= 0) & (d < window)).astype(np.float32)


def _bands(tok_tile, transpose):
    return jnp.asarray(np.stack([_band(tok_tile, w, transpose) for w in POOL_WINDOWS]), dtype=BF16)


def _sigmoid(x):
    return 1.0 / (1.0 + jnp.exp(-x))


def _dot(a, b):
    return jnp.dot(a, b, preferred_element_type=F32)


def _dot_nt(a, b):
    return lax.dot_general(a, b, (((1,), (1,)), ((), ())), preferred_element_type=F32)


def _dot_tn(a, b):
    return lax.dot_general(a, b, (((0,), (0,)), ((), ())), preferred_element_type=F32)


def _row_stats(x):
    mu = jnp.mean(x, axis=-1, keepdims=True)
    xc = x - mu
    var = jnp.mean(xc * xc, axis=-1, keepdims=True)
    rstd = lax.rsqrt(var + LN_EPS)
    return xc * rstd, rstd


def _ln_bwd(dxhat, xhat, rstd):
    m1 = jnp.mean(dxhat, axis=-1, keepdims=True)
    m2 = jnp.mean(dxhat * xhat, axis=-1, keepdims=True)
    return rstd * (dxhat - m1 - xhat * m2)


def _masked_sgu_w(sw_ref, hh):
    row = lax.broadcasted_iota(jnp.int32, (CHUNK, CHUNK), 0)
    col = lax.broadcasted_iota(jnp.int32, (CHUNK, CHUNK), 1)
    return jnp.where(row >= col, sw_ref[hh], 0.0)


def _inv_count(tile_index, tok_tile, window):
    tok = tile_index * tok_tile + lax.broadcasted_iota(jnp.int32, (tok_tile, 1), 0)
    return 1.0 / jnp.minimum(tok + 1, window).astype(F32)


def _params(**kw):
    return pltpu.CompilerParams(vmem_limit_bytes=VMEM_LIMIT, **kw)


def _forward_mixers(x, w_in, pool_w, pool_scale, sgu_ln_g, sgu_ln_b, sgu_w, sgu_bias_tile, tok_tile, sides=()):
    seq = x.shape[0]
    n_tiles = seq // tok_tile
    n_chunks = tok_tile // CHUNK
    side_refs = _SideRefs(sides, 8, 4, 2)

    def body(*refs):
        (x_ref, win_ref, pw_ref, ps_ref, lg_ref, lb_ref, sw_ref, sb_ref,
         hb_ref, y_ref, pooled_ref, xb_ref, aext_ref, h_ref) = side_refs.split(refs)
        i = pl.program_id(0)
        side_refs.emit(i == 0, i == n_tiles // 2, False, late=(i == (3 * n_tiles) // 4))

        @pl.when(i == 0)
        def _():
            aext_ref[0:HALO, :] = jnp.zeros((HALO, D_POOL), F32)

        xb = x_ref[...].astype(BF16)
        xb_ref[...] = xb
        for s in range(D_IN // 1024):
            cs = slice(s * 1024, (s + 1) * 1024)
            section = _dot(xb, win_ref[:, cs])
            h_ref[:, cs] = section
            if s >= 1:
                hb_ref[:, (s - 1) * 1024:s * 1024] = section.astype(BF16)

        aext_ref[HALO:HALO + tok_tile, :] = h_ref[:, 0:D_POOL]
        for g, window in enumerate(POOL_WINDOWS):
            cols = slice(g * POOL_GROUP, (g + 1) * POOL_GROUP)
            win = aext_ref[HALO:HALO + tok_tile, cols]
            for k in range(1, window):
                win = win + aext_ref[HALO - k:HALO - k + tok_tile, cols]
            pooled = win * _inv_count(i, tok_tile, window) - h_ref[:, cols]
            pb = pooled.astype(BF16)
            pooled_ref[:, cols] = pb
            mixed = _dot(pb, pw_ref[g])
            z = h_ref[:, Z0 + g * POOL_GROUP:Z0 + (g + 1) * POOL_GROUP]
            y_ref[:, cols] = (mixed * ps_ref[:, cols] * (z * _sigmoid(z))).astype(BF16)
        aext_ref[0:HALO, :] = aext_ref[tok_tile:tok_tile + HALO, :]

        for hh in range(N_HEADS):
            cols = slice(hh * HEAD, (hh + 1) * HEAD)
            swm = _masked_sgu_w(sw_ref, hh).astype(BF16)
            for n in range(n_chunks):
                rows = slice(n * CHUNK, (n + 1) * CHUNK)
                gu = _gelu(h_ref[rows, U0 + hh * HEAD:U0 + (hh + 1) * HEAD])
                gv = _gelu(h_ref[rows, V0 + hh * HEAD:V0 + (hh + 1) * HEAD])
                xhat, _ = _row_stats(gv)
                vln = xhat * lg_ref[:, cols] + lb_ref[:, cols]
                sv = _dot(swm, vln.astype(BF16)) + sb_ref[:, cols]
                z = h_ref[rows, Z0 + D_POOL + hh * HEAD:Z0 + D_POOL + (hh + 1) * HEAD]
                y_ref[rows, D_POOL + hh * HEAD:D_POOL + (hh + 1) * HEAD] = (
                    gu * sv * (z * _sigmoid(z))).astype(BF16)

        side_refs.emit(False, False, i == n_tiles - 1, late=False)

    tok = lambda width: pl.BlockSpec((tok_tile, width), lambda i: (i, 0))
    outs = pl.pallas_call(
        body, name="forward_mixers",
        grid=(n_tiles,),
        in_specs=[tok(D_MODEL)] + [VMEM_FULL] * 7 + side_refs.in_specs,
        out_specs=[tok(D_IN - D_POOL), tok(D_MIX), tok(D_POOL), tok(D_MODEL)] + side_refs.out_specs,
        out_shape=[jax.ShapeDtypeStruct((seq, D_IN - D_POOL), BF16), jax.ShapeDtypeStruct((seq, D_MIX), BF16),
                   jax.ShapeDtypeStruct((seq, D_POOL), BF16), jax.ShapeDtypeStruct((seq, D_MODEL), BF16)]
        + side_refs.out_shapes,
        scratch_shapes=[pltpu.VMEM((HALO + tok_tile, D_POOL), F32), pltpu.VMEM((tok_tile, D_IN), F32)]
        + side_refs.scratch,
        compiler_params=_params(dimension_semantics=("arbitrary",)),
    )(x, w_in, pool_w, pool_scale, sgu_ln_g, sgu_ln_b, sgu_w, sgu_bias_tile, *side_refs.inputs)
    return outs[:4], outs[4:]


def _head_fwd_bwd(x, y, p, target, w_out, gate_w, ple_w, ln_g, ln_b, gate_b, tok_tile, sub_tile):
    seq = x.shape[0]

    def body(x_ref, y_ref, p_ref, t_ref, wout_ref, gw_ref, plw_ref, lng_ref, lnb_ref, gb_ref,
             dxr_ref, xn_ref, dgp_ref, dpe_ref, dr_ref, loss_ref, dlng_ref, dlnb_ref, dgb_ref):
        i = pl.program_id(0)

        @pl.when(i == 0)
        def _():
            loss_ref[...] = jnp.zeros_like(loss_ref)
            dlng_ref[...] = jnp.zeros_like(dlng_ref)
            dlnb_ref[...] = jnp.zeros_like(dlnb_ref)
            dgb_ref[...] = jnp.zeros_like(dgb_ref)

        subs = [slice(s * sub_tile, (s + 1) * sub_tile) for s in range(tok_tile // sub_tile)]
        stats, xns, douts = [], [], []
        for rows in subs:
            r = ALPHA * x_ref[rows, :] + _dot(y_ref[rows, :], wout_ref[...])
            xhat, rstd = _row_stats(r)
            xn = xhat * lng_ref[...] + lnb_ref[...]
            xn_ref[rows, :] = xn.astype(BF16)
            stats.append((xhat, rstd))
            xns.append(xn)
        loss = jnp.zeros((1, 1), F32)
        dgb = jnp.zeros((1, D_MODEL), F32)
        for rows, xn in zip(subs, xns):
            gate = _sigmoid(_dot(xn_ref[rows, :], gw_ref[...]) + gb_ref[...])
            pe = _dot(p_ref[rows, :].astype(BF16), plw_ref[...])
            err = xn + gate * pe - t_ref[rows, :]
            loss = loss + jnp.sum(err * err, keepdims=True)
            dout = err * (1.0 / D_MODEL)
            dpe_ref[rows, :] = (dout * gate).astype(BF16)
            dgpre = dout * pe * gate * (1.0 - gate)
            dgb = dgb + jnp.sum(dgpre, axis=0, keepdims=True)
            dgp_ref[rows, :] = dgpre.astype(BF16)
            douts.append(dout)
        loss_ref[...] += (0.5 / D_MODEL) * loss
        dgb_ref[...] += dgb
        dlng = jnp.zeros((1, D_MODEL), F32)
        dlnb = jnp.zeros((1, D_MODEL), F32)
        for rows, (xhat, rstd), dout in zip(subs, stats, douts):
            dxn = dout + _dot_nt(dgp_ref[rows, :], gw_ref[...])
            dlng = dlng + jnp.sum(dxn * xhat, axis=0, keepdims=True)
            dlnb = dlnb + jnp.sum(dxn, axis=0, keepdims=True)
            dr = _ln_bwd(dxn * lng_ref[...], xhat, rstd)
            dxr_ref[rows, :] = ALPHA * dr
            dr_ref[rows, :] = dr.astype(BF16)
        dlng_ref[...] += dlng
        dlnb_ref[...] += dlnb

    tok = lambda width: pl.BlockSpec((tok_tile, width), lambda i: (i, 0))
    acc = lambda width: pl.BlockSpec((1, width), lambda i: (0, 0))
    vec = jax.ShapeDtypeStruct((1, D_MODEL), F32)
    return pl.pallas_call(
        body, name="head_fwd_bwd",
        grid=(seq // tok_tile,),
        in_specs=[tok(D_MODEL), tok(D_MIX), tok(D_PLE), tok(D_MODEL),
                  VMEM_FULL, VMEM_FULL, VMEM_FULL, VMEM_FULL, VMEM_FULL, VMEM_FULL],
        out_specs=[tok(D_MODEL), tok(D_MODEL), tok(D_MODEL), tok(D_MODEL), tok(D_MODEL),
                   acc(128), acc(D_MODEL), acc(D_MODEL), acc(D_MODEL)],
        out_shape=[jax.ShapeDtypeStruct((seq, D_MODEL), F32),
                   jax.ShapeDtypeStruct((seq, D_MODEL), BF16), jax.ShapeDtypeStruct((seq, D_MODEL), BF16),
                   jax.ShapeDtypeStruct((seq, D_MODEL), BF16), jax.ShapeDtypeStruct((seq, D_MODEL), BF16),
                   jax.ShapeDtypeStruct((1, 128), F32), vec, vec, vec],
        compiler_params=_params(dimension_semantics=("arbitrary",)),
    )(x, y, p, target, w_out, gate_w, ple_w, ln_g, ln_b, gate_b)


def _mixers_bwd(h, dr, dxr, pooled, w_in, pool_w, pool_scale, sgu_ln_g, sgu_ln_b, sgu_w, sgu_bias_tile, w_out,
                tok_tile):
    seq = h.shape[0]
    n_tiles = seq // tok_tile
    n_chunks = tok_tile // CHUNK
    pool_rows = POOL_GROUP // N_DEV

    def body(h_ref, dr_ref, dxr_ref, pooled_ref, win_ref, pw_ref, ps_ref, lg_ref, lb_ref, sw_ref, sb_ref, band_ref,
             wout_ref,
             dh_ref, dx_ref, dpw_ref, dps_ref, dlg_ref, dlb_ref, dsw_ref, dsb_ref,
             qhi_ref, qlo_ref, dpw_acc, dsb_acc, dys_ref):
        i = pl.program_id(0)
        tile = n_tiles - 1 - i

        @pl.when(i == 0)
        def _():
            qhi_ref[...] = jnp.zeros_like(qhi_ref)
            qlo_ref[...] = jnp.zeros_like(qlo_ref)
            dpw_acc[...] = jnp.zeros_like(dpw_acc)
            dsb_acc[...] = jnp.zeros_like(dsb_acc)
            dps_ref[...] = jnp.zeros_like(dps_ref)
            dlg_ref[...] = jnp.zeros_like(dlg_ref)
            dlb_ref[...] = jnp.zeros_like(dlb_ref)
            dsw_ref[...] = jnp.zeros_like(dsw_ref)

        for half in range(2):
            hc = slice(half * (D_MIX // 2), (half + 1) * (D_MIX // 2))
            dys_ref[:, hc] = _dot_nt(dr_ref[...], wout_ref[hc, :])

        def dy_slice(cols):
            return dys_ref[:, cols]

        def h_at(rows, cols):
            return h_ref[rows, cols.start - D_POOL:cols.stop - D_POOL].astype(F32)

        everything = slice(0, tok_tile)
        for part in (qhi_ref, qlo_ref):
            part[tok_tile:tok_tile + HALO, :] = part[0:HALO, :]
        for g, window in enumerate(POOL_WINDOWS):
            cols = slice(g * POOL_GROUP, (g + 1) * POOL_GROUP)
            zcols = slice(Z0 + g * POOL_GROUP, Z0 + (g + 1) * POOL_GROUP)
            z = h_at(everything, zcols)
            sz = _sigmoid(z)
            pb = pooled_ref[:, cols]
            mixed = _dot(pb, pw_ref[g])
            dyp = dy_slice(cols)
            dh_ref[:, zcols] = (dyp * (mixed * ps_ref[:, cols]) * (sz * (1.0 + z * (1.0 - sz)))).astype(BF16)
            dms = dyp * (z * sz)
            dps_ref[:, cols] += jnp.sum(dms * mixed, axis=0, keepdims=True)
            dmixed = (dms * ps_ref[:, cols]).astype(BF16)
            dpw_acc[g] += _dot_tn(pb, dmixed)
            dpooled = _dot_nt(dmixed, pw_ref[g])
            qhi_ref[everything, cols], qlo_ref[everything, cols] = _split_bf16(
                dpooled * _inv_count(tile, tok_tile, window))
            da = _dot(band_ref[g], qhi_ref[:, cols]) + _dot(band_ref[g], qlo_ref[:, cols]) - dpooled
            dh_ref[:, cols] = da.astype(BF16)

        pool_cols = [slice(o + g * POOL_GROUP, o + (g + 1) * POOL_GROUP)
                     for o in (0, Z0) for g in range(len(POOL_WINDOWS))]
        pool_slices = [pool_cols[0:3], pool_cols[3:6], pool_cols[6:8], []]
        for hh in range(N_HEADS):
            cols = slice(hh * HEAD, (hh + 1) * HEAD)
            ucols = slice(U0 + hh * HEAD, U0 + (hh + 1) * HEAD)
            vcols = slice(V0 + hh * HEAD, V0 + (hh + 1) * HEAD)
            zcols = slice(Z0 + D_POOL + hh * HEAD, Z0 + D_POOL + (hh + 1) * HEAD)
            sw32 = _masked_sgu_w(sw_ref, hh)
            swm = sw32.astype(BF16)
            swm_t = sw32.T.astype(BF16)
            dys_head = dy_slice(slice(D_POOL + hh * HEAD, D_POOL + (hh + 1) * HEAD))
            for n in range(n_chunks):
                rows = slice(n * CHUNK, (n + 1) * CHUNK)
                gu, dgu_du = _gelu_and_grad(h_at(rows, ucols))
                gv, dgv_dv = _gelu_and_grad(h_at(rows, vcols))
                xhat, rstd = _row_stats(gv)
                vb = (xhat * lg_ref[:, cols] + lb_ref[:, cols]).astype(BF16)
                sv = _dot(swm, vb) + sb_ref[:, cols]
                z = h_at(rows, zcols)
                sz = _sigmoid(z)
                dys = dys_head[rows, :]
                dh_ref[rows, zcols] = (dys * (gu * sv) * (sz * (1.0 + z * (1.0 - sz)))).astype(BF16)
                dyg = dys * (z * sz)
                dh_ref[rows, ucols] = (dyg * sv * dgu_du).astype(BF16)
                dsv = dyg * gu
                dsb_acc[:, cols] += dsv
                dsvb = dsv.astype(BF16)
                dsw_ref[hh] += _dot_nt(dsvb, vb)
                dvln = _dot(swm_t, dsvb)
                dlg_ref[:, cols] += jnp.sum(dvln * xhat, axis=0, keepdims=True)
                dlb_ref[:, cols] += jnp.sum(dvln, axis=0, keepdims=True)
                dgv = _ln_bwd(dvln * lg_ref[:, cols], xhat, rstd)
                dh_ref[rows, vcols] = (dgv * dgv_dv).astype(BF16)
            ready = [ucols, vcols, zcols] + pool_slices[hh]
            part = _dot_nt(dh_ref[:, ready[0]], win_ref[:, ready[0]])
            for sl in ready[1:]:
                part = part + _dot_nt(dh_ref[:, sl], win_ref[:, sl])
            if hh == 0:
                dx_ref[...] = dxr_ref[...] + part
            else:
                dx_ref[...] += part

        @pl.when(i == n_tiles - 1)
        def _():
            for g in range(len(POOL_WINDOWS)):
                for b in range(N_DEV):
                    dpw_ref[b, g] = dpw_acc[g, b * pool_rows:(b + 1) * pool_rows, :]
            row = lax.broadcasted_iota(jnp.int32, (CHUNK, CHUNK), 0)
            col = lax.broadcasted_iota(jnp.int32, (CHUNK, CHUNK), 1)
            for hh in range(N_HEADS):
                dsw_ref[hh] = jnp.where(row >= col, dsw_ref[hh], 0.0)
                total = jnp.sum(dsb_acc[:, hh * HEAD:(hh + 1) * HEAD], axis=1, keepdims=True)
                dsb_ref[hh:hh + 1, :] = jnp.broadcast_to(total, (CHUNK, CHUNK)).T[0:1, :]

    tok = lambda width: pl.BlockSpec((tok_tile, width), lambda i: (n_tiles - 1 - i, 0))
    whole = lambda shape: pl.BlockSpec(shape, lambda i: (0,) * len(shape))
    vec = jax.ShapeDtypeStruct((1, D_MODEL), F32)
    return pl.pallas_call(
        body, name="mixers_bwd",
        grid=(n_tiles,),
        in_specs=[tok(D_IN - D_POOL), tok(D_MODEL), tok(D_MODEL), tok(D_POOL)] + [VMEM_FULL] * 9,
        out_specs=[tok(D_IN), tok(D_MODEL), whole((N_DEV, len(POOL_WINDOWS), pool_rows, POOL_GROUP)),
                   whole((1, D_POOL)), whole((1, D_SGU)), whole((1, D_SGU)),
                   whole((N_HEADS, CHUNK, CHUNK)), whole((N_HEADS, CHUNK))],
        out_shape=[jax.ShapeDtypeStruct((seq, D_IN), BF16), jax.ShapeDtypeStruct((seq, D_MODEL), F32),
                   jax.ShapeDtypeStruct((N_DEV, len(POOL_WINDOWS), pool_rows, POOL_GROUP), F32),
                   vec, vec, vec,
                   jax.ShapeDtypeStruct((N_HEADS, CHUNK, CHUNK), F32),
                   jax.ShapeDtypeStruct((N_HEADS, CHUNK), F32)],
        scratch_shapes=[pltpu.VMEM((tok_tile + BAND_PAD, D_POOL), BF16),
                        pltpu.VMEM((tok_tile + BAND_PAD, D_POOL), BF16),
                        pltpu.VMEM((len(POOL_WINDOWS), POOL_GROUP, POOL_GROUP), F32),
                        pltpu.VMEM((CHUNK, D_SGU), F32), pltpu.VMEM((tok_tile, D_MIX), F32)],
        compiler_params=_params(dimension_semantics=("arbitrary",)),
    )(h, dr, dxr, pooled, w_in, pool_w, pool_scale, sgu_ln_g, sgu_ln_b, sgu_w, sgu_bias_tile, _bands(tok_tile, True),
      w_out)


def _weight_grads(items, tok_tile, name, sides=()):
    n_items = len(items)
    seq = items[0][0].shape[0]
    n_k = seq // tok_tile
    side_refs = _SideRefs(sides, 2 * n_items, n_items, 0)

    def body(*refs):
        refs = side_refs.split(refs)
        k = pl.program_id(0)
        side_refs.emit(k == 0, k == n_k // 2, False)
        for t, (_, b, n_col_blocks) in enumerate(items):
            a_ref, b_ref, out_ref = refs[2 * t], refs[2 * t + 1], refs[2 * n_items + t]
            nb = b.shape[1] // n_col_blocks

            @pl.when(k == 0)
            def _():
                out_ref[...] = jnp.zeros_like(out_ref)

            res = _dot_tn(a_ref[...].astype(BF16), b_ref[...])
            for blk in range(n_col_blocks):
                out_ref[blk] += res[:, blk * nb:(blk + 1) * nb]
        side_refs.emit(False, False, k == n_k - 1)

    in_specs, operands, out_specs, out_shape = [], [], [], []
    for a, b, n_col_blocks in items:
        shape = (n_col_blocks, a.shape[1], b.shape[1] // n_col_blocks)
        in_specs += [pl.BlockSpec((tok_tile, a.shape[1]), lambda k: (k, 0)),
                     pl.BlockSpec((tok_tile, b.shape[1]), lambda k: (k, 0))]
        operands += [a, b]
        out_specs.append(pl.BlockSpec(shape, lambda k: (0, 0, 0)))
        out_shape.append(jax.ShapeDtypeStruct(shape, F32))
    outs = pl.pallas_call(
        body, name=name, grid=(n_k,),
        in_specs=in_specs + side_refs.in_specs, out_specs=out_specs + side_refs.out_specs,
        out_shape=out_shape + side_refs.out_shapes, scratch_shapes=side_refs.scratch,
        compiler_params=_params(dimension_semantics=("arbitrary",)),
    )(*operands, *side_refs.inputs)
    return outs[:n_items], outs[n_items:]


class _Side(NamedTuple):
    inputs: list
    out_shapes: list
    sem_shapes: list
    emit: Callable


def _when(cond):
    if cond is True:
        return lambda f: f()
    if cond is False:
        return lambda f: None
    return pl.when(cond)


def _place():
    return lax.axis_index("x"), lax.axis_index("y"), lax.axis_index("c")


def _other_chips(x, y):
    return [(1 - x, y), (x, 1 - y), (1 - x, 1 - y)]


def _gather_side(shards, out_shapes, views, halves):
    n = len(shards)
    n_sems = 10

    def emit(ins, outs, sems, first, mid, last, late=None):
        send_sems, recv_sems, local_sems = sems
        x, y, c = _place()
        here, x_nbr, y_nbr, diag = (x, y), (1 - x, y), (x, 1 - y), (1 - x, 1 - y)
        sibling = (x, y, 1 - c)

        def block(k, chip, core):
            return views[k](outs[k], 4 * chip[0] + 2 * chip[1] + core)

        def piece(k, ref, p):
            return ref if halves[k] is None else halves[k](ref)[p]

        def n_pieces(k):
            return 1 if halves[k] is None else 2

        def copy(k, s, src, dst, to):
            return pltpu.make_async_remote_copy(
                src_ref=src, dst_ref=dst, send_sem=send_sems.at[k, s], recv_sem=recv_sems.at[k, s],
                device_id=to, device_id_type=MESH)

        def outgoing(k, s):
            mine = block(k, here, c)
            if s == 0:
                return copy(k, 0, ins[k], mine, sibling)
            if s in (1, 2):
                return copy(k, s, piece(k, ins[k], s - 1), piece(k, mine, s - 1), (*x_nbr, c))
            if s in (3, 4):
                return copy(k, s, piece(k, ins[k], s - 3), piece(k, mine, s - 3), (*y_nbr, c))
            if s == 5:
                part = piece(k, block(k, x_nbr, c), 0)
                return copy(k, 5, part, part, (*y_nbr, c))
            if s == 6:
                part = piece(k, block(k, y_nbr, c), 1)
                return copy(k, 6, part, part, (*x_nbr, c))
            whole = block(k, (x_nbr, y_nbr, diag)[s - 7], c)
            return copy(k, s, whole, whole, sibling)

        def incoming(k, s):
            if s == 0:
                zone = block(k, here, 1 - c)
            elif s in (1, 2):
                zone = piece(k, block(k, x_nbr, c), s - 1)
            elif s in (3, 4):
                zone = piece(k, block(k, y_nbr, c), s - 3)
            elif s in (5, 6):
                zone = piece(k, block(k, diag, c), s - 5)
            else:
                zone = block(k, (x_nbr, y_nbr, diag)[s - 7], 1 - c)
            return copy(k, s, zone, zone, sibling)

        def own(k):
            return pltpu.make_async_copy(ins[k], block(k, here, c), local_sems.at[k])

        def used(k):
            return list(range(n_sems)) if n_pieces(k) == 2 else [0, 1, 3, 5, 7, 8, 9]

        @_when(first)
        def _():
            for k in range(n):
                own(k).start()
                for s in (0, 1, 4, 2, 3):
                    if s in used(k):
                        outgoing(k, s).start()

        @_when(mid)
        def _():
            for k in range(n):
                incoming(k, 1).wait_recv()
                outgoing(k, 5).start()
                if n_pieces(k) == 2:
                    incoming(k, 4).wait_recv()
                    outgoing(k, 6).start()
                    incoming(k, 2).wait_recv()
                outgoing(k, 7).start()
                incoming(k, 3).wait_recv()
                outgoing(k, 8).start()

        @_when(last if late is None else late)
        def _():
            for k in range(n):
                incoming(k, 5).wait_recv()
                if n_pieces(k) == 2:
                    incoming(k, 6).wait_recv()
                outgoing(k, 9).start()

        @_when(last)
        def _():
            for k in range(n):
                for s in (0, 7, 8, 9):
                    incoming(k, s).wait_recv()
            for k in range(n):
                for s in used(k):
                    outgoing(k, s).wait_send()
                own(k).wait()

    sems = [pltpu.SemaphoreType.DMA((n, n_sems)), pltpu.SemaphoreType.DMA((n, n_sems)),
            pltpu.SemaphoreType.DMA((n,))]
    return _Side(list(shards), list(out_shapes), sems, emit)


def _pair_side(grads):
    n = len(grads)

    def emit(ins, theirs, sems, first, mid, last, late=None):
        send_sems, recv_sems = sems
        x, y, c = _place()

        def copies():
            return [pltpu.make_async_remote_copy(
                src_ref=ins[k].at[2 * j + (1 - c)], dst_ref=theirs[k].at[j],
                send_sem=send_sems.at[k, j], recv_sem=recv_sems.at[k, j],
                device_id=(x, y, 1 - c), device_id_type=MESH) for k in range(n) for j in range(4)]

        @_when(first)
        def _():
            for cp in copies():
                cp.start()

        @_when(last)
        def _():
            for cp in copies():
                cp.wait_recv()
            for cp in copies():
                cp.wait_send()

    shapes = [jax.ShapeDtypeStruct((4,) + g.shape[1:], g.dtype) for g in grads]
    return _Side(list(grads), shapes, [pltpu.SemaphoreType.DMA((n, 4)), pltpu.SemaphoreType.DMA((n, 4))], emit)


def _chip_side(sums):
    n = len(sums)

    def emit(ins, others, sems, first, mid, last, late=None):
        send_sems, recv_sems = sems
        x, y, c = _place()

        def copies():
            return [pltpu.make_async_remote_copy(
                src_ref=ins[k].at[2 * px + py], dst_ref=others[k].at[r],
                send_sem=send_sems.at[k, r], recv_sem=recv_sems.at[k, r],
                device_id=(px, py, c), device_id_type=MESH)
                for k in range(n) for r, (px, py) in enumerate(_other_chips(x, y))]

        @_when(first)
        def _():
            for cp in copies():
                cp.start()

        @_when(last)
        def _():
            for cp in copies():
                cp.wait_recv()
            for cp in copies():
                cp.wait_send()

    shapes = [jax.ShapeDtypeStruct((3,) + s.shape[1:], s.dtype) for s in sums]
    return _Side(list(sums), shapes, [pltpu.SemaphoreType.DMA((n, 3)), pltpu.SemaphoreType.DMA((n, 3))], emit)


def _comm_call(side, name):
    n_in, n_out = len(side.inputs), len(side.out_shapes)

    def body(*refs):
        side.emit(refs[:n_in], refs[n_in:n_in + n_out], refs[n_in + n_out:], True, True, True)

    return pl.pallas_call(
        body, name=name, in_specs=[ANY] * n_in, out_specs=[ANY] * n_out,
        out_shape=side.out_shapes, scratch_shapes=side.sem_shapes,
    )(*side.inputs)


class _SideRefs:
    def __init__(self, sides, n_in, n_out, n_scratch):
        self.sides, self.n_in, self.n_out, self.n_scratch = sides, n_in, n_out, n_scratch
        self.inputs = [a for s in sides for a in s.inputs]
        self.out_shapes = [o for s in sides for o in s.out_shapes]
        self.scratch = [m for s in sides for m in s.sem_shapes]
        self.in_specs = [ANY] * len(self.inputs)
        self.out_specs = [ANY] * len(self.out_shapes)

    def split(self, refs):
        refs = list(refs)
        n_side_in, n_side_out = len(self.inputs), len(self.out_shapes)
        ins, rest = refs[:self.n_in], refs[self.n_in:]
        side_in, rest = rest[:n_side_in], rest[n_side_in:]
        outs, rest = rest[:self.n_out], rest[self.n_out:]
        side_out, rest = rest[:n_side_out], rest[n_side_out:]
        scratch, side_sems = rest[:self.n_scratch], rest[self.n_scratch:]
        self._refs = (side_in, side_out, side_sems)
        return ins + outs + scratch

    def emit(self, first, mid, last, late=None):
        side_in, side_out, side_sems = self._refs
        for s in self.sides:
            a, b, m = len(s.inputs), len(s.out_shapes), len(s.sem_shapes)
            s.emit(side_in[:a], side_out[:b], side_sems[:m], first, mid, last, late)
            side_in, side_out, side_sems = side_in[a:], side_out[b:], side_sems[m:]


ROW_RELATIONS = (0, 1, 2)


def _row_order(x, y):
    chips = _other_chips(x, y)
    return jnp.stack([2 * px + py for px, py in [chips[r] for r in ROW_RELATIONS] + [(x, y)]]).astype(jnp.int32)


def _grad_w_in_reduced(order, a, b, tok_tile, sides=()):
    seq, m = a.shape
    nb = b.shape[1] // N_DEV
    n_k = seq // tok_tile
    n_rows = 4
    last_step = n_rows * n_k - 1
    assert n_k >= 3
    fetch_at, sum_at = 1, 2
    side_refs = _SideRefs(sides, 3, 3, 8)

    def body(*refs):
        (order_ref, a_ref, b_ref, own_ref, theirs_ref, others_ref,
         acc_ref, stage_ref, sumbf_ref, pair_send, pair_recv, ici_send, ici_recv, stage_sem) = side_refs.split(refs)
        j, k = pl.program_id(0), pl.program_id(1)
        step = j * n_k + k
        side_refs.emit(step == 0, step == (n_rows * n_k) // 2, False, late=(step == (3 * n_rows * n_k) // 4))
        x, y, c = _place()
        chips = _other_chips(x, y)

        def to_sibling(row):
            return pltpu.make_async_remote_copy(
                src_ref=acc_ref.at[row % 2, 1 - c], dst_ref=theirs_ref.at[row],
                send_sem=pair_send.at[row], recv_sem=pair_recv.at[row],
                device_id=(x, y, 1 - c), device_id_type=MESH)

        def to_owner(row):
            rel = ROW_RELATIONS[row]
            px, py = chips[rel]
            return pltpu.make_async_remote_copy(
                src_ref=sumbf_ref.at[row], dst_ref=others_ref.at[rel],
                send_sem=ici_send.at[rel], recv_sem=ici_recv.at[rel],
                device_id=(px, py, c), device_id_type=MESH)

        def staged(row):
            return pltpu.make_async_copy(theirs_ref.at[row], stage_ref, stage_sem.at[0])

        @pl.when(k == 0)
        def _():
            acc_ref[j % 2] = jnp.zeros((2, m, nb), F32)

        res = _dot_tn(a_ref[...].astype(BF16), b_ref[...])
        for blk in range(2):
            acc_ref[j % 2, blk] += res[:, blk * nb:(blk + 1) * nb]

        for row in range(n_rows):
            @pl.when((j == row) & (k == n_k - 1))
            def _():
                to_sibling(row).start()

            if row < n_rows - 1:
                @pl.when((j == row + 1) & (k == fetch_at))
                def _():
                    to_sibling(row).wait_recv()
                    staged(row).start()

                @pl.when((j == row + 1) & (k == sum_at))
                def _():
                    staged(row).wait()
                    to_sibling(row).wait_send()
                    sumbf_ref[row] = (acc_ref[row % 2, c] + stage_ref[...]).astype(BF16)
                    to_owner(row).start()

        @pl.when(step == last_step)
        def _():
            row = n_rows - 1
            to_sibling(row).wait_recv()
            staged(row).start()
            staged(row).wait()
            to_sibling(row).wait_send()
            own_ref[...] = acc_ref[row % 2, c] + stage_ref[...]
            for r in range(n_rows - 1):
                to_owner(r).wait_recv()
                to_owner(r).wait_send()

        side_refs.emit(False, False, step == last_step, late=False)

    block = jax.ShapeDtypeStruct((m, nb), F32)
    outs = pl.pallas_call(
        body, name="grad_w_in",
        grid_spec=pltpu.PrefetchScalarGridSpec(
            num_scalar_prefetch=1, grid=(n_rows, n_k),
            in_specs=[pl.BlockSpec((tok_tile, m), lambda j, k, order_ref: (k, 0)),
                      pl.BlockSpec((tok_tile, 2 * nb), lambda j, k, order_ref: (k, order_ref[j]))]
            + side_refs.in_specs,
            out_specs=[pl.BlockSpec((m, nb), lambda j, k, order_ref: (0, 0)), ANY, ANY] + side_refs.out_specs,
            scratch_shapes=[pltpu.VMEM((2, 2, m, nb), F32), pltpu.VMEM((m, nb), F32),
                            pltpu.VMEM((n_rows - 1, m, nb), BF16),
                            pltpu.SemaphoreType.DMA((n_rows,)), pltpu.SemaphoreType.DMA((n_rows,)),
                            pltpu.SemaphoreType.DMA((n_rows - 1,)), pltpu.SemaphoreType.DMA((n_rows - 1,)),
                            pltpu.SemaphoreType.DMA((1,))] + side_refs.scratch),
        out_shape=[block, jax.ShapeDtypeStruct((n_rows, m, nb), F32),
                   jax.ShapeDtypeStruct((n_rows - 1, m, nb), BF16)] + side_refs.out_shapes,
        compiler_params=_params(dimension_semantics=("arbitrary", "arbitrary")),
    )(order, a, b, *side_refs.inputs)
    return outs[0], outs[2], outs[3:]


def _row_tile(rows, cols):
    tile = rows
    while tile * cols > 256 * 1024 and tile % 16 == 0:
        tile //= 2
    return tile


def _pair_sum(core, grads, theirs, name):
    _, rows, cols = theirs.shape
    rt = _row_tile(rows, cols)

    def body(core_ref, a_ref, b_ref, o_ref, ob_ref):
        total = a_ref[...] + b_ref[...]
        o_ref[...] = total
        ob_ref[...] = total.astype(BF16)

    spec = pl.BlockSpec((None, rt, cols), lambda j, i, core_ref: (j, i, 0))
    mine = pl.BlockSpec((None, None, rt, cols), lambda j, i, core_ref: (j, core_ref[0], i, 0))
    return pl.pallas_call(
        body, name=name,
        grid_spec=pltpu.PrefetchScalarGridSpec(
            num_scalar_prefetch=1, grid=(4, rows // rt), in_specs=[mine, spec], out_specs=[spec, spec]),
        out_shape=[jax.ShapeDtypeStruct(theirs.shape, F32), jax.ShapeDtypeStruct(theirs.shape, BF16)],
        compiler_params=_params(dimension_semantics=("arbitrary", "arbitrary")),
    )(core, grads.reshape(4, 2, rows, cols), theirs)


def _adamw(w, g, m, v):
    m = ADAM_B1 * m + (1.0 - ADAM_B1) * g
    v = ADAM_B2 * v + (1.0 - ADAM_B2) * (g * g)
    m_hat = m / (1.0 - ADAM_B1 ** ADAM_STEP)
    v_hat = v / (1.0 - ADAM_B2 ** ADAM_STEP)
    delta = -ADAM_LR * (m_hat / (jnp.sqrt(v_hat) + ADAM_EPS) + ADAM_WD * w)
    return delta, m, v


def _sum_adamw(chip, sums, others, w, m, v, name):
    _, rows, cols = sums.shape
    rt = _row_tile(rows, cols)

    def body(chip_ref, own_ref, oth_ref, w_ref, m_ref, v_ref, g_ref, d_ref, nm_ref, nv_ref):
        g = ((own_ref[...] + oth_ref[0].astype(F32)) + oth_ref[1].astype(F32)) + oth_ref[2].astype(F32)
        g_ref[...] = g
        d_ref[...], nm_ref[...], nv_ref[...] = _adamw(w_ref[...], g, m_ref[...], v_ref[...])

    spec = pl.BlockSpec((rt, cols), lambda i, chip_ref: (i, 0))
    own = pl.BlockSpec((None, rt, cols), lambda i, chip_ref: (chip_ref[0], i, 0))
    shape = jax.ShapeDtypeStruct((rows, cols), F32)
    return pl.pallas_call(
        body, name=name,
        grid_spec=pltpu.PrefetchScalarGridSpec(
            num_scalar_prefetch=1, grid=(rows // rt,),
            in_specs=[own, pl.BlockSpec((3, rt, cols), lambda i, chip_ref: (0, i, 0)), spec, spec, spec],
            out_specs=[spec] * 4),
        out_shape=[shape] * 4,
        compiler_params=_params(dimension_semantics=("arbitrary",)),
    )(chip, sums, others, w, m, v)


def _pair_exchange_sum(grads, name):
    n = len(grads)

    def body(*refs):
        ins, out32, outbf = refs[:n], refs[n:2 * n], refs[2 * n:3 * n]
        mine_v, theirs_v = refs[3 * n:4 * n], refs[4 * n:5 * n]
        send_sems, recv_sems, local_sems = refs[5 * n:]
        x, y, c = _place()
        remote = [pltpu.make_async_remote_copy(
            src_ref=ins[k].at[2 * j + (1 - c)], dst_ref=theirs_v[k].at[j],
            send_sem=send_sems.at[k, j], recv_sem=recv_sems.at[k, j],
            device_id=(x, y, 1 - c), device_id_type=MESH) for k in range(n) for j in range(4)]
        local = [pltpu.make_async_copy(ins[k].at[2 * j + c], mine_v[k].at[j], local_sems.at[k, j])
                 for k in range(n) for j in range(4)]
        for cp in remote + local:
            cp.start()
        for cp in local:
            cp.wait()
        for cp in remote:
            cp.wait_recv()
        for k in range(n):
            total = mine_v[k][...] + theirs_v[k][...]
            out32[k][...] = total
            outbf[k][...] = total.astype(BF16)
        for cp in remote:
            cp.wait_send()

    halves = [(4,) + g.shape[1:] for g in grads]
    outs = pl.pallas_call(
        body, name=name, in_specs=[ANY] * n, out_specs=[VMEM_FULL] * (2 * n),
        out_shape=[jax.ShapeDtypeStruct(h, F32) for h in halves] + [jax.ShapeDtypeStruct(h, BF16) for h in halves],
        scratch_shapes=[pltpu.VMEM(h, F32) for h in halves] * 2
        + [pltpu.SemaphoreType.DMA((n, 4))] * 3,
        compiler_params=_params(),
    )(*grads)
    return [(outs[k], outs[n + k]) for k in range(n)]


def _final_adamw(chip, items, gathered, ws, ms, vs):
    n_items, n = len(items), len(ws)

    def body(chip_ref, *refs):
        refs = list(refs)
        ins, refs = refs[:5 * n_items], refs[5 * n_items:]
        g8, refs = refs[:n + 1], refs[n + 1:]
        w, m, v, refs = refs[:n], refs[n:2 * n], refs[2 * n:3 * n], refs[3 * n:]
        outs, refs = refs[:4 * n_items], refs[4 * n_items:]
        g_out, d_out, m_out, v_out = refs[:n + 1], refs[n + 1:2 * n + 1], refs[2 * n + 1:3 * n + 1], refs[3 * n + 1:]
        for k in range(n_items):
            own_ref, oth_ref, w_ref, m_ref, v_ref = ins[5 * k:5 * k + 5]
            g = ((own_ref[...] + oth_ref[0].astype(F32)) + oth_ref[1].astype(F32)) + oth_ref[2].astype(F32)
            outs[4 * k][...] = g
            outs[4 * k + 1][...], outs[4 * k + 2][...], outs[4 * k + 3][...] = _adamw(
                w_ref[...], g, m_ref[...], v_ref[...])
        for k in range(n + 1):
            g = g8[k][0]
            for b in range(1, N_DEV):
                g = g + g8[k][b]
            g_out[k][...] = g
            if k < n:
                d_out[k][...], m_out[k][...], v_out[k][...] = _adamw(w[k][...], g, m[k][...], v[k][...])

    whole = lambda a: pl.BlockSpec(a.shape, lambda i, chip_ref: (0,) * len(a.shape))
    in_specs, operands, out_specs, out_shape = [], [], [], []
    for sums, others, w, m, v in items:
        in_specs += [pl.BlockSpec((None,) + sums.shape[1:], lambda i, chip_ref: (chip_ref[0], 0, 0)),
                     whole(others), whole(w), whole(m), whole(v)]
        operands += [sums, others, w, m, v]
        out_specs += [whole(w)] * 4
        out_shape += [jax.ShapeDtypeStruct(w.shape, F32)] * 4
    small_in = list(gathered) + list(ws) + list(ms) + list(vs)
    small_out = ([jax.ShapeDtypeStruct(w.shape, F32) for w in ws]
                 + [jax.ShapeDtypeStruct(gathered[-1].shape[1:], F32)]
                 + [jax.ShapeDtypeStruct(w.shape, F32) for w in ws] * 3)
    outs = pl.pallas_call(
        body, name="final_adamw",
        grid_spec=pltpu.PrefetchScalarGridSpec(
            num_scalar_prefetch=1, grid=(1,), in_specs=in_specs + [whole(a) for a in small_in],
            out_specs=out_specs + [whole(a) for a in small_out]),
        out_shape=out_shape + small_out, compiler_params=_params(dimension_semantics=("arbitrary",)),
    )(chip, *operands, *small_in)
    big, small = outs[:4 * n_items], outs[4 * n_items:]
    return ([tuple(big[4 * k:4 * k + 4]) for k in range(n_items)],
            small[:n + 1], small[n + 1:2 * n + 1], small[2 * n + 1:3 * n + 1], small[3 * n + 1:])


SMALL_NAMES = ("pool_scale", "sgu_ln_g", "sgu_ln_b", "sgu_w", "sgu_b", "ln_g", "ln_b", "ple_gate_b")


TOK_TILE = 256
GRAD_TILE = 1024


def _weight_views():
    cols = lambda width: (lambda ref, b: ref.at[:, pl.ds(pl.multiple_of(b * width, 128), width)])
    rows = lambda height: (lambda ref, b: ref.at[pl.ds(pl.multiple_of(b * height, 16), height), :])
    pool_rows = POOL_GROUP // N_DEV
    return {"w_in": cols(D_IN // N_DEV),
            "pool_w": lambda ref, b: ref.at[:, pl.ds(pl.multiple_of(b * pool_rows, 16), pool_rows), :],
            "w_out": rows(D_MIX // N_DEV), "ple_w": cols(D_MODEL // N_DEV), "gate_w": rows(D_MODEL // N_DEV)}


WEIGHT_SHAPES = {"w_in": (D_MODEL, D_IN), "pool_w": (len(POOL_WINDOWS), POOL_GROUP, POOL_GROUP),
                 "w_out": (D_MIX, D_MODEL), "ple_w": (D_PLE, D_MODEL), "gate_w": (D_MODEL, D_MODEL)}


def _to_bf16(arrays):
    def body(*refs):
        for src, dst in zip(refs[:len(arrays)], refs[len(arrays):]):
            dst[...] = src[...].astype(BF16)

    return pl.pallas_call(
        body, name="cast_shards", in_specs=[VMEM_FULL] * len(arrays), out_specs=[VMEM_FULL] * len(arrays),
        out_shape=[jax.ShapeDtypeStruct(a.shape, BF16) for a in arrays], compiler_params=_params(),
    )(*arrays)


def _leading_halves(shape):
    whole_tiles = len(shape) >= 3 or shape[0] % 32 == 0
    if shape[0] % 2 or not whole_tiles:
        return None
    half = shape[0] // 2
    return lambda ref: (ref.at[pl.ds(0, half)], ref.at[pl.ds(half, half)])


def _weight_gather(shards, names):
    views = _weight_views()
    return _gather_side([shards[nm] for nm in names],
                        [jax.ShapeDtypeStruct(WEIGHT_SHAPES[nm], BF16) for nm in names], [views[nm] for nm in names],
                        [_leading_halves(shards[nm].shape) for nm in names])


def kernel(x, p, w_in, pool_w, pool_scale, sgu_ln_g, sgu_ln_b, sgu_w, sgu_b, w_out, ln_g, ln_b, ple_w, ple_gate_w, ple_gate_b, loss_target, m_w_in, m_pool_w, m_pool_scale, m_sgu_ln_g, m_sgu_ln_b, m_sgu_w, m_sgu_b, m_w_out, m_ln_g, m_ln_b, m_ple_w, m_ple_gate_w, m_ple_gate_b, v_w_in, v_pool_w, v_pool_scale, v_sgu_ln_g, v_sgu_ln_b, v_sgu_w, v_sgu_b, v_w_out, v_ln_g, v_ln_b, v_ple_w, v_ple_gate_w, v_ple_gate_b):
    seq = x.shape[1]
    x2, p2, target = x[0], p[0, 0], loss_target[0]
    core = lax.axis_index("c").astype(jnp.int32).reshape(1)
    chip = (2 * lax.axis_index("x") + lax.axis_index("y")).astype(jnp.int32).reshape(1)
    pool_rows = POOL_GROUP // N_DEV

    shard_names = ("w_in", "pool_w", "w_out", "ple_w", "gate_w")
    shards = dict(zip(shard_names, _to_bf16([w_in[0], pool_w[0], w_out[0], ple_w[0], ple_gate_w[0]])))
    w_in_f, pool_w_f = _comm_call(_weight_gather(shards, ("w_in", "pool_w")), "gather_mixer_weights")
    bias_tile = jnp.repeat(sgu_b[0].T, HEAD, axis=1)
    (h, y, pooled, xb), (w_out_f, ple_w_f, gate_w_f) = _forward_mixers(
        x2, w_in_f, pool_w_f, pool_scale, sgu_ln_g, sgu_ln_b, sgu_w[0], bias_tile, 2 * TOK_TILE,
        sides=(_weight_gather(shards, ("w_out", "ple_w", "gate_w")),))

    dxr, xn, dgp, dpe, dr, loss, d_ln_g, d_ln_b, d_gate_b = _head_fwd_bwd(
        x2, y, p2, target, w_out_f, gate_w_f, ple_w_f, ln_g, ln_b, ple_gate_b, 2 * TOK_TILE, TOK_TILE)

    (d_w_out,), _ = _weight_grads([(y, dr, 1)], GRAD_TILE, "grad_w_out")
    w_out_blocks = d_w_out.reshape(N_DEV, D_MIX // N_DEV, D_MODEL)
    (d_gate_w, d_ple_w), (w_out_theirs,) = _weight_grads(
        [(xn, dgp, 1), (p2, dpe, N_DEV)], GRAD_TILE, "grad_gate_ple_w", sides=(_pair_side([w_out_blocks]),))
    gate_blocks = d_gate_w.reshape(N_DEV, D_MODEL // N_DEV, D_MODEL)

    dh, dx, d_pool_w, d_pool_scale, d_sgu_ln_g, d_sgu_ln_b, d_sgu_w, d_sgu_b = _mixers_bwd(
        h, dr, dxr, pooled, w_in_f, pool_w_f, pool_scale, sgu_ln_g, sgu_ln_b, sgu_w[0], bias_tile, w_out_f, TOK_TILE)

    pool_blocks = d_pool_w.reshape(N_DEV, len(POOL_WINDOWS) * pool_rows, POOL_GROUP)
    hosted_names = ("w_out", "ple_w", "gate_w", "pool_w")
    ple_sums, gate_sums, pool_sums = _pair_exchange_sum([d_ple_w, gate_blocks, pool_blocks], "pair_exchange_sum")
    hosted_sums = [_pair_sum(core, w_out_blocks, w_out_theirs, "pair_sum_w_out"), ple_sums, gate_sums, pool_sums]

    small = (d_pool_scale, d_sgu_ln_g, d_sgu_ln_b, d_sgu_w, d_sgu_b, d_ln_g, d_ln_b, d_gate_b)
    parts = small + (loss,)
    small_gather = _gather_side(parts, [jax.ShapeDtypeStruct((N_DEV,) + a.shape, F32) for a in parts],
                                [lambda ref, b: ref.at[b]] * len(parts), [_leading_halves(a.shape) for a in parts])
    cx, cy = lax.axis_index("x"), lax.axis_index("y")
    order = _row_order(cx, cy)
    w_in_own, w_in_others, side_out = _grad_w_in_reduced(
        order, xb, dh, GRAD_TILE, sides=(_chip_side([s_bf for _, s_bf in hosted_sums]), small_gather))
    hosted_others, gathered = side_out[:len(hosted_sums)], side_out[len(hosted_sums):]

    shard_of = {"w_in": (w_in, m_w_in, v_w_in), "pool_w": (pool_w, m_pool_w, v_pool_w),
                "w_out": (w_out, m_w_out, v_w_out), "ple_w": (ple_w, m_ple_w, v_ple_w),
                "gate_w": (ple_gate_w, m_ple_gate_w, v_ple_gate_w)}
    reduced = [(nm, chip, s_f32, oth) for nm, (s_f32, _), oth in zip(hosted_names, hosted_sums, hosted_others)]
    reduced.append(("w_in", jnp.zeros((1,), jnp.int32), w_in_own[None], w_in_others))
    big_out, final_items = {}, []
    for nm, which, s_f32, oth in reduced:
        w, m, v = shard_of[nm]
        two_d = s_f32.shape[1:]
        args = (s_f32, oth, w.reshape(two_d), m.reshape(two_d), v.reshape(two_d))
        if nm == "w_in":
            big_out[nm] = [r.reshape(w.shape) for r in _sum_adamw(which, *args, "adamw_" + nm)]
        else:
            final_items.append((nm, args))

    small_w = (pool_scale, sgu_ln_g, sgu_ln_b, sgu_w, sgu_b, ln_g, ln_b, ple_gate_b)
    small_m = (m_pool_scale, m_sgu_ln_g, m_sgu_ln_b, m_sgu_w, m_sgu_b, m_ln_g, m_ln_b, m_ple_gate_b)
    small_v = (v_pool_scale, v_sgu_ln_g, v_sgu_ln_b, v_sgu_w, v_sgu_b, v_ln_g, v_ln_b, v_ple_gate_b)
    natural = [[a.reshape(g.shape) for a, g in zip(group, small)] for group in (small_w, small_m, small_v)]
    shard_res, *res = _final_adamw(chip, [a for _, a in final_items], list(gathered), *natural)
    for (nm, _), out in zip(final_items, shard_res):
        big_out[nm] = [r.reshape(shard_of[nm][0].shape) for r in out]
    g_s, d_s, m_s, v_s = [[r.reshape(w.shape) for r, w in zip(kind, small_w)] for kind in res]
    total_loss = res[0][-1][0, 0]

    order = ("w_in", "pool_w", "pool_scale", "sgu_ln_g", "sgu_ln_b", "sgu_w", "sgu_b", "w_out", "ln_g", "ln_b",
             "ple_w", "ple_gate_w", "ple_gate_b")
    outs = [total_loss, dx.reshape(1, seq, D_MODEL)]
    for kind in range(4):
        for nm in order:
            key = "gate_w" if nm == "ple_gate_w" else nm
            if key in big_out:
                outs.append(big_out[key][kind])
            else:
                outs.append((g_s, d_s, m_s, v_s)[kind][SMALL_NAMES.index(nm)])
    return tuple(outs)
```

```python
from typing import Callable, NamedTuple

import numpy as np
import jax
import jax.numpy as jnp
from jax import lax
from jax.experimental import pallas as pl
from jax.experimental.pallas import tpu as pltpu

F32 = jnp.float32
BF16 = jnp.bfloat16

N_DEV = 8
D_MODEL = 1024
D_POOL = 1024
D_SGU = 1024
D_MIX = 2048
D_IN = 5120
D_PLE = 256
POOL_WINDOWS = (2, 4, 8, 16)
POOL_GROUP = 256
N_HEADS = 4
HEAD = 256
CHUNK = 128
HALO = 16
BAND_PAD = 128
ALPHA = 2.0 ** 0.25
LN_EPS = 1e-5
ADAM_LR, ADAM_B1, ADAM_B2, ADAM_EPS, ADAM_WD, ADAM_STEP = 0.001, 0.9, 0.999, 1e-08, 0.01, 10

U0, V0, Z0 = D_POOL, D_POOL + D_SGU, D_POOL + 2 * D_SGU
VMEM_LIMIT = 56 * 1024 * 1024
MESH = pl.DeviceIdType.MESH
ANY = pl.BlockSpec(memory_space=pl.ANY)
VMEM_FULL = pl.BlockSpec(memory_space=pltpu.VMEM)

_GELU_C0 = 0.7978845608028654
_GELU_C1 = 0.044715


def _gelu_cdf(x, x2):
    return 1.0 / (1.0 + jnp.exp(x * ((-2.0 * _GELU_C0) + (-2.0 * _GELU_C0 * _GELU_C1) * x2)))


def _gelu_and_grad(x):
    x2 = x * x
    cdf = _gelu_cdf(x, x2)
    g = x * cdf
    dg = cdf + g * (1.0 - cdf) * ((2.0 * _GELU_C0) + (6.0 * _GELU_C0 * _GELU_C1) * x2)
    return g, dg


def _gelu(x):
    t = jnp.tanh(_GELU_C0 * (x + _GELU_C1 * (x * x * x)))
    return x * (0.5 * (1.0 + t))


def _split_bf16(x):
    hi = x.astype(BF16)
    return hi, (x - hi.astype(F32)).astype(BF16)


def _band(tok_tile, window, transpose):
    t = np.arange(tok_tile)[:, None]
    s = np.arange(tok_tile + BAND_PAD)[None, :]
    d = (s - t) if transpose else (t + BAND_PAD - s)
    return ((d >= 0) & (d < window)).astype(np.float32)


def _bands(tok_tile, transpose):
    return jnp.asarray(np.stack([_band(tok_tile, w, transpose) for w in POOL_WINDOWS]), dtype=BF16)


def _sigmoid(x):
    return 1.0 / (1.0 + jnp.exp(-x))


def _dot(a, b):
    return jnp.dot(a, b, preferred_element_type=F32)


def _dot_nt(a, b):
    return lax.dot_general(a, b, (((1,), (1,)), ((), ())), preferred_element_type=F32)


def _dot_tn(a, b):
    return lax.dot_general(a, b, (((0,), (0,)), ((), ())), preferred_element_type=F32)


def _row_stats(x):
    mu = jnp.mean(x, axis=-1, keepdims=True)
    xc = x - mu
    var = jnp.mean(xc * xc, axis=-1, keepdims=True)
    rstd = lax.rsqrt(var + LN_EPS)
    return xc * rstd, rstd


def _ln_bwd(dxhat, xhat, rstd):
    m1 = jnp.mean(dxhat, axis=-1, keepdims=True)
    m2 = jnp.mean(dxhat * xhat, axis=-1, keepdims=True)
    return rstd * (dxhat - m1 - xhat * m2)


def _masked_sgu_w(sw_ref, hh):
    row = lax.broadcasted_iota(jnp.int32, (CHUNK, CHUNK), 0)
    col = lax.broadcasted_iota(jnp.int32, (CHUNK, CHUNK), 1)
    return jnp.where(row >= col, sw_ref[hh], 0.0)


def _inv_count(tile_index, tok_tile, window):
    tok = tile_index * tok_tile + lax.broadcasted_iota(jnp.int32, (tok_tile, 1), 0)
    return 1.0 / jnp.minimum(tok + 1, window).astype(F32)


def _params(**kw):
    return pltpu.CompilerParams(vmem_limit_bytes=VMEM_LIMIT, **kw)


def _forward_mixers(x, w_in, pool_w, pool_scale, sgu_ln_g, sgu_ln_b, sgu_w, sgu_bias_tile, tok_tile, sides=()):
    seq = x.shape[0]
    n_tiles = seq // tok_tile
    n_chunks = tok_tile // CHUNK
    side_refs = _SideRefs(sides, 8, 4, 2)

    def body(*refs):
        (x_ref, win_ref, pw_ref, ps_ref, lg_ref, lb_ref, sw_ref, sb_ref,
         hb_ref, y_ref, pooled_ref, xb_ref, aext_ref, h_ref) = side_refs.split(refs)
        i = pl.program_id(0)
        side_refs.emit(i == 0, i == n_tiles // 2, False, late=(i == (3 * n_tiles) // 4))

        @pl.when(i == 0)
        def _():
            aext_ref[0:HALO, :] = jnp.zeros((HALO, D_POOL), F32)

        xb = x_ref[...].astype(BF16)
        xb_ref[...] = xb
        for s in range(D_IN // 1024):
            cs = slice(s * 1024, (s + 1) * 1024)
            section = _dot(xb, win_ref[:, cs])
            h_ref[:, cs] = section
            if s >= 1:
                hb_ref[:, (s - 1) * 1024:s * 1024] = section.astype(BF16)

        aext_ref[HALO:HALO + tok_tile, :] = h_ref[:, 0:D_POOL]
        for g, window in enumerate(POOL_WINDOWS):
            cols = slice(g * POOL_GROUP, (g + 1) * POOL_GROUP)
            win = aext_ref[HALO:HALO + tok_tile, cols]
            for k in range(1, window):
                win = win + aext_ref[HALO - k:HALO - k + tok_tile, cols]
            pooled = win * _inv_count(i, tok_tile, window) - h_ref[:, cols]
            pb = pooled.astype(BF16)
            pooled_ref[:, cols] = pb
            mixed = _dot(pb, pw_ref[g])
            z = h_ref[:, Z0 + g * POOL_GROUP:Z0 + (g + 1) * POOL_GROUP]
            y_ref[:, cols] = (mixed * ps_ref[:, cols] * (z * _sigmoid(z))).astype(BF16)
        aext_ref[0:HALO, :] = aext_ref[tok_tile:tok_tile + HALO, :]

        for hh in range(N_HEADS):
            cols = slice(hh * HEAD, (hh + 1) * HEAD)
            swm = _masked_sgu_w(sw_ref, hh).astype(BF16)
            for n in range(n_chunks):
                rows = slice(n * CHUNK, (n + 1) * CHUNK)
                gu = _gelu(h_ref[rows, U0 + hh * HEAD:U0 + (hh + 1) * HEAD])
                gv = _gelu(h_ref[rows, V0 + hh * HEAD:V0 + (hh + 1) * HEAD])
                xhat, _ = _row_stats(gv)
                vln = xhat * lg_ref[:, cols] + lb_ref[:, cols]
                sv = _dot(swm, vln.astype(BF16)) + sb_ref[:, cols]
                z = h_ref[rows, Z0 + D_POOL + hh * HEAD:Z0 + D_POOL + (hh + 1) * HEAD]
                y_ref[rows, D_POOL + hh * HEAD:D_POOL + (hh + 1) * HEAD] = (
                    gu * sv * (z * _sigmoid(z))).astype(BF16)

        side_refs.emit(False, False, i == n_tiles - 1, late=False)

    tok = lambda width: pl.BlockSpec((tok_tile, width), lambda i: (i, 0))
    outs = pl.pallas_call(
        body, name="forward_mixers",
        grid=(n_tiles,),
        in_specs=[tok(D_MODEL)] + [VMEM_FULL] * 7 + side_refs.in_specs,
        out_specs=[tok(D_IN - D_POOL), tok(D_MIX), tok(D_POOL), tok(D_MODEL)] + side_refs.out_specs,
        out_shape=[jax.ShapeDtypeStruct((seq, D_IN - D_POOL), BF16), jax.ShapeDtypeStruct((seq, D_MIX), BF16),
                   jax.ShapeDtypeStruct((seq, D_POOL), BF16), jax.ShapeDtypeStruct((seq, D_MODEL), BF16)]
        + side_refs.out_shapes,
        scratch_shapes=[pltpu.VMEM((HALO + tok_tile, D_POOL), F32), pltpu.VMEM((tok_tile, D_IN), F32)]
        + side_refs.scratch,
        compiler_params=_params(dimension_semantics=("arbitrary",)),
    )(x, w_in, pool_w, pool_scale, sgu_ln_g, sgu_ln_b, sgu_w, sgu_bias_tile, *side_refs.inputs)
    return outs[:4], outs[4:]


def _head_fwd_bwd(x, y, p, target, w_out, gate_w, ple_w, ln_g, ln_b, gate_b, tok_tile, sub_tile):
    seq = x.shape[0]

    def body(x_ref, y_ref, p_ref, t_ref, wout_ref, gw_ref, plw_ref, lng_ref, lnb_ref, gb_ref,
             dy_ref, dxr_ref, xn_ref, dgp_ref, dpe_ref, dr_ref, loss_ref, dlng_ref, dlnb_ref, dgb_ref):
        i = pl.program_id(0)

        @pl.when(i == 0)
        def _():
            loss_ref[...] = jnp.zeros_like(loss_ref)
            dlng_ref[...] = jnp.zeros_like(dlng_ref)
            dlnb_ref[...] = jnp.zeros_like(dlnb_ref)
            dgb_ref[...] = jnp.zeros_like(dgb_ref)

        subs = [slice(s * sub_tile, (s + 1) * sub_tile) for s in range(tok_tile // sub_tile)]
        stats, xns, douts = [], [], []
        for rows in subs:
            r = ALPHA * x_ref[rows, :] + _dot(y_ref[rows, :], wout_ref[...])
            xhat, rstd = _row_stats(r)
            xn = xhat * lng_ref[...] + lnb_ref[...]
            xn_ref[rows, :] = xn.astype(BF16)
            stats.append((xhat, rstd))
            xns.append(xn)
        loss = jnp.zeros((1, 1), F32)
        dgb = jnp.zeros((1, D_MODEL), F32)
        for rows, xn in zip(subs, xns):
            gate = _sigmoid(_dot(xn_ref[rows, :], gw_ref[...]) + gb_ref[...])
            pe = _dot(p_ref[rows, :].astype(BF16), plw_ref[...])
            err = xn + gate * pe - t_ref[rows, :]
            loss = loss + jnp.sum(err * err, keepdims=True)
            dout = err * (1.0 / D_MODEL)
            dpe_ref[rows, :] = (dout * gate).astype(BF16)
            dgpre = dout * pe * gate * (1.0 - gate)
            dgb = dgb + jnp.sum(dgpre, axis=0, keepdims=True)
            dgp_ref[rows, :] = dgpre.astype(BF16)
            douts.append(dout)
        loss_ref[...] += (0.5 / D_MODEL) * loss
        dgb_ref[...] += dgb
        dlng = jnp.zeros((1, D_MODEL), F32)
        dlnb = jnp.zeros((1, D_MODEL), F32)
        for rows, (xhat, rstd), dout in zip(subs, stats, douts):
            dxn = dout + _dot_nt(dgp_ref[rows, :], gw_ref[...])
            dlng = dlng + jnp.sum(dxn * xhat, axis=0, keepdims=True)
            dlnb = dlnb + jnp.sum(dxn, axis=0, keepdims=True)
            dr = _ln_bwd(dxn * lng_ref[...], xhat, rstd)
            dxr_ref[rows, :] = ALPHA * dr
            dr_ref[rows, :] = dr.astype(BF16)
        dlng_ref[...] += dlng
        dlnb_ref[...] += dlnb
        for rows in subs:
            dy_ref[rows, :] = _dot_nt(dr_ref[rows, :], wout_ref[...])

    tok = lambda width: pl.BlockSpec((tok_tile, width), lambda i: (i, 0))
    acc = lambda width: pl.BlockSpec((1, width), lambda i: (0, 0))
    vec = jax.ShapeDtypeStruct((1, D_MODEL), F32)
    return pl.pallas_call(
        body, name="head_fwd_bwd",
        grid=(seq // tok_tile,),
        in_specs=[tok(D_MODEL), tok(D_MIX), tok(D_PLE), tok(D_MODEL),
                  VMEM_FULL, VMEM_FULL, VMEM_FULL, VMEM_FULL, VMEM_FULL, VMEM_FULL],
        out_specs=[tok(D_MIX), tok(D_MODEL), tok(D_MODEL), tok(D_MODEL), tok(D_MODEL), tok(D_MODEL),
                   acc(128), acc(D_MODEL), acc(D_MODEL), acc(D_MODEL)],
        out_shape=[jax.ShapeDtypeStruct((seq, D_MIX), F32), jax.ShapeDtypeStruct((seq, D_MODEL), F32),
                   jax.ShapeDtypeStruct((seq, D_MODEL), BF16), jax.ShapeDtypeStruct((seq, D_MODEL), BF16),
                   jax.ShapeDtypeStruct((seq, D_MODEL), BF16), jax.ShapeDtypeStruct((seq, D_MODEL), BF16),
                   jax.ShapeDtypeStruct((1, 128), F32), vec, vec, vec],
        compiler_params=_params(dimension_semantics=("arbitrary",)),
    )(x, y, p, target, w_out, gate_w, ple_w, ln_g, ln_b, gate_b)


def _mixers_bwd(h, dy, dxr, pooled, w_in, pool_w, pool_scale, sgu_ln_g, sgu_ln_b, sgu_w, sgu_bias_tile, tok_tile):
    seq = h.shape[0]
    n_tiles = seq // tok_tile
    n_chunks = tok_tile // CHUNK
    pool_rows = POOL_GROUP // N_DEV

    def body(h_ref, dy_ref, dxr_ref, pooled_ref, win_ref, pw_ref, ps_ref, lg_ref, lb_ref, sw_ref, sb_ref, band_ref,
             dh_ref, dx_ref, dpw_ref, dps_ref, dlg_ref, dlb_ref, dsw_ref, dsb_ref,
             qhi_ref, qlo_ref, dpw_acc, dsb_acc):
        i = pl.program_id(0)
        tile = n_tiles - 1 - i

        @pl.when(i == 0)
        def _():
            qhi_ref[...] = jnp.zeros_like(qhi_ref)
            qlo_ref[...] = jnp.zeros_like(qlo_ref)
            dpw_acc[...] = jnp.zeros_like(dpw_acc)
            dsb_acc[...] = jnp.zeros_like(dsb_acc)
            dps_ref[...] = jnp.zeros_like(dps_ref)
            dlg_ref[...] = jnp.zeros_like(dlg_ref)
            dlb_ref[...] = jnp.zeros_like(dlb_ref)
            dsw_ref[...] = jnp.zeros_like(dsw_ref)

        def h_at(rows, cols):
            return h_ref[rows, cols.start - D_POOL:cols.stop - D_POOL].astype(F32)

        everything = slice(0, tok_tile)
        for part in (qhi_ref, qlo_ref):
            part[tok_tile:tok_tile + HALO, :] = part[0:HALO, :]
        for g, window in enumerate(POOL_WINDOWS):
            cols = slice(g * POOL_GROUP, (g + 1) * POOL_GROUP)
            zcols = slice(Z0 + g * POOL_GROUP, Z0 + (g + 1) * POOL_GROUP)
            z = h_at(everything, zcols)
            sz = _sigmoid(z)
            pb = pooled_ref[:, cols]
            mixed = _dot(pb, pw_ref[g])
            dyp = dy_ref[:, cols]
            dh_ref[:, zcols] = (dyp * (mixed * ps_ref[:, cols]) * (sz * (1.0 + z * (1.0 - sz)))).astype(BF16)
            dms = dyp * (z * sz)
            dps_ref[:, cols] += jnp.sum(dms * mixed, axis=0, keepdims=True)
            dmixed = (dms * ps_ref[:, cols]).astype(BF16)
            dpw_acc[g] += _dot_tn(pb, dmixed)
            dpooled = _dot_nt(dmixed, pw_ref[g])
            qhi_ref[everything, cols], qlo_ref[everything, cols] = _split_bf16(
                dpooled * _inv_count(tile, tok_tile, window))
            da = _dot(band_ref[g], qhi_ref[:, cols]) + _dot(band_ref[g], qlo_ref[:, cols]) - dpooled
            dh_ref[:, cols] = da.astype(BF16)

        pool_cols = [slice(o + g * POOL_GROUP, o + (g + 1) * POOL_GROUP)
                     for o in (0, Z0) for g in range(len(POOL_WINDOWS))]
        pool_slices = [pool_cols[0:3], pool_cols[3:6], pool_cols[6:8], []]
        for hh in range(N_HEADS):
            cols = slice(hh * HEAD, (hh + 1) * HEAD)
            ucols = slice(U0 + hh * HEAD, U0 + (hh + 1) * HEAD)
            vcols = slice(V0 + hh * HEAD, V0 + (hh + 1) * HEAD)
            zcols = slice(Z0 + D_POOL + hh * HEAD, Z0 + D_POOL + (hh + 1) * HEAD)
            sw32 = _masked_sgu_w(sw_ref, hh)
            swm = sw32.astype(BF16)
            swm_t = sw32.T.astype(BF16)
            for n in range(n_chunks):
                rows = slice(n * CHUNK, (n + 1) * CHUNK)
                gu, dgu_du = _gelu_and_grad(h_at(rows, ucols))
                gv, dgv_dv = _gelu_and_grad(h_at(rows, vcols))
                xhat, rstd = _row_stats(gv)
                vb = (xhat * lg_ref[:, cols] + lb_ref[:, cols]).astype(BF16)
                sv = _dot(swm, vb) + sb_ref[:, cols]
                z = h_at(rows, zcols)
                sz = _sigmoid(z)
                dys = dy_ref[rows, D_POOL + hh * HEAD:D_POOL + (hh + 1) * HEAD]
                dh_ref[rows, zcols] = (dys * (gu * sv) * (sz * (1.0 + z * (1.0 - sz)))).astype(BF16)
                dyg = dys * (z * sz)
                dh_ref[rows, ucols] = (dyg * sv * dgu_du).astype(BF16)
                dsv = dyg * gu
                dsb_acc[:, cols] += dsv
                dsvb = dsv.astype(BF16)
                dsw_ref[hh] += _dot_nt(dsvb, vb)
                dvln = _dot(swm_t, dsvb)
                dlg_ref[:, cols] += jnp.sum(dvln * xhat, axis=0, keepdims=True)
                dlb_ref[:, cols] += jnp.sum(dvln, axis=0, keepdims=True)
                dgv = _ln_bwd(dvln * lg_ref[:, cols], xhat, rstd)
                dh_ref[rows, vcols] = (dgv * dgv_dv).astype(BF16)
            ready = [ucols, vcols, zcols] + pool_slices[hh]
            part = _dot_nt(dh_ref[:, ready[0]], win_ref[:, ready[0]])
            for sl in ready[1:]:
                part = part + _dot_nt(dh_ref[:, sl], win_ref[:, sl])
            if hh == 0:
                dx_ref[...] = dxr_ref[...] + part
            else:
                dx_ref[...] += part

        @pl.when(i == n_tiles - 1)
        def _():
            for g in range(len(POOL_WINDOWS)):
                for b in range(N_DEV):
                    dpw_ref[b, g] = dpw_acc[g, b * pool_rows:(b + 1) * pool_rows, :]
            row = lax.broadcasted_iota(jnp.int32, (CHUNK, CHUNK), 0)
            col = lax.broadcasted_iota(jnp.int32, (CHUNK, CHUNK), 1)
            for hh in range(N_HEADS):
                dsw_ref[hh] = jnp.where(row >= col, dsw_ref[hh], 0.0)
                total = jnp.sum(dsb_acc[:, hh * HEAD:(hh + 1) * HEAD], axis=1, keepdims=True)
                dsb_ref[hh:hh + 1, :] = jnp.broadcast_to(total, (CHUNK, CHUNK)).T[0:1, :]

    tok = lambda width: pl.BlockSpec((tok_tile, width), lambda i: (n_tiles - 1 - i, 0))
    whole = lambda shape: pl.BlockSpec(shape, lambda i: (0,) * len(shape))
    vec = jax.ShapeDtypeStruct((1, D_MODEL), F32)
    return pl.pallas_call(
        body, name="mixers_bwd",
        grid=(n_tiles,),
        in_specs=[tok(D_IN - D_POOL), tok(D_MIX), tok(D_MODEL), tok(D_POOL)] + [VMEM_FULL] * 8,
        out_specs=[tok(D_IN), tok(D_MODEL), whole((N_DEV, len(POOL_WINDOWS), pool_rows, POOL_GROUP)),
                   whole((1, D_POOL)), whole((1, D_SGU)), whole((1, D_SGU)),
                   whole((N_HEADS, CHUNK, CHUNK)), whole((N_HEADS, CHUNK))],
        out_shape=[jax.ShapeDtypeStruct((seq, D_IN), BF16), jax.ShapeDtypeStruct((seq, D_MODEL), F32),
                   jax.ShapeDtypeStruct((N_DEV, len(POOL_WINDOWS), pool_rows, POOL_GROUP), F32),
                   vec, vec, vec,
                   jax.ShapeDtypeStruct((N_HEADS, CHUNK, CHUNK), F32),
                   jax.ShapeDtypeStruct((N_HEADS, CHUNK), F32)],
        scratch_shapes=[pltpu.VMEM((tok_tile + BAND_PAD, D_POOL), BF16),
                        pltpu.VMEM((tok_tile + BAND_PAD, D_POOL), BF16),
                        pltpu.VMEM((len(POOL_WINDOWS), POOL_GROUP, POOL_GROUP), F32),
                        pltpu.VMEM((CHUNK, D_SGU), F32)],
        compiler_params=_params(dimension_semantics=("arbitrary",)),
    )(h, dy, dxr, pooled, w_in, pool_w, pool_scale, sgu_ln_g, sgu_ln_b, sgu_w, sgu_bias_tile, _bands(tok_tile, True))


def _weight_grads(items, tok_tile, name, sides=()):
    n_items = len(items)
    seq = items[0][0].shape[0]
    n_k = seq // tok_tile
    side_refs = _SideRefs(sides, 2 * n_items, n_items, 0)

    def body(*refs):
        refs = side_refs.split(refs)
        k = pl.program_id(0)
        side_refs.emit(k == 0, k == n_k // 2, False)
        for t, (_, b, n_col_blocks) in enumerate(items):
            a_ref, b_ref, out_ref = refs[2 * t], refs[2 * t + 1], refs[2 * n_items + t]
            nb = b.shape[1] // n_col_blocks

            @pl.when(k == 0)
            def _():
                out_ref[...] = jnp.zeros_like(out_ref)

            res = _dot_tn(a_ref[...].astype(BF16), b_ref[...])
            for blk in range(n_col_blocks):
                out_ref[blk] += res[:, blk * nb:(blk + 1) * nb]
        side_refs.emit(False, False, k == n_k - 1)

    in_specs, operands, out_specs, out_shape = [], [], [], []
    for a, b, n_col_blocks in items:
        shape = (n_col_blocks, a.shape[1], b.shape[1] // n_col_blocks)
        in_specs += [pl.BlockSpec((tok_tile, a.shape[1]), lambda k: (k, 0)),
                     pl.BlockSpec((tok_tile, b.shape[1]), lambda k: (k, 0))]
        operands += [a, b]
        out_specs.append(pl.BlockSpec(shape, lambda k: (0, 0, 0)))
        out_shape.append(jax.ShapeDtypeStruct(shape, F32))
    outs = pl.pallas_call(
        body, name=name, grid=(n_k,),
        in_specs=in_specs + side_refs.in_specs, out_specs=out_specs + side_refs.out_specs,
        out_shape=out_shape + side_refs.out_shapes, scratch_shapes=side_refs.scratch,
        compiler_params=_params(dimension_semantics=("arbitrary",)),
    )(*operands, *side_refs.inputs)
    return outs[:n_items], outs[n_items:]


class _Side(NamedTuple):
    inputs: list
    out_shapes: list
    sem_shapes: list
    emit: Callable


def _when(cond):
    if cond is True:
        return lambda f: f()
    if cond is False:
        return lambda f: None
    return pl.when(cond)


def _place():
    return lax.axis_index("x"), lax.axis_index("y"), lax.axis_index("c")


def _other_chips(x, y):
    return [(1 - x, y), (x, 1 - y), (1 - x, 1 - y)]


def _gather_side(shards, out_shapes, views, halves):
    n = len(shards)
    n_sems = 10

    def emit(ins, outs, sems, first, mid, last, late=None):
        send_sems, recv_sems, local_sems = sems
        x, y, c = _place()
        here, x_nbr, y_nbr, diag = (x, y), (1 - x, y), (x, 1 - y), (1 - x, 1 - y)
        sibling = (x, y, 1 - c)

        def block(k, chip, core):
            return views[k](outs[k], 4 * chip[0] + 2 * chip[1] + core)

        def piece(k, ref, p):
            return ref if halves[k] is None else halves[k](ref)[p]

        def n_pieces(k):
            return 1 if halves[k] is None else 2

        def copy(k, s, src, dst, to):
            return pltpu.make_async_remote_copy(
                src_ref=src, dst_ref=dst, send_sem=send_sems.at[k, s], recv_sem=recv_sems.at[k, s],
                device_id=to, device_id_type=MESH)

        def outgoing(k, s):
            mine = block(k, here, c)
            if s == 0:
                return copy(k, 0, ins[k], mine, sibling)
            if s in (1, 2):
                return copy(k, s, piece(k, ins[k], s - 1), piece(k, mine, s - 1), (*x_nbr, c))
            if s in (3, 4):
                return copy(k, s, piece(k, ins[k], s - 3), piece(k, mine, s - 3), (*y_nbr, c))
            if s == 5:
                part = piece(k, block(k, x_nbr, c), 0)
                return copy(k, 5, part, part, (*y_nbr, c))
            if s == 6:
                part = piece(k, block(k, y_nbr, c), 1)
                return copy(k, 6, part, part, (*x_nbr, c))
            whole = block(k, (x_nbr, y_nbr, diag)[s - 7], c)
            return copy(k, s, whole, whole, sibling)

        def incoming(k, s):
            if s == 0:
                zone = block(k, here, 1 - c)
            elif s in (1, 2):
                zone = piece(k, block(k, x_nbr, c), s - 1)
            elif s in (3, 4):
                zone = piece(k, block(k, y_nbr, c), s - 3)
            elif s in (5, 6):
                zone = piece(k, block(k, diag, c), s - 5)
            else:
                zone = block(k, (x_nbr, y_nbr, diag)[s - 7], 1 - c)
            return copy(k, s, zone, zone, sibling)

        def own(k):
            return pltpu.make_async_copy(ins[k], block(k, here, c), local_sems.at[k])

        def used(k):
            return list(range(n_sems)) if n_pieces(k) == 2 else [0, 1, 3, 5, 7, 8, 9]

        @_when(first)
        def _():
            for k in range(n):
                own(k).start()
                for s in (0, 1, 4, 2, 3):
                    if s in used(k):
                        outgoing(k, s).start()

        @_when(mid)
        def _():
            for k in range(n):
                incoming(k, 1).wait_recv()
                outgoing(k, 5).start()
                if n_pieces(k) == 2:
                    incoming(k, 4).wait_recv()
                    outgoing(k, 6).start()
                    incoming(k, 2).wait_recv()
                outgoing(k, 7).start()
                incoming(k, 3).wait_recv()
                outgoing(k, 8).start()

        @_when(last if late is None else late)
        def _():
            for k in range(n):
                incoming(k, 5).wait_recv()
                if n_pieces(k) == 2:
                    incoming(k, 6).wait_recv()
                outgoing(k, 9).start()

        @_when(last)
        def _():
            for k in range(n):
                for s in (0, 7, 8, 9):
                    incoming(k, s).wait_recv()
            for k in range(n):
                for s in used(k):
                    outgoing(k, s).wait_send()
                own(k).wait()

    sems = [pltpu.SemaphoreType.DMA((n, n_sems)), pltpu.SemaphoreType.DMA((n, n_sems)),
            pltpu.SemaphoreType.DMA((n,))]
    return _Side(list(shards), list(out_shapes), sems, emit)


def _pair_side(grads):
    n = len(grads)

    def emit(ins, theirs, sems, first, mid, last, late=None):
        send_sems, recv_sems = sems
        x, y, c = _place()

        def copies():
            return [pltpu.make_async_remote_copy(
                src_ref=ins[k].at[2 * j + (1 - c)], dst_ref=theirs[k].at[j],
                send_sem=send_sems.at[k, j], recv_sem=recv_sems.at[k, j],
                device_id=(x, y, 1 - c), device_id_type=MESH) for k in range(n) for j in range(4)]

        @_when(first)
        def _():
            for cp in copies():
                cp.start()

        @_when(last)
        def _():
            for cp in copies():
                cp.wait_recv()
            for cp in copies():
                cp.wait_send()

    shapes = [jax.ShapeDtypeStruct((4,) + g.shape[1:], g.dtype) for g in grads]
    return _Side(list(grads), shapes, [pltpu.SemaphoreType.DMA((n, 4)), pltpu.SemaphoreType.DMA((n, 4))], emit)


def _chip_side(sums):
    n = len(sums)

    def emit(ins, others, sems, first, mid, last, late=None):
        send_sems, recv_sems = sems
        x, y, c = _place()

        def copies():
            return [pltpu.make_async_remote_copy(
                src_ref=ins[k].at[2 * px + py], dst_ref=others[k].at[r],
                send_sem=send_sems.at[k, r], recv_sem=recv_sems.at[k, r],
                device_id=(px, py, c), device_id_type=MESH)
                for k in range(n) for r, (px, py) in enumerate(_other_chips(x, y))]

        @_when(first)
        def _():
            for cp in copies():
                cp.start()

        @_when(last)
        def _():
            for cp in copies():
                cp.wait_recv()
            for cp in copies():
                cp.wait_send()

    shapes = [jax.ShapeDtypeStruct((3,) + s.shape[1:], s.dtype) for s in sums]
    return _Side(list(sums), shapes, [pltpu.SemaphoreType.DMA((n, 3)), pltpu.SemaphoreType.DMA((n, 3))], emit)


def _comm_call(side, name):
    n_in, n_out = len(side.inputs), len(side.out_shapes)

    def body(*refs):
        side.emit(refs[:n_in], refs[n_in:n_in + n_out], refs[n_in + n_out:], True, True, True)

    return pl.pallas_call(
        body, name=name, in_specs=[ANY] * n_in, out_specs=[ANY] * n_out,
        out_shape=side.out_shapes, scratch_shapes=side.sem_shapes,
    )(*side.inputs)


class _SideRefs:
    def __init__(self, sides, n_in, n_out, n_scratch):
        self.sides, self.n_in, self.n_out, self.n_scratch = sides, n_in, n_out, n_scratch
        self.inputs = [a for s in sides for a in s.inputs]
        self.out_shapes = [o for s in sides for o in s.out_shapes]
        self.scratch = [m for s in sides for m in s.sem_shapes]
        self.in_specs = [ANY] * len(self.inputs)
        self.out_specs = [ANY] * len(self.out_shapes)

    def split(self, refs):
        refs = list(refs)
        n_side_in, n_side_out = len(self.inputs), len(self.out_shapes)
        ins, rest = refs[:self.n_in], refs[self.n_in:]
        side_in, rest = rest[:n_side_in], rest[n_side_in:]
        outs, rest = rest[:self.n_out], rest[self.n_out:]
        side_out, rest = rest[:n_side_out], rest[n_side_out:]
        scratch, side_sems = rest[:self.n_scratch], rest[self.n_scratch:]
        self._refs = (side_in, side_out, side_sems)
        return ins + outs + scratch

    def emit(self, first, mid, last, late=None):
        side_in, side_out, side_sems = self._refs
        for s in self.sides:
            a, b, m = len(s.inputs), len(s.out_shapes), len(s.sem_shapes)
            s.emit(side_in[:a], side_out[:b], side_sems[:m], first, mid, last, late)
            side_in, side_out, side_sems = side_in[a:], side_out[b:], side_sems[m:]


ROW_RELATIONS = (0, 1, 2)


def _row_order(x, y):
    chips = _other_chips(x, y)
    return jnp.stack([2 * px + py for px, py in [chips[r] for r in ROW_RELATIONS] + [(x, y)]]).astype(jnp.int32)


def _grad_w_in_reduced(order, a, b, tok_tile, sides=()):
    seq, m = a.shape
    nb = b.shape[1] // N_DEV
    n_k = seq // tok_tile
    n_rows = 4
    last_step = n_rows * n_k - 1
    assert n_k >= 3
    fetch_at, sum_at = 1, 1
    side_refs = _SideRefs(sides, 3, 3, 8)

    def body(*refs):
        (order_ref, a_ref, b_ref, own_ref, theirs_ref, others_ref,
         acc_ref, stage_ref, sumbf_ref, pair_send, pair_recv, ici_send, ici_recv, stage_sem) = side_refs.split(refs)
        j, k = pl.program_id(0), pl.program_id(1)
        step = j * n_k + k
        side_refs.emit(step == 0, step == (n_rows * n_k) // 2, False, late=(step == (3 * n_rows * n_k) // 4))
        x, y, c = _place()
        chips = _other_chips(x, y)

        def to_sibling(row):
            return pltpu.make_async_remote_copy(
                src_ref=acc_ref.at[row % 2, 1 - c], dst_ref=theirs_ref.at[row],
                send_sem=pair_send.at[row], recv_sem=pair_recv.at[row],
                device_id=(x, y, 1 - c), device_id_type=MESH)

        def to_owner(row):
            rel = ROW_RELATIONS[row]
            px, py = chips[rel]
            return pltpu.make_async_remote_copy(
                src_ref=sumbf_ref.at[row], dst_ref=others_ref.at[rel],
                send_sem=ici_send.at[rel], recv_sem=ici_recv.at[rel],
                device_id=(px, py, c), device_id_type=MESH)

        def staged(row):
            return pltpu.make_async_copy(theirs_ref.at[row], stage_ref, stage_sem.at[0])

        @pl.when(k == 0)
        def _():
            acc_ref[j % 2] = jnp.zeros((2, m, nb), F32)

        res = _dot_tn(a_ref[...].astype(BF16), b_ref[...])
        for blk in range(2):
            acc_ref[j % 2, blk] += res[:, blk * nb:(blk + 1) * nb]

        for row in range(n_rows):
            @pl.when((j == row) & (k == n_k - 1))
            def _():
                to_sibling(row).start()

            if row < n_rows - 1:
                @pl.when((j == row + 1) & (k == fetch_at))
                def _():
                    to_sibling(row).wait_recv()
                    staged(row).start()

                @pl.when((j == row + 1) & (k == sum_at))
                def _():
                    staged(row).wait()
                    to_sibling(row).wait_send()
                    sumbf_ref[row] = (acc_ref[row % 2, c] + stage_ref[...]).astype(BF16)
                    to_owner(row).start()

        @pl.when(step == last_step)
        def _():
            row = n_rows - 1
            to_sibling(row).wait_recv()
            staged(row).start()
            staged(row).wait()
            to_sibling(row).wait_send()
            own_ref[...] = acc_ref[row % 2, c] + stage_ref[...]
            for r in range(n_rows - 1):
                to_owner(r).wait_recv()
                to_owner(r).wait_send()

        side_refs.emit(False, False, step == last_step, late=False)

    block = jax.ShapeDtypeStruct((m, nb), F32)
    outs = pl.pallas_call(
        body, name="grad_w_in",
        grid_spec=pltpu.PrefetchScalarGridSpec(
            num_scalar_prefetch=1, grid=(n_rows, n_k),
            in_specs=[pl.BlockSpec((tok_tile, m), lambda j, k, order_ref: (k, 0)),
                      pl.BlockSpec((tok_tile, 2 * nb), lambda j, k, order_ref: (k, order_ref[j]))]
            + side_refs.in_specs,
            out_specs=[pl.BlockSpec((m, nb), lambda j, k, order_ref: (0, 0)), ANY, ANY] + side_refs.out_specs,
            scratch_shapes=[pltpu.VMEM((2, 2, m, nb), F32), pltpu.VMEM((m, nb), F32),
                            pltpu.VMEM((n_rows - 1, m, nb), BF16),
                            pltpu.SemaphoreType.DMA((n_rows,)), pltpu.SemaphoreType.DMA((n_rows,)),
                            pltpu.SemaphoreType.DMA((n_rows - 1,)), pltpu.SemaphoreType.DMA((n_rows - 1,)),
                            pltpu.SemaphoreType.DMA((1,))] + side_refs.scratch),
        out_shape=[block, jax.ShapeDtypeStruct((n_rows, m, nb), F32),
                   jax.ShapeDtypeStruct((n_rows - 1, m, nb), BF16)] + side_refs.out_shapes,
        compiler_params=_params(dimension_semantics=("arbitrary", "arbitrary")),
    )(order, a, b, *side_refs.inputs)
    return outs[0], outs[2], outs[3:]


def _row_tile(rows, cols):
    tile = rows
    while tile * cols > 256 * 1024 and tile % 16 == 0:
        tile //= 2
    return tile


def _pair_sum(core, grads, theirs, name):
    _, rows, cols = theirs.shape
    rt = _row_tile(rows, cols)

    def body(core_ref, a_ref, b_ref, o_ref, ob_ref):
        total = a_ref[...] + b_ref[...]
        o_ref[...] = total
        ob_ref[...] = total.astype(BF16)

    spec = pl.BlockSpec((None, rt, cols), lambda j, i, core_ref: (j, i, 0))
    mine = pl.BlockSpec((None, None, rt, cols), lambda j, i, core_ref: (j, core_ref[0], i, 0))
    return pl.pallas_call(
        body, name=name,
        grid_spec=pltpu.PrefetchScalarGridSpec(
            num_scalar_prefetch=1, grid=(4, rows // rt), in_specs=[mine, spec], out_specs=[spec, spec]),
        out_shape=[jax.ShapeDtypeStruct(theirs.shape, F32), jax.ShapeDtypeStruct(theirs.shape, BF16)],
        compiler_params=_params(dimension_semantics=("arbitrary", "arbitrary")),
    )(core, grads.reshape(4, 2, rows, cols), theirs)


def _adamw(w, g, m, v):
    m = ADAM_B1 * m + (1.0 - ADAM_B1) * g
    v = ADAM_B2 * v + (1.0 - ADAM_B2) * (g * g)
    m_hat = m / (1.0 - ADAM_B1 ** ADAM_STEP)
    v_hat = v / (1.0 - ADAM_B2 ** ADAM_STEP)
    delta = -ADAM_LR * (m_hat / (jnp.sqrt(v_hat) + ADAM_EPS) + ADAM_WD * w)
    return delta, m, v


def _sum_adamw(chip, sums, others, w, m, v, name):
    _, rows, cols = sums.shape
    rt = _row_tile(rows, cols)

    def body(chip_ref, own_ref, oth_ref, w_ref, m_ref, v_ref, g_ref, d_ref, nm_ref, nv_ref):
        g = ((own_ref[...] + oth_ref[0].astype(F32)) + oth_ref[1].astype(F32)) + oth_ref[2].astype(F32)
        g_ref[...] = g
        d_ref[...], nm_ref[...], nv_ref[...] = _adamw(w_ref[...], g, m_ref[...], v_ref[...])

    spec = pl.BlockSpec((rt, cols), lambda i, chip_ref: (i, 0))
    own = pl.BlockSpec((None, rt, cols), lambda i, chip_ref: (chip_ref[0], i, 0))
    shape = jax.ShapeDtypeStruct((rows, cols), F32)
    return pl.pallas_call(
        body, name=name,
        grid_spec=pltpu.PrefetchScalarGridSpec(
            num_scalar_prefetch=1, grid=(rows // rt,),
            in_specs=[own, pl.BlockSpec((3, rt, cols), lambda i, chip_ref: (0, i, 0)), spec, spec, spec],
            out_specs=[spec] * 4),
        out_shape=[shape] * 4,
        compiler_params=_params(dimension_semantics=("arbitrary",)),
    )(chip, sums, others, w, m, v)


def _pair_exchange_sum(grads, name):
    n = len(grads)

    def body(*refs):
        ins, out32, outbf = refs[:n], refs[n:2 * n], refs[2 * n:3 * n]
        mine_v, theirs_v = refs[3 * n:4 * n], refs[4 * n:5 * n]
        send_sems, recv_sems, local_sems = refs[5 * n:]
        x, y, c = _place()
        remote = [pltpu.make_async_remote_copy(
            src_ref=ins[k].at[2 * j + (1 - c)], dst_ref=theirs_v[k].at[j],
            send_sem=send_sems.at[k, j], recv_sem=recv_sems.at[k, j],
            device_id=(x, y, 1 - c), device_id_type=MESH) for k in range(n) for j in range(4)]
        local = [pltpu.make_async_copy(ins[k].at[2 * j + c], mine_v[k].at[j], local_sems.at[k, j])
                 for k in range(n) for j in range(4)]
        for cp in remote + local:
            cp.start()
        for cp in local:
            cp.wait()
        for cp in remote:
            cp.wait_recv()
        for k in range(n):
            total = mine_v[k][...] + theirs_v[k][...]
            out32[k][...] = total
            outbf[k][...] = total.astype(BF16)
        for cp in remote:
            cp.wait_send()

    halves = [(4,) + g.shape[1:] for g in grads]
    outs = pl.pallas_call(
        body, name=name, in_specs=[ANY] * n, out_specs=[VMEM_FULL] * (2 * n),
        out_shape=[jax.ShapeDtypeStruct(h, F32) for h in halves] + [jax.ShapeDtypeStruct(h, BF16) for h in halves],
        scratch_shapes=[pltpu.VMEM(h, F32) for h in halves] * 2
        + [pltpu.SemaphoreType.DMA((n, 4))] * 3,
        compiler_params=_params(),
    )(*grads)
    return [(outs[k], outs[n + k]) for k in range(n)]


def _final_adamw(chip, items, gathered, ws, ms, vs):
    n_items, n = len(items), len(ws)

    def body(chip_ref, *refs):
        refs = list(refs)
        ins, refs = refs[:5 * n_items], refs[5 * n_items:]
        g8, refs = refs[:n + 1], refs[n + 1:]
        w, m, v, refs = refs[:n], refs[n:2 * n], refs[2 * n:3 * n], refs[3 * n:]
        outs, refs = refs[:4 * n_items], refs[4 * n_items:]
        g_out, d_out, m_out, v_out = refs[:n + 1], refs[n + 1:2 * n + 1], refs[2 * n + 1:3 * n + 1], refs[3 * n + 1:]
        for k in range(n_items):
            own_ref, oth_ref, w_ref, m_ref, v_ref = ins[5 * k:5 * k + 5]
            g = ((own_ref[...] + oth_ref[0].astype(F32)) + oth_ref[1].astype(F32)) + oth_ref[2].astype(F32)
            outs[4 * k][...] = g
            outs[4 * k + 1][...], outs[4 * k + 2][...], outs[4 * k + 3][...] = _adamw(
                w_ref[...], g, m_ref[...], v_ref[...])
        for k in range(n + 1):
            g = g8[k][0]
            for b in range(1, N_DEV):
                g = g + g8[k][b]
            g_out[k][...] = g
            if k < n:
                d_out[k][...], m_out[k][...], v_out[k][...] = _adamw(w[k][...], g, m[k][...], v[k][...])

    whole = lambda a: pl.BlockSpec(a.shape, lambda i, chip_ref: (0,) * len(a.shape))
    in_specs, operands, out_specs, out_shape = [], [], [], []
    for sums, others, w, m, v in items:
        in_specs += [pl.BlockSpec((None,) + sums.shape[1:], lambda i, chip_ref: (chip_ref[0], 0, 0)),
                     whole(others), whole(w), whole(m), whole(v)]
        operands += [sums, others, w, m, v]
        out_specs += [whole(w)] * 4
        out_shape += [jax.ShapeDtypeStruct(w.shape, F32)] * 4
    small_in = list(gathered) + list(ws) + list(ms) + list(vs)
    small_out = ([jax.ShapeDtypeStruct(w.shape, F32) for w in ws]
                 + [jax.ShapeDtypeStruct(gathered[-1].shape[1:], F32)]
                 + [jax.ShapeDtypeStruct(w.shape, F32) for w in ws] * 3)
    outs = pl.pallas_call(
        body, name="final_adamw",
        grid_spec=pltpu.PrefetchScalarGridSpec(
            num_scalar_prefetch=1, grid=(1,), in_specs=in_specs + [whole(a) for a in small_in],
            out_specs=out_specs + [whole(a) for a in small_out]),
        out_shape=out_shape + small_out, compiler_params=_params(dimension_semantics=("arbitrary",)),
    )(chip, *operands, *small_in)
    big, small = outs[:4 * n_items], outs[4 * n_items:]
    return ([tuple(big[4 * k:4 * k + 4]) for k in range(n_items)],
            small[:n + 1], small[n + 1:2 * n + 1], small[2 * n + 1:3 * n + 1], small[3 * n + 1:])


SMALL_NAMES = ("pool_scale", "sgu_ln_g", "sgu_ln_b", "sgu_w", "sgu_b", "ln_g", "ln_b", "ple_gate_b")


TOK_TILE = 256
GRAD_TILE = 1024


def _weight_views():
    cols = lambda width: (lambda ref, b: ref.at[:, pl.ds(pl.multiple_of(b * width, 128), width)])
    rows = lambda height: (lambda ref, b: ref.at[pl.ds(pl.multiple_of(b * height, 16), height), :])
    pool_rows = POOL_GROUP // N_DEV
    return {"w_in": cols(D_IN // N_DEV),
            "pool_w": lambda ref, b: ref.at[:, pl.ds(pl.multiple_of(b * pool_rows, 16), pool_rows), :],
            "w_out": rows(D_MIX // N_DEV), "ple_w": cols(D_MODEL // N_DEV), "gate_w": rows(D_MODEL // N_DEV)}


WEIGHT_SHAPES = {"w_in": (D_MODEL, D_IN), "pool_w": (len(POOL_WINDOWS), POOL_GROUP, POOL_GROUP),
                 "w_out": (D_MIX, D_MODEL), "ple_w": (D_PLE, D_MODEL), "gate_w": (D_MODEL, D_MODEL)}


def _to_bf16(arrays):
    def body(*refs):
        for src, dst in zip(refs[:len(arrays)], refs[len(arrays):]):
            dst[...] = src[...].astype(BF16)

    return pl.pallas_call(
        body, name="cast_shards", in_specs=[VMEM_FULL] * len(arrays), out_specs=[VMEM_FULL] * len(arrays),
        out_shape=[jax.ShapeDtypeStruct(a.shape, BF16) for a in arrays], compiler_params=_params(),
    )(*arrays)


def _leading_halves(shape):
    whole_tiles = len(shape) >= 3 or shape[0] % 32 == 0
    if shape[0] % 2 or not whole_tiles:
        return None
    half = shape[0] // 2
    return lambda ref: (ref.at[pl.ds(0, half)], ref.at[pl.ds(half, half)])


def _weight_gather(shards, names):
    views = _weight_views()
    return _gather_side([shards[nm] for nm in names],
                        [jax.ShapeDtypeStruct(WEIGHT_SHAPES[nm], BF16) for nm in names], [views[nm] for nm in names],
                        [_leading_halves(shards[nm].shape) for nm in names])


def kernel(x, p, w_in, pool_w, pool_scale, sgu_ln_g, sgu_ln_b, sgu_w, sgu_b, w_out, ln_g, ln_b, ple_w, ple_gate_w, ple_gate_b, loss_target, m_w_in, m_pool_w, m_pool_scale, m_sgu_ln_g, m_sgu_ln_b, m_sgu_w, m_sgu_b, m_w_out, m_ln_g, m_ln_b, m_ple_w, m_ple_gate_w, m_ple_gate_b, v_w_in, v_pool_w, v_pool_scale, v_sgu_ln_g, v_sgu_ln_b, v_sgu_w, v_sgu_b, v_w_out, v_ln_g, v_ln_b, v_ple_w, v_ple_gate_w, v_ple_gate_b):
    seq = x.shape[1]
    x2, p2, target = x[0], p[0, 0], loss_target[0]
    core = lax.axis_index("c").astype(jnp.int32).reshape(1)
    chip = (2 * lax.axis_index("x") + lax.axis_index("y")).astype(jnp.int32).reshape(1)
    pool_rows = POOL_GROUP // N_DEV

    shard_names = ("w_in", "pool_w", "w_out", "ple_w", "gate_w")
    shards = dict(zip(shard_names, _to_bf16([w_in[0], pool_w[0], w_out[0], ple_w[0], ple_gate_w[0]])))
    w_in_f, pool_w_f = _comm_call(_weight_gather(shards, ("w_in", "pool_w")), "gather_mixer_weights")
    bias_tile = jnp.repeat(sgu_b[0].T, HEAD, axis=1)
    (h, y, pooled, xb), (w_out_f, ple_w_f, gate_w_f) = _forward_mixers(
        x2, w_in_f, pool_w_f, pool_scale, sgu_ln_g, sgu_ln_b, sgu_w[0], bias_tile, 2 * TOK_TILE,
        sides=(_weight_gather(shards, ("w_out", "ple_w", "gate_w")),))

    dy, dxr, xn, dgp, dpe, dr, loss, d_ln_g, d_ln_b, d_gate_b = _head_fwd_bwd(
        x2, y, p2, target, w_out_f, gate_w_f, ple_w_f, ln_g, ln_b, ple_gate_b, 2 * TOK_TILE, TOK_TILE)

    (d_w_out,), _ = _weight_grads([(y, dr, 1)], GRAD_TILE, "grad_w_out")
    w_out_blocks = d_w_out.reshape(N_DEV, D_MIX // N_DEV, D_MODEL)
    (d_gate_w, d_ple_w), (w_out_theirs,) = _weight_grads(
        [(xn, dgp, 1), (p2, dpe, N_DEV)], GRAD_TILE, "grad_gate_ple_w", sides=(_pair_side([w_out_blocks]),))
    gate_blocks = d_gate_w.reshape(N_DEV, D_MODEL // N_DEV, D_MODEL)

    dh, dx, d_pool_w, d_pool_scale, d_sgu_ln_g, d_sgu_ln_b, d_sgu_w, d_sgu_b = _mixers_bwd(
        h, dy, dxr, pooled, w_in_f, pool_w_f, pool_scale, sgu_ln_g, sgu_ln_b, sgu_w[0], bias_tile, TOK_TILE)

    pool_blocks = d_pool_w.reshape(N_DEV, len(POOL_WINDOWS) * pool_rows, POOL_GROUP)
    hosted_names = ("w_out", "ple_w", "gate_w", "pool_w")
    ple_sums, gate_sums, pool_sums = _pair_exchange_sum([d_ple_w, gate_blocks, pool_blocks], "pair_exchange_sum")
    hosted_sums = [_pair_sum(core, w_out_blocks, w_out_theirs, "pair_sum_w_out"), ple_sums, gate_sums, pool_sums]

    small = (d_pool_scale, d_sgu_ln_g, d_sgu_ln_b, d_sgu_w, d_sgu_b, d_ln_g, d_ln_b, d_gate_b)
    parts = small + (loss,)
    small_gather = _gather_side(parts, [jax.ShapeDtypeStruct((N_DEV,) + a.shape, F32) for a in parts],
                                [lambda ref, b: ref.at[b]] * len(parts), [_leading_halves(a.shape) for a in parts])
    cx, cy = lax.axis_index("x"), lax.axis_index("y")
    order = _row_order(cx, cy)
    w_in_own, w_in_others, side_out = _grad_w_in_reduced(
        order, xb, dh, GRAD_TILE, sides=(_chip_side([s_bf for _, s_bf in hosted_sums]), small_gather))
    hosted_others, gathered = side_out[:len(hosted_sums)], side_out[len(hosted_sums):]

    shard_of = {"w_in": (w_in, m_w_in, v_w_in), "pool_w": (pool_w, m_pool_w, v_pool_w),
                "w_out": (w_out, m_w_out, v_w_out), "ple_w": (ple_w, m_ple_w, v_ple_w),
                "gate_w": (ple_gate_w, m_ple_gate_w, v_ple_gate_w)}
    reduced = [(nm, chip, s_f32, oth) for nm, (s_f32, _), oth in zip(hosted_names, hosted_sums, hosted_others)]
    reduced.append(("w_in", jnp.zeros((1,), jnp.int32), w_in_own[None], w_in_others))
    big_out, final_items = {}, []
    for nm, which, s_f32, oth in reduced:
        w, m, v = shard_of[nm]
        two_d = s_f32.shape[1:]
        args = (s_f32, oth, w.reshape(two_d), m.reshape(two_d), v.reshape(two_d))
        if nm == "w_in":
            big_out[nm] = [r.reshape(w.shape) for r in _sum_adamw(which, *args, "adamw_" + nm)]
        else:
            final_items.append((nm, args))

    small_w = (pool_scale, sgu_ln_g, sgu_ln_b, sgu_w, sgu_b, ln_g, ln_b, ple_gate_b)
    small_m = (m_pool_scale, m_sgu_ln_g, m_sgu_ln_b, m_sgu_w, m_sgu_b, m_ln_g, m_ln_b, m_ple_gate_b)
    small_v = (v_pool_scale, v_sgu_ln_g, v_sgu_ln_b, v_sgu_w, v_sgu_b, v_ln_g, v_ln_b, v_ple_gate_b)
    natural = [[a.reshape(g.shape) for a, g in zip(group, small)] for group in (small_w, small_m, small_v)]
    shard_res, *res = _final_adamw(chip, [a for _, a in final_items], list(gathered), *natural)
    for (nm, _), out in zip(final_items, shard_res):
        big_out[nm] = [r.reshape(shard_of[nm][0].shape) for r in out]
    g_s, d_s, m_s, v_s = [[r.reshape(w.shape) for r, w in zip(kind, small_w)] for kind in res]
    total_loss = res[0][-1][0, 0]

    order = ("w_in", "pool_w", "pool_scale", "sgu_ln_g", "sgu_ln_b", "sgu_w", "sgu_b", "w_out", "ln_g", "ln_b",
             "ple_w", "ple_gate_w", "ple_gate_b")
    outs = [total_loss, dx.reshape(1, seq, D_MODEL)]
    for kind in range(4):
        for nm in order:
            key = "gate_w" if nm == "ple_gate_w" else nm
            if key in big_out:
                outs.append(big_out[key][kind])
            else:
                outs.append((g_s, d_s, m_s, v_s)[kind][SMALL_NAMES.index(nm)])
    return tuple(outs)
```

```python
from typing import Callable, NamedTuple

import numpy as np
import jax
import jax.numpy as jnp
from jax import lax
from jax.experimental import pallas as pl
from jax.experimental.pallas import tpu as pltpu

F32 = jnp.float32
BF16 = jnp.bfloat16

N_DEV = 8
D_MODEL = 1024
D_POOL = 1024
D_SGU = 1024
D_MIX = 2048
D_IN = 5120
D_PLE = 256
POOL_WINDOWS = (2, 4, 8, 16)
POOL_GROUP = 256
N_HEADS = 4
HEAD = 256
CHUNK = 128
HALO = 16
BAND_PAD = 128
ALPHA = 2.0 ** 0.25
LN_EPS = 1e-5
ADAM_LR, ADAM_B1, ADAM_B2, ADAM_EPS, ADAM_WD, ADAM_STEP = 0.001, 0.9, 0.999, 1e-08, 0.01, 10

U0, V0, Z0 = D_POOL, D_POOL + D_SGU, D_POOL + 2 * D_SGU
VMEM_LIMIT = 56 * 1024 * 1024
MESH = pl.DeviceIdType.MESH
ANY = pl.BlockSpec(memory_space=pl.ANY)
VMEM_FULL = pl.BlockSpec(memory_space=pltpu.VMEM)

_GELU_C0 = 0.7978845608028654
_GELU_C1 = 0.044715


def _gelu_cdf(x, x2):
    return 1.0 / (1.0 + jnp.exp(x * ((-2.0 * _GELU_C0) + (-2.0 * _GELU_C0 * _GELU_C1) * x2)))


def _gelu_and_grad(x):
    x2 = x * x
    cdf = _gelu_cdf(x, x2)
    g = x * cdf
    dg = cdf + g * (1.0 - cdf) * ((2.0 * _GELU_C0) + (6.0 * _GELU_C0 * _GELU_C1) * x2)
    return g, dg


def _gelu(x):
    t = jnp.tanh(_GELU_C0 * (x + _GELU_C1 * (x * x * x)))
    return x * (0.5 * (1.0 + t))


def _split_bf16(x):
    hi = x.astype(BF16)
    return hi, (x - hi.astype(F32)).astype(BF16)


def _band(tok_tile, window, transpose):
    t = np.arange(tok_tile)[:, None]
    s = np.arange(tok_tile + BAND_PAD)[None, :]
    d = (s - t) if transpose else (t + BAND_PAD - s)
    return ((d >= 0) & (d < window)).astype(np.float32)


def _bands(tok_tile, transpose):
    return jnp.asarray(np.stack([_band(tok_tile, w, transpose) for w in POOL_WINDOWS]), dtype=BF16)


def _sigmoid(x):
    return 1.0 / (1.0 + jnp.exp(-x))


def _dot(a, b):
    return jnp.dot(a, b, preferred_element_type=F32)


def _dot_nt(a, b):
    return lax.dot_general(a, b, (((1,), (1,)), ((), ())), preferred_element_type=F32)


def _dot_tn(a, b):
    return lax.dot_general(a, b, (((0,), (0,)), ((), ())), preferred_element_type=F32)


def _row_stats(x):
    mu = jnp.mean(x, axis=-1, keepdims=True)
    xc = x - mu
    var = jnp.mean(xc * xc, axis=-1, keepdims=True)
    rstd = lax.rsqrt(var + LN_EPS)
    return xc * rstd, rstd


def _ln_bwd(dxhat, xhat, rstd):
    m1 = jnp.mean(dxhat, axis=-1, keepdims=True)
    m2 = jnp.mean(dxhat * xhat, axis=-1, keepdims=True)
    return rstd * (dxhat - m1 - xhat * m2)


def _masked_sgu_w(sw_ref, hh):
    row = lax.broadcasted_iota(jnp.int32, (CHUNK, CHUNK), 0)
    col = lax.broadcasted_iota(jnp.int32, (CHUNK, CHUNK), 1)
    return jnp.where(row >= col, sw_ref[hh], 0.0)


def _inv_count(tile_index, tok_tile, window):
    tok = tile_index * tok_tile + lax.broadcasted_iota(jnp.int32, (tok_tile, 1), 0)
    return 1.0 / jnp.minimum(tok + 1, window).astype(F32)


def _params(**kw):
    return pltpu.CompilerParams(vmem_limit_bytes=VMEM_LIMIT, **kw)


def _forward_mixers(x, w_in, pool_w, pool_scale, sgu_ln_g, sgu_ln_b, sgu_w, sgu_bias_tile, tok_tile, sides=()):
    seq = x.shape[0]
    n_tiles = seq // tok_tile
    n_chunks = tok_tile // CHUNK
    side_refs = _SideRefs(sides, 8, 4, 2)

    def body(*refs):
        (x_ref, win_ref, pw_ref, ps_ref, lg_ref, lb_ref, sw_ref, sb_ref,
         hb_ref, y_ref, pooled_ref, xb_ref, aext_ref, h_ref) = side_refs.split(refs)
        i = pl.program_id(0)
        side_refs.emit(i == 0, i == n_tiles // 2, False, late=(i == (3 * n_tiles) // 4))

        @pl.when(i == 0)
        def _():
            aext_ref[0:HALO, :] = jnp.zeros((HALO, D_POOL), F32)

        xb = x_ref[...].astype(BF16)
        xb_ref[...] = xb
        for s in range(D_IN // 1024):
            cs = slice(s * 1024, (s + 1) * 1024)
            section = _dot(xb, win_ref[:, cs])
            h_ref[:, cs] = section
            if s >= 1:
                hb_ref[:, (s - 1) * 1024:s * 1024] = section.astype(BF16)

        aext_ref[HALO:HALO + tok_tile, :] = h_ref[:, 0:D_POOL]
        for g, window in enumerate(POOL_WINDOWS):
            cols = slice(g * POOL_GROUP, (g + 1) * POOL_GROUP)
            win = aext_ref[HALO:HALO + tok_tile, cols]
            for k in range(1, window):
                win = win + aext_ref[HALO - k:HALO - k + tok_tile, cols]
            pooled = win * _inv_count(i, tok_tile, window) - h_ref[:, cols]
            pb = pooled.astype(BF16)
            pooled_ref[:, cols] = pb
            mixed = _dot(pb, pw_ref[g])
            z = h_ref[:, Z0 + g * POOL_GROUP:Z0 + (g + 1) * POOL_GROUP]
            y_ref[:, cols] = (mixed * ps_ref[:, cols] * (z * _sigmoid(z))).astype(BF16)
        aext_ref[0:HALO, :] = aext_ref[tok_tile:tok_tile + HALO, :]

        for hh in range(N_HEADS):
            cols = slice(hh * HEAD, (hh + 1) * HEAD)
            swm = _masked_sgu_w(sw_ref, hh).astype(BF16)
            for n in range(n_chunks):
                rows = slice(n * CHUNK, (n + 1) * CHUNK)
                gu = _gelu(h_ref[rows, U0 + hh * HEAD:U0 + (hh + 1) * HEAD])
                gv = _gelu(h_ref[rows, V0 + hh * HEAD:V0 + (hh + 1) * HEAD])
                xhat, _ = _row_stats(gv)
                vln = xhat * lg_ref[:, cols] + lb_ref[:, cols]
                sv = _dot(swm, vln.astype(BF16)) + sb_ref[:, cols]
                z = h_ref[rows, Z0 + D_POOL + hh * HEAD:Z0 + D_POOL + (hh + 1) * HEAD]
                y_ref[rows, D_POOL + hh * HEAD:D_POOL + (hh + 1) * HEAD] = (
                    gu * sv * (z * _sigmoid(z))).astype(BF16)

        side_refs.emit(False, False, i == n_tiles - 1, late=False)

    tok = lambda width: pl.BlockSpec((tok_tile, width), lambda i: (i, 0))
    outs = pl.pallas_call(
        body, name="forward_mixers",
        grid=(n_tiles,),
        in_specs=[tok(D_MODEL)] + [VMEM_FULL] * 7 + side_refs.in_specs,
        out_specs=[tok(D_IN - D_POOL), tok(D_MIX), tok(D_POOL), tok(D_MODEL)] + side_refs.out_specs,
        out_shape=[jax.ShapeDtypeStruct((seq, D_IN - D_POOL), BF16), jax.ShapeDtypeStruct((seq, D_MIX), BF16),
                   jax.ShapeDtypeStruct((seq, D_POOL), BF16), jax.ShapeDtypeStruct((seq, D_MODEL), BF16)]
        + side_refs.out_shapes,
        scratch_shapes=[pltpu.VMEM((HALO + tok_tile, D_POOL), F32), pltpu.VMEM((tok_tile, D_IN), F32)]
        + side_refs.scratch,
        compiler_params=_params(dimension_semantics=("arbitrary",)),
    )(x, w_in, pool_w, pool_scale, sgu_ln_g, sgu_ln_b, sgu_w, sgu_bias_tile, *side_refs.inputs)
    return outs[:4], outs[4:]


def _head_fwd_bwd(x, y, p, target, w_out, gate_w, ple_w, ln_g, ln_b, gate_b, tok_tile, sub_tile):
    seq = x.shape[0]

    def body(x_ref, y_ref, p_ref, t_ref, wout_ref, gw_ref, plw_ref, lng_ref, lnb_ref, gb_ref,
             dy_ref, dxr_ref, xn_ref, dgp_ref, dpe_ref, dr_ref, loss_ref, dlng_ref, dlnb_ref, dgb_ref):
        i = pl.program_id(0)

        @pl.when(i == 0)
        def _():
            loss_ref[...] = jnp.zeros_like(loss_ref)
            dlng_ref[...] = jnp.zeros_like(dlng_ref)
            dlnb_ref[...] = jnp.zeros_like(dlnb_ref)
            dgb_ref[...] = jnp.zeros_like(dgb_ref)

        subs = [slice(s * sub_tile, (s + 1) * sub_tile) for s in range(tok_tile // sub_tile)]
        stats, xns, douts = [], [], []
        for rows in subs:
            r = ALPHA * x_ref[rows, :] + _dot(y_ref[rows, :], wout_ref[...])
            xhat, rstd = _row_stats(r)
            xn = xhat * lng_ref[...] + lnb_ref[...]
            xn_ref[rows, :] = xn.astype(BF16)
            stats.append((xhat, rstd))
            xns.append(xn)
        loss = jnp.zeros((1, 1), F32)
        dgb = jnp.zeros((1, D_MODEL), F32)
        for rows, xn in zip(subs, xns):
            gate = _sigmoid(_dot(xn_ref[rows, :], gw_ref[...]) + gb_ref[...])
            pe = _dot(p_ref[rows, :].astype(BF16), plw_ref[...])
            err = xn + gate * pe - t_ref[rows, :]
            loss = loss + jnp.sum(err * err, keepdims=True)
            dout = err * (1.0 / D_MODEL)
            dpe_ref[rows, :] = (dout * gate).astype(BF16)
            dgpre = dout * pe * gate * (1.0 - gate)
            dgb = dgb + jnp.sum(dgpre, axis=0, keepdims=True)
            dgp_ref[rows, :] = dgpre.astype(BF16)
            douts.append(dout)
        loss_ref[...] += (0.5 / D_MODEL) * loss
        dgb_ref[...] += dgb
        dlng = jnp.zeros((1, D_MODEL), F32)
        dlnb = jnp.zeros((1, D_MODEL), F32)
        for rows, (xhat, rstd), dout in zip(subs, stats, douts):
            dxn = dout + _dot_nt(dgp_ref[rows, :], gw_ref[...])
            dlng = dlng + jnp.sum(dxn * xhat, axis=0, keepdims=True)
            dlnb = dlnb + jnp.sum(dxn, axis=0, keepdims=True)
            dr = _ln_bwd(dxn * lng_ref[...], xhat, rstd)
            dxr_ref[rows, :] = ALPHA * dr
            dr_ref[rows, :] = dr.astype(BF16)
        dlng_ref[...] += dlng
        dlnb_ref[...] += dlnb
        for rows in subs:
            dy_ref[rows, :] = _dot_nt(dr_ref[rows, :], wout_ref[...])

    tok = lambda width: pl.BlockSpec((tok_tile, width), lambda i: (i, 0))
    acc = lambda width: pl.BlockSpec((1, width), lambda i: (0, 0))
    vec = jax.ShapeDtypeStruct((1, D_MODEL), F32)
    return pl.pallas_call(
        body, name="head_fwd_bwd",
        grid=(seq // tok_tile,),
        in_specs=[tok(D_MODEL), tok(D_MIX), tok(D_PLE), tok(D_MODEL),
                  VMEM_FULL, VMEM_FULL, VMEM_FULL, VMEM_FULL, VMEM_FULL, VMEM_FULL],
        out_specs=[tok(D_MIX), tok(D_MODEL), tok(D_MODEL), tok(D_MODEL), tok(D_MODEL), tok(D_MODEL),
                   acc(128), acc(D_MODEL), acc(D_MODEL), acc(D_MODEL)],
        out_shape=[jax.ShapeDtypeStruct((seq, D_MIX), F32), jax.ShapeDtypeStruct((seq, D_MODEL), F32),
                   jax.ShapeDtypeStruct((seq, D_MODEL), BF16), jax.ShapeDtypeStruct((seq, D_MODEL), BF16),
                   jax.ShapeDtypeStruct((seq, D_MODEL), BF16), jax.ShapeDtypeStruct((seq, D_MODEL), BF16),
                   jax.ShapeDtypeStruct((1, 128), F32), vec, vec, vec],
        compiler_params=_params(dimension_semantics=("arbitrary",)),
    )(x, y, p, target, w_out, gate_w, ple_w, ln_g, ln_b, gate_b)


def _mixers_bwd(h, dy, dxr, pooled, w_in, pool_w, pool_scale, sgu_ln_g, sgu_ln_b, sgu_w, sgu_bias_tile, tok_tile):
    seq = h.shape[0]
    n_tiles = seq // tok_tile
    n_chunks = tok_tile // CHUNK
    pool_rows = POOL_GROUP // N_DEV

    def body(h_ref, dy_ref, dxr_ref, pooled_ref, win_ref, pw_ref, ps_ref, lg_ref, lb_ref, sw_ref, sb_ref, band_ref,
             dh_ref, dx_ref, dpw_ref, dps_ref, dlg_ref, dlb_ref, dsw_ref, dsb_ref,
             qhi_ref, qlo_ref, dpw_acc, dsb_acc):
        i = pl.program_id(0)
        tile = n_tiles - 1 - i

        @pl.when(i == 0)
        def _():
            qhi_ref[...] = jnp.zeros_like(qhi_ref)
            qlo_ref[...] = jnp.zeros_like(qlo_ref)
            dpw_acc[...] = jnp.zeros_like(dpw_acc)
            dsb_acc[...] = jnp.zeros_like(dsb_acc)
            dps_ref[...] = jnp.zeros_like(dps_ref)
            dlg_ref[...] = jnp.zeros_like(dlg_ref)
            dlb_ref[...] = jnp.zeros_like(dlb_ref)
            dsw_ref[...] = jnp.zeros_like(dsw_ref)

        def h_at(rows, cols):
            return h_ref[rows, cols.start - D_POOL:cols.stop - D_POOL].astype(F32)

        everything = slice(0, tok_tile)
        for part in (qhi_ref, qlo_ref):
            part[tok_tile:tok_tile + HALO, :] = part[0:HALO, :]
        for g, window in enumerate(POOL_WINDOWS):
            cols = slice(g * POOL_GROUP, (g + 1) * POOL_GROUP)
            zcols = slice(Z0 + g * POOL_GROUP, Z0 + (g + 1) * POOL_GROUP)
            z = h_at(everything, zcols)
            sz = _sigmoid(z)
            pb = pooled_ref[:, cols]
            mixed = _dot(pb, pw_ref[g])
            dyp = dy_ref[:, cols]
            dh_ref[:, zcols] = (dyp * (mixed * ps_ref[:, cols]) * (sz * (1.0 + z * (1.0 - sz)))).astype(BF16)
            dms = dyp * (z * sz)
            dps_ref[:, cols] += jnp.sum(dms * mixed, axis=0, keepdims=True)
            dmixed = (dms * ps_ref[:, cols]).astype(BF16)
            dpw_acc[g] += _dot_tn(pb, dmixed)
            dpooled = _dot_nt(dmixed, pw_ref[g])
            qhi_ref[everything, cols], qlo_ref[everything, cols] = _split_bf16(
                dpooled * _inv_count(tile, tok_tile, window))
            da = _dot(band_ref[g], qhi_ref[:, cols]) + _dot(band_ref[g], qlo_ref[:, cols]) - dpooled
            dh_ref[:, cols] = da.astype(BF16)

        pool_cols = [slice(o + g * POOL_GROUP, o + (g + 1) * POOL_GROUP)
                     for o in (0, Z0) for g in range(len(POOL_WINDOWS))]
        pool_slices = [pool_cols[0:3], pool_cols[3:6], pool_cols[6:8], []]
        for hh in range(N_HEADS):
            cols = slice(hh * HEAD, (hh + 1) * HEAD)
            ucols = slice(U0 + hh * HEAD, U0 + (hh + 1) * HEAD)
            vcols = slice(V0 + hh * HEAD, V0 + (hh + 1) * HEAD)
            zcols = slice(Z0 + D_POOL + hh * HEAD, Z0 + D_POOL + (hh + 1) * HEAD)
            sw32 = _masked_sgu_w(sw_ref, hh)
            swm = sw32.astype(BF16)
            swm_t = sw32.T.astype(BF16)
            for n in range(n_chunks):
                rows = slice(n * CHUNK, (n + 1) * CHUNK)
                gu, dgu_du = _gelu_and_grad(h_at(rows, ucols))
                gv, dgv_dv = _gelu_and_grad(h_at(rows, vcols))
                xhat, rstd = _row_stats(gv)
                vb = (xhat * lg_ref[:, cols] + lb_ref[:, cols]).astype(BF16)
                sv = _dot(swm, vb) + sb_ref[:, cols]
                z = h_at(rows, zcols)
                sz = _sigmoid(z)
                dys = dy_ref[rows, D_POOL + hh * HEAD:D_POOL + (hh + 1) * HEAD]
                dh_ref[rows, zcols] = (dys * (gu * sv) * (sz * (1.0 + z * (1.0 - sz)))).astype(BF16)
                dyg = dys * (z * sz)
                dh_ref[rows, ucols] = (dyg * sv * dgu_du).astype(BF16)
                dsv = dyg * gu
                dsb_acc[:, cols] += dsv
                dsvb = dsv.astype(BF16)
                dsw_ref[hh] += _dot_nt(dsvb, vb)
                dvln = _dot(swm_t, dsvb)
                dlg_ref[:, cols] += jnp.sum(dvln * xhat, axis=0, keepdims=True)
                dlb_ref[:, cols] += jnp.sum(dvln, axis=0, keepdims=True)
                dgv = _ln_bwd(dvln * lg_ref[:, cols], xhat, rstd)
                dh_ref[rows, vcols] = (dgv * dgv_dv).astype(BF16)
            ready = [ucols, vcols, zcols] + pool_slices[hh]
            part = _dot_nt(dh_ref[:, ready[0]], win_ref[:, ready[0]])
            for sl in ready[1:]:
                part = part + _dot_nt(dh_ref[:, sl], win_ref[:, sl])
            if hh == 0:
                dx_ref[...] = dxr_ref[...] + part
            else:
                dx_ref[...] += part

        @pl.when(i == n_tiles - 1)
        def _():
            for g in range(len(POOL_WINDOWS)):
                for b in range(N_DEV):
                    dpw_ref[b, g] = dpw_acc[g, b * pool_rows:(b + 1) * pool_rows, :]
            row = lax.broadcasted_iota(jnp.int32, (CHUNK, CHUNK), 0)
            col = lax.broadcasted_iota(jnp.int32, (CHUNK, CHUNK), 1)
            for hh in range(N_HEADS):
                dsw_ref[hh] = jnp.where(row >= col, dsw_ref[hh], 0.0)
                total = jnp.sum(dsb_acc[:, hh * HEAD:(hh + 1) * HEAD], axis=1, keepdims=True)
                dsb_ref[hh:hh + 1, :] = jnp.broadcast_to(total, (CHUNK, CHUNK)).T[0:1, :]

    tok = lambda width: pl.BlockSpec((tok_tile, width), lambda i: (n_tiles - 1 - i, 0))
    whole = lambda shape: pl.BlockSpec(shape, lambda i: (0,) * len(shape))
    vec = jax.ShapeDtypeStruct((1, D_MODEL), F32)
    return pl.pallas_call(
        body, name="mixers_bwd",
        grid=(n_tiles,),
        in_specs=[tok(D_IN - D_POOL), tok(D_MIX), tok(D_MODEL), tok(D_POOL)] + [VMEM_FULL] * 8,
        out_specs=[tok(D_IN), tok(D_MODEL), whole((N_DEV, len(POOL_WINDOWS), pool_rows, POOL_GROUP)),
                   whole((1, D_POOL)), whole((1, D_SGU)), whole((1, D_SGU)),
                   whole((N_HEADS, CHUNK, CHUNK)), whole((N_HEADS, CHUNK))],
        out_shape=[jax.ShapeDtypeStruct((seq, D_IN), BF16), jax.ShapeDtypeStruct((seq, D_MODEL), F32),
                   jax.ShapeDtypeStruct((N_DEV, len(POOL_WINDOWS), pool_rows, POOL_GROUP), F32),
                   vec, vec, vec,
                   jax.ShapeDtypeStruct((N_HEADS, CHUNK, CHUNK), F32),
                   jax.ShapeDtypeStruct((N_HEADS, CHUNK), F32)],
        scratch_shapes=[pltpu.VMEM((tok_tile + BAND_PAD, D_POOL), BF16),
                        pltpu.VMEM((tok_tile + BAND_PAD, D_POOL), BF16),
                        pltpu.VMEM((len(POOL_WINDOWS), POOL_GROUP, POOL_GROUP), F32),
                        pltpu.VMEM((CHUNK, D_SGU), F32)],
        compiler_params=_params(dimension_semantics=("arbitrary",)),
    )(h, dy, dxr, pooled, w_in, pool_w, pool_scale, sgu_ln_g, sgu_ln_b, sgu_w, sgu_bias_tile, _bands(tok_tile, True))


def _weight_grads(items, tok_tile, name, sides=()):
    n_items = len(items)
    seq = items[0][0].shape[0]
    n_k = seq // tok_tile
    side_refs = _SideRefs(sides, 2 * n_items, n_items, 0)

    def body(*refs):
        refs = side_refs.split(refs)
        k = pl.program_id(0)
        side_refs.emit(k == 0, k == n_k // 2, False)
        for t, (_, b, n_col_blocks) in enumerate(items):
            a_ref, b_ref, out_ref = refs[2 * t], refs[2 * t + 1], refs[2 * n_items + t]
            nb = b.shape[1] // n_col_blocks

            @pl.when(k == 0)
            def _():
                out_ref[...] = jnp.zeros_like(out_ref)

            res = _dot_tn(a_ref[...].astype(BF16), b_ref[...])
            for blk in range(n_col_blocks):
                out_ref[blk] += res[:, blk * nb:(blk + 1) * nb]
        side_refs.emit(False, False, k == n_k - 1)

    in_specs, operands, out_specs, out_shape = [], [], [], []
    for a, b, n_col_blocks in items:
        shape = (n_col_blocks, a.shape[1], b.shape[1] // n_col_blocks)
        in_specs += [pl.BlockSpec((tok_tile, a.shape[1]), lambda k: (k, 0)),
                     pl.BlockSpec((tok_tile, b.shape[1]), lambda k: (k, 0))]
        operands += [a, b]
        out_specs.append(pl.BlockSpec(shape, lambda k: (0, 0, 0)))
        out_shape.append(jax.ShapeDtypeStruct(shape, F32))
    outs = pl.pallas_call(
        body, name=name, grid=(n_k,),
        in_specs=in_specs + side_refs.in_specs, out_specs=out_specs + side_refs.out_specs,
        out_shape=out_shape + side_refs.out_shapes, scratch_shapes=side_refs.scratch,
        compiler_params=_params(dimension_semantics=("arbitrary",)),
    )(*operands, *side_refs.inputs)
    return outs[:n_items], outs[n_items:]


class _Side(NamedTuple):
    inputs: list
    out_shapes: list
    sem_shapes: list
    emit: Callable


def _when(cond):
    if cond is True:
        return lambda f: f()
    if cond is False:
        return lambda f: None
    return pl.when(cond)


def _place():
    return lax.axis_index("x"), lax.axis_index("y"), lax.axis_index("c")


def _other_chips(x, y):
    return [(1 - x, y), (x, 1 - y), (1 - x, 1 - y)]


def _gather_side(shards, out_shapes, views, halves):
    n = len(shards)
    n_sems = 10

    def emit(ins, outs, sems, first, mid, last, late=None):
        send_sems, recv_sems, local_sems = sems
        x, y, c = _place()
        here, x_nbr, y_nbr, diag = (x, y), (1 - x, y), (x, 1 - y), (1 - x, 1 - y)
        sibling = (x, y, 1 - c)

        def block(k, chip, core):
            return views[k](outs[k], 4 * chip[0] + 2 * chip[1] + core)

        def piece(k, ref, p):
            return ref if halves[k] is None else halves[k](ref)[p]

        def n_pieces(k):
            return 1 if halves[k] is None else 2

        def copy(k, s, src, dst, to):
            return pltpu.make_async_remote_copy(
                src_ref=src, dst_ref=dst, send_sem=send_sems.at[k, s], recv_sem=recv_sems.at[k, s],
                device_id=to, device_id_type=MESH)

        def outgoing(k, s):
            mine = block(k, here, c)
            if s == 0:
                return copy(k, 0, ins[k], mine, sibling)
            if s in (1, 2):
                return copy(k, s, piece(k, ins[k], s - 1), piece(k, mine, s - 1), (*x_nbr, c))
            if s in (3, 4):
                return copy(k, s, piece(k, ins[k], s - 3), piece(k, mine, s - 3), (*y_nbr, c))
            if s == 5:
                part = piece(k, block(k, x_nbr, c), 0)
                return copy(k, 5, part, part, (*y_nbr, c))
            if s == 6:
                part = piece(k, block(k, y_nbr, c), 1)
                return copy(k, 6, part, part, (*x_nbr, c))
            whole = block(k, (x_nbr, y_nbr, diag)[s - 7], c)
            return copy(k, s, whole, whole, sibling)

        def incoming(k, s):
            if s == 0:
                zone = block(k, here, 1 - c)
            elif s in (1, 2):
                zone = piece(k, block(k, x_nbr, c), s - 1)
            elif s in (3, 4):
                zone = piece(k, block(k, y_nbr, c), s - 3)
            elif s in (5, 6):
                zone = piece(k, block(k, diag, c), s - 5)
            else:
                zone = block(k, (x_nbr, y_nbr, diag)[s - 7], 1 - c)
            return copy(k, s, zone, zone, sibling)

        def own(k):
            return pltpu.make_async_copy(ins[k], block(k, here, c), local_sems.at[k])

        def used(k):
            return list(range(n_sems)) if n_pieces(k) == 2 else [0, 1, 3, 5, 7, 8, 9]

        @_when(first)
        def _():
            for k in range(n):
                own(k).start()
                for s in (0, 1, 4, 2, 3):
                    if s in used(k):
                        outgoing(k, s).start()

        @_when(mid)
        def _():
            for k in range(n):
                incoming(k, 1).wait_recv()
                outgoing(k, 5).start()
                if n_pieces(k) == 2:
                    incoming(k, 4).wait_recv()
                    outgoing(k, 6).start()
                    incoming(k, 2).wait_recv()
                outgoing(k, 7).start()
                incoming(k, 3).wait_recv()
                outgoing(k, 8).start()

        @_when(last if late is None else late)
        def _():
            for k in range(n):
                incoming(k, 5).wait_recv()
                if n_pieces(k) == 2:
                    incoming(k, 6).wait_recv()
                outgoing(k, 9).start()

        @_when(last)
        def _():
            for k in range(n):
                for s in (0, 7, 8, 9):
                    incoming(k, s).wait_recv()
            for k in range(n):
                for s in used(k):
                    outgoing(k, s).wait_send()
                own(k).wait()

    sems = [pltpu.SemaphoreType.DMA((n, n_sems)), pltpu.SemaphoreType.DMA((n, n_sems)),
            pltpu.SemaphoreType.DMA((n,))]
    return _Side(list(shards), list(out_shapes), sems, emit)


def _direct_gather_side(parts):
    n = len(parts)

    def emit(ins, outs, sems, first, mid, last, late=None):
        send_sems, recv_sems, local_sems = sems
        x, y, c = _place()

        def peer(r):
            flip = lambda v, bit: 1 - v if bit else v
            return flip(x, r & 4), flip(y, r & 2), flip(c, r & 1)

        def outgoing(k, r):
            return pltpu.make_async_remote_copy(
                src_ref=ins[k], dst_ref=outs[k].at[4 * x + 2 * y + c],
                send_sem=send_sems.at[k, r - 1], recv_sem=recv_sems.at[k, r - 1],
                device_id=peer(r), device_id_type=MESH)

        def incoming(k, r):
            px, py, pc = peer(r)
            zone = outs[k].at[4 * px + 2 * py + pc]
            return pltpu.make_async_remote_copy(
                src_ref=zone, dst_ref=zone, send_sem=send_sems.at[k, r - 1], recv_sem=recv_sems.at[k, r - 1],
                device_id=peer(r), device_id_type=MESH)

        def own(k):
            return pltpu.make_async_copy(ins[k], outs[k].at[4 * x + 2 * y + c], local_sems.at[k])

        @_when(first)
        def _():
            for k in range(n):
                own(k).start()
                for r in range(1, N_DEV):
                    outgoing(k, r).start()

        @_when(last)
        def _():
            for k in range(n):
                for r in range(1, N_DEV):
                    incoming(k, r).wait_recv()
            for k in range(n):
                for r in range(1, N_DEV):
                    outgoing(k, r).wait_send()
                own(k).wait()

    shapes = [jax.ShapeDtypeStruct((N_DEV,) + a.shape, a.dtype) for a in parts]
    sems = [pltpu.SemaphoreType.DMA((n, N_DEV - 1)), pltpu.SemaphoreType.DMA((n, N_DEV - 1)),
            pltpu.SemaphoreType.DMA((n,))]
    return _Side(list(parts), shapes, sems, emit)


def _pair_side(grads):
    n = len(grads)

    def emit(ins, theirs, sems, first, mid, last, late=None):
        send_sems, recv_sems = sems
        x, y, c = _place()

        def copies():
            return [pltpu.make_async_remote_copy(
                src_ref=ins[k].at[2 * j + (1 - c)], dst_ref=theirs[k].at[j],
                send_sem=send_sems.at[k, j], recv_sem=recv_sems.at[k, j],
                device_id=(x, y, 1 - c), device_id_type=MESH) for k in range(n) for j in range(4)]

        @_when(first)
        def _():
            for cp in copies():
                cp.start()

        @_when(last)
        def _():
            for cp in copies():
                cp.wait_recv()
            for cp in copies():
                cp.wait_send()

    shapes = [jax.ShapeDtypeStruct((4,) + g.shape[1:], g.dtype) for g in grads]
    return _Side(list(grads), shapes, [pltpu.SemaphoreType.DMA((n, 4)), pltpu.SemaphoreType.DMA((n, 4))], emit)


def _chip_side(sums):
    n = len(sums)

    def emit(ins, others, sems, first, mid, last, late=None):
        send_sems, recv_sems = sems
        x, y, c = _place()

        def copies():
            return [pltpu.make_async_remote_copy(
                src_ref=ins[k].at[2 * px + py], dst_ref=others[k].at[r],
                send_sem=send_sems.at[k, r], recv_sem=recv_sems.at[k, r],
                device_id=(px, py, c), device_id_type=MESH)
                for k in range(n) for r, (px, py) in enumerate(_other_chips(x, y))]

        @_when(first)
        def _():
            for cp in copies():
                cp.start()

        @_when(last)
        def _():
            for cp in copies():
                cp.wait_recv()
            for cp in copies():
                cp.wait_send()

    shapes = [jax.ShapeDtypeStruct((3,) + s.shape[1:], s.dtype) for s in sums]
    return _Side(list(sums), shapes, [pltpu.SemaphoreType.DMA((n, 3)), pltpu.SemaphoreType.DMA((n, 3))], emit)


def _comm_call(side, name):
    n_in, n_out = len(side.inputs), len(side.out_shapes)

    def body(*refs):
        side.emit(refs[:n_in], refs[n_in:n_in + n_out], refs[n_in + n_out:], True, True, True)

    return pl.pallas_call(
        body, name=name, in_specs=[ANY] * n_in, out_specs=[ANY] * n_out,
        out_shape=side.out_shapes, scratch_shapes=side.sem_shapes,
    )(*side.inputs)


class _SideRefs:
    def __init__(self, sides, n_in, n_out, n_scratch):
        self.sides, self.n_in, self.n_out, self.n_scratch = sides, n_in, n_out, n_scratch
        self.inputs = [a for s in sides for a in s.inputs]
        self.out_shapes = [o for s in sides for o in s.out_shapes]
        self.scratch = [m for s in sides for m in s.sem_shapes]
        self.in_specs = [ANY] * len(self.inputs)
        self.out_specs = [ANY] * len(self.out_shapes)

    def split(self, refs):
        refs = list(refs)
        n_side_in, n_side_out = len(self.inputs), len(self.out_shapes)
        ins, rest = refs[:self.n_in], refs[self.n_in:]
        side_in, rest = rest[:n_side_in], rest[n_side_in:]
        outs, rest = rest[:self.n_out], rest[self.n_out:]
        side_out, rest = rest[:n_side_out], rest[n_side_out:]
        scratch, side_sems = rest[:self.n_scratch], rest[self.n_scratch:]
        self._refs = (side_in, side_out, side_sems)
        return ins + outs + scratch

    def emit(self, first, mid, last, late=None):
        side_in, side_out, side_sems = self._refs
        for s in self.sides:
            a, b, m = len(s.inputs), len(s.out_shapes), len(s.sem_shapes)
            s.emit(side_in[:a], side_out[:b], side_sems[:m], first, mid, last, late)
            side_in, side_out, side_sems = side_in[a:], side_out[b:], side_sems[m:]


ROW_RELATIONS = (0, 1, 2)


def _row_order(x, y):
    chips = _other_chips(x, y)
    return jnp.stack([2 * px + py for px, py in [chips[r] for r in ROW_RELATIONS] + [(x, y)]]).astype(jnp.int32)


def _grad_w_in_reduced(order, a, b, tok_tile, sides=()):
    seq, m = a.shape
    nb = b.shape[1] // N_DEV
    n_k = seq // tok_tile
    n_rows = 4
    last_step = n_rows * n_k - 1
    assert n_k >= 3
    fetch_at, sum_at = 1, 2
    side_refs = _SideRefs(sides, 3, 3, 8)

    def body(*refs):
        (order_ref, a_ref, b_ref, own_ref, theirs_ref, others_ref,
         acc_ref, stage_ref, sumbf_ref, pair_send, pair_recv, ici_send, ici_recv, stage_sem) = side_refs.split(refs)
        j, k = pl.program_id(0), pl.program_id(1)
        step = j * n_k + k
        side_refs.emit(step == 0, step == (n_rows * n_k) // 2, False, late=(step == (3 * n_rows * n_k) // 4))
        x, y, c = _place()
        chips = _other_chips(x, y)

        def to_sibling(row):
            return pltpu.make_async_remote_copy(
                src_ref=acc_ref.at[row % 2, 1 - c], dst_ref=theirs_ref.at[row],
                send_sem=pair_send.at[row], recv_sem=pair_recv.at[row],
                device_id=(x, y, 1 - c), device_id_type=MESH)

        def to_owner(row):
            rel = ROW_RELATIONS[row]
            px, py = chips[rel]
            return pltpu.make_async_remote_copy(
                src_ref=sumbf_ref.at[row], dst_ref=others_ref.at[rel],
                send_sem=ici_send.at[rel], recv_sem=ici_recv.at[rel],
                device_id=(px, py, c), device_id_type=MESH)

        def staged(row):
            return pltpu.make_async_copy(theirs_ref.at[row], stage_ref, stage_sem.at[0])

        @pl.when(k == 0)
        def _():
            acc_ref[j % 2] = jnp.zeros((2, m, nb), F32)

        res = _dot_tn(a_ref[...].astype(BF16), b_ref[...])
        for blk in range(2):
            acc_ref[j % 2, blk] += res[:, blk * nb:(blk + 1) * nb]

        for row in range(n_rows):
            @pl.when((j == row) & (k == n_k - 1))
            def _():
                to_sibling(row).start()

            if row < n_rows - 1:
                @pl.when((j == row + 1) & (k == fetch_at))
                def _():
                    to_sibling(row).wait_recv()
                    staged(row).start()

                @pl.when((j == row + 1) & (k == sum_at))
                def _():
                    staged(row).wait()
                    to_sibling(row).wait_send()
                    sumbf_ref[row] = (acc_ref[row % 2, c] + stage_ref[...]).astype(BF16)
                    to_owner(row).start()

        @pl.when(step == last_step)
        def _():
            row = n_rows - 1
            to_sibling(row).wait_recv()
            staged(row).start()
            staged(row).wait()
            to_sibling(row).wait_send()
            own_ref[...] = acc_ref[row % 2, c] + stage_ref[...]
            for r in range(n_rows - 1):
                to_owner(r).wait_recv()
                to_owner(r).wait_send()

        side_refs.emit(False, False, step == last_step, late=False)

    block = jax.ShapeDtypeStruct((m, nb), F32)
    outs = pl.pallas_call(
        body, name="grad_w_in",
        grid_spec=pltpu.PrefetchScalarGridSpec(
            num_scalar_prefetch=1, grid=(n_rows, n_k),
            in_specs=[pl.BlockSpec((tok_tile, m), lambda j, k, order_ref: (k, 0)),
                      pl.BlockSpec((tok_tile, 2 * nb), lambda j, k, order_ref: (k, order_ref[j]))]
            + side_refs.in_specs,
            out_specs=[pl.BlockSpec((m, nb), lambda j, k, order_ref: (0, 0)), ANY, ANY] + side_refs.out_specs,
            scratch_shapes=[pltpu.VMEM((2, 2, m, nb), F32), pltpu.VMEM((m, nb), F32),
                            pltpu.VMEM((n_rows - 1, m, nb), BF16),
                            pltpu.SemaphoreType.DMA((n_rows,)), pltpu.SemaphoreType.DMA((n_rows,)),
                            pltpu.SemaphoreType.DMA((n_rows - 1,)), pltpu.SemaphoreType.DMA((n_rows - 1,)),
                            pltpu.SemaphoreType.DMA((1,))] + side_refs.scratch),
        out_shape=[block, jax.ShapeDtypeStruct((n_rows, m, nb), F32),
                   jax.ShapeDtypeStruct((n_rows - 1, m, nb), BF16)] + side_refs.out_shapes,
        compiler_params=_params(dimension_semantics=("arbitrary", "arbitrary")),
    )(order, a, b, *side_refs.inputs)
    return outs[0], outs[2], outs[3:]


def _row_tile(rows, cols):
    tile = rows
    while tile * cols > 256 * 1024 and tile % 16 == 0:
        tile //= 2
    return tile


def _pair_sum(core, grads, theirs, name):
    _, rows, cols = theirs.shape
    rt = _row_tile(rows, cols)

    def body(core_ref, a_ref, b_ref, o_ref, ob_ref):
        total = a_ref[...] + b_ref[...]
        o_ref[...] = total
        ob_ref[...] = total.astype(BF16)

    spec = pl.BlockSpec((None, rt, cols), lambda j, i, core_ref: (j, i, 0))
    mine = pl.BlockSpec((None, None, rt, cols), lambda j, i, core_ref: (j, core_ref[0], i, 0))
    return pl.pallas_call(
        body, name=name,
        grid_spec=pltpu.PrefetchScalarGridSpec(
            num_scalar_prefetch=1, grid=(4, rows // rt), in_specs=[mine, spec], out_specs=[spec, spec]),
        out_shape=[jax.ShapeDtypeStruct(theirs.shape, F32), jax.ShapeDtypeStruct(theirs.shape, BF16)],
        compiler_params=_params(dimension_semantics=("arbitrary", "arbitrary")),
    )(core, grads.reshape(4, 2, rows, cols), theirs)


def _adamw(w, g, m, v):
    m = ADAM_B1 * m + (1.0 - ADAM_B1) * g
    v = ADAM_B2 * v + (1.0 - ADAM_B2) * (g * g)
    m_hat = m / (1.0 - ADAM_B1 ** ADAM_STEP)
    v_hat = v / (1.0 - ADAM_B2 ** ADAM_STEP)
    delta = -ADAM_LR * (m_hat / (jnp.sqrt(v_hat) + ADAM_EPS) + ADAM_WD * w)
    return delta, m, v


def _sum_adamw(chip, sums, others, w, m, v, name):
    _, rows, cols = sums.shape
    rt = _row_tile(rows, cols)

    def body(chip_ref, own_ref, oth_ref, w_ref, m_ref, v_ref, g_ref, d_ref, nm_ref, nv_ref):
        g = ((own_ref[...] + oth_ref[0].astype(F32)) + oth_ref[1].astype(F32)) + oth_ref[2].astype(F32)
        g_ref[...] = g
        d_ref[...], nm_ref[...], nv_ref[...] = _adamw(w_ref[...], g, m_ref[...], v_ref[...])

    spec = pl.BlockSpec((rt, cols), lambda i, chip_ref: (i, 0))
    own = pl.BlockSpec((None, rt, cols), lambda i, chip_ref: (chip_ref[0], i, 0))
    shape = jax.ShapeDtypeStruct((rows, cols), F32)
    return pl.pallas_call(
        body, name=name,
        grid_spec=pltpu.PrefetchScalarGridSpec(
            num_scalar_prefetch=1, grid=(rows // rt,),
            in_specs=[own, pl.BlockSpec((3, rt, cols), lambda i, chip_ref: (0, i, 0)), spec, spec, spec],
            out_specs=[spec] * 4),
        out_shape=[shape] * 4,
        compiler_params=_params(dimension_semantics=("arbitrary",)),
    )(chip, sums, others, w, m, v)


def _pair_exchange_sum(grads, name):
    n = len(grads)

    def body(*refs):
        ins, out32, outbf = refs[:n], refs[n:2 * n], refs[2 * n:3 * n]
        mine_v, theirs_v = refs[3 * n:4 * n], refs[4 * n:5 * n]
        send_sems, recv_sems, local_sems = refs[5 * n:]
        x, y, c = _place()
        remote = [pltpu.make_async_remote_copy(
            src_ref=ins[k].at[2 * j + (1 - c)], dst_ref=theirs_v[k].at[j],
            send_sem=send_sems.at[k, j], recv_sem=recv_sems.at[k, j],
            device_id=(x, y, 1 - c), device_id_type=MESH) for k in range(n) for j in range(4)]
        local = [pltpu.make_async_copy(ins[k].at[2 * j + c], mine_v[k].at[j], local_sems.at[k, j])
                 for k in range(n) for j in range(4)]
        for cp in remote + local:
            cp.start()
        for cp in local:
            cp.wait()
        for cp in remote:
            cp.wait_recv()
        for k in range(n):
            total = mine_v[k][...] + theirs_v[k][...]
            out32[k][...] = total
            outbf[k][...] = total.astype(BF16)
        for cp in remote:
            cp.wait_send()

    halves = [(4,) + g.shape[1:] for g in grads]
    outs = pl.pallas_call(
        body, name=name, in_specs=[ANY] * n, out_specs=[VMEM_FULL] * (2 * n),
        out_shape=[jax.ShapeDtypeStruct(h, F32) for h in halves] + [jax.ShapeDtypeStruct(h, BF16) for h in halves],
        scratch_shapes=[pltpu.VMEM(h, F32) for h in halves] * 2
        + [pltpu.SemaphoreType.DMA((n, 4))] * 3,
        compiler_params=_params(),
    )(*grads)
    return [(outs[k], outs[n + k]) for k in range(n)]


def _final_adamw(chip, items, gathered, ws, ms, vs):
    n_items, n = len(items), len(ws)

    def body(chip_ref, *refs):
        refs = list(refs)
        ins, refs = refs[:5 * n_items], refs[5 * n_items:]
        g8, refs = refs[:n + 1], refs[n + 1:]
        w, m, v, refs = refs[:n], refs[n:2 * n], refs[2 * n:3 * n], refs[3 * n:]
        outs, refs = refs[:4 * n_items], refs[4 * n_items:]
        g_out, d_out, m_out, v_out = refs[:n + 1], refs[n + 1:2 * n + 1], refs[2 * n + 1:3 * n + 1], refs[3 * n + 1:]
        for k in range(n_items):
            own_ref, oth_ref, w_ref, m_ref, v_ref = ins[5 * k:5 * k + 5]
            g = ((own_ref[...] + oth_ref[0].astype(F32)) + oth_ref[1].astype(F32)) + oth_ref[2].astype(F32)
            outs[4 * k][...] = g
            outs[4 * k + 1][...], outs[4 * k + 2][...], outs[4 * k + 3][...] = _adamw(
                w_ref[...], g, m_ref[...], v_ref[...])
        for k in range(n + 1):
            g = g8[k][0]
            for b in range(1, N_DEV):
                g = g + g8[k][b]
            g_out[k][...] = g
            if k < n:
                d_out[k][...], m_out[k][...], v_out[k][...] = _adamw(w[k][...], g, m[k][...], v[k][...])

    whole = lambda a: pl.BlockSpec(a.shape, lambda i, chip_ref: (0,) * len(a.shape))
    in_specs, operands, out_specs, out_shape = [], [], [], []
    for sums, others, w, m, v in items:
        in_specs += [pl.BlockSpec((None,) + sums.shape[1:], lambda i, chip_ref: (chip_ref[0], 0, 0)),
                     whole(others), whole(w), whole(m), whole(v)]
        operands += [sums, others, w, m, v]
        out_specs += [whole(w)] * 4
        out_shape += [jax.ShapeDtypeStruct(w.shape, F32)] * 4
    small_in = list(gathered) + list(ws) + list(ms) + list(vs)
    small_out = ([jax.ShapeDtypeStruct(w.shape, F32) for w in ws]
                 + [jax.ShapeDtypeStruct(gathered[-1].shape[1:], F32)]
                 + [jax.ShapeDtypeStruct(w.shape, F32) for w in ws] * 3)
    outs = pl.pallas_call(
        body, name="final_adamw",
        grid_spec=pltpu.PrefetchScalarGridSpec(
            num_scalar_prefetch=1, grid=(1,), in_specs=in_specs + [whole(a) for a in small_in],
            out_specs=out_specs + [whole(a) for a in small_out]),
        out_shape=out_shape + small_out, compiler_params=_params(dimension_semantics=("arbitrary",)),
    )(chip, *operands, *small_in)
    big, small = outs[:4 * n_items], outs[4 * n_items:]
    return ([tuple(big[4 * k:4 * k + 4]) for k in range(n_items)],
            small[:n + 1], small[n + 1:2 * n + 1], small[2 * n + 1:3 * n + 1], small[3 * n + 1:])


SMALL_NAMES = ("pool_scale", "sgu_ln_g", "sgu_ln_b", "sgu_w", "sgu_b", "ln_g", "ln_b", "ple_gate_b")


TOK_TILE = 256
GRAD_TILE = 1024


def _weight_views():
    cols = lambda width: (lambda ref, b: ref.at[:, pl.ds(pl.multiple_of(b * width, 128), width)])
    rows = lambda height: (lambda ref, b: ref.at[pl.ds(pl.multiple_of(b * height, 16), height), :])
    pool_rows = POOL_GROUP // N_DEV
    return {"w_in": cols(D_IN // N_DEV),
            "pool_w": lambda ref, b: ref.at[:, pl.ds(pl.multiple_of(b * pool_rows, 16), pool_rows), :],
            "w_out": rows(D_MIX // N_DEV), "ple_w": cols(D_MODEL // N_DEV), "gate_w": rows(D_MODEL // N_DEV)}


WEIGHT_SHAPES = {"w_in": (D_MODEL, D_IN), "pool_w": (len(POOL_WINDOWS), POOL_GROUP, POOL_GROUP),
                 "w_out": (D_MIX, D_MODEL), "ple_w": (D_PLE, D_MODEL), "gate_w": (D_MODEL, D_MODEL)}


def _to_bf16(arrays):
    def body(*refs):
        for src, dst in zip(refs[:len(arrays)], refs[len(arrays):]):
            dst[...] = src[...].astype(BF16)

    return pl.pallas_call(
        body, name="cast_shards", in_specs=[VMEM_FULL] * len(arrays), out_specs=[VMEM_FULL] * len(arrays),
        out_shape=[jax.ShapeDtypeStruct(a.shape, BF16) for a in arrays], compiler_params=_params(),
    )(*arrays)


def _leading_halves(shape):
    whole_tiles = len(shape) >= 3 or shape[0] % 32 == 0
    if shape[0] % 2 or not whole_tiles:
        return None
    half = shape[0] // 2
    return lambda ref: (ref.at[pl.ds(0, half)], ref.at[pl.ds(half, half)])


def _weight_gather(shards, names):
    views = _weight_views()
    return _gather_side([shards[nm] for nm in names],
                        [jax.ShapeDtypeStruct(WEIGHT_SHAPES[nm], BF16) for nm in names], [views[nm] for nm in names],
                        [_leading_halves(shards[nm].shape) for nm in names])


def kernel(x, p, w_in, pool_w, pool_scale, sgu_ln_g, sgu_ln_b, sgu_w, sgu_b, w_out, ln_g, ln_b, ple_w, ple_gate_w, ple_gate_b, loss_target, m_w_in, m_pool_w, m_pool_scale, m_sgu_ln_g, m_sgu_ln_b, m_sgu_w, m_sgu_b, m_w_out, m_ln_g, m_ln_b, m_ple_w, m_ple_gate_w, m_ple_gate_b, v_w_in, v_pool_w, v_pool_scale, v_sgu_ln_g, v_sgu_ln_b, v_sgu_w, v_sgu_b, v_w_out, v_ln_g, v_ln_b, v_ple_w, v_ple_gate_w, v_ple_gate_b):
    seq = x.shape[1]
    x2, p2, target = x[0], p[0, 0], loss_target[0]
    core = lax.axis_index("c").astype(jnp.int32).reshape(1)
    chip = (2 * lax.axis_index("x") + lax.axis_index("y")).astype(jnp.int32).reshape(1)
    pool_rows = POOL_GROUP // N_DEV

    shard_names = ("w_in", "pool_w", "w_out", "ple_w", "gate_w")
    shards = dict(zip(shard_names, _to_bf16([w_in[0], pool_w[0], w_out[0], ple_w[0], ple_gate_w[0]])))
    w_in_f, pool_w_f = _comm_call(_weight_gather(shards, ("w_in", "pool_w")), "gather_mixer_weights")
    bias_tile = jnp.repeat(sgu_b[0].T, HEAD, axis=1)
    (h, y, pooled, xb), (w_out_f, ple_w_f, gate_w_f) = _forward_mixers(
        x2, w_in_f, pool_w_f, pool_scale, sgu_ln_g, sgu_ln_b, sgu_w[0], bias_tile, 2 * TOK_TILE,
        sides=(_weight_gather(shards, ("w_out", "ple_w", "gate_w")),))

    dy, dxr, xn, dgp, dpe, dr, loss, d_ln_g, d_ln_b, d_gate_b = _head_fwd_bwd(
        x2, y, p2, target, w_out_f, gate_w_f, ple_w_f, ln_g, ln_b, ple_gate_b, 2 * TOK_TILE, TOK_TILE)

    (d_w_out,), _ = _weight_grads([(y, dr, 1)], GRAD_TILE, "grad_w_out")
    w_out_blocks = d_w_out.reshape(N_DEV, D_MIX // N_DEV, D_MODEL)
    (d_gate_w, d_ple_w), (w_out_theirs,) = _weight_grads(
        [(xn, dgp, 1), (p2, dpe, N_DEV)], GRAD_TILE, "grad_gate_ple_w", sides=(_pair_side([w_out_blocks]),))
    gate_blocks = d_gate_w.reshape(N_DEV, D_MODEL // N_DEV, D_MODEL)

    dh, dx, d_pool_w, d_pool_scale, d_sgu_ln_g, d_sgu_ln_b, d_sgu_w, d_sgu_b = _mixers_bwd(
        h, dy, dxr, pooled, w_in_f, pool_w_f, pool_scale, sgu_ln_g, sgu_ln_b, sgu_w[0], bias_tile, TOK_TILE)

    pool_blocks = d_pool_w.reshape(N_DEV, len(POOL_WINDOWS) * pool_rows, POOL_GROUP)
    hosted_names = ("w_out", "ple_w", "gate_w", "pool_w")
    ple_sums, gate_sums, pool_sums = _pair_exchange_sum([d_ple_w, gate_blocks, pool_blocks], "pair_exchange_sum")
    hosted_sums = [_pair_sum(core, w_out_blocks, w_out_theirs, "pair_sum_w_out"), ple_sums, gate_sums, pool_sums]

    small = (d_pool_scale, d_sgu_ln_g, d_sgu_ln_b, d_sgu_w, d_sgu_b, d_ln_g, d_ln_b, d_gate_b)
    parts = small + (loss,)
    small_gather = _direct_gather_side(list(parts))
    cx, cy = lax.axis_index("x"), lax.axis_index("y")
    order = _row_order(cx, cy)
    w_in_own, w_in_others, side_out = _grad_w_in_reduced(
        order, xb, dh, GRAD_TILE, sides=(_chip_side([s_bf for _, s_bf in hosted_sums]), small_gather))
    hosted_others, gathered = side_out[:len(hosted_sums)], side_out[len(hosted_sums):]

    shard_of = {"w_in": (w_in, m_w_in, v_w_in), "pool_w": (pool_w, m_pool_w, v_pool_w),
                "w_out": (w_out, m_w_out, v_w_out), "ple_w": (ple_w, m_ple_w, v_ple_w),
                "gate_w": (ple_gate_w, m_ple_gate_w, v_ple_gate_w)}
    reduced = [(nm, chip, s_f32, oth) for nm, (s_f32, _), oth in zip(hosted_names, hosted_sums, hosted_others)]
    reduced.append(("w_in", jnp.zeros((1,), jnp.int32), w_in_own[None], w_in_others))
    big_out, final_items = {}, []
    for nm, which, s_f32, oth in reduced:
        w, m, v = shard_of[nm]
        two_d = s_f32.shape[1:]
        args = (s_f32, oth, w.reshape(two_d), m.reshape(two_d), v.reshape(two_d))
        if nm == "w_in":
            big_out[nm] = [r.reshape(w.shape) for r in _sum_adamw(which, *args, "adamw_" + nm)]
        else:
            final_items.append((nm, args))

    small_w = (pool_scale, sgu_ln_g, sgu_ln_b, sgu_w, sgu_b, ln_g, ln_b, ple_gate_b)
    small_m = (m_pool_scale, m_sgu_ln_g, m_sgu_ln_b, m_sgu_w, m_sgu_b, m_ln_g, m_ln_b, m_ple_gate_b)
    small_v = (v_pool_scale, v_sgu_ln_g, v_sgu_ln_b, v_sgu_w, v_sgu_b, v_ln_g, v_ln_b, v_ple_gate_b)
    natural = [[a.reshape(g.shape) for a, g in zip(group, small)] for group in (small_w, small_m, small_v)]
    shard_res, *res = _final_adamw(chip, [a for _, a in final_items], list(gathered), *natural)
    for (nm, _), out in zip(final_items, shard_res):
        big_out[nm] = [r.reshape(shard_of[nm][0].shape) for r in out]
    g_s, d_s, m_s, v_s = [[r.reshape(w.shape) for r, w in zip(kind, small_w)] for kind in res]
    total_loss = res[0][-1][0, 0]

    order = ("w_in", "pool_w", "pool_scale", "sgu_ln_g", "sgu_ln_b", "sgu_w", "sgu_b", "w_out", "ln_g", "ln_b",
             "ple_w", "ple_gate_w", "ple_gate_b")
    outs = [total_loss, dx.reshape(1, seq, D_MODEL)]
    for kind in range(4):
        for nm in order:
            key = "gate_w" if nm == "ple_gate_w" else nm
            if key in big_out:
                outs.append(big_out[key][kind])
            else:
                outs.append((g_s, d_s, m_s, v_s)[kind][SMALL_NAMES.index(nm)])
    return tuple(outs)
```

```python
from typing import Callable, NamedTuple

import numpy as np
import jax
import jax.numpy as jnp
from jax import lax
from jax.experimental import pallas as pl
from jax.experimental.pallas import tpu as pltpu

F32 = jnp.float32
BF16 = jnp.bfloat16

N_DEV = 8
D_MODEL = 1024
D_POOL = 1024
D_SGU = 1024
D_MIX = 2048
D_IN = 5120
D_PLE = 256
POOL_WINDOWS = (2, 4, 8, 16)
POOL_GROUP = 256
N_HEADS = 4
HEAD = 256
CHUNK = 128
HALO = 16
BAND_PAD = 128
ALPHA = 2.0 ** 0.25
LN_EPS = 1e-5
ADAM_LR, ADAM_B1, ADAM_B2, ADAM_EPS, ADAM_WD, ADAM_STEP = 0.001, 0.9, 0.999, 1e-08, 0.01, 10

U0, V0, Z0 = D_POOL, D_POOL + D_SGU, D_POOL + 2 * D_SGU
VMEM_LIMIT = 56 * 1024 * 1024
MESH = pl.DeviceIdType.MESH
ANY = pl.BlockSpec(memory_space=pl.ANY)
VMEM_FULL = pl.BlockSpec(memory_space=pltpu.VMEM)

_GELU_C0 = 0.7978845608028654
_GELU_C1 = 0.044715


def _gelu_cdf(x, x2):
    return 1.0 / (1.0 + jnp.exp(x * ((-2.0 * _GELU_C0) + (-2.0 * _GELU_C0 * _GELU_C1) * x2)))


def _gelu_and_grad(x):
    x2 = x * x
    cdf = _gelu_cdf(x, x2)
    g = x * cdf
    dg = cdf + g * (1.0 - cdf) * ((2.0 * _GELU_C0) + (6.0 * _GELU_C0 * _GELU_C1) * x2)
    return g, dg


def _gelu(x):
    t = jnp.tanh(_GELU_C0 * (x + _GELU_C1 * (x * x * x)))
    return x * (0.5 * (1.0 + t))


def _split_bf16(x):
    hi = x.astype(BF16)
    return hi, (x - hi.astype(F32)).astype(BF16)


def _band(tok_tile, window, transpose):
    t = np.arange(tok_tile)[:, None]
    s = np.arange(tok_tile + BAND_PAD)[None, :]
    d = (s - t) if transpose else (t + BAND_PAD - s)
    return ((d >= 0) & (d < window)).astype(np.float32)


def _bands(tok_tile, transpose):
    return jnp.asarray(np.stack([_band(tok_tile, w, transpose) for w in POOL_WINDOWS]), dtype=BF16)


def _sigmoid(x):
    return 1.0 / (1.0 + jnp.exp(-x))


def _dot(a, b):
    return jnp.dot(a, b, preferred_element_type=F32)


def _dot_nt(a, b):
    return lax.dot_general(a, b, (((1,), (1,)), ((), ())), preferred_element_type=F32)


def _dot_tn(a, b):
    return lax.dot_general(a, b, (((0,), (0,)), ((), ())), preferred_element_type=F32)


def _row_stats(x):
    mu = jnp.mean(x, axis=-1, keepdims=True)
    xc = x - mu
    var = jnp.mean(xc * xc, axis=-1, keepdims=True)
    rstd = lax.rsqrt(var + LN_EPS)
    return xc * rstd, rstd


def _ln_bwd(dxhat, xhat, rstd):
    m1 = jnp.mean(dxhat, axis=-1, keepdims=True)
    m2 = jnp.mean(dxhat * xhat, axis=-1, keepdims=True)
    return rstd * (dxhat - m1 - xhat * m2)


def _masked_sgu_w(sw_ref, hh):
    row = lax.broadcasted_iota(jnp.int32, (CHUNK, CHUNK), 0)
    col = lax.broadcasted_iota(jnp.int32, (CHUNK, CHUNK), 1)
    return jnp.where(row >= col, sw_ref[hh], 0.0)


def _inv_count(tile_index, tok_tile, window):
    tok = tile_index * tok_tile + lax.broadcasted_iota(jnp.int32, (tok_tile, 1), 0)
    return 1.0 / jnp.minimum(tok + 1, window).astype(F32)


def _params(**kw):
    return pltpu.CompilerParams(vmem_limit_bytes=VMEM_LIMIT, **kw)


def _forward_mixers(x, w_in, pool_w, pool_scale, sgu_ln_g, sgu_ln_b, sgu_w, sgu_bias_tile, tok_tile, sides=()):
    seq = x.shape[0]
    n_tiles = seq // tok_tile
    n_chunks = tok_tile // CHUNK
    side_refs = _SideRefs(sides, 8, 4, 2)

    def body(*refs):
        (x_ref, win_ref, pw_ref, ps_ref, lg_ref, lb_ref, sw_ref, sb_ref,
         hb_ref, y_ref, pooled_ref, xb_ref, aext_ref, h_ref) = side_refs.split(refs)
        i = pl.program_id(0)
        side_refs.emit(i == 0, i == n_tiles // 2, False, late=(i == (3 * n_tiles) // 4))

        @pl.when(i == 0)
        def _():
            aext_ref[0:HALO, :] = jnp.zeros((HALO, D_POOL), F32)

        xb = x_ref[...].astype(BF16)
        xb_ref[...] = xb
        for s in range(D_IN // 1024):
            cs = slice(s * 1024, (s + 1) * 1024)
            section = _dot(xb, win_ref[:, cs])
            h_ref[:, cs] = section
            if s >= 1:
                hb_ref[:, (s - 1) * 1024:s * 1024] = section.astype(BF16)

        aext_ref[HALO:HALO + tok_tile, :] = h_ref[:, 0:D_POOL]
        for g, window in enumerate(POOL_WINDOWS):
            cols = slice(g * POOL_GROUP, (g + 1) * POOL_GROUP)
            win = aext_ref[HALO:HALO + tok_tile, cols]
            for k in range(1, window):
                win = win + aext_ref[HALO - k:HALO - k + tok_tile, cols]
            pooled = win * _inv_count(i, tok_tile, window) - h_ref[:, cols]
            pb = pooled.astype(BF16)
            pooled_ref[:, cols] = pb
            mixed = _dot(pb, pw_ref[g])
            z = h_ref[:, Z0 + g * POOL_GROUP:Z0 + (g + 1) * POOL_GROUP]
            y_ref[:, cols] = (mixed * ps_ref[:, cols] * (z * _sigmoid(z))).astype(BF16)
        aext_ref[0:HALO, :] = aext_ref[tok_tile:tok_tile + HALO, :]

        for hh in range(N_HEADS):
            cols = slice(hh * HEAD, (hh + 1) * HEAD)
            swm = _masked_sgu_w(sw_ref, hh).astype(BF16)
            for n in range(n_chunks):
                rows = slice(n * CHUNK, (n + 1) * CHUNK)
                gu = _gelu(h_ref[rows, U0 + hh * HEAD:U0 + (hh + 1) * HEAD])
                gv = _gelu(h_ref[rows, V0 + hh * HEAD:V0 + (hh + 1) * HEAD])
                xhat, _ = _row_stats(gv)
                vln = xhat * lg_ref[:, cols] + lb_ref[:, cols]
                sv = _dot(swm, vln.astype(BF16)) + sb_ref[:, cols]
                z = h_ref[rows, Z0 + D_POOL + hh * HEAD:Z0 + D_POOL + (hh + 1) * HEAD]
                y_ref[rows, D_POOL + hh * HEAD:D_POOL + (hh + 1) * HEAD] = (
                    gu * sv * (z * _sigmoid(z))).astype(BF16)

        side_refs.emit(False, False, i == n_tiles - 1, late=False)

    tok = lambda width: pl.BlockSpec((tok_tile, width), lambda i: (i, 0))
    outs = pl.pallas_call(
        body, name="forward_mixers",
        grid=(n_tiles,),
        in_specs=[tok(D_MODEL)] + [VMEM_FULL] * 7 + side_refs.in_specs,
        out_specs=[tok(D_IN - D_POOL), tok(D_MIX), tok(D_POOL), tok(D_MODEL)] + side_refs.out_specs,
        out_shape=[jax.ShapeDtypeStruct((seq, D_IN - D_POOL), BF16), jax.ShapeDtypeStruct((seq, D_MIX), BF16),
                   jax.ShapeDtypeStruct((seq, D_POOL), BF16), jax.ShapeDtypeStruct((seq, D_MODEL), BF16)]
        + side_refs.out_shapes,
        scratch_shapes=[pltpu.VMEM((HALO + tok_tile, D_POOL), F32), pltpu.VMEM((tok_tile, D_IN), F32)]
        + side_refs.scratch,
        compiler_params=_params(dimension_semantics=("arbitrary",)),
    )(x, w_in, pool_w, pool_scale, sgu_ln_g, sgu_ln_b, sgu_w, sgu_bias_tile, *side_refs.inputs)
    return outs[:4], outs[4:]


def _head_fwd_bwd(x, y, p, target, w_out, gate_w, ple_w, ln_g, ln_b, gate_b, tok_tile, sub_tile):
    seq = x.shape[0]

    def body(x_ref, y_ref, p_ref, t_ref, wout_ref, gw_ref, plw_ref, lng_ref, lnb_ref, gb_ref,
             dy_ref, dxr_ref, xn_ref, dgp_ref, dpe_ref, dr_ref, loss_ref, dlng_ref, dlnb_ref, dgb_ref):
        i = pl.program_id(0)

        @pl.when(i == 0)
        def _():
            loss_ref[...] = jnp.zeros_like(loss_ref)
            dlng_ref[...] = jnp.zeros_like(dlng_ref)
            dlnb_ref[...] = jnp.zeros_like(dlnb_ref)
            dgb_ref[...] = jnp.zeros_like(dgb_ref)

        subs = [slice(s * sub_tile, (s + 1) * sub_tile) for s in range(tok_tile // sub_tile)]
        stats, xns, douts = [], [], []
        for rows in subs:
            r = ALPHA * x_ref[rows, :] + _dot(y_ref[rows, :], wout_ref[...])
            xhat, rstd = _row_stats(r)
            xn = xhat * lng_ref[...] + lnb_ref[...]
            xn_ref[rows, :] = xn.astype(BF16)
            stats.append((xhat, rstd))
            xns.append(xn)
        loss = jnp.zeros((1, 1), F32)
        dgb = jnp.zeros((1, D_MODEL), F32)
        for rows, xn in zip(subs, xns):
            gate = _sigmoid(_dot(xn_ref[rows, :], gw_ref[...]) + gb_ref[...])
            pe = _dot(p_ref[rows, :].astype(BF16), plw_ref[...])
            err = xn + gate * pe - t_ref[rows, :]
            loss = loss + jnp.sum(err * err, keepdims=True)
            dout = err * (1.0 / D_MODEL)
            dpe_ref[rows, :] = (dout * gate).astype(BF16)
            dgpre = dout * pe * gate * (1.0 - gate)
            dgb = dgb + jnp.sum(dgpre, axis=0, keepdims=True)
            dgp_ref[rows, :] = dgpre.astype(BF16)
            douts.append(dout)
        loss_ref[...] += (0.5 / D_MODEL) * loss
        dgb_ref[...] += dgb
        dlng = jnp.zeros((1, D_MODEL), F32)
        dlnb = jnp.zeros((1, D_MODEL), F32)
        for rows, (xhat, rstd), dout in zip(subs, stats, douts):
            dxn = dout + _dot_nt(dgp_ref[rows, :], gw_ref[...])
            dlng = dlng + jnp.sum(dxn * xhat, axis=0, keepdims=True)
            dlnb = dlnb + jnp.sum(dxn, axis=0, keepdims=True)
            dr = _ln_bwd(dxn * lng_ref[...], xhat, rstd)
            dxr_ref[rows, :] = ALPHA * dr
            dr_ref[rows, :] = dr.astype(BF16)
        dlng_ref[...] += dlng
        dlnb_ref[...] += dlnb
        for rows in subs:
            dy_ref[rows, :] = _dot_nt(dr_ref[rows, :], wout_ref[...])

    tok = lambda width: pl.BlockSpec((tok_tile, width), lambda i: (i, 0))
    acc = lambda width: pl.BlockSpec((1, width), lambda i: (0, 0))
    vec = jax.ShapeDtypeStruct((1, D_MODEL), F32)
    return pl.pallas_call(
        body, name="head_fwd_bwd",
        grid=(seq // tok_tile,),
        in_specs=[tok(D_MODEL), tok(D_MIX), tok(D_PLE), tok(D_MODEL),
                  VMEM_FULL, VMEM_FULL, VMEM_FULL, VMEM_FULL, VMEM_FULL, VMEM_FULL],
        out_specs=[tok(D_MIX), tok(D_MODEL), tok(D_MODEL), tok(D_MODEL), tok(D_MODEL), tok(D_MODEL),
                   acc(128), acc(D_MODEL), acc(D_MODEL), acc(D_MODEL)],
        out_shape=[jax.ShapeDtypeStruct((seq, D_MIX), F32), jax.ShapeDtypeStruct((seq, D_MODEL), F32),
                   jax.ShapeDtypeStruct((seq, D_MODEL), BF16), jax.ShapeDtypeStruct((seq, D_MODEL), BF16),
                   jax.ShapeDtypeStruct((seq, D_MODEL), BF16), jax.ShapeDtypeStruct((seq, D_MODEL), BF16),
                   jax.ShapeDtypeStruct((1, 128), F32), vec, vec, vec],
        compiler_params=_params(dimension_semantics=("arbitrary",)),
    )(x, y, p, target, w_out, gate_w, ple_w, ln_g, ln_b, gate_b)


def _mixers_bwd(h, dy, dxr, pooled, w_in, pool_w, pool_scale, sgu_ln_g, sgu_ln_b, sgu_w, sgu_bias_tile, tok_tile):
    seq = h.shape[0]
    n_tiles = seq // tok_tile
    n_chunks = tok_tile // CHUNK
    pool_rows = POOL_GROUP // N_DEV

    def body(h_ref, dy_ref, dxr_ref, pooled_ref, win_ref, pw_ref, ps_ref, lg_ref, lb_ref, sw_ref, sb_ref, band_ref,
             dh_ref, dx_ref, dpw_ref, dps_ref, dlg_ref, dlb_ref, dsw_ref, dsb_ref,
             qhi_ref, qlo_ref, dpw_acc, dsb_acc):
        i = pl.program_id(0)
        tile = n_tiles - 1 - i

        @pl.when(i == 0)
        def _():
            qhi_ref[...] = jnp.zeros_like(qhi_ref)
            qlo_ref[...] = jnp.zeros_like(qlo_ref)
            dpw_acc[...] = jnp.zeros_like(dpw_acc)
            dsb_acc[...] = jnp.zeros_like(dsb_acc)
            dps_ref[...] = jnp.zeros_like(dps_ref)
            dlg_ref[...] = jnp.zeros_like(dlg_ref)
            dlb_ref[...] = jnp.zeros_like(dlb_ref)
            dsw_ref[...] = jnp.zeros_like(dsw_ref)

        def h_at(rows, cols):
            return h_ref[rows, cols.start - D_POOL:cols.stop - D_POOL].astype(F32)

        everything = slice(0, tok_tile)
        for part in (qhi_ref, qlo_ref):
            part[tok_tile:tok_tile + HALO, :] = part[0:HALO, :]
        for g, window in enumerate(POOL_WINDOWS):
            cols = slice(g * POOL_GROUP, (g + 1) * POOL_GROUP)
            zcols = slice(Z0 + g * POOL_GROUP, Z0 + (g + 1) * POOL_GROUP)
            z = h_at(everything, zcols)
            sz = _sigmoid(z)
            pb = pooled_ref[:, cols]
            mixed = _dot(pb, pw_ref[g])
            dyp = dy_ref[:, cols]
            dh_ref[:, zcols] = (dyp * (mixed * ps_ref[:, cols]) * (sz * (1.0 + z * (1.0 - sz)))).astype(BF16)
            dms = dyp * (z * sz)
            dps_ref[:, cols] += jnp.sum(dms * mixed, axis=0, keepdims=True)
            dmixed = (dms * ps_ref[:, cols]).astype(BF16)
            dpw_acc[g] += _dot_tn(pb, dmixed)
            dpooled = _dot_nt(dmixed, pw_ref[g])
            qhi_ref[everything, cols], qlo_ref[everything, cols] = _split_bf16(
                dpooled * _inv_count(tile, tok_tile, window))
            da = _dot(band_ref[g], qhi_ref[:, cols]) + _dot(band_ref[g], qlo_ref[:, cols]) - dpooled
            dh_ref[:, cols] = da.astype(BF16)

        pool_cols = [slice(o + g * POOL_GROUP, o + (g + 1) * POOL_GROUP)
                     for o in (0, Z0) for g in range(len(POOL_WINDOWS))]
        pool_slices = [pool_cols[0:3], pool_cols[3:6], pool_cols[6:8], []]
        for hh in range(N_HEADS):
            cols = slice(hh * HEAD, (hh + 1) * HEAD)
            ucols = slice(U0 + hh * HEAD, U0 + (hh + 1) * HEAD)
            vcols = slice(V0 + hh * HEAD, V0 + (hh + 1) * HEAD)
            zcols = slice(Z0 + D_POOL + hh * HEAD, Z0 + D_POOL + (hh + 1) * HEAD)
            sw32 = _masked_sgu_w(sw_ref, hh)
            swm = sw32.astype(BF16)
            swm_t = sw32.T.astype(BF16)
            for n in range(n_chunks):
                rows = slice(n * CHUNK, (n + 1) * CHUNK)
                gu, dgu_du = _gelu_and_grad(h_at(rows, ucols))
                gv, dgv_dv = _gelu_and_grad(h_at(rows, vcols))
                xhat, rstd = _row_stats(gv)
                vb = (xhat * lg_ref[:, cols] + lb_ref[:, cols]).astype(BF16)
                sv = _dot(swm, vb) + sb_ref[:, cols]
                z = h_at(rows, zcols)
                sz = _sigmoid(z)
                dys = dy_ref[rows, D_POOL + hh * HEAD:D_POOL + (hh + 1) * HEAD]
                dh_ref[rows, zcols] = (dys * (gu * sv) * (sz * (1.0 + z * (1.0 - sz)))).astype(BF16)
                dyg = dys * (z * sz)
                dh_ref[rows, ucols] = (dyg * sv * dgu_du).astype(BF16)
                dsv = dyg * gu
                dsb_acc[:, cols] += dsv
                dsvb = dsv.astype(BF16)
                dsw_ref[hh] += _dot_nt(dsvb, vb)
                dvln = _dot(swm_t, dsvb)
                dlg_ref[:, cols] += jnp.sum(dvln * xhat, axis=0, keepdims=True)
                dlb_ref[:, cols] += jnp.sum(dvln, axis=0, keepdims=True)
                dgv = _ln_bwd(dvln * lg_ref[:, cols], xhat, rstd)
                dh_ref[rows, vcols] = (dgv * dgv_dv).astype(BF16)
            ready = [ucols, vcols, zcols] + pool_slices[hh]
            part = _dot_nt(dh_ref[:, ready[0]], win_ref[:, ready[0]])
            for sl in ready[1:]:
                part = part + _dot_nt(dh_ref[:, sl], win_ref[:, sl])
            if hh == 0:
                dx_ref[...] = dxr_ref[...] + part
            else:
                dx_ref[...] += part

        @pl.when(i == n_tiles - 1)
        def _():
            for g in range(len(POOL_WINDOWS)):
                for b in range(N_DEV):
                    dpw_ref[b, g] = dpw_acc[g, b * pool_rows:(b + 1) * pool_rows, :]
            row = lax.broadcasted_iota(jnp.int32, (CHUNK, CHUNK), 0)
            col = lax.broadcasted_iota(jnp.int32, (CHUNK, CHUNK), 1)
            for hh in range(N_HEADS):
                dsw_ref[hh] = jnp.where(row >= col, dsw_ref[hh], 0.0)
                total = jnp.sum(dsb_acc[:, hh * HEAD:(hh + 1) * HEAD], axis=1, keepdims=True)
                dsb_ref[hh:hh + 1, :] = jnp.broadcast_to(total, (CHUNK, CHUNK)).T[0:1, :]

    tok = lambda width: pl.BlockSpec((tok_tile, width), lambda i: (n_tiles - 1 - i, 0))
    whole = lambda shape: pl.BlockSpec(shape, lambda i: (0,) * len(shape))
    vec = jax.ShapeDtypeStruct((1, D_MODEL), F32)
    return pl.pallas_call(
        body, name="mixers_bwd",
        grid=(n_tiles,),
        in_specs=[tok(D_IN - D_POOL), tok(D_MIX), tok(D_MODEL), tok(D_POOL)] + [VMEM_FULL] * 8,
        out_specs=[tok(D_IN), tok(D_MODEL), whole((N_DEV, len(POOL_WINDOWS), pool_rows, POOL_GROUP)),
                   whole((1, D_POOL)), whole((1, D_SGU)), whole((1, D_SGU)),
                   whole((N_HEADS, CHUNK, CHUNK)), whole((N_HEADS, CHUNK))],
        out_shape=[jax.ShapeDtypeStruct((seq, D_IN), BF16), jax.ShapeDtypeStruct((seq, D_MODEL), F32),
                   jax.ShapeDtypeStruct((N_DEV, len(POOL_WINDOWS), pool_rows, POOL_GROUP), F32),
                   vec, vec, vec,
                   jax.ShapeDtypeStruct((N_HEADS, CHUNK, CHUNK), F32),
                   jax.ShapeDtypeStruct((N_HEADS, CHUNK), F32)],
        scratch_shapes=[pltpu.VMEM((tok_tile + BAND_PAD, D_POOL), BF16),
                        pltpu.VMEM((tok_tile + BAND_PAD, D_POOL), BF16),
                        pltpu.VMEM((len(POOL_WINDOWS), POOL_GROUP, POOL_GROUP), F32),
                        pltpu.VMEM((CHUNK, D_SGU), F32)],
        compiler_params=_params(dimension_semantics=("arbitrary",)),
    )(h, dy, dxr, pooled, w_in, pool_w, pool_scale, sgu_ln_g, sgu_ln_b, sgu_w, sgu_bias_tile, _bands(tok_tile, True))


def _weight_grads(items, tok_tile, name, sides=()):
    n_items = len(items)
    seq = items[0][0].shape[0]
    n_k = seq // tok_tile
    side_refs = _SideRefs(sides, 2 * n_items, n_items, 0)

    def body(*refs):
        refs = side_refs.split(refs)
        k = pl.program_id(0)
        side_refs.emit(k == 0, k == n_k // 2, False)
        for t, (_, b, n_col_blocks) in enumerate(items):
            a_ref, b_ref, out_ref = refs[2 * t], refs[2 * t + 1], refs[2 * n_items + t]
            nb = b.shape[1] // n_col_blocks

            @pl.when(k == 0)
            def _():
                out_ref[...] = jnp.zeros_like(out_ref)

            res = _dot_tn(a_ref[...].astype(BF16), b_ref[...])
            for blk in range(n_col_blocks):
                out_ref[blk] += res[:, blk * nb:(blk + 1) * nb]
        side_refs.emit(False, False, k == n_k - 1)

    in_specs, operands, out_specs, out_shape = [], [], [], []
    for a, b, n_col_blocks in items:
        shape = (n_col_blocks, a.shape[1], b.shape[1] // n_col_blocks)
        in_specs += [pl.BlockSpec((tok_tile, a.shape[1]), lambda k: (k, 0)),
                     pl.BlockSpec((tok_tile, b.shape[1]), lambda k: (k, 0))]
        operands += [a, b]
        out_specs.append(pl.BlockSpec(shape, lambda k: (0, 0, 0)))
        out_shape.append(jax.ShapeDtypeStruct(shape, F32))
    outs = pl.pallas_call(
        body, name=name, grid=(n_k,),
        in_specs=in_specs + side_refs.in_specs, out_specs=out_specs + side_refs.out_specs,
        out_shape=out_shape + side_refs.out_shapes, scratch_shapes=side_refs.scratch,
        compiler_params=_params(dimension_semantics=("arbitrary",)),
    )(*operands, *side_refs.inputs)
    return outs[:n_items], outs[n_items:]


class _Side(NamedTuple):
    inputs: list
    out_shapes: list
    sem_shapes: list
    emit: Callable


def _when(cond):
    if cond is True:
        return lambda f: f()
    if cond is False:
        return lambda f: None
    return pl.when(cond)


def _place():
    return lax.axis_index("x"), lax.axis_index("y"), lax.axis_index("c")


def _other_chips(x, y):
    return [(1 - x, y), (x, 1 - y), (1 - x, 1 - y)]


def _gather_side(shards, out_shapes, views, halves):
    n = len(shards)
    n_sems = 10

    def emit(ins, outs, sems, first, mid, last, late=None):
        send_sems, recv_sems, local_sems = sems
        x, y, c = _place()
        here, x_nbr, y_nbr, diag = (x, y), (1 - x, y), (x, 1 - y), (1 - x, 1 - y)
        sibling = (x, y, 1 - c)

        def block(k, chip, core):
            return views[k](outs[k], 4 * chip[0] + 2 * chip[1] + core)

        def piece(k, ref, p):
            return ref if halves[k] is None else halves[k](ref)[p]

        def n_pieces(k):
            return 1 if halves[k] is None else 2

        def copy(k, s, src, dst, to):
            return pltpu.make_async_remote_copy(
                src_ref=src, dst_ref=dst, send_sem=send_sems.at[k, s], recv_sem=recv_sems.at[k, s],
                device_id=to, device_id_type=MESH)

        def outgoing(k, s):
            mine = block(k, here, c)
            if s == 0:
                return copy(k, 0, ins[k], mine, sibling)
            if s in (1, 2):
                return copy(k, s, piece(k, ins[k], s - 1), piece(k, mine, s - 1), (*x_nbr, c))
            if s in (3, 4):
                return copy(k, s, piece(k, ins[k], s - 3), piece(k, mine, s - 3), (*y_nbr, c))
            if s == 5:
                part = piece(k, block(k, x_nbr, c), 0)
                return copy(k, 5, part, part, (*y_nbr, c))
            if s == 6:
                part = piece(k, block(k, y_nbr, c), 1)
                return copy(k, 6, part, part, (*x_nbr, c))
            whole = block(k, (x_nbr, y_nbr, diag)[s - 7], c)
            return copy(k, s, whole, whole, sibling)

        def incoming(k, s):
            if s == 0:
                zone = block(k, here, 1 - c)
            elif s in (1, 2):
                zone = piece(k, block(k, x_nbr, c), s - 1)
            elif s in (3, 4):
                zone = piece(k, block(k, y_nbr, c), s - 3)
            elif s in (5, 6):
                zone = piece(k, block(k, diag, c), s - 5)
            else:
                zone = block(k, (x_nbr, y_nbr, diag)[s - 7], 1 - c)
            return copy(k, s, zone, zone, sibling)

        def own(k):
            return pltpu.make_async_copy(ins[k], block(k, here, c), local_sems.at[k])

        def used(k):
            return list(range(n_sems)) if n_pieces(k) == 2 else [0, 1, 3, 5, 7, 8, 9]

        @_when(first)
        def _():
            for k in range(n):
                own(k).start()
                for s in (0, 1, 4, 2, 3):
                    if s in used(k):
                        outgoing(k, s).start()

        @_when(mid)
        def _():
            for k in range(n):
                incoming(k, 1).wait_recv()
                outgoing(k, 5).start()
                if n_pieces(k) == 2:
                    incoming(k, 4).wait_recv()
                    outgoing(k, 6).start()
                    incoming(k, 2).wait_recv()
                outgoing(k, 7).start()
                incoming(k, 3).wait_recv()
                outgoing(k, 8).start()

        @_when(last if late is None else late)
        def _():
            for k in range(n):
                incoming(k, 5).wait_recv()
                if n_pieces(k) == 2:
                    incoming(k, 6).wait_recv()
                outgoing(k, 9).start()

        @_when(last)
        def _():
            for k in range(n):
                for s in (0, 7, 8, 9):
                    incoming(k, s).wait_recv()
            for k in range(n):
                for s in used(k):
                    outgoing(k, s).wait_send()
                own(k).wait()

    sems = [pltpu.SemaphoreType.DMA((n, n_sems)), pltpu.SemaphoreType.DMA((n, n_sems)),
            pltpu.SemaphoreType.DMA((n,))]
    return _Side(list(shards), list(out_shapes), sems, emit)


def _direct_gather_side(parts, out_shapes=None, views=None):
    n = len(parts)

    def place(k, ref, b):
        return ref.at[b] if views is None else views[k](ref, b)

    def emit(ins, outs, sems, first, mid, last, late=None):
        send_sems, recv_sems, local_sems = sems
        x, y, c = _place()

        def peer(r):
            flip = lambda v, bit: 1 - v if bit else v
            return flip(x, r & 4), flip(y, r & 2), flip(c, r & 1)

        def outgoing(k, r):
            return pltpu.make_async_remote_copy(
                src_ref=ins[k], dst_ref=place(k, outs[k], 4 * x + 2 * y + c),
                send_sem=send_sems.at[k, r - 1], recv_sem=recv_sems.at[k, r - 1],
                device_id=peer(r), device_id_type=MESH)

        def incoming(k, r):
            px, py, pc = peer(r)
            zone = place(k, outs[k], 4 * px + 2 * py + pc)
            return pltpu.make_async_remote_copy(
                src_ref=zone, dst_ref=zone, send_sem=send_sems.at[k, r - 1], recv_sem=recv_sems.at[k, r - 1],
                device_id=peer(r), device_id_type=MESH)

        def own(k):
            return pltpu.make_async_copy(ins[k], place(k, outs[k], 4 * x + 2 * y + c), local_sems.at[k])

        @_when(first)
        def _():
            for k in range(n):
                own(k).start()
                for r in range(1, N_DEV):
                    outgoing(k, r).start()

        @_when(last)
        def _():
            for k in range(n):
                for r in range(1, N_DEV):
                    incoming(k, r).wait_recv()
            for k in range(n):
                for r in range(1, N_DEV):
                    outgoing(k, r).wait_send()
                own(k).wait()

    shapes = out_shapes or [jax.ShapeDtypeStruct((N_DEV,) + a.shape, a.dtype) for a in parts]
    sems = [pltpu.SemaphoreType.DMA((n, N_DEV - 1)), pltpu.SemaphoreType.DMA((n, N_DEV - 1)),
            pltpu.SemaphoreType.DMA((n,))]
    return _Side(list(parts), shapes, sems, emit)


def _pair_side(grads):
    n = len(grads)

    def emit(ins, theirs, sems, first, mid, last, late=None):
        send_sems, recv_sems = sems
        x, y, c = _place()

        def copies():
            return [pltpu.make_async_remote_copy(
                src_ref=ins[k].at[2 * j + (1 - c)], dst_ref=theirs[k].at[j],
                send_sem=send_sems.at[k, j], recv_sem=recv_sems.at[k, j],
                device_id=(x, y, 1 - c), device_id_type=MESH) for k in range(n) for j in range(4)]

        @_when(first)
        def _():
            for cp in copies():
                cp.start()

        @_when(last)
        def _():
            for cp in copies():
                cp.wait_recv()
            for cp in copies():
                cp.wait_send()

    shapes = [jax.ShapeDtypeStruct((4,) + g.shape[1:], g.dtype) for g in grads]
    return _Side(list(grads), shapes, [pltpu.SemaphoreType.DMA((n, 4)), pltpu.SemaphoreType.DMA((n, 4))], emit)


def _chip_side(sums):
    n = len(sums)

    def emit(ins, others, sems, first, mid, last, late=None):
        send_sems, recv_sems = sems
        x, y, c = _place()

        def copies():
            return [pltpu.make_async_remote_copy(
                src_ref=ins[k].at[2 * px + py], dst_ref=others[k].at[r],
                send_sem=send_sems.at[k, r], recv_sem=recv_sems.at[k, r],
                device_id=(px, py, c), device_id_type=MESH)
                for k in range(n) for r, (px, py) in enumerate(_other_chips(x, y))]

        @_when(first)
        def _():
            for cp in copies():
                cp.start()

        @_when(last)
        def _():
            for cp in copies():
                cp.wait_recv()
            for cp in copies():
                cp.wait_send()

    shapes = [jax.ShapeDtypeStruct((3,) + s.shape[1:], s.dtype) for s in sums]
    return _Side(list(sums), shapes, [pltpu.SemaphoreType.DMA((n, 3)), pltpu.SemaphoreType.DMA((n, 3))], emit)


def _comm_call(side, name):
    n_in, n_out = len(side.inputs), len(side.out_shapes)

    def body(*refs):
        side.emit(refs[:n_in], refs[n_in:n_in + n_out], refs[n_in + n_out:], True, True, True)

    return pl.pallas_call(
        body, name=name, in_specs=[ANY] * n_in, out_specs=[ANY] * n_out,
        out_shape=side.out_shapes, scratch_shapes=side.sem_shapes,
    )(*side.inputs)


class _SideRefs:
    def __init__(self, sides, n_in, n_out, n_scratch):
        self.sides, self.n_in, self.n_out, self.n_scratch = sides, n_in, n_out, n_scratch
        self.inputs = [a for s in sides for a in s.inputs]
        self.out_shapes = [o for s in sides for o in s.out_shapes]
        self.scratch = [m for s in sides for m in s.sem_shapes]
        self.in_specs = [ANY] * len(self.inputs)
        self.out_specs = [ANY] * len(self.out_shapes)

    def split(self, refs):
        refs = list(refs)
        n_side_in, n_side_out = len(self.inputs), len(self.out_shapes)
        ins, rest = refs[:self.n_in], refs[self.n_in:]
        side_in, rest = rest[:n_side_in], rest[n_side_in:]
        outs, rest = rest[:self.n_out], rest[self.n_out:]
        side_out, rest = rest[:n_side_out], rest[n_side_out:]
        scratch, side_sems = rest[:self.n_scratch], rest[self.n_scratch:]
        self._refs = (side_in, side_out, side_sems)
        return ins + outs + scratch

    def emit(self, first, mid, last, late=None):
        side_in, side_out, side_sems = self._refs
        for s in self.sides:
            a, b, m = len(s.inputs), len(s.out_shapes), len(s.sem_shapes)
            s.emit(side_in[:a], side_out[:b], side_sems[:m], first, mid, last, late)
            side_in, side_out, side_sems = side_in[a:], side_out[b:], side_sems[m:]


ROW_RELATIONS = (0, 1, 2)


def _row_order(x, y):
    chips = _other_chips(x, y)
    return jnp.stack([2 * px + py for px, py in [chips[r] for r in ROW_RELATIONS] + [(x, y)]]).astype(jnp.int32)


def _grad_w_in_reduced(order, a, b, tok_tile, sides=()):
    seq, m = a.shape
    nb = b.shape[1] // N_DEV
    n_k = seq // tok_tile
    n_rows = 4
    last_step = n_rows * n_k - 1
    assert n_k >= 3
    fetch_at, sum_at = 1, 2
    side_refs = _SideRefs(sides, 3, 3, 8)

    def body(*refs):
        (order_ref, a_ref, b_ref, own_ref, theirs_ref, others_ref,
         acc_ref, stage_ref, sumbf_ref, pair_send, pair_recv, ici_send, ici_recv, stage_sem) = side_refs.split(refs)
        j, k = pl.program_id(0), pl.program_id(1)
        step = j * n_k + k
        side_refs.emit(step == 0, step == (n_rows * n_k) // 2, False, late=(step == (3 * n_rows * n_k) // 4))
        x, y, c = _place()
        chips = _other_chips(x, y)

        def to_sibling(row):
            return pltpu.make_async_remote_copy(
                src_ref=acc_ref.at[row % 2, 1 - c], dst_ref=theirs_ref.at[row],
                send_sem=pair_send.at[row], recv_sem=pair_recv.at[row],
                device_id=(x, y, 1 - c), device_id_type=MESH)

        def to_owner(row):
            rel = ROW_RELATIONS[row]
            px, py = chips[rel]
            return pltpu.make_async_remote_copy(
                src_ref=sumbf_ref.at[row], dst_ref=others_ref.at[rel],
                send_sem=ici_send.at[rel], recv_sem=ici_recv.at[rel],
                device_id=(px, py, c), device_id_type=MESH)

        def staged(row):
            return pltpu.make_async_copy(theirs_ref.at[row], stage_ref, stage_sem.at[0])

        @pl.when(k == 0)
        def _():
            acc_ref[j % 2] = jnp.zeros((2, m, nb), F32)

        res = _dot_tn(a_ref[...].astype(BF16), b_ref[...])
        for blk in range(2):
            acc_ref[j % 2, blk] += res[:, blk * nb:(blk + 1) * nb]

        for row in range(n_rows):
            @pl.when((j == row) & (k == n_k - 1))
            def _():
                to_sibling(row).start()

            if row < n_rows - 1:
                @pl.when((j == row + 1) & (k == fetch_at))
                def _():
                    to_sibling(row).wait_recv()
                    staged(row).start()

                @pl.when((j == row + 1) & (k == sum_at))
                def _():
                    staged(row).wait()
                    to_sibling(row).wait_send()
                    sumbf_ref[row] = (acc_ref[row % 2, c] + stage_ref[...]).astype(BF16)
                    to_owner(row).start()

        @pl.when(step == last_step)
        def _():
            row = n_rows - 1
            to_sibling(row).wait_recv()
            staged(row).start()
            staged(row).wait()
            to_sibling(row).wait_send()
            own_ref[...] = acc_ref[row % 2, c] + stage_ref[...]
            for r in range(n_rows - 1):
                to_owner(r).wait_recv()
                to_owner(r).wait_send()

        side_refs.emit(False, False, step == last_step, late=False)

    block = jax.ShapeDtypeStruct((m, nb), F32)
    outs = pl.pallas_call(
        body, name="grad_w_in",
        grid_spec=pltpu.PrefetchScalarGridSpec(
            num_scalar_prefetch=1, grid=(n_rows, n_k),
            in_specs=[pl.BlockSpec((tok_tile, m), lambda j, k, order_ref: (k, 0)),
                      pl.BlockSpec((tok_tile, 2 * nb), lambda j, k, order_ref: (k, order_ref[j]))]
            + side_refs.in_specs,
            out_specs=[pl.BlockSpec((m, nb), lambda j, k, order_ref: (0, 0)), ANY, ANY] + side_refs.out_specs,
            scratch_shapes=[pltpu.VMEM((2, 2, m, nb), F32), pltpu.VMEM((m, nb), F32),
                            pltpu.VMEM((n_rows - 1, m, nb), BF16),
                            pltpu.SemaphoreType.DMA((n_rows,)), pltpu.SemaphoreType.DMA((n_rows,)),
                            pltpu.SemaphoreType.DMA((n_rows - 1,)), pltpu.SemaphoreType.DMA((n_rows - 1,)),
                            pltpu.SemaphoreType.DMA((1,))] + side_refs.scratch),
        out_shape=[block, jax.ShapeDtypeStruct((n_rows, m, nb), F32),
                   jax.ShapeDtypeStruct((n_rows - 1, m, nb), BF16)] + side_refs.out_shapes,
        compiler_params=_params(dimension_semantics=("arbitrary", "arbitrary")),
    )(order, a, b, *side_refs.inputs)
    return outs[0], outs[2], outs[3:]


def _row_tile(rows, cols):
    tile = rows
    while tile * cols > 256 * 1024 and tile % 16 == 0:
        tile //= 2
    return tile


def _pair_sum(core, grads, theirs, name):
    _, rows, cols = theirs.shape
    rt = _row_tile(rows, cols)

    def body(core_ref, a_ref, b_ref, o_ref, ob_ref):
        total = a_ref[...] + b_ref[...]
        o_ref[...] = total
        ob_ref[...] = total.astype(BF16)

    spec = pl.BlockSpec((None, rt, cols), lambda j, i, core_ref: (j, i, 0))
    mine = pl.BlockSpec((None, None, rt, cols), lambda j, i, core_ref: (j, core_ref[0], i, 0))
    return pl.pallas_call(
        body, name=name,
        grid_spec=pltpu.PrefetchScalarGridSpec(
            num_scalar_prefetch=1, grid=(4, rows // rt), in_specs=[mine, spec], out_specs=[spec, spec]),
        out_shape=[jax.ShapeDtypeStruct(theirs.shape, F32), jax.ShapeDtypeStruct(theirs.shape, BF16)],
        compiler_params=_params(dimension_semantics=("arbitrary", "arbitrary")),
    )(core, grads.reshape(4, 2, rows, cols), theirs)


def _adamw(w, g, m, v):
    m = ADAM_B1 * m + (1.0 - ADAM_B1) * g
    v = ADAM_B2 * v + (1.0 - ADAM_B2) * (g * g)
    m_hat = m / (1.0 - ADAM_B1 ** ADAM_STEP)
    v_hat = v / (1.0 - ADAM_B2 ** ADAM_STEP)
    delta = -ADAM_LR * (m_hat / (jnp.sqrt(v_hat) + ADAM_EPS) + ADAM_WD * w)
    return delta, m, v


def _sum_adamw(chip, sums, others, w, m, v, name):
    _, rows, cols = sums.shape
    rt = _row_tile(rows, cols)

    def body(chip_ref, own_ref, oth_ref, w_ref, m_ref, v_ref, g_ref, d_ref, nm_ref, nv_ref):
        g = ((own_ref[...] + oth_ref[0].astype(F32)) + oth_ref[1].astype(F32)) + oth_ref[2].astype(F32)
        g_ref[...] = g
        d_ref[...], nm_ref[...], nv_ref[...] = _adamw(w_ref[...], g, m_ref[...], v_ref[...])

    spec = pl.BlockSpec((rt, cols), lambda i, chip_ref: (i, 0))
    own = pl.BlockSpec((None, rt, cols), lambda i, chip_ref: (chip_ref[0], i, 0))
    shape = jax.ShapeDtypeStruct((rows, cols), F32)
    return pl.pallas_call(
        body, name=name,
        grid_spec=pltpu.PrefetchScalarGridSpec(
            num_scalar_prefetch=1, grid=(rows // rt,),
            in_specs=[own, pl.BlockSpec((3, rt, cols), lambda i, chip_ref: (0, i, 0)), spec, spec, spec],
            out_specs=[spec] * 4),
        out_shape=[shape] * 4,
        compiler_params=_params(dimension_semantics=("arbitrary",)),
    )(chip, sums, others, w, m, v)


def _pair_exchange_sum(grads, name):
    n = len(grads)

    def body(*refs):
        ins, out32, outbf = refs[:n], refs[n:2 * n], refs[2 * n:3 * n]
        mine_v, theirs_v = refs[3 * n:4 * n], refs[4 * n:5 * n]
        send_sems, recv_sems, local_sems = refs[5 * n:]
        x, y, c = _place()
        remote = [pltpu.make_async_remote_copy(
            src_ref=ins[k].at[2 * j + (1 - c)], dst_ref=theirs_v[k].at[j],
            send_sem=send_sems.at[k, j], recv_sem=recv_sems.at[k, j],
            device_id=(x, y, 1 - c), device_id_type=MESH) for k in range(n) for j in range(4)]
        local = [pltpu.make_async_copy(ins[k].at[2 * j + c], mine_v[k].at[j], local_sems.at[k, j])
                 for k in range(n) for j in range(4)]
        for cp in remote + local:
            cp.start()
        for cp in local:
            cp.wait()
        for cp in remote:
            cp.wait_recv()
        for k in range(n):
            total = mine_v[k][...] + theirs_v[k][...]
            out32[k][...] = total
            outbf[k][...] = total.astype(BF16)
        for cp in remote:
            cp.wait_send()

    halves = [(4,) + g.shape[1:] for g in grads]
    outs = pl.pallas_call(
        body, name=name, in_specs=[ANY] * n, out_specs=[VMEM_FULL] * (2 * n),
        out_shape=[jax.ShapeDtypeStruct(h, F32) for h in halves] + [jax.ShapeDtypeStruct(h, BF16) for h in halves],
        scratch_shapes=[pltpu.VMEM(h, F32) for h in halves] * 2
        + [pltpu.SemaphoreType.DMA((n, 4))] * 3,
        compiler_params=_params(),
    )(*grads)
    return [(outs[k], outs[n + k]) for k in range(n)]


def _final_adamw(chip, items, gathered, ws, ms, vs):
    n_items, n = len(items), len(ws)

    def body(chip_ref, *refs):
        refs = list(refs)
        ins, refs = refs[:5 * n_items], refs[5 * n_items:]
        g8, refs = refs[:n + 1], refs[n + 1:]
        w, m, v, refs = refs[:n], refs[n:2 * n], refs[2 * n:3 * n], refs[3 * n:]
        outs, refs = refs[:4 * n_items], refs[4 * n_items:]
        g_out, d_out, m_out, v_out = refs[:n + 1], refs[n + 1:2 * n + 1], refs[2 * n + 1:3 * n + 1], refs[3 * n + 1:]
        for k in range(n_items):
            own_ref, oth_ref, w_ref, m_ref, v_ref = ins[5 * k:5 * k + 5]
            g = ((own_ref[...] + oth_ref[0].astype(F32)) + oth_ref[1].astype(F32)) + oth_ref[2].astype(F32)
            outs[4 * k][...] = g
            outs[4 * k + 1][...], outs[4 * k + 2][...], outs[4 * k + 3][...] = _adamw(
                w_ref[...], g, m_ref[...], v_ref[...])
        for k in range(n + 1):
            g = g8[k][0]
            for b in range(1, N_DEV):
                g = g + g8[k][b]
            g_out[k][...] = g
            if k < n:
                d_out[k][...], m_out[k][...], v_out[k][...] = _adamw(w[k][...], g, m[k][...], v[k][...])

    whole = lambda a: pl.BlockSpec(a.shape, lambda i, chip_ref: (0,) * len(a.shape))
    in_specs, operands, out_specs, out_shape = [], [], [], []
    for sums, others, w, m, v in items:
        in_specs += [pl.BlockSpec((None,) + sums.shape[1:], lambda i, chip_ref: (chip_ref[0], 0, 0)),
                     whole(others), whole(w), whole(m), whole(v)]
        operands += [sums, others, w, m, v]
        out_specs += [whole(w)] * 4
        out_shape += [jax.ShapeDtypeStruct(w.shape, F32)] * 4
    small_in = list(gathered) + list(ws) + list(ms) + list(vs)
    small_out = ([jax.ShapeDtypeStruct(w.shape, F32) for w in ws]
                 + [jax.ShapeDtypeStruct(gathered[-1].shape[1:], F32)]
                 + [jax.ShapeDtypeStruct(w.shape, F32) for w in ws] * 3)
    outs = pl.pallas_call(
        body, name="final_adamw",
        grid_spec=pltpu.PrefetchScalarGridSpec(
            num_scalar_prefetch=1, grid=(1,), in_specs=in_specs + [whole(a) for a in small_in],
            out_specs=out_specs + [whole(a) for a in small_out]),
        out_shape=out_shape + small_out, compiler_params=_params(dimension_semantics=("arbitrary",)),
    )(chip, *operands, *small_in)
    big, small = outs[:4 * n_items], outs[4 * n_items:]
    return ([tuple(big[4 * k:4 * k + 4]) for k in range(n_items)],
            small[:n + 1], small[n + 1:2 * n + 1], small[2 * n + 1:3 * n + 1], small[3 * n + 1:])


SMALL_NAMES = ("pool_scale", "sgu_ln_g", "sgu_ln_b", "sgu_w", "sgu_b", "ln_g", "ln_b", "ple_gate_b")


TOK_TILE = 256
GRAD_TILE = 1024


def _weight_views():
    cols = lambda width: (lambda ref, b: ref.at[:, pl.ds(pl.multiple_of(b * width, 128), width)])
    rows = lambda height: (lambda ref, b: ref.at[pl.ds(pl.multiple_of(b * height, 16), height), :])
    pool_rows = POOL_GROUP // N_DEV
    return {"w_in": cols(D_IN // N_DEV),
            "pool_w": lambda ref, b: ref.at[:, pl.ds(pl.multiple_of(b * pool_rows, 16), pool_rows), :],
            "w_out": rows(D_MIX // N_DEV), "ple_w": cols(D_MODEL // N_DEV), "gate_w": rows(D_MODEL // N_DEV)}


WEIGHT_SHAPES = {"w_in": (D_MODEL, D_IN), "pool_w": (len(POOL_WINDOWS), POOL_GROUP, POOL_GROUP),
                 "w_out": (D_MIX, D_MODEL), "ple_w": (D_PLE, D_MODEL), "gate_w": (D_MODEL, D_MODEL)}


def _to_bf16(arrays):
    def body(*refs):
        for src, dst in zip(refs[:len(arrays)], refs[len(arrays):]):
            dst[...] = src[...].astype(BF16)

    return pl.pallas_call(
        body, name="cast_shards", in_specs=[VMEM_FULL] * len(arrays), out_specs=[VMEM_FULL] * len(arrays),
        out_shape=[jax.ShapeDtypeStruct(a.shape, BF16) for a in arrays], compiler_params=_params(),
    )(*arrays)


def _leading_halves(shape):
    whole_tiles = len(shape) >= 3 or shape[0] % 32 == 0
    if shape[0] % 2 or not whole_tiles:
        return None
    half = shape[0] // 2
    return lambda ref: (ref.at[pl.ds(0, half)], ref.at[pl.ds(half, half)])


def _weight_gather(shards, names):
    views = _weight_views()
    return _gather_side([shards[nm] for nm in names],
                        [jax.ShapeDtypeStruct(WEIGHT_SHAPES[nm], BF16) for nm in names], [views[nm] for nm in names],
                        [_leading_halves(shards[nm].shape) for nm in names])


def _direct_weight_gather(shards, names):
    views = _weight_views()
    return _direct_gather_side([shards[nm] for nm in names],
                               [jax.ShapeDtypeStruct(WEIGHT_SHAPES[nm], BF16) for nm in names],
                               [views[nm] for nm in names])


def kernel(x, p, w_in, pool_w, pool_scale, sgu_ln_g, sgu_ln_b, sgu_w, sgu_b, w_out, ln_g, ln_b, ple_w, ple_gate_w, ple_gate_b, loss_target, m_w_in, m_pool_w, m_pool_scale, m_sgu_ln_g, m_sgu_ln_b, m_sgu_w, m_sgu_b, m_w_out, m_ln_g, m_ln_b, m_ple_w, m_ple_gate_w, m_ple_gate_b, v_w_in, v_pool_w, v_pool_scale, v_sgu_ln_g, v_sgu_ln_b, v_sgu_w, v_sgu_b, v_w_out, v_ln_g, v_ln_b, v_ple_w, v_ple_gate_w, v_ple_gate_b):
    seq = x.shape[1]
    x2, p2, target = x[0], p[0, 0], loss_target[0]
    core = lax.axis_index("c").astype(jnp.int32).reshape(1)
    chip = (2 * lax.axis_index("x") + lax.axis_index("y")).astype(jnp.int32).reshape(1)
    pool_rows = POOL_GROUP // N_DEV

    shard_names = ("w_in", "pool_w", "w_out", "ple_w", "gate_w")
    shards = dict(zip(shard_names, _to_bf16([w_in[0], pool_w[0], w_out[0], ple_w[0], ple_gate_w[0]])))
    w_in_f, pool_w_f = _comm_call(_weight_gather(shards, ("w_in", "pool_w")), "gather_mixer_weights")
    bias_tile = jnp.repeat(sgu_b[0].T, HEAD, axis=1)
    (h, y, pooled, xb), (w_out_f, ple_w_f, gate_w_f) = _forward_mixers(
        x2, w_in_f, pool_w_f, pool_scale, sgu_ln_g, sgu_ln_b, sgu_w[0], bias_tile, 2 * TOK_TILE,
        sides=(_direct_weight_gather(shards, ("w_out", "ple_w", "gate_w")),))

    dy, dxr, xn, dgp, dpe, dr, loss, d_ln_g, d_ln_b, d_gate_b = _head_fwd_bwd(
        x2, y, p2, target, w_out_f, gate_w_f, ple_w_f, ln_g, ln_b, ple_gate_b, 2 * TOK_TILE, TOK_TILE)

    (d_w_out,), _ = _weight_grads([(y, dr, 1)], GRAD_TILE, "grad_w_out")
    w_out_blocks = d_w_out.reshape(N_DEV, D_MIX // N_DEV, D_MODEL)
    (d_gate_w, d_ple_w), (w_out_theirs,) = _weight_grads(
        [(xn, dgp, 1), (p2, dpe, N_DEV)], GRAD_TILE, "grad_gate_ple_w", sides=(_pair_side([w_out_blocks]),))
    gate_blocks = d_gate_w.reshape(N_DEV, D_MODEL // N_DEV, D_MODEL)

    dh, dx, d_pool_w, d_pool_scale, d_sgu_ln_g, d_sgu_ln_b, d_sgu_w, d_sgu_b = _mixers_bwd(
        h, dy, dxr, pooled, w_in_f, pool_w_f, pool_scale, sgu_ln_g, sgu_ln_b, sgu_w[0], bias_tile, TOK_TILE)

    pool_blocks = d_pool_w.reshape(N_DEV, len(POOL_WINDOWS) * pool_rows, POOL_GROUP)
    hosted_names = ("w_out", "ple_w", "gate_w", "pool_w")
    ple_sums, gate_sums, pool_sums = _pair_exchange_sum([d_ple_w, gate_blocks, pool_blocks], "pair_exchange_sum")
    hosted_sums = [_pair_sum(core, w_out_blocks, w_out_theirs, "pair_sum_w_out"), ple_sums, gate_sums, pool_sums]

    small = (d_pool_scale, d_sgu_ln_g, d_sgu_ln_b, d_sgu_w, d_sgu_b, d_ln_g, d_ln_b, d_gate_b)
    parts = small + (loss,)
    small_gather = _direct_gather_side(list(parts))
    cx, cy = lax.axis_index("x"), lax.axis_index("y")
    order = _row_order(cx, cy)
    w_in_own, w_in_others, side_out = _grad_w_in_reduced(
        order, xb, dh, GRAD_TILE, sides=(_chip_side([s_bf for _, s_bf in hosted_sums]), small_gather))
    hosted_others, gathered = side_out[:len(hosted_sums)], side_out[len(hosted_sums):]

    shard_of = {"w_in": (w_in, m_w_in, v_w_in), "pool_w": (pool_w, m_pool_w, v_pool_w),
                "w_out": (w_out, m_w_out, v_w_out), "ple_w": (ple_w, m_ple_w, v_ple_w),
                "gate_w": (ple_gate_w, m_ple_gate_w, v_ple_gate_w)}
    reduced = [(nm, chip, s_f32, oth) for nm, (s_f32, _), oth in zip(hosted_names, hosted_sums, hosted_others)]
    reduced.append(("w_in", jnp.zeros((1,), jnp.int32), w_in_own[None], w_in_others))
    big_out, final_items = {}, []
    for nm, which, s_f32, oth in reduced:
        w, m, v = shard_of[nm]
        two_d = s_f32.shape[1:]
        args = (s_f32, oth, w.reshape(two_d), m.reshape(two_d), v.reshape(two_d))
        if nm == "w_in":
            big_out[nm] = [r.reshape(w.shape) for r in _sum_adamw(which, *args, "adamw_" + nm)]
        else:
            final_items.append((nm, args))

    small_w = (pool_scale, sgu_ln_g, sgu_ln_b, sgu_w, sgu_b, ln_g, ln_b, ple_gate_b)
    small_m = (m_pool_scale, m_sgu_ln_g, m_sgu_ln_b, m_sgu_w, m_sgu_b, m_ln_g, m_ln_b, m_ple_gate_b)
    small_v = (v_pool_scale, v_sgu_ln_g, v_sgu_ln_b, v_sgu_w, v_sgu_b, v_ln_g, v_ln_b, v_ple_gate_b)
    natural = [[a.reshape(g.shape) for a, g in zip(group, small)] for group in (small_w, small_m, small_v)]
    shard_res, *res = _final_adamw(chip, [a for _, a in final_items], list(gathered), *natural)
    for (nm, _), out in zip(final_items, shard_res):
        big_out[nm] = [r.reshape(shard_of[nm][0].shape) for r in out]
    g_s, d_s, m_s, v_s = [[r.reshape(w.shape) for r, w in zip(kind, small_w)] for kind in res]
    total_loss = res[0][-1][0, 0]

    order = ("w_in", "pool_w", "pool_scale", "sgu_ln_g", "sgu_ln_b", "sgu_w", "sgu_b", "w_out", "ln_g", "ln_b",
             "ple_w", "ple_gate_w", "ple_gate_b")
    outs = [total_loss, dx.reshape(1, seq, D_MODEL)]
    for kind in range(4):
        for nm in order:
            key = "gate_w" if nm == "ple_gate_w" else nm
            if key in big_out:
                outs.append(big_out[key][kind])
            else:
                outs.append((g_s, d_s, m_s, v_s)[kind][SMALL_NAMES.index(nm)])
    return tuple(outs)
```
